```python
import math
import jax, jax.numpy as jnp
from jax import lax
import numpy as np

D_MODEL = 1024
BATCH = 8
SEQ = 8192
DEPTH = 1

MIX_WIDTH = D_MODEL
ATTN_WIDTH = MIX_WIDTH // 2
SSM_WIDTH = MIX_WIDTH - ATTN_WIDTH
HEAD_DIM = 64
N_HEADS = ATTN_WIDTH // HEAD_DIM
DILATED_BRANCHES = ((128, 1), (512, 4), (2048, 16))
BLOCK = 128
SSM_GROUP = 16
N_SSM_GROUPS = SSM_WIDTH // SSM_GROUP
STATE_DIM = 64
D_FF = 2816
IN_WIDTH = 3 * ATTN_WIDTH + SSM_WIDTH
NORM_EPS = 1e-6
DT_MIN = 1e-3
DT_MAX = 1e-1

kernel_name = "hybrid_dilated_alibi_attn_s5_macaron_layer"


def rms_norm(x, g):
    xf = x.astype(jnp.float32)
    y = xf * lax.rsqrt(jnp.mean(xf * xf, axis=-1, keepdims=True) + NORM_EPS)
    return (y * g.astype(jnp.float32)).astype(x.dtype)


def swiglu(x, w_in, w_out):
    gate, up = jnp.split(x @ w_in, 2, axis=-1)
    return (jax.nn.silu(gate) * up) @ w_out


def alibi_slopes(n_heads):
    return 2.0 ** (-8.0 * jnp.arange(1, n_heads + 1, dtype=jnp.float32) / n_heads)


def dilated_window_branch(q, k, v, slopes, window, dilation):
    B, S, H, E = q.shape
    n_back = window // dilation
    L = -(-S // dilation)
    nb = -(-L // BLOCK)
    Lp = nb * BLOCK

    def to_blocks(t):
        t = jnp.pad(t, ((0, 0), (0, L * dilation - S), (0, 0), (0, 0)))
        t = t.reshape(B, L, dilation, H, E).transpose(0, 2, 1, 3, 4)
        t = jnp.pad(t, ((0, 0), (0, 0), (0, Lp - L), (0, 0), (0, 0)))
        return t.reshape(B, dilation, nb, BLOCK, H, E)

    def with_prev(t):
        prev = jnp.pad(t[:, :, :-1], ((0, 0), (0, 0), (1, 0), (0, 0), (0, 0), (0, 0)))
        return jnp.concatenate([prev, t], axis=3)

    def from_blocks(t):
        tail = t.shape[4:]
        t = t.reshape((B, dilation, Lp) + tail)[:, :, :L]
        t = jnp.moveaxis(t, 1, 2).reshape((B, L * dilation) + tail)
        return t[:, :S]

    qb, kb, vb = to_blocks(q), to_blocks(k), to_blocks(v)
    kk, vv = with_prev(kb), with_prev(vb)
    s = jnp.einsum('brnqhe,brnkhe->brnhqk', qb, kk) * (HEAD_DIM ** -0.5)

    qi = jnp.arange(BLOCK)[:, None]
    ci = jnp.arange(2 * BLOCK)[None, :]
    steps = BLOCK + qi - ci
    key_pos = (jnp.arange(nb)[:, None, None] - 1) * BLOCK + ci[None]
    valid = ((steps >= 0) & (steps <= n_back))[None] & (key_pos >= 0)
    dist = (steps * dilation).astype(jnp.float32)
    bias = -slopes[:, None, None] * dist[None]
    s = jnp.where(valid[None, None, :, None], s + bias, -jnp.inf)

    m = jnp.max(s, axis=-1, keepdims=True)
    p = jnp.exp(s - m)
    denom = jnp.sum(p, axis=-1, keepdims=True)
    o = jnp.einsum('brnhqk,brnkhe->brnqhe', p, vv)
    o = o * jnp.swapaxes(1.0 / denom[..., 0], -1, -2)[..., None]
    lse = jnp.swapaxes((m + jnp.log(denom))[..., 0], -1, -2)
    return from_blocks(o), from_blocks(lse)


def dilated_attention(q, k, v):
    B, S, _ = q.shape
    q, k, v = (t.astype(jnp.float32).reshape(B, S, N_HEADS, HEAD_DIM) for t in (q, k, v))
    slopes = alibi_slopes(N_HEADS)
    outs, lses = [], []
    for window, dilation in DILATED_BRANCHES:
        o, l = dilated_window_branch(q, k, v, slopes, window, dilation)
        outs.append(o)
        lses.append(l)
    w = jax.nn.softmax(jnp.stack(lses, axis=-1), axis=-1)
    o = jnp.einsum('bshn,nbshe->bshe', w, jnp.stack(outs, axis=0))
    return o.reshape(B, S, ATTN_WIDTH)


def s5_mixer(u, a_re, a_im, log_dt, b_re, b_im, c_re, c_im, d_skip, w_glu, b_glu):
    B, S, _ = u.shape
    f32 = jnp.float32
    uf = u.astype(f32).reshape(B, S, N_SSM_GROUPS, SSM_GROUP)
    dt = jnp.exp(log_dt.astype(f32))[:, None]
    a = lax.complex(a_re.astype(f32), a_im.astype(f32))
    a_bar = jnp.exp(dt * a)
    b = lax.complex(b_re.astype(f32), b_im.astype(f32))
    b_bar = ((a_bar - 1.0) / a)[..., None] * b
    bu = jnp.einsum('bsgc,gpc->bsgp', uf.astype(jnp.complex64), b_bar)
    a_seq = jnp.broadcast_to(a_bar, bu.shape)

    def combine(left, right):
        a_l, x_l = left
        a_r, x_r = right
        return a_r * a_l, a_r * x_l + x_r

    _, states = lax.associative_scan(combine, (a_seq, bu), axis=1)
    c = lax.complex(c_re.astype(f32), c_im.astype(f32))
    y = jnp.real(jnp.einsum('bsgp,gcp->bsgc', states, c))
    y = y + d_skip.astype(f32).reshape(N_SSM_GROUPS, SSM_GROUP) * uf
    y = jax.nn.gelu(y.reshape(B, S, SSM_WIDTH))
    return y * jax.nn.sigmoid(y @ w_glu.astype(f32) + b_glu.astype(f32))


def _fwd_setup_inputs(seed: int = 0) -> dict:
    key = jax.random.key(seed)
    ks = jax.random.split(key, 24)
    f32 = jnp.float32
    L = DEPTH

    def nrm(k, shape, scale):
        return jax.random.normal(k, shape, f32) * scale

    def gain(k):
        return 1.0 + 0.05 * jax.random.normal(k, (L, D_MODEL), f32)

    n_idx = jnp.arange(STATE_DIM, dtype=f32)
    a_re = -0.5 + 0.01 * jax.random.normal(ks[9], (L, N_SSM_GROUPS, STATE_DIM), f32)
    a_im = math.pi * n_idx + 0.01 * jax.random.normal(ks[10], (L, N_SSM_GROUPS, STATE_DIM), f32)
    log_dt = jax.random.uniform(ks[11], (L, N_SSM_GROUPS), f32,
                                math.log(DT_MIN), math.log(DT_MAX))
    return {
        "x": jax.random.normal(ks[0], (BATCH, SEQ, D_MODEL), f32),
        "ffn1_pre_g": gain(ks[1]),
        "ffn1_w_in": nrm(ks[2], (L, D_MODEL, 2 * D_FF), D_MODEL ** -0.5),
        "ffn1_w_out": nrm(ks[3], (L, D_FF, D_MODEL), D_FF ** -0.5),
        "ffn1_post_g": gain(ks[4]),
        "mix_pre_g": gain(ks[5]),
        "w_mix_in": nrm(ks[6], (L, D_MODEL, IN_WIDTH), D_MODEL ** -0.5),
        "a_re": a_re,
        "a_im": a_im,
        "log_dt": log_dt,
        "b_re": nrm(ks[12], (L, N_SSM_GROUPS, STATE_DIM, SSM_GROUP), (2 * SSM_GROUP) ** -0.5),
        "b_im": nrm(ks[13], (L, N_SSM_GROUPS, STATE_DIM, SSM_GROUP), (2 * SSM_GROUP) ** -0.5),
        "c_re": nrm(ks[14], (L, N_SSM_GROUPS, SSM_GROUP, STATE_DIM), (2 * STATE_DIM) ** -0.5),
        "c_im": nrm(ks[15], (L, N_SSM_GROUPS, SSM_GROUP, STATE_DIM), (2 * STATE_DIM) ** -0.5),
        "d_skip": nrm(ks[16], (L, SSM_WIDTH), 1.0),
        "w_glu": nrm(ks[17], (L, SSM_WIDTH, SSM_WIDTH), SSM_WIDTH ** -0.5),
        "b_glu": nrm(ks[18], (L, SSM_WIDTH), 0.01),
        "w_mix_out": nrm(ks[19], (L, MIX_WIDTH, D_MODEL), MIX_WIDTH ** -0.5),
        "mix_post_g": gain(ks[20]),
        "ffn2_pre_g": gain(ks[21]),
        "ffn2_w_in": nrm(ks[22], (L, D_MODEL, 2 * D_FF), D_MODEL ** -0.5),
        "ffn2_w_out": nrm(ks[23], (L, D_FF, D_MODEL), D_FF ** -0.5),
        "ffn2_post_g": gain(ks[7]),
    }


def _fwd_reference(x, ffn1_pre_g, ffn1_w_in, ffn1_w_out, ffn1_post_g, mix_pre_g, w_mix_in,
              a_re, a_im, log_dt, b_re, b_im, c_re, c_im, d_skip, w_glu, b_glu,
              w_mix_out, mix_post_g, ffn2_pre_g, ffn2_w_in, ffn2_w_out, ffn2_post_g):
    for l in range(DEPTH):
        h = rms_norm(x, ffn1_pre_g[l])
        x = x + 0.5 * rms_norm(swiglu(h, ffn1_w_in[l], ffn1_w_out[l]), ffn1_post_g[l])
        h = rms_norm(x, mix_pre_g[l])
        proj = h @ w_mix_in[l]
        q, k, v, u = jnp.split(proj, [ATTN_WIDTH, 2 * ATTN_WIDTH, 3 * ATTN_WIDTH], axis=-1)
        attn = dilated_attention(q, k, v).astype(x.dtype)
        ssm = s5_mixer(u, a_re[l], a_im[l], log_dt[l], b_re[l], b_im[l], c_re[l], c_im[l],
                       d_skip[l], w_glu[l], b_glu[l]).astype(x.dtype)
        mixed = jnp.concatenate([attn, ssm], axis=-1) @ w_mix_out[l]
        x = x + rms_norm(mixed, mix_post_g[l])
        h = rms_norm(x, ffn2_pre_g[l])
        x = x + 0.5 * rms_norm(swiglu(h, ffn2_w_in[l], ffn2_w_out[l]), ffn2_post_g[l])
    return x


import jax as _jax
import jax.numpy as _jnp

TWIN_FORMAT = 'train_step'
FWD_PARAMS = ['x', 'ffn1_pre_g', 'ffn1_w_in', 'ffn1_w_out', 'ffn1_post_g', 'mix_pre_g', 'w_mix_in', 'a_re', 'a_im', 'log_dt', 'b_re', 'b_im', 'c_re', 'c_im', 'd_skip', 'w_glu', 'b_glu', 'w_mix_out', 'mix_post_g', 'ffn2_pre_g', 'ffn2_w_in', 'ffn2_w_out', 'ffn2_post_g']
TWIN_WEIGHTS = ['ffn1_pre_g', 'ffn1_w_in', 'ffn1_w_out', 'ffn1_post_g', 'mix_pre_g', 'w_mix_in', 'a_re', 'a_im', 'log_dt', 'b_re', 'b_im', 'c_re', 'c_im', 'd_skip', 'w_glu', 'b_glu', 'w_mix_out', 'mix_post_g', 'ffn2_pre_g', 'ffn2_w_in', 'ffn2_w_out', 'ffn2_post_g']
TWIN_DIFF_INPUT = 'x'
TWIN_INPUTS = ['x', 'ffn1_pre_g', 'ffn1_w_in', 'ffn1_w_out', 'ffn1_post_g', 'mix_pre_g', 'w_mix_in', 'a_re', 'a_im', 'log_dt', 'b_re', 'b_im', 'c_re', 'c_im', 'd_skip', 'w_glu', 'b_glu', 'w_mix_out', 'mix_post_g', 'ffn2_pre_g', 'ffn2_w_in', 'ffn2_w_out', 'ffn2_post_g', 'loss_target', 'm_ffn1_pre_g', 'm_ffn1_w_in', 'm_ffn1_w_out', 'm_ffn1_post_g', 'm_mix_pre_g', 'm_w_mix_in', 'm_a_re', 'm_a_im', 'm_log_dt', 'm_b_re', 'm_b_im', 'm_c_re', 'm_c_im', 'm_d_skip', 'm_w_glu', 'm_b_glu', 'm_w_mix_out', 'm_mix_post_g', 'm_ffn2_pre_g', 'm_ffn2_w_in', 'm_ffn2_w_out', 'm_ffn2_post_g', 'v_ffn1_pre_g', 'v_ffn1_w_in', 'v_ffn1_w_out', 'v_ffn1_post_g', 'v_mix_pre_g', 'v_w_mix_in', 'v_a_re', 'v_a_im', 'v_log_dt', 'v_b_re', 'v_b_im', 'v_c_re', 'v_c_im', 'v_d_skip', 'v_w_glu', 'v_b_glu', 'v_w_mix_out', 'v_mix_post_g', 'v_ffn2_pre_g', 'v_ffn2_w_in', 'v_ffn2_w_out', 'v_ffn2_post_g']
TWIN_OUTPUTS = ['loss', 'grad_x', 'grad_ffn1_pre_g', 'grad_ffn1_w_in', 'grad_ffn1_w_out', 'grad_ffn1_post_g', 'grad_mix_pre_g', 'grad_w_mix_in', 'grad_a_re', 'grad_a_im', 'grad_log_dt', 'grad_b_re', 'grad_b_im', 'grad_c_re', 'grad_c_im', 'grad_d_skip', 'grad_w_glu', 'grad_b_glu', 'grad_w_mix_out', 'grad_mix_post_g', 'grad_ffn2_pre_g', 'grad_ffn2_w_in', 'grad_ffn2_w_out', 'grad_ffn2_post_g', 'delta_ffn1_pre_g', 'delta_ffn1_w_in', 'delta_ffn1_w_out', 'delta_ffn1_post_g', 'delta_mix_pre_g', 'delta_w_mix_in', 'delta_a_re', 'delta_a_im', 'delta_log_dt', 'delta_b_re', 'delta_b_im', 'delta_c_re', 'delta_c_im', 'delta_d_skip', 'delta_w_glu', 'delta_b_glu', 'delta_w_mix_out', 'delta_mix_post_g', 'delta_ffn2_pre_g', 'delta_ffn2_w_in', 'delta_ffn2_w_out', 'delta_ffn2_post_g', 'new_m_ffn1_pre_g', 'new_m_ffn1_w_in', 'new_m_ffn1_w_out', 'new_m_ffn1_post_g', 'new_m_mix_pre_g', 'new_m_w_mix_in', 'new_m_a_re', 'new_m_a_im', 'new_m_log_dt', 'new_m_b_re', 'new_m_b_im', 'new_m_c_re', 'new_m_c_im', 'new_m_d_skip', 'new_m_w_glu', 'new_m_b_glu', 'new_m_w_mix_out', 'new_m_mix_post_g', 'new_m_ffn2_pre_g', 'new_m_ffn2_w_in', 'new_m_ffn2_w_out', 'new_m_ffn2_post_g', 'new_v_ffn1_pre_g', 'new_v_ffn1_w_in', 'new_v_ffn1_w_out', 'new_v_ffn1_post_g', 'new_v_mix_pre_g', 'new_v_w_mix_in', 'new_v_a_re', 'new_v_a_im', 'new_v_log_dt', 'new_v_b_re', 'new_v_b_im', 'new_v_c_re', 'new_v_c_im', 'new_v_d_skip', 'new_v_w_glu', 'new_v_b_glu', 'new_v_w_mix_out', 'new_v_mix_post_g', 'new_v_ffn2_pre_g', 'new_v_ffn2_w_in', 'new_v_ffn2_w_out', 'new_v_ffn2_post_g']
TWIN_LEAF_KINDS = {'loss': 'loss', 'grad_x': 'grad_x', 'grad_ffn1_pre_g': 'grad_w', 'grad_ffn1_w_in': 'grad_w', 'grad_ffn1_w_out': 'grad_w', 'grad_ffn1_post_g': 'grad_w', 'grad_mix_pre_g': 'grad_w', 'grad_w_mix_in': 'grad_w', 'grad_a_re': 'grad_w', 'grad_a_im': 'grad_w', 'grad_log_dt': 'grad_w', 'grad_b_re': 'grad_w', 'grad_b_im': 'grad_w', 'grad_c_re': 'grad_w', 'grad_c_im': 'grad_w', 'grad_d_skip': 'grad_w', 'grad_w_glu': 'grad_w', 'grad_b_glu': 'grad_w', 'grad_w_mix_out': 'grad_w', 'grad_mix_post_g': 'grad_w', 'grad_ffn2_pre_g': 'grad_w', 'grad_ffn2_w_in': 'grad_w', 'grad_ffn2_w_out': 'grad_w', 'grad_ffn2_post_g': 'grad_w', 'delta_ffn1_pre_g': 'delta_w', 'delta_ffn1_w_in': 'delta_w', 'delta_ffn1_w_out': 'delta_w', 'delta_ffn1_post_g': 'delta_w', 'delta_mix_pre_g': 'delta_w', 'delta_w_mix_in': 'delta_w', 'delta_a_re': 'delta_w', 'delta_a_im': 'delta_w', 'delta_log_dt': 'delta_w', 'delta_b_re': 'delta_w', 'delta_b_im': 'delta_w', 'delta_c_re': 'delta_w', 'delta_c_im': 'delta_w', 'delta_d_skip': 'delta_w', 'delta_w_glu': 'delta_w', 'delta_b_glu': 'delta_w', 'delta_w_mix_out': 'delta_w', 'delta_mix_post_g': 'delta_w', 'delta_ffn2_pre_g': 'delta_w', 'delta_ffn2_w_in': 'delta_w', 'delta_ffn2_w_out': 'delta_w', 'delta_ffn2_post_g': 'delta_w', 'new_m_ffn1_pre_g': 'new_m', 'new_m_ffn1_w_in': 'new_m', 'new_m_ffn1_w_out': 'new_m', 'new_m_ffn1_post_g': 'new_m', 'new_m_mix_pre_g': 'new_m', 'new_m_w_mix_in': 'new_m', 'new_m_a_re': 'new_m', 'new_m_a_im': 'new_m', 'new_m_log_dt': 'new_m', 'new_m_b_re': 'new_m', 'new_m_b_im': 'new_m', 'new_m_c_re': 'new_m', 'new_m_c_im': 'new_m', 'new_m_d_skip': 'new_m', 'new_m_w_glu': 'new_m', 'new_m_b_glu': 'new_m', 'new_m_w_mix_out': 'new_m', 'new_m_mix_post_g': 'new_m', 'new_m_ffn2_pre_g': 'new_m', 'new_m_ffn2_w_in': 'new_m', 'new_m_ffn2_w_out': 'new_m', 'new_m_ffn2_post_g': 'new_m', 'new_v_ffn1_pre_g': 'new_v', 'new_v_ffn1_w_in': 'new_v', 'new_v_ffn1_w_out': 'new_v', 'new_v_ffn1_post_g': 'new_v', 'new_v_mix_pre_g': 'new_v', 'new_v_w_mix_in': 'new_v', 'new_v_a_re': 'new_v', 'new_v_a_im': 'new_v', 'new_v_log_dt': 'new_v', 'new_v_b_re': 'new_v', 'new_v_b_im': 'new_v', 'new_v_c_re': 'new_v', 'new_v_c_im': 'new_v', 'new_v_d_skip': 'new_v', 'new_v_w_glu': 'new_v', 'new_v_b_glu': 'new_v', 'new_v_w_mix_out': 'new_v', 'new_v_mix_post_g': 'new_v', 'new_v_ffn2_pre_g': 'new_v', 'new_v_ffn2_w_in': 'new_v', 'new_v_ffn2_w_out': 'new_v', 'new_v_ffn2_post_g': 'new_v'}


def _forward(args):
    return _fwd_reference(*[args[k] for k in FWD_PARAMS])


def _output_shape():
    def fwd():
        inp = _fwd_setup_inputs(0)
        return _fwd_reference(*[inp[k] for k in FWD_PARAMS])
    out = _jax.eval_shape(fwd)
    return out.shape, out.dtype

N_MICROBATCH = 1
ADAM_LR = 0.001
ADAM_B1 = 0.9
ADAM_B2 = 0.999
ADAM_EPS = 1e-08
ADAM_WD = 0.01
ADAM_STEP = 10
PER_EXAMPLE_BATCH_AXIS = {'x': 0, 'loss_target': 0}
SHARED_INPUTS = []
_WEIGHT_DTYPES = {'ffn1_pre_g': _jnp.float32, 'ffn1_w_in': _jnp.float32, 'ffn1_w_out': _jnp.float32, 'ffn1_post_g': _jnp.float32, 'mix_pre_g': _jnp.float32, 'w_mix_in': _jnp.float32, 'a_re': _jnp.float32, 'a_im': _jnp.float32, 'log_dt': _jnp.float32, 'b_re': _jnp.float32, 'b_im': _jnp.float32, 'c_re': _jnp.float32, 'c_im': _jnp.float32, 'd_skip': _jnp.float32, 'w_glu': _jnp.float32, 'b_glu': _jnp.float32, 'w_mix_out': _jnp.float32, 'mix_post_g': _jnp.float32, 'ffn2_pre_g': _jnp.float32, 'ffn2_w_in': _jnp.float32, 'ffn2_w_out': _jnp.float32, 'ffn2_post_g': _jnp.float32}
MOMENT_SCALE = {'ffn1_pre_g': 6.686813e-01, 'ffn1_w_in': 2.633517e-01, 'ffn1_w_out': 5.052387e-01, 'ffn1_post_g': 1.605908e+01, 'mix_pre_g': 1.312847e+00, 'w_mix_in': 8.542174e-01, 'a_re': 2.354159e-02, 'a_im': 2.809437e-02, 'log_dt': 1.393982e+01, 'b_re': 1.811555e-02, 'b_im': 1.852514e-02, 'c_re': 3.635514e-02, 'c_im': 3.650090e-02, 'd_skip': 2.687889e+00, 'w_glu': 3.781416e-01, 'b_glu': 1.226759e+00, 'w_mix_out': 1.892289e+00, 'mix_post_g': 6.522197e+01, 'ffn2_pre_g': 1.174296e+00, 'ffn2_w_in': 4.900191e-01, 'ffn2_w_out': 1.020593e+00, 'ffn2_post_g': 1.609103e+01}


def _to_microbatches(a, axis):
    t = _jnp.moveaxis(a, axis, 0)
    t = t.reshape((N_MICROBATCH, t.shape[0] // N_MICROBATCH) + t.shape[1:])
    return _jnp.moveaxis(t, 1, axis + 1)


def setup_inputs(seed: int = 0) -> dict:
    inp = _fwd_setup_inputs(seed)
    key = _jax.random.fold_in(_jax.random.key(seed), 7919)
    shape, _ = _output_shape()
    out = dict(inp)
    out["loss_target"] = _jax.random.normal(_jax.random.fold_in(key, 0), shape, _jnp.float32)
    for i, name in enumerate(TWIN_WEIGHTS):
        w = inp[name].astype(_jnp.float32)
        if MOMENT_SCALE is None:
            s = _jnp.sqrt(_jnp.mean(_jnp.square(w)) + 1e-30)
        else:
            s = MOMENT_SCALE[name]
        km, kv = _jax.random.split(_jax.random.fold_in(key, i + 1))
        out[name] = w
        out["m_" + name] = s * _jax.random.normal(km, w.shape, _jnp.float32)
        out["v_" + name] = (s * s) * _jax.random.uniform(kv, w.shape, _jnp.float32, 0.5, 1.5)
    if N_MICROBATCH > 1:
        for name, axis in PER_EXAMPLE_BATCH_AXIS.items():
            out[name] = _to_microbatches(out[name], axis)
    return {'x': out['x'], 'ffn1_pre_g': out['ffn1_pre_g'], 'ffn1_w_in': out['ffn1_w_in'], 'ffn1_w_out': out['ffn1_w_out'], 'ffn1_post_g': out['ffn1_post_g'], 'mix_pre_g': out['mix_pre_g'], 'w_mix_in': out['w_mix_in'], 'a_re': out['a_re'], 'a_im': out['a_im'], 'log_dt': out['log_dt'], 'b_re': out['b_re'], 'b_im': out['b_im'], 'c_re': out['c_re'], 'c_im': out['c_im'], 'd_skip': out['d_skip'], 'w_glu': out['w_glu'], 'b_glu': out['b_glu'], 'w_mix_out': out['w_mix_out'], 'mix_post_g': out['mix_post_g'], 'ffn2_pre_g': out['ffn2_pre_g'], 'ffn2_w_in': out['ffn2_w_in'], 'ffn2_w_out': out['ffn2_w_out'], 'ffn2_post_g': out['ffn2_post_g'], 'loss_target': out['loss_target'], 'm_ffn1_pre_g': out['m_ffn1_pre_g'], 'm_ffn1_w_in': out['m_ffn1_w_in'], 'm_ffn1_w_out': out['m_ffn1_w_out'], 'm_ffn1_post_g': out['m_ffn1_post_g'], 'm_mix_pre_g': out['m_mix_pre_g'], 'm_w_mix_in': out['m_w_mix_in'], 'm_a_re': out['m_a_re'], 'm_a_im': out['m_a_im'], 'm_log_dt': out['m_log_dt'], 'm_b_re': out['m_b_re'], 'm_b_im': out['m_b_im'], 'm_c_re': out['m_c_re'], 'm_c_im': out['m_c_im'], 'm_d_skip': out['m_d_skip'], 'm_w_glu': out['m_w_glu'], 'm_b_glu': out['m_b_glu'], 'm_w_mix_out': out['m_w_mix_out'], 'm_mix_post_g': out['m_mix_post_g'], 'm_ffn2_pre_g': out['m_ffn2_pre_g'], 'm_ffn2_w_in': out['m_ffn2_w_in'], 'm_ffn2_w_out': out['m_ffn2_w_out'], 'm_ffn2_post_g': out['m_ffn2_post_g'], 'v_ffn1_pre_g': out['v_ffn1_pre_g'], 'v_ffn1_w_in': out['v_ffn1_w_in'], 'v_ffn1_w_out': out['v_ffn1_w_out'], 'v_ffn1_post_g': out['v_ffn1_post_g'], 'v_mix_pre_g': out['v_mix_pre_g'], 'v_w_mix_in': out['v_w_mix_in'], 'v_a_re': out['v_a_re'], 'v_a_im': out['v_a_im'], 'v_log_dt': out['v_log_dt'], 'v_b_re': out['v_b_re'], 'v_b_im': out['v_b_im'], 'v_c_re': out['v_c_re'], 'v_c_im': out['v_c_im'], 'v_d_skip': out['v_d_skip'], 'v_w_glu': out['v_w_glu'], 'v_b_glu': out['v_b_glu'], 'v_w_mix_out': out['v_w_mix_out'], 'v_mix_post_g': out['v_mix_post_g'], 'v_ffn2_pre_g': out['v_ffn2_pre_g'], 'v_ffn2_w_in': out['v_ffn2_w_in'], 'v_ffn2_w_out': out['v_ffn2_w_out'], 'v_ffn2_post_g': out['v_ffn2_post_g']}


def _loss(weights, diff, rest, loss_target):
    with _jax.named_scope("forward"):
        args = {**rest, TWIN_DIFF_INPUT: diff, **{k: w.astype(_WEIGHT_DTYPES[k]) for k, w in weights.items()}}
        y = _forward(args)
    with _jax.named_scope("loss_head"):
        err = _jnp.square(y.astype(_jnp.float32) - loss_target)
        return 0.5 * _jnp.sum(_jnp.mean(err, axis=-1)) if err.ndim else 0.5 * err


def _adamw(w, g, m, v):
    m = ADAM_B1 * m + (1.0 - ADAM_B1) * g
    v = ADAM_B2 * v + (1.0 - ADAM_B2) * _jnp.square(g)
    m_hat = m / (1.0 - ADAM_B1 ** ADAM_STEP)
    v_hat = v / (1.0 - ADAM_B2 ** ADAM_STEP)
    delta = -ADAM_LR * (m_hat / (_jnp.sqrt(v_hat) + ADAM_EPS) + ADAM_WD * w)
    return delta, m, v


def reference(x, ffn1_pre_g, ffn1_w_in, ffn1_w_out, ffn1_post_g, mix_pre_g, w_mix_in, a_re, a_im, log_dt, b_re, b_im, c_re, c_im, d_skip, w_glu, b_glu, w_mix_out, mix_post_g, ffn2_pre_g, ffn2_w_in, ffn2_w_out, ffn2_post_g, loss_target, m_ffn1_pre_g, m_ffn1_w_in, m_ffn1_w_out, m_ffn1_post_g, m_mix_pre_g, m_w_mix_in, m_a_re, m_a_im, m_log_dt, m_b_re, m_b_im, m_c_re, m_c_im, m_d_skip, m_w_glu, m_b_glu, m_w_mix_out, m_mix_post_g, m_ffn2_pre_g, m_ffn2_w_in, m_ffn2_w_out, m_ffn2_post_g, v_ffn1_pre_g, v_ffn1_w_in, v_ffn1_w_out, v_ffn1_post_g, v_mix_pre_g, v_w_mix_in, v_a_re, v_a_im, v_log_dt, v_b_re, v_b_im, v_c_re, v_c_im, v_d_skip, v_w_glu, v_b_glu, v_w_mix_out, v_mix_post_g, v_ffn2_pre_g, v_ffn2_w_in, v_ffn2_w_out, v_ffn2_post_g):
    given = dict(x=x, ffn1_pre_g=ffn1_pre_g, ffn1_w_in=ffn1_w_in, ffn1_w_out=ffn1_w_out, ffn1_post_g=ffn1_post_g, mix_pre_g=mix_pre_g, w_mix_in=w_mix_in, a_re=a_re, a_im=a_im, log_dt=log_dt, b_re=b_re, b_im=b_im, c_re=c_re, c_im=c_im, d_skip=d_skip, w_glu=w_glu, b_glu=b_glu, w_mix_out=w_mix_out, mix_post_g=mix_post_g, ffn2_pre_g=ffn2_pre_g, ffn2_w_in=ffn2_w_in, ffn2_w_out=ffn2_w_out, ffn2_post_g=ffn2_post_g, loss_target=loss_target, m_ffn1_pre_g=m_ffn1_pre_g, m_ffn1_w_in=m_ffn1_w_in, m_ffn1_w_out=m_ffn1_w_out, m_ffn1_post_g=m_ffn1_post_g, m_mix_pre_g=m_mix_pre_g, m_w_mix_in=m_w_mix_in, m_a_re=m_a_re, m_a_im=m_a_im, m_log_dt=m_log_dt, m_b_re=m_b_re, m_b_im=m_b_im, m_c_re=m_c_re, m_c_im=m_c_im, m_d_skip=m_d_skip, m_w_glu=m_w_glu, m_b_glu=m_b_glu, m_w_mix_out=m_w_mix_out, m_mix_post_g=m_mix_post_g, m_ffn2_pre_g=m_ffn2_pre_g, m_ffn2_w_in=m_ffn2_w_in, m_ffn2_w_out=m_ffn2_w_out, m_ffn2_post_g=m_ffn2_post_g, v_ffn1_pre_g=v_ffn1_pre_g, v_ffn1_w_in=v_ffn1_w_in, v_ffn1_w_out=v_ffn1_w_out, v_ffn1_post_g=v_ffn1_post_g, v_mix_pre_g=v_mix_pre_g, v_w_mix_in=v_w_mix_in, v_a_re=v_a_re, v_a_im=v_a_im, v_log_dt=v_log_dt, v_b_re=v_b_re, v_b_im=v_b_im, v_c_re=v_c_re, v_c_im=v_c_im, v_d_skip=v_d_skip, v_w_glu=v_w_glu, v_b_glu=v_b_glu, v_w_mix_out=v_w_mix_out, v_mix_post_g=v_mix_post_g, v_ffn2_pre_g=v_ffn2_pre_g, v_ffn2_w_in=v_ffn2_w_in, v_ffn2_w_out=v_ffn2_w_out, v_ffn2_post_g=v_ffn2_post_g)
    weights = {n: given[n] for n in TWIN_WEIGHTS}
    shared = {n: given[n] for n in SHARED_INPUTS}
    per_example = {n: given[n] for n in ['x']}
    grad_fn = _jax.value_and_grad(_loss, argnums=(0, 1))

    def one_microbatch(ex, loss_target):
        ex = dict(ex)
        diff = ex.pop(TWIN_DIFF_INPUT)
        return grad_fn(weights, diff, {**shared, **ex}, loss_target)

    if N_MICROBATCH == 1:
        loss, (grad_w, grad_x) = one_microbatch(per_example, given["loss_target"])
    else:
        def body(carry, xs):
            loss_sum, grad_sum = carry
            l_k, (gw_k, gx_k) = one_microbatch(xs[0], xs[1])
            with _jax.named_scope("update"):
                return (loss_sum + l_k, _jax.tree.map(_jnp.add, grad_sum, gw_k)), gx_k

        init = (_jnp.zeros((), _jnp.float32), _jax.tree.map(_jnp.zeros_like, weights))
        (loss, grad_w), grad_x = _jax.lax.scan(body, init, (per_example, given["loss_target"]))
    with _jax.named_scope("update"):
        delta_w, new_m, new_v = {}, {}, {}
        for n in TWIN_WEIGHTS:
            delta_w[n], new_m[n], new_v[n] = _adamw(weights[n], grad_w[n], given["m_" + n], given["v_" + n])
    return (loss, grad_x, *[grad_w[n] for n in TWIN_WEIGHTS], *[delta_w[n] for n in TWIN_WEIGHTS],
            *[new_m[n] for n in TWIN_WEIGHTS], *[new_v[n] for n in TWIN_WEIGHTS])
```

```python
import functools
import math

import jax
import jax.numpy as jnp
from jax import lax
from jax.experimental import pallas as pl
from jax.experimental.pallas import tpu as pltpu

F32, BF16 = jnp.float32, jnp.bfloat16
SDS = jax.ShapeDtypeStruct

D_MODEL = 1024
N_DEV = 8
HEAD_DIM = 64
HEADS_PER_HALF = 4
QBLK = 128
DILATIONS = (1, 4, 16)
N_STATE = 64
HALF_STATES = 1024
NORM_EPS = 1e-6
NEG = -1e30
VMEM_LIMIT = 56 * 1024 * 1024
ADAM_LR, ADAM_B1, ADAM_B2, ADAM_EPS, ADAM_WD, ADAM_STEP = 1e-3, 0.9, 0.999, 1e-8, 0.01, 10
SCAN_TM = 256
SCAN_CW = 256


def _params(n_grid):
    return pltpu.CompilerParams(dimension_semantics=("arbitrary",) * n_grid, vmem_limit_bytes=VMEM_LIMIT)


def _dot(a, b):
    return jnp.dot(a, b, preferred_element_type=F32)


def _dot_nt(a, b):
    return lax.dot_general(a, b, (((1,), (1,)), ((), ())), preferred_element_type=F32)


def _dot_tn(a, b):
    return lax.dot_general(a, b, (((0,), (0,)), ((), ())), preferred_element_type=F32)


def _sigmoid(v):
    return 1.0 / (1.0 + jnp.exp(-v))


def rms_fwd(x, g, name):
    T, D = x.shape
    tm = 512

    def body(x_ref, g_ref, h_ref):
        xv = x_ref[...]
        r = lax.rsqrt(jnp.mean(xv * xv, axis=-1, keepdims=True) + NORM_EPS)
        h_ref[...] = (xv * r * g_ref[...]).astype(BF16)

    return pl.pallas_call(
        body, grid=(T // tm,),
        in_specs=[pl.BlockSpec((tm, D), lambda i: (i, 0)), pl.BlockSpec((1, D), lambda i: (0, 0))],
        out_specs=pl.BlockSpec((tm, D), lambda i: (i, 0)),
        out_shape=SDS((T, D), BF16), compiler_params=_params(1), name=name)(x, g)


def ffn_in(h, w, name):
    T, D = h.shape
    F = w.shape[2]
    tm = 512

    def body(h_ref, wg_ref, wu_ref, z_ref, a_ref):
        hv = h_ref[...]
        zg = _dot(hv, wg_ref[...])
        zu = _dot(hv, wu_ref[...])
        z_ref[0] = zg.astype(BF16)
        z_ref[1] = zu.astype(BF16)
        a_ref[...] = (zg * _sigmoid(zg) * zu).astype(BF16)

    return pl.pallas_call(
        body, grid=(4, T // tm),
        in_specs=[pl.BlockSpec((tm, D), lambda j, i: (i, 0)),
                  pl.BlockSpec((None, D, F), lambda j, i: (j, 0, 0)),
                  pl.BlockSpec((None, D, F), lambda j, i: (j + 4, 0, 0))],
        out_specs=[pl.BlockSpec((2, None, tm, F), lambda j, i: (0, j, i, 0)),
                   pl.BlockSpec((None, tm, F), lambda j, i: (j, i, 0))],
        out_shape=[SDS((2, 4, T, F), BF16), SDS((4, T, F), BF16)],
        compiler_params=_params(2), name=name)(h, w, w)


def mm_nn_b(a, w, w_off, nb, out_dtype, name):
    T, K = a.shape
    N = w.shape[2]
    tm = 512

    def body(a_ref, w_ref, o_ref):
        o_ref[...] = _dot(a_ref[...], w_ref[...]).astype(out_dtype)

    return pl.pallas_call(
        body, grid=(nb, T // tm),
        in_specs=[pl.BlockSpec((tm, K), lambda b, i: (i, 0)),
                  pl.BlockSpec((None, K, N), lambda b, i: (b + w_off, 0, 0))],
        out_specs=pl.BlockSpec((None, tm, N), lambda b, i: (b, i, 0)),
        out_shape=SDS((nb, T, N), out_dtype), compiler_params=_params(2), name=name)(a, w)


def mm_acc_norm(a, w, xres, g, scale, name):
    nb, T, K = a.shape
    D = w.shape[2]
    tm = 512

    def body(a_ref, w_ref, x_ref, g_ref, o_ref, y_ref):
        b = pl.program_id(1)

        @pl.when(b == 0)
        def _():
            o_ref[...] = jnp.zeros_like(o_ref)

        o_ref[...] += _dot(a_ref[...], w_ref[...])

        @pl.when(b == nb - 1)
        def _():
            o = o_ref[...]
            r = lax.rsqrt(jnp.mean(o * o, axis=-1, keepdims=True) + NORM_EPS)
            y_ref[...] = x_ref[...] + scale * (o * r * g_ref[...])

    return pl.pallas_call(
        body, grid=(T // tm, nb),
        in_specs=[pl.BlockSpec((None, tm, K), lambda i, b: (b, i, 0)),
                  pl.BlockSpec((None, K, D), lambda i, b: (b, 0, 0)),
                  pl.BlockSpec((tm, D), lambda i, b: (i, 0)),
                  pl.BlockSpec((1, D), lambda i, b: (0, 0))],
        out_specs=[pl.BlockSpec((tm, D), lambda i, b: (i, 0)), pl.BlockSpec((tm, D), lambda i, b: (i, 0))],
        out_shape=[SDS((T, D), F32), SDS((T, D), F32)],
        compiler_params=_params(2), name=name)(a, w, xres, g)


def loss_head(y, tgt, name):
    T, D = y.shape
    tm = 512

    def body(y_ref, t_ref, dy_ref, sq_ref):
        @pl.when(pl.program_id(0) == 0)
        def _():
            sq_ref[...] = jnp.zeros_like(sq_ref)

        e = y_ref[...] - t_ref[...]
        dy_ref[...] = e * (1.0 / D)
        sq_ref[...] += jnp.sum(e * e, axis=0, keepdims=True)

    return pl.pallas_call(
        body, grid=(T // tm,),
        in_specs=[pl.BlockSpec((tm, D), lambda i: (i, 0)), pl.BlockSpec((tm, D), lambda i: (i, 0))],
        out_specs=[pl.BlockSpec((tm, D), lambda i: (i, 0)), pl.BlockSpec((1, D), lambda i: (0, 0))],
        out_shape=[SDS((T, D), F32), SDS((1, D), F32)], compiler_params=_params(1), name=name)(y, tgt)


def post_bwd(dy, o, g, scale, name):
    T, D = o.shape
    tm = 512

    def body(dy_ref, o_ref, g_ref, do_ref, dg_ref):
        @pl.when(pl.program_id(0) == 0)
        def _():
            dg_ref[...] = jnp.zeros_like(dg_ref)

        ov = o_ref[...]
        r = scale * dy_ref[...]
        rstd = lax.rsqrt(jnp.mean(ov * ov, axis=-1, keepdims=True) + NORM_EPS)
        oh = ov * rstd
        dg_ref[...] += jnp.sum(r * oh, axis=0, keepdims=True)
        rg = r * g_ref[...]
        do_ref[...] = (rstd * (rg - oh * jnp.mean(rg * oh, axis=-1, keepdims=True))).astype(BF16)

    return pl.pallas_call(
        body, grid=(T // tm,),
        in_specs=[pl.BlockSpec((tm, D), lambda i: (i, 0)), pl.BlockSpec((tm, D), lambda i: (i, 0)),
                  pl.BlockSpec((1, D), lambda i: (0, 0))],
        out_specs=[pl.BlockSpec((tm, D), lambda i: (i, 0)), pl.BlockSpec((1, D), lambda i: (0, 0))],
        out_shape=[SDS((T, D), BF16), SDS((1, D), F32)], compiler_params=_params(1), name=name)(dy, o, g)


def mm_nt_b(gr, w, name):
    T, N = gr.shape
    nb, K, _ = w.shape
    tm = 512

    def body(g_ref, w_ref, o_ref):
        o_ref[...] = _dot_nt(g_ref[...], w_ref[...])

    return pl.pallas_call(
        body, grid=(nb, T // tm),
        in_specs=[pl.BlockSpec((tm, N), lambda b, i: (i, 0)), pl.BlockSpec((None, K, N), lambda b, i: (b, 0, 0))],
        out_specs=pl.BlockSpec((None, tm, K), lambda b, i: (b, i, 0)),
        out_shape=SDS((nb, T, K), F32), compiler_params=_params(2), name=name)(gr, w)


def ffn_dact(do, w_out, z, name):
    T, D = do.shape
    nb, F, _ = w_out.shape
    tm = 512

    def body(g_ref, w_ref, z_ref, dz_ref):
        da = _dot_nt(g_ref[...], w_ref[...])
        zg = z_ref[0].astype(F32)
        zu = z_ref[1].astype(F32)
        sg = _sigmoid(zg)
        dz_ref[0] = (da * zu * (sg * (1.0 + zg * (1.0 - sg)))).astype(BF16)
        dz_ref[1] = (da * zg * sg).astype(BF16)

    return pl.pallas_call(
        body, grid=(nb, T // tm),
        in_specs=[pl.BlockSpec((tm, D), lambda b, i: (i, 0)), pl.BlockSpec((None, F, D), lambda b, i: (b, 0, 0)),
                  pl.BlockSpec((2, None, tm, F), lambda b, i: (0, b, i, 0))],
        out_specs=pl.BlockSpec((2, None, tm, F), lambda b, i: (0, b, i, 0)),
        out_shape=SDS((2, nb, T, F), BF16), compiler_params=_params(2), name=name)(do, w_out, z)


def mm_tn(a, g, a_batched, g_batched, nb, name):
    T = a.shape[-2]
    K, N = a.shape[-1], g.shape[-1]
    tk = 512
    nk = T // tk

    def body(a_ref, g_ref, o_ref, acc):
        k = pl.program_id(1)

        @pl.when(k == 0)
        def _():
            acc[...] = jnp.zeros_like(acc)

        acc[...] += _dot_tn(a_ref[...].astype(BF16), g_ref[...].astype(BF16))

        @pl.when(k == nk - 1)
        def _():
            o_ref[...] = acc[...].astype(BF16)

    a_spec = (pl.BlockSpec((None, tk, K), lambda b, k: (b, k, 0)) if a_batched
              else pl.BlockSpec((tk, K), lambda b, k: (k, 0)))
    g_spec = (pl.BlockSpec((None, tk, N), lambda b, k: (b, k, 0)) if g_batched
              else pl.BlockSpec((tk, N), lambda b, k: (k, 0)))
    return pl.pallas_call(
        body, grid=(nb, nk), in_specs=[a_spec, g_spec],
        out_specs=pl.BlockSpec((None, K, N), lambda b, k: (b, 0, 0)),
        out_shape=SDS((nb, K, N), BF16), scratch_shapes=[pltpu.VMEM((K, N), F32)],
        compiler_params=_params(2), name=name)(a, g)


def dh_pre_bwd(dz, w, x, g, dyres, name):
    nb, T, F = dz.shape
    D = w.shape[1]
    tm = 512

    def body(dz_ref, w_ref, x_ref, g_ref, dy_ref, dx_ref, dg_ref, acc):
        i, b = pl.program_id(0), pl.program_id(1)

        @pl.when(jnp.logical_and(i == 0, b == 0))
        def _():
            dg_ref[...] = jnp.zeros_like(dg_ref)

        @pl.when(b == 0)
        def _():
            acc[...] = jnp.zeros_like(acc)

        acc[...] += _dot_nt(dz_ref[...].astype(BF16), w_ref[...])

        @pl.when(b == nb - 1)
        def _():
            dh = acc[...]
            xv = x_ref[...]
            rstd = lax.rsqrt(jnp.mean(xv * xv, axis=-1, keepdims=True) + NORM_EPS)
            xh = xv * rstd
            dg_ref[...] += jnp.sum(dh * xh, axis=0, keepdims=True)
            dhg = dh * g_ref[...]
            dx_ref[...] = dy_ref[...] + rstd * (dhg - xh * jnp.mean(dhg * xh, axis=-1, keepdims=True))

    return pl.pallas_call(
        body, grid=(T // tm, nb),
        in_specs=[pl.BlockSpec((None, tm, F), lambda i, b: (b, i, 0)),
                  pl.BlockSpec((None, D, F), lambda i, b: (b, 0, 0)),
                  pl.BlockSpec((tm, D), lambda i, b: (i, 0)),
                  pl.BlockSpec((1, D), lambda i, b: (0, 0)),
                  pl.BlockSpec((tm, D), lambda i, b: (i, 0))],
        out_specs=[pl.BlockSpec((tm, D), lambda i, b: (i, 0)), pl.BlockSpec((1, D), lambda i, b: (0, 0))],
        out_shape=[SDS((T, D), F32), SDS((1, D), F32)],
        scratch_shapes=[pltpu.VMEM((tm, D), F32)], compiler_params=_params(2), name=name)(dz, w, x, g, dyres)


def _attn_masks(n):
    qi = lax.broadcasted_iota(jnp.int32, (QBLK, QBLK), 0)
    kj = lax.broadcasted_iota(jnp.int32, (QBLK, QBLK), 1)
    cur_ok = kj <= qi
    prev_ok = kj >= qi
    dcur = (qi - kj).astype(F32)
    return cur_ok, prev_ok, dcur, dcur + float(QBLK)


def attn_fwd(qkv, d, name):
    _, L, W = qkv.shape
    nb = L // QBLK
    HW = HEADS_PER_HALF * HEAD_DIM

    def body(q_ref, kc_ref, kp_ref, vc_ref, vp_ref, o_ref, l_ref):
        n = pl.program_id(1)
        cur_ok, prev_ok, dcur, dprev = _attn_masks(n)
        prev_ok = jnp.logical_and(prev_ok, n > 0)
        lane_head = lax.broadcasted_iota(jnp.int32, (QBLK, HW), 1) // HEAD_DIM
        for hh in range(2):
            q = q_ref[hh].astype(F32)
            kc, kp, vc, vp = kc_ref[hh], kp_ref[hh], vc_ref[hh], vp_ref[hh]
            o_acc = jnp.zeros((QBLK, HW), F32)
            l_acc = jnp.zeros((QBLK, HW), F32)
            for hq in range(HEADS_PER_HALF):
                slope = float(d) * 2.0 ** -(hh * HEADS_PER_HALF + hq + 1)
                hm = lane_head == hq
                qm = jnp.where(hm, q, 0.0).astype(BF16)
                sc = jnp.where(cur_ok, _dot_nt(qm, kc) * 0.125 - slope * dcur, NEG)
                sp = jnp.where(prev_ok, _dot_nt(qm, kp) * 0.125 - slope * dprev, NEG)
                m = jnp.maximum(jnp.max(sc, axis=1, keepdims=True), jnp.max(sp, axis=1, keepdims=True))
                pc = jnp.exp(sc - m)
                pp = jnp.exp(sp - m)
                den = jnp.sum(pc, axis=1, keepdims=True) + jnp.sum(pp, axis=1, keepdims=True)
                pv = _dot(pc.astype(BF16), vc) + _dot(pp.astype(BF16), vp)
                o_acc = jnp.where(hm, pv * (1.0 / den), o_acc)
                l_acc = jnp.where(hm, m + jnp.log(den), l_acc)
            o_ref[hh] = o_acc
            l_ref[hh] = l_acc

    blk = (2, QBLK, HW)
    return pl.pallas_call(
        body, grid=(d, nb),
        in_specs=[pl.BlockSpec(blk, lambda r, n: (0, n, r)),
                  pl.BlockSpec(blk, lambda r, n: (1, n, r)),
                  pl.BlockSpec(blk, lambda r, n: (1, jnp.maximum(n - 1, 0), r)),
                  pl.BlockSpec(blk, lambda r, n: (2, n, r)),
                  pl.BlockSpec(blk, lambda r, n: (2, jnp.maximum(n - 1, 0), r))],
        out_specs=[pl.BlockSpec(blk, lambda r, n: (0, n, r)), pl.BlockSpec(blk, lambda r, n: (0, n, r))],
        out_shape=[SDS((2, L, W), F32), SDS((2, L, W), F32)],
        compiler_params=_params(2), name=name)(qkv, qkv, qkv, qkv, qkv)


def attn_merge(os_, ls_, name):
    _, T, HW = os_[0].shape
    tm = 512

    def body(o1, o2, o3, l1, l2, l3, o_ref, l_ref):
        a, b, c = l1[...], l2[...], l3[...]
        m = jnp.maximum(jnp.maximum(a, b), c)
        ea, eb, ec = jnp.exp(a - m), jnp.exp(b - m), jnp.exp(c - m)
        s = ea + eb + ec
        o_ref[...] = ((ea * o1[...] + eb * o2[...] + ec * o3[...]) * (1.0 / s)).astype(BF16)
        l_ref[...] = m + jnp.log(s)

    spec = pl.BlockSpec((None, tm, HW), lambda h, i: (h, i, 0))
    return pl.pallas_call(
        body, grid=(2, T // tm), in_specs=[spec] * 6, out_specs=[spec, spec],
        out_shape=[SDS((2, T, HW), BF16), SDS((2, T, HW), F32)],
        compiler_params=_params(2), name=name)(*os_, *ls_)


def attn_bwd(qkv, dcat, o, lse, acc, d, name):
    _, L, W = qkv.shape
    nb = L // QBLK
    HW = HEADS_PER_HALF * HEAD_DIM
    has_acc = acc is not None

    def body(*refs):
        (qc_ref, qn_ref, kc_ref, kp_ref, vc_ref, vp_ref, dc_ref, dn_ref, oc_ref, on_ref, lc_ref, ln_ref) = refs[:12]
        acc_ref = refs[12] if has_acc else None
        out_ref = refs[-1]
        n = pl.program_id(1)
        cur_ok, prev_ok, dcur, dprev = _attn_masks(n)
        pok_c = jnp.logical_and(prev_ok, n > 0)
        pok_n = jnp.logical_and(prev_ok, n < nb - 1)
        lane_head = lax.broadcasted_iota(jnp.int32, (QBLK, HW), 1) // HEAD_DIM
        for hh in range(2):
            q_c, q_n = qc_ref[hh].astype(F32), qn_ref[hh].astype(F32)
            k_c, k_p, v_c, v_p = kc_ref[hh], kp_ref[hh], vc_ref[hh], vp_ref[hh]
            do_c, do_n = dc_ref[hh], dn_ref[hh]
            o_c, o_n = oc_ref[hh].astype(F32), on_ref[hh].astype(F32)
            l_c, l_n = lc_ref[hh], ln_ref[hh]
            dq = jnp.zeros((QBLK, HW), F32)
            dk = jnp.zeros((QBLK, HW), F32)
            dv = jnp.zeros((QBLK, HW), F32)
            for hq in range(HEADS_PER_HALF):
                slope = float(d) * 2.0 ** -(hh * HEADS_PER_HALF + hq + 1)
                hm = lane_head == hq
                qm_c = jnp.where(hm, q_c, 0.0).astype(BF16)
                qm_n = jnp.where(hm, q_n, 0.0).astype(BF16)
                dom_c = jnp.where(hm, do_c, 0.0)
                dom_n = jnp.where(hm, do_n, 0.0)
                dd_c = jnp.sum(dom_c * o_c, axis=1, keepdims=True)
                dd_n = jnp.sum(dom_n * o_n, axis=1, keepdims=True)
                ls_c = jnp.max(jnp.where(hm, l_c, NEG), axis=1, keepdims=True)
                ls_n = jnp.max(jnp.where(hm, l_n, NEG), axis=1, keepdims=True)
                dob_c, dob_n = dom_c.astype(BF16), dom_n.astype(BF16)
                s = jnp.where(cur_ok, _dot_nt(qm_c, k_c) * 0.125 - slope * dcur, NEG)
                p = jnp.exp(s - ls_c)
                ds = (p * (_dot_nt(dob_c, v_c) - dd_c)).astype(BF16)
                dq_h = _dot(ds, k_c)
                dk_h = _dot_tn(ds, qm_c)
                dv_h = _dot_tn(p.astype(BF16), dob_c)
                s = jnp.where(pok_c, _dot_nt(qm_c, k_p) * 0.125 - slope * dprev, NEG)
                p = jnp.exp(s - ls_c)
                ds = (p * (_dot_nt(dob_c, v_p) - dd_c)).astype(BF16)
                dq_h += _dot(ds, k_p)
                s = jnp.where(pok_n, _dot_nt(qm_n, k_c) * 0.125 - slope * dprev, NEG)
                p = jnp.exp(s - ls_n)
                ds = (p * (_dot_nt(dob_n, v_c) - dd_n)).astype(BF16)
                dk_h += _dot_tn(ds, qm_n)
                dv_h += _dot_tn(p.astype(BF16), dob_n)
                dq = jnp.where(hm, dq_h * 0.125, dq)
                dk += dk_h * 0.125
                dv += dv_h
            if has_acc:
                dq += acc_ref[hh]
                dk += acc_ref[2 + hh]
                dv += acc_ref[4 + hh]
            out_ref[hh] = dq
            out_ref[2 + hh] = dk
            out_ref[4 + hh] = dv

    blk = (2, QBLK, HW)
    cur = lambda part: pl.BlockSpec(blk, lambda r, n: (part, n, r))
    prv = lambda part: pl.BlockSpec(blk, lambda r, n: (part, jnp.maximum(n - 1, 0), r))
    nxt = lambda part: pl.BlockSpec(blk, lambda r, n: (part, jnp.minimum(n + 1, nb - 1), r))
    full = pl.BlockSpec((6, QBLK, HW), lambda r, n: (0, n, r))
    in_specs = [cur(0), nxt(0), cur(1), prv(1), cur(2), prv(2), cur(0), nxt(0), cur(0), nxt(0), cur(0), nxt(0)]
    args = [qkv, qkv, qkv, qkv, qkv, qkv, dcat, dcat, o, o, lse, lse]
    if has_acc:
        in_specs.append(full)
        args.append(acc)
    return pl.pallas_call(
        body, grid=(d, nb), in_specs=in_specs, out_specs=full,
        out_shape=SDS((6, L, W), F32), compiler_params=_params(2), name=name)(*args)


def _scan_chunk(buf, apow_ref, c0, reverse, ptab=None, carry=None):
    tm = buf.shape[0]
    cre = pl.ds(c0, SCAN_CW)
    cim = pl.ds(pl.multiple_of(c0 + HALF_STATES, 128), SCAN_CW)
    re, im = buf[:, cre], buf[:, cim]
    row = lax.broadcasted_iota(jnp.int32, (tm, SCAN_CW), 0)
    sgn = -1.0 if reverse else 1.0
    k, s = 1, 0
    while k < tm:
        ar = apow_ref[pl.ds(s, 1), cre]
        ai = sgn * apow_ref[pl.ds(s, 1), cim]
        if reverse:
            ok, shift = row < tm - k, tm - k
        else:
            ok, shift = row >= k, k
        sre = jnp.where(ok, pltpu.roll(re, shift, 0), 0.0)
        sim = jnp.where(ok, pltpu.roll(im, shift, 0), 0.0)
        re, im = re + ar * sre - ai * sim, im + ar * sim + ai * sre
        k, s = 2 * k, s + 1
    if ptab is not None:
        pr, pi = ptab[:, cre], ptab[:, cim]
        cr, ci = carry[:, cre], carry[:, cim]
        re, im = re + pr * cr - pi * ci, im + pr * ci + pi * cr
    buf[:, cre] = re
    buf[:, cim] = im


def _scan_tile(buf, apow_ref, reverse, ptab=None, carry=None):
    per_half = HALF_STATES // SCAN_CW

    def chunk(j, _):
        c0 = pl.multiple_of((j // per_half) * 2 * HALF_STATES + (j % per_half) * SCAN_CW, 128)
        _scan_chunk(buf, apow_ref, c0, reverse, ptab, carry)
        return 0

    lax.fori_loop(0, 2 * per_half, chunk, 0)


def _init_ptab(ptab, apow_ref, reverse):
    tm, width = ptab.shape
    row = lax.broadcasted_iota(jnp.int32, (tm, width), 0)
    col = lax.broadcasted_iota(jnp.int32, (1, width), 1)
    a = apow_ref[pl.ds(0, 1), :]
    if reverse:
        is_im = (col // HALF_STATES) % 2 == 1
        a = jnp.where(is_im, -a, a)
    seed = (tm - 1) if reverse else 0
    ptab[...] = jnp.where(row == seed, a, 0.0)
    _scan_tile(ptab, apow_ref, reverse)


def ssm_fwd(u, bh, ch, apow, dskip, name):
    _, T, C = u.shape
    tm = SCAN_TM
    SW = 4 * HALF_STATES

    def body(u_ref, bh_ref, ch_ref, apow_ref, dsk_ref, y_ref, s_ref, ptab, carry):
        @pl.when(pl.program_id(0) == 0)
        def _():
            _init_ptab(ptab, apow_ref, False)
            carry[...] = jnp.zeros_like(carry)

        for h in range(2):
            s_ref[:, pl.ds(h * 2 * HALF_STATES, 2 * HALF_STATES)] = _dot(u_ref[h].astype(BF16), bh_ref[h])
        _scan_tile(s_ref, apow_ref, False, ptab, carry)
        carry[...] = s_ref[pl.ds(tm - 1, 1), :]
        for h in range(2):
            sv = s_ref[:, pl.ds(h * 2 * HALF_STATES, 2 * HALF_STATES)].astype(BF16)
            y_ref[h] = _dot(sv, ch_ref[h]) + dsk_ref[h] * u_ref[h]

    return pl.pallas_call(
        body, grid=(T // tm,),
        in_specs=[pl.BlockSpec((2, tm, C), lambda i: (0, i, 0)),
                  pl.BlockSpec((2, C, 2 * HALF_STATES), lambda i: (0, 0, 0)),
                  pl.BlockSpec((2, 2 * HALF_STATES, C), lambda i: (0, 0, 0)),
                  pl.BlockSpec((8, SW), lambda i: (0, 0)),
                  pl.BlockSpec((2, 1, C), lambda i: (0, 0, 0))],
        out_specs=[pl.BlockSpec((2, tm, C), lambda i: (0, i, 0)), pl.BlockSpec((tm, SW), lambda i: (i, 0))],
        out_shape=[SDS((2, T, C), F32), SDS((T, SW), F32)],
        scratch_shapes=[pltpu.VMEM((tm, SW), F32), pltpu.VMEM((1, SW), F32)],
        compiler_params=_params(1), name=name)(u, bh, ch, apow, dskip)


def ssm_bwd(dy, u, st, bh, ch, apow, dskip, name):
    _, T, C = u.shape
    tm = SCAN_TM
    nt = T // tm
    SW = 4 * HALF_STATES
    HS2 = 2 * HALF_STATES

    def body(dy_ref, u_ref, s_ref, sp_ref, bh_ref, ch_ref, apow_ref, dsk_ref,
             du_ref, da_ref, dbh_ref, dch_ref, dd_ref, lam, ptab, carry):
        i = pl.program_id(0)

        @pl.when(i == 0)
        def _():
            _init_ptab(ptab, apow_ref, True)
            carry[...] = jnp.zeros_like(carry)
            da_ref[...] = jnp.zeros_like(da_ref)
            dbh_ref[...] = jnp.zeros_like(dbh_ref)
            dch_ref[...] = jnp.zeros_like(dch_ref)
            dd_ref[...] = jnp.zeros_like(dd_ref)

        for h in range(2):
            lam[:, pl.ds(h * HS2, HS2)] = _dot_nt(dy_ref[h].astype(BF16), ch_ref[h])
        _scan_tile(lam, apow_ref, True, ptab, carry)
        carry[...] = lam[pl.ds(0, 1), :]

        first = i == nt - 1
        per_half = HALF_STATES // SCAN_CW

        def chunk(j, _):
            c0 = pl.multiple_of((j // per_half) * HS2 + (j % per_half) * SCAN_CW, 128)
            cre, cim = pl.ds(c0, SCAN_CW), pl.ds(pl.multiple_of(c0 + HALF_STATES, 128), SCAN_CW)
            row = lax.broadcasted_iota(jnp.int32, (tm, SCAN_CW), 0)
            pre = jnp.where(first, 0.0, sp_ref[pl.ds(7, 1), cre])
            pim = jnp.where(first, 0.0, sp_ref[pl.ds(7, 1), cim])
            spr = jnp.where(row == 0, pre, pltpu.roll(s_ref[:, cre], 1, 0))
            spi = jnp.where(row == 0, pim, pltpu.roll(s_ref[:, cim], 1, 0))
            lr, li = lam[:, cre], lam[:, cim]
            da_ref[:, cre] += jnp.sum(lr * spr + li * spi, axis=0, keepdims=True)
            da_ref[:, cim] += jnp.sum(li * spr - lr * spi, axis=0, keepdims=True)
            return 0

        lax.fori_loop(0, 2 * per_half, chunk, 0)

        for h in range(2):
            lb = lam[:, pl.ds(h * HS2, HS2)].astype(BF16)
            dyv, uv = dy_ref[h], u_ref[h]
            du_ref[h] = _dot_nt(lb, bh_ref[h]) + dsk_ref[h] * dyv
            dbh_ref[h] += _dot_tn(uv.astype(BF16), lb)
            dch_ref[h] += _dot_tn(s_ref[:, pl.ds(h * HS2, HS2)].astype(BF16), dyv.astype(BF16))
            dd_ref[h] += jnp.sum(dyv * uv, axis=0, keepdims=True)

    rev = lambda i: nt - 1 - i
    return pl.pallas_call(
        body, grid=(nt,),
        in_specs=[pl.BlockSpec((2, tm, C), lambda i: (0, rev(i), 0)),
                  pl.BlockSpec((2, tm, C), lambda i: (0, rev(i), 0)),
                  pl.BlockSpec((tm, SW), lambda i: (rev(i), 0)),
                  pl.BlockSpec((8, SW), lambda i: (jnp.maximum(rev(i) * (tm // 8) - 1, 0), 0)),
                  pl.BlockSpec((2, C, HS2), lambda i: (0, 0, 0)),
                  pl.BlockSpec((2, HS2, C), lambda i: (0, 0, 0)),
                  pl.BlockSpec((8, SW), lambda i: (0, 0)),
                  pl.BlockSpec((2, 1, C), lambda i: (0, 0, 0))],
        out_specs=[pl.BlockSpec((2, tm, C), lambda i: (0, rev(i), 0)),
                   pl.BlockSpec((1, SW), lambda i: (0, 0)),
                   pl.BlockSpec((2, C, HS2), lambda i: (0, 0, 0)),
                   pl.BlockSpec((2, HS2, C), lambda i: (0, 0, 0)),
                   pl.BlockSpec((2, 1, C), lambda i: (0, 0, 0))],
        out_shape=[SDS((2, T, C), F32), SDS((1, SW), F32), SDS((2, C, HS2), F32), SDS((2, HS2, C), F32),
                   SDS((2, 1, C), F32)],
        scratch_shapes=[pltpu.VMEM((tm, SW), F32), pltpu.VMEM((tm, SW), F32), pltpu.VMEM((1, SW), F32)],
        compiler_params=_params(1), name=name)(dy, u, st, st, bh, ch, apow, dskip)


_GELU_C = math.sqrt(2.0 / math.pi)


def _gelu(x):
    t = jnp.tanh(_GELU_C * (x + 0.044715 * x * x * x))
    return 0.5 * x * (1.0 + t), t


def glu_fwd(y, w, b, name):
    _, T, C = y.shape
    tm = 512

    def body(y_ref, w_ref, b_ref, o_ref, lg_ref):
        y0, _ = _gelu(y_ref[0])
        y1, _ = _gelu(y_ref[1])
        lg = _dot(y0.astype(BF16), w_ref[0]) + _dot(y1.astype(BF16), w_ref[1]) + b_ref[...]
        sg = _sigmoid(lg)
        o_ref[0] = (y0 * sg[:, :C]).astype(BF16)
        o_ref[1] = (y1 * sg[:, C:]).astype(BF16)
        lg_ref[0] = lg[:, :C]
        lg_ref[1] = lg[:, C:]

    return pl.pallas_call(
        body, grid=(T // tm,),
        in_specs=[pl.BlockSpec((2, tm, C), lambda i: (0, i, 0)), pl.BlockSpec((2, C, 2 * C), lambda i: (0, 0, 0)),
                  pl.BlockSpec((1, 2 * C), lambda i: (0, 0))],
        out_specs=[pl.BlockSpec((2, tm, C), lambda i: (0, i, 0)), pl.BlockSpec((2, tm, C), lambda i: (0, i, 0))],
        out_shape=[SDS((2, T, C), BF16), SDS((2, T, C), F32)], compiler_params=_params(1), name=name)(y, w, b)


def glu_bwd(dcat, y, lg, w, name):
    _, T, C = y.shape
    tm = 512

    def body(d_ref, y_ref, lg_ref, w_ref, dy_ref, dw_ref, db_ref):
        @pl.when(pl.program_id(0) == 0)
        def _():
            dw_ref[...] = jnp.zeros_like(dw_ref)
            db_ref[...] = jnp.zeros_like(db_ref)

        y2, th, sg, dlg = [], [], [], []
        for h in range(2):
            yy, tt = _gelu(y_ref[h])
            ss = _sigmoid(lg_ref[h])
            y2.append(yy)
            th.append(tt)
            sg.append(ss)
            dlg.append(d_ref[h] * yy * ss * (1.0 - ss))
        dl = jnp.concatenate(dlg, axis=1)
        dlb = dl.astype(BF16)
        db_ref[...] += jnp.sum(dl, axis=0, keepdims=True)
        for h in range(2):
            dy2 = d_ref[h] * sg[h] + _dot_nt(dlb, w_ref[h])
            yv = y_ref[h]
            dgelu = 0.5 * (1.0 + th[h]) + 0.5 * yv * (1.0 - th[h] * th[h]) * _GELU_C * (1.0 + 3 * 0.044715 * yv * yv)
            dy_ref[h] = dy2 * dgelu
            dw_ref[h] += _dot_tn(y2[h].astype(BF16), dlb)

    return pl.pallas_call(
        body, grid=(T // tm,),
        in_specs=[pl.BlockSpec((2, tm, C), lambda i: (1, i, 0)), pl.BlockSpec((2, tm, C), lambda i: (0, i, 0)),
                  pl.BlockSpec((2, tm, C), lambda i: (0, i, 0)), pl.BlockSpec((2, C, 2 * C), lambda i: (0, 0, 0))],
        out_specs=[pl.BlockSpec((2, tm, C), lambda i: (0, i, 0)), pl.BlockSpec((2, C, 2 * C), lambda i: (0, 0, 0)),
                   pl.BlockSpec((1, 2 * C), lambda i: (0, 0))],
        out_shape=[SDS((2, T, C), F32), SDS((2, C, 2 * C), F32), SDS((1, 2 * C), F32)],
        compiler_params=_params(1), name=name)(dcat, y, lg, w)


def adamw(w, m, v, slots, name):
    R, C = w.shape
    tr = R
    for cand in (512, 256, 128, 64, 32, 16, 8):
        if R % cand == 0 and cand * C * 4 <= 2 * 1024 * 1024:
            tr = cand
            break
    c1 = 1.0 / (1.0 - ADAM_B1 ** ADAM_STEP)
    c2 = 1.0 / (1.0 - ADAM_B2 ** ADAM_STEP)

    def body(w_ref, m_ref, v_ref, s_ref, g_ref, d_ref, nm_ref, nv_ref):
        g = s_ref[0].astype(F32)
        for j in range(1, N_DEV):
            g = g + s_ref[j].astype(F32)
        nm = ADAM_B1 * m_ref[...] + (1.0 - ADAM_B1) * g
        nv = ADAM_B2 * v_ref[...] + (1.0 - ADAM_B2) * (g * g)
        g_ref[...] = g
        nm_ref[...] = nm
        nv_ref[...] = nv
        d_ref[...] = -ADAM_LR * ((nm * c1) / (jnp.sqrt(nv * c2) + ADAM_EPS) + ADAM_WD * w_ref[...])

    spec = pl.BlockSpec((tr, C), lambda i: (i, 0))
    return pl.pallas_call(
        body, grid=(R // tr,),
        in_specs=[spec, spec, spec, pl.BlockSpec((N_DEV, tr, C), lambda i: (0, i, 0))],
        out_specs=[spec] * 4, out_shape=[SDS((R, C), F32)] * 4, compiler_params=_params(1), name=name)(w, m, v, slots)


def exchange(arrs, scatter, name):
    n = len(arrs)
    out_shapes = [SDS(a.shape if scatter[i] else (N_DEV,) + a.shape, a.dtype) for i, a in enumerate(arrs)]

    def body(*refs):
        ins, outs = refs[:n], refs[n:2 * n]
        send_sems, recv_sems, loc_sems = refs[2 * n:]
        x, y, c = lax.axis_index("x"), lax.axis_index("y"), lax.axis_index("c")
        me = 4 * x + 2 * y + c
        own_copies, sends, arrivals = [], [], []
        for i in range(n):
            own = ins[i].at[me] if scatter[i] else ins[i]
            lc = pltpu.make_async_copy(own, outs[i].at[me], loc_sems.at[i])
            lc.start()
            own_copies.append(lc)
            for k in range(1, N_DEV):
                px = 1 - x if k & 4 else x
                py = 1 - y if k & 2 else y
                pc = 1 - c if k & 1 else c
                peer = 4 * px + 2 * py + pc
                src = ins[i].at[peer] if scatter[i] else ins[i]
                cp = pltpu.make_async_remote_copy(
                    src_ref=src, dst_ref=outs[i].at[me], send_sem=send_sems.at[i, k - 1],
                    recv_sem=recv_sems.at[i, k - 1], device_id=(px, py, pc), device_id_type=pl.DeviceIdType.MESH)
                cp.start()
                sends.append(cp)
                arrivals.append(pltpu.make_async_remote_copy(
                    src_ref=src, dst_ref=outs[i].at[peer], send_sem=send_sems.at[i, k - 1],
                    recv_sem=recv_sems.at[i, k - 1], device_id=(px, py, pc), device_id_type=pl.DeviceIdType.MESH))
        for cp in arrivals:
            cp.wait_recv()
        for cp in sends:
            cp.wait_send()
        for cp in own_copies:
            cp.wait()

    anyspec = pl.BlockSpec(memory_space=pl.ANY)
    return pl.pallas_call(
        body, in_specs=[anyspec] * n, out_specs=[anyspec] * n, out_shape=out_shapes,
        scratch_shapes=[pltpu.SemaphoreType.DMA((n, N_DEV - 1)), pltpu.SemaphoreType.DMA((n, N_DEV - 1)),
                        pltpu.SemaphoreType.DMA((n,))],
        compiler_params=pltpu.CompilerParams(has_side_effects=True), name=name)(*arrs)


def _discretise(a_re, a_im, log_dt, b_re, b_im):
    dt = jnp.exp(log_dt)[:, None]
    e = jnp.exp(dt * a_re)
    ar, ai = e * jnp.cos(dt * a_im), e * jnp.sin(dt * a_im)
    den = a_re * a_re + a_im * a_im
    nr, ni = ar - 1.0, ai
    wr = (nr * a_re + ni * a_im) / den
    wi = (ni * a_re - nr * a_im) / den
    bbr = wr[..., None] * b_re - wi[..., None] * b_im
    bbi = wr[..., None] * b_im + wi[..., None] * b_re
    return ar, ai, bbr, bbi


def _block_diag(t):
    eye = jnp.eye(16, dtype=t.dtype)
    r, c = t.shape[1], t.shape[2]
    return jnp.einsum("hgrc,gk->hgrkc", t.reshape(2, 16, r, c), eye).reshape(2, 16 * r, 16 * c)


def _diag_blocks(m, r, c):
    eye = jnp.eye(16, dtype=m.dtype)
    return jnp.einsum("hgrkc,gk->hgrc", m.reshape(2, 16, r, 16, c), eye).reshape(32, r, c)


def _state_vec(re, im):
    return jnp.stack([re.reshape(2, HALF_STATES), im.reshape(2, HALF_STATES)], axis=1).reshape(-1)


def _ffn_fwd(x, pre_g, w_in, w_out4, post_g, tag):
    h = rms_fwd(x, pre_g, f"{tag}_rms")
    z, a = ffn_in(h, w_in, f"{tag}_in")
    o, y = mm_acc_norm(a, w_out4, x, post_g, 0.5, f"{tag}_out")
    return y, (h, z, a, o)


def _ffn_bwd(dy, saved, x, pre_g, w_in, w_out4, post_g, tag):
    h, z, a, o = saved
    T = x.shape[0]
    do, dg_post = post_bwd(dy, o, post_g, 0.5, f"{tag}_post_bwd")
    dz = ffn_dact(do, w_out4, z, f"{tag}_dact")
    dw_out = mm_tn(a, do, True, False, 4, f"{tag}_dwout")
    dz8 = dz.reshape(8, T, dz.shape[-1])
    dw_in = mm_tn(h, dz8, False, True, 8, f"{tag}_dwin")
    dx, dg_pre = dh_pre_bwd(dz8, w_in, x, pre_g, dy, f"{tag}_dh")
    return dx, dg_pre, dw_in, dw_out, dg_post


def local_step(x, tgt, gw, sp):
    T = x.shape[0]
    w1_in, w1_out4 = gw["ffn1_w_in"], gw["ffn1_w_out"].reshape(4, -1, D_MODEL)
    w2_in, w2_out4 = gw["ffn2_w_in"], gw["ffn2_w_out"].reshape(4, -1, D_MODEL)
    w_mi, w_mo4 = gw["w_mix_in"], gw["w_mix_out"].reshape(4, 256, D_MODEL)
    w_glu2 = gw["w_glu"].reshape(2, 256, 512)

    ar, ai, bbr, bbi = _discretise(sp["a_re"], sp["a_im"], sp["log_dt"], sp["b_re"], sp["b_im"])
    pr, pi, rows = ar, ai, []
    for _ in range(8):
        rows.append(_state_vec(pr, pi))
        pr, pi = pr * pr - pi * pi, 2.0 * pr * pi
    apow = jnp.stack(rows)
    bh = jnp.concatenate([_block_diag(bbr.transpose(0, 2, 1)), _block_diag(bbi.transpose(0, 2, 1))], axis=2)
    ch = jnp.concatenate([_block_diag(sp["c_re"].transpose(0, 2, 1)), _block_diag(-sp["c_im"].transpose(0, 2, 1))], axis=1)
    bh, ch = bh.astype(BF16), ch.astype(BF16)
    dskip = sp["d_skip"].reshape(2, 1, 256)

    x1, ffn1_saved = _ffn_fwd(x, sp["ffn1_pre_g"], w1_in, w1_out4, sp["ffn1_post_g"], "ffn1")
    h2 = rms_fwd(x1, sp["mix_pre_g"], "mix_rms")
    qkv = mm_nn_b(h2, w_mi, 0, 6, BF16, "mix_proj_qkv")
    u = mm_nn_b(h2, w_mi, 6, 2, F32, "mix_proj_u")
    os_, ls_ = [], []
    for d in DILATIONS:
        o_d, l_d = attn_fwd(qkv.reshape(6, T // d, d * 256), d, f"attn_fwd_d{d}")
        os_.append(o_d.reshape(2, T, 256))
        ls_.append(l_d.reshape(2, T, 256))
    o_att, lse = attn_merge(os_, ls_, "attn_merge")
    y_ssm, states = ssm_fwd(u, bh, ch, apow, dskip, "ssm_fwd")
    o_ssm, lg = glu_fwd(y_ssm, w_glu2, sp["b_glu"], "glu_fwd")
    cat = jnp.concatenate([o_att, o_ssm], axis=0)
    mixed, x2 = mm_acc_norm(cat, w_mo4, x1, sp["mix_post_g"], 1.0, "mix_out")
    x3, ffn2_saved = _ffn_fwd(x2, sp["ffn2_pre_g"], w2_in, w2_out4, sp["ffn2_post_g"], "ffn2")
    dy3, sq = loss_head(x3, tgt, "loss_head")

    dx2, dg_f2pre, dw2_in, dw2_out, dg_f2post = _ffn_bwd(
        dy3, ffn2_saved, x2, sp["ffn2_pre_g"], w2_in, w2_out4, sp["ffn2_post_g"], "ffn2")
    dmixed, dg_mpost = post_bwd(dx2, mixed, sp["mix_post_g"], 1.0, "mix_post_bwd")
    dcat = mm_nt_b(dmixed, w_mo4, "mix_dcat")
    dw_mo = mm_tn(cat, dmixed, True, False, 4, "mix_dwout")
    dy_ssm, dw_glu, db_glu = glu_bwd(dcat, y_ssm, lg, w_glu2, "glu_bwd")
    du, da, dbh, dch, dd = ssm_bwd(dy_ssm, u, states, bh, ch, apow, dskip, "ssm_bwd")
    dqkv = None
    for d in DILATIONS:
        v = lambda t: t.reshape(t.shape[0], T // d, d * 256)
        dqkv = attn_bwd(v(qkv), v(dcat), v(o_att), v(lse), None if dqkv is None else v(dqkv), d, f"attn_bwd_d{d}")
    dproj = jnp.concatenate([dqkv.reshape(6, T, 256), du], axis=0)
    dw_mi = mm_tn(h2, dproj, False, True, 8, "mix_dwin")
    dx1, dg_mpre = dh_pre_bwd(dproj, w_mi, x1, sp["mix_pre_g"], dx2, "mix_dh")
    dx0, dg_f1pre, dw1_in, dw1_out, dg_f1post = _ffn_bwd(
        dx1, ffn1_saved, x, sp["ffn1_pre_g"], w1_in, w1_out4, sp["ffn1_post_g"], "ffn1")

    da4 = da.reshape(2, 2, HALF_STATES)
    d_ar, d_ai = da4[:, 0].reshape(32, N_STATE), da4[:, 1].reshape(32, N_STATE)
    d_bbr = _diag_blocks(dbh[:, :, :HALF_STATES], 16, N_STATE).transpose(0, 2, 1)
    d_bbi = _diag_blocks(dbh[:, :, HALF_STATES:], 16, N_STATE).transpose(0, 2, 1)
    _, disc_vjp = jax.vjp(_discretise, sp["a_re"], sp["a_im"], sp["log_dt"], sp["b_re"], sp["b_im"])
    g_are, g_aim, g_ldt, g_bre, g_bim = disc_vjp((d_ar, d_ai, d_bbr, d_bbi))
    g_cre = _diag_blocks(dch[:, :HALF_STATES], N_STATE, 16).transpose(0, 2, 1)
    g_cim = -_diag_blocks(dch[:, HALF_STATES:], N_STATE, 16).transpose(0, 2, 1)

    big = {
        "ffn1_w_in": dw1_in, "ffn1_w_out": dw1_out.reshape(8, -1, D_MODEL), "w_mix_in": dw_mi,
        "w_glu": dw_glu.astype(BF16).reshape(8, 64, 512), "w_mix_out": dw_mo.reshape(8, 128, D_MODEL),
        "ffn2_w_in": dw2_in, "ffn2_w_out": dw2_out.reshape(8, -1, D_MODEL),
    }
    small = {
        "ffn1_pre_g": dg_f1pre, "ffn1_post_g": dg_f1post, "mix_pre_g": dg_mpre, "a_re": g_are, "a_im": g_aim,
        "log_dt": g_ldt, "b_re": g_bre, "b_im": g_bim, "c_re": g_cre, "c_im": g_cim, "d_skip": dd.reshape(1, 512),
        "b_glu": db_glu, "mix_post_g": dg_mpost, "ffn2_pre_g": dg_f2pre, "ffn2_post_g": dg_f2post,
    }
    return sq, dx0, big, small


BIG = ("ffn1_w_in", "ffn1_w_out", "w_mix_in", "w_glu", "w_mix_out", "ffn2_w_in", "ffn2_w_out")
WEIGHTS = ("ffn1_pre_g", "ffn1_w_in", "ffn1_w_out", "ffn1_post_g", "mix_pre_g", "w_mix_in", "a_re", "a_im", "log_dt",
           "b_re", "b_im", "c_re", "c_im", "d_skip", "w_glu", "b_glu", "w_mix_out", "mix_post_g", "ffn2_pre_g",
           "ffn2_w_in", "ffn2_w_out", "ffn2_post_g")
SMALL = tuple(n for n in WEIGHTS if n not in BIG)
PACK_COLS = 1024


def _pack(parts):
    flat = jnp.concatenate([p.reshape(-1) for p in parts])
    rows = -(-flat.shape[0] // (8 * PACK_COLS)) * 8
    return jnp.pad(flat, (0, rows * PACK_COLS - flat.shape[0])).reshape(rows, PACK_COLS)


def _unpack(packed, shapes):
    flat, out, off = packed.reshape(-1), [], 0
    for s in shapes:
        n = math.prod(s)
        out.append(flat[off:off + n].reshape(s))
        off += n
    return out


def kernel(x, ffn1_pre_g, ffn1_w_in, ffn1_w_out, ffn1_post_g, mix_pre_g, w_mix_in, a_re, a_im, log_dt, b_re, b_im, c_re, c_im, d_skip, w_glu, b_glu, w_mix_out, mix_post_g, ffn2_pre_g, ffn2_w_in, ffn2_w_out, ffn2_post_g, loss_target, m_ffn1_pre_g, m_ffn1_w_in, m_ffn1_w_out, m_ffn1_post_g, m_mix_pre_g, m_w_mix_in, m_a_re, m_a_im, m_log_dt, m_b_re, m_b_im, m_c_re, m_c_im, m_d_skip, m_w_glu, m_b_glu, m_w_mix_out, m_mix_post_g, m_ffn2_pre_g, m_ffn2_w_in, m_ffn2_w_out, m_ffn2_post_g, v_ffn1_pre_g, v_ffn1_w_in, v_ffn1_w_out, v_ffn1_post_g, v_mix_pre_g, v_w_mix_in, v_a_re, v_a_im, v_log_dt, v_b_re, v_b_im, v_c_re, v_c_im, v_d_skip, v_w_glu, v_b_glu, v_w_mix_out, v_mix_post_g, v_ffn2_pre_g, v_ffn2_w_in, v_ffn2_w_out, v_ffn2_post_g):
    args = dict(locals())
    w = {n: args[n][0] for n in WEIGHTS}
    m = {n: args["m_" + n][0] for n in WEIGHTS}
    v = {n: args["v_" + n][0] for n in WEIGHTS}

    gathered = exchange([w[n].astype(BF16) for n in BIG], [False] * len(BIG), "gather_weights")
    gw = dict(zip(BIG, gathered))
    sp = {n: w[n] for n in SMALL}
    for n in ("ffn1_pre_g", "ffn1_post_g", "mix_pre_g", "mix_post_g", "ffn2_pre_g", "ffn2_post_g", "b_glu", "d_skip"):
        sp[n] = w[n].reshape(1, -1)

    sq, grad_x, big, small = local_step(x[0], loss_target[0], gw, sp)
    loss = lax.psum(0.5 / D_MODEL * jnp.sum(sq), ("x", "y", "c"))

    small_packed = _pack([small[n] for n in SMALL])
    slots = exchange([big[n] for n in BIG] + [small_packed], [True] * len(BIG) + [False], "exchange_grads")

    outs = {}
    for n, s in zip(BIG, slots[:-1]):
        shp = w[n].shape
        r2 = lambda t: t.reshape(-1, shp[-1])
        res = adamw(r2(w[n]), r2(m[n]), r2(v[n]), s.reshape(N_DEV, -1, shp[-1]), f"adamw_{n}")
        outs[n] = [t.reshape((1,) + shp) for t in res]
    res = adamw(_pack([w[n] for n in SMALL]), _pack([m[n] for n in SMALL]), _pack([v[n] for n in SMALL]),
                slots[-1], "adamw_small")
    shapes = [(1,) + w[n].shape for n in SMALL]
    unpacked = [_unpack(t, shapes) for t in res]
    for j, n in enumerate(SMALL):
        outs[n] = [unpacked[k][j] for k in range(4)]

    result = [loss, grad_x[None]]
    for k in range(4):
        result += [outs[n][k] for n in WEIGHTS]
    return tuple(result)
```

```python
import functools
import math

import jax
import jax.numpy as jnp
from jax import lax
from jax.experimental import pallas as pl
from jax.experimental.pallas import tpu as pltpu

F32, BF16 = jnp.float32, jnp.bfloat16
SDS = jax.ShapeDtypeStruct

D_MODEL = 1024
N_DEV = 8
HEAD_DIM = 64
PAIR_W = 128
QBLK = 128
DILATIONS = (1, 4, 16)
N_STATE = 64
HALF_STATES = 1024
NORM_EPS = 1e-6
NEG = -1e30
VMEM_LIMIT = 56 * 1024 * 1024
ADAM_LR, ADAM_B1, ADAM_B2, ADAM_EPS, ADAM_WD, ADAM_STEP = 1e-3, 0.9, 0.999, 1e-8, 0.01, 10
SCAN_TM = 256
SCAN_CW = 256


def _params(n_grid):
    return pltpu.CompilerParams(dimension_semantics=("arbitrary",) * n_grid, vmem_limit_bytes=VMEM_LIMIT)


def _dot(a, b):
    return jnp.dot(a, b, preferred_element_type=F32)


def _dot_nt(a, b):
    return lax.dot_general(a, b, (((1,), (1,)), ((), ())), preferred_element_type=F32)


def _dot_tn(a, b):
    return lax.dot_general(a, b, (((0,), (0,)), ((), ())), preferred_element_type=F32)


def _sigmoid(v):
    return 1.0 / (1.0 + jnp.exp(-v))


def _exchange_phase(ins, outs, scatter, sems, start):
    send_sems, recv_sems, loc_sems = sems
    x, y, c = lax.axis_index("x"), lax.axis_index("y"), lax.axis_index("c")
    me = 4 * x + 2 * y + c
    own_copies, sends, arrivals = [], [], []
    for i in range(len(ins)):
        own = ins[i].at[me] if scatter[i] else ins[i]
        own_copies.append(pltpu.make_async_copy(own, outs[i].at[me], loc_sems.at[i]))
        for k in range(1, N_DEV):
            px = 1 - x if k & 4 else x
            py = 1 - y if k & 2 else y
            pc = 1 - c if k & 1 else c
            peer = 4 * px + 2 * py + pc
            src = ins[i].at[peer] if scatter[i] else ins[i]
            common = dict(src_ref=src, send_sem=send_sems.at[i, k - 1], recv_sem=recv_sems.at[i, k - 1],
                          device_id=(px, py, pc), device_id_type=pl.DeviceIdType.MESH)
            sends.append(pltpu.make_async_remote_copy(dst_ref=outs[i].at[me], **common))
            arrivals.append(pltpu.make_async_remote_copy(dst_ref=outs[i].at[peer], **common))
    if start:
        for cp in own_copies + sends:
            cp.start()
    else:
        for cp in arrivals:
            cp.wait_recv()
        for cp in sends:
            cp.wait_send()
        for cp in own_copies:
            cp.wait()


def _comm_shapes(arrs, scatter):
    n = len(arrs)
    out_shapes = [SDS(a.shape if scatter[i] else (N_DEV,) + a.shape, a.dtype) for i, a in enumerate(arrs)]
    sems = [pltpu.SemaphoreType.DMA((n, N_DEV - 1)), pltpu.SemaphoreType.DMA((n, N_DEV - 1)),
            pltpu.SemaphoreType.DMA((n,))]
    return out_shapes, sems


def exchange(arrs, scatter, name):
    n = len(arrs)
    out_shapes, sems = _comm_shapes(arrs, scatter)

    def body(*refs):
        ins, outs, sem_refs = refs[:n], refs[n:2 * n], refs[2 * n:]
        _exchange_phase(ins, outs, scatter, sem_refs, True)
        _exchange_phase(ins, outs, scatter, sem_refs, False)

    anyspec = pl.BlockSpec(memory_space=pl.ANY)
    return pl.pallas_call(
        body, in_specs=[anyspec] * n, out_specs=[anyspec] * n, out_shape=out_shapes, scratch_shapes=sems,
        compiler_params=pltpu.CompilerParams(has_side_effects=True), name=name)(*arrs)


def _call(body, *, grid, in_specs, out_specs, out_shape, args, name, scratch_shapes=(), comm=None):
    n_grid, scratch_shapes = len(grid), list(scratch_shapes)
    if comm is None:
        outs = pl.pallas_call(body, grid=grid, in_specs=in_specs, out_specs=out_specs, out_shape=out_shape,
                              scratch_shapes=scratch_shapes, compiler_params=_params(n_grid), name=name)(*args)
        return outs, []
    arrs, scatter = comm
    nc, n_in, n_out, n_sc = len(arrs), len(in_specs), len(out_specs), len(scratch_shapes)
    comm_shapes, sems = _comm_shapes(arrs, scatter)

    def wrapped(*refs):
        ins, cins = refs[:n_in], refs[n_in:n_in + nc]
        o0 = n_in + nc
        outs, couts = refs[o0:o0 + n_out], refs[o0 + n_out:o0 + n_out + nc]
        s0 = o0 + n_out + nc
        scratch, sem_refs = refs[s0:s0 + n_sc], refs[s0 + n_sc:]
        first = functools.reduce(jnp.logical_and, [pl.program_id(k) == 0 for k in range(n_grid)])
        last = functools.reduce(jnp.logical_and, [pl.program_id(k) == grid[k] - 1 for k in range(n_grid)])

        @pl.when(first)
        def _():
            _exchange_phase(cins, couts, scatter, sem_refs, True)

        body(*ins, *outs, *scratch)

        @pl.when(last)
        def _():
            _exchange_phase(cins, couts, scatter, sem_refs, False)

    anyspec = pl.BlockSpec(memory_space=pl.ANY)
    res = pl.pallas_call(
        wrapped, grid=grid, in_specs=list(in_specs) + [anyspec] * nc, out_specs=list(out_specs) + [anyspec] * nc,
        out_shape=list(out_shape) + comm_shapes, scratch_shapes=scratch_shapes + sems,
        compiler_params=pltpu.CompilerParams(dimension_semantics=("arbitrary",) * n_grid,
                                             vmem_limit_bytes=VMEM_LIMIT, has_side_effects=True),
        name=name)(*args, *arrs)
    return res[:n_out], res[n_out:]


def rms_fwd(x, g, name):
    T, D = x.shape
    tm = 512

    def body(x_ref, g_ref, h_ref):
        xv = x_ref[...]
        r = lax.rsqrt(jnp.mean(xv * xv, axis=-1, keepdims=True) + NORM_EPS)
        h_ref[...] = (xv * r * g_ref[...]).astype(BF16)

    return pl.pallas_call(
        body, grid=(T // tm,),
        in_specs=[pl.BlockSpec((tm, D), lambda i: (i, 0)), pl.BlockSpec((1, D), lambda i: (0, 0))],
        out_specs=pl.BlockSpec((tm, D), lambda i: (i, 0)),
        out_shape=SDS((T, D), BF16), compiler_params=_params(1), name=name)(x, g)


def ffn_in(h, w, name, comm=None):
    T, D = h.shape
    F = w.shape[2]
    tm = 512

    def body(h_ref, wg_ref, wu_ref, z_ref, a_ref):
        hv = h_ref[...]
        zg = _dot(hv, wg_ref[...])
        zu = _dot(hv, wu_ref[...])
        z_ref[0] = zg.astype(BF16)
        z_ref[1] = zu.astype(BF16)
        a_ref[...] = (zg * _sigmoid(zg) * zu).astype(BF16)

    return _call(
        body, grid=(4, T // tm),
        in_specs=[pl.BlockSpec((tm, D), lambda j, i: (i, 0)),
                  pl.BlockSpec((None, D, F), lambda j, i: (j, 0, 0)),
                  pl.BlockSpec((None, D, F), lambda j, i: (j + 4, 0, 0))],
        out_specs=[pl.BlockSpec((2, None, tm, F), lambda j, i: (0, j, i, 0)),
                   pl.BlockSpec((None, tm, F), lambda j, i: (j, i, 0))],
        out_shape=[SDS((2, 4, T, F), BF16), SDS((4, T, F), BF16)],
        args=(h, w, w), name=name, comm=comm)


def mm_nn_b(a, w, nb, out_dtype, name):
    T, K = a.shape
    N = w.shape[2]
    tm = 512

    def body(a_ref, w_ref, o_ref):
        o_ref[...] = _dot(a_ref[...], w_ref[...]).astype(out_dtype)

    return pl.pallas_call(
        body, grid=(nb, T // tm),
        in_specs=[pl.BlockSpec((tm, K), lambda b, i: (i, 0)),
                  pl.BlockSpec((None, K, N), lambda b, i: (b, 0, 0))],
        out_specs=pl.BlockSpec((None, tm, N), lambda b, i: (b, i, 0)),
        out_shape=SDS((nb, T, N), out_dtype), compiler_params=_params(2), name=name)(a, w)


def mm_acc_norm(a, w, xres, g, scale, name, comm=None):
    nb, T, K = a.shape
    D = w.shape[2]
    tm = 512

    def body(a_ref, w_ref, x_ref, g_ref, o_ref, y_ref):
        b = pl.program_id(1)

        @pl.when(b == 0)
        def _():
            o_ref[...] = jnp.zeros_like(o_ref)

        o_ref[...] += _dot(a_ref[...].astype(BF16), w_ref[...])

        @pl.when(b == nb - 1)
        def _():
            o = o_ref[...]
            r = lax.rsqrt(jnp.mean(o * o, axis=-1, keepdims=True) + NORM_EPS)
            y_ref[...] = x_ref[...] + scale * (o * r * g_ref[...])

    return _call(
        body, grid=(T // tm, nb),
        in_specs=[pl.BlockSpec((None, tm, K), lambda i, b: (b, i, 0)),
                  pl.BlockSpec((None, K, D), lambda i, b: (b, 0, 0)),
                  pl.BlockSpec((tm, D), lambda i, b: (i, 0)),
                  pl.BlockSpec((1, D), lambda i, b: (0, 0))],
        out_specs=[pl.BlockSpec((tm, D), lambda i, b: (i, 0)), pl.BlockSpec((tm, D), lambda i, b: (i, 0))],
        out_shape=[SDS((T, D), F32), SDS((T, D), F32)],
        args=(a, w, xres, g), name=name, comm=comm)


def loss_head(y, tgt, name):
    T, D = y.shape
    tm = 512

    def body(y_ref, t_ref, dy_ref, sq_ref):
        @pl.when(pl.program_id(0) == 0)
        def _():
            sq_ref[...] = jnp.zeros_like(sq_ref)

        e = y_ref[...] - t_ref[...]
        dy_ref[...] = e * (1.0 / D)
        sq_ref[...] += jnp.sum(e * e, axis=0, keepdims=True)

    return pl.pallas_call(
        body, grid=(T // tm,),
        in_specs=[pl.BlockSpec((tm, D), lambda i: (i, 0)), pl.BlockSpec((tm, D), lambda i: (i, 0))],
        out_specs=[pl.BlockSpec((tm, D), lambda i: (i, 0)), pl.BlockSpec((1, D), lambda i: (0, 0))],
        out_shape=[SDS((T, D), F32), SDS((1, D), F32)], compiler_params=_params(1), name=name)(y, tgt)


def post_bwd(dy, o, g, scale, name):
    T, D = o.shape
    tm = 512

    def body(dy_ref, o_ref, g_ref, do_ref, dg_ref):
        @pl.when(pl.program_id(0) == 0)
        def _():
            dg_ref[...] = jnp.zeros_like(dg_ref)

        ov = o_ref[...]
        r = scale * dy_ref[...]
        rstd = lax.rsqrt(jnp.mean(ov * ov, axis=-1, keepdims=True) + NORM_EPS)
        oh = ov * rstd
        dg_ref[...] += jnp.sum(r * oh, axis=0, keepdims=True)
        rg = r * g_ref[...]
        do_ref[...] = (rstd * (rg - oh * jnp.mean(rg * oh, axis=-1, keepdims=True))).astype(BF16)

    return pl.pallas_call(
        body, grid=(T // tm,),
        in_specs=[pl.BlockSpec((tm, D), lambda i: (i, 0)), pl.BlockSpec((tm, D), lambda i: (i, 0)),
                  pl.BlockSpec((1, D), lambda i: (0, 0))],
        out_specs=[pl.BlockSpec((tm, D), lambda i: (i, 0)), pl.BlockSpec((1, D), lambda i: (0, 0))],
        out_shape=[SDS((T, D), BF16), SDS((1, D), F32)], compiler_params=_params(1), name=name)(dy, o, g)


def mm_nt_b(gr, w, name):
    T, N = gr.shape
    nb, K, _ = w.shape
    tm = 512

    def body(g_ref, w_ref, o_ref):
        o_ref[...] = _dot_nt(g_ref[...], w_ref[...])

    return pl.pallas_call(
        body, grid=(nb, T // tm),
        in_specs=[pl.BlockSpec((tm, N), lambda b, i: (i, 0)), pl.BlockSpec((None, K, N), lambda b, i: (b, 0, 0))],
        out_specs=pl.BlockSpec((None, tm, K), lambda b, i: (b, i, 0)),
        out_shape=SDS((nb, T, K), F32), compiler_params=_params(2), name=name)(gr, w)


def ffn_dact(do, w_out, z, name):
    T, D = do.shape
    nb, F, _ = w_out.shape
    tm = 512

    def body(g_ref, w_ref, z_ref, dz_ref):
        da = _dot_nt(g_ref[...], w_ref[...])
        zg = z_ref[0].astype(F32)
        zu = z_ref[1].astype(F32)
        sg = _sigmoid(zg)
        dz_ref[0] = (da * zu * (sg * (1.0 + zg * (1.0 - sg)))).astype(BF16)
        dz_ref[1] = (da * zg * sg).astype(BF16)

    return pl.pallas_call(
        body, grid=(nb, T // tm),
        in_specs=[pl.BlockSpec((tm, D), lambda b, i: (i, 0)), pl.BlockSpec((None, F, D), lambda b, i: (b, 0, 0)),
                  pl.BlockSpec((2, None, tm, F), lambda b, i: (0, b, i, 0))],
        out_specs=pl.BlockSpec((2, None, tm, F), lambda b, i: (0, b, i, 0)),
        out_shape=SDS((2, nb, T, F), BF16), compiler_params=_params(2), name=name)(do, w_out, z)


def mm_tn(a, g, a_batched, g_batched, nb, name):
    T = a.shape[-2]
    K, N = a.shape[-1], g.shape[-1]
    tk = 512
    nk = T // tk

    def body(a_ref, g_ref, o_ref, acc):
        k = pl.program_id(1)

        @pl.when(k == 0)
        def _():
            acc[...] = jnp.zeros_like(acc)

        acc[...] += _dot_tn(a_ref[...].astype(BF16), g_ref[...].astype(BF16))

        @pl.when(k == nk - 1)
        def _():
            o_ref[...] = acc[...].astype(BF16)

    a_spec = (pl.BlockSpec((None, tk, K), lambda b, k: (b, k, 0)) if a_batched
              else pl.BlockSpec((tk, K), lambda b, k: (k, 0)))
    g_spec = (pl.BlockSpec((None, tk, N), lambda b, k: (b, k, 0)) if g_batched
              else pl.BlockSpec((tk, N), lambda b, k: (k, 0)))
    return pl.pallas_call(
        body, grid=(nb, nk), in_specs=[a_spec, g_spec],
        out_specs=pl.BlockSpec((None, K, N), lambda b, k: (b, 0, 0)),
        out_shape=SDS((nb, K, N), BF16), scratch_shapes=[pltpu.VMEM((K, N), F32)],
        compiler_params=_params(2), name=name)(a, g)


def dh_pre_bwd(dz, w, x, g, dyres, name, comm=None):
    nb, T, F = dz.shape
    D = w.shape[1]
    tm = 512

    def body(dz_ref, w_ref, x_ref, g_ref, dy_ref, dx_ref, dg_ref, acc):
        i, b = pl.program_id(0), pl.program_id(1)

        @pl.when(jnp.logical_and(i == 0, b == 0))
        def _():
            dg_ref[...] = jnp.zeros_like(dg_ref)

        @pl.when(b == 0)
        def _():
            acc[...] = jnp.zeros_like(acc)

        acc[...] += _dot_nt(dz_ref[...].astype(BF16), w_ref[...])

        @pl.when(b == nb - 1)
        def _():
            dh = acc[...]
            xv = x_ref[...]
            rstd = lax.rsqrt(jnp.mean(xv * xv, axis=-1, keepdims=True) + NORM_EPS)
            xh = xv * rstd
            dg_ref[...] += jnp.sum(dh * xh, axis=0, keepdims=True)
            dhg = dh * g_ref[...]
            dx_ref[...] = dy_ref[...] + rstd * (dhg - xh * jnp.mean(dhg * xh, axis=-1, keepdims=True))

    return _call(
        body, grid=(T // tm, nb),
        in_specs=[pl.BlockSpec((None, tm, F), lambda i, b: (b, i, 0)),
                  pl.BlockSpec((None, D, F), lambda i, b: (b, 0, 0)),
                  pl.BlockSpec((tm, D), lambda i, b: (i, 0)),
                  pl.BlockSpec((1, D), lambda i, b: (0, 0)),
                  pl.BlockSpec((tm, D), lambda i, b: (i, 0))],
        out_specs=[pl.BlockSpec((tm, D), lambda i, b: (i, 0)), pl.BlockSpec((1, D), lambda i, b: (0, 0))],
        out_shape=[SDS((T, D), F32), SDS((1, D), F32)],
        scratch_shapes=[pltpu.VMEM((tm, D), F32)], args=(dz, w, x, g, dyres), name=name, comm=comm)


def _attn_masks():
    qi = lax.broadcasted_iota(jnp.int32, (QBLK, QBLK), 0)
    kj = lax.broadcasted_iota(jnp.int32, (QBLK, QBLK), 1)
    cur_ok = kj <= qi
    prev_ok = kj >= qi
    dcur = (qi - kj).astype(F32)
    return cur_ok, prev_ok, dcur, dcur + float(QBLK)


def _head_slopes(p, d):
    out = []
    for hq in range(2):
        v = [float(d) * 2.0 ** -(2 * q + hq + 1) for q in range(4)]
        out.append(jnp.where(p == 0, v[0], jnp.where(p == 1, v[1], jnp.where(p == 2, v[2], v[3]))))
    return out


def _rows(r, d):
    return pl.ds(r, QBLK, stride=d) if d > 1 else pl.ds(0, QBLK)


def _pair_spec(rb, part, off, nblk):
    return pl.BlockSpec((None, rb, PAIR_W),
                        lambda p, n: (2 * part + p // 2, jnp.clip(n + off, 0, nblk - 1), p % 2))


def attn_fwd(proj, d, name):
    T = proj.shape[1]
    rb = QBLK * d
    nblk = T // rb

    def body(q_ref, kc_ref, kp_ref, vc_ref, vp_ref, o_ref, l_ref):
        p, n = pl.program_id(0), pl.program_id(1)
        cur_ok, prev_ok, dcur, dprev = _attn_masks()
        prev_ok = jnp.logical_and(prev_ok, n > 0)
        lane_head = lax.broadcasted_iota(jnp.int32, (QBLK, PAIR_W), 1) // HEAD_DIM
        slopes = _head_slopes(p, d)

        def per_residue(r, carry):
            rows = _rows(r, d)
            q = q_ref[rows, :]
            kc, kp = kc_ref[rows, :].astype(BF16), kp_ref[rows, :].astype(BF16)
            vc, vp = vc_ref[rows, :].astype(BF16), vp_ref[rows, :].astype(BF16)
            o_acc = jnp.zeros((QBLK, PAIR_W), F32)
            l_acc = jnp.zeros((QBLK, PAIR_W), F32)
            for hq in range(2):
                hm = lane_head == hq
                qm = jnp.where(hm, q, 0.0).astype(BF16)
                sc = jnp.where(cur_ok, _dot_nt(qm, kc) * 0.125 - slopes[hq] * dcur, NEG)
                sp = jnp.where(prev_ok, _dot_nt(qm, kp) * 0.125 - slopes[hq] * dprev, NEG)
                m = jnp.maximum(jnp.max(sc, axis=1, keepdims=True), jnp.max(sp, axis=1, keepdims=True))
                pc = jnp.exp(sc - m)
                pp = jnp.exp(sp - m)
                den = jnp.sum(pc, axis=1, keepdims=True) + jnp.sum(pp, axis=1, keepdims=True)
                pv = _dot(pc.astype(BF16), vc) + _dot(pp.astype(BF16), vp)
                o_acc = jnp.where(hm, pv * (1.0 / den), o_acc)
                l_acc = jnp.where(hm, m + jnp.log(den), l_acc)
            o_ref[rows, :] = o_acc
            l_ref[rows, :] = l_acc
            return carry

        if d == 1:
            per_residue(0, 0)
        else:
            lax.fori_loop(0, d, per_residue, 0)

    out_spec = _pair_spec(rb, 0, 0, nblk)
    return pl.pallas_call(
        body, grid=(4, nblk),
        in_specs=[_pair_spec(rb, 0, 0, nblk), _pair_spec(rb, 1, 0, nblk), _pair_spec(rb, 1, -1, nblk),
                  _pair_spec(rb, 2, 0, nblk), _pair_spec(rb, 2, -1, nblk)],
        out_specs=[out_spec, out_spec],
        out_shape=[SDS((2, T, 2 * PAIR_W), F32), SDS((2, T, 2 * PAIR_W), F32)],
        compiler_params=_params(2), name=name)(proj, proj, proj, proj, proj)


def attn_merge(os_, ls_, name):
    _, T, HW = os_[0].shape
    tm = 512

    def body(o1, o2, o3, l1, l2, l3, o_ref, l_ref):
        a, b, c = l1[...], l2[...], l3[...]
        m = jnp.maximum(jnp.maximum(a, b), c)
        ea, eb, ec = jnp.exp(a - m), jnp.exp(b - m), jnp.exp(c - m)
        s = ea + eb + ec
        o_ref[...] = (ea * o1[...] + eb * o2[...] + ec * o3[...]) * (1.0 / s)
        l_ref[...] = m + jnp.log(s)

    spec = pl.BlockSpec((None, tm, HW), lambda h, i: (h, i, 0))
    return pl.pallas_call(
        body, grid=(2, T // tm), in_specs=[spec] * 6, out_specs=[spec, spec],
        out_shape=[SDS((2, T, HW), F32), SDS((2, T, HW), F32)],
        compiler_params=_params(2), name=name)(*os_, *ls_)


def attn_bwd(proj, dcat, o, lse, acc, d, name):
    T = proj.shape[1]
    rb = QBLK * d
    nblk = T // rb
    has_acc = acc is not None

    def body(*refs):
        (qc_ref, qn_ref, kc_ref, kp_ref, vc_ref, vp_ref, dc_ref, dn_ref, oc_ref, on_ref, lc_ref, ln_ref) = refs[:12]
        acc_ref = refs[12] if has_acc else None
        out_ref = refs[-1]
        p, n = pl.program_id(0), pl.program_id(1)
        cur_ok, prev_ok, dcur, dprev = _attn_masks()
        pok_c = jnp.logical_and(prev_ok, n > 0)
        pok_n = jnp.logical_and(prev_ok, n < nblk - 1)
        lane_head = lax.broadcasted_iota(jnp.int32, (QBLK, PAIR_W), 1) // HEAD_DIM
        slopes = _head_slopes(p, d)

        def per_residue(r, carry):
            rows = _rows(r, d)
            q_c, q_n = qc_ref[rows, :], qn_ref[rows, :]
            k_c, k_p = kc_ref[rows, :].astype(BF16), kp_ref[rows, :].astype(BF16)
            v_c, v_p = vc_ref[rows, :].astype(BF16), vp_ref[rows, :].astype(BF16)
            do_c, do_n = dc_ref[rows, :], dn_ref[rows, :]
            o_c, o_n = oc_ref[rows, :], on_ref[rows, :]
            l_c, l_n = lc_ref[rows, :], ln_ref[rows, :]
            dq = jnp.zeros((QBLK, PAIR_W), F32)
            dk = jnp.zeros((QBLK, PAIR_W), F32)
            dv = jnp.zeros((QBLK, PAIR_W), F32)
            for hq in range(2):
                slope = slopes[hq]
                hm = lane_head == hq
                qm_c = jnp.where(hm, q_c, 0.0).astype(BF16)
                qm_n = jnp.where(hm, q_n, 0.0).astype(BF16)
                dom_c = jnp.where(hm, do_c, 0.0)
                dom_n = jnp.where(hm, do_n, 0.0)
                dd_c = jnp.sum(dom_c * o_c, axis=1, keepdims=True)
                dd_n = jnp.sum(dom_n * o_n, axis=1, keepdims=True)
                ls_c = jnp.max(jnp.where(hm, l_c, NEG), axis=1, keepdims=True)
                ls_n = jnp.max(jnp.where(hm, l_n, NEG), axis=1, keepdims=True)
                dob_c, dob_n = dom_c.astype(BF16), dom_n.astype(BF16)
                s = jnp.where(cur_ok, _dot_nt(qm_c, k_c) * 0.125 - slope * dcur, NEG)
                pr = jnp.exp(s - ls_c)
                ds = (pr * (_dot_nt(dob_c, v_c) - dd_c)).astype(BF16)
                dq_h = _dot(ds, k_c)
                dk_h = _dot_tn(ds, qm_c)
                dv_h = _dot_tn(pr.astype(BF16), dob_c)
                s = jnp.where(pok_c, _dot_nt(qm_c, k_p) * 0.125 - slope * dprev, NEG)
                pr = jnp.exp(s - ls_c)
                ds = (pr * (_dot_nt(dob_c, v_p) - dd_c)).astype(BF16)
                dq_h += _dot(ds, k_p)
                s = jnp.where(pok_n, _dot_nt(qm_n, k_c) * 0.125 - slope * dprev, NEG)
                pr = jnp.exp(s - ls_n)
                ds = (pr * (_dot_nt(dob_n, v_c) - dd_n)).astype(BF16)
                dk_h += _dot_tn(ds, qm_n)
                dv_h += _dot_tn(pr.astype(BF16), dob_n)
                dq = jnp.where(hm, dq_h * 0.125, dq)
                dk += dk_h * 0.125
                dv += dv_h
            for part, val in enumerate((dq, dk, dv)):
                if has_acc:
                    val = val + acc_ref.at[part][rows, :]
                out_ref.at[part][rows, :] = val
            return carry

        if d == 1:
            per_residue(0, 0)
        else:
            lax.fori_loop(0, d, per_residue, 0)

    cur = lambda part: _pair_spec(rb, part, 0, nblk)
    prv = lambda part: _pair_spec(rb, part, -1, nblk)
    nxt = lambda part: _pair_spec(rb, part, 1, nblk)
    full = pl.BlockSpec((3, None, rb, PAIR_W), lambda p, n: (0, p // 2, n, p % 2))
    in_specs = [cur(0), nxt(0), cur(1), prv(1), cur(2), prv(2), cur(0), nxt(0), cur(0), nxt(0), cur(0), nxt(0)]
    args = [proj, proj, proj, proj, proj, proj, dcat, dcat, o, o, lse, lse]
    if has_acc:
        in_specs.append(full)
        args.append(acc)
    return pl.pallas_call(
        body, grid=(4, nblk), in_specs=in_specs, out_specs=full,
        out_shape=SDS((3, 2, T, 2 * PAIR_W), F32), compiler_params=_params(2), name=name)(*args)


def _scan_chunk(buf, apow_ref, c0, reverse, ptab=None, carry=None):
    tm = buf.shape[0]
    cre = pl.ds(c0, SCAN_CW)
    cim = pl.ds(pl.multiple_of(c0 + HALF_STATES, 128), SCAN_CW)
    re, im = buf[:, cre], buf[:, cim]
    row = lax.broadcasted_iota(jnp.int32, (tm, SCAN_CW), 0)
    sgn = -1.0 if reverse else 1.0
    k, s = 1, 0
    while k < tm:
        ar = apow_ref[pl.ds(s, 1), cre]
        ai = sgn * apow_ref[pl.ds(s, 1), cim]
        if reverse:
            ok, shift = row < tm - k, tm - k
        else:
            ok, shift = row >= k, k
        sre = jnp.where(ok, pltpu.roll(re, shift, 0), 0.0)
        sim = jnp.where(ok, pltpu.roll(im, shift, 0), 0.0)
        re, im = re + ar * sre - ai * sim, im + ar * sim + ai * sre
        k, s = 2 * k, s + 1
    if ptab is not None:
        pr, pi = ptab[:, cre], ptab[:, cim]
        cr, ci = carry[:, cre], carry[:, cim]
        re, im = re + pr * cr - pi * ci, im + pr * ci + pi * cr
    buf[:, cre] = re
    buf[:, cim] = im


def _scan_tile(buf, apow_ref, reverse, ptab=None, carry=None):
    per_half = HALF_STATES // SCAN_CW

    def chunk(j, _):
        c0 = pl.multiple_of((j // per_half) * 2 * HALF_STATES + (j % per_half) * SCAN_CW, 128)
        _scan_chunk(buf, apow_ref, c0, reverse, ptab, carry)
        return 0

    lax.fori_loop(0, 2 * per_half, chunk, 0)


def _init_ptab(ptab, apow_ref, reverse):
    tm, width = ptab.shape
    row = lax.broadcasted_iota(jnp.int32, (tm, width), 0)
    col = lax.broadcasted_iota(jnp.int32, (1, width), 1)
    a = apow_ref[pl.ds(0, 1), :]
    if reverse:
        is_im = (col // HALF_STATES) % 2 == 1
        a = jnp.where(is_im, -a, a)
    seed = (tm - 1) if reverse else 0
    ptab[...] = jnp.where(row == seed, a, 0.0)
    _scan_tile(ptab, apow_ref, reverse)


def ssm_fwd(proj, bh, ch, apow, dskip, name, comm=None):
    _, T, C = proj.shape
    tm = SCAN_TM
    SW = 4 * HALF_STATES

    def body(u_ref, bh_ref, ch_ref, apow_ref, dsk_ref, y_ref, s_ref, ptab, carry):
        @pl.when(pl.program_id(0) == 0)
        def _():
            _init_ptab(ptab, apow_ref, False)
            carry[...] = jnp.zeros_like(carry)

        for h in range(2):
            s_ref[:, pl.ds(h * 2 * HALF_STATES, 2 * HALF_STATES)] = _dot(u_ref[h].astype(BF16), bh_ref[h])
        _scan_tile(s_ref, apow_ref, False, ptab, carry)
        carry[...] = s_ref[pl.ds(tm - 1, 1), :]
        for h in range(2):
            sv = s_ref[:, pl.ds(h * 2 * HALF_STATES, 2 * HALF_STATES)].astype(BF16)
            y_ref[h] = _dot(sv, ch_ref[h]) + dsk_ref[h] * u_ref[h]

    return _call(
        body, grid=(T // tm,),
        in_specs=[pl.BlockSpec((2, tm, C), lambda i: (3, i, 0)),
                  pl.BlockSpec((2, C, 2 * HALF_STATES), lambda i: (0, 0, 0)),
                  pl.BlockSpec((2, 2 * HALF_STATES, C), lambda i: (0, 0, 0)),
                  pl.BlockSpec((8, SW), lambda i: (0, 0)),
                  pl.BlockSpec((2, 1, C), lambda i: (0, 0, 0))],
        out_specs=[pl.BlockSpec((2, tm, C), lambda i: (0, i, 0)), pl.BlockSpec((tm, SW), lambda i: (i, 0))],
        out_shape=[SDS((2, T, C), F32), SDS((T, SW), F32)],
        scratch_shapes=[pltpu.VMEM((tm, SW), F32), pltpu.VMEM((1, SW), F32)],
        args=(proj, bh, ch, apow, dskip), name=name, comm=comm)


def ssm_bwd(dy, proj, st, bh, ch, apow, dskip, name):
    _, T, C = proj.shape
    tm = SCAN_TM
    nt = T // tm
    SW = 4 * HALF_STATES
    HS2 = 2 * HALF_STATES

    def body(dy_ref, u_ref, s_ref, sp_ref, bh_ref, ch_ref, apow_ref, dsk_ref,
             du_ref, da_ref, dbh_ref, dch_ref, dd_ref, lam, ptab, carry):
        i = pl.program_id(0)

        @pl.when(i == 0)
        def _():
            _init_ptab(ptab, apow_ref, True)
            carry[...] = jnp.zeros_like(carry)
            da_ref[...] = jnp.zeros_like(da_ref)
            dbh_ref[...] = jnp.zeros_like(dbh_ref)
            dch_ref[...] = jnp.zeros_like(dch_ref)
            dd_ref[...] = jnp.zeros_like(dd_ref)

        for h in range(2):
            lam[:, pl.ds(h * HS2, HS2)] = _dot_nt(dy_ref[h].astype(BF16), ch_ref[h])
        _scan_tile(lam, apow_ref, True, ptab, carry)
        carry[...] = lam[pl.ds(0, 1), :]

        first = i == nt - 1
        per_half = HALF_STATES // SCAN_CW

        def chunk(j, _):
            c0 = pl.multiple_of((j // per_half) * HS2 + (j % per_half) * SCAN_CW, 128)
            cre, cim = pl.ds(c0, SCAN_CW), pl.ds(pl.multiple_of(c0 + HALF_STATES, 128), SCAN_CW)
            row = lax.broadcasted_iota(jnp.int32, (tm, SCAN_CW), 0)
            pre = jnp.where(first, 0.0, sp_ref[pl.ds(7, 1), cre])
            pim = jnp.where(first, 0.0, sp_ref[pl.ds(7, 1), cim])
            spr = jnp.where(row == 0, pre, pltpu.roll(s_ref[:, cre], 1, 0))
            spi = jnp.where(row == 0, pim, pltpu.roll(s_ref[:, cim], 1, 0))
            lr, li = lam[:, cre], lam[:, cim]
            da_ref[:, cre] += jnp.sum(lr * spr + li * spi, axis=0, keepdims=True)
            da_ref[:, cim] += jnp.sum(li * spr - lr * spi, axis=0, keepdims=True)
            return 0

        lax.fori_loop(0, 2 * per_half, chunk, 0)

        for h in range(2):
            lb = lam[:, pl.ds(h * HS2, HS2)].astype(BF16)
            dyv, uv = dy_ref[h], u_ref[h]
            du_ref[h] = _dot_nt(lb, bh_ref[h]) + dsk_ref[h] * dyv
            dbh_ref[h] += _dot_tn(uv.astype(BF16), lb)
            dch_ref[h] += _dot_tn(s_ref[:, pl.ds(h * HS2, HS2)].astype(BF16), dyv.astype(BF16))
            dd_ref[h] += jnp.sum(dyv * uv, axis=0, keepdims=True)

    rev = lambda i: nt - 1 - i
    return pl.pallas_call(
        body, grid=(nt,),
        in_specs=[pl.BlockSpec((2, tm, C), lambda i: (0, rev(i), 0)),
                  pl.BlockSpec((2, tm, C), lambda i: (3, rev(i), 0)),
                  pl.BlockSpec((tm, SW), lambda i: (rev(i), 0)),
                  pl.BlockSpec((8, SW), lambda i: (jnp.maximum(rev(i) * (tm // 8) - 1, 0), 0)),
                  pl.BlockSpec((2, C, HS2), lambda i: (0, 0, 0)),
                  pl.BlockSpec((2, HS2, C), lambda i: (0, 0, 0)),
                  pl.BlockSpec((8, SW), lambda i: (0, 0)),
                  pl.BlockSpec((2, 1, C), lambda i: (0, 0, 0))],
        out_specs=[pl.BlockSpec((2, tm, C), lambda i: (0, rev(i), 0)),
                   pl.BlockSpec((1, SW), lambda i: (0, 0)),
                   pl.BlockSpec((2, C, HS2), lambda i: (0, 0, 0)),
                   pl.BlockSpec((2, HS2, C), lambda i: (0, 0, 0)),
                   pl.BlockSpec((2, 1, C), lambda i: (0, 0, 0))],
        out_shape=[SDS((2, T, C), F32), SDS((1, SW), F32), SDS((2, C, HS2), F32), SDS((2, HS2, C), F32),
                   SDS((2, 1, C), F32)],
        scratch_shapes=[pltpu.VMEM((tm, SW), F32), pltpu.VMEM((tm, SW), F32), pltpu.VMEM((1, SW), F32)],
        compiler_params=_params(1), name=name)(dy, proj, st, st, bh, ch, apow, dskip)


_GELU_C = math.sqrt(2.0 / math.pi)


def _gelu(x):
    t = jnp.tanh(_GELU_C * (x + 0.044715 * x * x * x))
    return 0.5 * x * (1.0 + t), t


def glu_fwd(y, w, b, name):
    _, T, C = y.shape
    tm = 512

    def body(y_ref, w_ref, b_ref, o_ref, lg_ref):
        y0, _ = _gelu(y_ref[0])
        y1, _ = _gelu(y_ref[1])
        lg = _dot(y0.astype(BF16), w_ref[0]) + _dot(y1.astype(BF16), w_ref[1]) + b_ref[...]
        sg = _sigmoid(lg)
        o_ref[0] = y0 * sg[:, :C]
        o_ref[1] = y1 * sg[:, C:]
        lg_ref[0] = lg[:, :C]
        lg_ref[1] = lg[:, C:]

    return pl.pallas_call(
        body, grid=(T // tm,),
        in_specs=[pl.BlockSpec((2, tm, C), lambda i: (0, i, 0)), pl.BlockSpec((2, C, 2 * C), lambda i: (0, 0, 0)),
                  pl.BlockSpec((1, 2 * C), lambda i: (0, 0))],
        out_specs=[pl.BlockSpec((2, tm, C), lambda i: (0, i, 0)), pl.BlockSpec((2, tm, C), lambda i: (0, i, 0))],
        out_shape=[SDS((2, T, C), F32), SDS((2, T, C), F32)], compiler_params=_params(1), name=name)(y, w, b)


def glu_bwd(dcat, y, lg, w, name):
    _, T, C = y.shape
    tm = 512

    def body(d_ref, y_ref, lg_ref, w_ref, dy_ref, dw_ref, db_ref):
        @pl.when(pl.program_id(0) == 0)
        def _():
            dw_ref[...] = jnp.zeros_like(dw_ref)
            db_ref[...] = jnp.zeros_like(db_ref)

        y2, th, sg, dlg = [], [], [], []
        for h in range(2):
            yy, tt = _gelu(y_ref[h])
            ss = _sigmoid(lg_ref[h])
            y2.append(yy)
            th.append(tt)
            sg.append(ss)
            dlg.append(d_ref[h] * yy * ss * (1.0 - ss))
        dl = jnp.concatenate(dlg, axis=1)
        dlb = dl.astype(BF16)
        db_ref[...] += jnp.sum(dl, axis=0, keepdims=True)
        for h in range(2):
            dy2 = d_ref[h] * sg[h] + _dot_nt(dlb, w_ref[h])
            yv = y_ref[h]
            dgelu = 0.5 * (1.0 + th[h]) + 0.5 * yv * (1.0 - th[h] * th[h]) * _GELU_C * (1.0 + 3 * 0.044715 * yv * yv)
            dy_ref[h] = dy2 * dgelu
            dw_ref[h] += _dot_tn(y2[h].astype(BF16), dlb)

    return pl.pallas_call(
        body, grid=(T // tm,),
        in_specs=[pl.BlockSpec((2, tm, C), lambda i: (1, i, 0)), pl.BlockSpec((2, tm, C), lambda i: (0, i, 0)),
                  pl.BlockSpec((2, tm, C), lambda i: (0, i, 0)), pl.BlockSpec((2, C, 2 * C), lambda i: (0, 0, 0))],
        out_specs=[pl.BlockSpec((2, tm, C), lambda i: (0, i, 0)), pl.BlockSpec((2, C, 2 * C), lambda i: (0, 0, 0)),
                   pl.BlockSpec((1, 2 * C), lambda i: (0, 0))],
        out_shape=[SDS((2, T, C), F32), SDS((2, C, 2 * C), F32), SDS((1, 2 * C), F32)],
        compiler_params=_params(1), name=name)(dcat, y, lg, w)


def adamw(w, m, v, slots, name):
    R, C = w.shape
    tr = R
    for cand in (512, 256, 128, 64, 32, 16, 8):
        if R % cand == 0 and cand * C * 4 <= 2 * 1024 * 1024:
            tr = cand
            break
    c1 = 1.0 / (1.0 - ADAM_B1 ** ADAM_STEP)
    c2 = 1.0 / (1.0 - ADAM_B2 ** ADAM_STEP)

    def body(w_ref, m_ref, v_ref, s_ref, g_ref, d_ref, nm_ref, nv_ref):
        g = s_ref[0].astype(F32)
        for j in range(1, N_DEV):
            g = g + s_ref[j].astype(F32)
        nm = ADAM_B1 * m_ref[...] + (1.0 - ADAM_B1) * g
        nv = ADAM_B2 * v_ref[...] + (1.0 - ADAM_B2) * (g * g)
        g_ref[...] = g
        nm_ref[...] = nm
        nv_ref[...] = nv
        d_ref[...] = -ADAM_LR * ((nm * c1) / (jnp.sqrt(nv * c2) + ADAM_EPS) + ADAM_WD * w_ref[...])

    spec = pl.BlockSpec((tr, C), lambda i: (i, 0))
    return pl.pallas_call(
        body, grid=(R // tr,),
        in_specs=[spec, spec, spec, pl.BlockSpec((N_DEV, tr, C), lambda i: (0, i, 0))],
        out_specs=[spec] * 4, out_shape=[SDS((R, C), F32)] * 4, compiler_params=_params(1), name=name)(w, m, v, slots)


def _discretise(a_re, a_im, log_dt, b_re, b_im):
    dt = jnp.exp(log_dt)[:, None]
    e = jnp.exp(dt * a_re)
    ar, ai = e * jnp.cos(dt * a_im), e * jnp.sin(dt * a_im)
    den = a_re * a_re + a_im * a_im
    nr, ni = ar - 1.0, ai
    wr = (nr * a_re + ni * a_im) / den
    wi = (ni * a_re - nr * a_im) / den
    bbr = wr[..., None] * b_re - wi[..., None] * b_im
    bbi = wr[..., None] * b_im + wi[..., None] * b_re
    return ar, ai, bbr, bbi


def _block_diag(t):
    eye = jnp.eye(16, dtype=t.dtype)
    r, c = t.shape[1], t.shape[2]
    return jnp.einsum("hgrc,gk->hgrkc", t.reshape(2, 16, r, c), eye).reshape(2, 16 * r, 16 * c)


def _diag_blocks(m, r, c):
    eye = jnp.eye(16, dtype=m.dtype)
    return jnp.einsum("hgrkc,gk->hgrc", m.reshape(2, 16, r, 16, c), eye).reshape(32, r, c)


def _state_vec(re, im):
    return jnp.stack([re.reshape(2, HALF_STATES), im.reshape(2, HALF_STATES)], axis=1).reshape(-1)


BIG = ("ffn1_w_in", "ffn1_w_out", "w_mix_in", "w_glu", "w_mix_out", "ffn2_w_in", "ffn2_w_out")
WEIGHTS = ("ffn1_pre_g", "ffn1_w_in", "ffn1_w_out", "ffn1_post_g", "mix_pre_g", "w_mix_in", "a_re", "a_im", "log_dt",
           "b_re", "b_im", "c_re", "c_im", "d_skip", "w_glu", "b_glu", "w_mix_out", "mix_post_g", "ffn2_pre_g",
           "ffn2_w_in", "ffn2_w_out", "ffn2_post_g")
SMALL = tuple(n for n in WEIGHTS if n not in BIG)
PACK_COLS = 1024


def _pack(parts):
    flat = jnp.concatenate([p.reshape(-1) for p in parts])
    rows = -(-flat.shape[0] // (8 * PACK_COLS)) * 8
    return jnp.pad(flat, (0, rows * PACK_COLS - flat.shape[0])).reshape(rows, PACK_COLS)


def _unpack(packed, shapes):
    flat, out, off = packed.reshape(-1), [], 0
    for s in shapes:
        n = math.prod(s)
        out.append(flat[off:off + n].reshape(s))
        off += n
    return out


def _gather(names, wb):
    return [wb[n] for n in names], [False] * len(names)


def _ffn_bwd(dy, saved, x, pre_g, w_in, w_out4, post_g, tag):
    h, z, a, o = saved
    T = x.shape[0]
    do, dg_post = post_bwd(dy, o, post_g, 0.5, f"{tag}_post_bwd")
    dz = ffn_dact(do, w_out4, z, f"{tag}_dact")
    dz8 = dz.reshape(8, T, dz.shape[-1])
    dw_out = mm_tn(a, do, True, False, 4, f"{tag}_dwout").reshape(8, -1, D_MODEL)
    dw_in = mm_tn(h, dz8, False, True, 8, f"{tag}_dwin")
    (dx, dg_pre), slots = dh_pre_bwd(dz8, w_in, x, pre_g, dy, f"{tag}_dh", comm=([dw_in, dw_out], [True, True]))
    return dx, dg_pre, dg_post, slots


def local_step(x, tgt, sp, wb):
    T = x.shape[0]
    ar, ai, bbr, bbi = _discretise(sp["a_re"], sp["a_im"], sp["log_dt"], sp["b_re"], sp["b_im"])
    pr, pi, rows = ar, ai, []
    for _ in range(8):
        rows.append(_state_vec(pr, pi))
        pr, pi = pr * pr - pi * pi, 2.0 * pr * pi
    apow = jnp.stack(rows)
    bh = jnp.concatenate([_block_diag(bbr.transpose(0, 2, 1)), _block_diag(bbi.transpose(0, 2, 1))], axis=2)
    ch = jnp.concatenate([_block_diag(sp["c_re"].transpose(0, 2, 1)), _block_diag(-sp["c_im"].transpose(0, 2, 1))], axis=1)
    bh, ch = bh.astype(BF16), ch.astype(BF16)
    dskip = sp["d_skip"].reshape(2, 1, 256)

    (w1_in,) = exchange(*_gather(["ffn1_w_in"], wb), "gather_w1in")
    h1 = rms_fwd(x, sp["ffn1_pre_g"], "ffn1_rms")
    (z1, a1), (w1_out, w_mi) = ffn_in(h1, w1_in, "ffn1_in", comm=_gather(["ffn1_w_out", "w_mix_in"], wb))
    w1_out4 = w1_out.reshape(4, -1, D_MODEL)
    (o1, x1), (w_glu, w_mo, w2_out) = mm_acc_norm(
        a1, w1_out4, x, sp["ffn1_post_g"], 0.5, "ffn1_out", comm=_gather(["w_glu", "w_mix_out", "ffn2_w_out"], wb))
    w_glu2, w_mo4, w2_out4 = w_glu.reshape(2, 256, 512), w_mo.reshape(4, 256, D_MODEL), w2_out.reshape(4, -1, D_MODEL)
    h2 = rms_fwd(x1, sp["mix_pre_g"], "mix_rms")
    proj = mm_nn_b(h2, w_mi, 8, F32, "mix_proj")
    (y_ssm, states), (w2_in,) = ssm_fwd(proj, bh, ch, apow, dskip, "ssm_fwd", comm=_gather(["ffn2_w_in"], wb))
    os_, ls_ = [], []
    for d in DILATIONS:
        o_d, l_d = attn_fwd(proj, d, f"attn_fwd_d{d}")
        os_.append(o_d)
        ls_.append(l_d)
    o_att, lse = attn_merge(os_, ls_, "attn_merge")
    o_ssm, lg = glu_fwd(y_ssm, w_glu2, sp["b_glu"], "glu_fwd")
    cat = jnp.concatenate([o_att, o_ssm], axis=0)
    (mixed, x2), _ = mm_acc_norm(cat, w_mo4, x1, sp["mix_post_g"], 1.0, "mix_out")
    h3 = rms_fwd(x2, sp["ffn2_pre_g"], "ffn2_rms")
    (z3, a3), _ = ffn_in(h3, w2_in, "ffn2_in")
    (o3, x3), _ = mm_acc_norm(a3, w2_out4, x2, sp["ffn2_post_g"], 0.5, "ffn2_out")
    dy3, sq = loss_head(x3, tgt, "loss_head")

    dx2, dg_f2pre, dg_f2post, (s_w2in, s_w2out) = _ffn_bwd(
        dy3, (h3, z3, a3, o3), x2, sp["ffn2_pre_g"], w2_in, w2_out4, sp["ffn2_post_g"], "ffn2")
    dmixed, dg_mpost = post_bwd(dx2, mixed, sp["mix_post_g"], 1.0, "mix_post_bwd")
    dcat = mm_nt_b(dmixed, w_mo4, "mix_dcat")
    dw_mo = mm_tn(cat, dmixed, True, False, 4, "mix_dwout").reshape(8, 128, D_MODEL)
    dy_ssm, dw_glu, db_glu = glu_bwd(dcat, y_ssm, lg, w_glu2, "glu_bwd")
    du, da, dbh, dch, dd = ssm_bwd(dy_ssm, proj, states, bh, ch, apow, dskip, "ssm_bwd")
    dqkv = None
    for d in DILATIONS:
        dqkv = attn_bwd(proj, dcat, o_att, lse, dqkv, d, f"attn_bwd_d{d}")
    dproj = jnp.concatenate([dqkv.reshape(6, T, 256), du], axis=0)
    dw_mi = mm_tn(h2, dproj, False, True, 8, "mix_dwin")
    (dx1, dg_mpre), (s_wmi, s_wglu, s_wmo) = dh_pre_bwd(
        dproj, w_mi, x1, sp["mix_pre_g"], dx2, "mix_dh",
        comm=([dw_mi, dw_glu.astype(BF16).reshape(8, 64, 512), dw_mo], [True, True, True]))
    dx0, dg_f1pre, dg_f1post, (s_w1in, s_w1out) = _ffn_bwd(
        dx1, (h1, z1, a1, o1), x, sp["ffn1_pre_g"], w1_in, w1_out4, sp["ffn1_post_g"], "ffn1")

    da4 = da.reshape(2, 2, HALF_STATES)
    d_ar, d_ai = da4[:, 0].reshape(32, N_STATE), da4[:, 1].reshape(32, N_STATE)
    d_bbr = _diag_blocks(dbh[:, :, :HALF_STATES], 16, N_STATE).transpose(0, 2, 1)
    d_bbi = _diag_blocks(dbh[:, :, HALF_STATES:], 16, N_STATE).transpose(0, 2, 1)
    _, disc_vjp = jax.vjp(_discretise, sp["a_re"], sp["a_im"], sp["log_dt"], sp["b_re"], sp["b_im"])
    g_are, g_aim, g_ldt, g_bre, g_bim = disc_vjp((d_ar, d_ai, d_bbr, d_bbi))
    g_cre = _diag_blocks(dch[:, :HALF_STATES], N_STATE, 16).transpose(0, 2, 1)
    g_cim = -_diag_blocks(dch[:, HALF_STATES:], N_STATE, 16).transpose(0, 2, 1)
    small = {
        "ffn1_pre_g": dg_f1pre, "ffn1_post_g": dg_f1post, "mix_pre_g": dg_mpre, "a_re": g_are, "a_im": g_aim,
        "log_dt": g_ldt, "b_re": g_bre, "b_im": g_bim, "c_re": g_cre, "c_im": g_cim, "d_skip": dd.reshape(1, 512),
        "b_glu": db_glu, "mix_post_g": dg_mpost, "ffn2_pre_g": dg_f2pre, "ffn2_post_g": dg_f2post,
    }
    (small_slots,) = exchange([_pack([small[n] for n in SMALL])], [False], "exchange_small")
    big_slots = {"ffn1_w_in": s_w1in, "ffn1_w_out": s_w1out, "w_mix_in": s_wmi, "w_glu": s_wglu, "w_mix_out": s_wmo,
                 "ffn2_w_in": s_w2in, "ffn2_w_out": s_w2out}
    return sq, dx0, big_slots, small_slots


def kernel(x, ffn1_pre_g, ffn1_w_in, ffn1_w_out, ffn1_post_g, mix_pre_g, w_mix_in, a_re, a_im, log_dt, b_re, b_im, c_re, c_im, d_skip, w_glu, b_glu, w_mix_out, mix_post_g, ffn2_pre_g, ffn2_w_in, ffn2_w_out, ffn2_post_g, loss_target, m_ffn1_pre_g, m_ffn1_w_in, m_ffn1_w_out, m_ffn1_post_g, m_mix_pre_g, m_w_mix_in, m_a_re, m_a_im, m_log_dt, m_b_re, m_b_im, m_c_re, m_c_im, m_d_skip, m_w_glu, m_b_glu, m_w_mix_out, m_mix_post_g, m_ffn2_pre_g, m_ffn2_w_in, m_ffn2_w_out, m_ffn2_post_g, v_ffn1_pre_g, v_ffn1_w_in, v_ffn1_w_out, v_ffn1_post_g, v_mix_pre_g, v_w_mix_in, v_a_re, v_a_im, v_log_dt, v_b_re, v_b_im, v_c_re, v_c_im, v_d_skip, v_w_glu, v_b_glu, v_w_mix_out, v_mix_post_g, v_ffn2_pre_g, v_ffn2_w_in, v_ffn2_w_out, v_ffn2_post_g):
    args = dict(locals())
    w = {n: args[n][0] for n in WEIGHTS}
    m = {n: args["m_" + n][0] for n in WEIGHTS}
    v = {n: args["v_" + n][0] for n in WEIGHTS}

    wb = {n: w[n].astype(BF16) for n in BIG}
    sp = {n: w[n] for n in SMALL}
    for n in ("ffn1_pre_g", "ffn1_post_g", "mix_pre_g", "mix_post_g", "ffn2_pre_g", "ffn2_post_g", "b_glu", "d_skip"):
        sp[n] = w[n].reshape(1, -1)

    sq, grad_x, big_slots, small_slots = local_step(x[0], loss_target[0], sp, wb)
    loss = lax.psum(0.5 / D_MODEL * jnp.sum(sq), ("x", "y", "c"))

    outs = {}
    for n in BIG:
        shp = w[n].shape
        r2 = lambda t: t.reshape(-1, shp[-1])
        res = adamw(r2(w[n]), r2(m[n]), r2(v[n]), big_slots[n].reshape(N_DEV, -1, shp[-1]), f"adamw_{n}")
        outs[n] = [t.reshape((1,) + shp) for t in res]
    res = adamw(_pack([w[n] for n in SMALL]), _pack([m[n] for n in SMALL]), _pack([v[n] for n in SMALL]),
                small_slots, "adamw_small")
    shapes = [(1,) + w[n].shape for n in SMALL]
    unpacked = [_unpack(t, shapes) for t in res]
    for j, n in enumerate(SMALL):
        outs[n] = [unpacked[k][j] for k in range(4)]

    result = [loss, grad_x[None]]
    for k in range(4):
        result += [outs[n][k] for n in WEIGHTS]
    return tuple(result)
```

```python
import functools
import math

import jax
import jax.numpy as jnp
from jax import lax
from jax.experimental import pallas as pl
from jax.experimental.pallas import tpu as pltpu

F32, BF16 = jnp.float32, jnp.bfloat16
SDS = jax.ShapeDtypeStruct

D_MODEL = 1024
N_DEV = 8
HEAD_DIM = 64
PAIR_W = 128
QBLK = 128
DILATIONS = (1, 4, 16)
N_STATE = 64
HALF_STATES = 1024
NORM_EPS = 1e-6
NEG = -1e30
VMEM_LIMIT = 56 * 1024 * 1024
ADAM_LR, ADAM_B1, ADAM_B2, ADAM_EPS, ADAM_WD, ADAM_STEP = 1e-3, 0.9, 0.999, 1e-8, 0.01, 10
SCAN_TM = 256
SCAN_CW = 256


def _params(n_grid):
    return pltpu.CompilerParams(dimension_semantics=("arbitrary",) * n_grid, vmem_limit_bytes=VMEM_LIMIT)


def _dot(a, b):
    return jnp.dot(a, b, preferred_element_type=F32)


def _dot_nt(a, b):
    return lax.dot_general(a, b, (((1,), (1,)), ((), ())), preferred_element_type=F32)


def _dot_tn(a, b):
    return lax.dot_general(a, b, (((0,), (0,)), ((), ())), preferred_element_type=F32)


def _sigmoid(v):
    return 1.0 / (1.0 + jnp.exp(-v))


def _exchange_phase(ins, outs, scatter, sems, start):
    send_sems, recv_sems, loc_sems = sems
    x, y, c = lax.axis_index("x"), lax.axis_index("y"), lax.axis_index("c")
    me = 4 * x + 2 * y + c
    own_copies, sends, arrivals = [], [], []
    for i in range(len(ins)):
        own = ins[i].at[me] if scatter[i] else ins[i]
        own_copies.append(pltpu.make_async_copy(own, outs[i].at[me], loc_sems.at[i]))
        for k in range(1, N_DEV):
            px = 1 - x if k & 4 else x
            py = 1 - y if k & 2 else y
            pc = 1 - c if k & 1 else c
            peer = 4 * px + 2 * py + pc
            src = ins[i].at[peer] if scatter[i] else ins[i]
            common = dict(src_ref=src, send_sem=send_sems.at[i, k - 1], recv_sem=recv_sems.at[i, k - 1],
                          device_id=(px, py, pc), device_id_type=pl.DeviceIdType.MESH)
            sends.append(pltpu.make_async_remote_copy(dst_ref=outs[i].at[me], **common))
            if not start:
                arrivals.append(pltpu.make_async_remote_copy(dst_ref=outs[i].at[peer], **common))
    if start:
        for cp in own_copies + sends:
            cp.start()
    else:
        for cp in arrivals:
            cp.wait_recv()
        for cp in sends:
            cp.wait_send()
        for cp in own_copies:
            cp.wait()


def _comm_shapes(arrs, scatter):
    n = len(arrs)
    out_shapes = [SDS(a.shape if scatter[i] else (N_DEV,) + a.shape, a.dtype) for i, a in enumerate(arrs)]
    sems = [pltpu.SemaphoreType.DMA((n, N_DEV - 1)), pltpu.SemaphoreType.DMA((n, N_DEV - 1)),
            pltpu.SemaphoreType.DMA((n,))]
    return out_shapes, sems


def exchange(arrs, scatter, name):
    n = len(arrs)
    out_shapes, sems = _comm_shapes(arrs, scatter)

    def body(*refs):
        ins, outs, sem_refs = refs[:n], refs[n:2 * n], refs[2 * n:]
        _exchange_phase(ins, outs, scatter, sem_refs, True)
        _exchange_phase(ins, outs, scatter, sem_refs, False)

    anyspec = pl.BlockSpec(memory_space=pl.ANY)
    return pl.pallas_call(
        body, in_specs=[anyspec] * n, out_specs=[anyspec] * n, out_shape=out_shapes, scratch_shapes=sems,
        compiler_params=pltpu.CompilerParams(has_side_effects=True), name=name)(*arrs)


def _call(body, *, grid, in_specs, out_specs, out_shape, args, name, scratch_shapes=(), comm=None):
    n_grid, scratch_shapes = len(grid), list(scratch_shapes)
    if comm is None:
        outs = pl.pallas_call(body, grid=grid, in_specs=in_specs, out_specs=out_specs, out_shape=out_shape,
                              scratch_shapes=scratch_shapes, compiler_params=_params(n_grid), name=name)(*args)
        return outs, []
    arrs, scatter = comm
    nc, n_in, n_out, n_sc = len(arrs), len(in_specs), len(out_specs), len(scratch_shapes)
    comm_shapes, sems = _comm_shapes(arrs, scatter)

    def wrapped(*refs):
        ins, cins = refs[:n_in], refs[n_in:n_in + nc]
        o0 = n_in + nc
        outs, couts = refs[o0:o0 + n_out], refs[o0 + n_out:o0 + n_out + nc]
        s0 = o0 + n_out + nc
        scratch, sem_refs = refs[s0:s0 + n_sc], refs[s0 + n_sc:]
        first = functools.reduce(jnp.logical_and, [pl.program_id(k) == 0 for k in range(n_grid)])
        last = functools.reduce(jnp.logical_and, [pl.program_id(k) == grid[k] - 1 for k in range(n_grid)])

        @pl.when(first)
        def _():
            _exchange_phase(cins, couts, scatter, sem_refs, True)

        body(*ins, *outs, *scratch)

        @pl.when(last)
        def _():
            _exchange_phase(cins, couts, scatter, sem_refs, False)

    anyspec = pl.BlockSpec(memory_space=pl.ANY)
    res = pl.pallas_call(
        wrapped, grid=grid, in_specs=list(in_specs) + [anyspec] * nc, out_specs=list(out_specs) + [anyspec] * nc,
        out_shape=list(out_shape) + comm_shapes, scratch_shapes=scratch_shapes + sems,
        compiler_params=pltpu.CompilerParams(dimension_semantics=("arbitrary",) * n_grid,
                                             vmem_limit_bytes=VMEM_LIMIT, has_side_effects=True),
        name=name)(*args, *arrs)
    return res[:n_out], res[n_out:]


def rms_fwd(x, g, name):
    T, D = x.shape
    tm = 512

    def body(x_ref, g_ref, h_ref):
        xv = x_ref[...]
        r = lax.rsqrt(jnp.mean(xv * xv, axis=-1, keepdims=True) + NORM_EPS)
        h_ref[...] = (xv * r * g_ref[...]).astype(BF16)

    return pl.pallas_call(
        body, grid=(T // tm,),
        in_specs=[pl.BlockSpec((tm, D), lambda i: (i, 0)), pl.BlockSpec((1, D), lambda i: (0, 0))],
        out_specs=pl.BlockSpec((tm, D), lambda i: (i, 0)),
        out_shape=SDS((T, D), BF16), compiler_params=_params(1), name=name)(x, g)


def ffn_in(h, w, name, comm=None):
    T, D = h.shape
    F = w.shape[2]
    tm = 512

    def body(h_ref, wg_ref, wu_ref, z_ref, a_ref):
        hv = h_ref[...]
        zg = _dot(hv, wg_ref[...])
        zu = _dot(hv, wu_ref[...])
        z_ref[0] = zg.astype(BF16)
        z_ref[1] = zu.astype(BF16)
        a_ref[...] = (zg * _sigmoid(zg) * zu).astype(BF16)

    return _call(
        body, grid=(4, T // tm),
        in_specs=[pl.BlockSpec((tm, D), lambda j, i: (i, 0)),
                  pl.BlockSpec((None, D, F), lambda j, i: (j, 0, 0)),
                  pl.BlockSpec((None, D, F), lambda j, i: (j + 4, 0, 0))],
        out_specs=[pl.BlockSpec((2, None, tm, F), lambda j, i: (0, j, i, 0)),
                   pl.BlockSpec((None, tm, F), lambda j, i: (j, i, 0))],
        out_shape=[SDS((2, 4, T, F), BF16), SDS((4, T, F), BF16)],
        args=(h, w, w), name=name, comm=comm)


def mm_nn_b(a, w, nb, out_dtype, name):
    T, K = a.shape
    N = w.shape[2]
    tm = 512

    def body(a_ref, w_ref, o_ref):
        o_ref[...] = _dot(a_ref[...], w_ref[...]).astype(out_dtype)

    return pl.pallas_call(
        body, grid=(nb, T // tm),
        in_specs=[pl.BlockSpec((tm, K), lambda b, i: (i, 0)),
                  pl.BlockSpec((None, K, N), lambda b, i: (b, 0, 0))],
        out_specs=pl.BlockSpec((None, tm, N), lambda b, i: (b, i, 0)),
        out_shape=SDS((nb, T, N), out_dtype), compiler_params=_params(2), name=name)(a, w)


def mm_acc_norm(a, w, xres, g, scale, name, comm=None):
    nb, T, K = a.shape
    D = w.shape[2]
    tm = 512

    def body(a_ref, w_ref, x_ref, g_ref, o_ref, y_ref):
        b = pl.program_id(1)

        @pl.when(b == 0)
        def _():
            o_ref[...] = jnp.zeros_like(o_ref)

        o_ref[...] += _dot(a_ref[...].astype(BF16), w_ref[...])

        @pl.when(b == nb - 1)
        def _():
            o = o_ref[...]
            r = lax.rsqrt(jnp.mean(o * o, axis=-1, keepdims=True) + NORM_EPS)
            y_ref[...] = x_ref[...] + scale * (o * r * g_ref[...])

    return _call(
        body, grid=(T // tm, nb),
        in_specs=[pl.BlockSpec((None, tm, K), lambda i, b: (b, i, 0)),
                  pl.BlockSpec((None, K, D), lambda i, b: (b, 0, 0)),
                  pl.BlockSpec((tm, D), lambda i, b: (i, 0)),
                  pl.BlockSpec((1, D), lambda i, b: (0, 0))],
        out_specs=[pl.BlockSpec((tm, D), lambda i, b: (i, 0)), pl.BlockSpec((tm, D), lambda i, b: (i, 0))],
        out_shape=[SDS((T, D), F32), SDS((T, D), F32)],
        args=(a, w, xres, g), name=name, comm=comm)


def loss_head(y, tgt, name):
    T, D = y.shape
    tm = 512

    def body(y_ref, t_ref, dy_ref, sq_ref):
        @pl.when(pl.program_id(0) == 0)
        def _():
            sq_ref[...] = jnp.zeros_like(sq_ref)

        e = y_ref[...] - t_ref[...]
        dy_ref[...] = e * (1.0 / D)
        sq_ref[...] += jnp.sum(e * e, axis=0, keepdims=True)

    return pl.pallas_call(
        body, grid=(T // tm,),
        in_specs=[pl.BlockSpec((tm, D), lambda i: (i, 0)), pl.BlockSpec((tm, D), lambda i: (i, 0))],
        out_specs=[pl.BlockSpec((tm, D), lambda i: (i, 0)), pl.BlockSpec((1, D), lambda i: (0, 0))],
        out_shape=[SDS((T, D), F32), SDS((1, D), F32)], compiler_params=_params(1), name=name)(y, tgt)


def post_bwd(dy, o, g, scale, name):
    T, D = o.shape
    tm = 512

    def body(dy_ref, o_ref, g_ref, do_ref, dg_ref):
        @pl.when(pl.program_id(0) == 0)
        def _():
            dg_ref[...] = jnp.zeros_like(dg_ref)

        ov = o_ref[...]
        r = scale * dy_ref[...]
        rstd = lax.rsqrt(jnp.mean(ov * ov, axis=-1, keepdims=True) + NORM_EPS)
        oh = ov * rstd
        dg_ref[...] += jnp.sum(r * oh, axis=0, keepdims=True)
        rg = r * g_ref[...]
        do_ref[...] = (rstd * (rg - oh * jnp.mean(rg * oh, axis=-1, keepdims=True))).astype(BF16)

    return pl.pallas_call(
        body, grid=(T // tm,),
        in_specs=[pl.BlockSpec((tm, D), lambda i: (i, 0)), pl.BlockSpec((tm, D), lambda i: (i, 0)),
                  pl.BlockSpec((1, D), lambda i: (0, 0))],
        out_specs=[pl.BlockSpec((tm, D), lambda i: (i, 0)), pl.BlockSpec((1, D), lambda i: (0, 0))],
        out_shape=[SDS((T, D), BF16), SDS((1, D), F32)], compiler_params=_params(1), name=name)(dy, o, g)


def mm_nt_b(gr, w, name):
    T, N = gr.shape
    nb, K, _ = w.shape
    tm = 512

    def body(g_ref, w_ref, o_ref):
        o_ref[...] = _dot_nt(g_ref[...], w_ref[...])

    return pl.pallas_call(
        body, grid=(nb, T // tm),
        in_specs=[pl.BlockSpec((tm, N), lambda b, i: (i, 0)), pl.BlockSpec((None, K, N), lambda b, i: (b, 0, 0))],
        out_specs=pl.BlockSpec((None, tm, K), lambda b, i: (b, i, 0)),
        out_shape=SDS((nb, T, K), F32), compiler_params=_params(2), name=name)(gr, w)


def ffn_dact(do, w_out, z, name):
    T, D = do.shape
    nb, F, _ = w_out.shape
    tm = 512

    def body(g_ref, w_ref, z_ref, dz_ref):
        da = _dot_nt(g_ref[...], w_ref[...])
        zg = z_ref[0].astype(F32)
        zu = z_ref[1].astype(F32)
        sg = _sigmoid(zg)
        dz_ref[0] = (da * zu * (sg * (1.0 + zg * (1.0 - sg)))).astype(BF16)
        dz_ref[1] = (da * zg * sg).astype(BF16)

    return pl.pallas_call(
        body, grid=(nb, T // tm),
        in_specs=[pl.BlockSpec((tm, D), lambda b, i: (i, 0)), pl.BlockSpec((None, F, D), lambda b, i: (b, 0, 0)),
                  pl.BlockSpec((2, None, tm, F), lambda b, i: (0, b, i, 0))],
        out_specs=pl.BlockSpec((2, None, tm, F), lambda b, i: (0, b, i, 0)),
        out_shape=SDS((2, nb, T, F), BF16), compiler_params=_params(2), name=name)(do, w_out, z)


def mm_tn(a, g, a_batched, g_batched, nb, name):
    T = a.shape[-2]
    K, N = a.shape[-1], g.shape[-1]
    tk = 512
    nk = T // tk

    def body(a_ref, g_ref, o_ref, acc):
        k = pl.program_id(1)

        @pl.when(k == 0)
        def _():
            acc[...] = jnp.zeros_like(acc)

        acc[...] += _dot_tn(a_ref[...].astype(BF16), g_ref[...].astype(BF16))

        @pl.when(k == nk - 1)
        def _():
            o_ref[...] = acc[...].astype(BF16)

    a_spec = (pl.BlockSpec((None, tk, K), lambda b, k: (b, k, 0)) if a_batched
              else pl.BlockSpec((tk, K), lambda b, k: (k, 0)))
    g_spec = (pl.BlockSpec((None, tk, N), lambda b, k: (b, k, 0)) if g_batched
              else pl.BlockSpec((tk, N), lambda b, k: (k, 0)))
    return pl.pallas_call(
        body, grid=(nb, nk), in_specs=[a_spec, g_spec],
        out_specs=pl.BlockSpec((None, K, N), lambda b, k: (b, 0, 0)),
        out_shape=SDS((nb, K, N), BF16), scratch_shapes=[pltpu.VMEM((K, N), F32)],
        compiler_params=_params(2), name=name)(a, g)


def dh_pre_bwd(dz, w, x, g, dyres, name, comm=None):
    nb, T, F = dz.shape
    D = w.shape[1]
    tm = 512

    def body(dz_ref, w_ref, x_ref, g_ref, dy_ref, dx_ref, dg_ref, acc):
        i, b = pl.program_id(0), pl.program_id(1)

        @pl.when(jnp.logical_and(i == 0, b == 0))
        def _():
            dg_ref[...] = jnp.zeros_like(dg_ref)

        @pl.when(b == 0)
        def _():
            acc[...] = jnp.zeros_like(acc)

        acc[...] += _dot_nt(dz_ref[...].astype(BF16), w_ref[...])

        @pl.when(b == nb - 1)
        def _():
            dh = acc[...]
            xv = x_ref[...]
            rstd = lax.rsqrt(jnp.mean(xv * xv, axis=-1, keepdims=True) + NORM_EPS)
            xh = xv * rstd
            dg_ref[...] += jnp.sum(dh * xh, axis=0, keepdims=True)
            dhg = dh * g_ref[...]
            dx_ref[...] = dy_ref[...] + rstd * (dhg - xh * jnp.mean(dhg * xh, axis=-1, keepdims=True))

    return _call(
        body, grid=(T // tm, nb),
        in_specs=[pl.BlockSpec((None, tm, F), lambda i, b: (b, i, 0)),
                  pl.BlockSpec((None, D, F), lambda i, b: (b, 0, 0)),
                  pl.BlockSpec((tm, D), lambda i, b: (i, 0)),
                  pl.BlockSpec((1, D), lambda i, b: (0, 0)),
                  pl.BlockSpec((tm, D), lambda i, b: (i, 0))],
        out_specs=[pl.BlockSpec((tm, D), lambda i, b: (i, 0)), pl.BlockSpec((1, D), lambda i, b: (0, 0))],
        out_shape=[SDS((T, D), F32), SDS((1, D), F32)],
        scratch_shapes=[pltpu.VMEM((tm, D), F32)], args=(dz, w, x, g, dyres), name=name, comm=comm)


ATTN_GROUP = {1: 4, 4: 1, 16: 1}
ATTN_UNROLL = 4


def _attn_masks():
    qi = lax.broadcasted_iota(jnp.int32, (QBLK, QBLK), 0)
    kj = lax.broadcasted_iota(jnp.int32, (QBLK, QBLK), 1)
    cur_ok = kj <= qi
    prev_ok = kj >= qi
    dcur = (qi - kj).astype(F32)
    return cur_ok, prev_ok, dcur, dcur + float(QBLK)


def _head_slopes(p, d):
    out = []
    for hq in range(2):
        v = [float(d) * 2.0 ** -(2 * q + hq + 1) for q in range(4)]
        out.append(jnp.where(p == 0, v[0], jnp.where(p == 1, v[1], jnp.where(p == 2, v[2], v[3]))))
    return out


def _rows(start, d):
    return pl.ds(start, QBLK, stride=d) if d > 1 else pl.ds(start, QBLK)


def _pair_spec(rows, part, blk):
    return pl.BlockSpec((None, rows, PAIR_W), lambda p, n: (2 * part + p // 2, blk(n), p % 2))


def _for_query_blocks(d, groups, several):
    blocks = [(g, r) for g in range(groups) for r in range(d)]
    if len(blocks) <= 2 * ATTN_UNROLL:
        for s in range(0, len(blocks), ATTN_UNROLL):
            several(blocks[s:s + ATTN_UNROLL])
    else:
        def some(i, carry):
            several([(0, i * ATTN_UNROLL + j) for j in range(ATTN_UNROLL)])
            return carry

        lax.fori_loop(0, d // ATTN_UNROLL, some, 0)


def attn_fwd(proj, d, name):
    T = proj.shape[1]
    sb, groups = QBLK * d, ATTN_GROUP[d]
    rb = sb * groups
    nblk = T // rb

    def body(q_ref, kc_ref, kp_ref, vc_ref, vp_ref, o_ref, l_ref):
        p, n = pl.program_id(0), pl.program_id(1)
        cur_ok, prev_ok, dcur, dprev = _attn_masks()
        first_ok = jnp.logical_and(prev_ok, n > 0)
        lane_head = lax.broadcasted_iota(jnp.int32, (QBLK, PAIR_W), 1) // HEAD_DIM
        slopes = _head_slopes(p, d)

        def several(blocks):
            work = []
            for g, r in blocks:
                rows = _rows(g * sb + r, d)
                q = q_ref[rows, :]
                kc, vc = kc_ref[rows, :].astype(BF16), vc_ref[rows, :].astype(BF16)
                if g == 0:
                    prow, pok = _rows(r, d), first_ok
                    kp, vp = kp_ref[prow, :].astype(BF16), vp_ref[prow, :].astype(BF16)
                else:
                    prow, pok = _rows((g - 1) * sb + r, d), prev_ok
                    kp, vp = kc_ref[prow, :].astype(BF16), vc_ref[prow, :].astype(BF16)
                for hq in range(2):
                    qm = jnp.where(lane_head == hq, q, 0.0).astype(BF16)
                    work.append([rows, hq, pok, vc, vp, _dot_nt(qm, kc), _dot_nt(qm, kp)])
            for w in work:
                _, hq, pok, _, _, sc, sp = w
                sc = jnp.where(cur_ok, sc * 0.125 - slopes[hq] * dcur, NEG)
                sp = jnp.where(pok, sp * 0.125 - slopes[hq] * dprev, NEG)
                m = jnp.maximum(jnp.max(sc, axis=1, keepdims=True), jnp.max(sp, axis=1, keepdims=True))
                pc = jnp.exp(sc - m)
                pp = jnp.exp(sp - m)
                den = jnp.sum(pc, axis=1, keepdims=True) + jnp.sum(pp, axis=1, keepdims=True)
                w[5:] = [pc.astype(BF16), pp.astype(BF16), 1.0 / den, m + jnp.log(den)]
            for i in range(0, len(work), 2):
                o_acc = jnp.zeros((QBLK, PAIR_W), F32)
                l_acc = jnp.zeros((QBLK, PAIR_W), F32)
                for rows, hq, _, vc, vp, pc, pp, inv, lse in work[i:i + 2]:
                    hm = lane_head == hq
                    o_acc = jnp.where(hm, (_dot(pc, vc) + _dot(pp, vp)) * inv, o_acc)
                    l_acc = jnp.where(hm, lse, l_acc)
                o_ref[rows, :] = o_acc
                l_ref[rows, :] = l_acc

        _for_query_blocks(d, groups, several)

    cur = lambda part: _pair_spec(rb, part, lambda n: n)
    prv = lambda part: _pair_spec(sb, part, lambda n: jnp.maximum(n * groups - 1, 0))
    return pl.pallas_call(
        body, grid=(4, nblk), in_specs=[cur(0), cur(1), prv(1), cur(2), prv(2)], out_specs=[cur(0), cur(0)],
        out_shape=[SDS((2, T, 2 * PAIR_W), F32), SDS((2, T, 2 * PAIR_W), F32)],
        compiler_params=_params(2), name=name)(proj, proj, proj, proj, proj)


def attn_merge(os_, ls_, name):
    _, T, HW = os_[0].shape
    tm = 512

    def body(o1, o2, o3, l1, l2, l3, o_ref, l_ref):
        a, b, c = l1[...], l2[...], l3[...]
        m = jnp.maximum(jnp.maximum(a, b), c)
        ea, eb, ec = jnp.exp(a - m), jnp.exp(b - m), jnp.exp(c - m)
        s = ea + eb + ec
        o_ref[...] = (ea * o1[...] + eb * o2[...] + ec * o3[...]) * (1.0 / s)
        l_ref[...] = m + jnp.log(s)

    spec = pl.BlockSpec((None, tm, HW), lambda h, i: (h, i, 0))
    return pl.pallas_call(
        body, grid=(2, T // tm), in_specs=[spec] * 6, out_specs=[spec, spec],
        out_shape=[SDS((2, T, HW), F32), SDS((2, T, HW), F32)],
        compiler_params=_params(2), name=name)(*os_, *ls_)


def attn_bwd(proj, dcat, o, lse, acc, d, name):
    T = proj.shape[1]
    sb, groups = QBLK * d, ATTN_GROUP[d]
    rb = sb * groups
    nblk = T // rb
    has_acc = acc is not None

    def body(*refs):
        (qc_ref, qn_ref, kc_ref, kp_ref, vc_ref, vp_ref, dc_ref, dn_ref, oc_ref, on_ref, lc_ref, ln_ref) = refs[:12]
        acc_ref = refs[12] if has_acc else None
        out_ref = refs[-1]
        p, n = pl.program_id(0), pl.program_id(1)
        cur_ok, prev_ok, dcur, dprev = _attn_masks()
        first_ok = jnp.logical_and(prev_ok, n > 0)
        last_ok = jnp.logical_and(prev_ok, n < nblk - 1)
        lane_head = lax.broadcasted_iota(jnp.int32, (QBLK, PAIR_W), 1) // HEAD_DIM
        slopes = _head_slopes(p, d)

        def one(g, r):
            rows = _rows(g * sb + r, d)
            q_c, do_c, o_c, l_c = qc_ref[rows, :], dc_ref[rows, :], oc_ref[rows, :], lc_ref[rows, :]
            k_c, v_c = kc_ref[rows, :].astype(BF16), vc_ref[rows, :].astype(BF16)
            if g == 0:
                prow, pok_c = _rows(r, d), first_ok
                k_p, v_p = kp_ref[prow, :].astype(BF16), vp_ref[prow, :].astype(BF16)
            else:
                prow, pok_c = _rows((g - 1) * sb + r, d), prev_ok
                k_p, v_p = kc_ref[prow, :].astype(BF16), vc_ref[prow, :].astype(BF16)
            if g == groups - 1:
                nrow, pok_n = _rows(r, d), last_ok
                q_n, do_n, o_n, l_n = qn_ref[nrow, :], dn_ref[nrow, :], on_ref[nrow, :], ln_ref[nrow, :]
            else:
                nrow, pok_n = _rows((g + 1) * sb + r, d), prev_ok
                q_n, do_n, o_n, l_n = qc_ref[nrow, :], dc_ref[nrow, :], oc_ref[nrow, :], lc_ref[nrow, :]
            heads = []
            for hq in range(2):
                hm = lane_head == hq
                qm_c = jnp.where(hm, q_c, 0.0).astype(BF16)
                qm_n = jnp.where(hm, q_n, 0.0).astype(BF16)
                dom_c = jnp.where(hm, do_c, 0.0)
                dom_n = jnp.where(hm, do_n, 0.0)
                dd_c = jnp.sum(dom_c * o_c, axis=1, keepdims=True)
                dd_n = jnp.sum(dom_n * o_n, axis=1, keepdims=True)
                ls_c = jnp.max(jnp.where(hm, l_c, NEG), axis=1, keepdims=True)
                ls_n = jnp.max(jnp.where(hm, l_n, NEG), axis=1, keepdims=True)
                dob_c, dob_n = dom_c.astype(BF16), dom_n.astype(BF16)
                mm = [(_dot_nt(qm_c, k_c), _dot_nt(dob_c, v_c)), (_dot_nt(qm_c, k_p), _dot_nt(dob_c, v_p)),
                      (_dot_nt(qm_n, k_c), _dot_nt(dob_n, v_c))]
                heads.append(dict(hq=hq, qm_c=qm_c, qm_n=qm_n, dob_c=dob_c, dob_n=dob_n, mm=mm,
                                  dd=(dd_c, dd_c, dd_n), ls=(ls_c, ls_c, ls_n)))
            return dict(rows=rows, k_c=k_c, k_p=k_p, heads=heads, oks=(cur_ok, pok_c, pok_n))

        def several(blocks):
            work = [one(g, r) for g, r in blocks]
            for w in work:
                for h in w["heads"]:
                    slope, dist = slopes[h["hq"]], (dcur, dprev, dprev)
                    h["pr"], h["ds"] = [], []
                    for j in range(3):
                        s = jnp.where(w["oks"][j], h["mm"][j][0] * 0.125 - slope * dist[j], NEG)
                        pr = jnp.exp(s - h["ls"][j])
                        h["pr"].append(pr.astype(BF16))
                        h["ds"].append((pr * (h["mm"][j][1] - h["dd"][j])).astype(BF16))
            for w in work:
                dq = jnp.zeros((QBLK, PAIR_W), F32)
                dk = jnp.zeros((QBLK, PAIR_W), F32)
                dv = jnp.zeros((QBLK, PAIR_W), F32)
                for h in w["heads"]:
                    ds, pr = h["ds"], h["pr"]
                    dq_h = _dot(ds[0], w["k_c"]) + _dot(ds[1], w["k_p"])
                    dk += (_dot_tn(ds[0], h["qm_c"]) + _dot_tn(ds[2], h["qm_n"])) * 0.125
                    dv += _dot_tn(pr[0], h["dob_c"]) + _dot_tn(pr[2], h["dob_n"])
                    dq = jnp.where(lane_head == h["hq"], dq_h * 0.125, dq)
                for part, val in enumerate((dq, dk, dv)):
                    if has_acc:
                        val = val + acc_ref.at[part][w["rows"], :]
                    out_ref.at[part][w["rows"], :] = val

        _for_query_blocks(d, groups, several)

    cur = lambda part: _pair_spec(rb, part, lambda n: n)
    prv = lambda part: _pair_spec(sb, part, lambda n: jnp.maximum(n * groups - 1, 0))
    nxt = lambda part: _pair_spec(sb, part, lambda n: jnp.minimum((n + 1) * groups, T // sb - 1))
    full = pl.BlockSpec((3, None, rb, PAIR_W), lambda p, n: (0, p // 2, n, p % 2))
    in_specs = [cur(0), nxt(0), cur(1), prv(1), cur(2), prv(2), cur(0), nxt(0), cur(0), nxt(0), cur(0), nxt(0)]
    args = [proj, proj, proj, proj, proj, proj, dcat, dcat, o, o, lse, lse]
    if has_acc:
        in_specs.append(full)
        args.append(acc)
    return pl.pallas_call(
        body, grid=(4, nblk), in_specs=in_specs, out_specs=full,
        out_shape=SDS((3, 2, T, 2 * PAIR_W), F32), compiler_params=_params(2), name=name)(*args)


def _scan_chunk(buf, apow_ref, c0, reverse, ptab=None, carry=None):
    tm = buf.shape[0]
    cre = pl.ds(c0, SCAN_CW)
    cim = pl.ds(pl.multiple_of(c0 + HALF_STATES, 128), SCAN_CW)
    re, im = buf[:, cre], buf[:, cim]
    row = lax.broadcasted_iota(jnp.int32, (tm, SCAN_CW), 0)
    sgn = -1.0 if reverse else 1.0
    k, s = 1, 0
    while k < tm:
        ar = apow_ref[pl.ds(s, 1), cre]
        ai = sgn * apow_ref[pl.ds(s, 1), cim]
        if reverse:
            ok, shift = row < tm - k, tm - k
        else:
            ok, shift = row >= k, k
        sre = jnp.where(ok, pltpu.roll(re, shift, 0), 0.0)
        sim = jnp.where(ok, pltpu.roll(im, shift, 0), 0.0)
        re, im = re + ar * sre - ai * sim, im + ar * sim + ai * sre
        k, s = 2 * k, s + 1
    if ptab is not None:
        pr, pi = ptab[:, cre], ptab[:, cim]
        cr, ci = carry[:, cre], carry[:, cim]
        re, im = re + pr * cr - pi * ci, im + pr * ci + pi * cr
    buf[:, cre] = re
    buf[:, cim] = im


def _scan_tile(buf, apow_ref, reverse, ptab=None, carry=None):
    per_half = HALF_STATES // SCAN_CW

    def chunk(j, _):
        c0 = pl.multiple_of((j // per_half) * 2 * HALF_STATES + (j % per_half) * SCAN_CW, 128)
        _scan_chunk(buf, apow_ref, c0, reverse, ptab, carry)
        return 0

    lax.fori_loop(0, 2 * per_half, chunk, 0)


def _init_ptab(ptab, apow_ref, reverse):
    tm, width = ptab.shape
    row = lax.broadcasted_iota(jnp.int32, (tm, width), 0)
    col = lax.broadcasted_iota(jnp.int32, (1, width), 1)
    a = apow_ref[pl.ds(0, 1), :]
    if reverse:
        is_im = (col // HALF_STATES) % 2 == 1
        a = jnp.where(is_im, -a, a)
    seed = (tm - 1) if reverse else 0
    ptab[...] = jnp.where(row == seed, a, 0.0)
    _scan_tile(ptab, apow_ref, reverse)


def ssm_fwd(proj, bh, ch, apow, dskip, name, comm=None):
    _, T, C = proj.shape
    tm = SCAN_TM
    SW = 4 * HALF_STATES

    def body(u_ref, bh_ref, ch_ref, apow_ref, dsk_ref, y_ref, s_ref, ptab, carry):
        @pl.when(pl.program_id(0) == 0)
        def _():
            _init_ptab(ptab, apow_ref, False)
            carry[...] = jnp.zeros_like(carry)

        for h in range(2):
            s_ref[:, pl.ds(h * 2 * HALF_STATES, 2 * HALF_STATES)] = _dot(u_ref[h].astype(BF16), bh_ref[h])
        _scan_tile(s_ref, apow_ref, False, ptab, carry)
        carry[...] = s_ref[pl.ds(tm - 1, 1), :]
        for h in range(2):
            sv = s_ref[:, pl.ds(h * 2 * HALF_STATES, 2 * HALF_STATES)].astype(BF16)
            y_ref[h] = _dot(sv, ch_ref[h]) + dsk_ref[h] * u_ref[h]

    return _call(
        body, grid=(T // tm,),
        in_specs=[pl.BlockSpec((2, tm, C), lambda i: (3, i, 0)),
                  pl.BlockSpec((2, C, 2 * HALF_STATES), lambda i: (0, 0, 0)),
                  pl.BlockSpec((2, 2 * HALF_STATES, C), lambda i: (0, 0, 0)),
                  pl.BlockSpec((8, SW), lambda i: (0, 0)),
                  pl.BlockSpec((2, 1, C), lambda i: (0, 0, 0))],
        out_specs=[pl.BlockSpec((2, tm, C), lambda i: (0, i, 0)), pl.BlockSpec((tm, SW), lambda i: (i, 0))],
        out_shape=[SDS((2, T, C), F32), SDS((T, SW), F32)],
        scratch_shapes=[pltpu.VMEM((tm, SW), F32), pltpu.VMEM((1, SW), F32)],
        args=(proj, bh, ch, apow, dskip), name=name, comm=comm)


def ssm_bwd(dy, proj, st, bh, ch, apow, dskip, name):
    _, T, C = proj.shape
    tm = SCAN_TM
    nt = T // tm
    SW = 4 * HALF_STATES
    HS2 = 2 * HALF_STATES

    def body(dy_ref, u_ref, s_ref, sp_ref, bh_ref, ch_ref, apow_ref, dsk_ref,
             du_ref, da_ref, dbh_ref, dch_ref, dd_ref, lam, ptab, carry):
        i = pl.program_id(0)

        @pl.when(i == 0)
        def _():
            _init_ptab(ptab, apow_ref, True)
            carry[...] = jnp.zeros_like(carry)
            da_ref[...] = jnp.zeros_like(da_ref)
            dbh_ref[...] = jnp.zeros_like(dbh_ref)
            dch_ref[...] = jnp.zeros_like(dch_ref)
            dd_ref[...] = jnp.zeros_like(dd_ref)

        for h in range(2):
            lam[:, pl.ds(h * HS2, HS2)] = _dot_nt(dy_ref[h].astype(BF16), ch_ref[h])
        _scan_tile(lam, apow_ref, True, ptab, carry)
        carry[...] = lam[pl.ds(0, 1), :]

        first = i == nt - 1
        per_half = HALF_STATES // SCAN_CW

        def chunk(j, _):
            c0 = pl.multiple_of((j // per_half) * HS2 + (j % per_half) * SCAN_CW, 128)
            cre, cim = pl.ds(c0, SCAN_CW), pl.ds(pl.multiple_of(c0 + HALF_STATES, 128), SCAN_CW)
            row = lax.broadcasted_iota(jnp.int32, (tm, SCAN_CW), 0)
            pre = jnp.where(first, 0.0, sp_ref[pl.ds(7, 1), cre])
            pim = jnp.where(first, 0.0, sp_ref[pl.ds(7, 1), cim])
            spr = jnp.where(row == 0, pre, pltpu.roll(s_ref[:, cre], 1, 0))
            spi = jnp.where(row == 0, pim, pltpu.roll(s_ref[:, cim], 1, 0))
            lr, li = lam[:, cre], lam[:, cim]
            da_ref[:, cre] += jnp.sum(lr * spr + li * spi, axis=0, keepdims=True)
            da_ref[:, cim] += jnp.sum(li * spr - lr * spi, axis=0, keepdims=True)
            return 0

        lax.fori_loop(0, 2 * per_half, chunk, 0)

        for h in range(2):
            lb = lam[:, pl.ds(h * HS2, HS2)].astype(BF16)
            dyv, uv = dy_ref[h], u_ref[h]
            du_ref[h] = _dot_nt(lb, bh_ref[h]) + dsk_ref[h] * dyv
            dbh_ref[h] += _dot_tn(uv.astype(BF16), lb)
            dch_ref[h] += _dot_tn(s_ref[:, pl.ds(h * HS2, HS2)].astype(BF16), dyv.astype(BF16))
            dd_ref[h] += jnp.sum(dyv * uv, axis=0, keepdims=True)

    rev = lambda i: nt - 1 - i
    return pl.pallas_call(
        body, grid=(nt,),
        in_specs=[pl.BlockSpec((2, tm, C), lambda i: (0, rev(i), 0)),
                  pl.BlockSpec((2, tm, C), lambda i: (3, rev(i), 0)),
                  pl.BlockSpec((tm, SW), lambda i: (rev(i), 0)),
                  pl.BlockSpec((8, SW), lambda i: (jnp.maximum(rev(i) * (tm // 8) - 1, 0), 0)),
                  pl.BlockSpec((2, C, HS2), lambda i: (0, 0, 0)),
                  pl.BlockSpec((2, HS2, C), lambda i: (0, 0, 0)),
                  pl.BlockSpec((8, SW), lambda i: (0, 0)),
                  pl.BlockSpec((2, 1, C), lambda i: (0, 0, 0))],
        out_specs=[pl.BlockSpec((2, tm, C), lambda i: (0, rev(i), 0)),
                   pl.BlockSpec((1, SW), lambda i: (0, 0)),
                   pl.BlockSpec((2, C, HS2), lambda i: (0, 0, 0)),
                   pl.BlockSpec((2, HS2, C), lambda i: (0, 0, 0)),
                   pl.BlockSpec((2, 1, C), lambda i: (0, 0, 0))],
        out_shape=[SDS((2, T, C), F32), SDS((1, SW), F32), SDS((2, C, HS2), F32), SDS((2, HS2, C), F32),
                   SDS((2, 1, C), F32)],
        scratch_shapes=[pltpu.VMEM((tm, SW), F32), pltpu.VMEM((tm, SW), F32), pltpu.VMEM((1, SW), F32)],
        compiler_params=_params(1), name=name)(dy, proj, st, st, bh, ch, apow, dskip)


_GELU_C = math.sqrt(2.0 / math.pi)


def _gelu(x):
    t = jnp.tanh(_GELU_C * (x + 0.044715 * x * x * x))
    return 0.5 * x * (1.0 + t), t


def glu_fwd(y, w, b, name):
    _, T, C = y.shape
    tm = 512

    def body(y_ref, w_ref, b_ref, o_ref, lg_ref):
        y0, _ = _gelu(y_ref[0])
        y1, _ = _gelu(y_ref[1])
        lg = _dot(y0.astype(BF16), w_ref[0]) + _dot(y1.astype(BF16), w_ref[1]) + b_ref[...]
        sg = _sigmoid(lg)
        o_ref[0] = y0 * sg[:, :C]
        o_ref[1] = y1 * sg[:, C:]
        lg_ref[0] = lg[:, :C]
        lg_ref[1] = lg[:, C:]

    return pl.pallas_call(
        body, grid=(T // tm,),
        in_specs=[pl.BlockSpec((2, tm, C), lambda i: (0, i, 0)), pl.BlockSpec((2, C, 2 * C), lambda i: (0, 0, 0)),
                  pl.BlockSpec((1, 2 * C), lambda i: (0, 0))],
        out_specs=[pl.BlockSpec((2, tm, C), lambda i: (0, i, 0)), pl.BlockSpec((2, tm, C), lambda i: (0, i, 0))],
        out_shape=[SDS((2, T, C), F32), SDS((2, T, C), F32)], compiler_params=_params(1), name=name)(y, w, b)


def glu_bwd(dcat, y, lg, w, name):
    _, T, C = y.shape
    tm = 512

    def body(d_ref, y_ref, lg_ref, w_ref, dy_ref, dw_ref, db_ref):
        @pl.when(pl.program_id(0) == 0)
        def _():
            dw_ref[...] = jnp.zeros_like(dw_ref)
            db_ref[...] = jnp.zeros_like(db_ref)

        y2, th, sg, dlg = [], [], [], []
        for h in range(2):
            yy, tt = _gelu(y_ref[h])
            ss = _sigmoid(lg_ref[h])
            y2.append(yy)
            th.append(tt)
            sg.append(ss)
            dlg.append(d_ref[h] * yy * ss * (1.0 - ss))
        dl = jnp.concatenate(dlg, axis=1)
        dlb = dl.astype(BF16)
        db_ref[...] += jnp.sum(dl, axis=0, keepdims=True)
        for h in range(2):
            dy2 = d_ref[h] * sg[h] + _dot_nt(dlb, w_ref[h])
            yv = y_ref[h]
            dgelu = 0.5 * (1.0 + th[h]) + 0.5 * yv * (1.0 - th[h] * th[h]) * _GELU_C * (1.0 + 3 * 0.044715 * yv * yv)
            dy_ref[h] = dy2 * dgelu
            dw_ref[h] += _dot_tn(y2[h].astype(BF16), dlb)

    return pl.pallas_call(
        body, grid=(T // tm,),
        in_specs=[pl.BlockSpec((2, tm, C), lambda i: (1, i, 0)), pl.BlockSpec((2, tm, C), lambda i: (0, i, 0)),
                  pl.BlockSpec((2, tm, C), lambda i: (0, i, 0)), pl.BlockSpec((2, C, 2 * C), lambda i: (0, 0, 0))],
        out_specs=[pl.BlockSpec((2, tm, C), lambda i: (0, i, 0)), pl.BlockSpec((2, C, 2 * C), lambda i: (0, 0, 0)),
                   pl.BlockSpec((1, 2 * C), lambda i: (0, 0))],
        out_shape=[SDS((2, T, C), F32), SDS((2, C, 2 * C), F32), SDS((1, 2 * C), F32)],
        compiler_params=_params(1), name=name)(dcat, y, lg, w)


def adamw(w, m, v, slots, name):
    R, C = w.shape
    tr = R
    for cand in (512, 256, 128, 64, 32, 16, 8):
        if R % cand == 0 and cand * C * 4 <= 2 * 1024 * 1024:
            tr = cand
            break
    c1 = 1.0 / (1.0 - ADAM_B1 ** ADAM_STEP)
    c2 = 1.0 / (1.0 - ADAM_B2 ** ADAM_STEP)

    def body(w_ref, m_ref, v_ref, s_ref, g_ref, d_ref, nm_ref, nv_ref):
        g = s_ref[0].astype(F32)
        for j in range(1, N_DEV):
            g = g + s_ref[j].astype(F32)
        nm = ADAM_B1 * m_ref[...] + (1.0 - ADAM_B1) * g
        nv = ADAM_B2 * v_ref[...] + (1.0 - ADAM_B2) * (g * g)
        g_ref[...] = g
        nm_ref[...] = nm
        nv_ref[...] = nv
        d_ref[...] = -ADAM_LR * ((nm * c1) / (jnp.sqrt(nv * c2) + ADAM_EPS) + ADAM_WD * w_ref[...])

    spec = pl.BlockSpec((tr, C), lambda i: (i, 0))
    return pl.pallas_call(
        body, grid=(R // tr,),
        in_specs=[spec, spec, spec, pl.BlockSpec((N_DEV, tr, C), lambda i: (0, i, 0))],
        out_specs=[spec] * 4, out_shape=[SDS((R, C), F32)] * 4, compiler_params=_params(1), name=name)(w, m, v, slots)


def _discretise(a_re, a_im, log_dt, b_re, b_im):
    dt = jnp.exp(log_dt)[:, None]
    e = jnp.exp(dt * a_re)
    ar, ai = e * jnp.cos(dt * a_im), e * jnp.sin(dt * a_im)
    den = a_re * a_re + a_im * a_im
    nr, ni = ar - 1.0, ai
    wr = (nr * a_re + ni * a_im) / den
    wi = (ni * a_re - nr * a_im) / den
    bbr = wr[..., None] * b_re - wi[..., None] * b_im
    bbi = wr[..., None] * b_im + wi[..., None] * b_re
    return ar, ai, bbr, bbi


def _block_diag(t):
    eye = jnp.eye(16, dtype=t.dtype)
    r, c = t.shape[1], t.shape[2]
    return jnp.einsum("hgrc,gk->hgrkc", t.reshape(2, 16, r, c), eye).reshape(2, 16 * r, 16 * c)


def _diag_blocks(m, r, c):
    eye = jnp.eye(16, dtype=m.dtype)
    return jnp.einsum("hgrkc,gk->hgrc", m.reshape(2, 16, r, 16, c), eye).reshape(32, r, c)


def _state_vec(re, im):
    return jnp.stack([re.reshape(2, HALF_STATES), im.reshape(2, HALF_STATES)], axis=1).reshape(-1)


BIG = ("ffn1_w_in", "ffn1_w_out", "w_mix_in", "w_glu", "w_mix_out", "ffn2_w_in", "ffn2_w_out")
WEIGHTS = ("ffn1_pre_g", "ffn1_w_in", "ffn1_w_out", "ffn1_post_g", "mix_pre_g", "w_mix_in", "a_re", "a_im", "log_dt",
           "b_re", "b_im", "c_re", "c_im", "d_skip", "w_glu", "b_glu", "w_mix_out", "mix_post_g", "ffn2_pre_g",
           "ffn2_w_in", "ffn2_w_out", "ffn2_post_g")
SMALL = tuple(n for n in WEIGHTS if n not in BIG)
PACK_COLS = 1024


def _pack(parts):
    flat = jnp.concatenate([p.reshape(-1) for p in parts])
    rows = -(-flat.shape[0] // (8 * PACK_COLS)) * 8
    return jnp.pad(flat, (0, rows * PACK_COLS - flat.shape[0])).reshape(rows, PACK_COLS)


def _unpack(packed, shapes):
    flat, out, off = packed.reshape(-1), [], 0
    for s in shapes:
        n = math.prod(s)
        out.append(flat[off:off + n].reshape(s))
        off += n
    return out


def _gather(names, wb):
    return [wb[n] for n in names], [False] * len(names)


def _ffn_bwd(dy, saved, x, pre_g, w_in, w_out4, post_g, tag):
    h, z, a, o = saved
    T = x.shape[0]
    do, dg_post = post_bwd(dy, o, post_g, 0.5, f"{tag}_post_bwd")
    dz = ffn_dact(do, w_out4, z, f"{tag}_dact")
    dz8 = dz.reshape(8, T, dz.shape[-1])
    dw_out = mm_tn(a, do, True, False, 4, f"{tag}_dwout").reshape(8, -1, D_MODEL)
    dw_in = mm_tn(h, dz8, False, True, 8, f"{tag}_dwin")
    (dx, dg_pre), slots = dh_pre_bwd(dz8, w_in, x, pre_g, dy, f"{tag}_dh", comm=([dw_in, dw_out], [True, True]))
    return dx, dg_pre, dg_post, slots


def local_step(x, tgt, sp, wb):
    T = x.shape[0]
    ar, ai, bbr, bbi = _discretise(sp["a_re"], sp["a_im"], sp["log_dt"], sp["b_re"], sp["b_im"])
    pr, pi, rows = ar, ai, []
    for _ in range(8):
        rows.append(_state_vec(pr, pi))
        pr, pi = pr * pr - pi * pi, 2.0 * pr * pi
    apow = jnp.stack(rows)
    bh = jnp.concatenate([_block_diag(bbr.transpose(0, 2, 1)), _block_diag(bbi.transpose(0, 2, 1))], axis=2)
    ch = jnp.concatenate([_block_diag(sp["c_re"].transpose(0, 2, 1)), _block_diag(-sp["c_im"].transpose(0, 2, 1))], axis=1)
    bh, ch = bh.astype(BF16), ch.astype(BF16)
    dskip = sp["d_skip"].reshape(2, 1, 256)

    (w1_in,) = exchange(*_gather(["ffn1_w_in"], wb), "gather_w1in")
    h1 = rms_fwd(x, sp["ffn1_pre_g"], "ffn1_rms")
    (z1, a1), (w1_out, w_mi) = ffn_in(h1, w1_in, "ffn1_in", comm=_gather(["ffn1_w_out", "w_mix_in"], wb))
    w1_out4 = w1_out.reshape(4, -1, D_MODEL)
    (o1, x1), (w_glu, w_mo, w2_out) = mm_acc_norm(
        a1, w1_out4, x, sp["ffn1_post_g"], 0.5, "ffn1_out", comm=_gather(["w_glu", "w_mix_out", "ffn2_w_out"], wb))
    w_glu2, w_mo4, w2_out4 = w_glu.reshape(2, 256, 512), w_mo.reshape(4, 256, D_MODEL), w2_out.reshape(4, -1, D_MODEL)
    h2 = rms_fwd(x1, sp["mix_pre_g"], "mix_rms")
    proj = mm_nn_b(h2, w_mi, 8, F32, "mix_proj")
    (y_ssm, states), (w2_in,) = ssm_fwd(proj, bh, ch, apow, dskip, "ssm_fwd", comm=_gather(["ffn2_w_in"], wb))
    os_, ls_ = [], []
    for d in DILATIONS:
        o_d, l_d = attn_fwd(proj, d, f"attn_fwd_d{d}")
        os_.append(o_d)
        ls_.append(l_d)
    o_att, lse = attn_merge(os_, ls_, "attn_merge")
    o_ssm, lg = glu_fwd(y_ssm, w_glu2, sp["b_glu"], "glu_fwd")
    cat = jnp.concatenate([o_att, o_ssm], axis=0)
    (mixed, x2), _ = mm_acc_norm(cat, w_mo4, x1, sp["mix_post_g"], 1.0, "mix_out")
    h3 = rms_fwd(x2, sp["ffn2_pre_g"], "ffn2_rms")
    (z3, a3), _ = ffn_in(h3, w2_in, "ffn2_in")
    (o3, x3), _ = mm_acc_norm(a3, w2_out4, x2, sp["ffn2_post_g"], 0.5, "ffn2_out")
    dy3, sq = loss_head(x3, tgt, "loss_head")

    dx2, dg_f2pre, dg_f2post, (s_w2in, s_w2out) = _ffn_bwd(
        dy3, (h3, z3, a3, o3), x2, sp["ffn2_pre_g"], w2_in, w2_out4, sp["ffn2_post_g"], "ffn2")
    dmixed, dg_mpost = post_bwd(dx2, mixed, sp["mix_post_g"], 1.0, "mix_post_bwd")
    dcat = mm_nt_b(dmixed, w_mo4, "mix_dcat")
    dw_mo = mm_tn(cat, dmixed, True, False, 4, "mix_dwout").reshape(8, 128, D_MODEL)
    dy_ssm, dw_glu, db_glu = glu_bwd(dcat, y_ssm, lg, w_glu2, "glu_bwd")
    du, da, dbh, dch, dd = ssm_bwd(dy_ssm, proj, states, bh, ch, apow, dskip, "ssm_bwd")
    dqkv = None
    for d in DILATIONS:
        dqkv = attn_bwd(proj, dcat, o_att, lse, dqkv, d, f"attn_bwd_d{d}")
    dproj = jnp.concatenate([dqkv.reshape(6, T, 256), du], axis=0)
    dw_mi = mm_tn(h2, dproj, False, True, 8, "mix_dwin")
    (dx1, dg_mpre), (s_wmi, s_wglu, s_wmo) = dh_pre_bwd(
        dproj, w_mi, x1, sp["mix_pre_g"], dx2, "mix_dh",
        comm=([dw_mi, dw_glu.astype(BF16).reshape(8, 64, 512), dw_mo], [True, True, True]))
    dx0, dg_f1pre, dg_f1post, (s_w1in, s_w1out) = _ffn_bwd(
        dx1, (h1, z1, a1, o1), x, sp["ffn1_pre_g"], w1_in, w1_out4, sp["ffn1_post_g"], "ffn1")

    da4 = da.reshape(2, 2, HALF_STATES)
    d_ar, d_ai = da4[:, 0].reshape(32, N_STATE), da4[:, 1].reshape(32, N_STATE)
    d_bbr = _diag_blocks(dbh[:, :, :HALF_STATES], 16, N_STATE).transpose(0, 2, 1)
    d_bbi = _diag_blocks(dbh[:, :, HALF_STATES:], 16, N_STATE).transpose(0, 2, 1)
    _, disc_vjp = jax.vjp(_discretise, sp["a_re"], sp["a_im"], sp["log_dt"], sp["b_re"], sp["b_im"])
    g_are, g_aim, g_ldt, g_bre, g_bim = disc_vjp((d_ar, d_ai, d_bbr, d_bbi))
    g_cre = _diag_blocks(dch[:, :HALF_STATES], N_STATE, 16).transpose(0, 2, 1)
    g_cim = -_diag_blocks(dch[:, HALF_STATES:], N_STATE, 16).transpose(0, 2, 1)
    small = {
        "ffn1_pre_g": dg_f1pre, "ffn1_post_g": dg_f1post, "mix_pre_g": dg_mpre, "a_re": g_are, "a_im": g_aim,
        "log_dt": g_ldt, "b_re": g_bre, "b_im": g_bim, "c_re": g_cre, "c_im": g_cim, "d_skip": dd.reshape(1, 512),
        "b_glu": db_glu, "mix_post_g": dg_mpost, "ffn2_pre_g": dg_f2pre, "ffn2_post_g": dg_f2post,
    }
    (small_slots,) = exchange([_pack([small[n] for n in SMALL])], [False], "exchange_small")
    big_slots = {"ffn1_w_in": s_w1in, "ffn1_w_out": s_w1out, "w_mix_in": s_wmi, "w_glu": s_wglu, "w_mix_out": s_wmo,
                 "ffn2_w_in": s_w2in, "ffn2_w_out": s_w2out}
    return sq, dx0, big_slots, small_slots


def kernel(x, ffn1_pre_g, ffn1_w_in, ffn1_w_out, ffn1_post_g, mix_pre_g, w_mix_in, a_re, a_im, log_dt, b_re, b_im, c_re, c_im, d_skip, w_glu, b_glu, w_mix_out, mix_post_g, ffn2_pre_g, ffn2_w_in, ffn2_w_out, ffn2_post_g, loss_target, m_ffn1_pre_g, m_ffn1_w_in, m_ffn1_w_out, m_ffn1_post_g, m_mix_pre_g, m_w_mix_in, m_a_re, m_a_im, m_log_dt, m_b_re, m_b_im, m_c_re, m_c_im, m_d_skip, m_w_glu, m_b_glu, m_w_mix_out, m_mix_post_g, m_ffn2_pre_g, m_ffn2_w_in, m_ffn2_w_out, m_ffn2_post_g, v_ffn1_pre_g, v_ffn1_w_in, v_ffn1_w_out, v_ffn1_post_g, v_mix_pre_g, v_w_mix_in, v_a_re, v_a_im, v_log_dt, v_b_re, v_b_im, v_c_re, v_c_im, v_d_skip, v_w_glu, v_b_glu, v_w_mix_out, v_mix_post_g, v_ffn2_pre_g, v_ffn2_w_in, v_ffn2_w_out, v_ffn2_post_g):
    args = dict(locals())
    w = {n: args[n][0] for n in WEIGHTS}
    m = {n: args["m_" + n][0] for n in WEIGHTS}
    v = {n: args["v_" + n][0] for n in WEIGHTS}

    wb = {n: w[n].astype(BF16) for n in BIG}
    sp = {n: w[n] for n in SMALL}
    for n in ("ffn1_pre_g", "ffn1_post_g", "mix_pre_g", "mix_post_g", "ffn2_pre_g", "ffn2_post_g", "b_glu", "d_skip"):
        sp[n] = w[n].reshape(1, -1)

    sq, grad_x, big_slots, small_slots = local_step(x[0], loss_target[0], sp, wb)
    loss = lax.psum(0.5 / D_MODEL * jnp.sum(sq), ("x", "y", "c"))

    outs = {}
    for n in BIG:
        shp = w[n].shape
        r2 = lambda t: t.reshape(-1, shp[-1])
        res = adamw(r2(w[n]), r2(m[n]), r2(v[n]), big_slots[n].reshape(N_DEV, -1, shp[-1]), f"adamw_{n}")
        outs[n] = [t.reshape((1,) + shp) for t in res]
    res = adamw(_pack([w[n] for n in SMALL]), _pack([m[n] for n in SMALL]), _pack([v[n] for n in SMALL]),
                small_slots, "adamw_small")
    shapes = [(1,) + w[n].shape for n in SMALL]
    unpacked = [_unpack(t, shapes) for t in res]
    for j, n in enumerate(SMALL):
        outs[n] = [unpacked[k][j] for k in range(4)]

    result = [loss, grad_x[None]]
    for k in range(4):
        result += [outs[n][k] for n in WEIGHTS]
    return tuple(result)
```

```python
import functools
import math

import jax
import jax.numpy as jnp
from jax import lax
from jax.experimental import pallas as pl
from jax.experimental.pallas import tpu as pltpu

F32, BF16 = jnp.float32, jnp.bfloat16
SDS = jax.ShapeDtypeStruct

D_MODEL = 1024
N_DEV = 8
HEAD_DIM = 64
PAIR_W = 128
QBLK = 128
DILATIONS = (1, 4, 16)
N_STATE = 64
HALF_STATES = 1024
NORM_EPS = 1e-6
NEG = -1e30
VMEM_LIMIT = 56 * 1024 * 1024
ADAM_LR, ADAM_B1, ADAM_B2, ADAM_EPS, ADAM_WD, ADAM_STEP = 1e-3, 0.9, 0.999, 1e-8, 0.01, 10
SCAN_TM = 256
SCAN_CW = 256


def _params(n_grid):
    return pltpu.CompilerParams(dimension_semantics=("arbitrary",) * n_grid, vmem_limit_bytes=VMEM_LIMIT)


def _dot(a, b):
    return jnp.dot(a, b, preferred_element_type=F32)


def _dot_nt(a, b):
    return lax.dot_general(a, b, (((1,), (1,)), ((), ())), preferred_element_type=F32)


def _dot_tn(a, b):
    return lax.dot_general(a, b, (((0,), (0,)), ((), ())), preferred_element_type=F32)


def _sigmoid(v):
    return 1.0 / (1.0 + jnp.exp(-v))


def _resident(shape):
    return pl.BlockSpec(shape, lambda i: (0,) * len(shape), pipeline_mode=pl.Buffered(1))


ROW_SPLIT = 2


def _exchange_phase(ins, outs, scatter, sems, start):
    send_sems, recv_sems, loc_sems = sems
    x, y, c = lax.axis_index("x"), lax.axis_index("y"), lax.axis_index("c")
    me = 4 * x + 2 * y + c
    own_copies, sends, arrivals = [], [], []
    for i in range(len(ins)):
        own = ins[i].at[me] if scatter[i] else ins[i]
        own_copies.append(pltpu.make_async_copy(own, outs[i].at[me], loc_sems.at[i]))
        for k in range(1, N_DEV):
            px = 1 - x if k & 4 else x
            py = 1 - y if k & 2 else y
            pc = 1 - c if k & 1 else c
            peer = 4 * px + 2 * py + pc
            src = ins[i].at[peer] if scatter[i] else ins[i]
            common = dict(src_ref=src, send_sem=send_sems.at[i, k - 1], recv_sem=recv_sems.at[i, k - 1],
                          device_id=(px, py, pc), device_id_type=pl.DeviceIdType.MESH)
            sends.append(pltpu.make_async_remote_copy(dst_ref=outs[i].at[me], **common))
            if not start:
                arrivals.append(pltpu.make_async_remote_copy(dst_ref=outs[i].at[peer], **common))
    if start:
        for cp in own_copies + sends:
            cp.start()
    else:
        for cp in arrivals:
            cp.wait_recv()
        for cp in sends:
            cp.wait_send()
        for cp in own_copies:
            cp.wait()


def _comm_shapes(arrs, scatter):
    n = len(arrs)
    out_shapes = [SDS(a.shape if scatter[i] else (N_DEV,) + a.shape, a.dtype) for i, a in enumerate(arrs)]
    sems = [pltpu.SemaphoreType.DMA((n, N_DEV - 1)), pltpu.SemaphoreType.DMA((n, N_DEV - 1)),
            pltpu.SemaphoreType.DMA((n,))]
    return out_shapes, sems


def exchange(arrs, scatter, name):
    n = len(arrs)
    out_shapes, sems = _comm_shapes(arrs, scatter)

    def body(*refs):
        ins, outs, sem_refs = refs[:n], refs[n:2 * n], refs[2 * n:]
        _exchange_phase(ins, outs, scatter, sem_refs, True)
        _exchange_phase(ins, outs, scatter, sem_refs, False)

    anyspec = pl.BlockSpec(memory_space=pl.ANY)
    return pl.pallas_call(
        body, in_specs=[anyspec] * n, out_specs=[anyspec] * n, out_shape=out_shapes, scratch_shapes=sems,
        compiler_params=pltpu.CompilerParams(has_side_effects=True), name=name)(*arrs)


def _call(body, *, grid, in_specs, out_specs, out_shape, args, name, scratch_shapes=(), comm=None):
    n_grid, scratch_shapes = len(grid), list(scratch_shapes)
    if comm is None:
        outs = pl.pallas_call(body, grid=grid, in_specs=in_specs, out_specs=out_specs, out_shape=out_shape,
                              scratch_shapes=scratch_shapes, compiler_params=_params(n_grid), name=name)(*args)
        return outs, []
    arrs, scatter = comm
    nc, n_in, n_out, n_sc = len(arrs), len(in_specs), len(out_specs), len(scratch_shapes)
    comm_shapes, sems = _comm_shapes(arrs, scatter)

    def wrapped(*refs):
        ins, cins = refs[:n_in], refs[n_in:n_in + nc]
        o0 = n_in + nc
        outs, couts = refs[o0:o0 + n_out], refs[o0 + n_out:o0 + n_out + nc]
        s0 = o0 + n_out + nc
        scratch, sem_refs = refs[s0:s0 + n_sc], refs[s0 + n_sc:]
        first = functools.reduce(jnp.logical_and, [pl.program_id(k) == 0 for k in range(n_grid)])
        last = functools.reduce(jnp.logical_and, [pl.program_id(k) == grid[k] - 1 for k in range(n_grid)])

        @pl.when(first)
        def _():
            _exchange_phase(cins, couts, scatter, sem_refs, True)

        body(*ins, *outs, *scratch)

        @pl.when(last)
        def _():
            _exchange_phase(cins, couts, scatter, sem_refs, False)

    anyspec = pl.BlockSpec(memory_space=pl.ANY)
    res = pl.pallas_call(
        wrapped, grid=grid, in_specs=list(in_specs) + [anyspec] * nc, out_specs=list(out_specs) + [anyspec] * nc,
        out_shape=list(out_shape) + comm_shapes, scratch_shapes=scratch_shapes + sems,
        compiler_params=pltpu.CompilerParams(dimension_semantics=("arbitrary",) * n_grid,
                                             vmem_limit_bytes=VMEM_LIMIT, has_side_effects=True),
        name=name)(*args, *arrs)
    return res[:n_out], res[n_out:]


def rms_fwd(x, g, name):
    T, D = x.shape
    tm = 512

    def body(x_ref, g_ref, h_ref):
        xv = x_ref[...]
        r = lax.rsqrt(jnp.mean(xv * xv, axis=-1, keepdims=True) + NORM_EPS)
        h_ref[...] = (xv * r * g_ref[...]).astype(BF16)

    return pl.pallas_call(
        body, grid=(T // tm,),
        in_specs=[pl.BlockSpec((tm, D), lambda i: (i, 0)), pl.BlockSpec((1, D), lambda i: (0, 0))],
        out_specs=pl.BlockSpec((tm, D), lambda i: (i, 0)),
        out_shape=SDS((T, D), BF16), compiler_params=_params(1), name=name)(x, g)


def ffn_in(h, w, name, comm=None):
    T, D = h.shape
    F = w.shape[2]
    tm = 512

    def body(h_ref, wg_ref, wu_ref, z_ref, a_ref):
        hv = h_ref[...]
        zg = _dot(hv, wg_ref[...])
        zu = _dot(hv, wu_ref[...])
        z_ref[0] = zg.astype(BF16)
        z_ref[1] = zu.astype(BF16)
        a_ref[...] = (zg * _sigmoid(zg) * zu).astype(BF16)

    return _call(
        body, grid=(4, T // tm),
        in_specs=[pl.BlockSpec((tm, D), lambda j, i: (i, 0)),
                  pl.BlockSpec((None, D, F), lambda j, i: (j, 0, 0)),
                  pl.BlockSpec((None, D, F), lambda j, i: (j + 4, 0, 0))],
        out_specs=[pl.BlockSpec((2, None, tm, F), lambda j, i: (0, j, i, 0)),
                   pl.BlockSpec((None, tm, F), lambda j, i: (j, i, 0))],
        out_shape=[SDS((2, 4, T, F), BF16), SDS((4, T, F), BF16)],
        args=(h, w, w), name=name, comm=comm)


def mm_nn_b(a, w, nb, out_dtype, name):
    T, K = a.shape
    N = w.shape[2]
    tm = 512

    def body(a_ref, w_ref, o_ref):
        av = a_ref[...]
        for b in range(nb):
            o_ref[b] = _dot(av, w_ref[b]).astype(out_dtype)

    return pl.pallas_call(
        body, grid=(T // tm,),
        in_specs=[pl.BlockSpec((tm, K), lambda i: (i, 0)), _resident((nb, K, N))],
        out_specs=pl.BlockSpec((nb, tm, N), lambda i: (0, i, 0)),
        out_shape=SDS((nb, T, N), out_dtype), compiler_params=_params(1), name=name)(a, w)


def mm_acc_norm(a, w, xres, g, scale, name, comm=None):
    nb, T, K = a.shape
    D = w.shape[2]
    tm = 512

    rc = tm // ROW_SPLIT

    def body(a_ref, w_ref, x_ref, g_ref, o_ref, y_ref):
        accs = []
        for c in range(ROW_SPLIT):
            rows = pl.ds(c * rc, rc)
            o = _dot(a_ref[0, rows, :].astype(BF16), w_ref[0])
            for b in range(1, nb):
                o += _dot(a_ref[b, rows, :].astype(BF16), w_ref[b])
            accs.append(o)
        for c, o in enumerate(accs):
            rows = pl.ds(c * rc, rc)
            r = lax.rsqrt(jnp.mean(o * o, axis=-1, keepdims=True) + NORM_EPS)
            o_ref[rows, :] = o
            y_ref[rows, :] = x_ref[rows, :] + scale * (o * r * g_ref[...])

    return _call(
        body, grid=(T // tm,),
        in_specs=[pl.BlockSpec((nb, tm, K), lambda i: (0, i, 0)), _resident((nb, K, D)),
                  pl.BlockSpec((tm, D), lambda i: (i, 0)), pl.BlockSpec((1, D), lambda i: (0, 0))],
        out_specs=[pl.BlockSpec((tm, D), lambda i: (i, 0)), pl.BlockSpec((tm, D), lambda i: (i, 0))],
        out_shape=[SDS((T, D), F32), SDS((T, D), F32)],
        args=(a, w, xres, g), name=name, comm=comm)


def loss_head(y, tgt, name):
    T, D = y.shape
    tm = 512

    def body(y_ref, t_ref, dy_ref, sq_ref):
        @pl.when(pl.program_id(0) == 0)
        def _():
            sq_ref[...] = jnp.zeros_like(sq_ref)

        e = y_ref[...] - t_ref[...]
        dy_ref[...] = e * (1.0 / D)
        sq_ref[...] += jnp.sum(e * e, axis=0, keepdims=True)

    return pl.pallas_call(
        body, grid=(T // tm,),
        in_specs=[pl.BlockSpec((tm, D), lambda i: (i, 0)), pl.BlockSpec((tm, D), lambda i: (i, 0))],
        out_specs=[pl.BlockSpec((tm, D), lambda i: (i, 0)), pl.BlockSpec((1, D), lambda i: (0, 0))],
        out_shape=[SDS((T, D), F32), SDS((1, D), F32)], compiler_params=_params(1), name=name)(y, tgt)


def post_bwd(dy, o, g, scale, name):
    T, D = o.shape
    tm = 512

    def body(dy_ref, o_ref, g_ref, do_ref, dg_ref):
        @pl.when(pl.program_id(0) == 0)
        def _():
            dg_ref[...] = jnp.zeros_like(dg_ref)

        ov = o_ref[...]
        r = scale * dy_ref[...]
        rstd = lax.rsqrt(jnp.mean(ov * ov, axis=-1, keepdims=True) + NORM_EPS)
        oh = ov * rstd
        dg_ref[...] += jnp.sum(r * oh, axis=0, keepdims=True)
        rg = r * g_ref[...]
        do_ref[...] = (rstd * (rg - oh * jnp.mean(rg * oh, axis=-1, keepdims=True))).astype(BF16)

    return pl.pallas_call(
        body, grid=(T // tm,),
        in_specs=[pl.BlockSpec((tm, D), lambda i: (i, 0)), pl.BlockSpec((tm, D), lambda i: (i, 0)),
                  pl.BlockSpec((1, D), lambda i: (0, 0))],
        out_specs=[pl.BlockSpec((tm, D), lambda i: (i, 0)), pl.BlockSpec((1, D), lambda i: (0, 0))],
        out_shape=[SDS((T, D), BF16), SDS((1, D), F32)], compiler_params=_params(1), name=name)(dy, o, g)


def mm_nt_b(gr, w, name):
    T, N = gr.shape
    nb, K, _ = w.shape
    tm = 512

    def body(g_ref, w_ref, o_ref):
        o_ref[...] = _dot_nt(g_ref[...], w_ref[...])

    return pl.pallas_call(
        body, grid=(nb, T // tm),
        in_specs=[pl.BlockSpec((tm, N), lambda b, i: (i, 0)), pl.BlockSpec((None, K, N), lambda b, i: (b, 0, 0))],
        out_specs=pl.BlockSpec((None, tm, K), lambda b, i: (b, i, 0)),
        out_shape=SDS((nb, T, K), F32), compiler_params=_params(2), name=name)(gr, w)


def ffn_dact(do, w_out, z, name):
    T, D = do.shape
    nb, F, _ = w_out.shape
    tm = 512

    def body(g_ref, w_ref, z_ref, dz_ref):
        da = _dot_nt(g_ref[...], w_ref[...])
        zg = z_ref[0].astype(F32)
        zu = z_ref[1].astype(F32)
        sg = _sigmoid(zg)
        dz_ref[0] = (da * zu * (sg * (1.0 + zg * (1.0 - sg)))).astype(BF16)
        dz_ref[1] = (da * zg * sg).astype(BF16)

    return pl.pallas_call(
        body, grid=(nb, T // tm),
        in_specs=[pl.BlockSpec((tm, D), lambda b, i: (i, 0)), pl.BlockSpec((None, F, D), lambda b, i: (b, 0, 0)),
                  pl.BlockSpec((2, None, tm, F), lambda b, i: (0, b, i, 0))],
        out_specs=pl.BlockSpec((2, None, tm, F), lambda b, i: (0, b, i, 0)),
        out_shape=SDS((2, nb, T, F), BF16), compiler_params=_params(2), name=name)(do, w_out, z)


def mm_tn(a, g, a_batched, g_batched, nb, name):
    T = a.shape[-2]
    K, N = a.shape[-1], g.shape[-1]
    tk = 2048
    nk = T // tk

    def body(a_ref, g_ref, o_ref, acc):
        k = pl.program_id(1)

        @pl.when(k == 0)
        def _():
            acc[...] = jnp.zeros_like(acc)

        acc[...] += _dot_tn(a_ref[...].astype(BF16), g_ref[...].astype(BF16))

        @pl.when(k == nk - 1)
        def _():
            o_ref[...] = acc[...].astype(BF16)

    a_spec = (pl.BlockSpec((None, tk, K), lambda b, k: (b, k, 0)) if a_batched
              else pl.BlockSpec((tk, K), lambda b, k: (k, 0)))
    g_spec = (pl.BlockSpec((None, tk, N), lambda b, k: (b, k, 0)) if g_batched
              else pl.BlockSpec((tk, N), lambda b, k: (k, 0)))
    return pl.pallas_call(
        body, grid=(nb, nk), in_specs=[a_spec, g_spec],
        out_specs=pl.BlockSpec((None, K, N), lambda b, k: (b, 0, 0)),
        out_shape=SDS((nb, K, N), BF16), scratch_shapes=[pltpu.VMEM((K, N), F32)],
        compiler_params=_params(2), name=name)(a, g)


def dh_pre_bwd(dz, w, x, g, dyres, name, comm=None):
    nb, T, F = dz.shape
    D = w.shape[1]
    tm = 512

    rc = tm // ROW_SPLIT

    def body(dz_ref, w_ref, x_ref, g_ref, dy_ref, dx_ref, dg_ref):
        @pl.when(pl.program_id(0) == 0)
        def _():
            dg_ref[...] = jnp.zeros_like(dg_ref)

        accs = []
        for c in range(ROW_SPLIT):
            rows = pl.ds(c * rc, rc)
            dh = _dot_nt(dz_ref[0, rows, :].astype(BF16), w_ref[0])
            for b in range(1, nb):
                dh += _dot_nt(dz_ref[b, rows, :].astype(BF16), w_ref[b])
            accs.append(dh)
        for c, dh in enumerate(accs):
            rows = pl.ds(c * rc, rc)
            xv = x_ref[rows, :]
            rstd = lax.rsqrt(jnp.mean(xv * xv, axis=-1, keepdims=True) + NORM_EPS)
            xh = xv * rstd
            dg_ref[...] += jnp.sum(dh * xh, axis=0, keepdims=True)
            dhg = dh * g_ref[...]
            dx_ref[rows, :] = dy_ref[rows, :] + rstd * (dhg - xh * jnp.mean(dhg * xh, axis=-1, keepdims=True))

    return _call(
        body, grid=(T // tm,),
        in_specs=[pl.BlockSpec((nb, tm, F), lambda i: (0, i, 0)), _resident((nb, D, F)),
                  pl.BlockSpec((tm, D), lambda i: (i, 0)), pl.BlockSpec((1, D), lambda i: (0, 0)),
                  pl.BlockSpec((tm, D), lambda i: (i, 0))],
        out_specs=[pl.BlockSpec((tm, D), lambda i: (i, 0)), pl.BlockSpec((1, D), lambda i: (0, 0))],
        out_shape=[SDS((T, D), F32), SDS((1, D), F32)],
        args=(dz, w, x, g, dyres), name=name, comm=comm)


ATTN_GROUP = {1: 4, 4: 1, 16: 1}
ATTN_UNROLL = 4


def _attn_masks():
    qi = lax.broadcasted_iota(jnp.int32, (QBLK, QBLK), 0)
    kj = lax.broadcasted_iota(jnp.int32, (QBLK, QBLK), 1)
    cur_ok = kj <= qi
    prev_ok = kj >= qi
    dcur = (qi - kj).astype(F32)
    return cur_ok, prev_ok, dcur, dcur + float(QBLK)


def _head_slopes(p, d):
    out = []
    for hq in range(2):
        v = [float(d) * 2.0 ** -(2 * q + hq + 1) for q in range(4)]
        out.append(jnp.where(p == 0, v[0], jnp.where(p == 1, v[1], jnp.where(p == 2, v[2], v[3]))))
    return out


def _rows(start, d):
    return pl.ds(start, QBLK, stride=d) if d > 1 else pl.ds(start, QBLK)


def _pair_spec(rows, part, blk):
    return pl.BlockSpec((None, rows, PAIR_W), lambda p, n: (2 * part + p // 2, blk(n), p % 2))


def _for_query_blocks(d, groups, several):
    blocks = [(g, r) for g in range(groups) for r in range(d)]
    if len(blocks) <= 2 * ATTN_UNROLL:
        for s in range(0, len(blocks), ATTN_UNROLL):
            several(blocks[s:s + ATTN_UNROLL])
    else:
        def some(i, carry):
            several([(0, i * ATTN_UNROLL + j) for j in range(ATTN_UNROLL)])
            return carry

        lax.fori_loop(0, d // ATTN_UNROLL, some, 0)


def attn_fwd(proj, d, name):
    T = proj.shape[1]
    sb, groups = QBLK * d, ATTN_GROUP[d]
    rb = sb * groups
    nblk = T // rb

    def body(q_ref, kc_ref, kp_ref, vc_ref, vp_ref, o_ref, l_ref):
        p, n = pl.program_id(0), pl.program_id(1)
        cur_ok, prev_ok, dcur, dprev = _attn_masks()
        first_ok = jnp.logical_and(prev_ok, n > 0)
        lane_head = lax.broadcasted_iota(jnp.int32, (QBLK, PAIR_W), 1) // HEAD_DIM
        slopes = _head_slopes(p, d)

        def several(blocks):
            work = []
            for g, r in blocks:
                rows = _rows(g * sb + r, d)
                q = q_ref[rows, :]
                kc, vc = kc_ref[rows, :].astype(BF16), vc_ref[rows, :].astype(BF16)
                if g == 0:
                    prow, pok = _rows(r, d), first_ok
                    kp, vp = kp_ref[prow, :].astype(BF16), vp_ref[prow, :].astype(BF16)
                else:
                    prow, pok = _rows((g - 1) * sb + r, d), prev_ok
                    kp, vp = kc_ref[prow, :].astype(BF16), vc_ref[prow, :].astype(BF16)
                for hq in range(2):
                    qm = jnp.where(lane_head == hq, q, 0.0).astype(BF16)
                    work.append([rows, hq, pok, vc, vp, _dot_nt(qm, kc), _dot_nt(qm, kp)])
            for w in work:
                _, hq, pok, _, _, sc, sp = w
                sc = jnp.where(cur_ok, sc * 0.125 - slopes[hq] * dcur, NEG)
                sp = jnp.where(pok, sp * 0.125 - slopes[hq] * dprev, NEG)
                m = jnp.maximum(jnp.max(sc, axis=1, keepdims=True), jnp.max(sp, axis=1, keepdims=True))
                pc = jnp.exp(sc - m)
                pp = jnp.exp(sp - m)
                den = jnp.sum(pc, axis=1, keepdims=True) + jnp.sum(pp, axis=1, keepdims=True)
                w[5:] = [pc.astype(BF16), pp.astype(BF16), 1.0 / den, m + jnp.log(den)]
            for i in range(0, len(work), 2):
                o_acc = jnp.zeros((QBLK, PAIR_W), F32)
                l_acc = jnp.zeros((QBLK, PAIR_W), F32)
                for rows, hq, _, vc, vp, pc, pp, inv, lse in work[i:i + 2]:
                    hm = lane_head == hq
                    o_acc = jnp.where(hm, (_dot(pc, vc) + _dot(pp, vp)) * inv, o_acc)
                    l_acc = jnp.where(hm, lse, l_acc)
                o_ref[rows, :] = o_acc
                l_ref[rows, :] = l_acc

        _for_query_blocks(d, groups, several)

    cur = lambda part: _pair_spec(rb, part, lambda n: n)
    prv = lambda part: _pair_spec(sb, part, lambda n: jnp.maximum(n * groups - 1, 0))
    return pl.pallas_call(
        body, grid=(4, nblk), in_specs=[cur(0), cur(1), prv(1), cur(2), prv(2)], out_specs=[cur(0), cur(0)],
        out_shape=[SDS((2, T, 2 * PAIR_W), F32), SDS((2, T, 2 * PAIR_W), F32)],
        compiler_params=_params(2), name=name)(proj, proj, proj, proj, proj)


def attn_merge(os_, ls_, name):
    _, T, HW = os_[0].shape
    tm = 512

    def body(o1, o2, o3, l1, l2, l3, o_ref, l_ref):
        a, b, c = l1[...], l2[...], l3[...]
        m = jnp.maximum(jnp.maximum(a, b), c)
        ea, eb, ec = jnp.exp(a - m), jnp.exp(b - m), jnp.exp(c - m)
        s = ea + eb + ec
        o_ref[...] = (ea * o1[...] + eb * o2[...] + ec * o3[...]) * (1.0 / s)
        l_ref[...] = m + jnp.log(s)

    spec = pl.BlockSpec((None, tm, HW), lambda h, i: (h, i, 0))
    return pl.pallas_call(
        body, grid=(2, T // tm), in_specs=[spec] * 6, out_specs=[spec, spec],
        out_shape=[SDS((2, T, HW), F32), SDS((2, T, HW), F32)],
        compiler_params=_params(2), name=name)(*os_, *ls_)


def attn_bwd(proj, dcat, o, lse, acc, d, name):
    T = proj.shape[1]
    sb, groups = QBLK * d, ATTN_GROUP[d]
    rb = sb * groups
    nblk = T // rb
    has_acc = acc is not None

    def body(*refs):
        (qc_ref, qn_ref, kc_ref, kp_ref, vc_ref, vp_ref, dc_ref, dn_ref, oc_ref, on_ref, lc_ref, ln_ref) = refs[:12]
        acc_ref = refs[12] if has_acc else None
        out_ref = refs[-1]
        p, n = pl.program_id(0), pl.program_id(1)
        cur_ok, prev_ok, dcur, dprev = _attn_masks()
        first_ok = jnp.logical_and(prev_ok, n > 0)
        last_ok = jnp.logical_and(prev_ok, n < nblk - 1)
        lane_head = lax.broadcasted_iota(jnp.int32, (QBLK, PAIR_W), 1) // HEAD_DIM
        slopes = _head_slopes(p, d)

        def one(g, r):
            rows = _rows(g * sb + r, d)
            q_c, do_c, o_c, l_c = qc_ref[rows, :], dc_ref[rows, :], oc_ref[rows, :], lc_ref[rows, :]
            k_c, v_c = kc_ref[rows, :].astype(BF16), vc_ref[rows, :].astype(BF16)
            if g == 0:
                prow, pok_c = _rows(r, d), first_ok
                k_p, v_p = kp_ref[prow, :].astype(BF16), vp_ref[prow, :].astype(BF16)
            else:
                prow, pok_c = _rows((g - 1) * sb + r, d), prev_ok
                k_p, v_p = kc_ref[prow, :].astype(BF16), vc_ref[prow, :].astype(BF16)
            if g == groups - 1:
                nrow, pok_n = _rows(r, d), last_ok
                q_n, do_n, o_n, l_n = qn_ref[nrow, :], dn_ref[nrow, :], on_ref[nrow, :], ln_ref[nrow, :]
            else:
                nrow, pok_n = _rows((g + 1) * sb + r, d), prev_ok
                q_n, do_n, o_n, l_n = qc_ref[nrow, :], dc_ref[nrow, :], oc_ref[nrow, :], lc_ref[nrow, :]
            heads = []
            for hq in range(2):
                hm = lane_head == hq
                qm_c = jnp.where(hm, q_c, 0.0).astype(BF16)
                qm_n = jnp.where(hm, q_n, 0.0).astype(BF16)
                dom_c = jnp.where(hm, do_c, 0.0)
                dom_n = jnp.where(hm, do_n, 0.0)
                dd_c = jnp.sum(dom_c * o_c, axis=1, keepdims=True)
                dd_n = jnp.sum(dom_n * o_n, axis=1, keepdims=True)
                ls_c = jnp.max(jnp.where(hm, l_c, NEG), axis=1, keepdims=True)
                ls_n = jnp.max(jnp.where(hm, l_n, NEG), axis=1, keepdims=True)
                dob_c, dob_n = dom_c.astype(BF16), dom_n.astype(BF16)
                mm = [(_dot_nt(qm_c, k_c), _dot_nt(dob_c, v_c)), (_dot_nt(qm_c, k_p), _dot_nt(dob_c, v_p)),
                      (_dot_nt(qm_n, k_c), _dot_nt(dob_n, v_c))]
                heads.append(dict(hq=hq, qm_c=qm_c, qm_n=qm_n, dob_c=dob_c, dob_n=dob_n, mm=mm,
                                  dd=(dd_c, dd_c, dd_n), ls=(ls_c, ls_c, ls_n)))
            return dict(rows=rows, k_c=k_c, k_p=k_p, heads=heads, oks=(cur_ok, pok_c, pok_n))

        def several(blocks):
            work = [one(g, r) for g, r in blocks]
            for w in work:
                for h in w["heads"]:
                    slope, dist = slopes[h["hq"]], (dcur, dprev, dprev)
                    h["pr"], h["ds"] = [], []
                    for j in range(3):
                        s = jnp.where(w["oks"][j], h["mm"][j][0] * 0.125 - slope * dist[j], NEG)
                        pr = jnp.exp(s - h["ls"][j])
                        h["pr"].append(pr.astype(BF16))
                        h["ds"].append((pr * (h["mm"][j][1] - h["dd"][j])).astype(BF16))
            for w in work:
                dq = jnp.zeros((QBLK, PAIR_W), F32)
                dk = jnp.zeros((QBLK, PAIR_W), F32)
                dv = jnp.zeros((QBLK, PAIR_W), F32)
                for h in w["heads"]:
                    ds, pr = h["ds"], h["pr"]
                    dq_h = _dot(ds[0], w["k_c"]) + _dot(ds[1], w["k_p"])
                    dk += (_dot_tn(ds[0], h["qm_c"]) + _dot_tn(ds[2], h["qm_n"])) * 0.125
                    dv += _dot_tn(pr[0], h["dob_c"]) + _dot_tn(pr[2], h["dob_n"])
                    dq = jnp.where(lane_head == h["hq"], dq_h * 0.125, dq)
                for part, val in enumerate((dq, dk, dv)):
                    if has_acc:
                        val = val + acc_ref.at[part][w["rows"], :]
                    out_ref.at[part][w["rows"], :] = val

        _for_query_blocks(d, groups, several)

    cur = lambda part: _pair_spec(rb, part, lambda n: n)
    prv = lambda part: _pair_spec(sb, part, lambda n: jnp.maximum(n * groups - 1, 0))
    nxt = lambda part: _pair_spec(sb, part, lambda n: jnp.minimum((n + 1) * groups, T // sb - 1))
    full = pl.BlockSpec((3, None, rb, PAIR_W), lambda p, n: (0, p // 2, n, p % 2))
    in_specs = [cur(0), nxt(0), cur(1), prv(1), cur(2), prv(2), cur(0), nxt(0), cur(0), nxt(0), cur(0), nxt(0)]
    args = [proj, proj, proj, proj, proj, proj, dcat, dcat, o, o, lse, lse]
    if has_acc:
        in_specs.append(full)
        args.append(acc)
    return pl.pallas_call(
        body, grid=(4, nblk), in_specs=in_specs, out_specs=full,
        out_shape=SDS((3, 2, T, 2 * PAIR_W), F32), compiler_params=_params(2), name=name)(*args)


def _scan_chunk(buf, apow_ref, c0, reverse, ptab=None, carry=None):
    tm = buf.shape[0]
    cre = pl.ds(c0, SCAN_CW)
    cim = pl.ds(pl.multiple_of(c0 + HALF_STATES, 128), SCAN_CW)
    re, im = buf[:, cre], buf[:, cim]
    row = lax.broadcasted_iota(jnp.int32, (tm, SCAN_CW), 0)
    sgn = -1.0 if reverse else 1.0
    k, s = 1, 0
    while k < tm:
        ar = apow_ref[pl.ds(s, 1), cre]
        ai = sgn * apow_ref[pl.ds(s, 1), cim]
        if reverse:
            ok, shift = row < tm - k, tm - k
        else:
            ok, shift = row >= k, k
        sre = jnp.where(ok, pltpu.roll(re, shift, 0), 0.0)
        sim = jnp.where(ok, pltpu.roll(im, shift, 0), 0.0)
        re, im = re + ar * sre - ai * sim, im + ar * sim + ai * sre
        k, s = 2 * k, s + 1
    if ptab is not None:
        pr, pi = ptab[:, cre], ptab[:, cim]
        cr, ci = carry[:, cre], carry[:, cim]
        re, im = re + pr * cr - pi * ci, im + pr * ci + pi * cr
    buf[:, cre] = re
    buf[:, cim] = im


def _scan_tile(buf, apow_ref, reverse, ptab=None, carry=None):
    per_half = HALF_STATES // SCAN_CW

    def chunk(j, _):
        c0 = pl.multiple_of((j // per_half) * 2 * HALF_STATES + (j % per_half) * SCAN_CW, 128)
        _scan_chunk(buf, apow_ref, c0, reverse, ptab, carry)
        return 0

    lax.fori_loop(0, 2 * per_half, chunk, 0)


def _init_ptab(ptab, apow_ref, reverse):
    tm, width = ptab.shape
    row = lax.broadcasted_iota(jnp.int32, (tm, width), 0)
    col = lax.broadcasted_iota(jnp.int32, (1, width), 1)
    a = apow_ref[pl.ds(0, 1), :]
    if reverse:
        is_im = (col // HALF_STATES) % 2 == 1
        a = jnp.where(is_im, -a, a)
    seed = (tm - 1) if reverse else 0
    ptab[...] = jnp.where(row == seed, a, 0.0)
    _scan_tile(ptab, apow_ref, reverse)


def ssm_fwd(proj, bh, ch, apow, dskip, name, comm=None):
    _, T, C = proj.shape
    tm = SCAN_TM
    SW = 4 * HALF_STATES

    def body(u_ref, bh_ref, ch_ref, apow_ref, dsk_ref, y_ref, s_ref, ptab, carry):
        @pl.when(pl.program_id(0) == 0)
        def _():
            _init_ptab(ptab, apow_ref, False)
            carry[...] = jnp.zeros_like(carry)

        for h in range(2):
            s_ref[:, pl.ds(h * 2 * HALF_STATES, 2 * HALF_STATES)] = _dot(u_ref[h].astype(BF16), bh_ref[h])
        _scan_tile(s_ref, apow_ref, False, ptab, carry)
        carry[...] = s_ref[pl.ds(tm - 1, 1), :]
        for h in range(2):
            sv = s_ref[:, pl.ds(h * 2 * HALF_STATES, 2 * HALF_STATES)].astype(BF16)
            y_ref[h] = _dot(sv, ch_ref[h]) + dsk_ref[h] * u_ref[h]

    return _call(
        body, grid=(T // tm,),
        in_specs=[pl.BlockSpec((2, tm, C), lambda i: (3, i, 0)),
                  pl.BlockSpec((2, C, 2 * HALF_STATES), lambda i: (0, 0, 0)),
                  pl.BlockSpec((2, 2 * HALF_STATES, C), lambda i: (0, 0, 0)),
                  pl.BlockSpec((8, SW), lambda i: (0, 0)),
                  pl.BlockSpec((2, 1, C), lambda i: (0, 0, 0))],
        out_specs=[pl.BlockSpec((2, tm, C), lambda i: (0, i, 0)), pl.BlockSpec((tm, SW), lambda i: (i, 0))],
        out_shape=[SDS((2, T, C), F32), SDS((T, SW), F32)],
        scratch_shapes=[pltpu.VMEM((tm, SW), F32), pltpu.VMEM((1, SW), F32)],
        args=(proj, bh, ch, apow, dskip), name=name, comm=comm)


def ssm_bwd(dy, proj, st, bh, ch, apow, dskip, name):
    _, T, C = proj.shape
    tm = SCAN_TM
    nt = T // tm
    SW = 4 * HALF_STATES
    HS2 = 2 * HALF_STATES

    def body(dy_ref, u_ref, s_ref, sp_ref, bh_ref, ch_ref, apow_ref, dsk_ref,
             du_ref, da_ref, dbh_ref, dch_ref, dd_ref, lam, ptab, carry):
        i = pl.program_id(0)

        @pl.when(i == 0)
        def _():
            _init_ptab(ptab, apow_ref, True)
            carry[...] = jnp.zeros_like(carry)
            da_ref[...] = jnp.zeros_like(da_ref)
            dbh_ref[...] = jnp.zeros_like(dbh_ref)
            dch_ref[...] = jnp.zeros_like(dch_ref)
            dd_ref[...] = jnp.zeros_like(dd_ref)

        for h in range(2):
            lam[:, pl.ds(h * HS2, HS2)] = _dot_nt(dy_ref[h].astype(BF16), ch_ref[h])
        _scan_tile(lam, apow_ref, True, ptab, carry)
        carry[...] = lam[pl.ds(0, 1), :]

        first = i == nt - 1
        per_half = HALF_STATES // SCAN_CW

        def chunk(j, _):
            c0 = pl.multiple_of((j // per_half) * HS2 + (j % per_half) * SCAN_CW, 128)
            cre, cim = pl.ds(c0, SCAN_CW), pl.ds(pl.multiple_of(c0 + HALF_STATES, 128), SCAN_CW)
            row = lax.broadcasted_iota(jnp.int32, (tm, SCAN_CW), 0)
            pre = jnp.where(first, 0.0, sp_ref[pl.ds(7, 1), cre])
            pim = jnp.where(first, 0.0, sp_ref[pl.ds(7, 1), cim])
            spr = jnp.where(row == 0, pre, pltpu.roll(s_ref[:, cre], 1, 0))
            spi = jnp.where(row == 0, pim, pltpu.roll(s_ref[:, cim], 1, 0))
            lr, li = lam[:, cre], lam[:, cim]
            da_ref[:, cre] += jnp.sum(lr * spr + li * spi, axis=0, keepdims=True)
            da_ref[:, cim] += jnp.sum(li * spr - lr * spi, axis=0, keepdims=True)
            return 0

        lax.fori_loop(0, 2 * per_half, chunk, 0)

        for h in range(2):
            lb = lam[:, pl.ds(h * HS2, HS2)].astype(BF16)
            dyv, uv = dy_ref[h], u_ref[h]
            du_ref[h] = _dot_nt(lb, bh_ref[h]) + dsk_ref[h] * dyv
            dbh_ref[h] += _dot_tn(uv.astype(BF16), lb)
            dch_ref[h] += _dot_tn(s_ref[:, pl.ds(h * HS2, HS2)].astype(BF16), dyv.astype(BF16))
            dd_ref[h] += jnp.sum(dyv * uv, axis=0, keepdims=True)

    rev = lambda i: nt - 1 - i
    return pl.pallas_call(
        body, grid=(nt,),
        in_specs=[pl.BlockSpec((2, tm, C), lambda i: (0, rev(i), 0)),
                  pl.BlockSpec((2, tm, C), lambda i: (3, rev(i), 0)),
                  pl.BlockSpec((tm, SW), lambda i: (rev(i), 0)),
                  pl.BlockSpec((8, SW), lambda i: (jnp.maximum(rev(i) * (tm // 8) - 1, 0), 0)),
                  pl.BlockSpec((2, C, HS2), lambda i: (0, 0, 0)),
                  pl.BlockSpec((2, HS2, C), lambda i: (0, 0, 0)),
                  pl.BlockSpec((8, SW), lambda i: (0, 0)),
                  pl.BlockSpec((2, 1, C), lambda i: (0, 0, 0))],
        out_specs=[pl.BlockSpec((2, tm, C), lambda i: (0, rev(i), 0)),
                   pl.BlockSpec((1, SW), lambda i: (0, 0)),
                   pl.BlockSpec((2, C, HS2), lambda i: (0, 0, 0)),
                   pl.BlockSpec((2, HS2, C), lambda i: (0, 0, 0)),
                   pl.BlockSpec((2, 1, C), lambda i: (0, 0, 0))],
        out_shape=[SDS((2, T, C), F32), SDS((1, SW), F32), SDS((2, C, HS2), F32), SDS((2, HS2, C), F32),
                   SDS((2, 1, C), F32)],
        scratch_shapes=[pltpu.VMEM((tm, SW), F32), pltpu.VMEM((tm, SW), F32), pltpu.VMEM((1, SW), F32)],
        compiler_params=_params(1), name=name)(dy, proj, st, st, bh, ch, apow, dskip)


_GELU_C = math.sqrt(2.0 / math.pi)


def _gelu(x):
    t = jnp.tanh(_GELU_C * (x + 0.044715 * x * x * x))
    return 0.5 * x * (1.0 + t), t


def glu_fwd(y, w, b, name):
    _, T, C = y.shape
    tm = 512

    def body(y_ref, w_ref, b_ref, o_ref, lg_ref):
        y0, _ = _gelu(y_ref[0])
        y1, _ = _gelu(y_ref[1])
        lg = _dot(y0.astype(BF16), w_ref[0]) + _dot(y1.astype(BF16), w_ref[1]) + b_ref[...]
        sg = _sigmoid(lg)
        o_ref[0] = y0 * sg[:, :C]
        o_ref[1] = y1 * sg[:, C:]
        lg_ref[0] = lg[:, :C]
        lg_ref[1] = lg[:, C:]

    return pl.pallas_call(
        body, grid=(T // tm,),
        in_specs=[pl.BlockSpec((2, tm, C), lambda i: (0, i, 0)), pl.BlockSpec((2, C, 2 * C), lambda i: (0, 0, 0)),
                  pl.BlockSpec((1, 2 * C), lambda i: (0, 0))],
        out_specs=[pl.BlockSpec((2, tm, C), lambda i: (0, i, 0)), pl.BlockSpec((2, tm, C), lambda i: (0, i, 0))],
        out_shape=[SDS((2, T, C), F32), SDS((2, T, C), F32)], compiler_params=_params(1), name=name)(y, w, b)


def glu_bwd(dcat, y, lg, w, name):
    _, T, C = y.shape
    tm = 512

    def body(d_ref, y_ref, lg_ref, w_ref, dy_ref, dw_ref, db_ref):
        @pl.when(pl.program_id(0) == 0)
        def _():
            dw_ref[...] = jnp.zeros_like(dw_ref)
            db_ref[...] = jnp.zeros_like(db_ref)

        y2, th, sg, dlg = [], [], [], []
        for h in range(2):
            yy, tt = _gelu(y_ref[h])
            ss = _sigmoid(lg_ref[h])
            y2.append(yy)
            th.append(tt)
            sg.append(ss)
            dlg.append(d_ref[h] * yy * ss * (1.0 - ss))
        dl = jnp.concatenate(dlg, axis=1)
        dlb = dl.astype(BF16)
        db_ref[...] += jnp.sum(dl, axis=0, keepdims=True)
        for h in range(2):
            dy2 = d_ref[h] * sg[h] + _dot_nt(dlb, w_ref[h])
            yv = y_ref[h]
            dgelu = 0.5 * (1.0 + th[h]) + 0.5 * yv * (1.0 - th[h] * th[h]) * _GELU_C * (1.0 + 3 * 0.044715 * yv * yv)
            dy_ref[h] = dy2 * dgelu
            dw_ref[h] += _dot_tn(y2[h].astype(BF16), dlb)

    return pl.pallas_call(
        body, grid=(T // tm,),
        in_specs=[pl.BlockSpec((2, tm, C), lambda i: (1, i, 0)), pl.BlockSpec((2, tm, C), lambda i: (0, i, 0)),
                  pl.BlockSpec((2, tm, C), lambda i: (0, i, 0)), pl.BlockSpec((2, C, 2 * C), lambda i: (0, 0, 0))],
        out_specs=[pl.BlockSpec((2, tm, C), lambda i: (0, i, 0)), pl.BlockSpec((2, C, 2 * C), lambda i: (0, 0, 0)),
                   pl.BlockSpec((1, 2 * C), lambda i: (0, 0))],
        out_shape=[SDS((2, T, C), F32), SDS((2, C, 2 * C), F32), SDS((1, 2 * C), F32)],
        compiler_params=_params(1), name=name)(dcat, y, lg, w)


def adamw(w, m, v, slots, name):
    R, C = w.shape
    tr = R
    for cand in (512, 256, 128, 64, 32, 16, 8):
        if R % cand == 0 and cand * C * 4 <= 2 * 1024 * 1024:
            tr = cand
            break
    c1 = 1.0 / (1.0 - ADAM_B1 ** ADAM_STEP)
    c2 = 1.0 / (1.0 - ADAM_B2 ** ADAM_STEP)

    def body(w_ref, m_ref, v_ref, s_ref, g_ref, d_ref, nm_ref, nv_ref):
        g = s_ref[0].astype(F32)
        for j in range(1, N_DEV):
            g = g + s_ref[j].astype(F32)
        nm = ADAM_B1 * m_ref[...] + (1.0 - ADAM_B1) * g
        nv = ADAM_B2 * v_ref[...] + (1.0 - ADAM_B2) * (g * g)
        g_ref[...] = g
        nm_ref[...] = nm
        nv_ref[...] = nv
        d_ref[...] = -ADAM_LR * ((nm * c1) / (jnp.sqrt(nv * c2) + ADAM_EPS) + ADAM_WD * w_ref[...])

    spec = pl.BlockSpec((tr, C), lambda i: (i, 0))
    return pl.pallas_call(
        body, grid=(R // tr,),
        in_specs=[spec, spec, spec, pl.BlockSpec((N_DEV, tr, C), lambda i: (0, i, 0))],
        out_specs=[spec] * 4, out_shape=[SDS((R, C), F32)] * 4, compiler_params=_params(1), name=name)(w, m, v, slots)


def _discretise(a_re, a_im, log_dt, b_re, b_im):
    dt = jnp.exp(log_dt)[:, None]
    e = jnp.exp(dt * a_re)
    ar, ai = e * jnp.cos(dt * a_im), e * jnp.sin(dt * a_im)
    den = a_re * a_re + a_im * a_im
    nr, ni = ar - 1.0, ai
    wr = (nr * a_re + ni * a_im) / den
    wi = (ni * a_re - nr * a_im) / den
    bbr = wr[..., None] * b_re - wi[..., None] * b_im
    bbi = wr[..., None] * b_im + wi[..., None] * b_re
    return ar, ai, bbr, bbi


def _block_diag(t):
    eye = jnp.eye(16, dtype=t.dtype)
    r, c = t.shape[1], t.shape[2]
    return jnp.einsum("hgrc,gk->hgrkc", t.reshape(2, 16, r, c), eye).reshape(2, 16 * r, 16 * c)


def _diag_blocks(m, r, c):
    eye = jnp.eye(16, dtype=m.dtype)
    return jnp.einsum("hgrkc,gk->hgrc", m.reshape(2, 16, r, 16, c), eye).reshape(32, r, c)


def _state_vec(re, im):
    return jnp.stack([re.reshape(2, HALF_STATES), im.reshape(2, HALF_STATES)], axis=1).reshape(-1)


BIG = ("ffn1_w_in", "ffn1_w_out", "w_mix_in", "w_glu", "w_mix_out", "ffn2_w_in", "ffn2_w_out")
WEIGHTS = ("ffn1_pre_g", "ffn1_w_in", "ffn1_w_out", "ffn1_post_g", "mix_pre_g", "w_mix_in", "a_re", "a_im", "log_dt",
           "b_re", "b_im", "c_re", "c_im", "d_skip", "w_glu", "b_glu", "w_mix_out", "mix_post_g", "ffn2_pre_g",
           "ffn2_w_in", "ffn2_w_out", "ffn2_post_g")
SMALL = tuple(n for n in WEIGHTS if n not in BIG)
PACK_COLS = 1024


def _pack(parts):
    flat = jnp.concatenate([p.reshape(-1) for p in parts])
    rows = -(-flat.shape[0] // (8 * PACK_COLS)) * 8
    return jnp.pad(flat, (0, rows * PACK_COLS - flat.shape[0])).reshape(rows, PACK_COLS)


def _unpack(packed, shapes):
    flat, out, off = packed.reshape(-1), [], 0
    for s in shapes:
        n = math.prod(s)
        out.append(flat[off:off + n].reshape(s))
        off += n
    return out


def _gather(names, wb):
    return [wb[n] for n in names], [False] * len(names)


def _ffn_bwd(dy, saved, x, pre_g, w_in, w_out4, post_g, tag):
    h, z, a, o = saved
    T = x.shape[0]
    do, dg_post = post_bwd(dy, o, post_g, 0.5, f"{tag}_post_bwd")
    dz = ffn_dact(do, w_out4, z, f"{tag}_dact")
    dz8 = dz.reshape(8, T, dz.shape[-1])
    dw_out = mm_tn(a, do, True, False, 4, f"{tag}_dwout").reshape(8, -1, D_MODEL)
    dw_in = mm_tn(h, dz8, False, True, 8, f"{tag}_dwin")
    (dx, dg_pre), slots = dh_pre_bwd(dz8, w_in, x, pre_g, dy, f"{tag}_dh", comm=([dw_in, dw_out], [True, True]))
    return dx, dg_pre, dg_post, slots


def local_step(x, tgt, sp, wb):
    T = x.shape[0]
    ar, ai, bbr, bbi = _discretise(sp["a_re"], sp["a_im"], sp["log_dt"], sp["b_re"], sp["b_im"])
    pr, pi, rows = ar, ai, []
    for _ in range(8):
        rows.append(_state_vec(pr, pi))
        pr, pi = pr * pr - pi * pi, 2.0 * pr * pi
    apow = jnp.stack(rows)
    bh = jnp.concatenate([_block_diag(bbr.transpose(0, 2, 1)), _block_diag(bbi.transpose(0, 2, 1))], axis=2)
    ch = jnp.concatenate([_block_diag(sp["c_re"].transpose(0, 2, 1)), _block_diag(-sp["c_im"].transpose(0, 2, 1))], axis=1)
    bh, ch = bh.astype(BF16), ch.astype(BF16)
    dskip = sp["d_skip"].reshape(2, 1, 256)

    (w1_in,) = exchange(*_gather(["ffn1_w_in"], wb), "gather_w1in")
    h1 = rms_fwd(x, sp["ffn1_pre_g"], "ffn1_rms")
    (z1, a1), (w1_out, w_mi) = ffn_in(h1, w1_in, "ffn1_in", comm=_gather(["ffn1_w_out", "w_mix_in"], wb))
    w1_out4 = w1_out.reshape(4, -1, D_MODEL)
    (o1, x1), (w_glu, w_mo, w2_out) = mm_acc_norm(
        a1, w1_out4, x, sp["ffn1_post_g"], 0.5, "ffn1_out", comm=_gather(["w_glu", "w_mix_out", "ffn2_w_out"], wb))
    w_glu2, w_mo4, w2_out4 = w_glu.reshape(2, 256, 512), w_mo.reshape(4, 256, D_MODEL), w2_out.reshape(4, -1, D_MODEL)
    h2 = rms_fwd(x1, sp["mix_pre_g"], "mix_rms")
    proj = mm_nn_b(h2, w_mi, 8, F32, "mix_proj")
    (y_ssm, states), (w2_in,) = ssm_fwd(proj, bh, ch, apow, dskip, "ssm_fwd", comm=_gather(["ffn2_w_in"], wb))
    os_, ls_ = [], []
    for d in DILATIONS:
        o_d, l_d = attn_fwd(proj, d, f"attn_fwd_d{d}")
        os_.append(o_d)
        ls_.append(l_d)
    o_att, lse = attn_merge(os_, ls_, "attn_merge")
    o_ssm, lg = glu_fwd(y_ssm, w_glu2, sp["b_glu"], "glu_fwd")
    cat = jnp.concatenate([o_att, o_ssm], axis=0)
    (mixed, x2), _ = mm_acc_norm(cat, w_mo4, x1, sp["mix_post_g"], 1.0, "mix_out")
    h3 = rms_fwd(x2, sp["ffn2_pre_g"], "ffn2_rms")
    (z3, a3), _ = ffn_in(h3, w2_in, "ffn2_in")
    (o3, x3), _ = mm_acc_norm(a3, w2_out4, x2, sp["ffn2_post_g"], 0.5, "ffn2_out")
    dy3, sq = loss_head(x3, tgt, "loss_head")

    dx2, dg_f2pre, dg_f2post, (s_w2in, s_w2out) = _ffn_bwd(
        dy3, (h3, z3, a3, o3), x2, sp["ffn2_pre_g"], w2_in, w2_out4, sp["ffn2_post_g"], "ffn2")
    dmixed, dg_mpost = post_bwd(dx2, mixed, sp["mix_post_g"], 1.0, "mix_post_bwd")
    dcat = mm_nt_b(dmixed, w_mo4, "mix_dcat")
    dw_mo = mm_tn(cat, dmixed, True, False, 4, "mix_dwout").reshape(8, 128, D_MODEL)
    dy_ssm, dw_glu, db_glu = glu_bwd(dcat, y_ssm, lg, w_glu2, "glu_bwd")
    du, da, dbh, dch, dd = ssm_bwd(dy_ssm, proj, states, bh, ch, apow, dskip, "ssm_bwd")
    dqkv = None
    for d in DILATIONS:
        dqkv = attn_bwd(proj, dcat, o_att, lse, dqkv, d, f"attn_bwd_d{d}")
    dproj = jnp.concatenate([dqkv.reshape(6, T, 256), du], axis=0)
    dw_mi = mm_tn(h2, dproj, False, True, 8, "mix_dwin")
    (dx1, dg_mpre), (s_wmi, s_wglu, s_wmo) = dh_pre_bwd(
        dproj, w_mi, x1, sp["mix_pre_g"], dx2, "mix_dh",
        comm=([dw_mi, dw_glu.astype(BF16).reshape(8, 64, 512), dw_mo], [True, True, True]))
    dx0, dg_f1pre, dg_f1post, (s_w1in, s_w1out) = _ffn_bwd(
        dx1, (h1, z1, a1, o1), x, sp["ffn1_pre_g"], w1_in, w1_out4, sp["ffn1_post_g"], "ffn1")

    da4 = da.reshape(2, 2, HALF_STATES)
    d_ar, d_ai = da4[:, 0].reshape(32, N_STATE), da4[:, 1].reshape(32, N_STATE)
    d_bbr = _diag_blocks(dbh[:, :, :HALF_STATES], 16, N_STATE).transpose(0, 2, 1)
    d_bbi = _diag_blocks(dbh[:, :, HALF_STATES:], 16, N_STATE).transpose(0, 2, 1)
    _, disc_vjp = jax.vjp(_discretise, sp["a_re"], sp["a_im"], sp["log_dt"], sp["b_re"], sp["b_im"])
    g_are, g_aim, g_ldt, g_bre, g_bim = disc_vjp((d_ar, d_ai, d_bbr, d_bbi))
    g_cre = _diag_blocks(dch[:, :HALF_STATES], N_STATE, 16).transpose(0, 2, 1)
    g_cim = -_diag_blocks(dch[:, HALF_STATES:], N_STATE, 16).transpose(0, 2, 1)
    small = {
        "ffn1_pre_g": dg_f1pre, "ffn1_post_g": dg_f1post, "mix_pre_g": dg_mpre, "a_re": g_are, "a_im": g_aim,
        "log_dt": g_ldt, "b_re": g_bre, "b_im": g_bim, "c_re": g_cre, "c_im": g_cim, "d_skip": dd.reshape(1, 512),
        "b_glu": db_glu, "mix_post_g": dg_mpost, "ffn2_pre_g": dg_f2pre, "ffn2_post_g": dg_f2post,
    }
    (small_slots,) = exchange([_pack([small[n] for n in SMALL])], [False], "exchange_small")
    big_slots = {"ffn1_w_in": s_w1in, "ffn1_w_out": s_w1out, "w_mix_in": s_wmi, "w_glu": s_wglu, "w_mix_out": s_wmo,
                 "ffn2_w_in": s_w2in, "ffn2_w_out": s_w2out}
    return sq, dx0, big_slots, small_slots


def kernel(x, ffn1_pre_g, ffn1_w_in, ffn1_w_out, ffn1_post_g, mix_pre_g, w_mix_in, a_re, a_im, log_dt, b_re, b_im, c_re, c_im, d_skip, w_glu, b_glu, w_mix_out, mix_post_g, ffn2_pre_g, ffn2_w_in, ffn2_w_out, ffn2_post_g, loss_target, m_ffn1_pre_g, m_ffn1_w_in, m_ffn1_w_out, m_ffn1_post_g, m_mix_pre_g, m_w_mix_in, m_a_re, m_a_im, m_log_dt, m_b_re, m_b_im, m_c_re, m_c_im, m_d_skip, m_w_glu, m_b_glu, m_w_mix_out, m_mix_post_g, m_ffn2_pre_g, m_ffn2_w_in, m_ffn2_w_out, m_ffn2_post_g, v_ffn1_pre_g, v_ffn1_w_in, v_ffn1_w_out, v_ffn1_post_g, v_mix_pre_g, v_w_mix_in, v_a_re, v_a_im, v_log_dt, v_b_re, v_b_im, v_c_re, v_c_im, v_d_skip, v_w_glu, v_b_glu, v_w_mix_out, v_mix_post_g, v_ffn2_pre_g, v_ffn2_w_in, v_ffn2_w_out, v_ffn2_post_g):
    args = dict(locals())
    w = {n: args[n][0] for n in WEIGHTS}
    m = {n: args["m_" + n][0] for n in WEIGHTS}
    v = {n: args["v_" + n][0] for n in WEIGHTS}

    wb = {n: w[n].astype(BF16) for n in BIG}
    sp = {n: w[n] for n in SMALL}
    for n in ("ffn1_pre_g", "ffn1_post_g", "mix_pre_g", "mix_post_g", "ffn2_pre_g", "ffn2_post_g", "b_glu", "d_skip"):
        sp[n] = w[n].reshape(1, -1)

    sq, grad_x, big_slots, small_slots = local_step(x[0], loss_target[0], sp, wb)
    loss = lax.psum(0.5 / D_MODEL * jnp.sum(sq), ("x", "y", "c"))

    outs = {}
    for n in BIG:
        shp = w[n].shape
        r2 = lambda t: t.reshape(-1, shp[-1])
        res = adamw(r2(w[n]), r2(m[n]), r2(v[n]), big_slots[n].reshape(N_DEV, -1, shp[-1]), f"adamw_{n}")
        outs[n] = [t.reshape((1,) + shp) for t in res]
    res = adamw(_pack([w[n] for n in SMALL]), _pack([m[n] for n in SMALL]), _pack([v[n] for n in SMALL]),
                small_slots, "adamw_small")
    shapes = [(1,) + w[n].shape for n in SMALL]
    unpacked = [_unpack(t, shapes) for t in res]
    for j, n in enumerate(SMALL):
        outs[n] = [unpacked[k][j] for k in range(4)]

    result = [loss, grad_x[None]]
    for k in range(4):
        result += [outs[n][k] for n in WEIGHTS]
    return tuple(result)
```

```python
import functools
import math

import jax
import jax.numpy as jnp
from jax import lax
from jax.experimental import pallas as pl
from jax.experimental.pallas import tpu as pltpu

F32, BF16 = jnp.float32, jnp.bfloat16
SDS = jax.ShapeDtypeStruct

D_MODEL = 1024
N_DEV = 8
HEAD_DIM = 64
PAIR_W = 128
QBLK = 128
DILATIONS = (1, 4, 16)
N_STATE = 64
HALF_STATES = 1024
NORM_EPS = 1e-6
NEG = -1e30
VMEM_LIMIT = 56 * 1024 * 1024
ADAM_LR, ADAM_B1, ADAM_B2, ADAM_EPS, ADAM_WD, ADAM_STEP = 1e-3, 0.9, 0.999, 1e-8, 0.01, 10
SCAN_TM = 256
SCAN_CW = 256


def _params(n_grid):
    return pltpu.CompilerParams(dimension_semantics=("arbitrary",) * n_grid, vmem_limit_bytes=VMEM_LIMIT)


def _dot(a, b):
    return jnp.dot(a, b, preferred_element_type=F32)


def _dot_nt(a, b):
    return lax.dot_general(a, b, (((1,), (1,)), ((), ())), preferred_element_type=F32)


def _dot_tn(a, b):
    return lax.dot_general(a, b, (((0,), (0,)), ((), ())), preferred_element_type=F32)


def _sigmoid(v):
    return 1.0 / (1.0 + jnp.exp(-v))


def _resident(shape):
    return pl.BlockSpec(shape, lambda i: (0,) * len(shape), pipeline_mode=pl.Buffered(1))


ROW_SPLIT = 2


def _exchange_phase(ins, outs, scatter, sems, start):
    send_sems, recv_sems, loc_sems = sems
    x, y, c = lax.axis_index("x"), lax.axis_index("y"), lax.axis_index("c")
    me = 4 * x + 2 * y + c
    own_copies, sends, arrivals = [], [], []
    for i in range(len(ins)):
        own = ins[i].at[me] if scatter[i] else ins[i]
        own_copies.append(pltpu.make_async_copy(own, outs[i].at[me], loc_sems.at[i]))
        for k in range(1, N_DEV):
            px = 1 - x if k & 4 else x
            py = 1 - y if k & 2 else y
            pc = 1 - c if k & 1 else c
            peer = 4 * px + 2 * py + pc
            src = ins[i].at[peer] if scatter[i] else ins[i]
            common = dict(src_ref=src, send_sem=send_sems.at[i, k - 1], recv_sem=recv_sems.at[i, k - 1],
                          device_id=(px, py, pc), device_id_type=pl.DeviceIdType.MESH)
            sends.append(pltpu.make_async_remote_copy(dst_ref=outs[i].at[me], **common))
            if not start:
                arrivals.append(pltpu.make_async_remote_copy(dst_ref=outs[i].at[peer], **common))
    if start:
        for cp in own_copies + sends:
            cp.start()
    else:
        for cp in arrivals:
            cp.wait_recv()
        for cp in sends:
            cp.wait_send()
        for cp in own_copies:
            cp.wait()


def _comm_shapes(arrs, scatter):
    n = len(arrs)
    out_shapes = [SDS(a.shape if scatter[i] else (N_DEV,) + a.shape, a.dtype) for i, a in enumerate(arrs)]
    sems = [pltpu.SemaphoreType.DMA((n, N_DEV - 1)), pltpu.SemaphoreType.DMA((n, N_DEV - 1)),
            pltpu.SemaphoreType.DMA((n,))]
    return out_shapes, sems


def exchange(arrs, scatter, name):
    n = len(arrs)
    out_shapes, sems = _comm_shapes(arrs, scatter)

    def body(*refs):
        ins, outs, sem_refs = refs[:n], refs[n:2 * n], refs[2 * n:]
        _exchange_phase(ins, outs, scatter, sem_refs, True)
        _exchange_phase(ins, outs, scatter, sem_refs, False)

    anyspec = pl.BlockSpec(memory_space=pl.ANY)
    return pl.pallas_call(
        body, in_specs=[anyspec] * n, out_specs=[anyspec] * n, out_shape=out_shapes, scratch_shapes=sems,
        compiler_params=pltpu.CompilerParams(has_side_effects=True), name=name)(*arrs)


def _call(body, *, grid, in_specs, out_specs, out_shape, args, name, scratch_shapes=(), comm=None):
    n_grid, scratch_shapes = len(grid), list(scratch_shapes)
    if comm is None:
        outs = pl.pallas_call(body, grid=grid, in_specs=in_specs, out_specs=out_specs, out_shape=out_shape,
                              scratch_shapes=scratch_shapes, compiler_params=_params(n_grid), name=name)(*args)
        return outs, []
    arrs, scatter = comm
    nc, n_in, n_out, n_sc = len(arrs), len(in_specs), len(out_specs), len(scratch_shapes)
    comm_shapes, sems = _comm_shapes(arrs, scatter)

    def wrapped(*refs):
        ins, cins = refs[:n_in], refs[n_in:n_in + nc]
        o0 = n_in + nc
        outs, couts = refs[o0:o0 + n_out], refs[o0 + n_out:o0 + n_out + nc]
        s0 = o0 + n_out + nc
        scratch, sem_refs = refs[s0:s0 + n_sc], refs[s0 + n_sc:]
        first = functools.reduce(jnp.logical_and, [pl.program_id(k) == 0 for k in range(n_grid)])
        last = functools.reduce(jnp.logical_and, [pl.program_id(k) == grid[k] - 1 for k in range(n_grid)])

        @pl.when(first)
        def _():
            _exchange_phase(cins, couts, scatter, sem_refs, True)

        body(*ins, *outs, *scratch)

        @pl.when(last)
        def _():
            _exchange_phase(cins, couts, scatter, sem_refs, False)

    anyspec = pl.BlockSpec(memory_space=pl.ANY)
    res = pl.pallas_call(
        wrapped, grid=grid, in_specs=list(in_specs) + [anyspec] * nc, out_specs=list(out_specs) + [anyspec] * nc,
        out_shape=list(out_shape) + comm_shapes, scratch_shapes=scratch_shapes + sems,
        compiler_params=pltpu.CompilerParams(dimension_semantics=("arbitrary",) * n_grid,
                                             vmem_limit_bytes=VMEM_LIMIT, has_side_effects=True),
        name=name)(*args, *arrs)
    return res[:n_out], res[n_out:]


def rms_fwd(x, g, name):
    T, D = x.shape
    tm = 512

    def body(x_ref, g_ref, h_ref):
        xv = x_ref[...]
        r = lax.rsqrt(jnp.mean(xv * xv, axis=-1, keepdims=True) + NORM_EPS)
        h_ref[...] = (xv * r * g_ref[...]).astype(BF16)

    return pl.pallas_call(
        body, grid=(T // tm,),
        in_specs=[pl.BlockSpec((tm, D), lambda i: (i, 0)), pl.BlockSpec((1, D), lambda i: (0, 0))],
        out_specs=pl.BlockSpec((tm, D), lambda i: (i, 0)),
        out_shape=SDS((T, D), BF16), compiler_params=_params(1), name=name)(x, g)


def ffn_in(h, w, name, comm=None):
    T, D = h.shape
    F = w.shape[2]
    tm = 512

    def body(h_ref, wg_ref, wu_ref, z_ref, a_ref):
        hv = h_ref[...]
        zg = _dot(hv, wg_ref[...])
        zu = _dot(hv, wu_ref[...])
        z_ref[0] = zg.astype(BF16)
        z_ref[1] = zu.astype(BF16)
        a_ref[...] = (zg * _sigmoid(zg) * zu).astype(BF16)

    return _call(
        body, grid=(4, T // tm),
        in_specs=[pl.BlockSpec((tm, D), lambda j, i: (i, 0)),
                  pl.BlockSpec((None, D, F), lambda j, i: (j, 0, 0)),
                  pl.BlockSpec((None, D, F), lambda j, i: (j + 4, 0, 0))],
        out_specs=[pl.BlockSpec((2, None, tm, F), lambda j, i: (0, j, i, 0)),
                   pl.BlockSpec((None, tm, F), lambda j, i: (j, i, 0))],
        out_shape=[SDS((2, 4, T, F), BF16), SDS((4, T, F), BF16)],
        args=(h, w, w), name=name, comm=comm)


def mm_nn_b(a, w, nb, out_dtype, name):
    T, K = a.shape
    N = w.shape[2]
    tm = 512

    def body(a_ref, w_ref, o_ref):
        av = a_ref[...]
        for b in range(nb):
            o_ref[b] = _dot(av, w_ref[b]).astype(out_dtype)

    return pl.pallas_call(
        body, grid=(T // tm,),
        in_specs=[pl.BlockSpec((tm, K), lambda i: (i, 0)), _resident((nb, K, N))],
        out_specs=pl.BlockSpec((nb, tm, N), lambda i: (0, i, 0)),
        out_shape=SDS((nb, T, N), out_dtype), compiler_params=_params(1), name=name)(a, w)


def mm_acc_norm(a, w, xres, g, scale, name, comm=None):
    nb, T, K = a.shape
    D = w.shape[2]
    tm = 512

    rc = tm // ROW_SPLIT

    def body(a_ref, w_ref, x_ref, g_ref, o_ref, y_ref):
        accs = []
        for c in range(ROW_SPLIT):
            rows = pl.ds(c * rc, rc)
            o = _dot(a_ref[0, rows, :].astype(BF16), w_ref[0])
            for b in range(1, nb):
                o += _dot(a_ref[b, rows, :].astype(BF16), w_ref[b])
            accs.append(o)
        for c, o in enumerate(accs):
            rows = pl.ds(c * rc, rc)
            r = lax.rsqrt(jnp.mean(o * o, axis=-1, keepdims=True) + NORM_EPS)
            o_ref[rows, :] = o
            y_ref[rows, :] = x_ref[rows, :] + scale * (o * r * g_ref[...])

    return _call(
        body, grid=(T // tm,),
        in_specs=[pl.BlockSpec((nb, tm, K), lambda i: (0, i, 0)), _resident((nb, K, D)),
                  pl.BlockSpec((tm, D), lambda i: (i, 0)), pl.BlockSpec((1, D), lambda i: (0, 0))],
        out_specs=[pl.BlockSpec((tm, D), lambda i: (i, 0)), pl.BlockSpec((tm, D), lambda i: (i, 0))],
        out_shape=[SDS((T, D), F32), SDS((T, D), F32)],
        args=(a, w, xres, g), name=name, comm=comm)


def loss_head(y, tgt, name):
    T, D = y.shape
    tm = 512

    def body(y_ref, t_ref, dy_ref, sq_ref):
        @pl.when(pl.program_id(0) == 0)
        def _():
            sq_ref[...] = jnp.zeros_like(sq_ref)

        e = y_ref[...] - t_ref[...]
        dy_ref[...] = e * (1.0 / D)
        sq_ref[...] += jnp.sum(e * e, axis=0, keepdims=True)

    return pl.pallas_call(
        body, grid=(T // tm,),
        in_specs=[pl.BlockSpec((tm, D), lambda i: (i, 0)), pl.BlockSpec((tm, D), lambda i: (i, 0))],
        out_specs=[pl.BlockSpec((tm, D), lambda i: (i, 0)), pl.BlockSpec((1, D), lambda i: (0, 0))],
        out_shape=[SDS((T, D), F32), SDS((1, D), F32)], compiler_params=_params(1), name=name)(y, tgt)


def post_bwd(dy, o, g, scale, name):
    T, D = o.shape
    tm = 512

    def body(dy_ref, o_ref, g_ref, do_ref, dg_ref):
        @pl.when(pl.program_id(0) == 0)
        def _():
            dg_ref[...] = jnp.zeros_like(dg_ref)

        ov = o_ref[...]
        r = scale * dy_ref[...]
        rstd = lax.rsqrt(jnp.mean(ov * ov, axis=-1, keepdims=True) + NORM_EPS)
        oh = ov * rstd
        dg_ref[...] += jnp.sum(r * oh, axis=0, keepdims=True)
        rg = r * g_ref[...]
        do_ref[...] = (rstd * (rg - oh * jnp.mean(rg * oh, axis=-1, keepdims=True))).astype(BF16)

    return pl.pallas_call(
        body, grid=(T // tm,),
        in_specs=[pl.BlockSpec((tm, D), lambda i: (i, 0)), pl.BlockSpec((tm, D), lambda i: (i, 0)),
                  pl.BlockSpec((1, D), lambda i: (0, 0))],
        out_specs=[pl.BlockSpec((tm, D), lambda i: (i, 0)), pl.BlockSpec((1, D), lambda i: (0, 0))],
        out_shape=[SDS((T, D), BF16), SDS((1, D), F32)], compiler_params=_params(1), name=name)(dy, o, g)


def mm_nt_b(gr, w, name):
    T, N = gr.shape
    nb, K, _ = w.shape
    tm = 512

    def body(g_ref, w_ref, o_ref):
        o_ref[...] = _dot_nt(g_ref[...], w_ref[...])

    return pl.pallas_call(
        body, grid=(nb, T // tm),
        in_specs=[pl.BlockSpec((tm, N), lambda b, i: (i, 0)), pl.BlockSpec((None, K, N), lambda b, i: (b, 0, 0))],
        out_specs=pl.BlockSpec((None, tm, K), lambda b, i: (b, i, 0)),
        out_shape=SDS((nb, T, K), F32), compiler_params=_params(2), name=name)(gr, w)


def ffn_dact(do, w_out, z, name):
    T, D = do.shape
    nb, F, _ = w_out.shape
    tm = 512

    def body(g_ref, w_ref, z_ref, dz_ref):
        da = _dot_nt(g_ref[...], w_ref[...])
        zg = z_ref[0].astype(F32)
        zu = z_ref[1].astype(F32)
        sg = _sigmoid(zg)
        dz_ref[0] = (da * zu * (sg * (1.0 + zg * (1.0 - sg)))).astype(BF16)
        dz_ref[1] = (da * zg * sg).astype(BF16)

    return pl.pallas_call(
        body, grid=(nb, T // tm),
        in_specs=[pl.BlockSpec((tm, D), lambda b, i: (i, 0)), pl.BlockSpec((None, F, D), lambda b, i: (b, 0, 0)),
                  pl.BlockSpec((2, None, tm, F), lambda b, i: (0, b, i, 0))],
        out_specs=pl.BlockSpec((2, None, tm, F), lambda b, i: (0, b, i, 0)),
        out_shape=SDS((2, nb, T, F), BF16), compiler_params=_params(2), name=name)(do, w_out, z)


def mm_tn(a, g, a_batched, g_batched, nb, name, comm=None):
    T = a.shape[-2]
    K, N = a.shape[-1], g.shape[-1]
    tk = 2048
    nk = T // tk

    def body(a_ref, g_ref, o_ref, acc):
        k = pl.program_id(1)

        @pl.when(k == 0)
        def _():
            acc[...] = jnp.zeros_like(acc)

        acc[...] += _dot_tn(a_ref[...].astype(BF16), g_ref[...].astype(BF16))

        @pl.when(k == nk - 1)
        def _():
            o_ref[...] = acc[...].astype(BF16)

    a_spec = (pl.BlockSpec((None, tk, K), lambda b, k: (b, k, 0)) if a_batched
              else pl.BlockSpec((tk, K), lambda b, k: (k, 0)))
    g_spec = (pl.BlockSpec((None, tk, N), lambda b, k: (b, k, 0)) if g_batched
              else pl.BlockSpec((tk, N), lambda b, k: (k, 0)))
    (out,), slots = _call(
        body, grid=(nb, nk), in_specs=[a_spec, g_spec],
        out_specs=[pl.BlockSpec((None, K, N), lambda b, k: (b, 0, 0))],
        out_shape=[SDS((nb, K, N), BF16)], scratch_shapes=[pltpu.VMEM((K, N), F32)],
        args=(a, g), name=name, comm=comm)
    return out, slots


def dh_pre_bwd(dz, w, x, g, dyres, name, comm=None):
    nb, T, F = dz.shape
    D = w.shape[1]
    tm = 512

    rc = tm // ROW_SPLIT

    def body(dz_ref, w_ref, x_ref, g_ref, dy_ref, dx_ref, dg_ref):
        @pl.when(pl.program_id(0) == 0)
        def _():
            dg_ref[...] = jnp.zeros_like(dg_ref)

        accs = []
        for c in range(ROW_SPLIT):
            rows = pl.ds(c * rc, rc)
            dh = _dot_nt(dz_ref[0, rows, :].astype(BF16), w_ref[0])
            for b in range(1, nb):
                dh += _dot_nt(dz_ref[b, rows, :].astype(BF16), w_ref[b])
            accs.append(dh)
        for c, dh in enumerate(accs):
            rows = pl.ds(c * rc, rc)
            xv = x_ref[rows, :]
            rstd = lax.rsqrt(jnp.mean(xv * xv, axis=-1, keepdims=True) + NORM_EPS)
            xh = xv * rstd
            dg_ref[...] += jnp.sum(dh * xh, axis=0, keepdims=True)
            dhg = dh * g_ref[...]
            dx_ref[rows, :] = dy_ref[rows, :] + rstd * (dhg - xh * jnp.mean(dhg * xh, axis=-1, keepdims=True))

    return _call(
        body, grid=(T // tm,),
        in_specs=[pl.BlockSpec((nb, tm, F), lambda i: (0, i, 0)), _resident((nb, D, F)),
                  pl.BlockSpec((tm, D), lambda i: (i, 0)), pl.BlockSpec((1, D), lambda i: (0, 0)),
                  pl.BlockSpec((tm, D), lambda i: (i, 0))],
        out_specs=[pl.BlockSpec((tm, D), lambda i: (i, 0)), pl.BlockSpec((1, D), lambda i: (0, 0))],
        out_shape=[SDS((T, D), F32), SDS((1, D), F32)],
        args=(dz, w, x, g, dyres), name=name, comm=comm)


ATTN_GROUP = {1: 4, 4: 1, 16: 1}
ATTN_UNROLL = 4


def _attn_masks():
    qi = lax.broadcasted_iota(jnp.int32, (QBLK, QBLK), 0)
    kj = lax.broadcasted_iota(jnp.int32, (QBLK, QBLK), 1)
    cur_ok = kj <= qi
    prev_ok = kj >= qi
    dcur = (qi - kj).astype(F32)
    return cur_ok, prev_ok, dcur, dcur + float(QBLK)


def _head_slopes(p, d):
    out = []
    for hq in range(2):
        v = [float(d) * 2.0 ** -(2 * q + hq + 1) for q in range(4)]
        out.append(jnp.where(p == 0, v[0], jnp.where(p == 1, v[1], jnp.where(p == 2, v[2], v[3]))))
    return out


def _rows(start, d):
    return pl.ds(start, QBLK, stride=d) if d > 1 else pl.ds(start, QBLK)


def _pair_spec(rows, part, blk):
    return pl.BlockSpec((None, rows, PAIR_W), lambda p, n: (2 * part + p // 2, blk(n), p % 2))


def _for_query_blocks(d, groups, several):
    blocks = [(g, r) for g in range(groups) for r in range(d)]
    if len(blocks) <= 2 * ATTN_UNROLL:
        for s in range(0, len(blocks), ATTN_UNROLL):
            several(blocks[s:s + ATTN_UNROLL])
    else:
        def some(i, carry):
            several([(0, i * ATTN_UNROLL + j) for j in range(ATTN_UNROLL)])
            return carry

        lax.fori_loop(0, d // ATTN_UNROLL, some, 0)


def attn_fwd(proj, d, name):
    T = proj.shape[1]
    sb, groups = QBLK * d, ATTN_GROUP[d]
    rb = sb * groups
    nblk = T // rb

    def body(q_ref, kc_ref, kp_ref, vc_ref, vp_ref, o_ref, l_ref):
        p, n = pl.program_id(0), pl.program_id(1)
        cur_ok, prev_ok, dcur, dprev = _attn_masks()
        first_ok = jnp.logical_and(prev_ok, n > 0)
        lane_head = lax.broadcasted_iota(jnp.int32, (QBLK, PAIR_W), 1) // HEAD_DIM
        slopes = _head_slopes(p, d)

        def several(blocks):
            work = []
            for g, r in blocks:
                rows = _rows(g * sb + r, d)
                q = q_ref[rows, :]
                kc, vc = kc_ref[rows, :].astype(BF16), vc_ref[rows, :].astype(BF16)
                if g == 0:
                    prow, pok = _rows(r, d), first_ok
                    kp, vp = kp_ref[prow, :].astype(BF16), vp_ref[prow, :].astype(BF16)
                else:
                    prow, pok = _rows((g - 1) * sb + r, d), prev_ok
                    kp, vp = kc_ref[prow, :].astype(BF16), vc_ref[prow, :].astype(BF16)
                for hq in range(2):
                    qm = jnp.where(lane_head == hq, q, 0.0).astype(BF16)
                    work.append([rows, hq, pok, vc, vp, _dot_nt(qm, kc), _dot_nt(qm, kp)])
            for w in work:
                _, hq, pok, _, _, sc, sp = w
                sc = jnp.where(cur_ok, sc * 0.125 - slopes[hq] * dcur, NEG)
                sp = jnp.where(pok, sp * 0.125 - slopes[hq] * dprev, NEG)
                m = jnp.maximum(jnp.max(sc, axis=1, keepdims=True), jnp.max(sp, axis=1, keepdims=True))
                pc = jnp.exp(sc - m)
                pp = jnp.exp(sp - m)
                den = jnp.sum(pc, axis=1, keepdims=True) + jnp.sum(pp, axis=1, keepdims=True)
                w[5:] = [pc.astype(BF16), pp.astype(BF16), 1.0 / den, m + jnp.log(den)]
            for i in range(0, len(work), 2):
                o_acc = jnp.zeros((QBLK, PAIR_W), F32)
                l_acc = jnp.zeros((QBLK, PAIR_W), F32)
                for rows, hq, _, vc, vp, pc, pp, inv, lse in work[i:i + 2]:
                    hm = lane_head == hq
                    o_acc = jnp.where(hm, (_dot(pc, vc) + _dot(pp, vp)) * inv, o_acc)
                    l_acc = jnp.where(hm, lse, l_acc)
                o_ref[rows, :] = o_acc
                l_ref[rows, :] = l_acc

        _for_query_blocks(d, groups, several)

    cur = lambda part: _pair_spec(rb, part, lambda n: n)
    prv = lambda part: _pair_spec(sb, part, lambda n: jnp.maximum(n * groups - 1, 0))
    return pl.pallas_call(
        body, grid=(4, nblk), in_specs=[cur(0), cur(1), prv(1), cur(2), prv(2)], out_specs=[cur(0), cur(0)],
        out_shape=[SDS((2, T, 2 * PAIR_W), F32), SDS((2, T, 2 * PAIR_W), F32)],
        compiler_params=_params(2), name=name)(proj, proj, proj, proj, proj)


def attn_merge(os_, ls_, name):
    _, T, HW = os_[0].shape
    tm = 512

    def body(o1, o2, o3, l1, l2, l3, o_ref, l_ref):
        a, b, c = l1[...], l2[...], l3[...]
        m = jnp.maximum(jnp.maximum(a, b), c)
        ea, eb, ec = jnp.exp(a - m), jnp.exp(b - m), jnp.exp(c - m)
        s = ea + eb + ec
        o_ref[...] = (ea * o1[...] + eb * o2[...] + ec * o3[...]) * (1.0 / s)
        l_ref[...] = m + jnp.log(s)

    spec = pl.BlockSpec((None, tm, HW), lambda h, i: (h, i, 0))
    return pl.pallas_call(
        body, grid=(2, T // tm), in_specs=[spec] * 6, out_specs=[spec, spec],
        out_shape=[SDS((2, T, HW), F32), SDS((2, T, HW), F32)],
        compiler_params=_params(2), name=name)(*os_, *ls_)


def attn_bwd(proj, dcat, o, lse, acc, d, name):
    T = proj.shape[1]
    sb, groups = QBLK * d, ATTN_GROUP[d]
    rb = sb * groups
    nblk = T // rb
    has_acc = acc is not None

    def body(*refs):
        (qc_ref, qn_ref, kc_ref, kp_ref, vc_ref, vp_ref, dc_ref, dn_ref, oc_ref, on_ref, lc_ref, ln_ref) = refs[:12]
        acc_ref = refs[12] if has_acc else None
        out_ref = refs[-1]
        p, n = pl.program_id(0), pl.program_id(1)
        cur_ok, prev_ok, dcur, dprev = _attn_masks()
        first_ok = jnp.logical_and(prev_ok, n > 0)
        last_ok = jnp.logical_and(prev_ok, n < nblk - 1)
        lane_head = lax.broadcasted_iota(jnp.int32, (QBLK, PAIR_W), 1) // HEAD_DIM
        slopes = _head_slopes(p, d)

        def one(g, r):
            rows = _rows(g * sb + r, d)
            q_c, do_c, o_c, l_c = qc_ref[rows, :], dc_ref[rows, :], oc_ref[rows, :], lc_ref[rows, :]
            k_c, v_c = kc_ref[rows, :].astype(BF16), vc_ref[rows, :].astype(BF16)
            if g == 0:
                prow, pok_c = _rows(r, d), first_ok
                k_p, v_p = kp_ref[prow, :].astype(BF16), vp_ref[prow, :].astype(BF16)
            else:
                prow, pok_c = _rows((g - 1) * sb + r, d), prev_ok
                k_p, v_p = kc_ref[prow, :].astype(BF16), vc_ref[prow, :].astype(BF16)
            if g == groups - 1:
                nrow, pok_n = _rows(r, d), last_ok
                q_n, do_n, o_n, l_n = qn_ref[nrow, :], dn_ref[nrow, :], on_ref[nrow, :], ln_ref[nrow, :]
            else:
                nrow, pok_n = _rows((g + 1) * sb + r, d), prev_ok
                q_n, do_n, o_n, l_n = qc_ref[nrow, :], dc_ref[nrow, :], oc_ref[nrow, :], lc_ref[nrow, :]
            heads = []
            for hq in range(2):
                hm = lane_head == hq
                qm_c = jnp.where(hm, q_c, 0.0).astype(BF16)
                qm_n = jnp.where(hm, q_n, 0.0).astype(BF16)
                dom_c = jnp.where(hm, do_c, 0.0)
                dom_n = jnp.where(hm, do_n, 0.0)
                dd_c = jnp.sum(dom_c * o_c, axis=1, keepdims=True)
                dd_n = jnp.sum(dom_n * o_n, axis=1, keepdims=True)
                ls_c = jnp.max(jnp.where(hm, l_c, NEG), axis=1, keepdims=True)
                ls_n = jnp.max(jnp.where(hm, l_n, NEG), axis=1, keepdims=True)
                dob_c, dob_n = dom_c.astype(BF16), dom_n.astype(BF16)
                mm = [(_dot_nt(qm_c, k_c), _dot_nt(dob_c, v_c)), (_dot_nt(qm_c, k_p), _dot_nt(dob_c, v_p)),
                      (_dot_nt(qm_n, k_c), _dot_nt(dob_n, v_c))]
                heads.append(dict(hq=hq, qm_c=qm_c, qm_n=qm_n, dob_c=dob_c, dob_n=dob_n, mm=mm,
                                  dd=(dd_c, dd_c, dd_n), ls=(ls_c, ls_c, ls_n)))
            return dict(rows=rows, k_c=k_c, k_p=k_p, heads=heads, oks=(cur_ok, pok_c, pok_n))

        def several(blocks):
            work = [one(g, r) for g, r in blocks]
            for w in work:
                for h in w["heads"]:
                    slope, dist = slopes[h["hq"]], (dcur, dprev, dprev)
                    h["pr"], h["ds"] = [], []
                    for j in range(3):
                        s = jnp.where(w["oks"][j], h["mm"][j][0] * 0.125 - slope * dist[j], NEG)
                        pr = jnp.exp(s - h["ls"][j])
                        h["pr"].append(pr.astype(BF16))
                        h["ds"].append((pr * (h["mm"][j][1] - h["dd"][j])).astype(BF16))
            for w in work:
                dq = jnp.zeros((QBLK, PAIR_W), F32)
                dk = jnp.zeros((QBLK, PAIR_W), F32)
                dv = jnp.zeros((QBLK, PAIR_W), F32)
                for h in w["heads"]:
                    ds, pr = h["ds"], h["pr"]
                    dq_h = _dot(ds[0], w["k_c"]) + _dot(ds[1], w["k_p"])
                    dk += (_dot_tn(ds[0], h["qm_c"]) + _dot_tn(ds[2], h["qm_n"])) * 0.125
                    dv += _dot_tn(pr[0], h["dob_c"]) + _dot_tn(pr[2], h["dob_n"])
                    dq = jnp.where(lane_head == h["hq"], dq_h * 0.125, dq)
                for part, val in enumerate((dq, dk, dv)):
                    if has_acc:
                        val = val + acc_ref.at[part][w["rows"], :]
                    out_ref.at[part][w["rows"], :] = val

        _for_query_blocks(d, groups, several)

    cur = lambda part: _pair_spec(rb, part, lambda n: n)
    prv = lambda part: _pair_spec(sb, part, lambda n: jnp.maximum(n * groups - 1, 0))
    nxt = lambda part: _pair_spec(sb, part, lambda n: jnp.minimum((n + 1) * groups, T // sb - 1))
    full = pl.BlockSpec((3, None, rb, PAIR_W), lambda p, n: (0, p // 2, n, p % 2))
    in_specs = [cur(0), nxt(0), cur(1), prv(1), cur(2), prv(2), cur(0), nxt(0), cur(0), nxt(0), cur(0), nxt(0)]
    args = [proj, proj, proj, proj, proj, proj, dcat, dcat, o, o, lse, lse]
    if has_acc:
        in_specs.append(full)
        args.append(acc)
    return pl.pallas_call(
        body, grid=(4, nblk), in_specs=in_specs, out_specs=full,
        out_shape=SDS((3, 2, T, 2 * PAIR_W), F32), compiler_params=_params(2), name=name)(*args)


def _scan_chunk(buf, apow_ref, c0, reverse, ptab=None, carry=None):
    tm = buf.shape[0]
    cre = pl.ds(c0, SCAN_CW)
    cim = pl.ds(pl.multiple_of(c0 + HALF_STATES, 128), SCAN_CW)
    re, im = buf[:, cre], buf[:, cim]
    row = lax.broadcasted_iota(jnp.int32, (tm, SCAN_CW), 0)
    sgn = -1.0 if reverse else 1.0
    k, s = 1, 0
    while k < tm:
        ar = apow_ref[pl.ds(s, 1), cre]
        ai = sgn * apow_ref[pl.ds(s, 1), cim]
        if reverse:
            ok, shift = row < tm - k, tm - k
        else:
            ok, shift = row >= k, k
        sre = jnp.where(ok, pltpu.roll(re, shift, 0), 0.0)
        sim = jnp.where(ok, pltpu.roll(im, shift, 0), 0.0)
        re, im = re + ar * sre - ai * sim, im + ar * sim + ai * sre
        k, s = 2 * k, s + 1
    if ptab is not None:
        pr, pi = ptab[:, cre], ptab[:, cim]
        cr, ci = carry[:, cre], carry[:, cim]
        re, im = re + pr * cr - pi * ci, im + pr * ci + pi * cr
    buf[:, cre] = re
    buf[:, cim] = im


def _scan_tile(buf, apow_ref, reverse, ptab=None, carry=None):
    per_half = HALF_STATES // SCAN_CW

    def chunk(j, _):
        c0 = pl.multiple_of((j // per_half) * 2 * HALF_STATES + (j % per_half) * SCAN_CW, 128)
        _scan_chunk(buf, apow_ref, c0, reverse, ptab, carry)
        return 0

    lax.fori_loop(0, 2 * per_half, chunk, 0)


def _init_ptab(ptab, apow_ref, reverse):
    tm, width = ptab.shape
    row = lax.broadcasted_iota(jnp.int32, (tm, width), 0)
    col = lax.broadcasted_iota(jnp.int32, (1, width), 1)
    a = apow_ref[pl.ds(0, 1), :]
    if reverse:
        is_im = (col // HALF_STATES) % 2 == 1
        a = jnp.where(is_im, -a, a)
    seed = (tm - 1) if reverse else 0
    ptab[...] = jnp.where(row == seed, a, 0.0)
    _scan_tile(ptab, apow_ref, reverse)


def ssm_fwd(proj, bh, ch, apow, dskip, name, comm=None):
    _, T, C = proj.shape
    tm = SCAN_TM
    SW = 4 * HALF_STATES

    def body(u_ref, bh_ref, ch_ref, apow_ref, dsk_ref, y_ref, s_ref, ptab, carry):
        @pl.when(pl.program_id(0) == 0)
        def _():
            _init_ptab(ptab, apow_ref, False)
            carry[...] = jnp.zeros_like(carry)

        for h in range(2):
            s_ref[:, pl.ds(h * 2 * HALF_STATES, 2 * HALF_STATES)] = _dot(u_ref[h].astype(BF16), bh_ref[h])
        _scan_tile(s_ref, apow_ref, False, ptab, carry)
        carry[...] = s_ref[pl.ds(tm - 1, 1), :]
        for h in range(2):
            sv = s_ref[:, pl.ds(h * 2 * HALF_STATES, 2 * HALF_STATES)].astype(BF16)
            y_ref[h] = _dot(sv, ch_ref[h]) + dsk_ref[h] * u_ref[h]

    return _call(
        body, grid=(T // tm,),
        in_specs=[pl.BlockSpec((2, tm, C), lambda i: (3, i, 0)),
                  pl.BlockSpec((2, C, 2 * HALF_STATES), lambda i: (0, 0, 0)),
                  pl.BlockSpec((2, 2 * HALF_STATES, C), lambda i: (0, 0, 0)),
                  pl.BlockSpec((8, SW), lambda i: (0, 0)),
                  pl.BlockSpec((2, 1, C), lambda i: (0, 0, 0))],
        out_specs=[pl.BlockSpec((2, tm, C), lambda i: (0, i, 0)), pl.BlockSpec((tm, SW), lambda i: (i, 0))],
        out_shape=[SDS((2, T, C), F32), SDS((T, SW), F32)],
        scratch_shapes=[pltpu.VMEM((tm, SW), F32), pltpu.VMEM((1, SW), F32)],
        args=(proj, bh, ch, apow, dskip), name=name, comm=comm)


def ssm_bwd(dy, proj, st, bh, ch, apow, dskip, name, comm=None):
    _, T, C = proj.shape
    tm = SCAN_TM
    nt = T // tm
    SW = 4 * HALF_STATES
    HS2 = 2 * HALF_STATES

    def body(dy_ref, u_ref, s_ref, sp_ref, bh_ref, ch_ref, apow_ref, dsk_ref,
             du_ref, da_ref, dbh_ref, dch_ref, dd_ref, lam, ptab, carry):
        i = pl.program_id(0)

        @pl.when(i == 0)
        def _():
            _init_ptab(ptab, apow_ref, True)
            carry[...] = jnp.zeros_like(carry)
            da_ref[...] = jnp.zeros_like(da_ref)
            dbh_ref[...] = jnp.zeros_like(dbh_ref)
            dch_ref[...] = jnp.zeros_like(dch_ref)
            dd_ref[...] = jnp.zeros_like(dd_ref)

        for h in range(2):
            lam[:, pl.ds(h * HS2, HS2)] = _dot_nt(dy_ref[h].astype(BF16), ch_ref[h])
        _scan_tile(lam, apow_ref, True, ptab, carry)
        carry[...] = lam[pl.ds(0, 1), :]

        first = i == nt - 1
        per_half = HALF_STATES // SCAN_CW

        def chunk(j, _):
            c0 = pl.multiple_of((j // per_half) * HS2 + (j % per_half) * SCAN_CW, 128)
            cre, cim = pl.ds(c0, SCAN_CW), pl.ds(pl.multiple_of(c0 + HALF_STATES, 128), SCAN_CW)
            row = lax.broadcasted_iota(jnp.int32, (tm, SCAN_CW), 0)
            pre = jnp.where(first, 0.0, sp_ref[pl.ds(7, 1), cre])
            pim = jnp.where(first, 0.0, sp_ref[pl.ds(7, 1), cim])
            spr = jnp.where(row == 0, pre, pltpu.roll(s_ref[:, cre], 1, 0))
            spi = jnp.where(row == 0, pim, pltpu.roll(s_ref[:, cim], 1, 0))
            lr, li = lam[:, cre], lam[:, cim]
            da_ref[:, cre] += jnp.sum(lr * spr + li * spi, axis=0, keepdims=True)
            da_ref[:, cim] += jnp.sum(li * spr - lr * spi, axis=0, keepdims=True)
            return 0

        lax.fori_loop(0, 2 * per_half, chunk, 0)

        for h in range(2):
            lb = lam[:, pl.ds(h * HS2, HS2)].astype(BF16)
            dyv, uv = dy_ref[h], u_ref[h]
            du_ref[h] = _dot_nt(lb, bh_ref[h]) + dsk_ref[h] * dyv
            dbh_ref[h] += _dot_tn(uv.astype(BF16), lb)
            dch_ref[h] += _dot_tn(s_ref[:, pl.ds(h * HS2, HS2)].astype(BF16), dyv.astype(BF16))
            dd_ref[h] += jnp.sum(dyv * uv, axis=0, keepdims=True)

    rev = lambda i: nt - 1 - i
    return _call(
        body, grid=(nt,),
        in_specs=[pl.BlockSpec((2, tm, C), lambda i: (0, rev(i), 0)),
                  pl.BlockSpec((2, tm, C), lambda i: (3, rev(i), 0)),
                  pl.BlockSpec((tm, SW), lambda i: (rev(i), 0)),
                  pl.BlockSpec((8, SW), lambda i: (jnp.maximum(rev(i) * (tm // 8) - 1, 0), 0)),
                  pl.BlockSpec((2, C, HS2), lambda i: (0, 0, 0)),
                  pl.BlockSpec((2, HS2, C), lambda i: (0, 0, 0)),
                  pl.BlockSpec((8, SW), lambda i: (0, 0)),
                  pl.BlockSpec((2, 1, C), lambda i: (0, 0, 0))],
        out_specs=[pl.BlockSpec((2, tm, C), lambda i: (0, rev(i), 0)),
                   pl.BlockSpec((1, SW), lambda i: (0, 0)),
                   pl.BlockSpec((2, C, HS2), lambda i: (0, 0, 0)),
                   pl.BlockSpec((2, HS2, C), lambda i: (0, 0, 0)),
                   pl.BlockSpec((2, 1, C), lambda i: (0, 0, 0))],
        out_shape=[SDS((2, T, C), F32), SDS((1, SW), F32), SDS((2, C, HS2), F32), SDS((2, HS2, C), F32),
                   SDS((2, 1, C), F32)],
        scratch_shapes=[pltpu.VMEM((tm, SW), F32), pltpu.VMEM((tm, SW), F32), pltpu.VMEM((1, SW), F32)],
        args=(dy, proj, st, st, bh, ch, apow, dskip), name=name, comm=comm)


_GELU_C = math.sqrt(2.0 / math.pi)


def _gelu(x):
    t = jnp.tanh(_GELU_C * (x + 0.044715 * x * x * x))
    return 0.5 * x * (1.0 + t), t


def glu_fwd(y, w, b, name):
    _, T, C = y.shape
    tm = 512

    def body(y_ref, w_ref, b_ref, o_ref, lg_ref):
        y0, _ = _gelu(y_ref[0])
        y1, _ = _gelu(y_ref[1])
        lg = _dot(y0.astype(BF16), w_ref[0]) + _dot(y1.astype(BF16), w_ref[1]) + b_ref[...]
        sg = _sigmoid(lg)
        o_ref[0] = y0 * sg[:, :C]
        o_ref[1] = y1 * sg[:, C:]
        lg_ref[0] = lg[:, :C]
        lg_ref[1] = lg[:, C:]

    return pl.pallas_call(
        body, grid=(T // tm,),
        in_specs=[pl.BlockSpec((2, tm, C), lambda i: (0, i, 0)), pl.BlockSpec((2, C, 2 * C), lambda i: (0, 0, 0)),
                  pl.BlockSpec((1, 2 * C), lambda i: (0, 0))],
        out_specs=[pl.BlockSpec((2, tm, C), lambda i: (0, i, 0)), pl.BlockSpec((2, tm, C), lambda i: (0, i, 0))],
        out_shape=[SDS((2, T, C), F32), SDS((2, T, C), F32)], compiler_params=_params(1), name=name)(y, w, b)


def glu_bwd(dcat, y, lg, w, name):
    _, T, C = y.shape
    tm = 512

    def body(d_ref, y_ref, lg_ref, w_ref, dy_ref, dw_ref, db_ref):
        @pl.when(pl.program_id(0) == 0)
        def _():
            dw_ref[...] = jnp.zeros_like(dw_ref)
            db_ref[...] = jnp.zeros_like(db_ref)

        y2, th, sg, dlg = [], [], [], []
        for h in range(2):
            yy, tt = _gelu(y_ref[h])
            ss = _sigmoid(lg_ref[h])
            y2.append(yy)
            th.append(tt)
            sg.append(ss)
            dlg.append(d_ref[h] * yy * ss * (1.0 - ss))
        dl = jnp.concatenate(dlg, axis=1)
        dlb = dl.astype(BF16)
        db_ref[...] += jnp.sum(dl, axis=0, keepdims=True)
        for h in range(2):
            dy2 = d_ref[h] * sg[h] + _dot_nt(dlb, w_ref[h])
            yv = y_ref[h]
            dgelu = 0.5 * (1.0 + th[h]) + 0.5 * yv * (1.0 - th[h] * th[h]) * _GELU_C * (1.0 + 3 * 0.044715 * yv * yv)
            dy_ref[h] = dy2 * dgelu
            dw_ref[h] += _dot_tn(y2[h].astype(BF16), dlb)

    return pl.pallas_call(
        body, grid=(T // tm,),
        in_specs=[pl.BlockSpec((2, tm, C), lambda i: (1, i, 0)), pl.BlockSpec((2, tm, C), lambda i: (0, i, 0)),
                  pl.BlockSpec((2, tm, C), lambda i: (0, i, 0)), pl.BlockSpec((2, C, 2 * C), lambda i: (0, 0, 0))],
        out_specs=[pl.BlockSpec((2, tm, C), lambda i: (0, i, 0)), pl.BlockSpec((2, C, 2 * C), lambda i: (0, 0, 0)),
                   pl.BlockSpec((1, 2 * C), lambda i: (0, 0))],
        out_shape=[SDS((2, T, C), F32), SDS((2, C, 2 * C), F32), SDS((1, 2 * C), F32)],
        compiler_params=_params(1), name=name)(dcat, y, lg, w)


def adamw(w, m, v, slots, name):
    R, C = w.shape
    tr = R
    for cand in (512, 256, 128, 64, 32, 16, 8):
        if R % cand == 0 and cand * C * 4 <= 2 * 1024 * 1024:
            tr = cand
            break
    c1 = 1.0 / (1.0 - ADAM_B1 ** ADAM_STEP)
    c2 = 1.0 / (1.0 - ADAM_B2 ** ADAM_STEP)

    def body(w_ref, m_ref, v_ref, s_ref, g_ref, d_ref, nm_ref, nv_ref):
        g = s_ref[0].astype(F32)
        for j in range(1, N_DEV):
            g = g + s_ref[j].astype(F32)
        nm = ADAM_B1 * m_ref[...] + (1.0 - ADAM_B1) * g
        nv = ADAM_B2 * v_ref[...] + (1.0 - ADAM_B2) * (g * g)
        g_ref[...] = g
        nm_ref[...] = nm
        nv_ref[...] = nv
        d_ref[...] = -ADAM_LR * ((nm * c1) / (jnp.sqrt(nv * c2) + ADAM_EPS) + ADAM_WD * w_ref[...])

    spec = pl.BlockSpec((tr, C), lambda i: (i, 0))
    return pl.pallas_call(
        body, grid=(R // tr,),
        in_specs=[spec, spec, spec, pl.BlockSpec((N_DEV, tr, C), lambda i: (0, i, 0))],
        out_specs=[spec] * 4, out_shape=[SDS((R, C), F32)] * 4, compiler_params=_params(1), name=name)(w, m, v, slots)


def _discretise(a_re, a_im, log_dt, b_re, b_im):
    dt = jnp.exp(log_dt)[:, None]
    e = jnp.exp(dt * a_re)
    ar, ai = e * jnp.cos(dt * a_im), e * jnp.sin(dt * a_im)
    den = a_re * a_re + a_im * a_im
    nr, ni = ar - 1.0, ai
    wr = (nr * a_re + ni * a_im) / den
    wi = (ni * a_re - nr * a_im) / den
    bbr = wr[..., None] * b_re - wi[..., None] * b_im
    bbi = wr[..., None] * b_im + wi[..., None] * b_re
    return ar, ai, bbr, bbi


def _block_diag(t):
    eye = jnp.eye(16, dtype=t.dtype).reshape(1, 16, 1, 16, 1)
    r, c = t.shape[1], t.shape[2]
    return (t.reshape(2, 16, r, 1, c) * eye).reshape(2, 16 * r, 16 * c)


def _diag_blocks(m, r, c):
    eye = jnp.eye(16, dtype=m.dtype).reshape(1, 16, 1, 16, 1)
    return jnp.sum(m.reshape(2, 16, r, 16, c) * eye, axis=3).reshape(32, r, c)


def _state_vec(re, im):
    return jnp.stack([re.reshape(2, HALF_STATES), im.reshape(2, HALF_STATES)], axis=1).reshape(-1)


BIG = ("ffn1_w_in", "ffn1_w_out", "w_mix_in", "w_glu", "w_mix_out", "ffn2_w_in", "ffn2_w_out")
WEIGHTS = ("ffn1_pre_g", "ffn1_w_in", "ffn1_w_out", "ffn1_post_g", "mix_pre_g", "w_mix_in", "a_re", "a_im", "log_dt",
           "b_re", "b_im", "c_re", "c_im", "d_skip", "w_glu", "b_glu", "w_mix_out", "mix_post_g", "ffn2_pre_g",
           "ffn2_w_in", "ffn2_w_out", "ffn2_post_g")
SMALL = tuple(n for n in WEIGHTS if n not in BIG)
PACK_COLS = 1024


def _pack(parts):
    flat = jnp.concatenate([p.reshape(-1) for p in parts])
    rows = -(-flat.shape[0] // (8 * PACK_COLS)) * 8
    return jnp.pad(flat, (0, rows * PACK_COLS - flat.shape[0])).reshape(rows, PACK_COLS)


def _unpack(packed, shapes):
    flat, out, off = packed.reshape(-1), [], 0
    for s in shapes:
        n = math.prod(s)
        out.append(flat[off:off + n].reshape(s))
        off += n
    return out


def _gather(names, wb):
    return [wb[n] for n in names], [False] * len(names)


def _ffn_bwd(dy, saved, x, pre_g, w_in, w_out4, post_g, tag):
    h, z, a, o = saved
    T = x.shape[0]
    do, dg_post = post_bwd(dy, o, post_g, 0.5, f"{tag}_post_bwd")
    dz = ffn_dact(do, w_out4, z, f"{tag}_dact")
    dz8 = dz.reshape(8, T, dz.shape[-1])
    dw_out, _ = mm_tn(a, do, True, False, 4, f"{tag}_dwout")
    dw_in, (s_out,) = mm_tn(h, dz8, False, True, 8, f"{tag}_dwin", comm=([dw_out.reshape(8, -1, D_MODEL)], [True]))
    (dx, dg_pre), (s_in,) = dh_pre_bwd(dz8, w_in, x, pre_g, dy, f"{tag}_dh", comm=([dw_in], [True]))
    return dx, dg_pre, dg_post, (s_in, s_out)


def local_step(x, tgt, sp, wb):
    T = x.shape[0]
    ar, ai, bbr, bbi = _discretise(sp["a_re"], sp["a_im"], sp["log_dt"], sp["b_re"], sp["b_im"])
    pr, pi, rows = ar, ai, []
    for _ in range(8):
        rows.append(_state_vec(pr, pi))
        pr, pi = pr * pr - pi * pi, 2.0 * pr * pi
    apow = jnp.stack(rows)
    bh = jnp.concatenate([_block_diag(bbr.transpose(0, 2, 1)), _block_diag(bbi.transpose(0, 2, 1))], axis=2)
    ch = jnp.concatenate([_block_diag(sp["c_re"].transpose(0, 2, 1)), _block_diag(-sp["c_im"].transpose(0, 2, 1))], axis=1)
    bh, ch = bh.astype(BF16), ch.astype(BF16)
    dskip = sp["d_skip"].reshape(2, 1, 256)

    (w1_in,) = exchange(*_gather(["ffn1_w_in"], wb), "gather_w1in")
    h1 = rms_fwd(x, sp["ffn1_pre_g"], "ffn1_rms")
    (z1, a1), (w1_out, w_mi) = ffn_in(h1, w1_in, "ffn1_in", comm=_gather(["ffn1_w_out", "w_mix_in"], wb))
    w1_out4 = w1_out.reshape(4, -1, D_MODEL)
    (o1, x1), (w_glu, w_mo) = mm_acc_norm(
        a1, w1_out4, x, sp["ffn1_post_g"], 0.5, "ffn1_out", comm=_gather(["w_glu", "w_mix_out"], wb))
    w_glu2, w_mo4 = w_glu.reshape(2, 256, 512), w_mo.reshape(4, 256, D_MODEL)
    h2 = rms_fwd(x1, sp["mix_pre_g"], "mix_rms")
    proj = mm_nn_b(h2, w_mi, 8, F32, "mix_proj")
    (y_ssm, states), (w2_in, w2_out) = ssm_fwd(
        proj, bh, ch, apow, dskip, "ssm_fwd", comm=_gather(["ffn2_w_in", "ffn2_w_out"], wb))
    w2_out4 = w2_out.reshape(4, -1, D_MODEL)
    os_, ls_ = [], []
    for d in DILATIONS:
        o_d, l_d = attn_fwd(proj, d, f"attn_fwd_d{d}")
        os_.append(o_d)
        ls_.append(l_d)
    o_att, lse = attn_merge(os_, ls_, "attn_merge")
    o_ssm, lg = glu_fwd(y_ssm, w_glu2, sp["b_glu"], "glu_fwd")
    cat = jnp.concatenate([o_att, o_ssm], axis=0)
    (mixed, x2), _ = mm_acc_norm(cat, w_mo4, x1, sp["mix_post_g"], 1.0, "mix_out")
    h3 = rms_fwd(x2, sp["ffn2_pre_g"], "ffn2_rms")
    (z3, a3), _ = ffn_in(h3, w2_in, "ffn2_in")
    (o3, x3), _ = mm_acc_norm(a3, w2_out4, x2, sp["ffn2_post_g"], 0.5, "ffn2_out")
    dy3, sq = loss_head(x3, tgt, "loss_head")

    dx2, dg_f2pre, dg_f2post, (s_w2in, s_w2out) = _ffn_bwd(
        dy3, (h3, z3, a3, o3), x2, sp["ffn2_pre_g"], w2_in, w2_out4, sp["ffn2_post_g"], "ffn2")
    dmixed, dg_mpost = post_bwd(dx2, mixed, sp["mix_post_g"], 1.0, "mix_post_bwd")
    dcat = mm_nt_b(dmixed, w_mo4, "mix_dcat")
    dw_mo, _ = mm_tn(cat, dmixed, True, False, 4, "mix_dwout")
    dy_ssm, dw_glu, db_glu = glu_bwd(dcat, y_ssm, lg, w_glu2, "glu_bwd")
    (du, da, dbh, dch, dd), (s_wmo, s_wglu) = ssm_bwd(
        dy_ssm, proj, states, bh, ch, apow, dskip, "ssm_bwd",
        comm=([dw_mo.reshape(8, 128, D_MODEL), dw_glu.astype(BF16).reshape(8, 64, 512)], [True, True]))
    dqkv = None
    for d in DILATIONS:
        dqkv = attn_bwd(proj, dcat, o_att, lse, dqkv, d, f"attn_bwd_d{d}")
    dproj = jnp.concatenate([dqkv.reshape(6, T, 256), du], axis=0)
    dw_mi, _ = mm_tn(h2, dproj, False, True, 8, "mix_dwin")
    (dx1, dg_mpre), (s_wmi,) = dh_pre_bwd(
        dproj, w_mi, x1, sp["mix_pre_g"], dx2, "mix_dh", comm=([dw_mi], [True]))
    dx0, dg_f1pre, dg_f1post, (s_w1in, s_w1out) = _ffn_bwd(
        dx1, (h1, z1, a1, o1), x, sp["ffn1_pre_g"], w1_in, w1_out4, sp["ffn1_post_g"], "ffn1")

    da4 = da.reshape(2, 2, HALF_STATES)
    d_ar, d_ai = da4[:, 0].reshape(32, N_STATE), da4[:, 1].reshape(32, N_STATE)
    d_bbr = _diag_blocks(dbh[:, :, :HALF_STATES], 16, N_STATE).transpose(0, 2, 1)
    d_bbi = _diag_blocks(dbh[:, :, HALF_STATES:], 16, N_STATE).transpose(0, 2, 1)
    _, disc_vjp = jax.vjp(_discretise, sp["a_re"], sp["a_im"], sp["log_dt"], sp["b_re"], sp["b_im"])
    g_are, g_aim, g_ldt, g_bre, g_bim = disc_vjp((d_ar, d_ai, d_bbr, d_bbi))
    g_cre = _diag_blocks(dch[:, :HALF_STATES], N_STATE, 16).transpose(0, 2, 1)
    g_cim = -_diag_blocks(dch[:, HALF_STATES:], N_STATE, 16).transpose(0, 2, 1)
    small = {
        "ffn1_pre_g": dg_f1pre, "ffn1_post_g": dg_f1post, "mix_pre_g": dg_mpre, "a_re": g_are, "a_im": g_aim,
        "log_dt": g_ldt, "b_re": g_bre, "b_im": g_bim, "c_re": g_cre, "c_im": g_cim, "d_skip": dd.reshape(1, 512),
        "b_glu": db_glu, "mix_post_g": dg_mpost, "ffn2_pre_g": dg_f2pre, "ffn2_post_g": dg_f2post,
    }
    (small_slots,) = exchange([_pack([small[n] for n in SMALL])], [False], "exchange_small")
    big_slots = {"ffn1_w_in": s_w1in, "ffn1_w_out": s_w1out, "w_mix_in": s_wmi, "w_glu": s_wglu, "w_mix_out": s_wmo,
                 "ffn2_w_in": s_w2in, "ffn2_w_out": s_w2out}
    return sq, dx0, big_slots, small_slots


def kernel(x, ffn1_pre_g, ffn1_w_in, ffn1_w_out, ffn1_post_g, mix_pre_g, w_mix_in, a_re, a_im, log_dt, b_re, b_im, c_re, c_im, d_skip, w_glu, b_glu, w_mix_out, mix_post_g, ffn2_pre_g, ffn2_w_in, ffn2_w_out, ffn2_post_g, loss_target, m_ffn1_pre_g, m_ffn1_w_in, m_ffn1_w_out, m_ffn1_post_g, m_mix_pre_g, m_w_mix_in, m_a_re, m_a_im, m_log_dt, m_b_re, m_b_im, m_c_re, m_c_im, m_d_skip, m_w_glu, m_b_glu, m_w_mix_out, m_mix_post_g, m_ffn2_pre_g, m_ffn2_w_in, m_ffn2_w_out, m_ffn2_post_g, v_ffn1_pre_g, v_ffn1_w_in, v_ffn1_w_out, v_ffn1_post_g, v_mix_pre_g, v_w_mix_in, v_a_re, v_a_im, v_log_dt, v_b_re, v_b_im, v_c_re, v_c_im, v_d_skip, v_w_glu, v_b_glu, v_w_mix_out, v_mix_post_g, v_ffn2_pre_g, v_ffn2_w_in, v_ffn2_w_out, v_ffn2_post_g):
    args = dict(locals())
    w = {n: args[n][0] for n in WEIGHTS}
    m = {n: args["m_" + n][0] for n in WEIGHTS}
    v = {n: args["v_" + n][0] for n in WEIGHTS}

    wb = {n: w[n].astype(BF16) for n in BIG}
    sp = {n: w[n] for n in SMALL}
    for n in ("ffn1_pre_g", "ffn1_post_g", "mix_pre_g", "mix_post_g", "ffn2_pre_g", "ffn2_post_g", "b_glu", "d_skip"):
        sp[n] = w[n].reshape(1, -1)

    sq, grad_x, big_slots, small_slots = local_step(x[0], loss_target[0], sp, wb)
    loss = lax.psum(0.5 / D_MODEL * jnp.sum(sq), ("x", "y", "c"))

    outs = {}
    for n in BIG:
        shp = w[n].shape
        r2 = lambda t: t.reshape(-1, shp[-1])
        res = adamw(r2(w[n]), r2(m[n]), r2(v[n]), big_slots[n].reshape(N_DEV, -1, shp[-1]), f"adamw_{n}")
        outs[n] = [t.reshape((1,) + shp) for t in res]
    res = adamw(_pack([w[n] for n in SMALL]), _pack([m[n] for n in SMALL]), _pack([v[n] for n in SMALL]),
                small_slots, "adamw_small")
    shapes = [(1,) + w[n].shape for n in SMALL]
    unpacked = [_unpack(t, shapes) for t in res]
    for j, n in enumerate(SMALL):
        outs[n] = [unpacked[k][j] for k in range(4)]

    result = [loss, grad_x[None]]
    for k in range(4):
        result += [outs[n][k] for n in WEIGHTS]
    return tuple(result)
```

```python
import functools
import math

import jax
import jax.numpy as jnp
from jax import lax
from jax.experimental import pallas as pl
from jax.experimental.pallas import tpu as pltpu

F32, BF16 = jnp.float32, jnp.bfloat16
SDS = jax.ShapeDtypeStruct

D_MODEL = 1024
N_DEV = 8
HEAD_DIM = 64
PAIR_W = 128
QBLK = 128
DILATIONS = (1, 4, 16)
N_STATE = 64
HALF_STATES = 1024
NORM_EPS = 1e-6
NEG = -1e30
VMEM_LIMIT = 56 * 1024 * 1024
ADAM_LR, ADAM_B1, ADAM_B2, ADAM_EPS, ADAM_WD, ADAM_STEP = 1e-3, 0.9, 0.999, 1e-8, 0.01, 10
SCAN_TM = 256
SCAN_CW = 512


def _params(n_grid):
    return pltpu.CompilerParams(dimension_semantics=("arbitrary",) * n_grid, vmem_limit_bytes=VMEM_LIMIT)


def _dot(a, b):
    return jnp.dot(a, b, preferred_element_type=F32)


def _dot_nt(a, b):
    return lax.dot_general(a, b, (((1,), (1,)), ((), ())), preferred_element_type=F32)


def _dot_tn(a, b):
    return lax.dot_general(a, b, (((0,), (0,)), ((), ())), preferred_element_type=F32)


def _sigmoid(v):
    return 1.0 / (1.0 + jnp.exp(-v))


def _resident(shape):
    return pl.BlockSpec(shape, lambda i: (0,) * len(shape), pipeline_mode=pl.Buffered(1))


ROW_SPLIT = 2


def _exchange_phase(ins, outs, scatter, sems, start):
    send_sems, recv_sems, loc_sems = sems
    x, y, c = lax.axis_index("x"), lax.axis_index("y"), lax.axis_index("c")
    me = 4 * x + 2 * y + c
    own_copies, sends, arrivals = [], [], []
    for i in range(len(ins)):
        own = ins[i].at[me] if scatter[i] else ins[i]
        own_copies.append(pltpu.make_async_copy(own, outs[i].at[me], loc_sems.at[i]))
        for k in range(1, N_DEV):
            px = 1 - x if k & 4 else x
            py = 1 - y if k & 2 else y
            pc = 1 - c if k & 1 else c
            peer = 4 * px + 2 * py + pc
            src = ins[i].at[peer] if scatter[i] else ins[i]
            common = dict(src_ref=src, send_sem=send_sems.at[i, k - 1], recv_sem=recv_sems.at[i, k - 1],
                          device_id=(px, py, pc), device_id_type=pl.DeviceIdType.MESH)
            sends.append(pltpu.make_async_remote_copy(dst_ref=outs[i].at[me], **common))
            if not start:
                arrivals.append(pltpu.make_async_remote_copy(dst_ref=outs[i].at[peer], **common))
    if start:
        for cp in own_copies + sends:
            cp.start()
    else:
        for cp in arrivals:
            cp.wait_recv()
        for cp in sends:
            cp.wait_send()
        for cp in own_copies:
            cp.wait()


def _comm_shapes(arrs, scatter):
    n = len(arrs)
    out_shapes = [SDS(a.shape if scatter[i] else (N_DEV,) + a.shape, a.dtype) for i, a in enumerate(arrs)]
    sems = [pltpu.SemaphoreType.DMA((n, N_DEV - 1)), pltpu.SemaphoreType.DMA((n, N_DEV - 1)),
            pltpu.SemaphoreType.DMA((n,))]
    return out_shapes, sems


def exchange(arrs, scatter, name):
    n = len(arrs)
    out_shapes, sems = _comm_shapes(arrs, scatter)

    def body(*refs):
        ins, outs, sem_refs = refs[:n], refs[n:2 * n], refs[2 * n:]
        _exchange_phase(ins, outs, scatter, sem_refs, True)
        _exchange_phase(ins, outs, scatter, sem_refs, False)

    anyspec = pl.BlockSpec(memory_space=pl.ANY)
    return pl.pallas_call(
        body, in_specs=[anyspec] * n, out_specs=[anyspec] * n, out_shape=out_shapes, scratch_shapes=sems,
        compiler_params=pltpu.CompilerParams(has_side_effects=True), name=name)(*arrs)


def _call(body, *, grid, in_specs, out_specs, out_shape, args, name, scratch_shapes=(), comm=None):
    n_grid, scratch_shapes = len(grid), list(scratch_shapes)
    if comm is None:
        outs = pl.pallas_call(body, grid=grid, in_specs=in_specs, out_specs=out_specs, out_shape=out_shape,
                              scratch_shapes=scratch_shapes, compiler_params=_params(n_grid), name=name)(*args)
        return outs, []
    arrs, scatter = comm
    nc, n_in, n_out, n_sc = len(arrs), len(in_specs), len(out_specs), len(scratch_shapes)
    comm_shapes, sems = _comm_shapes(arrs, scatter)

    def wrapped(*refs):
        ins, cins = refs[:n_in], refs[n_in:n_in + nc]
        o0 = n_in + nc
        outs, couts = refs[o0:o0 + n_out], refs[o0 + n_out:o0 + n_out + nc]
        s0 = o0 + n_out + nc
        scratch, sem_refs = refs[s0:s0 + n_sc], refs[s0 + n_sc:]
        first = functools.reduce(jnp.logical_and, [pl.program_id(k) == 0 for k in range(n_grid)])
        last = functools.reduce(jnp.logical_and, [pl.program_id(k) == grid[k] - 1 for k in range(n_grid)])

        @pl.when(first)
        def _():
            _exchange_phase(cins, couts, scatter, sem_refs, True)

        body(*ins, *outs, *scratch)

        @pl.when(last)
        def _():
            _exchange_phase(cins, couts, scatter, sem_refs, False)

    anyspec = pl.BlockSpec(memory_space=pl.ANY)
    res = pl.pallas_call(
        wrapped, grid=grid, in_specs=list(in_specs) + [anyspec] * nc, out_specs=list(out_specs) + [anyspec] * nc,
        out_shape=list(out_shape) + comm_shapes, scratch_shapes=scratch_shapes + sems,
        compiler_params=pltpu.CompilerParams(dimension_semantics=("arbitrary",) * n_grid,
                                             vmem_limit_bytes=VMEM_LIMIT, has_side_effects=True),
        name=name)(*args, *arrs)
    return res[:n_out], res[n_out:]


def rms_fwd(x, g, name):
    T, D = x.shape
    tm = 512

    def body(x_ref, g_ref, h_ref):
        xv = x_ref[...]
        r = lax.rsqrt(jnp.mean(xv * xv, axis=-1, keepdims=True) + NORM_EPS)
        h_ref[...] = (xv * r * g_ref[...]).astype(BF16)

    return pl.pallas_call(
        body, grid=(T // tm,),
        in_specs=[pl.BlockSpec((tm, D), lambda i: (i, 0)), pl.BlockSpec((1, D), lambda i: (0, 0))],
        out_specs=pl.BlockSpec((tm, D), lambda i: (i, 0)),
        out_shape=SDS((T, D), BF16), compiler_params=_params(1), name=name)(x, g)


def ffn_in(h, w, name, comm=None):
    T, D = h.shape
    F = w.shape[2]
    tm = 512

    def body(h_ref, wg_ref, wu_ref, z_ref, a_ref):
        hv = h_ref[...]
        zg = _dot(hv, wg_ref[...])
        zu = _dot(hv, wu_ref[...])
        z_ref[0] = zg.astype(BF16)
        z_ref[1] = zu.astype(BF16)
        a_ref[...] = (zg * _sigmoid(zg) * zu).astype(BF16)

    return _call(
        body, grid=(4, T // tm),
        in_specs=[pl.BlockSpec((tm, D), lambda j, i: (i, 0)),
                  pl.BlockSpec((None, D, F), lambda j, i: (j, 0, 0)),
                  pl.BlockSpec((None, D, F), lambda j, i: (j + 4, 0, 0))],
        out_specs=[pl.BlockSpec((2, None, tm, F), lambda j, i: (0, j, i, 0)),
                   pl.BlockSpec((None, tm, F), lambda j, i: (j, i, 0))],
        out_shape=[SDS((2, 4, T, F), BF16), SDS((4, T, F), BF16)],
        args=(h, w, w), name=name, comm=comm)


def mm_nn_b(a, w, nb, out_dtype, name):
    T, K = a.shape
    N = w.shape[2]
    tm = 512

    def body(a_ref, w_ref, o_ref):
        av = a_ref[...]
        for b in range(nb):
            o_ref[b] = _dot(av, w_ref[b]).astype(out_dtype)

    return pl.pallas_call(
        body, grid=(T // tm,),
        in_specs=[pl.BlockSpec((tm, K), lambda i: (i, 0)), _resident((nb, K, N))],
        out_specs=pl.BlockSpec((nb, tm, N), lambda i: (0, i, 0)),
        out_shape=SDS((nb, T, N), out_dtype), compiler_params=_params(1), name=name)(a, w)


def mm_acc_norm(a, w, xres, g, scale, name, comm=None):
    nb, T, K = a.shape
    D = w.shape[2]
    tm = 512

    rc = tm // ROW_SPLIT

    def body(a_ref, w_ref, x_ref, g_ref, o_ref, y_ref):
        accs = []
        for c in range(ROW_SPLIT):
            rows = pl.ds(c * rc, rc)
            o = _dot(a_ref[0, rows, :].astype(BF16), w_ref[0])
            for b in range(1, nb):
                o += _dot(a_ref[b, rows, :].astype(BF16), w_ref[b])
            accs.append(o)
        for c, o in enumerate(accs):
            rows = pl.ds(c * rc, rc)
            r = lax.rsqrt(jnp.mean(o * o, axis=-1, keepdims=True) + NORM_EPS)
            o_ref[rows, :] = o
            y_ref[rows, :] = x_ref[rows, :] + scale * (o * r * g_ref[...])

    return _call(
        body, grid=(T // tm,),
        in_specs=[pl.BlockSpec((nb, tm, K), lambda i: (0, i, 0)), _resident((nb, K, D)),
                  pl.BlockSpec((tm, D), lambda i: (i, 0)), pl.BlockSpec((1, D), lambda i: (0, 0))],
        out_specs=[pl.BlockSpec((tm, D), lambda i: (i, 0)), pl.BlockSpec((tm, D), lambda i: (i, 0))],
        out_shape=[SDS((T, D), F32), SDS((T, D), F32)],
        args=(a, w, xres, g), name=name, comm=comm)


def loss_head(y, tgt, name):
    T, D = y.shape
    tm = 512

    def body(y_ref, t_ref, dy_ref, sq_ref):
        @pl.when(pl.program_id(0) == 0)
        def _():
            sq_ref[...] = jnp.zeros_like(sq_ref)

        e = y_ref[...] - t_ref[...]
        dy_ref[...] = e * (1.0 / D)
        sq_ref[...] += jnp.sum(e * e, axis=0, keepdims=True)

    return pl.pallas_call(
        body, grid=(T // tm,),
        in_specs=[pl.BlockSpec((tm, D), lambda i: (i, 0)), pl.BlockSpec((tm, D), lambda i: (i, 0))],
        out_specs=[pl.BlockSpec((tm, D), lambda i: (i, 0)), pl.BlockSpec((1, D), lambda i: (0, 0))],
        out_shape=[SDS((T, D), F32), SDS((1, D), F32)], compiler_params=_params(1), name=name)(y, tgt)


def post_bwd(dy, o, g, scale, name):
    T, D = o.shape
    tm = 512

    def body(dy_ref, o_ref, g_ref, do_ref, dg_ref):
        @pl.when(pl.program_id(0) == 0)
        def _():
            dg_ref[...] = jnp.zeros_like(dg_ref)

        ov = o_ref[...]
        r = scale * dy_ref[...]
        rstd = lax.rsqrt(jnp.mean(ov * ov, axis=-1, keepdims=True) + NORM_EPS)
        oh = ov * rstd
        dg_ref[...] += jnp.sum(r * oh, axis=0, keepdims=True)
        rg = r * g_ref[...]
        do_ref[...] = (rstd * (rg - oh * jnp.mean(rg * oh, axis=-1, keepdims=True))).astype(BF16)

    return pl.pallas_call(
        body, grid=(T // tm,),
        in_specs=[pl.BlockSpec((tm, D), lambda i: (i, 0)), pl.BlockSpec((tm, D), lambda i: (i, 0)),
                  pl.BlockSpec((1, D), lambda i: (0, 0))],
        out_specs=[pl.BlockSpec((tm, D), lambda i: (i, 0)), pl.BlockSpec((1, D), lambda i: (0, 0))],
        out_shape=[SDS((T, D), BF16), SDS((1, D), F32)], compiler_params=_params(1), name=name)(dy, o, g)


def mm_nt_b(gr, w, name):
    T, N = gr.shape
    nb, K, _ = w.shape
    tm = 512

    def body(g_ref, w_ref, o_ref):
        o_ref[...] = _dot_nt(g_ref[...], w_ref[...])

    return pl.pallas_call(
        body, grid=(nb, T // tm),
        in_specs=[pl.BlockSpec((tm, N), lambda b, i: (i, 0)), pl.BlockSpec((None, K, N), lambda b, i: (b, 0, 0))],
        out_specs=pl.BlockSpec((None, tm, K), lambda b, i: (b, i, 0)),
        out_shape=SDS((nb, T, K), F32), compiler_params=_params(2), name=name)(gr, w)


def ffn_dact(do, w_out, z, name):
    T, D = do.shape
    nb, F, _ = w_out.shape
    tm = 512

    def body(g_ref, w_ref, z_ref, dz_ref):
        da = _dot_nt(g_ref[...], w_ref[...])
        zg = z_ref[0].astype(F32)
        zu = z_ref[1].astype(F32)
        sg = _sigmoid(zg)
        dz_ref[0] = (da * zu * (sg * (1.0 + zg * (1.0 - sg)))).astype(BF16)
        dz_ref[1] = (da * zg * sg).astype(BF16)

    return pl.pallas_call(
        body, grid=(nb, T // tm),
        in_specs=[pl.BlockSpec((tm, D), lambda b, i: (i, 0)), pl.BlockSpec((None, F, D), lambda b, i: (b, 0, 0)),
                  pl.BlockSpec((2, None, tm, F), lambda b, i: (0, b, i, 0))],
        out_specs=pl.BlockSpec((2, None, tm, F), lambda b, i: (0, b, i, 0)),
        out_shape=SDS((2, nb, T, F), BF16), compiler_params=_params(2), name=name)(do, w_out, z)


def mm_tn(a, g, a_batched, g_batched, nb, name, comm=None):
    T = a.shape[-2]
    K, N = a.shape[-1], g.shape[-1]
    tk = 2048
    nk = T // tk

    def body(a_ref, g_ref, o_ref, acc):
        k = pl.program_id(1)

        @pl.when(k == 0)
        def _():
            acc[...] = jnp.zeros_like(acc)

        acc[...] += _dot_tn(a_ref[...].astype(BF16), g_ref[...].astype(BF16))

        @pl.when(k == nk - 1)
        def _():
            o_ref[...] = acc[...].astype(BF16)

    a_spec = (pl.BlockSpec((None, tk, K), lambda b, k: (b, k, 0)) if a_batched
              else pl.BlockSpec((tk, K), lambda b, k: (k, 0)))
    g_spec = (pl.BlockSpec((None, tk, N), lambda b, k: (b, k, 0)) if g_batched
              else pl.BlockSpec((tk, N), lambda b, k: (k, 0)))
    (out,), slots = _call(
        body, grid=(nb, nk), in_specs=[a_spec, g_spec],
        out_specs=[pl.BlockSpec((None, K, N), lambda b, k: (b, 0, 0))],
        out_shape=[SDS((nb, K, N), BF16)], scratch_shapes=[pltpu.VMEM((K, N), F32)],
        args=(a, g), name=name, comm=comm)
    return out, slots


def dh_pre_bwd(dz, w, x, g, dyres, name, comm=None):
    nb, T, F = dz.shape
    D = w.shape[1]
    tm = 512

    rc = tm // ROW_SPLIT

    def body(dz_ref, w_ref, x_ref, g_ref, dy_ref, dx_ref, dg_ref):
        @pl.when(pl.program_id(0) == 0)
        def _():
            dg_ref[...] = jnp.zeros_like(dg_ref)

        accs = []
        for c in range(ROW_SPLIT):
            rows = pl.ds(c * rc, rc)
            dh = _dot_nt(dz_ref[0, rows, :].astype(BF16), w_ref[0])
            for b in range(1, nb):
                dh += _dot_nt(dz_ref[b, rows, :].astype(BF16), w_ref[b])
            accs.append(dh)
        for c, dh in enumerate(accs):
            rows = pl.ds(c * rc, rc)
            xv = x_ref[rows, :]
            rstd = lax.rsqrt(jnp.mean(xv * xv, axis=-1, keepdims=True) + NORM_EPS)
            xh = xv * rstd
            dg_ref[...] += jnp.sum(dh * xh, axis=0, keepdims=True)
            dhg = dh * g_ref[...]
            dx_ref[rows, :] = dy_ref[rows, :] + rstd * (dhg - xh * jnp.mean(dhg * xh, axis=-1, keepdims=True))

    return _call(
        body, grid=(T // tm,),
        in_specs=[pl.BlockSpec((nb, tm, F), lambda i: (0, i, 0)), _resident((nb, D, F)),
                  pl.BlockSpec((tm, D), lambda i: (i, 0)), pl.BlockSpec((1, D), lambda i: (0, 0)),
                  pl.BlockSpec((tm, D), lambda i: (i, 0))],
        out_specs=[pl.BlockSpec((tm, D), lambda i: (i, 0)), pl.BlockSpec((1, D), lambda i: (0, 0))],
        out_shape=[SDS((T, D), F32), SDS((1, D), F32)],
        args=(dz, w, x, g, dyres), name=name, comm=comm)


ATTN_GROUP = {1: 4, 4: 1, 16: 1}
ATTN_UNROLL = 4


def _attn_masks():
    qi = lax.broadcasted_iota(jnp.int32, (QBLK, QBLK), 0)
    kj = lax.broadcasted_iota(jnp.int32, (QBLK, QBLK), 1)
    cur_ok = kj <= qi
    prev_ok = kj >= qi
    dcur = (qi - kj).astype(F32)
    return cur_ok, prev_ok, dcur, dcur + float(QBLK)


def _head_slopes(p, d):
    out = []
    for hq in range(2):
        v = [float(d) * 2.0 ** -(2 * q + hq + 1) for q in range(4)]
        out.append(jnp.where(p == 0, v[0], jnp.where(p == 1, v[1], jnp.where(p == 2, v[2], v[3]))))
    return out


def _rows(start, d):
    return pl.ds(start, QBLK, stride=d) if d > 1 else pl.ds(start, QBLK)


def _pair_spec(rows, part, blk):
    return pl.BlockSpec((None, rows, PAIR_W), lambda p, n: (2 * part + p // 2, blk(n), p % 2))


def _for_query_blocks(d, groups, several):
    blocks = [(g, r) for g in range(groups) for r in range(d)]
    if len(blocks) <= 2 * ATTN_UNROLL:
        for s in range(0, len(blocks), ATTN_UNROLL):
            several(blocks[s:s + ATTN_UNROLL])
    else:
        def some(i, carry):
            several([(0, i * ATTN_UNROLL + j) for j in range(ATTN_UNROLL)])
            return carry

        lax.fori_loop(0, d // ATTN_UNROLL, some, 0)


def attn_fwd(proj, d, name):
    T = proj.shape[1]
    sb, groups = QBLK * d, ATTN_GROUP[d]
    rb = sb * groups
    nblk = T // rb

    def body(q_ref, kc_ref, kp_ref, vc_ref, vp_ref, o_ref, l_ref):
        p, n = pl.program_id(0), pl.program_id(1)
        cur_ok, prev_ok, dcur, dprev = _attn_masks()
        first_ok = jnp.logical_and(prev_ok, n > 0)
        lane_head = lax.broadcasted_iota(jnp.int32, (QBLK, PAIR_W), 1) // HEAD_DIM
        slopes = _head_slopes(p, d)

        def several(blocks):
            work = []
            for g, r in blocks:
                rows = _rows(g * sb + r, d)
                q = q_ref[rows, :]
                kc, vc = kc_ref[rows, :].astype(BF16), vc_ref[rows, :].astype(BF16)
                if g == 0:
                    prow, pok = _rows(r, d), first_ok
                    kp, vp = kp_ref[prow, :].astype(BF16), vp_ref[prow, :].astype(BF16)
                else:
                    prow, pok = _rows((g - 1) * sb + r, d), prev_ok
                    kp, vp = kc_ref[prow, :].astype(BF16), vc_ref[prow, :].astype(BF16)
                for hq in range(2):
                    qm = jnp.where(lane_head == hq, q, 0.0).astype(BF16)
                    work.append([rows, hq, pok, vc, vp, _dot_nt(qm, kc), _dot_nt(qm, kp)])
            for w in work:
                _, hq, pok, _, _, sc, sp = w
                sc = jnp.where(cur_ok, sc * 0.125 - slopes[hq] * dcur, NEG)
                sp = jnp.where(pok, sp * 0.125 - slopes[hq] * dprev, NEG)
                m = jnp.maximum(jnp.max(sc, axis=1, keepdims=True), jnp.max(sp, axis=1, keepdims=True))
                pc = jnp.exp(sc - m)
                pp = jnp.exp(sp - m)
                den = jnp.sum(pc, axis=1, keepdims=True) + jnp.sum(pp, axis=1, keepdims=True)
                w[5:] = [pc.astype(BF16), pp.astype(BF16), 1.0 / den, m + jnp.log(den)]
            for i in range(0, len(work), 2):
                o_acc = jnp.zeros((QBLK, PAIR_W), F32)
                l_acc = jnp.zeros((QBLK, PAIR_W), F32)
                for rows, hq, _, vc, vp, pc, pp, inv, lse in work[i:i + 2]:
                    hm = lane_head == hq
                    o_acc = jnp.where(hm, (_dot(pc, vc) + _dot(pp, vp)) * inv, o_acc)
                    l_acc = jnp.where(hm, lse, l_acc)
                o_ref[rows, :] = o_acc
                l_ref[rows, :] = l_acc

        _for_query_blocks(d, groups, several)

    cur = lambda part: _pair_spec(rb, part, lambda n: n)
    prv = lambda part: _pair_spec(sb, part, lambda n: jnp.maximum(n * groups - 1, 0))
    return pl.pallas_call(
        body, grid=(4, nblk), in_specs=[cur(0), cur(1), prv(1), cur(2), prv(2)], out_specs=[cur(0), cur(0)],
        out_shape=[SDS((2, T, 2 * PAIR_W), F32), SDS((2, T, 2 * PAIR_W), F32)],
        compiler_params=_params(2), name=name)(proj, proj, proj, proj, proj)


def attn_merge(os_, ls_, name):
    _, T, HW = os_[0].shape
    tm = 512

    def body(o1, o2, o3, l1, l2, l3, o_ref, l_ref):
        a, b, c = l1[...], l2[...], l3[...]
        m = jnp.maximum(jnp.maximum(a, b), c)
        ea, eb, ec = jnp.exp(a - m), jnp.exp(b - m), jnp.exp(c - m)
        s = ea + eb + ec
        o_ref[...] = (ea * o1[...] + eb * o2[...] + ec * o3[...]) * (1.0 / s)
        l_ref[...] = m + jnp.log(s)

    spec = pl.BlockSpec((None, tm, HW), lambda h, i: (h, i, 0))
    return pl.pallas_call(
        body, grid=(2, T // tm), in_specs=[spec] * 6, out_specs=[spec, spec],
        out_shape=[SDS((2, T, HW), F32), SDS((2, T, HW), F32)],
        compiler_params=_params(2), name=name)(*os_, *ls_)


def attn_bwd(proj, dcat, o, lse, acc, d, name):
    T = proj.shape[1]
    sb, groups = QBLK * d, ATTN_GROUP[d]
    rb = sb * groups
    nblk = T // rb
    has_acc = acc is not None

    def body(*refs):
        (qc_ref, qn_ref, kc_ref, kp_ref, vc_ref, vp_ref, dc_ref, dn_ref, oc_ref, on_ref, lc_ref, ln_ref) = refs[:12]
        acc_ref = refs[12] if has_acc else None
        out_ref = refs[-1]
        p, n = pl.program_id(0), pl.program_id(1)
        cur_ok, prev_ok, dcur, dprev = _attn_masks()
        first_ok = jnp.logical_and(prev_ok, n > 0)
        last_ok = jnp.logical_and(prev_ok, n < nblk - 1)
        lane_head = lax.broadcasted_iota(jnp.int32, (QBLK, PAIR_W), 1) // HEAD_DIM
        slopes = _head_slopes(p, d)

        def one(g, r):
            rows = _rows(g * sb + r, d)
            q_c, do_c, o_c, l_c = qc_ref[rows, :], dc_ref[rows, :], oc_ref[rows, :], lc_ref[rows, :]
            k_c, v_c = kc_ref[rows, :].astype(BF16), vc_ref[rows, :].astype(BF16)
            if g == 0:
                prow, pok_c = _rows(r, d), first_ok
                k_p, v_p = kp_ref[prow, :].astype(BF16), vp_ref[prow, :].astype(BF16)
            else:
                prow, pok_c = _rows((g - 1) * sb + r, d), prev_ok
                k_p, v_p = kc_ref[prow, :].astype(BF16), vc_ref[prow, :].astype(BF16)
            if g == groups - 1:
                nrow, pok_n = _rows(r, d), last_ok
                q_n, do_n, o_n, l_n = qn_ref[nrow, :], dn_ref[nrow, :], on_ref[nrow, :], ln_ref[nrow, :]
            else:
                nrow, pok_n = _rows((g + 1) * sb + r, d), prev_ok
                q_n, do_n, o_n, l_n = qc_ref[nrow, :], dc_ref[nrow, :], oc_ref[nrow, :], lc_ref[nrow, :]
            heads = []
            for hq in range(2):
                hm = lane_head == hq
                qm_c = jnp.where(hm, q_c, 0.0).astype(BF16)
                qm_n = jnp.where(hm, q_n, 0.0).astype(BF16)
                dom_c = jnp.where(hm, do_c, 0.0)
                dom_n = jnp.where(hm, do_n, 0.0)
                dd_c = jnp.sum(dom_c * o_c, axis=1, keepdims=True)
                dd_n = jnp.sum(dom_n * o_n, axis=1, keepdims=True)
                ls_c = jnp.max(jnp.where(hm, l_c, NEG), axis=1, keepdims=True)
                ls_n = jnp.max(jnp.where(hm, l_n, NEG), axis=1, keepdims=True)
                dob_c, dob_n = dom_c.astype(BF16), dom_n.astype(BF16)
                mm = [(_dot_nt(qm_c, k_c), _dot_nt(dob_c, v_c)), (_dot_nt(qm_c, k_p), _dot_nt(dob_c, v_p)),
                      (_dot_nt(qm_n, k_c), _dot_nt(dob_n, v_c))]
                heads.append(dict(hq=hq, qm_c=qm_c, qm_n=qm_n, dob_c=dob_c, dob_n=dob_n, mm=mm,
                                  dd=(dd_c, dd_c, dd_n), ls=(ls_c, ls_c, ls_n)))
            return dict(rows=rows, k_c=k_c, k_p=k_p, heads=heads, oks=(cur_ok, pok_c, pok_n))

        def several(blocks):
            work = [one(g, r) for g, r in blocks]
            for w in work:
                for h in w["heads"]:
                    slope, dist = slopes[h["hq"]], (dcur, dprev, dprev)
                    h["pr"], h["ds"] = [], []
                    for j in range(3):
                        s = jnp.where(w["oks"][j], h["mm"][j][0] * 0.125 - slope * dist[j], NEG)
                        pr = jnp.exp(s - h["ls"][j])
                        h["pr"].append(pr.astype(BF16))
                        h["ds"].append((pr * (h["mm"][j][1] - h["dd"][j])).astype(BF16))
            for w in work:
                dq = jnp.zeros((QBLK, PAIR_W), F32)
                dk = jnp.zeros((QBLK, PAIR_W), F32)
                dv = jnp.zeros((QBLK, PAIR_W), F32)
                for h in w["heads"]:
                    ds, pr = h["ds"], h["pr"]
                    dq_h = _dot(ds[0], w["k_c"]) + _dot(ds[1], w["k_p"])
                    dk += (_dot_tn(ds[0], h["qm_c"]) + _dot_tn(ds[2], h["qm_n"])) * 0.125
                    dv += _dot_tn(pr[0], h["dob_c"]) + _dot_tn(pr[2], h["dob_n"])
                    dq = jnp.where(lane_head == h["hq"], dq_h * 0.125, dq)
                for part, val in enumerate((dq, dk, dv)):
                    if has_acc:
                        val = val + acc_ref.at[part][w["rows"], :]
                    out_ref.at[part][w["rows"], :] = val

        _for_query_blocks(d, groups, several)

    cur = lambda part: _pair_spec(rb, part, lambda n: n)
    prv = lambda part: _pair_spec(sb, part, lambda n: jnp.maximum(n * groups - 1, 0))
    nxt = lambda part: _pair_spec(sb, part, lambda n: jnp.minimum((n + 1) * groups, T // sb - 1))
    full = pl.BlockSpec((3, None, rb, PAIR_W), lambda p, n: (0, p // 2, n, p % 2))
    in_specs = [cur(0), nxt(0), cur(1), prv(1), cur(2), prv(2), cur(0), nxt(0), cur(0), nxt(0), cur(0), nxt(0)]
    args = [proj, proj, proj, proj, proj, proj, dcat, dcat, o, o, lse, lse]
    if has_acc:
        in_specs.append(full)
        args.append(acc)
    return pl.pallas_call(
        body, grid=(4, nblk), in_specs=in_specs, out_specs=full,
        out_shape=SDS((3, 2, T, 2 * PAIR_W), F32), compiler_params=_params(2), name=name)(*args)


def _scan_rows(buf, tab_ref, reverse):
    n_tiles = (buf.shape[0] - 8) // 8
    per_half = HALF_STATES // SCAN_CW
    row = lax.broadcasted_iota(jnp.int32, (8, SCAN_CW), 0)
    sgn = -1.0 if reverse else 1.0

    def chunk(j, _):
        c0 = pl.multiple_of((j // per_half) * 2 * HALF_STATES + (j % per_half) * SCAN_CW, 128)
        cre = pl.ds(c0, SCAN_CW)
        cim = pl.ds(pl.multiple_of(c0 + HALF_STATES, 128), SCAN_CW)
        steps = []
        for s, k in enumerate((1, 2, 4)):
            ok, shift = (row < 8 - k, 8 - k) if reverse else (row >= k, k)
            steps.append((shift, jnp.where(ok, tab_ref[pl.ds(s, 1), cre], 0.0),
                          jnp.where(ok, sgn * tab_ref[pl.ds(s, 1), cim], 0.0)))
        trow = 16 if reverse else 8
        pr, pi = tab_ref[pl.ds(trow, 8), cre], tab_ref[pl.ds(trow, 8), cim]
        for t in range(n_tiles):
            base = 8 * (n_tiles - 1 - t) if reverse else 8 + 8 * t
            rows = pl.ds(base, 8)
            re, im = buf[rows, cre], buf[rows, cim]
            for shift, ar, ai in steps:
                sre, sim = pltpu.roll(re, shift, 0), pltpu.roll(im, shift, 0)
                re, im = re + ar * sre - ai * sim, im + ar * sim + ai * sre
            crow = pl.ds(base + 8 if reverse else base - 1, 1)
            cr, ci = buf[crow, cre], buf[crow, cim]
            buf[rows, cre] = re + pr * cr - pi * ci
            buf[rows, cim] = im + pr * ci + pi * cr
        return 0

    lax.fori_loop(0, 2 * per_half, chunk, 0)


def ssm_fwd(proj, bh, ch, apow, dskip, name, comm=None):
    _, T, C = proj.shape
    tm = SCAN_TM
    SW = 4 * HALF_STATES

    def body(u_ref, bh_ref, ch_ref, tab_ref, dsk_ref, y_ref, s_ref, buf):
        @pl.when(pl.program_id(0) == 0)
        def _():
            buf[pl.ds(0, 8), :] = jnp.zeros((8, SW), F32)

        for h in range(2):
            buf[pl.ds(8, tm), pl.ds(h * 2 * HALF_STATES, 2 * HALF_STATES)] = _dot(u_ref[h].astype(BF16), bh_ref[h])
        _scan_rows(buf, tab_ref, False)
        s_ref[...] = buf[pl.ds(8, tm), :]
        buf[pl.ds(0, 8), :] = buf[pl.ds(tm, 8), :]
        for h in range(2):
            sv = s_ref[:, pl.ds(h * 2 * HALF_STATES, 2 * HALF_STATES)].astype(BF16)
            y_ref[h] = _dot(sv, ch_ref[h]) + dsk_ref[h] * u_ref[h]

    return _call(
        body, grid=(T // tm,),
        in_specs=[pl.BlockSpec((2, tm, C), lambda i: (3, i, 0)),
                  pl.BlockSpec((2, C, 2 * HALF_STATES), lambda i: (0, 0, 0)),
                  pl.BlockSpec((2, 2 * HALF_STATES, C), lambda i: (0, 0, 0)),
                  pl.BlockSpec((24, SW), lambda i: (0, 0)),
                  pl.BlockSpec((2, 1, C), lambda i: (0, 0, 0))],
        out_specs=[pl.BlockSpec((2, tm, C), lambda i: (0, i, 0)), pl.BlockSpec((tm, SW), lambda i: (i, 0))],
        out_shape=[SDS((2, T, C), F32), SDS((T, SW), F32)],
        scratch_shapes=[pltpu.VMEM((tm + 8, SW), F32)],
        args=(proj, bh, ch, apow, dskip), name=name, comm=comm)


def ssm_bwd(dy, proj, st, bh, ch, apow, dskip, name, comm=None):
    _, T, C = proj.shape
    tm = SCAN_TM
    nt = T // tm
    SW = 4 * HALF_STATES
    HS2 = 2 * HALF_STATES

    def body(dy_ref, u_ref, s_ref, sp_ref, bh_ref, ch_ref, tab_ref, dsk_ref,
             du_ref, da_ref, dbh_ref, dch_ref, dd_ref, lam):
        i = pl.program_id(0)

        @pl.when(i == 0)
        def _():
            lam[pl.ds(tm, 8), :] = jnp.zeros((8, SW), F32)
            da_ref[...] = jnp.zeros_like(da_ref)
            dbh_ref[...] = jnp.zeros_like(dbh_ref)
            dch_ref[...] = jnp.zeros_like(dch_ref)
            dd_ref[...] = jnp.zeros_like(dd_ref)

        for h in range(2):
            lam[pl.ds(0, tm), pl.ds(h * HS2, HS2)] = _dot_nt(dy_ref[h].astype(BF16), ch_ref[h])
        _scan_rows(lam, tab_ref, True)

        first = i == nt - 1
        per_half = HALF_STATES // SCAN_CW

        def chunk(j, _):
            c0 = pl.multiple_of((j // per_half) * HS2 + (j % per_half) * SCAN_CW, 128)
            cre, cim = pl.ds(c0, SCAN_CW), pl.ds(pl.multiple_of(c0 + HALF_STATES, 128), SCAN_CW)
            row = lax.broadcasted_iota(jnp.int32, (tm, SCAN_CW), 0)
            pre = jnp.where(first, 0.0, sp_ref[pl.ds(7, 1), cre])
            pim = jnp.where(first, 0.0, sp_ref[pl.ds(7, 1), cim])
            spr = jnp.where(row == 0, pre, pltpu.roll(s_ref[:, cre], 1, 0))
            spi = jnp.where(row == 0, pim, pltpu.roll(s_ref[:, cim], 1, 0))
            lr, li = lam[pl.ds(0, tm), cre], lam[pl.ds(0, tm), cim]
            da_ref[:, cre] += jnp.sum(lr * spr + li * spi, axis=0, keepdims=True)
            da_ref[:, cim] += jnp.sum(li * spr - lr * spi, axis=0, keepdims=True)
            return 0

        lax.fori_loop(0, 2 * per_half, chunk, 0)
        lam[pl.ds(tm, 8), :] = lam[pl.ds(0, 8), :]

        for h in range(2):
            lb = lam[pl.ds(0, tm), pl.ds(h * HS2, HS2)].astype(BF16)
            dyv, uv = dy_ref[h], u_ref[h]
            du_ref[h] = _dot_nt(lb, bh_ref[h]) + dsk_ref[h] * dyv
            dbh_ref[h] += _dot_tn(uv.astype(BF16), lb)
            dch_ref[h] += _dot_tn(s_ref[:, pl.ds(h * HS2, HS2)].astype(BF16), dyv.astype(BF16))
            dd_ref[h] += jnp.sum(dyv * uv, axis=0, keepdims=True)

    rev = lambda i: nt - 1 - i
    return _call(
        body, grid=(nt,),
        in_specs=[pl.BlockSpec((2, tm, C), lambda i: (0, rev(i), 0)),
                  pl.BlockSpec((2, tm, C), lambda i: (3, rev(i), 0)),
                  pl.BlockSpec((tm, SW), lambda i: (rev(i), 0)),
                  pl.BlockSpec((8, SW), lambda i: (jnp.maximum(rev(i) * (tm // 8) - 1, 0), 0)),
                  pl.BlockSpec((2, C, HS2), lambda i: (0, 0, 0)),
                  pl.BlockSpec((2, HS2, C), lambda i: (0, 0, 0)),
                  pl.BlockSpec((24, SW), lambda i: (0, 0)),
                  pl.BlockSpec((2, 1, C), lambda i: (0, 0, 0))],
        out_specs=[pl.BlockSpec((2, tm, C), lambda i: (0, rev(i), 0)),
                   pl.BlockSpec((1, SW), lambda i: (0, 0)),
                   pl.BlockSpec((2, C, HS2), lambda i: (0, 0, 0)),
                   pl.BlockSpec((2, HS2, C), lambda i: (0, 0, 0)),
                   pl.BlockSpec((2, 1, C), lambda i: (0, 0, 0))],
        out_shape=[SDS((2, T, C), F32), SDS((1, SW), F32), SDS((2, C, HS2), F32), SDS((2, HS2, C), F32),
                   SDS((2, 1, C), F32)],
        scratch_shapes=[pltpu.VMEM((tm + 8, SW), F32)],
        args=(dy, proj, st, st, bh, ch, apow, dskip), name=name, comm=comm)


_GELU_C = math.sqrt(2.0 / math.pi)


def _gelu(x):
    t = jnp.tanh(_GELU_C * (x + 0.044715 * x * x * x))
    return 0.5 * x * (1.0 + t), t


def glu_fwd(y, w, b, name):
    _, T, C = y.shape
    tm = 512

    def body(y_ref, w_ref, b_ref, o_ref, lg_ref):
        y0, _ = _gelu(y_ref[0])
        y1, _ = _gelu(y_ref[1])
        lg = _dot(y0.astype(BF16), w_ref[0]) + _dot(y1.astype(BF16), w_ref[1]) + b_ref[...]
        sg = _sigmoid(lg)
        o_ref[0] = y0 * sg[:, :C]
        o_ref[1] = y1 * sg[:, C:]
        lg_ref[0] = lg[:, :C]
        lg_ref[1] = lg[:, C:]

    return pl.pallas_call(
        body, grid=(T // tm,),
        in_specs=[pl.BlockSpec((2, tm, C), lambda i: (0, i, 0)), pl.BlockSpec((2, C, 2 * C), lambda i: (0, 0, 0)),
                  pl.BlockSpec((1, 2 * C), lambda i: (0, 0))],
        out_specs=[pl.BlockSpec((2, tm, C), lambda i: (0, i, 0)), pl.BlockSpec((2, tm, C), lambda i: (0, i, 0))],
        out_shape=[SDS((2, T, C), F32), SDS((2, T, C), F32)], compiler_params=_params(1), name=name)(y, w, b)


def glu_bwd(dcat, y, lg, w, name):
    _, T, C = y.shape
    tm = 512

    def body(d_ref, y_ref, lg_ref, w_ref, dy_ref, dw_ref, db_ref):
        @pl.when(pl.program_id(0) == 0)
        def _():
            dw_ref[...] = jnp.zeros_like(dw_ref)
            db_ref[...] = jnp.zeros_like(db_ref)

        y2, th, sg, dlg = [], [], [], []
        for h in range(2):
            yy, tt = _gelu(y_ref[h])
            ss = _sigmoid(lg_ref[h])
            y2.append(yy)
            th.append(tt)
            sg.append(ss)
            dlg.append(d_ref[h] * yy * ss * (1.0 - ss))
        dl = jnp.concatenate(dlg, axis=1)
        dlb = dl.astype(BF16)
        db_ref[...] += jnp.sum(dl, axis=0, keepdims=True)
        for h in range(2):
            dy2 = d_ref[h] * sg[h] + _dot_nt(dlb, w_ref[h])
            yv = y_ref[h]
            dgelu = 0.5 * (1.0 + th[h]) + 0.5 * yv * (1.0 - th[h] * th[h]) * _GELU_C * (1.0 + 3 * 0.044715 * yv * yv)
            dy_ref[h] = dy2 * dgelu
            dw_ref[h] += _dot_tn(y2[h].astype(BF16), dlb)

    return pl.pallas_call(
        body, grid=(T // tm,),
        in_specs=[pl.BlockSpec((2, tm, C), lambda i: (1, i, 0)), pl.BlockSpec((2, tm, C), lambda i: (0, i, 0)),
                  pl.BlockSpec((2, tm, C), lambda i: (0, i, 0)), pl.BlockSpec((2, C, 2 * C), lambda i: (0, 0, 0))],
        out_specs=[pl.BlockSpec((2, tm, C), lambda i: (0, i, 0)), pl.BlockSpec((2, C, 2 * C), lambda i: (0, 0, 0)),
                   pl.BlockSpec((1, 2 * C), lambda i: (0, 0))],
        out_shape=[SDS((2, T, C), F32), SDS((2, C, 2 * C), F32), SDS((1, 2 * C), F32)],
        compiler_params=_params(1), name=name)(dcat, y, lg, w)


def adamw(w, m, v, slots, name):
    R, C = w.shape
    tr = R
    for cand in (512, 256, 128, 64, 32, 16, 8):
        if R % cand == 0 and cand * C * 4 <= 2 * 1024 * 1024:
            tr = cand
            break
    c1 = 1.0 / (1.0 - ADAM_B1 ** ADAM_STEP)
    c2 = 1.0 / (1.0 - ADAM_B2 ** ADAM_STEP)

    def body(w_ref, m_ref, v_ref, s_ref, g_ref, d_ref, nm_ref, nv_ref):
        g = s_ref[0].astype(F32)
        for j in range(1, N_DEV):
            g = g + s_ref[j].astype(F32)
        nm = ADAM_B1 * m_ref[...] + (1.0 - ADAM_B1) * g
        nv = ADAM_B2 * v_ref[...] + (1.0 - ADAM_B2) * (g * g)
        g_ref[...] = g
        nm_ref[...] = nm
        nv_ref[...] = nv
        d_ref[...] = -ADAM_LR * ((nm * c1) / (jnp.sqrt(nv * c2) + ADAM_EPS) + ADAM_WD * w_ref[...])

    spec = pl.BlockSpec((tr, C), lambda i: (i, 0))
    return pl.pallas_call(
        body, grid=(R // tr,),
        in_specs=[spec, spec, spec, pl.BlockSpec((N_DEV, tr, C), lambda i: (0, i, 0))],
        out_specs=[spec] * 4, out_shape=[SDS((R, C), F32)] * 4, compiler_params=_params(1), name=name)(w, m, v, slots)


def _discretise(a_re, a_im, log_dt, b_re, b_im):
    dt = jnp.exp(log_dt)[:, None]
    e = jnp.exp(dt * a_re)
    ar, ai = e * jnp.cos(dt * a_im), e * jnp.sin(dt * a_im)
    den = a_re * a_re + a_im * a_im
    nr, ni = ar - 1.0, ai
    wr = (nr * a_re + ni * a_im) / den
    wi = (ni * a_re - nr * a_im) / den
    bbr = wr[..., None] * b_re - wi[..., None] * b_im
    bbi = wr[..., None] * b_im + wi[..., None] * b_re
    return ar, ai, bbr, bbi


def _block_diag(t):
    eye = jnp.eye(16, dtype=t.dtype).reshape(1, 16, 1, 16, 1)
    r, c = t.shape[1], t.shape[2]
    return (t.reshape(2, 16, r, 1, c) * eye).reshape(2, 16 * r, 16 * c)


def _diag_blocks(m, r, c):
    eye = jnp.eye(16, dtype=m.dtype).reshape(1, 16, 1, 16, 1)
    return jnp.sum(m.reshape(2, 16, r, 16, c) * eye, axis=3).reshape(32, r, c)


def _state_vec(re, im):
    return jnp.stack([re.reshape(2, HALF_STATES), im.reshape(2, HALF_STATES)], axis=1).reshape(-1)


BIG = ("ffn1_w_in", "ffn1_w_out", "w_mix_in", "w_glu", "w_mix_out", "ffn2_w_in", "ffn2_w_out")
WEIGHTS = ("ffn1_pre_g", "ffn1_w_in", "ffn1_w_out", "ffn1_post_g", "mix_pre_g", "w_mix_in", "a_re", "a_im", "log_dt",
           "b_re", "b_im", "c_re", "c_im", "d_skip", "w_glu", "b_glu", "w_mix_out", "mix_post_g", "ffn2_pre_g",
           "ffn2_w_in", "ffn2_w_out", "ffn2_post_g")
SMALL = tuple(n for n in WEIGHTS if n not in BIG)
PACK_COLS = 1024


def _pack(parts):
    flat = jnp.concatenate([p.reshape(-1) for p in parts])
    rows = -(-flat.shape[0] // (8 * PACK_COLS)) * 8
    return jnp.pad(flat, (0, rows * PACK_COLS - flat.shape[0])).reshape(rows, PACK_COLS)


def _unpack(packed, shapes):
    flat, out, off = packed.reshape(-1), [], 0
    for s in shapes:
        n = math.prod(s)
        out.append(flat[off:off + n].reshape(s))
        off += n
    return out


def _gather(names, wb):
    return [wb[n] for n in names], [False] * len(names)


def _ffn_bwd(dy, saved, x, pre_g, w_in, w_out4, post_g, tag):
    h, z, a, o = saved
    T = x.shape[0]
    do, dg_post = post_bwd(dy, o, post_g, 0.5, f"{tag}_post_bwd")
    dz = ffn_dact(do, w_out4, z, f"{tag}_dact")
    dz8 = dz.reshape(8, T, dz.shape[-1])
    dw_out, _ = mm_tn(a, do, True, False, 4, f"{tag}_dwout")
    dw_in, (s_out,) = mm_tn(h, dz8, False, True, 8, f"{tag}_dwin", comm=([dw_out.reshape(8, -1, D_MODEL)], [True]))
    (dx, dg_pre), (s_in,) = dh_pre_bwd(dz8, w_in, x, pre_g, dy, f"{tag}_dh", comm=([dw_in], [True]))
    return dx, dg_pre, dg_post, (s_in, s_out)


def local_step(x, tgt, sp, wb):
    T = x.shape[0]
    ar, ai, bbr, bbi = _discretise(sp["a_re"], sp["a_im"], sp["log_dt"], sp["b_re"], sp["b_im"])
    powers = [(ar, ai)]
    for _ in range(7):
        pr, pi = powers[-1]
        powers.append((pr * ar - pi * ai, pr * ai + pi * ar))
    zero = jnp.zeros_like(ar)
    rows = [_state_vec(*powers[k - 1]) for k in (1, 2, 4)] + [_state_vec(zero, zero)] * 5
    rows += [_state_vec(pr, pi) for pr, pi in powers]
    rows += [_state_vec(pr, -pi) for pr, pi in reversed(powers)]
    apow = jnp.stack(rows)
    bh = jnp.concatenate([_block_diag(bbr.transpose(0, 2, 1)), _block_diag(bbi.transpose(0, 2, 1))], axis=2)
    ch = jnp.concatenate([_block_diag(sp["c_re"].transpose(0, 2, 1)), _block_diag(-sp["c_im"].transpose(0, 2, 1))], axis=1)
    bh, ch = bh.astype(BF16), ch.astype(BF16)
    dskip = sp["d_skip"].reshape(2, 1, 256)

    (w1_in,) = exchange(*_gather(["ffn1_w_in"], wb), "gather_w1in")
    h1 = rms_fwd(x, sp["ffn1_pre_g"], "ffn1_rms")
    (z1, a1), (w1_out, w_mi) = ffn_in(h1, w1_in, "ffn1_in", comm=_gather(["ffn1_w_out", "w_mix_in"], wb))
    w1_out4 = w1_out.reshape(4, -1, D_MODEL)
    (o1, x1), (w_glu, w_mo) = mm_acc_norm(
        a1, w1_out4, x, sp["ffn1_post_g"], 0.5, "ffn1_out", comm=_gather(["w_glu", "w_mix_out"], wb))
    w_glu2, w_mo4 = w_glu.reshape(2, 256, 512), w_mo.reshape(4, 256, D_MODEL)
    h2 = rms_fwd(x1, sp["mix_pre_g"], "mix_rms")
    proj = mm_nn_b(h2, w_mi, 8, F32, "mix_proj")
    (y_ssm, states), (w2_in, w2_out) = ssm_fwd(
        proj, bh, ch, apow, dskip, "ssm_fwd", comm=_gather(["ffn2_w_in", "ffn2_w_out"], wb))
    w2_out4 = w2_out.reshape(4, -1, D_MODEL)
    os_, ls_ = [], []
    for d in DILATIONS:
        o_d, l_d = attn_fwd(proj, d, f"attn_fwd_d{d}")
        os_.append(o_d)
        ls_.append(l_d)
    o_att, lse = attn_merge(os_, ls_, "attn_merge")
    o_ssm, lg = glu_fwd(y_ssm, w_glu2, sp["b_glu"], "glu_fwd")
    cat = jnp.concatenate([o_att, o_ssm], axis=0)
    (mixed, x2), _ = mm_acc_norm(cat, w_mo4, x1, sp["mix_post_g"], 1.0, "mix_out")
    h3 = rms_fwd(x2, sp["ffn2_pre_g"], "ffn2_rms")
    (z3, a3), _ = ffn_in(h3, w2_in, "ffn2_in")
    (o3, x3), _ = mm_acc_norm(a3, w2_out4, x2, sp["ffn2_post_g"], 0.5, "ffn2_out")
    dy3, sq = loss_head(x3, tgt, "loss_head")

    dx2, dg_f2pre, dg_f2post, (s_w2in, s_w2out) = _ffn_bwd(
        dy3, (h3, z3, a3, o3), x2, sp["ffn2_pre_g"], w2_in, w2_out4, sp["ffn2_post_g"], "ffn2")
    dmixed, dg_mpost = post_bwd(dx2, mixed, sp["mix_post_g"], 1.0, "mix_post_bwd")
    dcat = mm_nt_b(dmixed, w_mo4, "mix_dcat")
    dw_mo, _ = mm_tn(cat, dmixed, True, False, 4, "mix_dwout")
    dy_ssm, dw_glu, db_glu = glu_bwd(dcat, y_ssm, lg, w_glu2, "glu_bwd")
    (du, da, dbh, dch, dd), (s_wmo, s_wglu) = ssm_bwd(
        dy_ssm, proj, states, bh, ch, apow, dskip, "ssm_bwd",
        comm=([dw_mo.reshape(8, 128, D_MODEL), dw_glu.astype(BF16).reshape(8, 64, 512)], [True, True]))
    dqkv = None
    for d in DILATIONS:
        dqkv = attn_bwd(proj, dcat, o_att, lse, dqkv, d, f"attn_bwd_d{d}")
    dproj = jnp.concatenate([dqkv.reshape(6, T, 256), du], axis=0)
    dw_mi, _ = mm_tn(h2, dproj, False, True, 8, "mix_dwin")
    (dx1, dg_mpre), (s_wmi,) = dh_pre_bwd(
        dproj, w_mi, x1, sp["mix_pre_g"], dx2, "mix_dh", comm=([dw_mi], [True]))
    dx0, dg_f1pre, dg_f1post, (s_w1in, s_w1out) = _ffn_bwd(
        dx1, (h1, z1, a1, o1), x, sp["ffn1_pre_g"], w1_in, w1_out4, sp["ffn1_post_g"], "ffn1")

    da4 = da.reshape(2, 2, HALF_STATES)
    d_ar, d_ai = da4[:, 0].reshape(32, N_STATE), da4[:, 1].reshape(32, N_STATE)
    d_bbr = _diag_blocks(dbh[:, :, :HALF_STATES], 16, N_STATE).transpose(0, 2, 1)
    d_bbi = _diag_blocks(dbh[:, :, HALF_STATES:], 16, N_STATE).transpose(0, 2, 1)
    _, disc_vjp = jax.vjp(_discretise, sp["a_re"], sp["a_im"], sp["log_dt"], sp["b_re"], sp["b_im"])
    g_are, g_aim, g_ldt, g_bre, g_bim = disc_vjp((d_ar, d_ai, d_bbr, d_bbi))
    g_cre = _diag_blocks(dch[:, :HALF_STATES], N_STATE, 16).transpose(0, 2, 1)
    g_cim = -_diag_blocks(dch[:, HALF_STATES:], N_STATE, 16).transpose(0, 2, 1)
    small = {
        "ffn1_pre_g": dg_f1pre, "ffn1_post_g": dg_f1post, "mix_pre_g": dg_mpre, "a_re": g_are, "a_im": g_aim,
        "log_dt": g_ldt, "b_re": g_bre, "b_im": g_bim, "c_re": g_cre, "c_im": g_cim, "d_skip": dd.reshape(1, 512),
        "b_glu": db_glu, "mix_post_g": dg_mpost, "ffn2_pre_g": dg_f2pre, "ffn2_post_g": dg_f2post,
    }
    (small_slots,) = exchange([_pack([small[n] for n in SMALL])], [False], "exchange_small")
    big_slots = {"ffn1_w_in": s_w1in, "ffn1_w_out": s_w1out, "w_mix_in": s_wmi, "w_glu": s_wglu, "w_mix_out": s_wmo,
                 "ffn2_w_in": s_w2in, "ffn2_w_out": s_w2out}
    return sq, dx0, big_slots, small_slots


def kernel(x, ffn1_pre_g, ffn1_w_in, ffn1_w_out, ffn1_post_g, mix_pre_g, w_mix_in, a_re, a_im, log_dt, b_re, b_im, c_re, c_im, d_skip, w_glu, b_glu, w_mix_out, mix_post_g, ffn2_pre_g, ffn2_w_in, ffn2_w_out, ffn2_post_g, loss_target, m_ffn1_pre_g, m_ffn1_w_in, m_ffn1_w_out, m_ffn1_post_g, m_mix_pre_g, m_w_mix_in, m_a_re, m_a_im, m_log_dt, m_b_re, m_b_im, m_c_re, m_c_im, m_d_skip, m_w_glu, m_b_glu, m_w_mix_out, m_mix_post_g, m_ffn2_pre_g, m_ffn2_w_in, m_ffn2_w_out, m_ffn2_post_g, v_ffn1_pre_g, v_ffn1_w_in, v_ffn1_w_out, v_ffn1_post_g, v_mix_pre_g, v_w_mix_in, v_a_re, v_a_im, v_log_dt, v_b_re, v_b_im, v_c_re, v_c_im, v_d_skip, v_w_glu, v_b_glu, v_w_mix_out, v_mix_post_g, v_ffn2_pre_g, v_ffn2_w_in, v_ffn2_w_out, v_ffn2_post_g):
    args = dict(locals())
    w = {n: args[n][0] for n in WEIGHTS}
    m = {n: args["m_" + n][0] for n in WEIGHTS}
    v = {n: args["v_" + n][0] for n in WEIGHTS}

    wb = {n: w[n].astype(BF16) for n in BIG}
    sp = {n: w[n] for n in SMALL}
    for n in ("ffn1_pre_g", "ffn1_post_g", "mix_pre_g", "mix_post_g", "ffn2_pre_g", "ffn2_post_g", "b_glu", "d_skip"):
        sp[n] = w[n].reshape(1, -1)

    sq, grad_x, big_slots, small_slots = local_step(x[0], loss_target[0], sp, wb)
    loss = lax.psum(0.5 / D_MODEL * jnp.sum(sq), ("x", "y", "c"))

    outs = {}
    for n in BIG:
        shp = w[n].shape
        r2 = lambda t: t.reshape(-1, shp[-1])
        res = adamw(r2(w[n]), r2(m[n]), r2(v[n]), big_slots[n].reshape(N_DEV, -1, shp[-1]), f"adamw_{n}")
        outs[n] = [t.reshape((1,) + shp) for t in res]
    res = adamw(_pack([w[n] for n in SMALL]), _pack([m[n] for n in SMALL]), _pack([v[n] for n in SMALL]),
                small_slots, "adamw_small")
    shapes = [(1,) + w[n].shape for n in SMALL]
    unpacked = [_unpack(t, shapes) for t in res]
    for j, n in enumerate(SMALL):
        outs[n] = [unpacked[k][j] for k in range(4)]

    result = [loss, grad_x[None]]
    for k in range(4):
        result += [outs[n][k] for n in WEIGHTS]
    return tuple(result)
```

```python
import functools
import math

import jax
import jax.numpy as jnp
from jax import lax
from jax.experimental import pallas as pl
from jax.experimental.pallas import tpu as pltpu

F32, BF16 = jnp.float32, jnp.bfloat16
SDS = jax.ShapeDtypeStruct

D_MODEL = 1024
N_DEV = 8
HEAD_DIM = 64
PAIR_W = 128
QBLK = 128
DILATIONS = (1, 4, 16)
N_STATE = 64
HALF_STATES = 1024
NORM_EPS = 1e-6
NEG = -1e30
VMEM_LIMIT = 56 * 1024 * 1024
ADAM_LR, ADAM_B1, ADAM_B2, ADAM_EPS, ADAM_WD, ADAM_STEP = 1e-3, 0.9, 0.999, 1e-8, 0.01, 10
SCAN_TM = 256
SCAN_CW = 512


def _params(n_grid):
    return pltpu.CompilerParams(dimension_semantics=("arbitrary",) * n_grid, vmem_limit_bytes=VMEM_LIMIT)


def _dot(a, b):
    return jnp.dot(a, b, preferred_element_type=F32)


def _dot_nt(a, b):
    return lax.dot_general(a, b, (((1,), (1,)), ((), ())), preferred_element_type=F32)


def _dot_tn(a, b):
    return lax.dot_general(a, b, (((0,), (0,)), ((), ())), preferred_element_type=F32)


def _sigmoid(v):
    return 1.0 / (1.0 + jnp.exp(-v))


def _resident(shape):
    return pl.BlockSpec(shape, lambda i: (0,) * len(shape), pipeline_mode=pl.Buffered(1))


ROW_SPLIT = 2


def _exchange_phase(ins, outs, scatter, sems, start):
    send_sems, recv_sems, loc_sems = sems
    x, y, c = lax.axis_index("x"), lax.axis_index("y"), lax.axis_index("c")
    me = 4 * x + 2 * y + c
    own_copies, sends, arrivals = [], [], []
    for i in range(len(ins)):
        own = ins[i].at[me] if scatter[i] else ins[i]
        own_copies.append(pltpu.make_async_copy(own, outs[i].at[me], loc_sems.at[i]))
        for k in range(1, N_DEV):
            px = 1 - x if k & 4 else x
            py = 1 - y if k & 2 else y
            pc = 1 - c if k & 1 else c
            peer = 4 * px + 2 * py + pc
            src = ins[i].at[peer] if scatter[i] else ins[i]
            common = dict(src_ref=src, send_sem=send_sems.at[i, k - 1], recv_sem=recv_sems.at[i, k - 1],
                          device_id=(px, py, pc), device_id_type=pl.DeviceIdType.MESH)
            sends.append(pltpu.make_async_remote_copy(dst_ref=outs[i].at[me], **common))
            if not start:
                arrivals.append(pltpu.make_async_remote_copy(dst_ref=outs[i].at[peer], **common))
    if start:
        for cp in own_copies + sends:
            cp.start()
    else:
        for cp in arrivals:
            cp.wait_recv()
        for cp in sends:
            cp.wait_send()
        for cp in own_copies:
            cp.wait()


def _comm_shapes(arrs, scatter):
    n = len(arrs)
    out_shapes = [SDS(a.shape if scatter[i] else (N_DEV,) + a.shape, a.dtype) for i, a in enumerate(arrs)]
    sems = [pltpu.SemaphoreType.DMA((n, N_DEV - 1)), pltpu.SemaphoreType.DMA((n, N_DEV - 1)),
            pltpu.SemaphoreType.DMA((n,))]
    return out_shapes, sems


def exchange(arrs, scatter, name):
    n = len(arrs)
    out_shapes, sems = _comm_shapes(arrs, scatter)

    def body(*refs):
        ins, outs, sem_refs = refs[:n], refs[n:2 * n], refs[2 * n:]
        _exchange_phase(ins, outs, scatter, sem_refs, True)
        _exchange_phase(ins, outs, scatter, sem_refs, False)

    anyspec = pl.BlockSpec(memory_space=pl.ANY)
    return pl.pallas_call(
        body, in_specs=[anyspec] * n, out_specs=[anyspec] * n, out_shape=out_shapes, scratch_shapes=sems,
        compiler_params=pltpu.CompilerParams(has_side_effects=True), name=name)(*arrs)


def _call(body, *, grid, in_specs, out_specs, out_shape, args, name, scratch_shapes=(), comm=None):
    n_grid, scratch_shapes = len(grid), list(scratch_shapes)
    if comm is None:
        outs = pl.pallas_call(body, grid=grid, in_specs=in_specs, out_specs=out_specs, out_shape=out_shape,
                              scratch_shapes=scratch_shapes, compiler_params=_params(n_grid), name=name)(*args)
        return outs, []
    arrs, scatter = comm
    nc, n_in, n_out, n_sc = len(arrs), len(in_specs), len(out_specs), len(scratch_shapes)
    comm_shapes, sems = _comm_shapes(arrs, scatter)

    def wrapped(*refs):
        ins, cins = refs[:n_in], refs[n_in:n_in + nc]
        o0 = n_in + nc
        outs, couts = refs[o0:o0 + n_out], refs[o0 + n_out:o0 + n_out + nc]
        s0 = o0 + n_out + nc
        scratch, sem_refs = refs[s0:s0 + n_sc], refs[s0 + n_sc:]
        first = functools.reduce(jnp.logical_and, [pl.program_id(k) == 0 for k in range(n_grid)])
        last = functools.reduce(jnp.logical_and, [pl.program_id(k) == grid[k] - 1 for k in range(n_grid)])

        @pl.when(first)
        def _():
            _exchange_phase(cins, couts, scatter, sem_refs, True)

        body(*ins, *outs, *scratch)

        @pl.when(last)
        def _():
            _exchange_phase(cins, couts, scatter, sem_refs, False)

    anyspec = pl.BlockSpec(memory_space=pl.ANY)
    res = pl.pallas_call(
        wrapped, grid=grid, in_specs=list(in_specs) + [anyspec] * nc, out_specs=list(out_specs) + [anyspec] * nc,
        out_shape=list(out_shape) + comm_shapes, scratch_shapes=scratch_shapes + sems,
        compiler_params=pltpu.CompilerParams(dimension_semantics=("arbitrary",) * n_grid,
                                             vmem_limit_bytes=VMEM_LIMIT, has_side_effects=True),
        name=name)(*args, *arrs)
    return res[:n_out], res[n_out:]


def rms_fwd(x, g, name):
    T, D = x.shape
    tm = 512

    def body(x_ref, g_ref, h_ref):
        xv = x_ref[...]
        r = lax.rsqrt(jnp.mean(xv * xv, axis=-1, keepdims=True) + NORM_EPS)
        h_ref[...] = (xv * r * g_ref[...]).astype(BF16)

    return pl.pallas_call(
        body, grid=(T // tm,),
        in_specs=[pl.BlockSpec((tm, D), lambda i: (i, 0)), pl.BlockSpec((1, D), lambda i: (0, 0))],
        out_specs=pl.BlockSpec((tm, D), lambda i: (i, 0)),
        out_shape=SDS((T, D), BF16), compiler_params=_params(1), name=name)(x, g)


def ffn_in(h, w, name, comm=None):
    T, D = h.shape
    F = w.shape[2]
    tm = 512

    rc = tm // ROW_SPLIT

    def body(h_ref, wg_ref, wu_ref, z_ref, a_ref):
        zs = []
        for c in range(ROW_SPLIT):
            hv = h_ref[pl.ds(c * rc, rc), :]
            zs.append((_dot(hv, wg_ref[...]), _dot(hv, wu_ref[...])))
        for c, (zg, zu) in enumerate(zs):
            rows = pl.ds(c * rc, rc)
            z_ref[0, rows, :] = zg.astype(BF16)
            z_ref[1, rows, :] = zu.astype(BF16)
            a_ref[rows, :] = (zg * _sigmoid(zg) * zu).astype(BF16)

    return _call(
        body, grid=(4, T // tm),
        in_specs=[pl.BlockSpec((tm, D), lambda j, i: (i, 0)),
                  pl.BlockSpec((None, D, F), lambda j, i: (j, 0, 0)),
                  pl.BlockSpec((None, D, F), lambda j, i: (j + 4, 0, 0))],
        out_specs=[pl.BlockSpec((2, None, tm, F), lambda j, i: (0, j, i, 0)),
                   pl.BlockSpec((None, tm, F), lambda j, i: (j, i, 0))],
        out_shape=[SDS((2, 4, T, F), BF16), SDS((4, T, F), BF16)],
        args=(h, w, w), name=name, comm=comm)


def mm_nn_b(a, w, nb, out_dtype, name):
    T, K = a.shape
    N = w.shape[2]
    tm = 512

    def body(a_ref, w_ref, o_ref):
        av = a_ref[...]
        for b in range(nb):
            o_ref[b] = _dot(av, w_ref[b]).astype(out_dtype)

    return pl.pallas_call(
        body, grid=(T // tm,),
        in_specs=[pl.BlockSpec((tm, K), lambda i: (i, 0)), _resident((nb, K, N))],
        out_specs=pl.BlockSpec((nb, tm, N), lambda i: (0, i, 0)),
        out_shape=SDS((nb, T, N), out_dtype), compiler_params=_params(1), name=name)(a, w)


def mm_acc_norm(a, w, xres, g, scale, name, comm=None):
    nb, T, K = a.shape
    D = w.shape[2]
    tm = 512

    rc = tm // ROW_SPLIT

    def body(a_ref, w_ref, x_ref, g_ref, o_ref, y_ref):
        accs = []
        for c in range(ROW_SPLIT):
            rows = pl.ds(c * rc, rc)
            o = _dot(a_ref[0, rows, :].astype(BF16), w_ref[0])
            for b in range(1, nb):
                o += _dot(a_ref[b, rows, :].astype(BF16), w_ref[b])
            accs.append(o)
        for c, o in enumerate(accs):
            rows = pl.ds(c * rc, rc)
            r = lax.rsqrt(jnp.mean(o * o, axis=-1, keepdims=True) + NORM_EPS)
            o_ref[rows, :] = o
            y_ref[rows, :] = x_ref[rows, :] + scale * (o * r * g_ref[...])

    return _call(
        body, grid=(T // tm,),
        in_specs=[pl.BlockSpec((nb, tm, K), lambda i: (0, i, 0)), _resident((nb, K, D)),
                  pl.BlockSpec((tm, D), lambda i: (i, 0)), pl.BlockSpec((1, D), lambda i: (0, 0))],
        out_specs=[pl.BlockSpec((tm, D), lambda i: (i, 0)), pl.BlockSpec((tm, D), lambda i: (i, 0))],
        out_shape=[SDS((T, D), F32), SDS((T, D), F32)],
        args=(a, w, xres, g), name=name, comm=comm)


def loss_head(y, tgt, name):
    T, D = y.shape
    tm = 512

    def body(y_ref, t_ref, dy_ref, sq_ref):
        @pl.when(pl.program_id(0) == 0)
        def _():
            sq_ref[...] = jnp.zeros_like(sq_ref)

        e = y_ref[...] - t_ref[...]
        dy_ref[...] = e * (1.0 / D)
        sq_ref[...] += jnp.sum(e * e, axis=0, keepdims=True)

    return pl.pallas_call(
        body, grid=(T // tm,),
        in_specs=[pl.BlockSpec((tm, D), lambda i: (i, 0)), pl.BlockSpec((tm, D), lambda i: (i, 0))],
        out_specs=[pl.BlockSpec((tm, D), lambda i: (i, 0)), pl.BlockSpec((1, D), lambda i: (0, 0))],
        out_shape=[SDS((T, D), F32), SDS((1, D), F32)], compiler_params=_params(1), name=name)(y, tgt)


def post_bwd(dy, o, g, scale, name):
    T, D = o.shape
    tm = 512

    def body(dy_ref, o_ref, g_ref, do_ref, dg_ref):
        @pl.when(pl.program_id(0) == 0)
        def _():
            dg_ref[...] = jnp.zeros_like(dg_ref)

        ov = o_ref[...]
        r = scale * dy_ref[...]
        rstd = lax.rsqrt(jnp.mean(ov * ov, axis=-1, keepdims=True) + NORM_EPS)
        oh = ov * rstd
        dg_ref[...] += jnp.sum(r * oh, axis=0, keepdims=True)
        rg = r * g_ref[...]
        do_ref[...] = (rstd * (rg - oh * jnp.mean(rg * oh, axis=-1, keepdims=True))).astype(BF16)

    return pl.pallas_call(
        body, grid=(T // tm,),
        in_specs=[pl.BlockSpec((tm, D), lambda i: (i, 0)), pl.BlockSpec((tm, D), lambda i: (i, 0)),
                  pl.BlockSpec((1, D), lambda i: (0, 0))],
        out_specs=[pl.BlockSpec((tm, D), lambda i: (i, 0)), pl.BlockSpec((1, D), lambda i: (0, 0))],
        out_shape=[SDS((T, D), BF16), SDS((1, D), F32)], compiler_params=_params(1), name=name)(dy, o, g)


def mm_nt_b(gr, w, name):
    T, N = gr.shape
    nb, K, _ = w.shape
    tm = 512

    def body(g_ref, w_ref, o_ref):
        gv = g_ref[...]
        for b in range(nb):
            o_ref[b] = _dot_nt(gv, w_ref[b])

    return pl.pallas_call(
        body, grid=(T // tm,),
        in_specs=[pl.BlockSpec((tm, N), lambda i: (i, 0)), _resident((nb, K, N))],
        out_specs=pl.BlockSpec((nb, tm, K), lambda i: (0, i, 0)),
        out_shape=SDS((nb, T, K), F32), compiler_params=_params(1), name=name)(gr, w)


def ffn_dact(do, w_out, z, name):
    T, D = do.shape
    nb, F, _ = w_out.shape
    tm = 512

    rc = tm // ROW_SPLIT

    def body(g_ref, w_ref, z_ref, dz_ref):
        das = [_dot_nt(g_ref[pl.ds(c * rc, rc), :], w_ref[...]) for c in range(ROW_SPLIT)]
        for c, da in enumerate(das):
            rows = pl.ds(c * rc, rc)
            zg = z_ref[0, rows, :].astype(F32)
            zu = z_ref[1, rows, :].astype(F32)
            sg = _sigmoid(zg)
            dz_ref[0, rows, :] = (da * zu * (sg * (1.0 + zg * (1.0 - sg)))).astype(BF16)
            dz_ref[1, rows, :] = (da * zg * sg).astype(BF16)

    return pl.pallas_call(
        body, grid=(nb, T // tm),
        in_specs=[pl.BlockSpec((tm, D), lambda b, i: (i, 0)), pl.BlockSpec((None, F, D), lambda b, i: (b, 0, 0)),
                  pl.BlockSpec((2, None, tm, F), lambda b, i: (0, b, i, 0))],
        out_specs=pl.BlockSpec((2, None, tm, F), lambda b, i: (0, b, i, 0)),
        out_shape=SDS((2, nb, T, F), BF16), compiler_params=_params(2), name=name)(do, w_out, z)


def mm_tn(a, g, a_batched, g_batched, nb, name, comm=None):
    T = a.shape[-2]
    K, N = a.shape[-1], g.shape[-1]
    tk = 2048
    nk = T // tk

    def body(a_ref, g_ref, o_ref, acc):
        k = pl.program_id(1)

        @pl.when(k == 0)
        def _():
            acc[...] = jnp.zeros_like(acc)

        acc[...] += _dot_tn(a_ref[...].astype(BF16), g_ref[...].astype(BF16))

        @pl.when(k == nk - 1)
        def _():
            o_ref[...] = acc[...].astype(BF16)

    a_spec = (pl.BlockSpec((None, tk, K), lambda b, k: (b, k, 0)) if a_batched
              else pl.BlockSpec((tk, K), lambda b, k: (k, 0)))
    g_spec = (pl.BlockSpec((None, tk, N), lambda b, k: (b, k, 0)) if g_batched
              else pl.BlockSpec((tk, N), lambda b, k: (k, 0)))
    (out,), slots = _call(
        body, grid=(nb, nk), in_specs=[a_spec, g_spec],
        out_specs=[pl.BlockSpec((None, K, N), lambda b, k: (b, 0, 0))],
        out_shape=[SDS((nb, K, N), BF16)], scratch_shapes=[pltpu.VMEM((K, N), F32)],
        args=(a, g), name=name, comm=comm)
    return out, slots


def dh_pre_bwd(dz, w, x, g, dyres, name, comm=None):
    nb, T, F = dz.shape
    D = w.shape[1]
    tm = 512

    rc = tm // ROW_SPLIT

    def body(dz_ref, w_ref, x_ref, g_ref, dy_ref, dx_ref, dg_ref):
        @pl.when(pl.program_id(0) == 0)
        def _():
            dg_ref[...] = jnp.zeros_like(dg_ref)

        accs = []
        for c in range(ROW_SPLIT):
            rows = pl.ds(c * rc, rc)
            dh = _dot_nt(dz_ref[0, rows, :].astype(BF16), w_ref[0])
            for b in range(1, nb):
                dh += _dot_nt(dz_ref[b, rows, :].astype(BF16), w_ref[b])
            accs.append(dh)
        for c, dh in enumerate(accs):
            rows = pl.ds(c * rc, rc)
            xv = x_ref[rows, :]
            rstd = lax.rsqrt(jnp.mean(xv * xv, axis=-1, keepdims=True) + NORM_EPS)
            xh = xv * rstd
            dg_ref[...] += jnp.sum(dh * xh, axis=0, keepdims=True)
            dhg = dh * g_ref[...]
            dx_ref[rows, :] = dy_ref[rows, :] + rstd * (dhg - xh * jnp.mean(dhg * xh, axis=-1, keepdims=True))

    return _call(
        body, grid=(T // tm,),
        in_specs=[pl.BlockSpec((nb, tm, F), lambda i: (0, i, 0)), _resident((nb, D, F)),
                  pl.BlockSpec((tm, D), lambda i: (i, 0)), pl.BlockSpec((1, D), lambda i: (0, 0)),
                  pl.BlockSpec((tm, D), lambda i: (i, 0))],
        out_specs=[pl.BlockSpec((tm, D), lambda i: (i, 0)), pl.BlockSpec((1, D), lambda i: (0, 0))],
        out_shape=[SDS((T, D), F32), SDS((1, D), F32)],
        args=(dz, w, x, g, dyres), name=name, comm=comm)


ATTN_GROUP = {1: 4, 4: 1, 16: 1}
ATTN_UNROLL = 4


def _attn_masks():
    qi = lax.broadcasted_iota(jnp.int32, (QBLK, QBLK), 0)
    kj = lax.broadcasted_iota(jnp.int32, (QBLK, QBLK), 1)
    cur_ok = kj <= qi
    prev_ok = kj >= qi
    dcur = (qi - kj).astype(F32)
    return cur_ok, prev_ok, dcur, dcur + float(QBLK)


def _head_slopes(p, d):
    out = []
    for hq in range(2):
        v = [float(d) * 2.0 ** -(2 * q + hq + 1) for q in range(4)]
        out.append(jnp.where(p == 0, v[0], jnp.where(p == 1, v[1], jnp.where(p == 2, v[2], v[3]))))
    return out


def _rows(start, d):
    return pl.ds(start, QBLK, stride=d) if d > 1 else pl.ds(start, QBLK)


def _pair_spec(rows, part, blk):
    return pl.BlockSpec((None, rows, PAIR_W), lambda p, n: (2 * part + p // 2, blk(n), p % 2))


def _for_query_blocks(d, groups, several):
    blocks = [(g, r) for g in range(groups) for r in range(d)]
    for s in range(0, len(blocks), ATTN_UNROLL):
        several(blocks[s:s + ATTN_UNROLL])


def attn_fwd(proj, d, name, comm=None):
    T = proj.shape[1]
    sb, groups = QBLK * d, ATTN_GROUP[d]
    rb = sb * groups
    nblk = T // rb

    def body(q_ref, kc_ref, kp_ref, vc_ref, vp_ref, o_ref, l_ref):
        p, n = pl.program_id(0), pl.program_id(1)
        cur_ok, prev_ok, dcur, dprev = _attn_masks()
        first_ok = jnp.logical_and(prev_ok, n > 0)
        lane_head = lax.broadcasted_iota(jnp.int32, (QBLK, PAIR_W), 1) // HEAD_DIM
        slopes = _head_slopes(p, d)

        def several(blocks):
            work = []
            for g, r in blocks:
                rows = _rows(g * sb + r, d)
                q = q_ref[rows, :]
                kc, vc = kc_ref[rows, :].astype(BF16), vc_ref[rows, :].astype(BF16)
                if g == 0:
                    prow, pok = _rows(r, d), first_ok
                    kp, vp = kp_ref[prow, :].astype(BF16), vp_ref[prow, :].astype(BF16)
                else:
                    prow, pok = _rows((g - 1) * sb + r, d), prev_ok
                    kp, vp = kc_ref[prow, :].astype(BF16), vc_ref[prow, :].astype(BF16)
                for hq in range(2):
                    qm = jnp.where(lane_head == hq, q, 0.0).astype(BF16)
                    work.append([rows, hq, pok, vc, vp, _dot_nt(qm, kc), _dot_nt(qm, kp)])
            for w in work:
                _, hq, pok, _, _, sc, sp = w
                sc = jnp.where(cur_ok, sc * 0.125 - slopes[hq] * dcur, NEG)
                sp = jnp.where(pok, sp * 0.125 - slopes[hq] * dprev, NEG)
                m = jnp.maximum(jnp.max(sc, axis=1, keepdims=True), jnp.max(sp, axis=1, keepdims=True))
                pc = jnp.exp(sc - m)
                pp = jnp.exp(sp - m)
                den = jnp.sum(pc, axis=1, keepdims=True) + jnp.sum(pp, axis=1, keepdims=True)
                w[5:] = [pc.astype(BF16), pp.astype(BF16), 1.0 / den, m + jnp.log(den)]
            for i in range(0, len(work), 2):
                o_acc = jnp.zeros((QBLK, PAIR_W), F32)
                l_acc = jnp.zeros((QBLK, PAIR_W), F32)
                for rows, hq, _, vc, vp, pc, pp, inv, lse in work[i:i + 2]:
                    hm = lane_head == hq
                    o_acc = jnp.where(hm, (_dot(pc, vc) + _dot(pp, vp)) * inv, o_acc)
                    l_acc = jnp.where(hm, lse, l_acc)
                o_ref[rows, :] = o_acc
                l_ref[rows, :] = l_acc

        _for_query_blocks(d, groups, several)

    cur = lambda part: _pair_spec(rb, part, lambda n: n)
    prv = lambda part: _pair_spec(sb, part, lambda n: jnp.maximum(n * groups - 1, 0))
    return _call(
        body, grid=(4, nblk), in_specs=[cur(0), cur(1), prv(1), cur(2), prv(2)], out_specs=[cur(0), cur(0)],
        out_shape=[SDS((2, T, 2 * PAIR_W), F32), SDS((2, T, 2 * PAIR_W), F32)],
        args=(proj, proj, proj, proj, proj), name=name, comm=comm)


def attn_merge(os_, ls_, name):
    _, T, HW = os_[0].shape
    tm = 512

    def body(o1, o2, o3, l1, l2, l3, o_ref, l_ref):
        a, b, c = l1[...], l2[...], l3[...]
        m = jnp.maximum(jnp.maximum(a, b), c)
        ea, eb, ec = jnp.exp(a - m), jnp.exp(b - m), jnp.exp(c - m)
        s = ea + eb + ec
        o_ref[...] = (ea * o1[...] + eb * o2[...] + ec * o3[...]) * (1.0 / s)
        l_ref[...] = m + jnp.log(s)

    spec = pl.BlockSpec((None, tm, HW), lambda h, i: (h, i, 0))
    return pl.pallas_call(
        body, grid=(2, T // tm), in_specs=[spec] * 6, out_specs=[spec, spec],
        out_shape=[SDS((2, T, HW), F32), SDS((2, T, HW), F32)],
        compiler_params=_params(2), name=name)(*os_, *ls_)


def attn_bwd(proj, dcat, o, lse, acc, d, name):
    T = proj.shape[1]
    sb, groups = QBLK * d, ATTN_GROUP[d]
    rb = sb * groups
    nblk = T // rb
    has_acc = acc is not None

    def body(*refs):
        (qc_ref, qn_ref, kc_ref, kp_ref, vc_ref, vp_ref, dc_ref, dn_ref, oc_ref, on_ref, lc_ref, ln_ref) = refs[:12]
        acc_ref = refs[12] if has_acc else None
        out_ref = refs[-1]
        p, n = pl.program_id(0), pl.program_id(1)
        cur_ok, prev_ok, dcur, dprev = _attn_masks()
        first_ok = jnp.logical_and(prev_ok, n > 0)
        last_ok = jnp.logical_and(prev_ok, n < nblk - 1)
        lane_head = lax.broadcasted_iota(jnp.int32, (QBLK, PAIR_W), 1) // HEAD_DIM
        slopes = _head_slopes(p, d)

        def one(g, r):
            rows = _rows(g * sb + r, d)
            q_c, do_c, o_c, l_c = qc_ref[rows, :], dc_ref[rows, :], oc_ref[rows, :], lc_ref[rows, :]
            k_c, v_c = kc_ref[rows, :].astype(BF16), vc_ref[rows, :].astype(BF16)
            if g == 0:
                prow, pok_c = _rows(r, d), first_ok
                k_p, v_p = kp_ref[prow, :].astype(BF16), vp_ref[prow, :].astype(BF16)
            else:
                prow, pok_c = _rows((g - 1) * sb + r, d), prev_ok
                k_p, v_p = kc_ref[prow, :].astype(BF16), vc_ref[prow, :].astype(BF16)
            if g == groups - 1:
                nrow, pok_n = _rows(r, d), last_ok
                q_n, do_n, o_n, l_n = qn_ref[nrow, :], dn_ref[nrow, :], on_ref[nrow, :], ln_ref[nrow, :]
            else:
                nrow, pok_n = _rows((g + 1) * sb + r, d), prev_ok
                q_n, do_n, o_n, l_n = qc_ref[nrow, :], dc_ref[nrow, :], oc_ref[nrow, :], lc_ref[nrow, :]
            heads = []
            for hq in range(2):
                hm = lane_head == hq
                qm_c = jnp.where(hm, q_c, 0.0).astype(BF16)
                qm_n = jnp.where(hm, q_n, 0.0).astype(BF16)
                dom_c = jnp.where(hm, do_c, 0.0)
                dom_n = jnp.where(hm, do_n, 0.0)
                dd_c = jnp.sum(dom_c * o_c, axis=1, keepdims=True)
                dd_n = jnp.sum(dom_n * o_n, axis=1, keepdims=True)
                ls_c = jnp.max(jnp.where(hm, l_c, NEG), axis=1, keepdims=True)
                ls_n = jnp.max(jnp.where(hm, l_n, NEG), axis=1, keepdims=True)
                dob_c, dob_n = dom_c.astype(BF16), dom_n.astype(BF16)
                mm = [(_dot_nt(qm_c, k_c), _dot_nt(dob_c, v_c)), (_dot_nt(qm_c, k_p), _dot_nt(dob_c, v_p)),
                      (_dot_nt(qm_n, k_c), _dot_nt(dob_n, v_c))]
                heads.append(dict(hq=hq, qm_c=qm_c, qm_n=qm_n, dob_c=dob_c, dob_n=dob_n, mm=mm,
                                  dd=(dd_c, dd_c, dd_n), ls=(ls_c, ls_c, ls_n)))
            return dict(rows=rows, k_c=k_c, k_p=k_p, heads=heads, oks=(cur_ok, pok_c, pok_n))

        def several(blocks):
            work = [one(g, r) for g, r in blocks]
            for w in work:
                for h in w["heads"]:
                    slope, dist = slopes[h["hq"]], (dcur, dprev, dprev)
                    h["pr"], h["ds"] = [], []
                    for j in range(3):
                        s = jnp.where(w["oks"][j], h["mm"][j][0] * 0.125 - slope * dist[j], NEG)
                        pr = jnp.exp(s - h["ls"][j])
                        h["pr"].append(pr.astype(BF16))
                        h["ds"].append((pr * (h["mm"][j][1] - h["dd"][j])).astype(BF16))
            for w in work:
                dq = jnp.zeros((QBLK, PAIR_W), F32)
                dk = jnp.zeros((QBLK, PAIR_W), F32)
                dv = jnp.zeros((QBLK, PAIR_W), F32)
                for h in w["heads"]:
                    ds, pr = h["ds"], h["pr"]
                    dq_h = _dot(ds[0], w["k_c"]) + _dot(ds[1], w["k_p"])
                    dk += (_dot_tn(ds[0], h["qm_c"]) + _dot_tn(ds[2], h["qm_n"])) * 0.125
                    dv += _dot_tn(pr[0], h["dob_c"]) + _dot_tn(pr[2], h["dob_n"])
                    dq = jnp.where(lane_head == h["hq"], dq_h * 0.125, dq)
                for part, val in enumerate((dq, dk, dv)):
                    if has_acc:
                        val = val + acc_ref.at[part][w["rows"], :]
                    out_ref.at[part][w["rows"], :] = val

        _for_query_blocks(d, groups, several)

    cur = lambda part: _pair_spec(rb, part, lambda n: n)
    prv = lambda part: _pair_spec(sb, part, lambda n: jnp.maximum(n * groups - 1, 0))
    nxt = lambda part: _pair_spec(sb, part, lambda n: jnp.minimum((n + 1) * groups, T // sb - 1))
    full = pl.BlockSpec((3, None, rb, PAIR_W), lambda p, n: (0, p // 2, n, p % 2))
    in_specs = [cur(0), nxt(0), cur(1), prv(1), cur(2), prv(2), cur(0), nxt(0), cur(0), nxt(0), cur(0), nxt(0)]
    args = [proj, proj, proj, proj, proj, proj, dcat, dcat, o, o, lse, lse]
    if has_acc:
        in_specs.append(full)
        args.append(acc)
    return pl.pallas_call(
        body, grid=(4, nblk), in_specs=in_specs, out_specs=full,
        out_shape=SDS((3, 2, T, 2 * PAIR_W), F32), compiler_params=_params(2), name=name)(*args)


def _scan_rows(buf, tab_ref, reverse):
    n_tiles = (buf.shape[0] - 8) // 8
    per_half = HALF_STATES // SCAN_CW
    row = lax.broadcasted_iota(jnp.int32, (8, SCAN_CW), 0)
    sgn = -1.0 if reverse else 1.0

    def chunk(j, _):
        c0 = pl.multiple_of((j // per_half) * 2 * HALF_STATES + (j % per_half) * SCAN_CW, 128)
        cre = pl.ds(c0, SCAN_CW)
        cim = pl.ds(pl.multiple_of(c0 + HALF_STATES, 128), SCAN_CW)
        steps = []
        for s, k in enumerate((1, 2, 4)):
            ok, shift = (row < 8 - k, 8 - k) if reverse else (row >= k, k)
            steps.append((shift, jnp.where(ok, tab_ref[pl.ds(s, 1), cre], 0.0),
                          jnp.where(ok, sgn * tab_ref[pl.ds(s, 1), cim], 0.0)))
        trow = 16 if reverse else 8
        pr, pi = tab_ref[pl.ds(trow, 8), cre], tab_ref[pl.ds(trow, 8), cim]
        for t in range(n_tiles):
            base = 8 * (n_tiles - 1 - t) if reverse else 8 + 8 * t
            rows = pl.ds(base, 8)
            re, im = buf[rows, cre], buf[rows, cim]
            for shift, ar, ai in steps:
                sre, sim = pltpu.roll(re, shift, 0), pltpu.roll(im, shift, 0)
                re, im = re + ar * sre - ai * sim, im + ar * sim + ai * sre
            crow = pl.ds(base + 8 if reverse else base - 1, 1)
            cr, ci = buf[crow, cre], buf[crow, cim]
            buf[rows, cre] = re + pr * cr - pi * ci
            buf[rows, cim] = im + pr * ci + pi * cr
        return 0

    lax.fori_loop(0, 2 * per_half, chunk, 0)


def ssm_fwd(proj, bh, ch, apow, dskip, name, comm=None):
    _, T, C = proj.shape
    tm = SCAN_TM
    SW = 4 * HALF_STATES

    def body(u_ref, bh_ref, ch_ref, tab_ref, dsk_ref, y_ref, s_ref, buf):
        @pl.when(pl.program_id(0) == 0)
        def _():
            buf[pl.ds(0, 8), :] = jnp.zeros((8, SW), F32)

        for h in range(2):
            buf[pl.ds(8, tm), pl.ds(h * 2 * HALF_STATES, 2 * HALF_STATES)] = _dot(u_ref[h].astype(BF16), bh_ref[h])
        _scan_rows(buf, tab_ref, False)
        s_ref[...] = buf[pl.ds(8, tm), :]
        buf[pl.ds(0, 8), :] = buf[pl.ds(tm, 8), :]
        for h in range(2):
            sv = s_ref[:, pl.ds(h * 2 * HALF_STATES, 2 * HALF_STATES)].astype(BF16)
            y_ref[h] = _dot(sv, ch_ref[h]) + dsk_ref[h] * u_ref[h]

    return _call(
        body, grid=(T // tm,),
        in_specs=[pl.BlockSpec((2, tm, C), lambda i: (3, i, 0)),
                  pl.BlockSpec((2, C, 2 * HALF_STATES), lambda i: (0, 0, 0)),
                  pl.BlockSpec((2, 2 * HALF_STATES, C), lambda i: (0, 0, 0)),
                  pl.BlockSpec((24, SW), lambda i: (0, 0)),
                  pl.BlockSpec((2, 1, C), lambda i: (0, 0, 0))],
        out_specs=[pl.BlockSpec((2, tm, C), lambda i: (0, i, 0)), pl.BlockSpec((tm, SW), lambda i: (i, 0))],
        out_shape=[SDS((2, T, C), F32), SDS((T, SW), F32)],
        scratch_shapes=[pltpu.VMEM((tm + 8, SW), F32)],
        args=(proj, bh, ch, apow, dskip), name=name, comm=comm)


def ssm_bwd(dy, proj, st, bh, ch, apow, dskip, name, comm=None):
    _, T, C = proj.shape
    tm = SCAN_TM
    nt = T // tm
    SW = 4 * HALF_STATES
    HS2 = 2 * HALF_STATES

    def body(dy_ref, u_ref, s_ref, sp_ref, bh_ref, ch_ref, tab_ref, dsk_ref,
             du_ref, da_ref, dbh_ref, dch_ref, dd_ref, lam):
        i = pl.program_id(0)

        @pl.when(i == 0)
        def _():
            lam[pl.ds(tm, 8), :] = jnp.zeros((8, SW), F32)
            da_ref[...] = jnp.zeros_like(da_ref)
            dbh_ref[...] = jnp.zeros_like(dbh_ref)
            dch_ref[...] = jnp.zeros_like(dch_ref)
            dd_ref[...] = jnp.zeros_like(dd_ref)

        for h in range(2):
            lam[pl.ds(0, tm), pl.ds(h * HS2, HS2)] = _dot_nt(dy_ref[h].astype(BF16), ch_ref[h])
        _scan_rows(lam, tab_ref, True)

        first = i == nt - 1
        per_half = HALF_STATES // SCAN_CW

        def chunk(j, _):
            c0 = pl.multiple_of((j // per_half) * HS2 + (j % per_half) * SCAN_CW, 128)
            cre, cim = pl.ds(c0, SCAN_CW), pl.ds(pl.multiple_of(c0 + HALF_STATES, 128), SCAN_CW)
            row = lax.broadcasted_iota(jnp.int32, (tm, SCAN_CW), 0)
            pre = jnp.where(first, 0.0, sp_ref[pl.ds(7, 1), cre])
            pim = jnp.where(first, 0.0, sp_ref[pl.ds(7, 1), cim])
            spr = jnp.where(row == 0, pre, pltpu.roll(s_ref[:, cre], 1, 0))
            spi = jnp.where(row == 0, pim, pltpu.roll(s_ref[:, cim], 1, 0))
            lr, li = lam[pl.ds(0, tm), cre], lam[pl.ds(0, tm), cim]
            da_ref[:, cre] += jnp.sum(lr * spr + li * spi, axis=0, keepdims=True)
            da_ref[:, cim] += jnp.sum(li * spr - lr * spi, axis=0, keepdims=True)
            return 0

        lax.fori_loop(0, 2 * per_half, chunk, 0)
        lam[pl.ds(tm, 8), :] = lam[pl.ds(0, 8), :]

        for h in range(2):
            lb = lam[pl.ds(0, tm), pl.ds(h * HS2, HS2)].astype(BF16)
            dyv, uv = dy_ref[h], u_ref[h]
            du_ref[h] = _dot_nt(lb, bh_ref[h]) + dsk_ref[h] * dyv
            dbh_ref[h] += _dot_tn(uv.astype(BF16), lb)
            dch_ref[h] += _dot_tn(s_ref[:, pl.ds(h * HS2, HS2)].astype(BF16), dyv.astype(BF16))
            dd_ref[h] += jnp.sum(dyv * uv, axis=0, keepdims=True)

    rev = lambda i: nt - 1 - i
    return _call(
        body, grid=(nt,),
        in_specs=[pl.BlockSpec((2, tm, C), lambda i: (0, rev(i), 0)),
                  pl.BlockSpec((2, tm, C), lambda i: (3, rev(i), 0)),
                  pl.BlockSpec((tm, SW), lambda i: (rev(i), 0)),
                  pl.BlockSpec((8, SW), lambda i: (jnp.maximum(rev(i) * (tm // 8) - 1, 0), 0)),
                  pl.BlockSpec((2, C, HS2), lambda i: (0, 0, 0)),
                  pl.BlockSpec((2, HS2, C), lambda i: (0, 0, 0)),
                  pl.BlockSpec((24, SW), lambda i: (0, 0)),
                  pl.BlockSpec((2, 1, C), lambda i: (0, 0, 0))],
        out_specs=[pl.BlockSpec((2, tm, C), lambda i: (0, rev(i), 0)),
                   pl.BlockSpec((1, SW), lambda i: (0, 0)),
                   pl.BlockSpec((2, C, HS2), lambda i: (0, 0, 0)),
                   pl.BlockSpec((2, HS2, C), lambda i: (0, 0, 0)),
                   pl.BlockSpec((2, 1, C), lambda i: (0, 0, 0))],
        out_shape=[SDS((2, T, C), F32), SDS((1, SW), F32), SDS((2, C, HS2), F32), SDS((2, HS2, C), F32),
                   SDS((2, 1, C), F32)],
        scratch_shapes=[pltpu.VMEM((tm + 8, SW), F32)],
        args=(dy, proj, st, st, bh, ch, apow, dskip), name=name, comm=comm)


_GELU_C = math.sqrt(2.0 / math.pi)


def _gelu(x):
    t = jnp.tanh(_GELU_C * (x + 0.044715 * x * x * x))
    return 0.5 * x * (1.0 + t), t


def glu_fwd(y, w, b, name):
    _, T, C = y.shape
    tm = 512

    def body(y_ref, w_ref, b_ref, o_ref, lg_ref):
        y0, _ = _gelu(y_ref[0])
        y1, _ = _gelu(y_ref[1])
        lg = _dot(y0.astype(BF16), w_ref[0]) + _dot(y1.astype(BF16), w_ref[1]) + b_ref[...]
        sg = _sigmoid(lg)
        o_ref[0] = y0 * sg[:, :C]
        o_ref[1] = y1 * sg[:, C:]
        lg_ref[0] = lg[:, :C]
        lg_ref[1] = lg[:, C:]

    return pl.pallas_call(
        body, grid=(T // tm,),
        in_specs=[pl.BlockSpec((2, tm, C), lambda i: (0, i, 0)), pl.BlockSpec((2, C, 2 * C), lambda i: (0, 0, 0)),
                  pl.BlockSpec((1, 2 * C), lambda i: (0, 0))],
        out_specs=[pl.BlockSpec((2, tm, C), lambda i: (0, i, 0)), pl.BlockSpec((2, tm, C), lambda i: (0, i, 0))],
        out_shape=[SDS((2, T, C), F32), SDS((2, T, C), F32)], compiler_params=_params(1), name=name)(y, w, b)


def glu_bwd(dcat, y, lg, w, name):
    _, T, C = y.shape
    tm = 512

    def body(d_ref, y_ref, lg_ref, w_ref, dy_ref, dw_ref, db_ref):
        @pl.when(pl.program_id(0) == 0)
        def _():
            dw_ref[...] = jnp.zeros_like(dw_ref)
            db_ref[...] = jnp.zeros_like(db_ref)

        y2, th, sg, dlg = [], [], [], []
        for h in range(2):
            yy, tt = _gelu(y_ref[h])
            ss = _sigmoid(lg_ref[h])
            y2.append(yy)
            th.append(tt)
            sg.append(ss)
            dlg.append(d_ref[h] * yy * ss * (1.0 - ss))
        dl = jnp.concatenate(dlg, axis=1)
        dlb = dl.astype(BF16)
        db_ref[...] += jnp.sum(dl, axis=0, keepdims=True)
        for h in range(2):
            dy2 = d_ref[h] * sg[h] + _dot_nt(dlb, w_ref[h])
            yv = y_ref[h]
            dgelu = 0.5 * (1.0 + th[h]) + 0.5 * yv * (1.0 - th[h] * th[h]) * _GELU_C * (1.0 + 3 * 0.044715 * yv * yv)
            dy_ref[h] = dy2 * dgelu
            dw_ref[h] += _dot_tn(y2[h].astype(BF16), dlb)

    return pl.pallas_call(
        body, grid=(T // tm,),
        in_specs=[pl.BlockSpec((2, tm, C), lambda i: (1, i, 0)), pl.BlockSpec((2, tm, C), lambda i: (0, i, 0)),
                  pl.BlockSpec((2, tm, C), lambda i: (0, i, 0)), pl.BlockSpec((2, C, 2 * C), lambda i: (0, 0, 0))],
        out_specs=[pl.BlockSpec((2, tm, C), lambda i: (0, i, 0)), pl.BlockSpec((2, C, 2 * C), lambda i: (0, 0, 0)),
                   pl.BlockSpec((1, 2 * C), lambda i: (0, 0))],
        out_shape=[SDS((2, T, C), F32), SDS((2, C, 2 * C), F32), SDS((1, 2 * C), F32)],
        compiler_params=_params(1), name=name)(dcat, y, lg, w)


def adamw(w, m, v, slots, name):
    R, C = w.shape
    tr = R
    for cand in (512, 256, 128, 64, 32, 16, 8):
        if R % cand == 0 and cand * C * 4 <= 2 * 1024 * 1024:
            tr = cand
            break
    c1 = 1.0 / (1.0 - ADAM_B1 ** ADAM_STEP)
    c2 = 1.0 / (1.0 - ADAM_B2 ** ADAM_STEP)

    def body(w_ref, m_ref, v_ref, s_ref, g_ref, d_ref, nm_ref, nv_ref):
        g = s_ref[0].astype(F32)
        for j in range(1, N_DEV):
            g = g + s_ref[j].astype(F32)
        nm = ADAM_B1 * m_ref[...] + (1.0 - ADAM_B1) * g
        nv = ADAM_B2 * v_ref[...] + (1.0 - ADAM_B2) * (g * g)
        g_ref[...] = g
        nm_ref[...] = nm
        nv_ref[...] = nv
        d_ref[...] = -ADAM_LR * ((nm * c1) / (jnp.sqrt(nv * c2) + ADAM_EPS) + ADAM_WD * w_ref[...])

    spec = pl.BlockSpec((tr, C), lambda i: (i, 0))
    return pl.pallas_call(
        body, grid=(R // tr,),
        in_specs=[spec, spec, spec, pl.BlockSpec((N_DEV, tr, C), lambda i: (0, i, 0))],
        out_specs=[spec] * 4, out_shape=[SDS((R, C), F32)] * 4, compiler_params=_params(1), name=name)(w, m, v, slots)


def _discretise(a_re, a_im, log_dt, b_re, b_im):
    dt = jnp.exp(log_dt)[:, None]
    e = jnp.exp(dt * a_re)
    ar, ai = e * jnp.cos(dt * a_im), e * jnp.sin(dt * a_im)
    den = a_re * a_re + a_im * a_im
    nr, ni = ar - 1.0, ai
    wr = (nr * a_re + ni * a_im) / den
    wi = (ni * a_re - nr * a_im) / den
    bbr = wr[..., None] * b_re - wi[..., None] * b_im
    bbi = wr[..., None] * b_im + wi[..., None] * b_re
    return ar, ai, bbr, bbi


def _block_diag(t):
    eye = jnp.eye(16, dtype=t.dtype).reshape(1, 16, 1, 16, 1)
    r, c = t.shape[1], t.shape[2]
    return (t.reshape(2, 16, r, 1, c) * eye).reshape(2, 16 * r, 16 * c)


def _diag_blocks(m, r, c):
    eye = jnp.eye(16, dtype=m.dtype).reshape(1, 16, 1, 16, 1)
    return jnp.sum(m.reshape(2, 16, r, 16, c) * eye, axis=3).reshape(32, r, c)


def _state_vec(re, im):
    return jnp.stack([re.reshape(2, HALF_STATES), im.reshape(2, HALF_STATES)], axis=1).reshape(-1)


BIG = ("ffn1_w_in", "ffn1_w_out", "w_mix_in", "w_glu", "w_mix_out", "ffn2_w_in", "ffn2_w_out")
WEIGHTS = ("ffn1_pre_g", "ffn1_w_in", "ffn1_w_out", "ffn1_post_g", "mix_pre_g", "w_mix_in", "a_re", "a_im", "log_dt",
           "b_re", "b_im", "c_re", "c_im", "d_skip", "w_glu", "b_glu", "w_mix_out", "mix_post_g", "ffn2_pre_g",
           "ffn2_w_in", "ffn2_w_out", "ffn2_post_g")
SMALL = tuple(n for n in WEIGHTS if n not in BIG)
PACK_COLS = 1024


def _pack(parts):
    flat = jnp.concatenate([p.reshape(-1) for p in parts])
    rows = -(-flat.shape[0] // (8 * PACK_COLS)) * 8
    return jnp.pad(flat, (0, rows * PACK_COLS - flat.shape[0])).reshape(rows, PACK_COLS)


def _unpack(packed, shapes):
    flat, out, off = packed.reshape(-1), [], 0
    for s in shapes:
        n = math.prod(s)
        out.append(flat[off:off + n].reshape(s))
        off += n
    return out


def _gather(names, wb):
    return [wb[n] for n in names], [False] * len(names)


def _ffn_bwd(dy, saved, x, pre_g, w_in, w_out4, post_g, tag):
    h, z, a, o = saved
    T = x.shape[0]
    do, dg_post = post_bwd(dy, o, post_g, 0.5, f"{tag}_post_bwd")
    dz = ffn_dact(do, w_out4, z, f"{tag}_dact")
    dz8 = dz.reshape(8, T, dz.shape[-1])
    dw_out, _ = mm_tn(a, do, True, False, 4, f"{tag}_dwout")
    dw_in, (s_out,) = mm_tn(h, dz8, False, True, 8, f"{tag}_dwin", comm=([dw_out.reshape(8, -1, D_MODEL)], [True]))
    (dx, dg_pre), (s_in,) = dh_pre_bwd(dz8, w_in, x, pre_g, dy, f"{tag}_dh", comm=([dw_in], [True]))
    return dx, dg_pre, dg_post, (s_in, s_out)


def local_step(x, tgt, sp, wb):
    T = x.shape[0]
    ar, ai, bbr, bbi = _discretise(sp["a_re"], sp["a_im"], sp["log_dt"], sp["b_re"], sp["b_im"])
    powers = [(ar, ai)]
    for _ in range(7):
        pr, pi = powers[-1]
        powers.append((pr * ar - pi * ai, pr * ai + pi * ar))
    zero = jnp.zeros_like(ar)
    rows = [_state_vec(*powers[k - 1]) for k in (1, 2, 4)] + [_state_vec(zero, zero)] * 5
    rows += [_state_vec(pr, pi) for pr, pi in powers]
    rows += [_state_vec(pr, -pi) for pr, pi in reversed(powers)]
    apow = jnp.stack(rows)
    bh = jnp.concatenate([_block_diag(bbr.transpose(0, 2, 1)), _block_diag(bbi.transpose(0, 2, 1))], axis=2)
    ch = jnp.concatenate([_block_diag(sp["c_re"].transpose(0, 2, 1)), _block_diag(-sp["c_im"].transpose(0, 2, 1))], axis=1)
    bh, ch = bh.astype(BF16), ch.astype(BF16)
    dskip = sp["d_skip"].reshape(2, 1, 256)

    (w1_in,) = exchange(*_gather(["ffn1_w_in"], wb), "gather_w1in")
    h1 = rms_fwd(x, sp["ffn1_pre_g"], "ffn1_rms")
    (z1, a1), (w1_out, w_mi) = ffn_in(h1, w1_in, "ffn1_in", comm=_gather(["ffn1_w_out", "w_mix_in"], wb))
    w1_out4 = w1_out.reshape(4, -1, D_MODEL)
    (o1, x1), (w_glu, w_mo) = mm_acc_norm(
        a1, w1_out4, x, sp["ffn1_post_g"], 0.5, "ffn1_out", comm=_gather(["w_glu", "w_mix_out"], wb))
    w_glu2, w_mo4 = w_glu.reshape(2, 256, 512), w_mo.reshape(4, 256, D_MODEL)
    h2 = rms_fwd(x1, sp["mix_pre_g"], "mix_rms")
    proj = mm_nn_b(h2, w_mi, 8, F32, "mix_proj")
    (y_ssm, states), (w2_in,) = ssm_fwd(proj, bh, ch, apow, dskip, "ssm_fwd", comm=_gather(["ffn2_w_in"], wb))
    os_, ls_ = [], []
    for d in DILATIONS:
        (o_d, l_d), got = attn_fwd(proj, d, f"attn_fwd_d{d}",
                                   comm=_gather(["ffn2_w_out"], wb) if d == DILATIONS[-1] else None)
        os_.append(o_d)
        ls_.append(l_d)
    w2_out4 = got[0].reshape(4, -1, D_MODEL)
    o_att, lse = attn_merge(os_, ls_, "attn_merge")
    o_ssm, lg = glu_fwd(y_ssm, w_glu2, sp["b_glu"], "glu_fwd")
    cat = jnp.concatenate([o_att, o_ssm], axis=0)
    (mixed, x2), _ = mm_acc_norm(cat, w_mo4, x1, sp["mix_post_g"], 1.0, "mix_out")
    h3 = rms_fwd(x2, sp["ffn2_pre_g"], "ffn2_rms")
    (z3, a3), _ = ffn_in(h3, w2_in, "ffn2_in")
    (o3, x3), _ = mm_acc_norm(a3, w2_out4, x2, sp["ffn2_post_g"], 0.5, "ffn2_out")
    dy3, sq = loss_head(x3, tgt, "loss_head")

    dx2, dg_f2pre, dg_f2post, (s_w2in, s_w2out) = _ffn_bwd(
        dy3, (h3, z3, a3, o3), x2, sp["ffn2_pre_g"], w2_in, w2_out4, sp["ffn2_post_g"], "ffn2")
    dmixed, dg_mpost = post_bwd(dx2, mixed, sp["mix_post_g"], 1.0, "mix_post_bwd")
    dcat = mm_nt_b(dmixed, w_mo4, "mix_dcat")
    dw_mo, _ = mm_tn(cat, dmixed, True, False, 4, "mix_dwout")
    dy_ssm, dw_glu, db_glu = glu_bwd(dcat, y_ssm, lg, w_glu2, "glu_bwd")
    (du, da, dbh, dch, dd), (s_wmo, s_wglu) = ssm_bwd(
        dy_ssm, proj, states, bh, ch, apow, dskip, "ssm_bwd",
        comm=([dw_mo.reshape(8, 128, D_MODEL), dw_glu.astype(BF16).reshape(8, 64, 512)], [True, True]))
    dqkv = None
    for d in DILATIONS:
        dqkv = attn_bwd(proj, dcat, o_att, lse, dqkv, d, f"attn_bwd_d{d}")
    dproj = jnp.concatenate([dqkv.reshape(6, T, 256), du], axis=0)
    dw_mi, _ = mm_tn(h2, dproj, False, True, 8, "mix_dwin")
    (dx1, dg_mpre), (s_wmi,) = dh_pre_bwd(
        dproj, w_mi, x1, sp["mix_pre_g"], dx2, "mix_dh", comm=([dw_mi], [True]))
    dx0, dg_f1pre, dg_f1post, (s_w1in, s_w1out) = _ffn_bwd(
        dx1, (h1, z1, a1, o1), x, sp["ffn1_pre_g"], w1_in, w1_out4, sp["ffn1_post_g"], "ffn1")

    da4 = da.reshape(2, 2, HALF_STATES)
    d_ar, d_ai = da4[:, 0].reshape(32, N_STATE), da4[:, 1].reshape(32, N_STATE)
    d_bbr = _diag_blocks(dbh[:, :, :HALF_STATES], 16, N_STATE).transpose(0, 2, 1)
    d_bbi = _diag_blocks(dbh[:, :, HALF_STATES:], 16, N_STATE).transpose(0, 2, 1)
    _, disc_vjp = jax.vjp(_discretise, sp["a_re"], sp["a_im"], sp["log_dt"], sp["b_re"], sp["b_im"])
    g_are, g_aim, g_ldt, g_bre, g_bim = disc_vjp((d_ar, d_ai, d_bbr, d_bbi))
    g_cre = _diag_blocks(dch[:, :HALF_STATES], N_STATE, 16).transpose(0, 2, 1)
    g_cim = -_diag_blocks(dch[:, HALF_STATES:], N_STATE, 16).transpose(0, 2, 1)
    small = {
        "ffn1_pre_g": dg_f1pre, "ffn1_post_g": dg_f1post, "mix_pre_g": dg_mpre, "a_re": g_are, "a_im": g_aim,
        "log_dt": g_ldt, "b_re": g_bre, "b_im": g_bim, "c_re": g_cre, "c_im": g_cim, "d_skip": dd.reshape(1, 512),
        "b_glu": db_glu, "mix_post_g": dg_mpost, "ffn2_pre_g": dg_f2pre, "ffn2_post_g": dg_f2post,
    }
    (small_slots,) = exchange([_pack([small[n] for n in SMALL])], [False], "exchange_small")
    big_slots = {"ffn1_w_in": s_w1in, "ffn1_w_out": s_w1out, "w_mix_in": s_wmi, "w_glu": s_wglu, "w_mix_out": s_wmo,
                 "ffn2_w_in": s_w2in, "ffn2_w_out": s_w2out}
    return sq, dx0, big_slots, small_slots


def kernel(x, ffn1_pre_g, ffn1_w_in, ffn1_w_out, ffn1_post_g, mix_pre_g, w_mix_in, a_re, a_im, log_dt, b_re, b_im, c_re, c_im, d_skip, w_glu, b_glu, w_mix_out, mix_post_g, ffn2_pre_g, ffn2_w_in, ffn2_w_out, ffn2_post_g, loss_target, m_ffn1_pre_g, m_ffn1_w_in, m_ffn1_w_out, m_ffn1_post_g, m_mix_pre_g, m_w_mix_in, m_a_re, m_a_im, m_log_dt, m_b_re, m_b_im, m_c_re, m_c_im, m_d_skip, m_w_glu, m_b_glu, m_w_mix_out, m_mix_post_g, m_ffn2_pre_g, m_ffn2_w_in, m_ffn2_w_out, m_ffn2_post_g, v_ffn1_pre_g, v_ffn1_w_in, v_ffn1_w_out, v_ffn1_post_g, v_mix_pre_g, v_w_mix_in, v_a_re, v_a_im, v_log_dt, v_b_re, v_b_im, v_c_re, v_c_im, v_d_skip, v_w_glu, v_b_glu, v_w_mix_out, v_mix_post_g, v_ffn2_pre_g, v_ffn2_w_in, v_ffn2_w_out, v_ffn2_post_g):
    args = dict(locals())
    w = {n: args[n][0] for n in WEIGHTS}
    m = {n: args["m_" + n][0] for n in WEIGHTS}
    v = {n: args["v_" + n][0] for n in WEIGHTS}

    wb = {n: w[n].astype(BF16) for n in BIG}
    sp = {n: w[n] for n in SMALL}
    for n in ("ffn1_pre_g", "ffn1_post_g", "mix_pre_g", "mix_post_g", "ffn2_pre_g", "ffn2_post_g", "b_glu", "d_skip"):
        sp[n] = w[n].reshape(1, -1)

    sq, grad_x, big_slots, small_slots = local_step(x[0], loss_target[0], sp, wb)
    loss = lax.psum(0.5 / D_MODEL * jnp.sum(sq), ("x", "y", "c"))

    outs = {}
    for n in BIG:
        shp = w[n].shape
        r2 = lambda t: t.reshape(-1, shp[-1])
        res = adamw(r2(w[n]), r2(m[n]), r2(v[n]), big_slots[n].reshape(N_DEV, -1, shp[-1]), f"adamw_{n}")
        outs[n] = [t.reshape((1,) + shp) for t in res]
    res = adamw(_pack([w[n] for n in SMALL]), _pack([m[n] for n in SMALL]), _pack([v[n] for n in SMALL]),
                small_slots, "adamw_small")
    shapes = [(1,) + w[n].shape for n in SMALL]
    unpacked = [_unpack(t, shapes) for t in res]
    for j, n in enumerate(SMALL):
        outs[n] = [unpacked[k][j] for k in range(4)]

    result = [loss, grad_x[None]]
    for k in range(4):
        result += [outs[n][k] for n in WEIGHTS]
    return tuple(result)
```

```python
import functools
import math

import jax
import jax.numpy as jnp
from jax import lax
from jax.experimental import pallas as pl
from jax.experimental.pallas import tpu as pltpu

F32, BF16 = jnp.float32, jnp.bfloat16
SDS = jax.ShapeDtypeStruct

D_MODEL = 1024
N_DEV = 8
HEAD_DIM = 64
PAIR_W = 128
QBLK = 128
DILATIONS = (1, 4, 16)
N_STATE = 64
HALF_STATES = 1024
NORM_EPS = 1e-6
NEG = -1e30
VMEM_LIMIT = 56 * 1024 * 1024
ADAM_LR, ADAM_B1, ADAM_B2, ADAM_EPS, ADAM_WD, ADAM_STEP = 1e-3, 0.9, 0.999, 1e-8, 0.01, 10
SCAN_TM = 256
SCAN_CW = 512


def _params(n_grid):
    return pltpu.CompilerParams(dimension_semantics=("arbitrary",) * n_grid, vmem_limit_bytes=VMEM_LIMIT)


def _dot(a, b):
    return jnp.dot(a, b, preferred_element_type=F32)


def _dot_nt(a, b):
    return lax.dot_general(a, b, (((1,), (1,)), ((), ())), preferred_element_type=F32)


def _dot_tn(a, b):
    return lax.dot_general(a, b, (((0,), (0,)), ((), ())), preferred_element_type=F32)


def _sigmoid(v):
    return 1.0 / (1.0 + jnp.exp(-v))


def _resident(shape):
    return pl.BlockSpec(shape, lambda i: (0,) * len(shape), pipeline_mode=pl.Buffered(1))


ROW_SPLIT = 2


def _exchange_phase(ins, outs, scatter, sems, start):
    send_sems, recv_sems, loc_sems = sems
    x, y, c = lax.axis_index("x"), lax.axis_index("y"), lax.axis_index("c")
    me = 4 * x + 2 * y + c
    own_copies, sends, arrivals = [], [], []
    for i in range(len(ins)):
        own = ins[i].at[me] if scatter[i] else ins[i]
        own_copies.append(pltpu.make_async_copy(own, outs[i].at[me], loc_sems.at[i]))
        for k in range(1, N_DEV):
            px = 1 - x if k & 4 else x
            py = 1 - y if k & 2 else y
            pc = 1 - c if k & 1 else c
            peer = 4 * px + 2 * py + pc
            src = ins[i].at[peer] if scatter[i] else ins[i]
            common = dict(src_ref=src, send_sem=send_sems.at[i, k - 1], recv_sem=recv_sems.at[i, k - 1],
                          device_id=(px, py, pc), device_id_type=pl.DeviceIdType.MESH)
            sends.append(pltpu.make_async_remote_copy(dst_ref=outs[i].at[me], **common))
            if not start:
                arrivals.append(pltpu.make_async_remote_copy(dst_ref=outs[i].at[peer], **common))
    if start:
        for cp in own_copies + sends:
            cp.start()
    else:
        for cp in arrivals:
            cp.wait_recv()
        for cp in sends:
            cp.wait_send()
        for cp in own_copies:
            cp.wait()


def _comm_shapes(arrs, scatter):
    n = len(arrs)
    out_shapes = [SDS(a.shape if scatter[i] else (N_DEV,) + a.shape, a.dtype) for i, a in enumerate(arrs)]
    sems = [pltpu.SemaphoreType.DMA((n, N_DEV - 1)), pltpu.SemaphoreType.DMA((n, N_DEV - 1)),
            pltpu.SemaphoreType.DMA((n,))]
    return out_shapes, sems


def gather_two_level(arr, name):
    def body(x_ref, out_ref, send_sems, recv_sems, local_sem):
        x, y, c = lax.axis_index("x"), lax.axis_index("y"), lax.axis_index("c")
        sibling = (x, y, 1 - c)
        chips = [(1 - x, y), (x, 1 - y), (1 - x, 1 - y)]

        def slot(px, py, pc):
            return out_ref.at[4 * px + 2 * py + pc]

        def copy(k, block, to, src=None):
            return pltpu.make_async_remote_copy(
                src_ref=slot(*block) if src is None else src, dst_ref=slot(*block),
                send_sem=send_sems.at[k], recv_sem=recv_sems.at[k], device_id=to, device_id_type=pl.DeviceIdType.MESH)

        mine = pltpu.make_async_copy(x_ref, slot(x, y, c), local_sem)
        mine.start()
        first = [copy(0, (x, y, c), sibling, src=x_ref)]
        first += [copy(1 + j, (x, y, c), (*chip, c), src=x_ref) for j, chip in enumerate(chips)]
        for cp in first:
            cp.start()
        passed = [copy(4 + j, (*chip, c), sibling) for j, chip in enumerate(chips)]
        for j, chip in enumerate(chips):
            copy(1 + j, (*chip, c), (x, y, c)).wait_recv()
            passed[j].start()
        copy(0, sibling, (x, y, c)).wait_recv()
        for j, chip in enumerate(chips):
            copy(4 + j, (*chip, 1 - c), (x, y, c)).wait_recv()
        for cp in first + passed:
            cp.wait_send()
        mine.wait()

    anyspec = pl.BlockSpec(memory_space=pl.ANY)
    return pl.pallas_call(
        body, in_specs=[anyspec], out_specs=anyspec, out_shape=SDS((N_DEV,) + arr.shape, arr.dtype),
        scratch_shapes=[pltpu.SemaphoreType.DMA((N_DEV - 1,)), pltpu.SemaphoreType.DMA((N_DEV - 1,)),
                        pltpu.SemaphoreType.DMA],
        compiler_params=pltpu.CompilerParams(has_side_effects=True), name=name)(arr)


def _call(body, *, grid, in_specs, out_specs, out_shape, args, name, scratch_shapes=(), comm=None):
    n_grid, scratch_shapes = len(grid), list(scratch_shapes)
    if comm is None:
        outs = pl.pallas_call(body, grid=grid, in_specs=in_specs, out_specs=out_specs, out_shape=out_shape,
                              scratch_shapes=scratch_shapes, compiler_params=_params(n_grid), name=name)(*args)
        return outs, []
    arrs, scatter = comm
    nc, n_in, n_out, n_sc = len(arrs), len(in_specs), len(out_specs), len(scratch_shapes)
    comm_shapes, sems = _comm_shapes(arrs, scatter)

    def wrapped(*refs):
        ins, cins = refs[:n_in], refs[n_in:n_in + nc]
        o0 = n_in + nc
        outs, couts = refs[o0:o0 + n_out], refs[o0 + n_out:o0 + n_out + nc]
        s0 = o0 + n_out + nc
        scratch, sem_refs = refs[s0:s0 + n_sc], refs[s0 + n_sc:]
        first = functools.reduce(jnp.logical_and, [pl.program_id(k) == 0 for k in range(n_grid)])
        last = functools.reduce(jnp.logical_and, [pl.program_id(k) == grid[k] - 1 for k in range(n_grid)])

        @pl.when(first)
        def _():
            _exchange_phase(cins, couts, scatter, sem_refs, True)

        body(*ins, *outs, *scratch)

        @pl.when(last)
        def _():
            _exchange_phase(cins, couts, scatter, sem_refs, False)

    anyspec = pl.BlockSpec(memory_space=pl.ANY)
    res = pl.pallas_call(
        wrapped, grid=grid, in_specs=list(in_specs) + [anyspec] * nc, out_specs=list(out_specs) + [anyspec] * nc,
        out_shape=list(out_shape) + comm_shapes, scratch_shapes=scratch_shapes + sems,
        compiler_params=pltpu.CompilerParams(dimension_semantics=("arbitrary",) * n_grid,
                                             vmem_limit_bytes=VMEM_LIMIT, has_side_effects=True),
        name=name)(*args, *arrs)
    return res[:n_out], res[n_out:]


def _rms(xv, g):
    r = lax.rsqrt(jnp.mean(xv * xv, axis=-1, keepdims=True) + NORM_EPS)
    return (xv * r * g).astype(BF16)


def ffn_in(x, g, w, name, comm=None):
    T, D = x.shape
    F = w.shape[2]
    tm = 512

    def body(x_ref, g_ref, w_ref, h_ref, z_ref, a_ref):
        hv = _rms(x_ref[...], g_ref[...])
        h_ref[...] = hv
        pending = None
        for j in range(5):
            if j < 4:
                zs = (_dot(hv, w_ref[j]), _dot(hv, w_ref[j + 4]))
            if pending is not None:
                zg, zu = pending
                z_ref[0, j - 1] = zg.astype(BF16)
                z_ref[1, j - 1] = zu.astype(BF16)
                a_ref[j - 1] = (zg * _sigmoid(zg) * zu).astype(BF16)
            pending = zs

    return _call(
        body, grid=(T // tm,),
        in_specs=[pl.BlockSpec((tm, D), lambda i: (i, 0)), pl.BlockSpec((1, D), lambda i: (0, 0)),
                  _resident((8, D, F))],
        out_specs=[pl.BlockSpec((tm, D), lambda i: (i, 0)), pl.BlockSpec((2, 4, tm, F), lambda i: (0, 0, i, 0)),
                   pl.BlockSpec((4, tm, F), lambda i: (0, i, 0))],
        out_shape=[SDS((T, D), BF16), SDS((2, 4, T, F), BF16), SDS((4, T, F), BF16)],
        args=(x, g, w), name=name, comm=comm)


def norm_proj(x, g, w, name):
    T, K = x.shape
    nb, _, N = w.shape
    tm = 512

    def body(x_ref, g_ref, w_ref, h_ref, o_ref):
        hv = _rms(x_ref[...], g_ref[...])
        h_ref[...] = hv
        for b in range(nb):
            o_ref[b] = _dot(hv, w_ref[b])

    return pl.pallas_call(
        body, grid=(T // tm,),
        in_specs=[pl.BlockSpec((tm, K), lambda i: (i, 0)), pl.BlockSpec((1, K), lambda i: (0, 0)),
                  _resident((nb, K, N))],
        out_specs=[pl.BlockSpec((tm, K), lambda i: (i, 0)), pl.BlockSpec((nb, tm, N), lambda i: (0, i, 0))],
        out_shape=[SDS((T, K), BF16), SDS((nb, T, N), F32)], compiler_params=_params(1), name=name)(x, g, w)


def mm_acc_norm(a, w, xres, g, scale, name, comm=None, tgt=None):
    nb, T, K = a.shape
    D = w.shape[2]
    tm = 512
    rc = tm // ROW_SPLIT
    with_loss = tgt is not None

    def body(a_ref, w_ref, x_ref, g_ref, *rest):
        if with_loss:
            t_ref, o_ref, dy_ref, sq_ref = rest

            @pl.when(pl.program_id(0) == 0)
            def _():
                sq_ref[...] = jnp.zeros_like(sq_ref)
        else:
            o_ref, y_ref = rest
        accs = []
        for c in range(ROW_SPLIT):
            rows = pl.ds(c * rc, rc)
            o = _dot(a_ref[0, rows, :].astype(BF16), w_ref[0])
            for b in range(1, nb):
                o += _dot(a_ref[b, rows, :].astype(BF16), w_ref[b])
            accs.append(o)
        for c, o in enumerate(accs):
            rows = pl.ds(c * rc, rc)
            r = lax.rsqrt(jnp.mean(o * o, axis=-1, keepdims=True) + NORM_EPS)
            o_ref[rows, :] = o
            y = x_ref[rows, :] + scale * (o * r * g_ref[...])
            if with_loss:
                e = y - t_ref[rows, :]
                dy_ref[rows, :] = e * (1.0 / D)
                sq_ref[...] += jnp.sum(e * e, axis=0, keepdims=True)
            else:
                y_ref[rows, :] = y

    tile = pl.BlockSpec((tm, D), lambda i: (i, 0))
    row = pl.BlockSpec((1, D), lambda i: (0, 0))
    in_specs = [pl.BlockSpec((nb, tm, K), lambda i: (0, i, 0)), _resident((nb, K, D)), tile, row]
    args = (a, w, xres, g)
    if with_loss:
        return _call(body, grid=(T // tm,), in_specs=in_specs + [tile], out_specs=[tile, tile, row],
                     out_shape=[SDS((T, D), F32), SDS((T, D), F32), SDS((1, D), F32)],
                     args=args + (tgt,), name=name, comm=comm)
    return _call(body, grid=(T // tm,), in_specs=in_specs, out_specs=[tile, tile],
                 out_shape=[SDS((T, D), F32), SDS((T, D), F32)], args=args, name=name, comm=comm)


def post_bwd(dy, o, g, scale, name):
    T, D = o.shape
    tm = 512

    def body(dy_ref, o_ref, g_ref, do_ref, dg_ref):
        @pl.when(pl.program_id(0) == 0)
        def _():
            dg_ref[...] = jnp.zeros_like(dg_ref)

        ov = o_ref[...]
        r = scale * dy_ref[...]
        rstd = lax.rsqrt(jnp.mean(ov * ov, axis=-1, keepdims=True) + NORM_EPS)
        oh = ov * rstd
        dg_ref[...] += jnp.sum(r * oh, axis=0, keepdims=True)
        rg = r * g_ref[...]
        do_ref[...] = (rstd * (rg - oh * jnp.mean(rg * oh, axis=-1, keepdims=True))).astype(BF16)

    return pl.pallas_call(
        body, grid=(T // tm,),
        in_specs=[pl.BlockSpec((tm, D), lambda i: (i, 0)), pl.BlockSpec((tm, D), lambda i: (i, 0)),
                  pl.BlockSpec((1, D), lambda i: (0, 0))],
        out_specs=[pl.BlockSpec((tm, D), lambda i: (i, 0)), pl.BlockSpec((1, D), lambda i: (0, 0))],
        out_shape=[SDS((T, D), BF16), SDS((1, D), F32)], compiler_params=_params(1), name=name)(dy, o, g)


def mm_nt_b(gr, w, name):
    T, N = gr.shape
    nb, K, _ = w.shape
    tm = 512

    def body(g_ref, w_ref, o_ref):
        gv = g_ref[...]
        for b in range(nb):
            o_ref[b] = _dot_nt(gv, w_ref[b])

    return pl.pallas_call(
        body, grid=(T // tm,),
        in_specs=[pl.BlockSpec((tm, N), lambda i: (i, 0)), _resident((nb, K, N))],
        out_specs=pl.BlockSpec((nb, tm, K), lambda i: (0, i, 0)),
        out_shape=SDS((nb, T, K), F32), compiler_params=_params(1), name=name)(gr, w)


def ffn_dact(do, w_out, z, name):
    T, D = do.shape
    nb, F, _ = w_out.shape
    tm = 512

    rc = tm // ROW_SPLIT

    def body(g_ref, w_ref, z_ref, dz_ref):
        das = [_dot_nt(g_ref[pl.ds(c * rc, rc), :], w_ref[...]) for c in range(ROW_SPLIT)]
        for c, da in enumerate(das):
            rows = pl.ds(c * rc, rc)
            zg = z_ref[0, rows, :].astype(F32)
            zu = z_ref[1, rows, :].astype(F32)
            sg = _sigmoid(zg)
            dz_ref[0, rows, :] = (da * zu * (sg * (1.0 + zg * (1.0 - sg)))).astype(BF16)
            dz_ref[1, rows, :] = (da * zg * sg).astype(BF16)

    return pl.pallas_call(
        body, grid=(nb, T // tm),
        in_specs=[pl.BlockSpec((tm, D), lambda b, i: (i, 0)), pl.BlockSpec((None, F, D), lambda b, i: (b, 0, 0)),
                  pl.BlockSpec((2, None, tm, F), lambda b, i: (0, b, i, 0))],
        out_specs=pl.BlockSpec((2, None, tm, F), lambda b, i: (0, b, i, 0)),
        out_shape=SDS((2, nb, T, F), BF16), compiler_params=_params(2), name=name)(do, w_out, z)


def mm_tn(a, g, a_batched, g_batched, nb, name, comm=None):
    T = a.shape[-2]
    K, N = a.shape[-1], g.shape[-1]
    tk = 2048
    nk = T // tk

    def body(a_ref, g_ref, o_ref, acc):
        k = pl.program_id(1)

        @pl.when(k == 0)
        def _():
            acc[...] = jnp.zeros_like(acc)

        acc[...] += _dot_tn(a_ref[...].astype(BF16), g_ref[...].astype(BF16))

        @pl.when(k == nk - 1)
        def _():
            o_ref[...] = acc[...].astype(BF16)

    a_spec = (pl.BlockSpec((None, tk, K), lambda b, k: (b, k, 0)) if a_batched
              else pl.BlockSpec((tk, K), lambda b, k: (k, 0)))
    g_spec = (pl.BlockSpec((None, tk, N), lambda b, k: (b, k, 0)) if g_batched
              else pl.BlockSpec((tk, N), lambda b, k: (k, 0)))
    (out,), slots = _call(
        body, grid=(nb, nk), in_specs=[a_spec, g_spec],
        out_specs=[pl.BlockSpec((None, K, N), lambda b, k: (b, 0, 0))],
        out_shape=[SDS((nb, K, N), BF16)], scratch_shapes=[pltpu.VMEM((K, N), F32)],
        args=(a, g), name=name, comm=comm)
    return out, slots


def dh_pre_bwd(dz, w, x, g, dyres, name, comm=None):
    nb, T, F = dz.shape
    D = w.shape[1]
    tm = 512

    rc = tm // ROW_SPLIT

    def body(dz_ref, w_ref, x_ref, g_ref, dy_ref, dx_ref, dg_ref):
        @pl.when(pl.program_id(0) == 0)
        def _():
            dg_ref[...] = jnp.zeros_like(dg_ref)

        accs = []
        for c in range(ROW_SPLIT):
            rows = pl.ds(c * rc, rc)
            dh = _dot_nt(dz_ref[0, rows, :].astype(BF16), w_ref[0])
            for b in range(1, nb):
                dh += _dot_nt(dz_ref[b, rows, :].astype(BF16), w_ref[b])
            accs.append(dh)
        for c, dh in enumerate(accs):
            rows = pl.ds(c * rc, rc)
            xv = x_ref[rows, :]
            rstd = lax.rsqrt(jnp.mean(xv * xv, axis=-1, keepdims=True) + NORM_EPS)
            xh = xv * rstd
            dg_ref[...] += jnp.sum(dh * xh, axis=0, keepdims=True)
            dhg = dh * g_ref[...]
            dx_ref[rows, :] = dy_ref[rows, :] + rstd * (dhg - xh * jnp.mean(dhg * xh, axis=-1, keepdims=True))

    return _call(
        body, grid=(T // tm,),
        in_specs=[pl.BlockSpec((nb, tm, F), lambda i: (0, i, 0)), _resident((nb, D, F)),
                  pl.BlockSpec((tm, D), lambda i: (i, 0)), pl.BlockSpec((1, D), lambda i: (0, 0)),
                  pl.BlockSpec((tm, D), lambda i: (i, 0))],
        out_specs=[pl.BlockSpec((tm, D), lambda i: (i, 0)), pl.BlockSpec((1, D), lambda i: (0, 0))],
        out_shape=[SDS((T, D), F32), SDS((1, D), F32)],
        args=(dz, w, x, g, dyres), name=name, comm=comm)


ATTN_GROUP = {1: 4, 4: 1, 16: 1}
ATTN_UNROLL = 4


def _attn_masks():
    qi = lax.broadcasted_iota(jnp.int32, (QBLK, QBLK), 0)
    kj = lax.broadcasted_iota(jnp.int32, (QBLK, QBLK), 1)
    cur_ok = kj <= qi
    prev_ok = kj >= qi
    dcur = (qi - kj).astype(F32)
    return cur_ok, prev_ok, dcur, dcur + float(QBLK)


def _head_slopes(p, d):
    out = []
    for hq in range(2):
        v = [float(d) * 2.0 ** -(2 * q + hq + 1) for q in range(4)]
        out.append(jnp.where(p == 0, v[0], jnp.where(p == 1, v[1], jnp.where(p == 2, v[2], v[3]))))
    return out


def _rows(start, d):
    return pl.ds(start, QBLK, stride=d) if d > 1 else pl.ds(start, QBLK)


def _pair_spec(rows, part, blk):
    return pl.BlockSpec((None, rows, PAIR_W), lambda p, n: (2 * part + p // 2, blk(n), p % 2))


def _for_query_blocks(d, groups, several):
    blocks = [(g, r) for g in range(groups) for r in range(d)]
    for s in range(0, len(blocks), ATTN_UNROLL):
        several(blocks[s:s + ATTN_UNROLL])


def attn_fwd(proj, d, name, comm=None):
    T = proj.shape[1]
    sb, groups = QBLK * d, ATTN_GROUP[d]
    rb = sb * groups
    nblk = T // rb

    def body(q_ref, kc_ref, kp_ref, vc_ref, vp_ref, o_ref, l_ref):
        p, n = pl.program_id(0), pl.program_id(1)
        cur_ok, prev_ok, dcur, dprev = _attn_masks()
        first_ok = jnp.logical_and(prev_ok, n > 0)
        lane_head = lax.broadcasted_iota(jnp.int32, (QBLK, PAIR_W), 1) // HEAD_DIM
        slopes = _head_slopes(p, d)

        def several(blocks):
            work = []
            for g, r in blocks:
                rows = _rows(g * sb + r, d)
                q = q_ref[rows, :]
                kc, vc = kc_ref[rows, :].astype(BF16), vc_ref[rows, :].astype(BF16)
                if g == 0:
                    prow, pok = _rows(r, d), first_ok
                    kp, vp = kp_ref[prow, :].astype(BF16), vp_ref[prow, :].astype(BF16)
                else:
                    prow, pok = _rows((g - 1) * sb + r, d), prev_ok
                    kp, vp = kc_ref[prow, :].astype(BF16), vc_ref[prow, :].astype(BF16)
                for hq in range(2):
                    qm = jnp.where(lane_head == hq, q, 0.0).astype(BF16)
                    work.append([rows, hq, pok, vc, vp, _dot_nt(qm, kc), _dot_nt(qm, kp)])
            for w in work:
                _, hq, pok, _, _, sc, sp = w
                sc = jnp.where(cur_ok, sc * 0.125 - slopes[hq] * dcur, NEG)
                sp = jnp.where(pok, sp * 0.125 - slopes[hq] * dprev, NEG)
                m = jnp.maximum(jnp.max(sc, axis=1, keepdims=True), jnp.max(sp, axis=1, keepdims=True))
                pc = jnp.exp(sc - m)
                pp = jnp.exp(sp - m)
                den = jnp.sum(pc, axis=1, keepdims=True) + jnp.sum(pp, axis=1, keepdims=True)
                w[5:] = [pc.astype(BF16), pp.astype(BF16), 1.0 / den, m + jnp.log(den)]
            for i in range(0, len(work), 2):
                o_acc = jnp.zeros((QBLK, PAIR_W), F32)
                l_acc = jnp.zeros((QBLK, PAIR_W), F32)
                for rows, hq, _, vc, vp, pc, pp, inv, lse in work[i:i + 2]:
                    hm = lane_head == hq
                    o_acc = jnp.where(hm, (_dot(pc, vc) + _dot(pp, vp)) * inv, o_acc)
                    l_acc = jnp.where(hm, lse, l_acc)
                o_ref[rows, :] = o_acc
                l_ref[rows, :] = l_acc

        _for_query_blocks(d, groups, several)

    cur = lambda part: _pair_spec(rb, part, lambda n: n)
    prv = lambda part: _pair_spec(sb, part, lambda n: jnp.maximum(n * groups - 1, 0))
    return _call(
        body, grid=(4, nblk), in_specs=[cur(0), cur(1), prv(1), cur(2), prv(2)], out_specs=[cur(0), cur(0)],
        out_shape=[SDS((2, T, 2 * PAIR_W), F32), SDS((2, T, 2 * PAIR_W), F32)],
        args=(proj, proj, proj, proj, proj), name=name, comm=comm)


def attn_merge(os_, ls_, name):
    _, T, HW = os_[0].shape
    tm = 512

    def body(o1, o2, o3, l1, l2, l3, o_ref, l_ref):
        a, b, c = l1[...], l2[...], l3[...]
        m = jnp.maximum(jnp.maximum(a, b), c)
        ea, eb, ec = jnp.exp(a - m), jnp.exp(b - m), jnp.exp(c - m)
        s = ea + eb + ec
        o_ref[...] = (ea * o1[...] + eb * o2[...] + ec * o3[...]) * (1.0 / s)
        l_ref[...] = m + jnp.log(s)

    spec = pl.BlockSpec((None, tm, HW), lambda h, i: (h, i, 0))
    return pl.pallas_call(
        body, grid=(2, T // tm), in_specs=[spec] * 6, out_specs=[spec, spec],
        out_shape=[SDS((2, T, HW), F32), SDS((2, T, HW), F32)],
        compiler_params=_params(2), name=name)(*os_, *ls_)


def attn_bwd(proj, dcat, o, lse, acc, d, name):
    T = proj.shape[1]
    sb, groups = QBLK * d, ATTN_GROUP[d]
    rb = sb * groups
    nblk = T // rb
    has_acc = acc is not None

    def body(*refs):
        (qc_ref, qn_ref, kc_ref, kp_ref, vc_ref, vp_ref, dc_ref, dn_ref, oc_ref, on_ref, lc_ref, ln_ref) = refs[:12]
        acc_ref = refs[12] if has_acc else None
        out_ref = refs[-1]
        p, n = pl.program_id(0), pl.program_id(1)
        cur_ok, prev_ok, dcur, dprev = _attn_masks()
        first_ok = jnp.logical_and(prev_ok, n > 0)
        last_ok = jnp.logical_and(prev_ok, n < nblk - 1)
        lane_head = lax.broadcasted_iota(jnp.int32, (QBLK, PAIR_W), 1) // HEAD_DIM
        slopes = _head_slopes(p, d)

        def one(g, r):
            rows = _rows(g * sb + r, d)
            q_c, do_c, o_c, l_c = qc_ref[rows, :], dc_ref[rows, :], oc_ref[rows, :], lc_ref[rows, :]
            k_c, v_c = kc_ref[rows, :].astype(BF16), vc_ref[rows, :].astype(BF16)
            if g == 0:
                prow, pok_c = _rows(r, d), first_ok
                k_p, v_p = kp_ref[prow, :].astype(BF16), vp_ref[prow, :].astype(BF16)
            else:
                prow, pok_c = _rows((g - 1) * sb + r, d), prev_ok
                k_p, v_p = kc_ref[prow, :].astype(BF16), vc_ref[prow, :].astype(BF16)
            if g == groups - 1:
                nrow, pok_n = _rows(r, d), last_ok
                q_n, do_n, o_n, l_n = qn_ref[nrow, :], dn_ref[nrow, :], on_ref[nrow, :], ln_ref[nrow, :]
            else:
                nrow, pok_n = _rows((g + 1) * sb + r, d), prev_ok
                q_n, do_n, o_n, l_n = qc_ref[nrow, :], dc_ref[nrow, :], oc_ref[nrow, :], lc_ref[nrow, :]
            heads = []
            for hq in range(2):
                hm = lane_head == hq
                qm_c = jnp.where(hm, q_c, 0.0).astype(BF16)
                qm_n = jnp.where(hm, q_n, 0.0).astype(BF16)
                dom_c = jnp.where(hm, do_c, 0.0)
                dom_n = jnp.where(hm, do_n, 0.0)
                dd_c = jnp.sum(dom_c * o_c, axis=1, keepdims=True)
                dd_n = jnp.sum(dom_n * o_n, axis=1, keepdims=True)
                ls_c = jnp.max(jnp.where(hm, l_c, NEG), axis=1, keepdims=True)
                ls_n = jnp.max(jnp.where(hm, l_n, NEG), axis=1, keepdims=True)
                dob_c, dob_n = dom_c.astype(BF16), dom_n.astype(BF16)
                mm = [(_dot_nt(qm_c, k_c), _dot_nt(dob_c, v_c)), (_dot_nt(qm_c, k_p), _dot_nt(dob_c, v_p)),
                      (_dot_nt(qm_n, k_c), _dot_nt(dob_n, v_c))]
                heads.append(dict(hq=hq, qm_c=qm_c, qm_n=qm_n, dob_c=dob_c, dob_n=dob_n, mm=mm,
                                  dd=(dd_c, dd_c, dd_n), ls=(ls_c, ls_c, ls_n)))
            return dict(rows=rows, k_c=k_c, k_p=k_p, heads=heads, oks=(cur_ok, pok_c, pok_n))

        def several(blocks):
            work = [one(g, r) for g, r in blocks]
            for w in work:
                for h in w["heads"]:
                    slope, dist = slopes[h["hq"]], (dcur, dprev, dprev)
                    h["pr"], h["ds"] = [], []
                    for j in range(3):
                        s = jnp.where(w["oks"][j], h["mm"][j][0] * 0.125 - slope * dist[j], NEG)
                        pr = jnp.exp(s - h["ls"][j])
                        h["pr"].append(pr.astype(BF16))
                        h["ds"].append((pr * (h["mm"][j][1] - h["dd"][j])).astype(BF16))
            for w in work:
                dq = jnp.zeros((QBLK, PAIR_W), F32)
                dk = jnp.zeros((QBLK, PAIR_W), F32)
                dv = jnp.zeros((QBLK, PAIR_W), F32)
                for h in w["heads"]:
                    ds, pr = h["ds"], h["pr"]
                    dq_h = _dot(ds[0], w["k_c"]) + _dot(ds[1], w["k_p"])
                    dk += (_dot_tn(ds[0], h["qm_c"]) + _dot_tn(ds[2], h["qm_n"])) * 0.125
                    dv += _dot_tn(pr[0], h["dob_c"]) + _dot_tn(pr[2], h["dob_n"])
                    dq = jnp.where(lane_head == h["hq"], dq_h * 0.125, dq)
                for part, val in enumerate((dq, dk, dv)):
                    if has_acc:
                        val = val + acc_ref.at[part][w["rows"], :]
                    out_ref.at[part][w["rows"], :] = val

        _for_query_blocks(d, groups, several)

    cur = lambda part: _pair_spec(rb, part, lambda n: n)
    prv = lambda part: _pair_spec(sb, part, lambda n: jnp.maximum(n * groups - 1, 0))
    nxt = lambda part: _pair_spec(sb, part, lambda n: jnp.minimum((n + 1) * groups, T // sb - 1))
    full = pl.BlockSpec((3, None, rb, PAIR_W), lambda p, n: (0, p // 2, n, p % 2))
    in_specs = [cur(0), nxt(0), cur(1), prv(1), cur(2), prv(2), cur(0), nxt(0), cur(0), nxt(0), cur(0), nxt(0)]
    args = [proj, proj, proj, proj, proj, proj, dcat, dcat, o, o, lse, lse]
    if has_acc:
        in_specs.append(full)
        args.append(acc)
    return pl.pallas_call(
        body, grid=(4, nblk), in_specs=in_specs, out_specs=full,
        out_shape=SDS((3, 2, T, 2 * PAIR_W), F32), compiler_params=_params(2), name=name)(*args)


def _scan_rows(buf, tab_ref, reverse):
    n_tiles = (buf.shape[0] - 8) // 8
    per_half = HALF_STATES // SCAN_CW
    row = lax.broadcasted_iota(jnp.int32, (8, SCAN_CW), 0)
    sgn = -1.0 if reverse else 1.0

    def chunk(j, _):
        c0 = pl.multiple_of((j // per_half) * 2 * HALF_STATES + (j % per_half) * SCAN_CW, 128)
        cre = pl.ds(c0, SCAN_CW)
        cim = pl.ds(pl.multiple_of(c0 + HALF_STATES, 128), SCAN_CW)
        steps = []
        for s, k in enumerate((1, 2, 4)):
            ok, shift = (row < 8 - k, 8 - k) if reverse else (row >= k, k)
            steps.append((shift, jnp.where(ok, tab_ref[pl.ds(s, 1), cre], 0.0),
                          jnp.where(ok, sgn * tab_ref[pl.ds(s, 1), cim], 0.0)))
        trow = 16 if reverse else 8
        pr, pi = tab_ref[pl.ds(trow, 8), cre], tab_ref[pl.ds(trow, 8), cim]
        for t in range(n_tiles):
            base = 8 * (n_tiles - 1 - t) if reverse else 8 + 8 * t
            rows = pl.ds(base, 8)
            re, im = buf[rows, cre], buf[rows, cim]
            for shift, ar, ai in steps:
                sre, sim = pltpu.roll(re, shift, 0), pltpu.roll(im, shift, 0)
                re, im = re + ar * sre - ai * sim, im + ar * sim + ai * sre
            crow = pl.ds(base + 8 if reverse else base - 1, 1)
            cr, ci = buf[crow, cre], buf[crow, cim]
            buf[rows, cre] = re + pr * cr - pi * ci
            buf[rows, cim] = im + pr * ci + pi * cr
        return 0

    lax.fori_loop(0, 2 * per_half, chunk, 0)


def ssm_fwd(proj, bh, ch, apow, dskip, name, comm=None):
    _, T, C = proj.shape
    tm = SCAN_TM
    SW = 4 * HALF_STATES

    def body(u_ref, bh_ref, ch_ref, tab_ref, dsk_ref, y_ref, s_ref, buf):
        @pl.when(pl.program_id(0) == 0)
        def _():
            buf[pl.ds(0, 8), :] = jnp.zeros((8, SW), F32)

        for h in range(2):
            buf[pl.ds(8, tm), pl.ds(h * 2 * HALF_STATES, 2 * HALF_STATES)] = _dot(u_ref[h].astype(BF16), bh_ref[h])
        _scan_rows(buf, tab_ref, False)
        s_ref[...] = buf[pl.ds(8, tm), :]
        buf[pl.ds(0, 8), :] = buf[pl.ds(tm, 8), :]
        for h in range(2):
            sv = s_ref[:, pl.ds(h * 2 * HALF_STATES, 2 * HALF_STATES)].astype(BF16)
            y_ref[h] = _dot(sv, ch_ref[h]) + dsk_ref[h] * u_ref[h]

    return _call(
        body, grid=(T // tm,),
        in_specs=[pl.BlockSpec((2, tm, C), lambda i: (3, i, 0)),
                  pl.BlockSpec((2, C, 2 * HALF_STATES), lambda i: (0, 0, 0)),
                  pl.BlockSpec((2, 2 * HALF_STATES, C), lambda i: (0, 0, 0)),
                  pl.BlockSpec((24, SW), lambda i: (0, 0)),
                  pl.BlockSpec((2, 1, C), lambda i: (0, 0, 0))],
        out_specs=[pl.BlockSpec((2, tm, C), lambda i: (0, i, 0)), pl.BlockSpec((tm, SW), lambda i: (i, 0))],
        out_shape=[SDS((2, T, C), F32), SDS((T, SW), F32)],
        scratch_shapes=[pltpu.VMEM((tm + 8, SW), F32)],
        args=(proj, bh, ch, apow, dskip), name=name, comm=comm)


def ssm_bwd(dy, proj, st, bh, ch, apow, dskip, name, comm=None):
    _, T, C = proj.shape
    tm = SCAN_TM
    nt = T // tm
    SW = 4 * HALF_STATES
    HS2 = 2 * HALF_STATES

    def body(dy_ref, u_ref, s_ref, sp_ref, bh_ref, ch_ref, tab_ref, dsk_ref,
             du_ref, da_ref, dbh_ref, dch_ref, dd_ref, lam):
        i = pl.program_id(0)

        @pl.when(i == 0)
        def _():
            lam[pl.ds(tm, 8), :] = jnp.zeros((8, SW), F32)
            da_ref[...] = jnp.zeros_like(da_ref)
            dbh_ref[...] = jnp.zeros_like(dbh_ref)
            dch_ref[...] = jnp.zeros_like(dch_ref)
            dd_ref[...] = jnp.zeros_like(dd_ref)

        for h in range(2):
            lam[pl.ds(0, tm), pl.ds(h * HS2, HS2)] = _dot_nt(dy_ref[h].astype(BF16), ch_ref[h])
        _scan_rows(lam, tab_ref, True)

        first = i == nt - 1
        per_half = HALF_STATES // SCAN_CW

        def chunk(j, _):
            c0 = pl.multiple_of((j // per_half) * HS2 + (j % per_half) * SCAN_CW, 128)
            cre, cim = pl.ds(c0, SCAN_CW), pl.ds(pl.multiple_of(c0 + HALF_STATES, 128), SCAN_CW)
            row = lax.broadcasted_iota(jnp.int32, (tm, SCAN_CW), 0)
            pre = jnp.where(first, 0.0, sp_ref[pl.ds(7, 1), cre])
            pim = jnp.where(first, 0.0, sp_ref[pl.ds(7, 1), cim])
            spr = jnp.where(row == 0, pre, pltpu.roll(s_ref[:, cre], 1, 0))
            spi = jnp.where(row == 0, pim, pltpu.roll(s_ref[:, cim], 1, 0))
            lr, li = lam[pl.ds(0, tm), cre], lam[pl.ds(0, tm), cim]
            da_ref[:, cre] += jnp.sum(lr * spr + li * spi, axis=0, keepdims=True)
            da_ref[:, cim] += jnp.sum(li * spr - lr * spi, axis=0, keepdims=True)
            return 0

        lax.fori_loop(0, 2 * per_half, chunk, 0)
        lam[pl.ds(tm, 8), :] = lam[pl.ds(0, 8), :]

        for h in range(2):
            lb = lam[pl.ds(0, tm), pl.ds(h * HS2, HS2)].astype(BF16)
            dyv, uv = dy_ref[h], u_ref[h]
            du_ref[h] = _dot_nt(lb, bh_ref[h]) + dsk_ref[h] * dyv
            dbh_ref[h] += _dot_tn(uv.astype(BF16), lb)
            dch_ref[h] += _dot_tn(s_ref[:, pl.ds(h * HS2, HS2)].astype(BF16), dyv.astype(BF16))
            dd_ref[h] += jnp.sum(dyv * uv, axis=0, keepdims=True)

    rev = lambda i: nt - 1 - i
    return _call(
        body, grid=(nt,),
        in_specs=[pl.BlockSpec((2, tm, C), lambda i: (0, rev(i), 0)),
                  pl.BlockSpec((2, tm, C), lambda i: (3, rev(i), 0)),
                  pl.BlockSpec((tm, SW), lambda i: (rev(i), 0)),
                  pl.BlockSpec((8, SW), lambda i: (jnp.maximum(rev(i) * (tm // 8) - 1, 0), 0)),
                  pl.BlockSpec((2, C, HS2), lambda i: (0, 0, 0)),
                  pl.BlockSpec((2, HS2, C), lambda i: (0, 0, 0)),
                  pl.BlockSpec((24, SW), lambda i: (0, 0)),
                  pl.BlockSpec((2, 1, C), lambda i: (0, 0, 0))],
        out_specs=[pl.BlockSpec((2, tm, C), lambda i: (0, rev(i), 0)),
                   pl.BlockSpec((1, SW), lambda i: (0, 0)),
                   pl.BlockSpec((2, C, HS2), lambda i: (0, 0, 0)),
                   pl.BlockSpec((2, HS2, C), lambda i: (0, 0, 0)),
                   pl.BlockSpec((2, 1, C), lambda i: (0, 0, 0))],
        out_shape=[SDS((2, T, C), F32), SDS((1, SW), F32), SDS((2, C, HS2), F32), SDS((2, HS2, C), F32),
                   SDS((2, 1, C), F32)],
        scratch_shapes=[pltpu.VMEM((tm + 8, SW), F32)],
        args=(dy, proj, st, st, bh, ch, apow, dskip), name=name, comm=comm)


_GELU_C = math.sqrt(2.0 / math.pi)


def _gelu(x):
    t = jnp.tanh(_GELU_C * (x + 0.044715 * x * x * x))
    return 0.5 * x * (1.0 + t), t


def glu_fwd(y, w, b, name):
    _, T, C = y.shape
    tm = 512

    def body(y_ref, w_ref, b_ref, o_ref, lg_ref):
        y0, _ = _gelu(y_ref[0])
        y1, _ = _gelu(y_ref[1])
        lg = _dot(y0.astype(BF16), w_ref[0]) + _dot(y1.astype(BF16), w_ref[1]) + b_ref[...]
        sg = _sigmoid(lg)
        o_ref[0] = y0 * sg[:, :C]
        o_ref[1] = y1 * sg[:, C:]
        lg_ref[0] = lg[:, :C]
        lg_ref[1] = lg[:, C:]

    return pl.pallas_call(
        body, grid=(T // tm,),
        in_specs=[pl.BlockSpec((2, tm, C), lambda i: (0, i, 0)), pl.BlockSpec((2, C, 2 * C), lambda i: (0, 0, 0)),
                  pl.BlockSpec((1, 2 * C), lambda i: (0, 0))],
        out_specs=[pl.BlockSpec((2, tm, C), lambda i: (0, i, 0)), pl.BlockSpec((2, tm, C), lambda i: (0, i, 0))],
        out_shape=[SDS((2, T, C), F32), SDS((2, T, C), F32)], compiler_params=_params(1), name=name)(y, w, b)


def glu_bwd(dcat, y, lg, w, name):
    _, T, C = y.shape
    tm = 512

    def body(d_ref, y_ref, lg_ref, w_ref, dy_ref, dw_ref, db_ref):
        @pl.when(pl.program_id(0) == 0)
        def _():
            dw_ref[...] = jnp.zeros_like(dw_ref)
            db_ref[...] = jnp.zeros_like(db_ref)

        y2, th, sg, dlg = [], [], [], []
        for h in range(2):
            yy, tt = _gelu(y_ref[h])
            ss = _sigmoid(lg_ref[h])
            y2.append(yy)
            th.append(tt)
            sg.append(ss)
            dlg.append(d_ref[h] * yy * ss * (1.0 - ss))
        dl = jnp.concatenate(dlg, axis=1)
        dlb = dl.astype(BF16)
        db_ref[...] += jnp.sum(dl, axis=0, keepdims=True)
        for h in range(2):
            dy2 = d_ref[h] * sg[h] + _dot_nt(dlb, w_ref[h])
            yv = y_ref[h]
            dgelu = 0.5 * (1.0 + th[h]) + 0.5 * yv * (1.0 - th[h] * th[h]) * _GELU_C * (1.0 + 3 * 0.044715 * yv * yv)
            dy_ref[h] = dy2 * dgelu
            dw_ref[h] += _dot_tn(y2[h].astype(BF16), dlb)

    return pl.pallas_call(
        body, grid=(T // tm,),
        in_specs=[pl.BlockSpec((2, tm, C), lambda i: (1, i, 0)), pl.BlockSpec((2, tm, C), lambda i: (0, i, 0)),
                  pl.BlockSpec((2, tm, C), lambda i: (0, i, 0)), pl.BlockSpec((2, C, 2 * C), lambda i: (0, 0, 0))],
        out_specs=[pl.BlockSpec((2, tm, C), lambda i: (0, i, 0)), pl.BlockSpec((2, C, 2 * C), lambda i: (0, 0, 0)),
                   pl.BlockSpec((1, 2 * C), lambda i: (0, 0))],
        out_shape=[SDS((2, T, C), F32), SDS((2, C, 2 * C), F32), SDS((1, 2 * C), F32)],
        compiler_params=_params(1), name=name)(dcat, y, lg, w)


def adamw(w, m, v, slots, name):
    R, C = w.shape
    tr = R
    for cand in (512, 256, 128, 64, 32, 16, 8):
        if R % cand == 0 and cand * C * 4 <= 2 * 1024 * 1024:
            tr = cand
            break
    c1 = 1.0 / (1.0 - ADAM_B1 ** ADAM_STEP)
    c2 = 1.0 / (1.0 - ADAM_B2 ** ADAM_STEP)

    def body(w_ref, m_ref, v_ref, s_ref, g_ref, d_ref, nm_ref, nv_ref):
        g = s_ref[0].astype(F32)
        for j in range(1, N_DEV):
            g = g + s_ref[j].astype(F32)
        nm = ADAM_B1 * m_ref[...] + (1.0 - ADAM_B1) * g
        nv = ADAM_B2 * v_ref[...] + (1.0 - ADAM_B2) * (g * g)
        g_ref[...] = g
        nm_ref[...] = nm
        nv_ref[...] = nv
        d_ref[...] = -ADAM_LR * ((nm * c1) / (jnp.sqrt(nv * c2) + ADAM_EPS) + ADAM_WD * w_ref[...])

    spec = pl.BlockSpec((tr, C), lambda i: (i, 0))
    return pl.pallas_call(
        body, grid=(R // tr,),
        in_specs=[spec, spec, spec, pl.BlockSpec((N_DEV, tr, C), lambda i: (0, i, 0))],
        out_specs=[spec] * 4, out_shape=[SDS((R, C), F32)] * 4, compiler_params=_params(1), name=name)(w, m, v, slots)


def _discretise(a_re, a_im, log_dt, b_re, b_im):
    dt = jnp.exp(log_dt)[:, None]
    e = jnp.exp(dt * a_re)
    ar, ai = e * jnp.cos(dt * a_im), e * jnp.sin(dt * a_im)
    den = a_re * a_re + a_im * a_im
    nr, ni = ar - 1.0, ai
    wr = (nr * a_re + ni * a_im) / den
    wi = (ni * a_re - nr * a_im) / den
    bbr = wr[..., None] * b_re - wi[..., None] * b_im
    bbi = wr[..., None] * b_im + wi[..., None] * b_re
    return ar, ai, bbr, bbi


def _block_diag(t):
    eye = jnp.eye(16, dtype=t.dtype).reshape(1, 16, 1, 16, 1)
    r, c = t.shape[1], t.shape[2]
    return (t.reshape(2, 16, r, 1, c) * eye).reshape(2, 16 * r, 16 * c)


def _diag_blocks(m, r, c):
    eye = jnp.eye(16, dtype=m.dtype).reshape(1, 16, 1, 16, 1)
    return jnp.sum(m.reshape(2, 16, r, 16, c) * eye, axis=3).reshape(32, r, c)


def _state_vec(re, im):
    return jnp.stack([re.reshape(2, HALF_STATES), im.reshape(2, HALF_STATES)], axis=1).reshape(-1)


BIG = ("ffn1_w_in", "ffn1_w_out", "w_mix_in", "w_glu", "w_mix_out", "ffn2_w_in", "ffn2_w_out")
WEIGHTS = ("ffn1_pre_g", "ffn1_w_in", "ffn1_w_out", "ffn1_post_g", "mix_pre_g", "w_mix_in", "a_re", "a_im", "log_dt",
           "b_re", "b_im", "c_re", "c_im", "d_skip", "w_glu", "b_glu", "w_mix_out", "mix_post_g", "ffn2_pre_g",
           "ffn2_w_in", "ffn2_w_out", "ffn2_post_g")
SMALL = tuple(n for n in WEIGHTS if n not in BIG)
PACK_COLS = 1024


def _pack(parts):
    flat = jnp.concatenate([p.reshape(-1) for p in parts])
    rows = -(-flat.shape[0] // (8 * PACK_COLS)) * 8
    return jnp.pad(flat, (0, rows * PACK_COLS - flat.shape[0])).reshape(rows, PACK_COLS)


def _unpack(packed, shapes):
    flat, out, off = packed.reshape(-1), [], 0
    for s in shapes:
        n = math.prod(s)
        out.append(flat[off:off + n].reshape(s))
        off += n
    return out


def _gather(names, wb):
    return [wb[n] for n in names], [False] * len(names)


def _ffn_bwd(dy, saved, x, pre_g, w_in, w_out4, post_g, tag):
    h, z, a, o = saved
    T = x.shape[0]
    do, dg_post = post_bwd(dy, o, post_g, 0.5, f"{tag}_post_bwd")
    dz = ffn_dact(do, w_out4, z, f"{tag}_dact")
    dz8 = dz.reshape(8, T, dz.shape[-1])
    dw_out, _ = mm_tn(a, do, True, False, 4, f"{tag}_dwout")
    dw_in, (s_out,) = mm_tn(h, dz8, False, True, 8, f"{tag}_dwin", comm=([dw_out.reshape(8, -1, D_MODEL)], [True]))
    (dx, dg_pre), (s_in,) = dh_pre_bwd(dz8, w_in, x, pre_g, dy, f"{tag}_dh", comm=([dw_in], [True]))
    return dx, dg_pre, dg_post, (s_in, s_out)


def local_step(x, tgt, sp, wb):
    T = x.shape[0]
    ar, ai, bbr, bbi = _discretise(sp["a_re"], sp["a_im"], sp["log_dt"], sp["b_re"], sp["b_im"])
    powers = [(ar, ai)]
    for _ in range(7):
        pr, pi = powers[-1]
        powers.append((pr * ar - pi * ai, pr * ai + pi * ar))
    zero = jnp.zeros_like(ar)
    rows = [_state_vec(*powers[k - 1]) for k in (1, 2, 4)] + [_state_vec(zero, zero)] * 5
    rows += [_state_vec(pr, pi) for pr, pi in powers]
    rows += [_state_vec(pr, -pi) for pr, pi in reversed(powers)]
    apow = jnp.stack(rows)
    bh = jnp.concatenate([_block_diag(bbr.transpose(0, 2, 1)), _block_diag(bbi.transpose(0, 2, 1))], axis=2)
    ch = jnp.concatenate([_block_diag(sp["c_re"].transpose(0, 2, 1)), _block_diag(-sp["c_im"].transpose(0, 2, 1))], axis=1)
    bh, ch = bh.astype(BF16), ch.astype(BF16)
    dskip = sp["d_skip"].reshape(2, 1, 256)

    w1_in = gather_two_level(wb["ffn1_w_in"], "gather_w1in")
    (h1, z1, a1), (w1_out, w_mi) = ffn_in(
        x, sp["ffn1_pre_g"], w1_in, "ffn1_in", comm=_gather(["ffn1_w_out", "w_mix_in"], wb))
    w1_out4 = w1_out.reshape(4, -1, D_MODEL)
    (o1, x1), (w_glu, w_mo) = mm_acc_norm(
        a1, w1_out4, x, sp["ffn1_post_g"], 0.5, "ffn1_out", comm=_gather(["w_glu", "w_mix_out"], wb))
    w_glu2, w_mo4 = w_glu.reshape(2, 256, 512), w_mo.reshape(4, 256, D_MODEL)
    h2, proj = norm_proj(x1, sp["mix_pre_g"], w_mi, "mix_proj")
    (y_ssm, states), (w2_in,) = ssm_fwd(proj, bh, ch, apow, dskip, "ssm_fwd", comm=_gather(["ffn2_w_in"], wb))
    os_, ls_ = [], []
    for d in DILATIONS:
        (o_d, l_d), got = attn_fwd(proj, d, f"attn_fwd_d{d}",
                                   comm=_gather(["ffn2_w_out"], wb) if d == DILATIONS[-1] else None)
        os_.append(o_d)
        ls_.append(l_d)
    w2_out4 = got[0].reshape(4, -1, D_MODEL)
    o_att, lse = attn_merge(os_, ls_, "attn_merge")
    o_ssm, lg = glu_fwd(y_ssm, w_glu2, sp["b_glu"], "glu_fwd")
    cat = jnp.concatenate([o_att, o_ssm], axis=0)
    (mixed, x2), _ = mm_acc_norm(cat, w_mo4, x1, sp["mix_post_g"], 1.0, "mix_out")
    (h3, z3, a3), _ = ffn_in(x2, sp["ffn2_pre_g"], w2_in, "ffn2_in")
    (o3, dy3, sq), _ = mm_acc_norm(a3, w2_out4, x2, sp["ffn2_post_g"], 0.5, "ffn2_out", tgt=tgt)

    dx2, dg_f2pre, dg_f2post, (s_w2in, s_w2out) = _ffn_bwd(
        dy3, (h3, z3, a3, o3), x2, sp["ffn2_pre_g"], w2_in, w2_out4, sp["ffn2_post_g"], "ffn2")
    dmixed, dg_mpost = post_bwd(dx2, mixed, sp["mix_post_g"], 1.0, "mix_post_bwd")
    dcat = mm_nt_b(dmixed, w_mo4, "mix_dcat")
    dw_mo, _ = mm_tn(cat, dmixed, True, False, 4, "mix_dwout")
    dy_ssm, dw_glu, db_glu = glu_bwd(dcat, y_ssm, lg, w_glu2, "glu_bwd")
    (du, da, dbh, dch, dd), (s_wmo, s_wglu) = ssm_bwd(
        dy_ssm, proj, states, bh, ch, apow, dskip, "ssm_bwd",
        comm=([dw_mo.reshape(8, 128, D_MODEL), dw_glu.astype(BF16).reshape(8, 64, 512)], [True, True]))
    dqkv = None
    for d in DILATIONS:
        dqkv = attn_bwd(proj, dcat, o_att, lse, dqkv, d, f"attn_bwd_d{d}")
    dproj = jnp.concatenate([dqkv.reshape(6, T, 256), du], axis=0)
    dw_mi, _ = mm_tn(h2, dproj, False, True, 8, "mix_dwin")
    (dx1, dg_mpre), (s_wmi,) = dh_pre_bwd(
        dproj, w_mi, x1, sp["mix_pre_g"], dx2, "mix_dh", comm=([dw_mi], [True]))
    dx0, dg_f1pre, dg_f1post, (s_w1in, s_w1out) = _ffn_bwd(
        dx1, (h1, z1, a1, o1), x, sp["ffn1_pre_g"], w1_in, w1_out4, sp["ffn1_post_g"], "ffn1")

    da4 = da.reshape(2, 2, HALF_STATES)
    d_ar, d_ai = da4[:, 0].reshape(32, N_STATE), da4[:, 1].reshape(32, N_STATE)
    d_bbr = _diag_blocks(dbh[:, :, :HALF_STATES], 16, N_STATE).transpose(0, 2, 1)
    d_bbi = _diag_blocks(dbh[:, :, HALF_STATES:], 16, N_STATE).transpose(0, 2, 1)
    _, disc_vjp = jax.vjp(_discretise, sp["a_re"], sp["a_im"], sp["log_dt"], sp["b_re"], sp["b_im"])
    g_are, g_aim, g_ldt, g_bre, g_bim = disc_vjp((d_ar, d_ai, d_bbr, d_bbi))
    g_cre = _diag_blocks(dch[:, :HALF_STATES], N_STATE, 16).transpose(0, 2, 1)
    g_cim = -_diag_blocks(dch[:, HALF_STATES:], N_STATE, 16).transpose(0, 2, 1)
    small = {
        "ffn1_pre_g": dg_f1pre, "ffn1_post_g": dg_f1post, "mix_pre_g": dg_mpre, "a_re": g_are, "a_im": g_aim,
        "log_dt": g_ldt, "b_re": g_bre, "b_im": g_bim, "c_re": g_cre, "c_im": g_cim, "d_skip": dd.reshape(1, 512),
        "b_glu": db_glu, "mix_post_g": dg_mpost, "ffn2_pre_g": dg_f2pre, "ffn2_post_g": dg_f2post,
    }
    small_slots = gather_two_level(_pack([small[n] for n in SMALL]), "exchange_small")
    big_slots = {"ffn1_w_in": s_w1in, "ffn1_w_out": s_w1out, "w_mix_in": s_wmi, "w_glu": s_wglu, "w_mix_out": s_wmo,
                 "ffn2_w_in": s_w2in, "ffn2_w_out": s_w2out}
    return sq, dx0, big_slots, small_slots


def kernel(x, ffn1_pre_g, ffn1_w_in, ffn1_w_out, ffn1_post_g, mix_pre_g, w_mix_in, a_re, a_im, log_dt, b_re, b_im, c_re, c_im, d_skip, w_glu, b_glu, w_mix_out, mix_post_g, ffn2_pre_g, ffn2_w_in, ffn2_w_out, ffn2_post_g, loss_target, m_ffn1_pre_g, m_ffn1_w_in, m_ffn1_w_out, m_ffn1_post_g, m_mix_pre_g, m_w_mix_in, m_a_re, m_a_im, m_log_dt, m_b_re, m_b_im, m_c_re, m_c_im, m_d_skip, m_w_glu, m_b_glu, m_w_mix_out, m_mix_post_g, m_ffn2_pre_g, m_ffn2_w_in, m_ffn2_w_out, m_ffn2_post_g, v_ffn1_pre_g, v_ffn1_w_in, v_ffn1_w_out, v_ffn1_post_g, v_mix_pre_g, v_w_mix_in, v_a_re, v_a_im, v_log_dt, v_b_re, v_b_im, v_c_re, v_c_im, v_d_skip, v_w_glu, v_b_glu, v_w_mix_out, v_mix_post_g, v_ffn2_pre_g, v_ffn2_w_in, v_ffn2_w_out, v_ffn2_post_g):
    args = dict(locals())
    w = {n: args[n][0] for n in WEIGHTS}
    m = {n: args["m_" + n][0] for n in WEIGHTS}
    v = {n: args["v_" + n][0] for n in WEIGHTS}

    wb = {n: w[n].astype(BF16) for n in BIG}
    sp = {n: w[n] for n in SMALL}
    for n in ("ffn1_pre_g", "ffn1_post_g", "mix_pre_g", "mix_post_g", "ffn2_pre_g", "ffn2_post_g", "b_glu", "d_skip"):
        sp[n] = w[n].reshape(1, -1)

    sq, grad_x, big_slots, small_slots = local_step(x[0], loss_target[0], sp, wb)
    loss = lax.psum(0.5 / D_MODEL * jnp.sum(sq), ("x", "y", "c"))

    outs = {}
    for n in BIG:
        shp = w[n].shape
        r2 = lambda t: t.reshape(-1, shp[-1])
        res = adamw(r2(w[n]), r2(m[n]), r2(v[n]), big_slots[n].reshape(N_DEV, -1, shp[-1]), f"adamw_{n}")
        outs[n] = [t.reshape((1,) + shp) for t in res]
    res = adamw(_pack([w[n] for n in SMALL]), _pack([m[n] for n in SMALL]), _pack([v[n] for n in SMALL]),
                small_slots, "adamw_small")
    shapes = [(1,) + w[n].shape for n in SMALL]
    unpacked = [_unpack(t, shapes) for t in res]
    for j, n in enumerate(SMALL):
        outs[n] = [unpacked[k][j] for k in range(4)]

    result = [loss, grad_x[None]]
    for k in range(4):
        result += [outs[n][k] for n in WEIGHTS]
    return tuple(result)
```

```python
import functools
import math

import jax
import jax.numpy as jnp
from jax import lax
from jax.experimental import pallas as pl
from jax.experimental.pallas import tpu as pltpu

F32, BF16 = jnp.float32, jnp.bfloat16
SDS = jax.ShapeDtypeStruct

D_MODEL = 1024
N_DEV = 8
HEAD_DIM = 64
PAIR_W = 128
QBLK = 128
DILATIONS = (1, 4, 16)
N_STATE = 64
HALF_STATES = 1024
NORM_EPS = 1e-6
NEG = -1e30
VMEM_LIMIT = 56 * 1024 * 1024
ADAM_LR, ADAM_B1, ADAM_B2, ADAM_EPS, ADAM_WD, ADAM_STEP = 1e-3, 0.9, 0.999, 1e-8, 0.01, 10
SCAN_TM = 256
SCAN_CW = 512


def _params(n_grid):
    return pltpu.CompilerParams(dimension_semantics=("arbitrary",) * n_grid, vmem_limit_bytes=VMEM_LIMIT)


def _dot(a, b):
    return jnp.dot(a, b, preferred_element_type=F32)


def _dot_nt(a, b):
    return lax.dot_general(a, b, (((1,), (1,)), ((), ())), preferred_element_type=F32)


def _dot_tn(a, b):
    return lax.dot_general(a, b, (((0,), (0,)), ((), ())), preferred_element_type=F32)


def _sigmoid(v):
    return 1.0 / (1.0 + jnp.exp(-v))


def _resident(shape):
    return pl.BlockSpec(shape, lambda i: (0,) * len(shape), pipeline_mode=pl.Buffered(1))


ROW_SPLIT = 2


def _exchange_phase(ins, outs, scatter, sems, start):
    send_sems, recv_sems, loc_sems = sems
    x, y, c = lax.axis_index("x"), lax.axis_index("y"), lax.axis_index("c")
    me = 4 * x + 2 * y + c
    own_copies, sends, arrivals = [], [], []
    for i in range(len(ins)):
        own = ins[i].at[me] if scatter[i] else ins[i]
        own_copies.append(pltpu.make_async_copy(own, outs[i].at[me], loc_sems.at[i]))
        for k in range(1, N_DEV):
            px = 1 - x if k & 4 else x
            py = 1 - y if k & 2 else y
            pc = 1 - c if k & 1 else c
            peer = 4 * px + 2 * py + pc
            src = ins[i].at[peer] if scatter[i] else ins[i]
            common = dict(src_ref=src, send_sem=send_sems.at[i, k - 1], recv_sem=recv_sems.at[i, k - 1],
                          device_id=(px, py, pc), device_id_type=pl.DeviceIdType.MESH)
            sends.append(pltpu.make_async_remote_copy(dst_ref=outs[i].at[me], **common))
            if not start:
                arrivals.append(pltpu.make_async_remote_copy(dst_ref=outs[i].at[peer], **common))
    if start:
        for cp in own_copies + sends:
            cp.start()
    else:
        for cp in arrivals:
            cp.wait_recv()
        for cp in sends:
            cp.wait_send()
        for cp in own_copies:
            cp.wait()


def _comm_shapes(arrs, scatter):
    n = len(arrs)
    out_shapes = [SDS(a.shape if scatter[i] else (N_DEV,) + a.shape, a.dtype) for i, a in enumerate(arrs)]
    sems = [pltpu.SemaphoreType.DMA((n, N_DEV - 1)), pltpu.SemaphoreType.DMA((n, N_DEV - 1)),
            pltpu.SemaphoreType.DMA((n,))]
    return out_shapes, sems


def gather_two_level(arr, name):
    def body(x_ref, out_ref, send_sems, recv_sems, local_sem):
        x, y, c = lax.axis_index("x"), lax.axis_index("y"), lax.axis_index("c")
        sibling = (x, y, 1 - c)
        chips = [(1 - x, y), (x, 1 - y), (1 - x, 1 - y)]

        def slot(px, py, pc):
            return out_ref.at[4 * px + 2 * py + pc]

        def copy(k, block, to, src=None):
            return pltpu.make_async_remote_copy(
                src_ref=slot(*block) if src is None else src, dst_ref=slot(*block),
                send_sem=send_sems.at[k], recv_sem=recv_sems.at[k], device_id=to, device_id_type=pl.DeviceIdType.MESH)

        mine = pltpu.make_async_copy(x_ref, slot(x, y, c), local_sem)
        mine.start()
        first = [copy(0, (x, y, c), sibling, src=x_ref)]
        first += [copy(1 + j, (x, y, c), (*chip, c), src=x_ref) for j, chip in enumerate(chips)]
        for cp in first:
            cp.start()
        passed = [copy(4 + j, (*chip, c), sibling) for j, chip in enumerate(chips)]
        for j, chip in enumerate(chips):
            copy(1 + j, (*chip, c), (x, y, c)).wait_recv()
            passed[j].start()
        copy(0, sibling, (x, y, c)).wait_recv()
        for j, chip in enumerate(chips):
            copy(4 + j, (*chip, 1 - c), (x, y, c)).wait_recv()
        for cp in first + passed:
            cp.wait_send()
        mine.wait()

    anyspec = pl.BlockSpec(memory_space=pl.ANY)
    return pl.pallas_call(
        body, in_specs=[anyspec], out_specs=anyspec, out_shape=SDS((N_DEV,) + arr.shape, arr.dtype),
        scratch_shapes=[pltpu.SemaphoreType.DMA((N_DEV - 1,)), pltpu.SemaphoreType.DMA((N_DEV - 1,)),
                        pltpu.SemaphoreType.DMA],
        compiler_params=pltpu.CompilerParams(has_side_effects=True), name=name)(arr)


def _call(body, *, grid, in_specs, out_specs, out_shape, args, name, scratch_shapes=(), comm=None):
    n_grid, scratch_shapes = len(grid), list(scratch_shapes)
    if comm is None:
        outs = pl.pallas_call(body, grid=grid, in_specs=in_specs, out_specs=out_specs, out_shape=out_shape,
                              scratch_shapes=scratch_shapes, compiler_params=_params(n_grid), name=name)(*args)
        return outs, []
    arrs, scatter = comm
    nc, n_in, n_out, n_sc = len(arrs), len(in_specs), len(out_specs), len(scratch_shapes)
    comm_shapes, sems = _comm_shapes(arrs, scatter)

    def wrapped(*refs):
        ins, cins = refs[:n_in], refs[n_in:n_in + nc]
        o0 = n_in + nc
        outs, couts = refs[o0:o0 + n_out], refs[o0 + n_out:o0 + n_out + nc]
        s0 = o0 + n_out + nc
        scratch, sem_refs = refs[s0:s0 + n_sc], refs[s0 + n_sc:]
        first = functools.reduce(jnp.logical_and, [pl.program_id(k) == 0 for k in range(n_grid)])
        last = functools.reduce(jnp.logical_and, [pl.program_id(k) == grid[k] - 1 for k in range(n_grid)])

        @pl.when(first)
        def _():
            _exchange_phase(cins, couts, scatter, sem_refs, True)

        body(*ins, *outs, *scratch)

        @pl.when(last)
        def _():
            _exchange_phase(cins, couts, scatter, sem_refs, False)

    anyspec = pl.BlockSpec(memory_space=pl.ANY)
    res = pl.pallas_call(
        wrapped, grid=grid, in_specs=list(in_specs) + [anyspec] * nc, out_specs=list(out_specs) + [anyspec] * nc,
        out_shape=list(out_shape) + comm_shapes, scratch_shapes=scratch_shapes + sems,
        compiler_params=pltpu.CompilerParams(dimension_semantics=("arbitrary",) * n_grid,
                                             vmem_limit_bytes=VMEM_LIMIT, has_side_effects=True),
        name=name)(*args, *arrs)
    return res[:n_out], res[n_out:]


def _rms(xv, g):
    r = lax.rsqrt(jnp.mean(xv * xv, axis=-1, keepdims=True) + NORM_EPS)
    return (xv * r * g).astype(BF16)


def ffn_in(x, g, w, name, comm=None):
    T, D = x.shape
    F = w.shape[2]
    tm = 512

    def body(x_ref, g_ref, w_ref, h_ref, z_ref, a_ref):
        hv = _rms(x_ref[...], g_ref[...])
        h_ref[...] = hv
        pending = None
        for j in range(5):
            if j < 4:
                zs = (_dot(hv, w_ref[j]), _dot(hv, w_ref[j + 4]))
            if pending is not None:
                zg, zu = pending
                z_ref[0, j - 1] = zg.astype(BF16)
                z_ref[1, j - 1] = zu.astype(BF16)
                a_ref[j - 1] = (zg * _sigmoid(zg) * zu).astype(BF16)
            pending = zs

    return _call(
        body, grid=(T // tm,),
        in_specs=[pl.BlockSpec((tm, D), lambda i: (i, 0)), pl.BlockSpec((1, D), lambda i: (0, 0)),
                  _resident((8, D, F))],
        out_specs=[pl.BlockSpec((tm, D), lambda i: (i, 0)), pl.BlockSpec((2, 4, tm, F), lambda i: (0, 0, i, 0)),
                   pl.BlockSpec((4, tm, F), lambda i: (0, i, 0))],
        out_shape=[SDS((T, D), BF16), SDS((2, 4, T, F), BF16), SDS((4, T, F), BF16)],
        args=(x, g, w), name=name, comm=comm)


def norm_proj(x, g, w, name):
    T, K = x.shape
    nb, _, N = w.shape
    tm = 512

    def body(x_ref, g_ref, w_ref, h_ref, o_ref):
        hv = _rms(x_ref[...], g_ref[...])
        h_ref[...] = hv
        for b in range(nb):
            o_ref[b] = _dot(hv, w_ref[b])

    return pl.pallas_call(
        body, grid=(T // tm,),
        in_specs=[pl.BlockSpec((tm, K), lambda i: (i, 0)), pl.BlockSpec((1, K), lambda i: (0, 0)),
                  _resident((nb, K, N))],
        out_specs=[pl.BlockSpec((tm, K), lambda i: (i, 0)), pl.BlockSpec((nb, tm, N), lambda i: (0, i, 0))],
        out_shape=[SDS((T, K), BF16), SDS((nb, T, N), F32)], compiler_params=_params(1), name=name)(x, g, w)


def mm_acc_norm(a, w, xres, g, scale, name, comm=None, tgt=None):
    nb, T, K = a.shape
    D = w.shape[2]
    tm = 512
    rc = tm // ROW_SPLIT
    with_loss = tgt is not None

    def body(a_ref, w_ref, x_ref, g_ref, *rest):
        if with_loss:
            t_ref, dy_ref, sq_ref, do_ref, dg_ref = rest

            @pl.when(pl.program_id(0) == 0)
            def _():
                sq_ref[...] = jnp.zeros_like(sq_ref)
                dg_ref[...] = jnp.zeros_like(dg_ref)
        else:
            o_ref, y_ref = rest
        accs = []
        for c in range(ROW_SPLIT):
            rows = pl.ds(c * rc, rc)
            o = _dot(a_ref[0, rows, :].astype(BF16), w_ref[0])
            for b in range(1, nb):
                o += _dot(a_ref[b, rows, :].astype(BF16), w_ref[b])
            accs.append(o)
        for c, o in enumerate(accs):
            rows = pl.ds(c * rc, rc)
            r = lax.rsqrt(jnp.mean(o * o, axis=-1, keepdims=True) + NORM_EPS)
            y = x_ref[rows, :] + scale * (o * r * g_ref[...])
            if with_loss:
                e = y - t_ref[rows, :]
                dy = e * (1.0 / D)
                dy_ref[rows, :] = dy
                sq_ref[...] += jnp.sum(e * e, axis=0, keepdims=True)
                do, dg = _post_bwd(dy, o, g_ref[...], scale)
                do_ref[rows, :] = do
                dg_ref[...] += dg
            else:
                o_ref[rows, :] = o
                y_ref[rows, :] = y

    tile = pl.BlockSpec((tm, D), lambda i: (i, 0))
    row = pl.BlockSpec((1, D), lambda i: (0, 0))
    in_specs = [pl.BlockSpec((nb, tm, K), lambda i: (0, i, 0)), _resident((nb, K, D)), tile, row]
    args = (a, w, xres, g)
    if with_loss:
        return _call(body, grid=(T // tm,), in_specs=in_specs + [tile], out_specs=[tile, row, tile, row],
                     out_shape=[SDS((T, D), F32), SDS((1, D), F32), SDS((T, D), BF16), SDS((1, D), F32)],
                     args=args + (tgt,), name=name, comm=comm)
    return _call(body, grid=(T // tm,), in_specs=in_specs, out_specs=[tile, tile],
                 out_shape=[SDS((T, D), F32), SDS((T, D), F32)], args=args, name=name, comm=comm)


def _post_bwd(dy, ov, g, scale):
    r = scale * dy
    rstd = lax.rsqrt(jnp.mean(ov * ov, axis=-1, keepdims=True) + NORM_EPS)
    oh = ov * rstd
    rg = r * g
    do = rstd * (rg - oh * jnp.mean(rg * oh, axis=-1, keepdims=True))
    return do.astype(BF16), jnp.sum(r * oh, axis=0, keepdims=True)


def mm_nt_b(gr, w, name):
    T, N = gr.shape
    nb, K, _ = w.shape
    tm = 512

    def body(g_ref, w_ref, o_ref):
        gv = g_ref[...]
        for b in range(nb):
            o_ref[b] = _dot_nt(gv, w_ref[b])

    return pl.pallas_call(
        body, grid=(T // tm,),
        in_specs=[pl.BlockSpec((tm, N), lambda i: (i, 0)), _resident((nb, K, N))],
        out_specs=pl.BlockSpec((nb, tm, K), lambda i: (0, i, 0)),
        out_shape=SDS((nb, T, K), F32), compiler_params=_params(1), name=name)(gr, w)


def ffn_dact(do, w_out, z, name):
    T, D = do.shape
    nb, F, _ = w_out.shape
    tm = 512

    rc = tm // ROW_SPLIT

    def body(g_ref, w_ref, z_ref, dz_ref):
        das = [_dot_nt(g_ref[pl.ds(c * rc, rc), :], w_ref[...]) for c in range(ROW_SPLIT)]
        for c, da in enumerate(das):
            rows = pl.ds(c * rc, rc)
            zg = z_ref[0, rows, :].astype(F32)
            zu = z_ref[1, rows, :].astype(F32)
            sg = _sigmoid(zg)
            dz_ref[0, rows, :] = (da * zu * (sg * (1.0 + zg * (1.0 - sg)))).astype(BF16)
            dz_ref[1, rows, :] = (da * zg * sg).astype(BF16)

    return pl.pallas_call(
        body, grid=(nb, T // tm),
        in_specs=[pl.BlockSpec((tm, D), lambda b, i: (i, 0)), pl.BlockSpec((None, F, D), lambda b, i: (b, 0, 0)),
                  pl.BlockSpec((2, None, tm, F), lambda b, i: (0, b, i, 0))],
        out_specs=pl.BlockSpec((2, None, tm, F), lambda b, i: (0, b, i, 0)),
        out_shape=SDS((2, nb, T, F), BF16), compiler_params=_params(2), name=name)(do, w_out, z)


def mm_tn(a, g, a_batched, g_batched, nb, name, comm=None):
    T = a.shape[-2]
    K, N = a.shape[-1], g.shape[-1]
    tk = 2048
    nk = T // tk

    def body(a_ref, g_ref, o_ref, acc):
        k = pl.program_id(1)

        @pl.when(k == 0)
        def _():
            acc[...] = jnp.zeros_like(acc)

        acc[...] += _dot_tn(a_ref[...].astype(BF16), g_ref[...].astype(BF16))

        @pl.when(k == nk - 1)
        def _():
            o_ref[...] = acc[...].astype(BF16)

    a_spec = (pl.BlockSpec((None, tk, K), lambda b, k: (b, k, 0)) if a_batched
              else pl.BlockSpec((tk, K), lambda b, k: (k, 0)))
    g_spec = (pl.BlockSpec((None, tk, N), lambda b, k: (b, k, 0)) if g_batched
              else pl.BlockSpec((tk, N), lambda b, k: (k, 0)))
    (out,), slots = _call(
        body, grid=(nb, nk), in_specs=[a_spec, g_spec],
        out_specs=[pl.BlockSpec((None, K, N), lambda b, k: (b, 0, 0))],
        out_shape=[SDS((nb, K, N), BF16)], scratch_shapes=[pltpu.VMEM((K, N), F32)],
        args=(a, g), name=name, comm=comm)
    return out, slots


def dh_pre_bwd(dz, w, x, g, dyres, name, comm=None, post=None):
    nb, T, F = dz.shape
    D = w.shape[1]
    tm = 512
    rc = tm // ROW_SPLIT

    def body(dz_ref, w_ref, x_ref, g_ref, dy_ref, *rest):
        if post is None:
            dx_ref, dg_ref = rest
        else:
            o_ref, gp_ref, dx_ref, dg_ref, do_ref, dgp_ref = rest

        @pl.when(pl.program_id(0) == 0)
        def _():
            dg_ref[...] = jnp.zeros_like(dg_ref)
            if post is not None:
                dgp_ref[...] = jnp.zeros_like(dgp_ref)

        accs = []
        for c in range(ROW_SPLIT):
            rows = pl.ds(c * rc, rc)
            dh = _dot_nt(dz_ref[0, rows, :].astype(BF16), w_ref[0])
            for b in range(1, nb):
                dh += _dot_nt(dz_ref[b, rows, :].astype(BF16), w_ref[b])
            accs.append(dh)
        for c, dh in enumerate(accs):
            rows = pl.ds(c * rc, rc)
            xv = x_ref[rows, :]
            rstd = lax.rsqrt(jnp.mean(xv * xv, axis=-1, keepdims=True) + NORM_EPS)
            xh = xv * rstd
            dg_ref[...] += jnp.sum(dh * xh, axis=0, keepdims=True)
            dhg = dh * g_ref[...]
            dx = dy_ref[rows, :] + rstd * (dhg - xh * jnp.mean(dhg * xh, axis=-1, keepdims=True))
            dx_ref[rows, :] = dx
            if post is not None:
                do, dgp = _post_bwd(dx, o_ref[rows, :], gp_ref[...], post[2])
                do_ref[rows, :] = do
                dgp_ref[...] += dgp

    tile = pl.BlockSpec((tm, D), lambda i: (i, 0))
    row = pl.BlockSpec((1, D), lambda i: (0, 0))
    in_specs = [pl.BlockSpec((nb, tm, F), lambda i: (0, i, 0)), _resident((nb, D, F)), tile, row, tile]
    out_specs, out_shape, args = [tile, row], [SDS((T, D), F32), SDS((1, D), F32)], (dz, w, x, g, dyres)
    if post is not None:
        in_specs += [tile, row]
        out_specs += [tile, row]
        out_shape += [SDS((T, D), BF16), SDS((1, D), F32)]
        args += (post[0], post[1])
    return _call(body, grid=(T // tm,), in_specs=in_specs, out_specs=out_specs, out_shape=out_shape,
                 args=args, name=name, comm=comm)


ATTN_GROUP = {1: 4, 4: 1, 16: 1}
ATTN_UNROLL = 4


def _attn_masks():
    qi = lax.broadcasted_iota(jnp.int32, (QBLK, QBLK), 0)
    kj = lax.broadcasted_iota(jnp.int32, (QBLK, QBLK), 1)
    cur_ok = kj <= qi
    prev_ok = kj >= qi
    dcur = (qi - kj).astype(F32)
    return cur_ok, prev_ok, dcur, dcur + float(QBLK)


def _head_slopes(p, d):
    out = []
    for hq in range(2):
        v = [float(d) * 2.0 ** -(2 * q + hq + 1) for q in range(4)]
        out.append(jnp.where(p == 0, v[0], jnp.where(p == 1, v[1], jnp.where(p == 2, v[2], v[3]))))
    return out


def _rows(start, d):
    return pl.ds(start, QBLK, stride=d) if d > 1 else pl.ds(start, QBLK)


def _pair_spec(rows, part, blk):
    return pl.BlockSpec((None, rows, PAIR_W), lambda p, n: (2 * part + p // 2, blk(n), p % 2))


def _for_query_blocks(d, groups, several):
    blocks = [(g, r) for g in range(groups) for r in range(d)]
    for s in range(0, len(blocks), ATTN_UNROLL):
        several(blocks[s:s + ATTN_UNROLL])


def attn_fwd(proj, d, name, comm=None):
    T = proj.shape[1]
    sb, groups = QBLK * d, ATTN_GROUP[d]
    rb = sb * groups
    nblk = T // rb

    def body(q_ref, kc_ref, kp_ref, vc_ref, vp_ref, o_ref, l_ref):
        p, n = pl.program_id(0), pl.program_id(1)
        cur_ok, prev_ok, dcur, dprev = _attn_masks()
        first_ok = jnp.logical_and(prev_ok, n > 0)
        lane_head = lax.broadcasted_iota(jnp.int32, (QBLK, PAIR_W), 1) // HEAD_DIM
        slopes = _head_slopes(p, d)

        def several(blocks):
            work = []
            for g, r in blocks:
                rows = _rows(g * sb + r, d)
                q = q_ref[rows, :]
                kc, vc = kc_ref[rows, :].astype(BF16), vc_ref[rows, :].astype(BF16)
                if g == 0:
                    prow, pok = _rows(r, d), first_ok
                    kp, vp = kp_ref[prow, :].astype(BF16), vp_ref[prow, :].astype(BF16)
                else:
                    prow, pok = _rows((g - 1) * sb + r, d), prev_ok
                    kp, vp = kc_ref[prow, :].astype(BF16), vc_ref[prow, :].astype(BF16)
                for hq in range(2):
                    qm = jnp.where(lane_head == hq, q, 0.0).astype(BF16)
                    work.append([rows, hq, pok, vc, vp, _dot_nt(qm, kc), _dot_nt(qm, kp)])
            for w in work:
                _, hq, pok, _, _, sc, sp = w
                sc = jnp.where(cur_ok, sc * 0.125 - slopes[hq] * dcur, NEG)
                sp = jnp.where(pok, sp * 0.125 - slopes[hq] * dprev, NEG)
                m = jnp.maximum(jnp.max(sc, axis=1, keepdims=True), jnp.max(sp, axis=1, keepdims=True))
                pc = jnp.exp(sc - m)
                pp = jnp.exp(sp - m)
                den = jnp.sum(pc, axis=1, keepdims=True) + jnp.sum(pp, axis=1, keepdims=True)
                w[5:] = [pc.astype(BF16), pp.astype(BF16), 1.0 / den, m + jnp.log(den)]
            for i in range(0, len(work), 2):
                o_acc = jnp.zeros((QBLK, PAIR_W), F32)
                l_acc = jnp.zeros((QBLK, PAIR_W), F32)
                for rows, hq, _, vc, vp, pc, pp, inv, lse in work[i:i + 2]:
                    hm = lane_head == hq
                    o_acc = jnp.where(hm, (_dot(pc, vc) + _dot(pp, vp)) * inv, o_acc)
                    l_acc = jnp.where(hm, lse, l_acc)
                o_ref[rows, :] = o_acc
                l_ref[rows, :] = l_acc

        _for_query_blocks(d, groups, several)

    cur = lambda part: _pair_spec(rb, part, lambda n: n)
    prv = lambda part: _pair_spec(sb, part, lambda n: jnp.maximum(n * groups - 1, 0))
    return _call(
        body, grid=(4, nblk), in_specs=[cur(0), cur(1), prv(1), cur(2), prv(2)], out_specs=[cur(0), cur(0)],
        out_shape=[SDS((2, T, 2 * PAIR_W), F32), SDS((2, T, 2 * PAIR_W), F32)],
        args=(proj, proj, proj, proj, proj), name=name, comm=comm)


def attn_merge(os_, ls_, o_ssm, name):
    _, T, HW = os_[0].shape
    tm = 512

    def body(o1, o2, o3, l1, l2, l3, s_ref, cat_ref, l_ref):
        a, b, c = l1[...], l2[...], l3[...]
        m = jnp.maximum(jnp.maximum(a, b), c)
        ea, eb, ec = jnp.exp(a - m), jnp.exp(b - m), jnp.exp(c - m)
        s = ea + eb + ec
        cat_ref[pl.ds(0, 2)] = (ea * o1[...] + eb * o2[...] + ec * o3[...]) * (1.0 / s)
        cat_ref[pl.ds(2, 2)] = s_ref[...]
        l_ref[...] = m + jnp.log(s)

    spec = pl.BlockSpec((2, tm, HW), lambda i: (0, i, 0))
    return pl.pallas_call(
        body, grid=(T // tm,), in_specs=[spec] * 7,
        out_specs=[pl.BlockSpec((4, tm, HW), lambda i: (0, i, 0)), spec],
        out_shape=[SDS((4, T, HW), F32), SDS((2, T, HW), F32)],
        compiler_params=_params(1), name=name)(*os_, *ls_, o_ssm)


def attn_bwd(proj, dcat, o, lse, acc, d, name, du=None):
    T = proj.shape[1]
    sb, groups = QBLK * d, ATTN_GROUP[d]
    rb = sb * groups
    nblk = T // rb
    has_acc = acc is not None
    n_parts = 3 if du is None else 4

    def body(*refs):
        (qc_ref, qn_ref, kc_ref, kp_ref, vc_ref, vp_ref, dc_ref, dn_ref, oc_ref, on_ref, lc_ref, ln_ref) = refs[:12]
        acc_ref = refs[12] if has_acc else None
        out_ref = refs[-1]
        if du is not None:
            out_ref[3] = refs[-2][...]
        p, n = pl.program_id(0), pl.program_id(1)
        cur_ok, prev_ok, dcur, dprev = _attn_masks()
        first_ok = jnp.logical_and(prev_ok, n > 0)
        last_ok = jnp.logical_and(prev_ok, n < nblk - 1)
        lane_head = lax.broadcasted_iota(jnp.int32, (QBLK, PAIR_W), 1) // HEAD_DIM
        slopes = _head_slopes(p, d)

        def one(g, r):
            rows = _rows(g * sb + r, d)
            q_c, do_c, o_c, l_c = qc_ref[rows, :], dc_ref[rows, :], oc_ref[rows, :], lc_ref[rows, :]
            k_c, v_c = kc_ref[rows, :].astype(BF16), vc_ref[rows, :].astype(BF16)
            if g == 0:
                prow, pok_c = _rows(r, d), first_ok
                k_p, v_p = kp_ref[prow, :].astype(BF16), vp_ref[prow, :].astype(BF16)
            else:
                prow, pok_c = _rows((g - 1) * sb + r, d), prev_ok
                k_p, v_p = kc_ref[prow, :].astype(BF16), vc_ref[prow, :].astype(BF16)
            if g == groups - 1:
                nrow, pok_n = _rows(r, d), last_ok
                q_n, do_n, o_n, l_n = qn_ref[nrow, :], dn_ref[nrow, :], on_ref[nrow, :], ln_ref[nrow, :]
            else:
                nrow, pok_n = _rows((g + 1) * sb + r, d), prev_ok
                q_n, do_n, o_n, l_n = qc_ref[nrow, :], dc_ref[nrow, :], oc_ref[nrow, :], lc_ref[nrow, :]
            heads = []
            for hq in range(2):
                hm = lane_head == hq
                qm_c = jnp.where(hm, q_c, 0.0).astype(BF16)
                qm_n = jnp.where(hm, q_n, 0.0).astype(BF16)
                dom_c = jnp.where(hm, do_c, 0.0)
                dom_n = jnp.where(hm, do_n, 0.0)
                dd_c = jnp.sum(dom_c * o_c, axis=1, keepdims=True)
                dd_n = jnp.sum(dom_n * o_n, axis=1, keepdims=True)
                ls_c = jnp.max(jnp.where(hm, l_c, NEG), axis=1, keepdims=True)
                ls_n = jnp.max(jnp.where(hm, l_n, NEG), axis=1, keepdims=True)
                dob_c, dob_n = dom_c.astype(BF16), dom_n.astype(BF16)
                mm = [(_dot_nt(qm_c, k_c), _dot_nt(dob_c, v_c)), (_dot_nt(qm_c, k_p), _dot_nt(dob_c, v_p)),
                      (_dot_nt(qm_n, k_c), _dot_nt(dob_n, v_c))]
                heads.append(dict(hq=hq, qm_c=qm_c, qm_n=qm_n, dob_c=dob_c, dob_n=dob_n, mm=mm,
                                  dd=(dd_c, dd_c, dd_n), ls=(ls_c, ls_c, ls_n)))
            return dict(rows=rows, k_c=k_c, k_p=k_p, heads=heads, oks=(cur_ok, pok_c, pok_n))

        def several(blocks):
            work = [one(g, r) for g, r in blocks]
            for w in work:
                for h in w["heads"]:
                    slope, dist = slopes[h["hq"]], (dcur, dprev, dprev)
                    h["pr"], h["ds"] = [], []
                    for j in range(3):
                        s = jnp.where(w["oks"][j], h["mm"][j][0] * 0.125 - slope * dist[j], NEG)
                        pr = jnp.exp(s - h["ls"][j])
                        h["pr"].append(pr.astype(BF16))
                        h["ds"].append((pr * (h["mm"][j][1] - h["dd"][j])).astype(BF16))
            for w in work:
                dq = jnp.zeros((QBLK, PAIR_W), F32)
                dk = jnp.zeros((QBLK, PAIR_W), F32)
                dv = jnp.zeros((QBLK, PAIR_W), F32)
                for h in w["heads"]:
                    ds, pr = h["ds"], h["pr"]
                    dq_h = _dot(ds[0], w["k_c"]) + _dot(ds[1], w["k_p"])
                    dk += (_dot_tn(ds[0], h["qm_c"]) + _dot_tn(ds[2], h["qm_n"])) * 0.125
                    dv += _dot_tn(pr[0], h["dob_c"]) + _dot_tn(pr[2], h["dob_n"])
                    dq = jnp.where(lane_head == h["hq"], dq_h * 0.125, dq)
                for part, val in enumerate((dq, dk, dv)):
                    if has_acc:
                        val = val + acc_ref.at[part][w["rows"], :]
                    out_ref.at[part][w["rows"], :] = val

        _for_query_blocks(d, groups, several)

    cur = lambda part: _pair_spec(rb, part, lambda n: n)
    prv = lambda part: _pair_spec(sb, part, lambda n: jnp.maximum(n * groups - 1, 0))
    nxt = lambda part: _pair_spec(sb, part, lambda n: jnp.minimum((n + 1) * groups, T // sb - 1))
    full = pl.BlockSpec((3, None, rb, PAIR_W), lambda p, n: (0, p // 2, n, p % 2))
    in_specs = [cur(0), nxt(0), cur(1), prv(1), cur(2), prv(2), cur(0), nxt(0), cur(0), nxt(0), cur(0), nxt(0)]
    args = [proj, proj, proj, proj, proj, proj, dcat, dcat, o, o, lse, lse]
    if has_acc:
        in_specs.append(full)
        args.append(acc)
    if du is not None:
        in_specs.append(cur(0))
        args.append(du)
    out_spec = pl.BlockSpec((n_parts, None, rb, PAIR_W), lambda p, n: (0, p // 2, n, p % 2))
    return pl.pallas_call(
        body, grid=(4, nblk), in_specs=in_specs, out_specs=out_spec,
        out_shape=SDS((n_parts, 2, T, 2 * PAIR_W), F32), compiler_params=_params(2), name=name)(*args)


def _scan_rows(buf, tab_ref, reverse):
    n_tiles = (buf.shape[0] - 8) // 8
    per_half = HALF_STATES // SCAN_CW
    row = lax.broadcasted_iota(jnp.int32, (8, SCAN_CW), 0)
    sgn = -1.0 if reverse else 1.0

    def chunk(j, _):
        c0 = pl.multiple_of((j // per_half) * 2 * HALF_STATES + (j % per_half) * SCAN_CW, 128)
        cre = pl.ds(c0, SCAN_CW)
        cim = pl.ds(pl.multiple_of(c0 + HALF_STATES, 128), SCAN_CW)
        steps = []
        for s, k in enumerate((1, 2, 4)):
            ok, shift = (row < 8 - k, 8 - k) if reverse else (row >= k, k)
            steps.append((shift, jnp.where(ok, tab_ref[pl.ds(s, 1), cre], 0.0),
                          jnp.where(ok, sgn * tab_ref[pl.ds(s, 1), cim], 0.0)))
        trow = 16 if reverse else 8
        pr, pi = tab_ref[pl.ds(trow, 8), cre], tab_ref[pl.ds(trow, 8), cim]
        for t in range(n_tiles):
            base = 8 * (n_tiles - 1 - t) if reverse else 8 + 8 * t
            rows = pl.ds(base, 8)
            re, im = buf[rows, cre], buf[rows, cim]
            for shift, ar, ai in steps:
                sre, sim = pltpu.roll(re, shift, 0), pltpu.roll(im, shift, 0)
                re, im = re + ar * sre - ai * sim, im + ar * sim + ai * sre
            crow = pl.ds(base + 8 if reverse else base - 1, 1)
            cr, ci = buf[crow, cre], buf[crow, cim]
            buf[rows, cre] = re + pr * cr - pi * ci
            buf[rows, cim] = im + pr * ci + pi * cr
        return 0

    lax.fori_loop(0, 2 * per_half, chunk, 0)


def ssm_fwd(proj, bh, ch, apow, dskip, name, comm=None):
    _, T, C = proj.shape
    tm = SCAN_TM
    SW = 4 * HALF_STATES

    def body(u_ref, bh_ref, ch_ref, tab_ref, dsk_ref, y_ref, s_ref, buf):
        @pl.when(pl.program_id(0) == 0)
        def _():
            buf[pl.ds(0, 8), :] = jnp.zeros((8, SW), F32)

        for h in range(2):
            buf[pl.ds(8, tm), pl.ds(h * 2 * HALF_STATES, 2 * HALF_STATES)] = _dot(u_ref[h].astype(BF16), bh_ref[h])
        _scan_rows(buf, tab_ref, False)
        s_ref[...] = buf[pl.ds(8, tm), :]
        buf[pl.ds(0, 8), :] = buf[pl.ds(tm, 8), :]
        for h in range(2):
            sv = s_ref[:, pl.ds(h * 2 * HALF_STATES, 2 * HALF_STATES)].astype(BF16)
            y_ref[h] = _dot(sv, ch_ref[h]) + dsk_ref[h] * u_ref[h]

    return _call(
        body, grid=(T // tm,),
        in_specs=[pl.BlockSpec((2, tm, C), lambda i: (3, i, 0)),
                  pl.BlockSpec((2, C, 2 * HALF_STATES), lambda i: (0, 0, 0)),
                  pl.BlockSpec((2, 2 * HALF_STATES, C), lambda i: (0, 0, 0)),
                  pl.BlockSpec((24, SW), lambda i: (0, 0)),
                  pl.BlockSpec((2, 1, C), lambda i: (0, 0, 0))],
        out_specs=[pl.BlockSpec((2, tm, C), lambda i: (0, i, 0)), pl.BlockSpec((tm, SW), lambda i: (i, 0))],
        out_shape=[SDS((2, T, C), F32), SDS((T, SW), F32)],
        scratch_shapes=[pltpu.VMEM((tm + 8, SW), F32)],
        args=(proj, bh, ch, apow, dskip), name=name, comm=comm)


def ssm_bwd(dy, proj, st, bh, ch, apow, dskip, name, comm=None):
    _, T, C = proj.shape
    tm = SCAN_TM
    nt = T // tm
    SW = 4 * HALF_STATES
    HS2 = 2 * HALF_STATES

    def body(dy_ref, u_ref, s_ref, sp_ref, bh_ref, ch_ref, tab_ref, dsk_ref,
             du_ref, da_ref, dbh_ref, dch_ref, dd_ref, lam):
        i = pl.program_id(0)

        @pl.when(i == 0)
        def _():
            lam[pl.ds(tm, 8), :] = jnp.zeros((8, SW), F32)
            da_ref[...] = jnp.zeros_like(da_ref)
            dbh_ref[...] = jnp.zeros_like(dbh_ref)
            dch_ref[...] = jnp.zeros_like(dch_ref)
            dd_ref[...] = jnp.zeros_like(dd_ref)

        for h in range(2):
            lam[pl.ds(0, tm), pl.ds(h * HS2, HS2)] = _dot_nt(dy_ref[h].astype(BF16), ch_ref[h])
        _scan_rows(lam, tab_ref, True)

        first = i == nt - 1
        per_half = HALF_STATES // SCAN_CW

        def chunk(j, _):
            c0 = pl.multiple_of((j // per_half) * HS2 + (j % per_half) * SCAN_CW, 128)
            cre, cim = pl.ds(c0, SCAN_CW), pl.ds(pl.multiple_of(c0 + HALF_STATES, 128), SCAN_CW)
            row = lax.broadcasted_iota(jnp.int32, (tm, SCAN_CW), 0)
            pre = jnp.where(first, 0.0, sp_ref[pl.ds(7, 1), cre])
            pim = jnp.where(first, 0.0, sp_ref[pl.ds(7, 1), cim])
            spr = jnp.where(row == 0, pre, pltpu.roll(s_ref[:, cre], 1, 0))
            spi = jnp.where(row == 0, pim, pltpu.roll(s_ref[:, cim], 1, 0))
            lr, li = lam[pl.ds(0, tm), cre], lam[pl.ds(0, tm), cim]
            da_ref[:, cre] += jnp.sum(lr * spr + li * spi, axis=0, keepdims=True)
            da_ref[:, cim] += jnp.sum(li * spr - lr * spi, axis=0, keepdims=True)
            return 0

        lax.fori_loop(0, 2 * per_half, chunk, 0)
        lam[pl.ds(tm, 8), :] = lam[pl.ds(0, 8), :]

        for h in range(2):
            lb = lam[pl.ds(0, tm), pl.ds(h * HS2, HS2)].astype(BF16)
            dyv, uv = dy_ref[h], u_ref[h]
            du_ref[h] = _dot_nt(lb, bh_ref[h]) + dsk_ref[h] * dyv
            dbh_ref[h] += _dot_tn(uv.astype(BF16), lb)
            dch_ref[h] += _dot_tn(s_ref[:, pl.ds(h * HS2, HS2)].astype(BF16), dyv.astype(BF16))
            dd_ref[h] += jnp.sum(dyv * uv, axis=0, keepdims=True)

    rev = lambda i: nt - 1 - i
    return _call(
        body, grid=(nt,),
        in_specs=[pl.BlockSpec((2, tm, C), lambda i: (0, rev(i), 0)),
                  pl.BlockSpec((2, tm, C), lambda i: (3, rev(i), 0)),
                  pl.BlockSpec((tm, SW), lambda i: (rev(i), 0)),
                  pl.BlockSpec((8, SW), lambda i: (jnp.maximum(rev(i) * (tm // 8) - 1, 0), 0)),
                  pl.BlockSpec((2, C, HS2), lambda i: (0, 0, 0)),
                  pl.BlockSpec((2, HS2, C), lambda i: (0, 0, 0)),
                  pl.BlockSpec((24, SW), lambda i: (0, 0)),
                  pl.BlockSpec((2, 1, C), lambda i: (0, 0, 0))],
        out_specs=[pl.BlockSpec((2, tm, C), lambda i: (0, rev(i), 0)),
                   pl.BlockSpec((1, SW), lambda i: (0, 0)),
                   pl.BlockSpec((2, C, HS2), lambda i: (0, 0, 0)),
                   pl.BlockSpec((2, HS2, C), lambda i: (0, 0, 0)),
                   pl.BlockSpec((2, 1, C), lambda i: (0, 0, 0))],
        out_shape=[SDS((2, T, C), F32), SDS((1, SW), F32), SDS((2, C, HS2), F32), SDS((2, HS2, C), F32),
                   SDS((2, 1, C), F32)],
        scratch_shapes=[pltpu.VMEM((tm + 8, SW), F32)],
        args=(dy, proj, st, st, bh, ch, apow, dskip), name=name, comm=comm)


_GELU_C = math.sqrt(2.0 / math.pi)


def _gelu(x):
    t = jnp.tanh(_GELU_C * (x + 0.044715 * x * x * x))
    return 0.5 * x * (1.0 + t), t


def glu_fwd(y, w, b, name):
    _, T, C = y.shape
    tm = 512

    def body(y_ref, w_ref, b_ref, o_ref, lg_ref):
        y0, _ = _gelu(y_ref[0])
        y1, _ = _gelu(y_ref[1])
        lg = _dot(y0.astype(BF16), w_ref[0]) + _dot(y1.astype(BF16), w_ref[1]) + b_ref[...]
        sg = _sigmoid(lg)
        o_ref[0] = y0 * sg[:, :C]
        o_ref[1] = y1 * sg[:, C:]
        lg_ref[0] = lg[:, :C]
        lg_ref[1] = lg[:, C:]

    return pl.pallas_call(
        body, grid=(T // tm,),
        in_specs=[pl.BlockSpec((2, tm, C), lambda i: (0, i, 0)), pl.BlockSpec((2, C, 2 * C), lambda i: (0, 0, 0)),
                  pl.BlockSpec((1, 2 * C), lambda i: (0, 0))],
        out_specs=[pl.BlockSpec((2, tm, C), lambda i: (0, i, 0)), pl.BlockSpec((2, tm, C), lambda i: (0, i, 0))],
        out_shape=[SDS((2, T, C), F32), SDS((2, T, C), F32)], compiler_params=_params(1), name=name)(y, w, b)


def glu_bwd(dcat, y, lg, w, name):
    _, T, C = y.shape
    tm = 512

    def body(d_ref, y_ref, lg_ref, w_ref, dy_ref, dw_ref, db_ref):
        @pl.when(pl.program_id(0) == 0)
        def _():
            dw_ref[...] = jnp.zeros_like(dw_ref)
            db_ref[...] = jnp.zeros_like(db_ref)

        y2, th, sg, dlg = [], [], [], []
        for h in range(2):
            yy, tt = _gelu(y_ref[h])
            ss = _sigmoid(lg_ref[h])
            y2.append(yy)
            th.append(tt)
            sg.append(ss)
            dlg.append(d_ref[h] * yy * ss * (1.0 - ss))
        dl = jnp.concatenate(dlg, axis=1)
        dlb = dl.astype(BF16)
        db_ref[...] += jnp.sum(dl, axis=0, keepdims=True)
        for h in range(2):
            dy2 = d_ref[h] * sg[h] + _dot_nt(dlb, w_ref[h])
            yv = y_ref[h]
            dgelu = 0.5 * (1.0 + th[h]) + 0.5 * yv * (1.0 - th[h] * th[h]) * _GELU_C * (1.0 + 3 * 0.044715 * yv * yv)
            dy_ref[h] = dy2 * dgelu
            dw_ref[h] += _dot_tn(y2[h].astype(BF16), dlb)

    return pl.pallas_call(
        body, grid=(T // tm,),
        in_specs=[pl.BlockSpec((2, tm, C), lambda i: (1, i, 0)), pl.BlockSpec((2, tm, C), lambda i: (0, i, 0)),
                  pl.BlockSpec((2, tm, C), lambda i: (0, i, 0)), pl.BlockSpec((2, C, 2 * C), lambda i: (0, 0, 0))],
        out_specs=[pl.BlockSpec((2, tm, C), lambda i: (0, i, 0)), pl.BlockSpec((2, C, 2 * C), lambda i: (0, 0, 0)),
                   pl.BlockSpec((1, 2 * C), lambda i: (0, 0))],
        out_shape=[SDS((2, T, C), F32), SDS((2, C, 2 * C), F32), SDS((1, 2 * C), F32)],
        compiler_params=_params(1), name=name)(dcat, y, lg, w)


def adamw(w, m, v, slots, name):
    R, C = w.shape
    tr = R
    for cand in (512, 256, 128, 64, 32, 16, 8):
        if R % cand == 0 and cand * C * 4 <= 2 * 1024 * 1024:
            tr = cand
            break
    c1 = 1.0 / (1.0 - ADAM_B1 ** ADAM_STEP)
    c2 = 1.0 / (1.0 - ADAM_B2 ** ADAM_STEP)

    def body(w_ref, m_ref, v_ref, s_ref, g_ref, d_ref, nm_ref, nv_ref):
        g = s_ref[0].astype(F32)
        for j in range(1, N_DEV):
            g = g + s_ref[j].astype(F32)
        nm = ADAM_B1 * m_ref[...] + (1.0 - ADAM_B1) * g
        nv = ADAM_B2 * v_ref[...] + (1.0 - ADAM_B2) * (g * g)
        g_ref[...] = g
        nm_ref[...] = nm
        nv_ref[...] = nv
        d_ref[...] = -ADAM_LR * ((nm * c1) / (jnp.sqrt(nv * c2) + ADAM_EPS) + ADAM_WD * w_ref[...])

    spec = pl.BlockSpec((tr, C), lambda i: (i, 0))
    return pl.pallas_call(
        body, grid=(R // tr,),
        in_specs=[spec, spec, spec, pl.BlockSpec((N_DEV, tr, C), lambda i: (0, i, 0))],
        out_specs=[spec] * 4, out_shape=[SDS((R, C), F32)] * 4, compiler_params=_params(1), name=name)(w, m, v, slots)


def _discretise(a_re, a_im, log_dt, b_re, b_im):
    dt = jnp.exp(log_dt)[:, None]
    e = jnp.exp(dt * a_re)
    ar, ai = e * jnp.cos(dt * a_im), e * jnp.sin(dt * a_im)
    den = a_re * a_re + a_im * a_im
    nr, ni = ar - 1.0, ai
    wr = (nr * a_re + ni * a_im) / den
    wi = (ni * a_re - nr * a_im) / den
    bbr = wr[..., None] * b_re - wi[..., None] * b_im
    bbi = wr[..., None] * b_im + wi[..., None] * b_re
    return ar, ai, bbr, bbi


def _block_diag(t):
    eye = jnp.eye(16, dtype=t.dtype).reshape(1, 16, 1, 16, 1)
    r, c = t.shape[1], t.shape[2]
    return (t.reshape(2, 16, r, 1, c) * eye).reshape(2, 16 * r, 16 * c)


def _diag_blocks(m, r, c):
    eye = jnp.eye(16, dtype=m.dtype).reshape(1, 16, 1, 16, 1)
    return jnp.sum(m.reshape(2, 16, r, 16, c) * eye, axis=3).reshape(32, r, c)


def _state_vec(re, im):
    return jnp.stack([re.reshape(2, HALF_STATES), im.reshape(2, HALF_STATES)], axis=1).reshape(-1)


BIG = ("ffn1_w_in", "ffn1_w_out", "w_mix_in", "w_glu", "w_mix_out", "ffn2_w_in", "ffn2_w_out")
WEIGHTS = ("ffn1_pre_g", "ffn1_w_in", "ffn1_w_out", "ffn1_post_g", "mix_pre_g", "w_mix_in", "a_re", "a_im", "log_dt",
           "b_re", "b_im", "c_re", "c_im", "d_skip", "w_glu", "b_glu", "w_mix_out", "mix_post_g", "ffn2_pre_g",
           "ffn2_w_in", "ffn2_w_out", "ffn2_post_g")
SMALL = tuple(n for n in WEIGHTS if n not in BIG)
PACK_COLS = 1024


def _pack(parts):
    flat = jnp.concatenate([p.reshape(-1) for p in parts])
    rows = -(-flat.shape[0] // (8 * PACK_COLS)) * 8
    return jnp.pad(flat, (0, rows * PACK_COLS - flat.shape[0])).reshape(rows, PACK_COLS)


def _unpack(packed, shapes):
    flat, out, off = packed.reshape(-1), [], 0
    for s in shapes:
        n = math.prod(s)
        out.append(flat[off:off + n].reshape(s))
        off += n
    return out


def _gather(names, wb):
    return [wb[n] for n in names], [False] * len(names)


def _ffn_bwd(dy, do, saved, x, pre_g, w_in, w_out4, tag, post=None):
    h, z, a = saved
    T = x.shape[0]
    dz = ffn_dact(do, w_out4, z, f"{tag}_dact")
    dz8 = dz.reshape(8, T, dz.shape[-1])
    dw_out, _ = mm_tn(a, do, True, False, 4, f"{tag}_dwout")
    dw_in, (s_out,) = mm_tn(h, dz8, False, True, 8, f"{tag}_dwin", comm=([dw_out.reshape(8, -1, D_MODEL)], [True]))
    outs, (s_in,) = dh_pre_bwd(dz8, w_in, x, pre_g, dy, f"{tag}_dh", comm=([dw_in], [True]), post=post)
    return outs, (s_in, s_out)


def local_step(x, tgt, sp, wb):
    T = x.shape[0]
    ar, ai, bbr, bbi = _discretise(sp["a_re"], sp["a_im"], sp["log_dt"], sp["b_re"], sp["b_im"])
    powers = [(ar, ai)]
    for _ in range(7):
        pr, pi = powers[-1]
        powers.append((pr * ar - pi * ai, pr * ai + pi * ar))
    zero = jnp.zeros_like(ar)
    rows = [_state_vec(*powers[k - 1]) for k in (1, 2, 4)] + [_state_vec(zero, zero)] * 5
    rows += [_state_vec(pr, pi) for pr, pi in powers]
    rows += [_state_vec(pr, -pi) for pr, pi in reversed(powers)]
    apow = jnp.stack(rows)
    bh = jnp.concatenate([_block_diag(bbr.transpose(0, 2, 1)), _block_diag(bbi.transpose(0, 2, 1))], axis=2)
    ch = jnp.concatenate([_block_diag(sp["c_re"].transpose(0, 2, 1)), _block_diag(-sp["c_im"].transpose(0, 2, 1))], axis=1)
    bh, ch = bh.astype(BF16), ch.astype(BF16)
    dskip = sp["d_skip"].reshape(2, 1, 256)

    w1_in = gather_two_level(wb["ffn1_w_in"], "gather_w1in")
    (h1, z1, a1), (w1_out, w_mi) = ffn_in(
        x, sp["ffn1_pre_g"], w1_in, "ffn1_in", comm=_gather(["ffn1_w_out", "w_mix_in"], wb))
    w1_out4 = w1_out.reshape(4, -1, D_MODEL)
    (o1, x1), (w_glu, w_mo) = mm_acc_norm(
        a1, w1_out4, x, sp["ffn1_post_g"], 0.5, "ffn1_out", comm=_gather(["w_glu", "w_mix_out"], wb))
    w_glu2, w_mo4 = w_glu.reshape(2, 256, 512), w_mo.reshape(4, 256, D_MODEL)
    h2, proj = norm_proj(x1, sp["mix_pre_g"], w_mi, "mix_proj")
    (y_ssm, states), (w2_in,) = ssm_fwd(proj, bh, ch, apow, dskip, "ssm_fwd", comm=_gather(["ffn2_w_in"], wb))
    os_, ls_ = [], []
    for d in DILATIONS:
        (o_d, l_d), got = attn_fwd(proj, d, f"attn_fwd_d{d}",
                                   comm=_gather(["ffn2_w_out"], wb) if d == DILATIONS[-1] else None)
        os_.append(o_d)
        ls_.append(l_d)
    w2_out4 = got[0].reshape(4, -1, D_MODEL)
    o_ssm, lg = glu_fwd(y_ssm, w_glu2, sp["b_glu"], "glu_fwd")
    cat, lse = attn_merge(os_, ls_, o_ssm, "attn_merge")
    (mixed, x2), _ = mm_acc_norm(cat, w_mo4, x1, sp["mix_post_g"], 1.0, "mix_out")
    (h3, z3, a3), _ = ffn_in(x2, sp["ffn2_pre_g"], w2_in, "ffn2_in")
    (dy3, sq, do3, dg_f2post), _ = mm_acc_norm(a3, w2_out4, x2, sp["ffn2_post_g"], 0.5, "ffn2_out", tgt=tgt)

    (dx2, dg_f2pre, dmixed, dg_mpost), (s_w2in, s_w2out) = _ffn_bwd(
        dy3, do3, (h3, z3, a3), x2, sp["ffn2_pre_g"], w2_in, w2_out4, "ffn2", post=(mixed, sp["mix_post_g"], 1.0))
    dcat = mm_nt_b(dmixed, w_mo4, "mix_dcat")
    dw_mo, _ = mm_tn(cat, dmixed, True, False, 4, "mix_dwout")
    dy_ssm, dw_glu, db_glu = glu_bwd(dcat, y_ssm, lg, w_glu2, "glu_bwd")
    (du, da, dbh, dch, dd), (s_wmo, s_wglu) = ssm_bwd(
        dy_ssm, proj, states, bh, ch, apow, dskip, "ssm_bwd",
        comm=([dw_mo.reshape(8, 128, D_MODEL), dw_glu.astype(BF16).reshape(8, 64, 512)], [True, True]))
    dqkv = None
    for d in DILATIONS:
        dqkv = attn_bwd(proj, dcat, cat, lse, dqkv, d, f"attn_bwd_d{d}", du=du if d == DILATIONS[-1] else None)
    dproj = dqkv.reshape(8, T, 256)
    dw_mi, _ = mm_tn(h2, dproj, False, True, 8, "mix_dwin")
    (dx1, dg_mpre, do1, dg_f1post), (s_wmi,) = dh_pre_bwd(
        dproj, w_mi, x1, sp["mix_pre_g"], dx2, "mix_dh", comm=([dw_mi], [True]), post=(o1, sp["ffn1_post_g"], 0.5))
    (dx0, dg_f1pre), (s_w1in, s_w1out) = _ffn_bwd(
        dx1, do1, (h1, z1, a1), x, sp["ffn1_pre_g"], w1_in, w1_out4, "ffn1")

    da4 = da.reshape(2, 2, HALF_STATES)
    d_ar, d_ai = da4[:, 0].reshape(32, N_STATE), da4[:, 1].reshape(32, N_STATE)
    d_bbr = _diag_blocks(dbh[:, :, :HALF_STATES], 16, N_STATE).transpose(0, 2, 1)
    d_bbi = _diag_blocks(dbh[:, :, HALF_STATES:], 16, N_STATE).transpose(0, 2, 1)
    _, disc_vjp = jax.vjp(_discretise, sp["a_re"], sp["a_im"], sp["log_dt"], sp["b_re"], sp["b_im"])
    g_are, g_aim, g_ldt, g_bre, g_bim = disc_vjp((d_ar, d_ai, d_bbr, d_bbi))
    g_cre = _diag_blocks(dch[:, :HALF_STATES], N_STATE, 16).transpose(0, 2, 1)
    g_cim = -_diag_blocks(dch[:, HALF_STATES:], N_STATE, 16).transpose(0, 2, 1)
    small = {
        "ffn1_pre_g": dg_f1pre, "ffn1_post_g": dg_f1post, "mix_pre_g": dg_mpre, "a_re": g_are, "a_im": g_aim,
        "log_dt": g_ldt, "b_re": g_bre, "b_im": g_bim, "c_re": g_cre, "c_im": g_cim, "d_skip": dd.reshape(1, 512),
        "b_glu": db_glu, "mix_post_g": dg_mpost, "ffn2_pre_g": dg_f2pre, "ffn2_post_g": dg_f2post,
    }
    small_slots = gather_two_level(_pack([small[n] for n in SMALL]), "exchange_small")
    big_slots = {"ffn1_w_in": s_w1in, "ffn1_w_out": s_w1out, "w_mix_in": s_wmi, "w_glu": s_wglu, "w_mix_out": s_wmo,
                 "ffn2_w_in": s_w2in, "ffn2_w_out": s_w2out}
    return sq, dx0, big_slots, small_slots


def kernel(x, ffn1_pre_g, ffn1_w_in, ffn1_w_out, ffn1_post_g, mix_pre_g, w_mix_in, a_re, a_im, log_dt, b_re, b_im, c_re, c_im, d_skip, w_glu, b_glu, w_mix_out, mix_post_g, ffn2_pre_g, ffn2_w_in, ffn2_w_out, ffn2_post_g, loss_target, m_ffn1_pre_g, m_ffn1_w_in, m_ffn1_w_out, m_ffn1_post_g, m_mix_pre_g, m_w_mix_in, m_a_re, m_a_im, m_log_dt, m_b_re, m_b_im, m_c_re, m_c_im, m_d_skip, m_w_glu, m_b_glu, m_w_mix_out, m_mix_post_g, m_ffn2_pre_g, m_ffn2_w_in, m_ffn2_w_out, m_ffn2_post_g, v_ffn1_pre_g, v_ffn1_w_in, v_ffn1_w_out, v_ffn1_post_g, v_mix_pre_g, v_w_mix_in, v_a_re, v_a_im, v_log_dt, v_b_re, v_b_im, v_c_re, v_c_im, v_d_skip, v_w_glu, v_b_glu, v_w_mix_out, v_mix_post_g, v_ffn2_pre_g, v_ffn2_w_in, v_ffn2_w_out, v_ffn2_post_g):
    args = dict(locals())
    w = {n: args[n][0] for n in WEIGHTS}
    m = {n: args["m_" + n][0] for n in WEIGHTS}
    v = {n: args["v_" + n][0] for n in WEIGHTS}

    wb = {n: w[n].astype(BF16) for n in BIG}
    sp = {n: w[n] for n in SMALL}
    for n in ("ffn1_pre_g", "ffn1_post_g", "mix_pre_g", "mix_post_g", "ffn2_pre_g", "ffn2_post_g", "b_glu", "d_skip"):
        sp[n] = w[n].reshape(1, -1)

    sq, grad_x, big_slots, small_slots = local_step(x[0], loss_target[0], sp, wb)
    loss = lax.psum(0.5 / D_MODEL * jnp.sum(sq), ("x", "y", "c"))

    outs = {}
    for n in BIG:
        shp = w[n].shape
        r2 = lambda t: t.reshape(-1, shp[-1])
        res = adamw(r2(w[n]), r2(m[n]), r2(v[n]), big_slots[n].reshape(N_DEV, -1, shp[-1]), f"adamw_{n}")
        outs[n] = [t.reshape((1,) + shp) for t in res]
    res = adamw(_pack([w[n] for n in SMALL]), _pack([m[n] for n in SMALL]), _pack([v[n] for n in SMALL]),
                small_slots, "adamw_small")
    shapes = [(1,) + w[n].shape for n in SMALL]
    unpacked = [_unpack(t, shapes) for t in res]
    for j, n in enumerate(SMALL):
        outs[n] = [unpacked[k][j] for k in range(4)]

    result = [loss, grad_x[None]]
    for k in range(4):
        result += [outs[n][k] for n in WEIGHTS]
    return tuple(result)
```

```python
import functools
import math

import jax
import jax.numpy as jnp
from jax import lax
from jax.experimental import pallas as pl
from jax.experimental.pallas import tpu as pltpu

F32, BF16 = jnp.float32, jnp.bfloat16
SDS = jax.ShapeDtypeStruct

D_MODEL = 1024
N_DEV = 8
HEAD_DIM = 64
PAIR_W = 128
QBLK = 128
DILATIONS = (1, 4, 16)
N_STATE = 64
HALF_STATES = 1024
NORM_EPS = 1e-6
NEG = -1e30
VMEM_LIMIT = 56 * 1024 * 1024
ADAM_LR, ADAM_B1, ADAM_B2, ADAM_EPS, ADAM_WD, ADAM_STEP = 1e-3, 0.9, 0.999, 1e-8, 0.01, 10
SCAN_TM = 256
SCAN_CW = 512


def _params(n_grid):
    return pltpu.CompilerParams(dimension_semantics=("arbitrary",) * n_grid, vmem_limit_bytes=VMEM_LIMIT)


def _dot(a, b):
    return jnp.dot(a, b, preferred_element_type=F32)


def _dot_nt(a, b):
    return lax.dot_general(a, b, (((1,), (1,)), ((), ())), preferred_element_type=F32)


def _dot_tn(a, b):
    return lax.dot_general(a, b, (((0,), (0,)), ((), ())), preferred_element_type=F32)


def _sigmoid(v):
    return 0.5 * jnp.tanh(0.5 * v) + 0.5


def _resident(shape):
    return pl.BlockSpec(shape, lambda i: (0,) * len(shape), pipeline_mode=pl.Buffered(1))


ROW_SPLIT = 2


def _exchange_phase(ins, outs, scatter, sems, start):
    send_sems, recv_sems, loc_sems = sems
    x, y, c = lax.axis_index("x"), lax.axis_index("y"), lax.axis_index("c")
    me = 4 * x + 2 * y + c
    own_copies, sends, arrivals = [], [], []
    for i in range(len(ins)):
        own = ins[i].at[me] if scatter[i] else ins[i]
        own_copies.append(pltpu.make_async_copy(own, outs[i].at[me], loc_sems.at[i]))
        for k in range(1, N_DEV):
            px = 1 - x if k & 4 else x
            py = 1 - y if k & 2 else y
            pc = 1 - c if k & 1 else c
            peer = 4 * px + 2 * py + pc
            src = ins[i].at[peer] if scatter[i] else ins[i]
            common = dict(src_ref=src, send_sem=send_sems.at[i, k - 1], recv_sem=recv_sems.at[i, k - 1],
                          device_id=(px, py, pc), device_id_type=pl.DeviceIdType.MESH)
            sends.append(pltpu.make_async_remote_copy(dst_ref=outs[i].at[me], **common))
            if not start:
                arrivals.append(pltpu.make_async_remote_copy(dst_ref=outs[i].at[peer], **common))
    if start:
        for cp in own_copies + sends:
            cp.start()
    else:
        for cp in arrivals:
            cp.wait_recv()
        for cp in sends:
            cp.wait_send()
        for cp in own_copies:
            cp.wait()


def _comm_shapes(arrs, scatter):
    n = len(arrs)
    out_shapes = [SDS(a.shape if scatter[i] else (N_DEV,) + a.shape, a.dtype) for i, a in enumerate(arrs)]
    sems = [pltpu.SemaphoreType.DMA((n, N_DEV - 1)), pltpu.SemaphoreType.DMA((n, N_DEV - 1)),
            pltpu.SemaphoreType.DMA((n,))]
    return out_shapes, sems


def gather_two_level(arr, name):
    def body(x_ref, out_ref, send_sems, recv_sems, local_sem):
        x, y, c = lax.axis_index("x"), lax.axis_index("y"), lax.axis_index("c")
        sibling = (x, y, 1 - c)
        chips = [(1 - x, y), (x, 1 - y), (1 - x, 1 - y)]

        def slot(px, py, pc):
            return out_ref.at[4 * px + 2 * py + pc]

        def copy(k, block, to, src=None):
            return pltpu.make_async_remote_copy(
                src_ref=slot(*block) if src is None else src, dst_ref=slot(*block),
                send_sem=send_sems.at[k], recv_sem=recv_sems.at[k], device_id=to, device_id_type=pl.DeviceIdType.MESH)

        mine = pltpu.make_async_copy(x_ref, slot(x, y, c), local_sem)
        mine.start()
        first = [copy(0, (x, y, c), sibling, src=x_ref)]
        first += [copy(1 + j, (x, y, c), (*chip, c), src=x_ref) for j, chip in enumerate(chips)]
        for cp in first:
            cp.start()
        passed = [copy(4 + j, (*chip, c), sibling) for j, chip in enumerate(chips)]
        for j, chip in enumerate(chips):
            copy(1 + j, (*chip, c), (x, y, c)).wait_recv()
            passed[j].start()
        copy(0, sibling, (x, y, c)).wait_recv()
        for j, chip in enumerate(chips):
            copy(4 + j, (*chip, 1 - c), (x, y, c)).wait_recv()
        for cp in first + passed:
            cp.wait_send()
        mine.wait()

    anyspec = pl.BlockSpec(memory_space=pl.ANY)
    return pl.pallas_call(
        body, in_specs=[anyspec], out_specs=anyspec, out_shape=SDS((N_DEV,) + arr.shape, arr.dtype),
        scratch_shapes=[pltpu.SemaphoreType.DMA((N_DEV - 1,)), pltpu.SemaphoreType.DMA((N_DEV - 1,)),
                        pltpu.SemaphoreType.DMA],
        compiler_params=pltpu.CompilerParams(has_side_effects=True), name=name)(arr)


def _call(body, *, grid, in_specs, out_specs, out_shape, args, name, scratch_shapes=(), comm=None):
    n_grid, scratch_shapes = len(grid), list(scratch_shapes)
    if comm is None:
        outs = pl.pallas_call(body, grid=grid, in_specs=in_specs, out_specs=out_specs, out_shape=out_shape,
                              scratch_shapes=scratch_shapes, compiler_params=_params(n_grid), name=name)(*args)
        return outs, []
    arrs, scatter = comm
    nc, n_in, n_out, n_sc = len(arrs), len(in_specs), len(out_specs), len(scratch_shapes)
    comm_shapes, sems = _comm_shapes(arrs, scatter)

    def wrapped(*refs):
        ins, cins = refs[:n_in], refs[n_in:n_in + nc]
        o0 = n_in + nc
        outs, couts = refs[o0:o0 + n_out], refs[o0 + n_out:o0 + n_out + nc]
        s0 = o0 + n_out + nc
        scratch, sem_refs = refs[s0:s0 + n_sc], refs[s0 + n_sc:]
        first = functools.reduce(jnp.logical_and, [pl.program_id(k) == 0 for k in range(n_grid)])
        last = functools.reduce(jnp.logical_and, [pl.program_id(k) == grid[k] - 1 for k in range(n_grid)])

        @pl.when(first)
        def _():
            _exchange_phase(cins, couts, scatter, sem_refs, True)

        body(*ins, *outs, *scratch)

        @pl.when(last)
        def _():
            _exchange_phase(cins, couts, scatter, sem_refs, False)

    anyspec = pl.BlockSpec(memory_space=pl.ANY)
    res = pl.pallas_call(
        wrapped, grid=grid, in_specs=list(in_specs) + [anyspec] * nc, out_specs=list(out_specs) + [anyspec] * nc,
        out_shape=list(out_shape) + comm_shapes, scratch_shapes=scratch_shapes + sems,
        compiler_params=pltpu.CompilerParams(dimension_semantics=("arbitrary",) * n_grid,
                                             vmem_limit_bytes=VMEM_LIMIT, has_side_effects=True),
        name=name)(*args, *arrs)
    return res[:n_out], res[n_out:]


def _rms(xv, g):
    r = lax.rsqrt(jnp.mean(xv * xv, axis=-1, keepdims=True) + NORM_EPS)
    return (xv * r * g).astype(BF16)


def ffn_in(x, g, w, name, comm=None):
    T, D = x.shape
    F = w.shape[2]
    tm = 512

    def body(x_ref, g_ref, w_ref, h_ref, z_ref, a_ref):
        hv = _rms(x_ref[...], g_ref[...])
        h_ref[...] = hv
        pending = None
        for j in range(5):
            if j < 4:
                zs = (_dot(hv, w_ref[j]), _dot(hv, w_ref[j + 4]))
            if pending is not None:
                zg, zu = pending
                sg = _sigmoid(zg)
                silu = zg * sg
                z_ref[0, j - 1] = (zu * (sg + silu - silu * sg)).astype(BF16)
                z_ref[1, j - 1] = silu.astype(BF16)
                a_ref[j - 1] = (silu * zu).astype(BF16)
            pending = zs

    return _call(
        body, grid=(T // tm,),
        in_specs=[pl.BlockSpec((tm, D), lambda i: (i, 0)), pl.BlockSpec((1, D), lambda i: (0, 0)),
                  _resident((8, D, F))],
        out_specs=[pl.BlockSpec((tm, D), lambda i: (i, 0)), pl.BlockSpec((2, 4, tm, F), lambda i: (0, 0, i, 0)),
                   pl.BlockSpec((4, tm, F), lambda i: (0, i, 0))],
        out_shape=[SDS((T, D), BF16), SDS((2, 4, T, F), BF16), SDS((4, T, F), BF16)],
        args=(x, g, w), name=name, comm=comm)


def norm_proj(x, g, w, name):
    T, K = x.shape
    nb, _, N = w.shape
    tm = 512

    def body(x_ref, g_ref, w_ref, h_ref, o_ref):
        hv = _rms(x_ref[...], g_ref[...])
        h_ref[...] = hv
        for b in range(nb):
            o_ref[b] = _dot(hv, w_ref[b])

    return pl.pallas_call(
        body, grid=(T // tm,),
        in_specs=[pl.BlockSpec((tm, K), lambda i: (i, 0)), pl.BlockSpec((1, K), lambda i: (0, 0)),
                  _resident((nb, K, N))],
        out_specs=[pl.BlockSpec((tm, K), lambda i: (i, 0)), pl.BlockSpec((nb, tm, N), lambda i: (0, i, 0))],
        out_shape=[SDS((T, K), BF16), SDS((nb, T, N), F32)], compiler_params=_params(1), name=name)(x, g, w)


def mm_acc_norm(a, w, xres, g, scale, name, comm=None, tgt=None):
    nb, T, K = a.shape
    D = w.shape[2]
    tm = 512
    rc = tm // ROW_SPLIT
    with_loss = tgt is not None

    def body(a_ref, w_ref, x_ref, g_ref, *rest):
        if with_loss:
            t_ref, dy_ref, sq_ref, do_ref, dg_ref = rest

            @pl.when(pl.program_id(0) == 0)
            def _():
                sq_ref[...] = jnp.zeros_like(sq_ref)
                dg_ref[...] = jnp.zeros_like(dg_ref)
        else:
            o_ref, y_ref = rest
        accs = []
        for c in range(ROW_SPLIT):
            rows = pl.ds(c * rc, rc)
            o = _dot(a_ref[0, rows, :].astype(BF16), w_ref[0])
            for b in range(1, nb):
                o += _dot(a_ref[b, rows, :].astype(BF16), w_ref[b])
            accs.append(o)
        for c, o in enumerate(accs):
            rows = pl.ds(c * rc, rc)
            r = lax.rsqrt(jnp.mean(o * o, axis=-1, keepdims=True) + NORM_EPS)
            y = x_ref[rows, :] + scale * (o * r * g_ref[...])
            if with_loss:
                e = y - t_ref[rows, :]
                dy = e * (1.0 / D)
                dy_ref[rows, :] = dy
                sq_ref[...] += jnp.sum(e * e, axis=0, keepdims=True)
                do, dg = _post_bwd(dy, o, g_ref[...], scale)
                do_ref[rows, :] = do
                dg_ref[...] += dg
            else:
                o_ref[rows, :] = o
                y_ref[rows, :] = y

    tile = pl.BlockSpec((tm, D), lambda i: (i, 0))
    row = pl.BlockSpec((1, D), lambda i: (0, 0))
    in_specs = [pl.BlockSpec((nb, tm, K), lambda i: (0, i, 0)), _resident((nb, K, D)), tile, row]
    args = (a, w, xres, g)
    if with_loss:
        return _call(body, grid=(T // tm,), in_specs=in_specs + [tile], out_specs=[tile, row, tile, row],
                     out_shape=[SDS((T, D), F32), SDS((1, D), F32), SDS((T, D), BF16), SDS((1, D), F32)],
                     args=args + (tgt,), name=name, comm=comm)
    return _call(body, grid=(T // tm,), in_specs=in_specs, out_specs=[tile, tile],
                 out_shape=[SDS((T, D), F32), SDS((T, D), F32)], args=args, name=name, comm=comm)


def _post_bwd(dy, ov, g, scale):
    r = scale * dy
    rstd = lax.rsqrt(jnp.mean(ov * ov, axis=-1, keepdims=True) + NORM_EPS)
    oh = ov * rstd
    rg = r * g
    do = rstd * (rg - oh * jnp.mean(rg * oh, axis=-1, keepdims=True))
    return do.astype(BF16), jnp.sum(r * oh, axis=0, keepdims=True)


def mm_nt_b(gr, w, name):
    T, N = gr.shape
    nb, K, _ = w.shape
    tm = 512

    def body(g_ref, w_ref, o_ref):
        gv = g_ref[...]
        for b in range(nb):
            o_ref[b] = _dot_nt(gv, w_ref[b])

    return pl.pallas_call(
        body, grid=(T // tm,),
        in_specs=[pl.BlockSpec((tm, N), lambda i: (i, 0)), _resident((nb, K, N))],
        out_specs=pl.BlockSpec((nb, tm, K), lambda i: (0, i, 0)),
        out_shape=SDS((nb, T, K), F32), compiler_params=_params(1), name=name)(gr, w)


def ffn_dact(do, w_out, z, name):
    T, D = do.shape
    nb, F, _ = w_out.shape
    tm = 512

    rc = tm // ROW_SPLIT

    def body(g_ref, w_ref, z_ref, dz_ref):
        das = [_dot_nt(g_ref[pl.ds(c * rc, rc), :], w_ref[...]) for c in range(ROW_SPLIT)]
        for c, da in enumerate(das):
            rows = pl.ds(c * rc, rc)
            dz_ref[0, rows, :] = (da * z_ref[0, rows, :].astype(F32)).astype(BF16)
            dz_ref[1, rows, :] = (da * z_ref[1, rows, :].astype(F32)).astype(BF16)

    return pl.pallas_call(
        body, grid=(nb, T // tm),
        in_specs=[pl.BlockSpec((tm, D), lambda b, i: (i, 0)), pl.BlockSpec((None, F, D), lambda b, i: (b, 0, 0)),
                  pl.BlockSpec((2, None, tm, F), lambda b, i: (0, b, i, 0))],
        out_specs=pl.BlockSpec((2, None, tm, F), lambda b, i: (0, b, i, 0)),
        out_shape=SDS((2, nb, T, F), BF16), compiler_params=_params(2), name=name)(do, w_out, z)


def mm_tn(a, g, a_batched, g_batched, nb, name, comm=None):
    T = a.shape[-2]
    K, N = a.shape[-1], g.shape[-1]
    tk = 2048
    nk = T // tk

    def body(a_ref, g_ref, o_ref, acc):
        k = pl.program_id(1)

        @pl.when(k == 0)
        def _():
            acc[...] = jnp.zeros_like(acc)

        acc[...] += _dot_tn(a_ref[...].astype(BF16), g_ref[...].astype(BF16))

        @pl.when(k == nk - 1)
        def _():
            o_ref[...] = acc[...].astype(BF16)

    a_spec = (pl.BlockSpec((None, tk, K), lambda b, k: (b, k, 0)) if a_batched
              else pl.BlockSpec((tk, K), lambda b, k: (k, 0)))
    g_spec = (pl.BlockSpec((None, tk, N), lambda b, k: (b, k, 0)) if g_batched
              else pl.BlockSpec((tk, N), lambda b, k: (k, 0)))
    (out,), slots = _call(
        body, grid=(nb, nk), in_specs=[a_spec, g_spec],
        out_specs=[pl.BlockSpec((None, K, N), lambda b, k: (b, 0, 0))],
        out_shape=[SDS((nb, K, N), BF16)], scratch_shapes=[pltpu.VMEM((K, N), F32)],
        args=(a, g), name=name, comm=comm)
    return out, slots


def dh_pre_bwd(dz, w, x, g, dyres, name, comm=None, post=None):
    nb, T, F = dz.shape
    D = w.shape[1]
    tm = 512
    rc = tm // ROW_SPLIT

    def body(dz_ref, w_ref, x_ref, g_ref, dy_ref, *rest):
        if post is None:
            dx_ref, dg_ref = rest
        else:
            o_ref, gp_ref, dx_ref, dg_ref, do_ref, dgp_ref = rest

        @pl.when(pl.program_id(0) == 0)
        def _():
            dg_ref[...] = jnp.zeros_like(dg_ref)
            if post is not None:
                dgp_ref[...] = jnp.zeros_like(dgp_ref)

        accs = []
        for c in range(ROW_SPLIT):
            rows = pl.ds(c * rc, rc)
            dh = _dot_nt(dz_ref[0, rows, :].astype(BF16), w_ref[0])
            for b in range(1, nb):
                dh += _dot_nt(dz_ref[b, rows, :].astype(BF16), w_ref[b])
            accs.append(dh)
        for c, dh in enumerate(accs):
            rows = pl.ds(c * rc, rc)
            xv = x_ref[rows, :]
            rstd = lax.rsqrt(jnp.mean(xv * xv, axis=-1, keepdims=True) + NORM_EPS)
            xh = xv * rstd
            dg_ref[...] += jnp.sum(dh * xh, axis=0, keepdims=True)
            dhg = dh * g_ref[...]
            dx = dy_ref[rows, :] + rstd * (dhg - xh * jnp.mean(dhg * xh, axis=-1, keepdims=True))
            dx_ref[rows, :] = dx
            if post is not None:
                do, dgp = _post_bwd(dx, o_ref[rows, :], gp_ref[...], post[2])
                do_ref[rows, :] = do
                dgp_ref[...] += dgp

    tile = pl.BlockSpec((tm, D), lambda i: (i, 0))
    row = pl.BlockSpec((1, D), lambda i: (0, 0))
    in_specs = [pl.BlockSpec((nb, tm, F), lambda i: (0, i, 0)), _resident((nb, D, F)), tile, row, tile]
    out_specs, out_shape, args = [tile, row], [SDS((T, D), F32), SDS((1, D), F32)], (dz, w, x, g, dyres)
    if post is not None:
        in_specs += [tile, row]
        out_specs += [tile, row]
        out_shape += [SDS((T, D), BF16), SDS((1, D), F32)]
        args += (post[0], post[1])
    return _call(body, grid=(T // tm,), in_specs=in_specs, out_specs=out_specs, out_shape=out_shape,
                 args=args, name=name, comm=comm)


ATTN_GROUP = {1: 4, 4: 1, 16: 1}
ATTN_UNROLL = 4


def _attn_masks():
    qi = lax.broadcasted_iota(jnp.int32, (QBLK, QBLK), 0)
    kj = lax.broadcasted_iota(jnp.int32, (QBLK, QBLK), 1)
    cur_ok = kj <= qi
    prev_ok = kj >= qi
    dcur = (qi - kj).astype(F32)
    return cur_ok, prev_ok, dcur, dcur + float(QBLK)


def _head_slopes(p, d):
    out = []
    for hq in range(2):
        v = [float(d) * 2.0 ** -(2 * q + hq + 1) for q in range(4)]
        out.append(jnp.where(p == 0, v[0], jnp.where(p == 1, v[1], jnp.where(p == 2, v[2], v[3]))))
    return out


def _rows(start, d):
    return pl.ds(start, QBLK, stride=d) if d > 1 else pl.ds(start, QBLK)


def _pair_spec(rows, part, blk):
    return pl.BlockSpec((None, rows, PAIR_W), lambda p, n: (2 * part + p // 2, blk(n), p % 2))


def _for_query_blocks(d, groups, several):
    blocks = [(g, r) for g in range(groups) for r in range(d)]
    for s in range(0, len(blocks), ATTN_UNROLL):
        several(blocks[s:s + ATTN_UNROLL])


def attn_fwd(proj, d, name, comm=None):
    T = proj.shape[1]
    sb, groups = QBLK * d, ATTN_GROUP[d]
    rb = sb * groups
    nblk = T // rb

    def body(q_ref, kc_ref, kp_ref, vc_ref, vp_ref, o_ref, l_ref):
        p, n = pl.program_id(0), pl.program_id(1)
        cur_ok, prev_ok, dcur, dprev = _attn_masks()
        first_ok = jnp.logical_and(prev_ok, n > 0)
        lane_head = lax.broadcasted_iota(jnp.int32, (QBLK, PAIR_W), 1) // HEAD_DIM
        slopes = _head_slopes(p, d)

        def several(blocks):
            work = []
            for g, r in blocks:
                rows = _rows(g * sb + r, d)
                q = q_ref[rows, :]
                kc, vc = kc_ref[rows, :].astype(BF16), vc_ref[rows, :].astype(BF16)
                if g == 0:
                    prow, pok = _rows(r, d), first_ok
                    kp, vp = kp_ref[prow, :].astype(BF16), vp_ref[prow, :].astype(BF16)
                else:
                    prow, pok = _rows((g - 1) * sb + r, d), prev_ok
                    kp, vp = kc_ref[prow, :].astype(BF16), vc_ref[prow, :].astype(BF16)
                for hq in range(2):
                    qm = jnp.where(lane_head == hq, q, 0.0).astype(BF16)
                    work.append([rows, hq, pok, vc, vp, _dot_nt(qm, kc), _dot_nt(qm, kp)])
            for w in work:
                _, hq, pok, _, _, sc, sp = w
                sc = jnp.where(cur_ok, sc * 0.125 - slopes[hq] * dcur, NEG)
                sp = jnp.where(pok, sp * 0.125 - slopes[hq] * dprev, NEG)
                m = jnp.maximum(jnp.max(sc, axis=1, keepdims=True), jnp.max(sp, axis=1, keepdims=True))
                pc = jnp.exp(sc - m)
                pp = jnp.exp(sp - m)
                den = jnp.sum(pc, axis=1, keepdims=True) + jnp.sum(pp, axis=1, keepdims=True)
                w[5:] = [pc.astype(BF16), pp.astype(BF16), 1.0 / den, m + jnp.log(den)]
            for i in range(0, len(work), 2):
                o_acc = jnp.zeros((QBLK, PAIR_W), F32)
                l_acc = jnp.zeros((QBLK, PAIR_W), F32)
                for rows, hq, _, vc, vp, pc, pp, inv, lse in work[i:i + 2]:
                    hm = lane_head == hq
                    o_acc = jnp.where(hm, (_dot(pc, vc) + _dot(pp, vp)) * inv, o_acc)
                    l_acc = jnp.where(hm, lse, l_acc)
                o_ref[rows, :] = o_acc
                l_ref[rows, :] = l_acc

        _for_query_blocks(d, groups, several)

    cur = lambda part: _pair_spec(rb, part, lambda n: n)
    prv = lambda part: _pair_spec(sb, part, lambda n: jnp.maximum(n * groups - 1, 0))
    return _call(
        body, grid=(4, nblk), in_specs=[cur(0), cur(1), prv(1), cur(2), prv(2)], out_specs=[cur(0), cur(0)],
        out_shape=[SDS((2, T, 2 * PAIR_W), F32), SDS((2, T, 2 * PAIR_W), F32)],
        args=(proj, proj, proj, proj, proj), name=name, comm=comm)


def attn_merge(os_, ls_, o_ssm, name):
    _, T, HW = os_[0].shape
    tm = 512

    def body(o1, o2, o3, l1, l2, l3, s_ref, cat_ref, l_ref):
        a, b, c = l1[...], l2[...], l3[...]
        m = jnp.maximum(jnp.maximum(a, b), c)
        ea, eb, ec = jnp.exp(a - m), jnp.exp(b - m), jnp.exp(c - m)
        s = ea + eb + ec
        cat_ref[pl.ds(0, 2)] = (ea * o1[...] + eb * o2[...] + ec * o3[...]) * (1.0 / s)
        cat_ref[pl.ds(2, 2)] = s_ref[...]
        l_ref[...] = m + jnp.log(s)

    spec = pl.BlockSpec((2, tm, HW), lambda i: (0, i, 0))
    return pl.pallas_call(
        body, grid=(T // tm,), in_specs=[spec] * 7,
        out_specs=[pl.BlockSpec((4, tm, HW), lambda i: (0, i, 0)), spec],
        out_shape=[SDS((4, T, HW), F32), SDS((2, T, HW), F32)],
        compiler_params=_params(1), name=name)(*os_, *ls_, o_ssm)


def attn_bwd(proj, dcat, o, lse, acc, d, name, du=None):
    T = proj.shape[1]
    sb, groups = QBLK * d, ATTN_GROUP[d]
    rb = sb * groups
    nblk = T // rb
    has_acc = acc is not None
    n_parts = 3 if du is None else 4

    def body(*refs):
        (qc_ref, qn_ref, kc_ref, kp_ref, vc_ref, vp_ref, dc_ref, dn_ref, oc_ref, on_ref, lc_ref, ln_ref) = refs[:12]
        acc_ref = refs[12] if has_acc else None
        out_ref = refs[-1]
        if du is not None:
            out_ref[3] = refs[-2][...]
        p, n = pl.program_id(0), pl.program_id(1)
        cur_ok, prev_ok, dcur, dprev = _attn_masks()
        first_ok = jnp.logical_and(prev_ok, n > 0)
        last_ok = jnp.logical_and(prev_ok, n < nblk - 1)
        lane_head = lax.broadcasted_iota(jnp.int32, (QBLK, PAIR_W), 1) // HEAD_DIM
        slopes = _head_slopes(p, d)

        def one(g, r):
            rows = _rows(g * sb + r, d)
            q_c, do_c, o_c, l_c = qc_ref[rows, :], dc_ref[rows, :], oc_ref[rows, :], lc_ref[rows, :]
            k_c, v_c = kc_ref[rows, :].astype(BF16), vc_ref[rows, :].astype(BF16)
            if g == 0:
                prow, pok_c = _rows(r, d), first_ok
                k_p, v_p = kp_ref[prow, :].astype(BF16), vp_ref[prow, :].astype(BF16)
            else:
                prow, pok_c = _rows((g - 1) * sb + r, d), prev_ok
                k_p, v_p = kc_ref[prow, :].astype(BF16), vc_ref[prow, :].astype(BF16)
            if g == groups - 1:
                nrow, pok_n = _rows(r, d), last_ok
                q_n, do_n, o_n, l_n = qn_ref[nrow, :], dn_ref[nrow, :], on_ref[nrow, :], ln_ref[nrow, :]
            else:
                nrow, pok_n = _rows((g + 1) * sb + r, d), prev_ok
                q_n, do_n, o_n, l_n = qc_ref[nrow, :], dc_ref[nrow, :], oc_ref[nrow, :], lc_ref[nrow, :]
            heads = []
            for hq in range(2):
                hm = lane_head == hq
                qm_c = jnp.where(hm, q_c, 0.0).astype(BF16)
                qm_n = jnp.where(hm, q_n, 0.0).astype(BF16)
                dom_c = jnp.where(hm, do_c, 0.0)
                dom_n = jnp.where(hm, do_n, 0.0)
                dd_c = jnp.sum(dom_c * o_c, axis=1, keepdims=True)
                dd_n = jnp.sum(dom_n * o_n, axis=1, keepdims=True)
                ls_c = jnp.max(jnp.where(hm, l_c, NEG), axis=1, keepdims=True)
                ls_n = jnp.max(jnp.where(hm, l_n, NEG), axis=1, keepdims=True)
                dob_c, dob_n = dom_c.astype(BF16), dom_n.astype(BF16)
                mm = [(_dot_nt(qm_c, k_c), _dot_nt(dob_c, v_c)), (_dot_nt(qm_c, k_p), _dot_nt(dob_c, v_p)),
                      (_dot_nt(qm_n, k_c), _dot_nt(dob_n, v_c))]
                heads.append(dict(hq=hq, qm_c=qm_c, qm_n=qm_n, dob_c=dob_c, dob_n=dob_n, mm=mm,
                                  dd=(dd_c, dd_c, dd_n), ls=(ls_c, ls_c, ls_n)))
            return dict(rows=rows, k_c=k_c, k_p=k_p, heads=heads, oks=(cur_ok, pok_c, pok_n))

        def several(blocks):
            work = [one(g, r) for g, r in blocks]
            for w in work:
                for h in w["heads"]:
                    slope, dist = slopes[h["hq"]], (dcur, dprev, dprev)
                    h["pr"], h["ds"] = [], []
                    for j in range(3):
                        s = jnp.where(w["oks"][j], h["mm"][j][0] * 0.125 - slope * dist[j], NEG)
                        pr = jnp.exp(s - h["ls"][j])
                        h["pr"].append(pr.astype(BF16))
                        h["ds"].append((pr * (h["mm"][j][1] - h["dd"][j])).astype(BF16))
            for w in work:
                dq = jnp.zeros((QBLK, PAIR_W), F32)
                dk = jnp.zeros((QBLK, PAIR_W), F32)
                dv = jnp.zeros((QBLK, PAIR_W), F32)
                for h in w["heads"]:
                    ds, pr = h["ds"], h["pr"]
                    dq_h = _dot(ds[0], w["k_c"]) + _dot(ds[1], w["k_p"])
                    dk += (_dot_tn(ds[0], h["qm_c"]) + _dot_tn(ds[2], h["qm_n"])) * 0.125
                    dv += _dot_tn(pr[0], h["dob_c"]) + _dot_tn(pr[2], h["dob_n"])
                    dq = jnp.where(lane_head == h["hq"], dq_h * 0.125, dq)
                for part, val in enumerate((dq, dk, dv)):
                    if has_acc:
                        val = val + acc_ref.at[part][w["rows"], :]
                    out_ref.at[part][w["rows"], :] = val

        _for_query_blocks(d, groups, several)

    cur = lambda part: _pair_spec(rb, part, lambda n: n)
    prv = lambda part: _pair_spec(sb, part, lambda n: jnp.maximum(n * groups - 1, 0))
    nxt = lambda part: _pair_spec(sb, part, lambda n: jnp.minimum((n + 1) * groups, T // sb - 1))
    full = pl.BlockSpec((3, None, rb, PAIR_W), lambda p, n: (0, p // 2, n, p % 2))
    in_specs = [cur(0), nxt(0), cur(1), prv(1), cur(2), prv(2), cur(0), nxt(0), cur(0), nxt(0), cur(0), nxt(0)]
    args = [proj, proj, proj, proj, proj, proj, dcat, dcat, o, o, lse, lse]
    if has_acc:
        in_specs.append(full)
        args.append(acc)
    if du is not None:
        in_specs.append(cur(0))
        args.append(du)
    out_spec = pl.BlockSpec((n_parts, None, rb, PAIR_W), lambda p, n: (0, p // 2, n, p % 2))
    return pl.pallas_call(
        body, grid=(4, nblk), in_specs=in_specs, out_specs=out_spec,
        out_shape=SDS((n_parts, 2, T, 2 * PAIR_W), F32), compiler_params=_params(2), name=name)(*args)


def _scan_rows(buf, tab_ref, reverse, half):
    n_tiles = (buf.shape[0] - 8) // 8
    per_half = HALF_STATES // SCAN_CW
    row = lax.broadcasted_iota(jnp.int32, (8, SCAN_CW), 0)
    sgn = -1.0 if reverse else 1.0

    for j in range(per_half):
        c0 = half * 2 * HALF_STATES + j * SCAN_CW
        cre = pl.ds(c0, SCAN_CW)
        cim = pl.ds(c0 + HALF_STATES, SCAN_CW)
        steps = []
        for s, k in enumerate((1, 2, 4)):
            ok, shift = (row < 8 - k, 8 - k) if reverse else (row >= k, k)
            steps.append((shift, jnp.where(ok, tab_ref[pl.ds(s, 1), cre], 0.0),
                          jnp.where(ok, sgn * tab_ref[pl.ds(s, 1), cim], 0.0)))
        trow = 16 if reverse else 8
        pr, pi = tab_ref[pl.ds(trow, 8), cre], tab_ref[pl.ds(trow, 8), cim]
        for t in range(n_tiles):
            base = 8 * (n_tiles - 1 - t) if reverse else 8 + 8 * t
            rows = pl.ds(base, 8)
            re, im = buf[rows, cre], buf[rows, cim]
            for shift, ar, ai in steps:
                sre, sim = pltpu.roll(re, shift, 0), pltpu.roll(im, shift, 0)
                re, im = re + ar * sre - ai * sim, im + ar * sim + ai * sre
            crow = pl.ds(base + 8 if reverse else base - 1, 1)
            cr, ci = buf[crow, cre], buf[crow, cim]
            buf[rows, cre] = re + pr * cr - pi * ci
            buf[rows, cim] = im + pr * ci + pi * cr


def ssm_fwd(proj, bh, ch, apow, dskip, name, comm=None):
    _, T, C = proj.shape
    tm = SCAN_TM
    SW = 4 * HALF_STATES

    def body(u_ref, bh_ref, ch_ref, tab_ref, dsk_ref, y_ref, s_ref, buf):
        @pl.when(pl.program_id(0) == 0)
        def _():
            buf[pl.ds(0, 8), :] = jnp.zeros((8, SW), F32)

        for h in range(2):
            buf[pl.ds(8, tm), pl.ds(h * 2 * HALF_STATES, 2 * HALF_STATES)] = _dot(u_ref[h].astype(BF16), bh_ref[h])
        for h in range(2):
            cols = pl.ds(h * 2 * HALF_STATES, 2 * HALF_STATES)
            _scan_rows(buf, tab_ref, False, h)
            sv = buf[pl.ds(8, tm), cols]
            s_ref[:, cols] = sv
            y_ref[h] = _dot(sv.astype(BF16), ch_ref[h]) + dsk_ref[h] * u_ref[h]
        buf[pl.ds(0, 8), :] = buf[pl.ds(tm, 8), :]

    return _call(
        body, grid=(T // tm,),
        in_specs=[pl.BlockSpec((2, tm, C), lambda i: (3, i, 0)),
                  pl.BlockSpec((2, C, 2 * HALF_STATES), lambda i: (0, 0, 0)),
                  pl.BlockSpec((2, 2 * HALF_STATES, C), lambda i: (0, 0, 0)),
                  pl.BlockSpec((24, SW), lambda i: (0, 0)),
                  pl.BlockSpec((2, 1, C), lambda i: (0, 0, 0))],
        out_specs=[pl.BlockSpec((2, tm, C), lambda i: (0, i, 0)), pl.BlockSpec((tm, SW), lambda i: (i, 0))],
        out_shape=[SDS((2, T, C), F32), SDS((T, SW), F32)],
        scratch_shapes=[pltpu.VMEM((tm + 8, SW), F32)],
        args=(proj, bh, ch, apow, dskip), name=name, comm=comm)


def ssm_bwd(dy, proj, st, bh, ch, apow, dskip, name, comm=None):
    _, T, C = proj.shape
    tm = SCAN_TM
    nt = T // tm
    SW = 4 * HALF_STATES
    HS2 = 2 * HALF_STATES

    def body(dy_ref, u_ref, s_ref, sp_ref, bh_ref, ch_ref, tab_ref, dsk_ref,
             du_ref, da_ref, dbh_ref, dch_ref, dd_ref, lam):
        i = pl.program_id(0)

        @pl.when(i == 0)
        def _():
            lam[pl.ds(tm, 8), :] = jnp.zeros((8, SW), F32)
            da_ref[...] = jnp.zeros_like(da_ref)
            dbh_ref[...] = jnp.zeros_like(dbh_ref)
            dch_ref[...] = jnp.zeros_like(dch_ref)
            dd_ref[...] = jnp.zeros_like(dd_ref)

        for h in range(2):
            lam[pl.ds(0, tm), pl.ds(h * HS2, HS2)] = _dot_nt(dy_ref[h].astype(BF16), ch_ref[h])
        for h in range(2):
            dyv, uv = dy_ref[h], u_ref[h]
            dch_ref[h] += _dot_tn(s_ref[:, pl.ds(h * HS2, HS2)].astype(BF16), dyv.astype(BF16))
            dd_ref[h] += jnp.sum(dyv * uv, axis=0, keepdims=True)
        for h in range(2):
            _scan_rows(lam, tab_ref, True, h)
            lb = lam[pl.ds(0, tm), pl.ds(h * HS2, HS2)].astype(BF16)
            du_ref[h] = _dot_nt(lb, bh_ref[h]) + dsk_ref[h] * dy_ref[h]
            dbh_ref[h] += _dot_tn(u_ref[h].astype(BF16), lb)

        first = i == nt - 1
        per_half = HALF_STATES // SCAN_CW

        def chunk(j, _):
            c0 = pl.multiple_of((j // per_half) * HS2 + (j % per_half) * SCAN_CW, 128)
            cre, cim = pl.ds(c0, SCAN_CW), pl.ds(pl.multiple_of(c0 + HALF_STATES, 128), SCAN_CW)
            row = lax.broadcasted_iota(jnp.int32, (tm, SCAN_CW), 0)
            pre = jnp.where(first, 0.0, sp_ref[pl.ds(7, 1), cre])
            pim = jnp.where(first, 0.0, sp_ref[pl.ds(7, 1), cim])
            spr = jnp.where(row == 0, pre, pltpu.roll(s_ref[:, cre], 1, 0))
            spi = jnp.where(row == 0, pim, pltpu.roll(s_ref[:, cim], 1, 0))
            lr, li = lam[pl.ds(0, tm), cre], lam[pl.ds(0, tm), cim]
            da_ref[:, cre] += jnp.sum(lr * spr + li * spi, axis=0, keepdims=True)
            da_ref[:, cim] += jnp.sum(li * spr - lr * spi, axis=0, keepdims=True)
            return 0

        lax.fori_loop(0, 2 * per_half, chunk, 0)
        lam[pl.ds(tm, 8), :] = lam[pl.ds(0, 8), :]

    rev = lambda i: nt - 1 - i
    return _call(
        body, grid=(nt,),
        in_specs=[pl.BlockSpec((2, tm, C), lambda i: (0, rev(i), 0)),
                  pl.BlockSpec((2, tm, C), lambda i: (3, rev(i), 0)),
                  pl.BlockSpec((tm, SW), lambda i: (rev(i), 0)),
                  pl.BlockSpec((8, SW), lambda i: (jnp.maximum(rev(i) * (tm // 8) - 1, 0), 0)),
                  pl.BlockSpec((2, C, HS2), lambda i: (0, 0, 0)),
                  pl.BlockSpec((2, HS2, C), lambda i: (0, 0, 0)),
                  pl.BlockSpec((24, SW), lambda i: (0, 0)),
                  pl.BlockSpec((2, 1, C), lambda i: (0, 0, 0))],
        out_specs=[pl.BlockSpec((2, tm, C), lambda i: (0, rev(i), 0)),
                   pl.BlockSpec((1, SW), lambda i: (0, 0)),
                   pl.BlockSpec((2, C, HS2), lambda i: (0, 0, 0)),
                   pl.BlockSpec((2, HS2, C), lambda i: (0, 0, 0)),
                   pl.BlockSpec((2, 1, C), lambda i: (0, 0, 0))],
        out_shape=[SDS((2, T, C), F32), SDS((1, SW), F32), SDS((2, C, HS2), F32), SDS((2, HS2, C), F32),
                   SDS((2, 1, C), F32)],
        scratch_shapes=[pltpu.VMEM((tm + 8, SW), F32)],
        args=(dy, proj, st, st, bh, ch, apow, dskip), name=name, comm=comm)


_GELU_C = math.sqrt(2.0 / math.pi)


def _gelu(x):
    t = jnp.tanh(_GELU_C * (x + 0.044715 * x * x * x))
    return 0.5 * x * (1.0 + t), t


def glu_fwd(y, w, b, name):
    _, T, C = y.shape
    tm = 512

    def body(y_ref, w_ref, b_ref, o_ref, lg_ref):
        y0, _ = _gelu(y_ref[0])
        y1, _ = _gelu(y_ref[1])
        lg = _dot(y0.astype(BF16), w_ref[0]) + _dot(y1.astype(BF16), w_ref[1]) + b_ref[...]
        sg = _sigmoid(lg)
        o_ref[0] = y0 * sg[:, :C]
        o_ref[1] = y1 * sg[:, C:]
        lg_ref[0] = lg[:, :C]
        lg_ref[1] = lg[:, C:]

    return pl.pallas_call(
        body, grid=(T // tm,),
        in_specs=[pl.BlockSpec((2, tm, C), lambda i: (0, i, 0)), pl.BlockSpec((2, C, 2 * C), lambda i: (0, 0, 0)),
                  pl.BlockSpec((1, 2 * C), lambda i: (0, 0))],
        out_specs=[pl.BlockSpec((2, tm, C), lambda i: (0, i, 0)), pl.BlockSpec((2, tm, C), lambda i: (0, i, 0))],
        out_shape=[SDS((2, T, C), F32), SDS((2, T, C), F32)], compiler_params=_params(1), name=name)(y, w, b)


def glu_bwd(dcat, y, lg, w, name):
    _, T, C = y.shape
    tm = 512

    def body(d_ref, y_ref, lg_ref, w_ref, dy_ref, dw_ref, db_ref):
        @pl.when(pl.program_id(0) == 0)
        def _():
            dw_ref[...] = jnp.zeros_like(dw_ref)
            db_ref[...] = jnp.zeros_like(db_ref)

        y2, th, sg, dlg = [], [], [], []
        for h in range(2):
            yy, tt = _gelu(y_ref[h])
            ss = _sigmoid(lg_ref[h])
            y2.append(yy)
            th.append(tt)
            sg.append(ss)
            dlg.append(d_ref[h] * yy * ss * (1.0 - ss))
        dl = jnp.concatenate(dlg, axis=1)
        dlb = dl.astype(BF16)
        db_ref[...] += jnp.sum(dl, axis=0, keepdims=True)
        for h in range(2):
            dy2 = d_ref[h] * sg[h] + _dot_nt(dlb, w_ref[h])
            yv = y_ref[h]
            dgelu = 0.5 * (1.0 + th[h]) + 0.5 * yv * (1.0 - th[h] * th[h]) * _GELU_C * (1.0 + 3 * 0.044715 * yv * yv)
            dy_ref[h] = dy2 * dgelu
            dw_ref[h] += _dot_tn(y2[h].astype(BF16), dlb)

    return pl.pallas_call(
        body, grid=(T // tm,),
        in_specs=[pl.BlockSpec((2, tm, C), lambda i: (1, i, 0)), pl.BlockSpec((2, tm, C), lambda i: (0, i, 0)),
                  pl.BlockSpec((2, tm, C), lambda i: (0, i, 0)), pl.BlockSpec((2, C, 2 * C), lambda i: (0, 0, 0))],
        out_specs=[pl.BlockSpec((2, tm, C), lambda i: (0, i, 0)), pl.BlockSpec((2, C, 2 * C), lambda i: (0, 0, 0)),
                   pl.BlockSpec((1, 2 * C), lambda i: (0, 0))],
        out_shape=[SDS((2, T, C), F32), SDS((2, C, 2 * C), F32), SDS((1, 2 * C), F32)],
        compiler_params=_params(1), name=name)(dcat, y, lg, w)


def adamw(w, m, v, slots, name):
    R, C = w.shape
    tr = R
    for cand in (512, 256, 128, 64, 32, 16, 8):
        if R % cand == 0 and cand * C * 4 <= 2 * 1024 * 1024:
            tr = cand
            break
    c1 = 1.0 / (1.0 - ADAM_B1 ** ADAM_STEP)
    c2 = 1.0 / (1.0 - ADAM_B2 ** ADAM_STEP)

    def body(w_ref, m_ref, v_ref, s_ref, g_ref, d_ref, nm_ref, nv_ref):
        g = s_ref[0].astype(F32)
        for j in range(1, N_DEV):
            g = g + s_ref[j].astype(F32)
        nm = ADAM_B1 * m_ref[...] + (1.0 - ADAM_B1) * g
        nv = ADAM_B2 * v_ref[...] + (1.0 - ADAM_B2) * (g * g)
        g_ref[...] = g
        nm_ref[...] = nm
        nv_ref[...] = nv
        d_ref[...] = -ADAM_LR * ((nm * c1) / (jnp.sqrt(nv * c2) + ADAM_EPS) + ADAM_WD * w_ref[...])

    spec = pl.BlockSpec((tr, C), lambda i: (i, 0))
    return pl.pallas_call(
        body, grid=(R // tr,),
        in_specs=[spec, spec, spec, pl.BlockSpec((N_DEV, tr, C), lambda i: (0, i, 0))],
        out_specs=[spec] * 4, out_shape=[SDS((R, C), F32)] * 4, compiler_params=_params(1), name=name)(w, m, v, slots)


def _discretise(a_re, a_im, log_dt, b_re, b_im):
    dt = jnp.exp(log_dt)[:, None]
    e = jnp.exp(dt * a_re)
    ar, ai = e * jnp.cos(dt * a_im), e * jnp.sin(dt * a_im)
    den = a_re * a_re + a_im * a_im
    nr, ni = ar - 1.0, ai
    wr = (nr * a_re + ni * a_im) / den
    wi = (ni * a_re - nr * a_im) / den
    bbr = wr[..., None] * b_re - wi[..., None] * b_im
    bbi = wr[..., None] * b_im + wi[..., None] * b_re
    return ar, ai, bbr, bbi


def _block_diag(t):
    eye = jnp.eye(16, dtype=t.dtype).reshape(1, 16, 1, 16, 1)
    r, c = t.shape[1], t.shape[2]
    return (t.reshape(2, 16, r, 1, c) * eye).reshape(2, 16 * r, 16 * c)


def _diag_blocks(m, r, c):
    eye = jnp.eye(16, dtype=m.dtype).reshape(1, 16, 1, 16, 1)
    return jnp.sum(m.reshape(2, 16, r, 16, c) * eye, axis=3).reshape(32, r, c)


def _state_vec(re, im):
    return jnp.stack([re.reshape(2, HALF_STATES), im.reshape(2, HALF_STATES)], axis=1).reshape(-1)


BIG = ("ffn1_w_in", "ffn1_w_out", "w_mix_in", "w_glu", "w_mix_out", "ffn2_w_in", "ffn2_w_out")
WEIGHTS = ("ffn1_pre_g", "ffn1_w_in", "ffn1_w_out", "ffn1_post_g", "mix_pre_g", "w_mix_in", "a_re", "a_im", "log_dt",
           "b_re", "b_im", "c_re", "c_im", "d_skip", "w_glu", "b_glu", "w_mix_out", "mix_post_g", "ffn2_pre_g",
           "ffn2_w_in", "ffn2_w_out", "ffn2_post_g")
SMALL = tuple(n for n in WEIGHTS if n not in BIG)
PACK_COLS = 1024


def _pack(parts):
    flat = jnp.concatenate([p.reshape(-1) for p in parts])
    rows = -(-flat.shape[0] // (8 * PACK_COLS)) * 8
    return jnp.pad(flat, (0, rows * PACK_COLS - flat.shape[0])).reshape(rows, PACK_COLS)


def _unpack(packed, shapes):
    flat, out, off = packed.reshape(-1), [], 0
    for s in shapes:
        n = math.prod(s)
        out.append(flat[off:off + n].reshape(s))
        off += n
    return out


def _gather(names, wb):
    return [wb[n] for n in names], [False] * len(names)


def _ffn_bwd(dy, do, saved, x, pre_g, w_in, w_out4, tag, post=None):
    h, z, a = saved
    T = x.shape[0]
    dz = ffn_dact(do, w_out4, z, f"{tag}_dact")
    dz8 = dz.reshape(8, T, dz.shape[-1])
    dw_out, _ = mm_tn(a, do, True, False, 4, f"{tag}_dwout")
    dw_in, (s_out,) = mm_tn(h, dz8, False, True, 8, f"{tag}_dwin", comm=([dw_out.reshape(8, -1, D_MODEL)], [True]))
    outs, (s_in,) = dh_pre_bwd(dz8, w_in, x, pre_g, dy, f"{tag}_dh", comm=([dw_in], [True]), post=post)
    return outs, (s_in, s_out)


def local_step(x, tgt, sp, wb):
    T = x.shape[0]
    ar, ai, bbr, bbi = _discretise(sp["a_re"], sp["a_im"], sp["log_dt"], sp["b_re"], sp["b_im"])
    powers = [(ar, ai)]
    for _ in range(7):
        pr, pi = powers[-1]
        powers.append((pr * ar - pi * ai, pr * ai + pi * ar))
    zero = jnp.zeros_like(ar)
    rows = [_state_vec(*powers[k - 1]) for k in (1, 2, 4)] + [_state_vec(zero, zero)] * 5
    rows += [_state_vec(pr, pi) for pr, pi in powers]
    rows += [_state_vec(pr, -pi) for pr, pi in reversed(powers)]
    apow = jnp.stack(rows)
    bh = jnp.concatenate([_block_diag(bbr.transpose(0, 2, 1)), _block_diag(bbi.transpose(0, 2, 1))], axis=2)
    ch = jnp.concatenate([_block_diag(sp["c_re"].transpose(0, 2, 1)), _block_diag(-sp["c_im"].transpose(0, 2, 1))], axis=1)
    bh, ch = bh.astype(BF16), ch.astype(BF16)
    dskip = sp["d_skip"].reshape(2, 1, 256)

    w1_in = gather_two_level(wb["ffn1_w_in"], "gather_w1in")
    (h1, z1, a1), (w1_out, w_mi) = ffn_in(
        x, sp["ffn1_pre_g"], w1_in, "ffn1_in", comm=_gather(["ffn1_w_out", "w_mix_in"], wb))
    w1_out4 = w1_out.reshape(4, -1, D_MODEL)
    (o1, x1), (w_glu, w_mo) = mm_acc_norm(
        a1, w1_out4, x, sp["ffn1_post_g"], 0.5, "ffn1_out", comm=_gather(["w_glu", "w_mix_out"], wb))
    w_glu2, w_mo4 = w_glu.reshape(2, 256, 512), w_mo.reshape(4, 256, D_MODEL)
    h2, proj = norm_proj(x1, sp["mix_pre_g"], w_mi, "mix_proj")
    (y_ssm, states), (w2_in,) = ssm_fwd(proj, bh, ch, apow, dskip, "ssm_fwd", comm=_gather(["ffn2_w_in"], wb))
    os_, ls_ = [], []
    for d in DILATIONS:
        (o_d, l_d), got = attn_fwd(proj, d, f"attn_fwd_d{d}",
                                   comm=_gather(["ffn2_w_out"], wb) if d == DILATIONS[-1] else None)
        os_.append(o_d)
        ls_.append(l_d)
    w2_out4 = got[0].reshape(4, -1, D_MODEL)
    o_ssm, lg = glu_fwd(y_ssm, w_glu2, sp["b_glu"], "glu_fwd")
    cat, lse = attn_merge(os_, ls_, o_ssm, "attn_merge")
    (mixed, x2), _ = mm_acc_norm(cat, w_mo4, x1, sp["mix_post_g"], 1.0, "mix_out")
    (h3, z3, a3), _ = ffn_in(x2, sp["ffn2_pre_g"], w2_in, "ffn2_in")
    (dy3, sq, do3, dg_f2post), _ = mm_acc_norm(a3, w2_out4, x2, sp["ffn2_post_g"], 0.5, "ffn2_out", tgt=tgt)

    (dx2, dg_f2pre, dmixed, dg_mpost), (s_w2in, s_w2out) = _ffn_bwd(
        dy3, do3, (h3, z3, a3), x2, sp["ffn2_pre_g"], w2_in, w2_out4, "ffn2", post=(mixed, sp["mix_post_g"], 1.0))
    dcat = mm_nt_b(dmixed, w_mo4, "mix_dcat")
    dw_mo, _ = mm_tn(cat, dmixed, True, False, 4, "mix_dwout")
    dy_ssm, dw_glu, db_glu = glu_bwd(dcat, y_ssm, lg, w_glu2, "glu_bwd")
    (du, da, dbh, dch, dd), (s_wmo, s_wglu) = ssm_bwd(
        dy_ssm, proj, states, bh, ch, apow, dskip, "ssm_bwd",
        comm=([dw_mo.reshape(8, 128, D_MODEL), dw_glu.astype(BF16).reshape(8, 64, 512)], [True, True]))
    dqkv = None
    for d in DILATIONS:
        dqkv = attn_bwd(proj, dcat, cat, lse, dqkv, d, f"attn_bwd_d{d}", du=du if d == DILATIONS[-1] else None)
    dproj = dqkv.reshape(8, T, 256)
    dw_mi, _ = mm_tn(h2, dproj, False, True, 8, "mix_dwin")
    (dx1, dg_mpre, do1, dg_f1post), (s_wmi,) = dh_pre_bwd(
        dproj, w_mi, x1, sp["mix_pre_g"], dx2, "mix_dh", comm=([dw_mi], [True]), post=(o1, sp["ffn1_post_g"], 0.5))
    (dx0, dg_f1pre), (s_w1in, s_w1out) = _ffn_bwd(
        dx1, do1, (h1, z1, a1), x, sp["ffn1_pre_g"], w1_in, w1_out4, "ffn1")

    da4 = da.reshape(2, 2, HALF_STATES)
    d_ar, d_ai = da4[:, 0].reshape(32, N_STATE), da4[:, 1].reshape(32, N_STATE)
    d_bbr = _diag_blocks(dbh[:, :, :HALF_STATES], 16, N_STATE).transpose(0, 2, 1)
    d_bbi = _diag_blocks(dbh[:, :, HALF_STATES:], 16, N_STATE).transpose(0, 2, 1)
    _, disc_vjp = jax.vjp(_discretise, sp["a_re"], sp["a_im"], sp["log_dt"], sp["b_re"], sp["b_im"])
    g_are, g_aim, g_ldt, g_bre, g_bim = disc_vjp((d_ar, d_ai, d_bbr, d_bbi))
    g_cre = _diag_blocks(dch[:, :HALF_STATES], N_STATE, 16).transpose(0, 2, 1)
    g_cim = -_diag_blocks(dch[:, HALF_STATES:], N_STATE, 16).transpose(0, 2, 1)
    small = {
        "ffn1_pre_g": dg_f1pre, "ffn1_post_g": dg_f1post, "mix_pre_g": dg_mpre, "a_re": g_are, "a_im": g_aim,
        "log_dt": g_ldt, "b_re": g_bre, "b_im": g_bim, "c_re": g_cre, "c_im": g_cim, "d_skip": dd.reshape(1, 512),
        "b_glu": db_glu, "mix_post_g": dg_mpost, "ffn2_pre_g": dg_f2pre, "ffn2_post_g": dg_f2post,
    }
    small_slots = gather_two_level(_pack([small[n] for n in SMALL]), "exchange_small")
    big_slots = {"ffn1_w_in": s_w1in, "ffn1_w_out": s_w1out, "w_mix_in": s_wmi, "w_glu": s_wglu, "w_mix_out": s_wmo,
                 "ffn2_w_in": s_w2in, "ffn2_w_out": s_w2out}
    return sq, dx0, big_slots, small_slots


def kernel(x, ffn1_pre_g, ffn1_w_in, ffn1_w_out, ffn1_post_g, mix_pre_g, w_mix_in, a_re, a_im, log_dt, b_re, b_im, c_re, c_im, d_skip, w_glu, b_glu, w_mix_out, mix_post_g, ffn2_pre_g, ffn2_w_in, ffn2_w_out, ffn2_post_g, loss_target, m_ffn1_pre_g, m_ffn1_w_in, m_ffn1_w_out, m_ffn1_post_g, m_mix_pre_g, m_w_mix_in, m_a_re, m_a_im, m_log_dt, m_b_re, m_b_im, m_c_re, m_c_im, m_d_skip, m_w_glu, m_b_glu, m_w_mix_out, m_mix_post_g, m_ffn2_pre_g, m_ffn2_w_in, m_ffn2_w_out, m_ffn2_post_g, v_ffn1_pre_g, v_ffn1_w_in, v_ffn1_w_out, v_ffn1_post_g, v_mix_pre_g, v_w_mix_in, v_a_re, v_a_im, v_log_dt, v_b_re, v_b_im, v_c_re, v_c_im, v_d_skip, v_w_glu, v_b_glu, v_w_mix_out, v_mix_post_g, v_ffn2_pre_g, v_ffn2_w_in, v_ffn2_w_out, v_ffn2_post_g):
    args = dict(locals())
    w = {n: args[n][0] for n in WEIGHTS}
    m = {n: args["m_" + n][0] for n in WEIGHTS}
    v = {n: args["v_" + n][0] for n in WEIGHTS}

    wb = {n: w[n].astype(BF16) for n in BIG}
    sp = {n: w[n] for n in SMALL}
    for n in ("ffn1_pre_g", "ffn1_post_g", "mix_pre_g", "mix_post_g", "ffn2_pre_g", "ffn2_post_g", "b_glu", "d_skip"):
        sp[n] = w[n].reshape(1, -1)

    sq, grad_x, big_slots, small_slots = local_step(x[0], loss_target[0], sp, wb)
    loss = lax.psum(0.5 / D_MODEL * jnp.sum(sq), ("x", "y", "c"))

    outs = {}
    for n in BIG:
        shp = w[n].shape
        r2 = lambda t: t.reshape(-1, shp[-1])
        res = adamw(r2(w[n]), r2(m[n]), r2(v[n]), big_slots[n].reshape(N_DEV, -1, shp[-1]), f"adamw_{n}")
        outs[n] = [t.reshape((1,) + shp) for t in res]
    res = adamw(_pack([w[n] for n in SMALL]), _pack([m[n] for n in SMALL]), _pack([v[n] for n in SMALL]),
                small_slots, "adamw_small")
    shapes = [(1,) + w[n].shape for n in SMALL]
    unpacked = [_unpack(t, shapes) for t in res]
    for j, n in enumerate(SMALL):
        outs[n] = [unpacked[k][j] for k in range(4)]

    result = [loss, grad_x[None]]
    for k in range(4):
        result += [outs[n][k] for n in WEIGHTS]
    return tuple(result)
```

```python
import functools
import math

import jax
import jax.numpy as jnp
from jax import lax
from jax.experimental import pallas as pl
from jax.experimental.pallas import tpu as pltpu

F32, BF16 = jnp.float32, jnp.bfloat16
SDS = jax.ShapeDtypeStruct

D_MODEL = 1024
N_DEV = 8
HEAD_DIM = 64
PAIR_W = 128
QBLK = 128
DILATIONS = (1, 4, 16)
N_STATE = 64
HALF_STATES = 1024
NORM_EPS = 1e-6
NEG = -1e30
VMEM_LIMIT = 56 * 1024 * 1024
ADAM_LR, ADAM_B1, ADAM_B2, ADAM_EPS, ADAM_WD, ADAM_STEP = 1e-3, 0.9, 0.999, 1e-8, 0.01, 10
SCAN_TM = 256
SCAN_CW = 512


def _params(n_grid):
    return pltpu.CompilerParams(dimension_semantics=("arbitrary",) * n_grid, vmem_limit_bytes=VMEM_LIMIT)


def _dot(a, b):
    return jnp.dot(a, b, preferred_element_type=F32)


def _dot_nt(a, b):
    return lax.dot_general(a, b, (((1,), (1,)), ((), ())), preferred_element_type=F32)


def _dot_tn(a, b):
    return lax.dot_general(a, b, (((0,), (0,)), ((), ())), preferred_element_type=F32)


def _sigmoid(v):
    return 0.5 * jnp.tanh(0.5 * v) + 0.5


def _resident(shape):
    return pl.BlockSpec(shape, lambda i: (0,) * len(shape), pipeline_mode=pl.Buffered(1))


ROW_SPLIT = 2


def _exchange_phase(ins, outs, scatter, sems, start):
    send_sems, recv_sems, loc_sems = sems
    x, y, c = lax.axis_index("x"), lax.axis_index("y"), lax.axis_index("c")
    me = 4 * x + 2 * y + c
    own_copies, sends, arrivals = [], [], []
    for i in range(len(ins)):
        own = ins[i].at[me] if scatter[i] else ins[i]
        own_copies.append(pltpu.make_async_copy(own, outs[i].at[me], loc_sems.at[i]))
        for k in range(1, N_DEV):
            px = 1 - x if k & 4 else x
            py = 1 - y if k & 2 else y
            pc = 1 - c if k & 1 else c
            peer = 4 * px + 2 * py + pc
            src = ins[i].at[peer] if scatter[i] else ins[i]
            common = dict(src_ref=src, send_sem=send_sems.at[i, k - 1], recv_sem=recv_sems.at[i, k - 1],
                          device_id=(px, py, pc), device_id_type=pl.DeviceIdType.MESH)
            sends.append(pltpu.make_async_remote_copy(dst_ref=outs[i].at[me], **common))
            if not start:
                arrivals.append(pltpu.make_async_remote_copy(dst_ref=outs[i].at[peer], **common))
    if start:
        for cp in own_copies + sends:
            cp.start()
    else:
        for cp in arrivals:
            cp.wait_recv()
        for cp in sends:
            cp.wait_send()
        for cp in own_copies:
            cp.wait()


def _comm_shapes(arrs, scatter):
    n = len(arrs)
    out_shapes = [SDS(a.shape if scatter[i] else (N_DEV,) + a.shape, a.dtype) for i, a in enumerate(arrs)]
    sems = [pltpu.SemaphoreType.DMA((n, N_DEV - 1)), pltpu.SemaphoreType.DMA((n, N_DEV - 1)),
            pltpu.SemaphoreType.DMA((n,))]
    return out_shapes, sems


def gather_two_level(arr, name):
    def body(x_ref, out_ref, send_sems, recv_sems, local_sem):
        x, y, c = lax.axis_index("x"), lax.axis_index("y"), lax.axis_index("c")
        sibling = (x, y, 1 - c)
        chips = [(1 - x, y), (x, 1 - y), (1 - x, 1 - y)]

        def slot(px, py, pc):
            return out_ref.at[4 * px + 2 * py + pc]

        def copy(k, block, to, src=None):
            return pltpu.make_async_remote_copy(
                src_ref=slot(*block) if src is None else src, dst_ref=slot(*block),
                send_sem=send_sems.at[k], recv_sem=recv_sems.at[k], device_id=to, device_id_type=pl.DeviceIdType.MESH)

        mine = pltpu.make_async_copy(x_ref, slot(x, y, c), local_sem)
        mine.start()
        first = [copy(0, (x, y, c), sibling, src=x_ref)]
        first += [copy(1 + j, (x, y, c), (*chip, c), src=x_ref) for j, chip in enumerate(chips)]
        for cp in first:
            cp.start()
        passed = [copy(4 + j, (*chip, c), sibling) for j, chip in enumerate(chips)]
        for j, chip in enumerate(chips):
            copy(1 + j, (*chip, c), (x, y, c)).wait_recv()
            passed[j].start()
        copy(0, sibling, (x, y, c)).wait_recv()
        for j, chip in enumerate(chips):
            copy(4 + j, (*chip, 1 - c), (x, y, c)).wait_recv()
        for cp in first + passed:
            cp.wait_send()
        mine.wait()

    anyspec = pl.BlockSpec(memory_space=pl.ANY)
    return pl.pallas_call(
        body, in_specs=[anyspec], out_specs=anyspec, out_shape=SDS((N_DEV,) + arr.shape, arr.dtype),
        scratch_shapes=[pltpu.SemaphoreType.DMA((N_DEV - 1,)), pltpu.SemaphoreType.DMA((N_DEV - 1,)),
                        pltpu.SemaphoreType.DMA],
        compiler_params=pltpu.CompilerParams(has_side_effects=True), name=name)(arr)


def _call(body, *, grid, in_specs, out_specs, out_shape, args, name, scratch_shapes=(), comm=None):
    n_grid, scratch_shapes = len(grid), list(scratch_shapes)
    if comm is None:
        outs = pl.pallas_call(body, grid=grid, in_specs=in_specs, out_specs=out_specs, out_shape=out_shape,
                              scratch_shapes=scratch_shapes, compiler_params=_params(n_grid), name=name)(*args)
        return outs, []
    arrs, scatter = comm
    nc, n_in, n_out, n_sc = len(arrs), len(in_specs), len(out_specs), len(scratch_shapes)
    comm_shapes, sems = _comm_shapes(arrs, scatter)

    def wrapped(*refs):
        ins, cins = refs[:n_in], refs[n_in:n_in + nc]
        o0 = n_in + nc
        outs, couts = refs[o0:o0 + n_out], refs[o0 + n_out:o0 + n_out + nc]
        s0 = o0 + n_out + nc
        scratch, sem_refs = refs[s0:s0 + n_sc], refs[s0 + n_sc:]
        first = functools.reduce(jnp.logical_and, [pl.program_id(k) == 0 for k in range(n_grid)])
        last = functools.reduce(jnp.logical_and, [pl.program_id(k) == grid[k] - 1 for k in range(n_grid)])

        @pl.when(first)
        def _():
            _exchange_phase(cins, couts, scatter, sem_refs, True)

        body(*ins, *outs, *scratch)

        @pl.when(last)
        def _():
            _exchange_phase(cins, couts, scatter, sem_refs, False)

    anyspec = pl.BlockSpec(memory_space=pl.ANY)
    res = pl.pallas_call(
        wrapped, grid=grid, in_specs=list(in_specs) + [anyspec] * nc, out_specs=list(out_specs) + [anyspec] * nc,
        out_shape=list(out_shape) + comm_shapes, scratch_shapes=scratch_shapes + sems,
        compiler_params=pltpu.CompilerParams(dimension_semantics=("arbitrary",) * n_grid,
                                             vmem_limit_bytes=VMEM_LIMIT, has_side_effects=True),
        name=name)(*args, *arrs)
    return res[:n_out], res[n_out:]


def _rms(xv, g):
    r = lax.rsqrt(jnp.mean(xv * xv, axis=-1, keepdims=True) + NORM_EPS)
    return (xv * r * g).astype(BF16)


def ffn_in(x, g, w, name, comm=None):
    T, D = x.shape
    F = w.shape[1]
    tm = 512

    def body(x_ref, g_ref, w_ref, h_ref, z_ref, a_ref):
        hv = _rms(x_ref[...], g_ref[...])
        h_ref[...] = hv
        pending = None
        for j in range(5):
            if j < 4:
                zs = (_dot_nt(hv, w_ref[j]), _dot_nt(hv, w_ref[j + 4]))
            if pending is not None:
                zg, zu = pending
                sg = _sigmoid(zg)
                silu = zg * sg
                z_ref[0, j - 1] = (zu * (sg + silu - silu * sg)).astype(BF16)
                z_ref[1, j - 1] = silu.astype(BF16)
                a_ref[j - 1] = (silu * zu).astype(BF16)
            pending = zs

    return _call(
        body, grid=(T // tm,),
        in_specs=[pl.BlockSpec((tm, D), lambda i: (i, 0)), pl.BlockSpec((1, D), lambda i: (0, 0)),
                  _resident((8, F, D))],
        out_specs=[pl.BlockSpec((tm, D), lambda i: (i, 0)), pl.BlockSpec((2, 4, tm, F), lambda i: (0, 0, i, 0)),
                   pl.BlockSpec((4, tm, F), lambda i: (0, i, 0))],
        out_shape=[SDS((T, D), BF16), SDS((2, 4, T, F), BF16), SDS((4, T, F), BF16)],
        args=(x, g, w), name=name, comm=comm)


def norm_proj(x, g, w, name):
    T, K = x.shape
    nb, _, N = w.shape
    tm = 512

    def body(x_ref, g_ref, w_ref, h_ref, o_ref):
        hv = _rms(x_ref[...], g_ref[...])
        h_ref[...] = hv
        for b in range(nb):
            o_ref[b] = _dot(hv, w_ref[b])

    return pl.pallas_call(
        body, grid=(T // tm,),
        in_specs=[pl.BlockSpec((tm, K), lambda i: (i, 0)), pl.BlockSpec((1, K), lambda i: (0, 0)),
                  _resident((nb, K, N))],
        out_specs=[pl.BlockSpec((tm, K), lambda i: (i, 0)), pl.BlockSpec((nb, tm, N), lambda i: (0, i, 0))],
        out_shape=[SDS((T, K), BF16), SDS((nb, T, N), F32)], compiler_params=_params(1), name=name)(x, g, w)


def mm_acc_norm(a, w, xres, g, scale, name, comm=None, tgt=None):
    nb, T, K = a.shape
    D = w.shape[2]
    tm = 512
    rc = tm // ROW_SPLIT
    with_loss = tgt is not None

    def body(a_ref, w_ref, x_ref, g_ref, *rest):
        if with_loss:
            t_ref, dy_ref, sq_ref, do_ref, dg_ref = rest

            @pl.when(pl.program_id(0) == 0)
            def _():
                sq_ref[...] = jnp.zeros_like(sq_ref)
                dg_ref[...] = jnp.zeros_like(dg_ref)
        else:
            o_ref, y_ref = rest
        accs = []
        for c in range(ROW_SPLIT):
            rows = pl.ds(c * rc, rc)
            o = _dot(a_ref[0, rows, :].astype(BF16), w_ref[0])
            for b in range(1, nb):
                o += _dot(a_ref[b, rows, :].astype(BF16), w_ref[b])
            accs.append(o)
        for c, o in enumerate(accs):
            rows = pl.ds(c * rc, rc)
            r = lax.rsqrt(jnp.mean(o * o, axis=-1, keepdims=True) + NORM_EPS)
            y = x_ref[rows, :] + scale * (o * r * g_ref[...])
            if with_loss:
                e = y - t_ref[rows, :]
                dy = e * (1.0 / D)
                dy_ref[rows, :] = dy
                sq_ref[...] += jnp.sum(e * e, axis=0, keepdims=True)
                do, dg = _post_bwd(dy, o, g_ref[...], scale)
                do_ref[rows, :] = do
                dg_ref[...] += dg
            else:
                o_ref[rows, :] = o
                y_ref[rows, :] = y

    tile = pl.BlockSpec((tm, D), lambda i: (i, 0))
    row = pl.BlockSpec((1, D), lambda i: (0, 0))
    in_specs = [pl.BlockSpec((nb, tm, K), lambda i: (0, i, 0)), _resident((nb, K, D)), tile, row]
    args = (a, w, xres, g)
    if with_loss:
        return _call(body, grid=(T // tm,), in_specs=in_specs + [tile], out_specs=[tile, row, tile, row],
                     out_shape=[SDS((T, D), F32), SDS((1, D), F32), SDS((T, D), BF16), SDS((1, D), F32)],
                     args=args + (tgt,), name=name, comm=comm)
    return _call(body, grid=(T // tm,), in_specs=in_specs, out_specs=[tile, tile],
                 out_shape=[SDS((T, D), F32), SDS((T, D), F32)], args=args, name=name, comm=comm)


def _post_bwd(dy, ov, g, scale):
    r = scale * dy
    rstd = lax.rsqrt(jnp.mean(ov * ov, axis=-1, keepdims=True) + NORM_EPS)
    oh = ov * rstd
    rg = r * g
    do = rstd * (rg - oh * jnp.mean(rg * oh, axis=-1, keepdims=True))
    return do.astype(BF16), jnp.sum(r * oh, axis=0, keepdims=True)


def mm_nt_b(gr, w, name):
    T, N = gr.shape
    nb, K, _ = w.shape
    tm = 512

    def body(g_ref, w_ref, o_ref):
        gv = g_ref[...]
        for b in range(nb):
            o_ref[b] = _dot_nt(gv, w_ref[b])

    return pl.pallas_call(
        body, grid=(T // tm,),
        in_specs=[pl.BlockSpec((tm, N), lambda i: (i, 0)), _resident((nb, K, N))],
        out_specs=pl.BlockSpec((nb, tm, K), lambda i: (0, i, 0)),
        out_shape=SDS((nb, T, K), F32), compiler_params=_params(1), name=name)(gr, w)


def ffn_dact(do, w_out, z, name):
    T, D = do.shape
    nb, F, _ = w_out.shape
    tm = 512

    rc = tm // ROW_SPLIT

    def body(g_ref, w_ref, z_ref, dz_ref):
        das = [_dot_nt(g_ref[pl.ds(c * rc, rc), :], w_ref[...]) for c in range(ROW_SPLIT)]
        for c, da in enumerate(das):
            rows = pl.ds(c * rc, rc)
            dz_ref[0, rows, :] = (da * z_ref[0, rows, :].astype(F32)).astype(BF16)
            dz_ref[1, rows, :] = (da * z_ref[1, rows, :].astype(F32)).astype(BF16)

    return pl.pallas_call(
        body, grid=(nb, T // tm),
        in_specs=[pl.BlockSpec((tm, D), lambda b, i: (i, 0)), pl.BlockSpec((None, F, D), lambda b, i: (b, 0, 0)),
                  pl.BlockSpec((2, None, tm, F), lambda b, i: (0, b, i, 0))],
        out_specs=pl.BlockSpec((2, None, tm, F), lambda b, i: (0, b, i, 0)),
        out_shape=SDS((2, nb, T, F), BF16), compiler_params=_params(2), name=name)(do, w_out, z)


def mm_tn(a, g, a_batched, g_batched, nb, name, comm=None):
    T = a.shape[-2]
    K, N = a.shape[-1], g.shape[-1]
    tk = 2048
    nk = T // tk

    def body(a_ref, g_ref, o_ref, acc):
        k = pl.program_id(1)

        @pl.when(k == 0)
        def _():
            acc[...] = jnp.zeros_like(acc)

        acc[...] += _dot_tn(a_ref[...].astype(BF16), g_ref[...].astype(BF16))

        @pl.when(k == nk - 1)
        def _():
            o_ref[...] = acc[...].astype(BF16)

    a_spec = (pl.BlockSpec((None, tk, K), lambda b, k: (b, k, 0)) if a_batched
              else pl.BlockSpec((tk, K), lambda b, k: (k, 0)))
    g_spec = (pl.BlockSpec((None, tk, N), lambda b, k: (b, k, 0)) if g_batched
              else pl.BlockSpec((tk, N), lambda b, k: (k, 0)))
    (out,), slots = _call(
        body, grid=(nb, nk), in_specs=[a_spec, g_spec],
        out_specs=[pl.BlockSpec((None, K, N), lambda b, k: (b, 0, 0))],
        out_shape=[SDS((nb, K, N), BF16)], scratch_shapes=[pltpu.VMEM((K, N), F32)],
        args=(a, g), name=name, comm=comm)
    return out, slots


def dh_pre_bwd(dz, w, x, g, dyres, name, comm=None, post=None, w_transposed=False):
    nb, T, F = dz.shape
    D = x.shape[1]
    tm = 512
    rc = tm // ROW_SPLIT
    mm = _dot if w_transposed else _dot_nt

    def body(dz_ref, w_ref, x_ref, g_ref, dy_ref, *rest):
        if post is None:
            dx_ref, dg_ref = rest
        else:
            o_ref, gp_ref, dx_ref, dg_ref, do_ref, dgp_ref = rest

        @pl.when(pl.program_id(0) == 0)
        def _():
            dg_ref[...] = jnp.zeros_like(dg_ref)
            if post is not None:
                dgp_ref[...] = jnp.zeros_like(dgp_ref)

        accs = []
        for c in range(ROW_SPLIT):
            rows = pl.ds(c * rc, rc)
            dh = mm(dz_ref[0, rows, :].astype(BF16), w_ref[0])
            for b in range(1, nb):
                dh += mm(dz_ref[b, rows, :].astype(BF16), w_ref[b])
            accs.append(dh)
        for c, dh in enumerate(accs):
            rows = pl.ds(c * rc, rc)
            xv = x_ref[rows, :]
            rstd = lax.rsqrt(jnp.mean(xv * xv, axis=-1, keepdims=True) + NORM_EPS)
            xh = xv * rstd
            dg_ref[...] += jnp.sum(dh * xh, axis=0, keepdims=True)
            dhg = dh * g_ref[...]
            dx = dy_ref[rows, :] + rstd * (dhg - xh * jnp.mean(dhg * xh, axis=-1, keepdims=True))
            dx_ref[rows, :] = dx
            if post is not None:
                do, dgp = _post_bwd(dx, o_ref[rows, :], gp_ref[...], post[2])
                do_ref[rows, :] = do
                dgp_ref[...] += dgp

    tile = pl.BlockSpec((tm, D), lambda i: (i, 0))
    row = pl.BlockSpec((1, D), lambda i: (0, 0))
    in_specs = [pl.BlockSpec((nb, tm, F), lambda i: (0, i, 0)), _resident(w.shape), tile, row, tile]
    out_specs, out_shape, args = [tile, row], [SDS((T, D), F32), SDS((1, D), F32)], (dz, w, x, g, dyres)
    if post is not None:
        in_specs += [tile, row]
        out_specs += [tile, row]
        out_shape += [SDS((T, D), BF16), SDS((1, D), F32)]
        args += (post[0], post[1])
    return _call(body, grid=(T // tm,), in_specs=in_specs, out_specs=out_specs, out_shape=out_shape,
                 args=args, name=name, comm=comm)


ATTN_GROUP = {1: 4, 4: 1, 16: 1}
ATTN_UNROLL = 4


def _attn_masks():
    qi = lax.broadcasted_iota(jnp.int32, (QBLK, QBLK), 0)
    kj = lax.broadcasted_iota(jnp.int32, (QBLK, QBLK), 1)
    cur_ok = kj <= qi
    prev_ok = kj >= qi
    dcur = (qi - kj).astype(F32)
    return cur_ok, prev_ok, dcur, dcur + float(QBLK)


def _head_slopes(p, d):
    out = []
    for hq in range(2):
        v = [float(d) * 2.0 ** -(2 * q + hq + 1) for q in range(4)]
        out.append(jnp.where(p == 0, v[0], jnp.where(p == 1, v[1], jnp.where(p == 2, v[2], v[3]))))
    return out


def _rows(start, d):
    return pl.ds(start, QBLK, stride=d) if d > 1 else pl.ds(start, QBLK)


def _pair_spec(rows, part, blk):
    return pl.BlockSpec((None, rows, PAIR_W), lambda p, n: (2 * part + p // 2, blk(n), p % 2))


def _for_query_blocks(d, groups, several):
    blocks = [(g, r) for g in range(groups) for r in range(d)]
    for s in range(0, len(blocks), ATTN_UNROLL):
        several(blocks[s:s + ATTN_UNROLL])


def attn_fwd(proj, d, name, comm=None):
    T = proj.shape[1]
    sb, groups = QBLK * d, ATTN_GROUP[d]
    rb = sb * groups
    nblk = T // rb

    def body(q_ref, kc_ref, kp_ref, vc_ref, vp_ref, o_ref, l_ref):
        p, n = pl.program_id(0), pl.program_id(1)
        cur_ok, prev_ok, dcur, dprev = _attn_masks()
        first_ok = jnp.logical_and(prev_ok, n > 0)
        lane_head = lax.broadcasted_iota(jnp.int32, (QBLK, PAIR_W), 1) // HEAD_DIM
        slopes = _head_slopes(p, d)

        def several(blocks):
            work = []
            for g, r in blocks:
                rows = _rows(g * sb + r, d)
                q = q_ref[rows, :]
                kc, vc = kc_ref[rows, :].astype(BF16), vc_ref[rows, :].astype(BF16)
                if g == 0:
                    prow, pok = _rows(r, d), first_ok
                    kp, vp = kp_ref[prow, :].astype(BF16), vp_ref[prow, :].astype(BF16)
                else:
                    prow, pok = _rows((g - 1) * sb + r, d), prev_ok
                    kp, vp = kc_ref[prow, :].astype(BF16), vc_ref[prow, :].astype(BF16)
                for hq in range(2):
                    qm = jnp.where(lane_head == hq, q, 0.0).astype(BF16)
                    work.append([rows, hq, pok, vc, vp, _dot_nt(qm, kc), _dot_nt(qm, kp)])
            for w in work:
                _, hq, pok, _, _, sc, sp = w
                sc = jnp.where(cur_ok, sc * 0.125 - slopes[hq] * dcur, NEG)
                sp = jnp.where(pok, sp * 0.125 - slopes[hq] * dprev, NEG)
                m = jnp.maximum(jnp.max(sc, axis=1, keepdims=True), jnp.max(sp, axis=1, keepdims=True))
                pc = jnp.exp(sc - m)
                pp = jnp.exp(sp - m)
                den = jnp.sum(pc, axis=1, keepdims=True) + jnp.sum(pp, axis=1, keepdims=True)
                w[5:] = [pc.astype(BF16), pp.astype(BF16), 1.0 / den, m + jnp.log(den)]
            for i in range(0, len(work), 2):
                o_acc = jnp.zeros((QBLK, PAIR_W), F32)
                l_acc = jnp.zeros((QBLK, PAIR_W), F32)
                for rows, hq, _, vc, vp, pc, pp, inv, lse in work[i:i + 2]:
                    hm = lane_head == hq
                    o_acc = jnp.where(hm, (_dot(pc, vc) + _dot(pp, vp)) * inv, o_acc)
                    l_acc = jnp.where(hm, lse, l_acc)
                o_ref[rows, :] = o_acc
                l_ref[rows, :] = l_acc

        _for_query_blocks(d, groups, several)

    cur = lambda part: _pair_spec(rb, part, lambda n: n)
    prv = lambda part: _pair_spec(sb, part, lambda n: jnp.maximum(n * groups - 1, 0))
    return _call(
        body, grid=(4, nblk), in_specs=[cur(0), cur(1), prv(1), cur(2), prv(2)], out_specs=[cur(0), cur(0)],
        out_shape=[SDS((2, T, 2 * PAIR_W), F32), SDS((2, T, 2 * PAIR_W), F32)],
        args=(proj, proj, proj, proj, proj), name=name, comm=comm)


def attn_merge(os_, ls_, o_ssm, name):
    _, T, HW = os_[0].shape
    tm = 512

    def body(o1, o2, o3, l1, l2, l3, s_ref, cat_ref, l_ref):
        a, b, c = l1[...], l2[...], l3[...]
        m = jnp.maximum(jnp.maximum(a, b), c)
        ea, eb, ec = jnp.exp(a - m), jnp.exp(b - m), jnp.exp(c - m)
        s = ea + eb + ec
        cat_ref[pl.ds(0, 2)] = (ea * o1[...] + eb * o2[...] + ec * o3[...]) * (1.0 / s)
        cat_ref[pl.ds(2, 2)] = s_ref[...]
        l_ref[...] = m + jnp.log(s)

    spec = pl.BlockSpec((2, tm, HW), lambda i: (0, i, 0))
    return pl.pallas_call(
        body, grid=(T // tm,), in_specs=[spec] * 7,
        out_specs=[pl.BlockSpec((4, tm, HW), lambda i: (0, i, 0)), spec],
        out_shape=[SDS((4, T, HW), F32), SDS((2, T, HW), F32)],
        compiler_params=_params(1), name=name)(*os_, *ls_, o_ssm)


def attn_bwd(proj, dcat, o, lse, acc, d, name, du=None):
    T = proj.shape[1]
    sb, groups = QBLK * d, ATTN_GROUP[d]
    rb = sb * groups
    nblk = T // rb
    has_acc = acc is not None
    n_parts = 3 if du is None else 4

    def body(*refs):
        (qc_ref, qn_ref, kc_ref, kp_ref, vc_ref, vp_ref, dc_ref, dn_ref, oc_ref, on_ref, lc_ref, ln_ref) = refs[:12]
        acc_ref = refs[12] if has_acc else None
        out_ref = refs[-1]
        if du is not None:
            out_ref[3] = refs[-2][...]
        p, n = pl.program_id(0), pl.program_id(1)
        cur_ok, prev_ok, dcur, dprev = _attn_masks()
        first_ok = jnp.logical_and(prev_ok, n > 0)
        last_ok = jnp.logical_and(prev_ok, n < nblk - 1)
        lane_head = lax.broadcasted_iota(jnp.int32, (QBLK, PAIR_W), 1) // HEAD_DIM
        slopes = _head_slopes(p, d)

        def one(g, r):
            rows = _rows(g * sb + r, d)
            q_c, do_c, o_c, l_c = qc_ref[rows, :], dc_ref[rows, :], oc_ref[rows, :], lc_ref[rows, :]
            k_c, v_c = kc_ref[rows, :].astype(BF16), vc_ref[rows, :].astype(BF16)
            if g == 0:
                prow, pok_c = _rows(r, d), first_ok
                k_p, v_p = kp_ref[prow, :].astype(BF16), vp_ref[prow, :].astype(BF16)
            else:
                prow, pok_c = _rows((g - 1) * sb + r, d), prev_ok
                k_p, v_p = kc_ref[prow, :].astype(BF16), vc_ref[prow, :].astype(BF16)
            if g == groups - 1:
                nrow, pok_n = _rows(r, d), last_ok
                q_n, do_n, o_n, l_n = qn_ref[nrow, :], dn_ref[nrow, :], on_ref[nrow, :], ln_ref[nrow, :]
            else:
                nrow, pok_n = _rows((g + 1) * sb + r, d), prev_ok
                q_n, do_n, o_n, l_n = qc_ref[nrow, :], dc_ref[nrow, :], oc_ref[nrow, :], lc_ref[nrow, :]
            heads = []
            for hq in range(2):
                hm = lane_head == hq
                qm_c = jnp.where(hm, q_c, 0.0).astype(BF16)
                qm_n = jnp.where(hm, q_n, 0.0).astype(BF16)
                dom_c = jnp.where(hm, do_c, 0.0)
                dom_n = jnp.where(hm, do_n, 0.0)
                dd_c = jnp.sum(dom_c * o_c, axis=1, keepdims=True)
                dd_n = jnp.sum(dom_n * o_n, axis=1, keepdims=True)
                ls_c = jnp.max(jnp.where(hm, l_c, NEG), axis=1, keepdims=True)
                ls_n = jnp.max(jnp.where(hm, l_n, NEG), axis=1, keepdims=True)
                dob_c, dob_n = dom_c.astype(BF16), dom_n.astype(BF16)
                mm = [(_dot_nt(qm_c, k_c), _dot_nt(dob_c, v_c)), (_dot_nt(qm_c, k_p), _dot_nt(dob_c, v_p)),
                      (_dot_nt(qm_n, k_c), _dot_nt(dob_n, v_c))]
                heads.append(dict(hq=hq, qm_c=qm_c, qm_n=qm_n, dob_c=dob_c, dob_n=dob_n, mm=mm,
                                  dd=(dd_c, dd_c, dd_n), ls=(ls_c, ls_c, ls_n)))
            return dict(rows=rows, k_c=k_c, k_p=k_p, heads=heads, oks=(cur_ok, pok_c, pok_n))

        def several(blocks):
            work = [one(g, r) for g, r in blocks]
            for w in work:
                for h in w["heads"]:
                    slope, dist = slopes[h["hq"]], (dcur, dprev, dprev)
                    h["pr"], h["ds"] = [], []
                    for j in range(3):
                        s = jnp.where(w["oks"][j], h["mm"][j][0] * 0.125 - slope * dist[j], NEG)
                        pr = jnp.exp(s - h["ls"][j])
                        h["pr"].append(pr.astype(BF16))
                        h["ds"].append((pr * (h["mm"][j][1] - h["dd"][j])).astype(BF16))
            for w in work:
                dq = jnp.zeros((QBLK, PAIR_W), F32)
                dk = jnp.zeros((QBLK, PAIR_W), F32)
                dv = jnp.zeros((QBLK, PAIR_W), F32)
                for h in w["heads"]:
                    ds, pr = h["ds"], h["pr"]
                    dq_h = _dot(ds[0], w["k_c"]) + _dot(ds[1], w["k_p"])
                    dk += (_dot_tn(ds[0], h["qm_c"]) + _dot_tn(ds[2], h["qm_n"])) * 0.125
                    dv += _dot_tn(pr[0], h["dob_c"]) + _dot_tn(pr[2], h["dob_n"])
                    dq = jnp.where(lane_head == h["hq"], dq_h * 0.125, dq)
                for part, val in enumerate((dq, dk, dv)):
                    if has_acc:
                        val = val + acc_ref.at[part][w["rows"], :]
                    out_ref.at[part][w["rows"], :] = val

        _for_query_blocks(d, groups, several)

    cur = lambda part: _pair_spec(rb, part, lambda n: n)
    prv = lambda part: _pair_spec(sb, part, lambda n: jnp.maximum(n * groups - 1, 0))
    nxt = lambda part: _pair_spec(sb, part, lambda n: jnp.minimum((n + 1) * groups, T // sb - 1))
    full = pl.BlockSpec((3, None, rb, PAIR_W), lambda p, n: (0, p // 2, n, p % 2))
    in_specs = [cur(0), nxt(0), cur(1), prv(1), cur(2), prv(2), cur(0), nxt(0), cur(0), nxt(0), cur(0), nxt(0)]
    args = [proj, proj, proj, proj, proj, proj, dcat, dcat, o, o, lse, lse]
    if has_acc:
        in_specs.append(full)
        args.append(acc)
    if du is not None:
        in_specs.append(cur(0))
        args.append(du)
    out_spec = pl.BlockSpec((n_parts, None, rb, PAIR_W), lambda p, n: (0, p // 2, n, p % 2))
    return pl.pallas_call(
        body, grid=(4, nblk), in_specs=in_specs, out_specs=out_spec,
        out_shape=SDS((n_parts, 2, T, 2 * PAIR_W), F32), compiler_params=_params(2), name=name)(*args)


def _scan_rows(buf, tab_ref, reverse, half):
    n_tiles = (buf.shape[0] - 8) // 8
    per_half = HALF_STATES // SCAN_CW
    row = lax.broadcasted_iota(jnp.int32, (8, SCAN_CW), 0)
    sgn = -1.0 if reverse else 1.0

    for j in range(per_half):
        c0 = half * 2 * HALF_STATES + j * SCAN_CW
        cre = pl.ds(c0, SCAN_CW)
        cim = pl.ds(c0 + HALF_STATES, SCAN_CW)
        steps = []
        for s, k in enumerate((1, 2, 4)):
            ok, shift = (row < 8 - k, 8 - k) if reverse else (row >= k, k)
            steps.append((shift, jnp.where(ok, tab_ref[pl.ds(s, 1), cre], 0.0),
                          jnp.where(ok, sgn * tab_ref[pl.ds(s, 1), cim], 0.0)))
        trow = 16 if reverse else 8
        pr, pi = tab_ref[pl.ds(trow, 8), cre], tab_ref[pl.ds(trow, 8), cim]
        for t in range(n_tiles):
            base = 8 * (n_tiles - 1 - t) if reverse else 8 + 8 * t
            rows = pl.ds(base, 8)
            re, im = buf[rows, cre], buf[rows, cim]
            for shift, ar, ai in steps:
                sre, sim = pltpu.roll(re, shift, 0), pltpu.roll(im, shift, 0)
                re, im = re + ar * sre - ai * sim, im + ar * sim + ai * sre
            crow = pl.ds(base + 8 if reverse else base - 1, 1)
            cr, ci = buf[crow, cre], buf[crow, cim]
            buf[rows, cre] = re + pr * cr - pi * ci
            buf[rows, cim] = im + pr * ci + pi * cr


def ssm_fwd(proj, bh, ch, apow, dskip, name, comm=None):
    _, T, C = proj.shape
    tm = SCAN_TM
    SW = 4 * HALF_STATES

    def body(u_ref, bh_ref, ch_ref, tab_ref, dsk_ref, y_ref, s_ref, buf):
        @pl.when(pl.program_id(0) == 0)
        def _():
            buf[pl.ds(0, 8), :] = jnp.zeros((8, SW), F32)

        for h in range(2):
            buf[pl.ds(8, tm), pl.ds(h * 2 * HALF_STATES, 2 * HALF_STATES)] = _dot(u_ref[h].astype(BF16), bh_ref[h])
        for h in range(2):
            cols = pl.ds(h * 2 * HALF_STATES, 2 * HALF_STATES)
            _scan_rows(buf, tab_ref, False, h)
            sv = buf[pl.ds(8, tm), cols]
            s_ref[:, cols] = sv
            y_ref[h] = _dot(sv.astype(BF16), ch_ref[h]) + dsk_ref[h] * u_ref[h]
        buf[pl.ds(0, 8), :] = buf[pl.ds(tm, 8), :]

    return _call(
        body, grid=(T // tm,),
        in_specs=[pl.BlockSpec((2, tm, C), lambda i: (3, i, 0)),
                  pl.BlockSpec((2, C, 2 * HALF_STATES), lambda i: (0, 0, 0)),
                  pl.BlockSpec((2, 2 * HALF_STATES, C), lambda i: (0, 0, 0)),
                  pl.BlockSpec((24, SW), lambda i: (0, 0)),
                  pl.BlockSpec((2, 1, C), lambda i: (0, 0, 0))],
        out_specs=[pl.BlockSpec((2, tm, C), lambda i: (0, i, 0)), pl.BlockSpec((tm, SW), lambda i: (i, 0))],
        out_shape=[SDS((2, T, C), F32), SDS((T, SW), F32)],
        scratch_shapes=[pltpu.VMEM((tm + 8, SW), F32)],
        args=(proj, bh, ch, apow, dskip), name=name, comm=comm)


def ssm_bwd(dy, proj, st, bh, ch, apow, dskip, name, comm=None):
    _, T, C = proj.shape
    tm = SCAN_TM
    nt = T // tm
    SW = 4 * HALF_STATES
    HS2 = 2 * HALF_STATES

    def body(dy_ref, u_ref, s_ref, sp_ref, bh_ref, ch_ref, tab_ref, dsk_ref,
             du_ref, da_ref, dbh_ref, dch_ref, dd_ref, lam):
        i = pl.program_id(0)

        @pl.when(i == 0)
        def _():
            lam[pl.ds(tm, 8), :] = jnp.zeros((8, SW), F32)
            da_ref[...] = jnp.zeros_like(da_ref)
            dbh_ref[...] = jnp.zeros_like(dbh_ref)
            dch_ref[...] = jnp.zeros_like(dch_ref)
            dd_ref[...] = jnp.zeros_like(dd_ref)

        for h in range(2):
            lam[pl.ds(0, tm), pl.ds(h * HS2, HS2)] = _dot_nt(dy_ref[h].astype(BF16), ch_ref[h])
        for h in range(2):
            dyv, uv = dy_ref[h], u_ref[h]
            dch_ref[h] += _dot_tn(s_ref[:, pl.ds(h * HS2, HS2)].astype(BF16), dyv.astype(BF16))
            dd_ref[h] += jnp.sum(dyv * uv, axis=0, keepdims=True)
        for h in range(2):
            _scan_rows(lam, tab_ref, True, h)
            lb = lam[pl.ds(0, tm), pl.ds(h * HS2, HS2)].astype(BF16)
            du_ref[h] = _dot_nt(lb, bh_ref[h]) + dsk_ref[h] * dy_ref[h]
            dbh_ref[h] += _dot_tn(u_ref[h].astype(BF16), lb)

        first = i == nt - 1
        per_half = HALF_STATES // SCAN_CW

        def chunk(j, _):
            c0 = pl.multiple_of((j // per_half) * HS2 + (j % per_half) * SCAN_CW, 128)
            cre, cim = pl.ds(c0, SCAN_CW), pl.ds(pl.multiple_of(c0 + HALF_STATES, 128), SCAN_CW)
            row = lax.broadcasted_iota(jnp.int32, (tm, SCAN_CW), 0)
            pre = jnp.where(first, 0.0, sp_ref[pl.ds(7, 1), cre])
            pim = jnp.where(first, 0.0, sp_ref[pl.ds(7, 1), cim])
            spr = jnp.where(row == 0, pre, pltpu.roll(s_ref[:, cre], 1, 0))
            spi = jnp.where(row == 0, pim, pltpu.roll(s_ref[:, cim], 1, 0))
            lr, li = lam[pl.ds(0, tm), cre], lam[pl.ds(0, tm), cim]
            da_ref[:, cre] += jnp.sum(lr * spr + li * spi, axis=0, keepdims=True)
            da_ref[:, cim] += jnp.sum(li * spr - lr * spi, axis=0, keepdims=True)
            return 0

        lax.fori_loop(0, 2 * per_half, chunk, 0)
        lam[pl.ds(tm, 8), :] = lam[pl.ds(0, 8), :]

    rev = lambda i: nt - 1 - i
    return _call(
        body, grid=(nt,),
        in_specs=[pl.BlockSpec((2, tm, C), lambda i: (0, rev(i), 0)),
                  pl.BlockSpec((2, tm, C), lambda i: (3, rev(i), 0)),
                  pl.BlockSpec((tm, SW), lambda i: (rev(i), 0)),
                  pl.BlockSpec((8, SW), lambda i: (jnp.maximum(rev(i) * (tm // 8) - 1, 0), 0)),
                  pl.BlockSpec((2, C, HS2), lambda i: (0, 0, 0)),
                  pl.BlockSpec((2, HS2, C), lambda i: (0, 0, 0)),
                  pl.BlockSpec((24, SW), lambda i: (0, 0)),
                  pl.BlockSpec((2, 1, C), lambda i: (0, 0, 0))],
        out_specs=[pl.BlockSpec((2, tm, C), lambda i: (0, rev(i), 0)),
                   pl.BlockSpec((1, SW), lambda i: (0, 0)),
                   pl.BlockSpec((2, C, HS2), lambda i: (0, 0, 0)),
                   pl.BlockSpec((2, HS2, C), lambda i: (0, 0, 0)),
                   pl.BlockSpec((2, 1, C), lambda i: (0, 0, 0))],
        out_shape=[SDS((2, T, C), F32), SDS((1, SW), F32), SDS((2, C, HS2), F32), SDS((2, HS2, C), F32),
                   SDS((2, 1, C), F32)],
        scratch_shapes=[pltpu.VMEM((tm + 8, SW), F32)],
        args=(dy, proj, st, st, bh, ch, apow, dskip), name=name, comm=comm)


_GELU_C = math.sqrt(2.0 / math.pi)


def _gelu(x):
    t = jnp.tanh(_GELU_C * (x + 0.044715 * x * x * x))
    return 0.5 * x * (1.0 + t), t


def glu_fwd(y, w, b, name):
    _, T, C = y.shape
    tm = 512

    def body(y_ref, w_ref, b_ref, o_ref, lg_ref):
        y0, _ = _gelu(y_ref[0])
        y1, _ = _gelu(y_ref[1])
        lg = _dot(y0.astype(BF16), w_ref[0]) + _dot(y1.astype(BF16), w_ref[1]) + b_ref[...]
        sg = _sigmoid(lg)
        o_ref[0] = y0 * sg[:, :C]
        o_ref[1] = y1 * sg[:, C:]
        lg_ref[0] = lg[:, :C]
        lg_ref[1] = lg[:, C:]

    return pl.pallas_call(
        body, grid=(T // tm,),
        in_specs=[pl.BlockSpec((2, tm, C), lambda i: (0, i, 0)), pl.BlockSpec((2, C, 2 * C), lambda i: (0, 0, 0)),
                  pl.BlockSpec((1, 2 * C), lambda i: (0, 0))],
        out_specs=[pl.BlockSpec((2, tm, C), lambda i: (0, i, 0)), pl.BlockSpec((2, tm, C), lambda i: (0, i, 0))],
        out_shape=[SDS((2, T, C), F32), SDS((2, T, C), F32)], compiler_params=_params(1), name=name)(y, w, b)


def glu_bwd(dcat, y, lg, w, name):
    _, T, C = y.shape
    tm = 512

    def body(d_ref, y_ref, lg_ref, w_ref, dy_ref, dw_ref, db_ref):
        @pl.when(pl.program_id(0) == 0)
        def _():
            dw_ref[...] = jnp.zeros_like(dw_ref)
            db_ref[...] = jnp.zeros_like(db_ref)

        y2, th, sg, dlg = [], [], [], []
        for h in range(2):
            yy, tt = _gelu(y_ref[h])
            ss = _sigmoid(lg_ref[h])
            y2.append(yy)
            th.append(tt)
            sg.append(ss)
            dlg.append(d_ref[h] * yy * ss * (1.0 - ss))
        dl = jnp.concatenate(dlg, axis=1)
        dlb = dl.astype(BF16)
        db_ref[...] += jnp.sum(dl, axis=0, keepdims=True)
        for h in range(2):
            dy2 = d_ref[h] * sg[h] + _dot_nt(dlb, w_ref[h])
            yv = y_ref[h]
            dgelu = 0.5 * (1.0 + th[h]) + 0.5 * yv * (1.0 - th[h] * th[h]) * _GELU_C * (1.0 + 3 * 0.044715 * yv * yv)
            dy_ref[h] = dy2 * dgelu
            dw_ref[h] += _dot_tn(y2[h].astype(BF16), dlb)

    return pl.pallas_call(
        body, grid=(T // tm,),
        in_specs=[pl.BlockSpec((2, tm, C), lambda i: (1, i, 0)), pl.BlockSpec((2, tm, C), lambda i: (0, i, 0)),
                  pl.BlockSpec((2, tm, C), lambda i: (0, i, 0)), pl.BlockSpec((2, C, 2 * C), lambda i: (0, 0, 0))],
        out_specs=[pl.BlockSpec((2, tm, C), lambda i: (0, i, 0)), pl.BlockSpec((2, C, 2 * C), lambda i: (0, 0, 0)),
                   pl.BlockSpec((1, 2 * C), lambda i: (0, 0))],
        out_shape=[SDS((2, T, C), F32), SDS((2, C, 2 * C), F32), SDS((1, 2 * C), F32)],
        compiler_params=_params(1), name=name)(dcat, y, lg, w)


def adamw(w, m, v, slots, name):
    R, C = w.shape
    tr = R
    for cand in (512, 256, 128, 64, 32, 16, 8):
        if R % cand == 0 and cand * C * 4 <= 2 * 1024 * 1024:
            tr = cand
            break
    c1 = 1.0 / (1.0 - ADAM_B1 ** ADAM_STEP)
    c2 = 1.0 / (1.0 - ADAM_B2 ** ADAM_STEP)

    def body(w_ref, m_ref, v_ref, s_ref, g_ref, d_ref, nm_ref, nv_ref):
        g = s_ref[0].astype(F32)
        for j in range(1, N_DEV):
            g = g + s_ref[j].astype(F32)
        nm = ADAM_B1 * m_ref[...] + (1.0 - ADAM_B1) * g
        nv = ADAM_B2 * v_ref[...] + (1.0 - ADAM_B2) * (g * g)
        g_ref[...] = g
        nm_ref[...] = nm
        nv_ref[...] = nv
        d_ref[...] = -ADAM_LR * ((nm * c1) / (jnp.sqrt(nv * c2) + ADAM_EPS) + ADAM_WD * w_ref[...])

    spec = pl.BlockSpec((tr, C), lambda i: (i, 0))
    return pl.pallas_call(
        body, grid=(R // tr,),
        in_specs=[spec, spec, spec, pl.BlockSpec((N_DEV, tr, C), lambda i: (0, i, 0))],
        out_specs=[spec] * 4, out_shape=[SDS((R, C), F32)] * 4, compiler_params=_params(1), name=name)(w, m, v, slots)


def _discretise(a_re, a_im, log_dt, b_re, b_im):
    dt = jnp.exp(log_dt)[:, None]
    e = jnp.exp(dt * a_re)
    ar, ai = e * jnp.cos(dt * a_im), e * jnp.sin(dt * a_im)
    den = a_re * a_re + a_im * a_im
    nr, ni = ar - 1.0, ai
    wr = (nr * a_re + ni * a_im) / den
    wi = (ni * a_re - nr * a_im) / den
    bbr = wr[..., None] * b_re - wi[..., None] * b_im
    bbi = wr[..., None] * b_im + wi[..., None] * b_re
    return ar, ai, bbr, bbi


def _block_diag(t):
    eye = jnp.eye(16, dtype=t.dtype).reshape(1, 16, 1, 16, 1)
    r, c = t.shape[1], t.shape[2]
    return (t.reshape(2, 16, r, 1, c) * eye).reshape(2, 16 * r, 16 * c)


def _diag_blocks(m, r, c):
    eye = jnp.eye(16, dtype=m.dtype).reshape(1, 16, 1, 16, 1)
    return jnp.sum(m.reshape(2, 16, r, 16, c) * eye, axis=3).reshape(32, r, c)


def _state_vec(re, im):
    return jnp.stack([re.reshape(2, HALF_STATES), im.reshape(2, HALF_STATES)], axis=1).reshape(-1)


BIG = ("ffn1_w_in", "ffn1_w_out", "w_mix_in", "w_glu", "w_mix_out", "ffn2_w_in", "ffn2_w_out")
WEIGHTS = ("ffn1_pre_g", "ffn1_w_in", "ffn1_w_out", "ffn1_post_g", "mix_pre_g", "w_mix_in", "a_re", "a_im", "log_dt",
           "b_re", "b_im", "c_re", "c_im", "d_skip", "w_glu", "b_glu", "w_mix_out", "mix_post_g", "ffn2_pre_g",
           "ffn2_w_in", "ffn2_w_out", "ffn2_post_g")
SMALL = tuple(n for n in WEIGHTS if n not in BIG)
TRANSPOSED = ("ffn1_w_in", "ffn2_w_in")
PACK_COLS = 1024


def _pack(parts):
    flat = jnp.concatenate([p.reshape(-1) for p in parts])
    rows = -(-flat.shape[0] // (8 * PACK_COLS)) * 8
    return jnp.pad(flat, (0, rows * PACK_COLS - flat.shape[0])).reshape(rows, PACK_COLS)


def _unpack(packed, shapes):
    flat, out, off = packed.reshape(-1), [], 0
    for s in shapes:
        n = math.prod(s)
        out.append(flat[off:off + n].reshape(s))
        off += n
    return out


def _gather(names, wb):
    return [wb[n] for n in names], [False] * len(names)


def _ffn_bwd(dy, do, saved, x, pre_g, w_in, w_out4, tag, post=None):
    h, z, a = saved
    T = x.shape[0]
    dz = ffn_dact(do, w_out4, z, f"{tag}_dact")
    dz8 = dz.reshape(8, T, dz.shape[-1])
    dw_out, _ = mm_tn(a, do, True, False, 4, f"{tag}_dwout")
    dw_in, (s_out,) = mm_tn(dz8, h, True, False, 8, f"{tag}_dwin", comm=([dw_out.reshape(8, -1, D_MODEL)], [True]))
    outs, (s_in,) = dh_pre_bwd(dz8, w_in, x, pre_g, dy, f"{tag}_dh", comm=([dw_in], [True]), post=post,
                               w_transposed=True)
    return outs, (s_in, s_out)


def local_step(x, tgt, sp, wb):
    T = x.shape[0]
    ar, ai, bbr, bbi = _discretise(sp["a_re"], sp["a_im"], sp["log_dt"], sp["b_re"], sp["b_im"])
    powers = [(ar, ai)]
    for _ in range(7):
        pr, pi = powers[-1]
        powers.append((pr * ar - pi * ai, pr * ai + pi * ar))
    zero = jnp.zeros_like(ar)
    rows = [_state_vec(*powers[k - 1]) for k in (1, 2, 4)] + [_state_vec(zero, zero)] * 5
    rows += [_state_vec(pr, pi) for pr, pi in powers]
    rows += [_state_vec(pr, -pi) for pr, pi in reversed(powers)]
    apow = jnp.stack(rows)
    bh = jnp.concatenate([_block_diag(bbr.transpose(0, 2, 1)), _block_diag(bbi.transpose(0, 2, 1))], axis=2)
    ch = jnp.concatenate([_block_diag(sp["c_re"].transpose(0, 2, 1)), _block_diag(-sp["c_im"].transpose(0, 2, 1))], axis=1)
    bh, ch = bh.astype(BF16), ch.astype(BF16)
    dskip = sp["d_skip"].reshape(2, 1, 256)

    w1_in = gather_two_level(wb["ffn1_w_in"], "gather_w1in")
    (h1, z1, a1), (w1_out, w_mi) = ffn_in(
        x, sp["ffn1_pre_g"], w1_in, "ffn1_in", comm=_gather(["ffn1_w_out", "w_mix_in"], wb))
    w1_out4 = w1_out.reshape(4, -1, D_MODEL)
    (o1, x1), (w_glu, w_mo) = mm_acc_norm(
        a1, w1_out4, x, sp["ffn1_post_g"], 0.5, "ffn1_out", comm=_gather(["w_glu", "w_mix_out"], wb))
    w_glu2, w_mo4 = w_glu.reshape(2, 256, 512), w_mo.reshape(4, 256, D_MODEL)
    h2, proj = norm_proj(x1, sp["mix_pre_g"], w_mi, "mix_proj")
    (y_ssm, states), (w2_in,) = ssm_fwd(proj, bh, ch, apow, dskip, "ssm_fwd", comm=_gather(["ffn2_w_in"], wb))
    os_, ls_ = [], []
    for d in DILATIONS:
        (o_d, l_d), got = attn_fwd(proj, d, f"attn_fwd_d{d}",
                                   comm=_gather(["ffn2_w_out"], wb) if d == DILATIONS[-1] else None)
        os_.append(o_d)
        ls_.append(l_d)
    w2_out4 = got[0].reshape(4, -1, D_MODEL)
    o_ssm, lg = glu_fwd(y_ssm, w_glu2, sp["b_glu"], "glu_fwd")
    cat, lse = attn_merge(os_, ls_, o_ssm, "attn_merge")
    (mixed, x2), _ = mm_acc_norm(cat, w_mo4, x1, sp["mix_post_g"], 1.0, "mix_out")
    (h3, z3, a3), _ = ffn_in(x2, sp["ffn2_pre_g"], w2_in, "ffn2_in")
    (dy3, sq, do3, dg_f2post), _ = mm_acc_norm(a3, w2_out4, x2, sp["ffn2_post_g"], 0.5, "ffn2_out", tgt=tgt)

    (dx2, dg_f2pre, dmixed, dg_mpost), (s_w2in, s_w2out) = _ffn_bwd(
        dy3, do3, (h3, z3, a3), x2, sp["ffn2_pre_g"], w2_in, w2_out4, "ffn2", post=(mixed, sp["mix_post_g"], 1.0))
    dcat = mm_nt_b(dmixed, w_mo4, "mix_dcat")
    dw_mo, _ = mm_tn(cat, dmixed, True, False, 4, "mix_dwout")
    dy_ssm, dw_glu, db_glu = glu_bwd(dcat, y_ssm, lg, w_glu2, "glu_bwd")
    (du, da, dbh, dch, dd), (s_wmo, s_wglu) = ssm_bwd(
        dy_ssm, proj, states, bh, ch, apow, dskip, "ssm_bwd",
        comm=([dw_mo.reshape(8, 128, D_MODEL), dw_glu.astype(BF16).reshape(8, 64, 512)], [True, True]))
    dqkv = None
    for d in DILATIONS:
        dqkv = attn_bwd(proj, dcat, cat, lse, dqkv, d, f"attn_bwd_d{d}", du=du if d == DILATIONS[-1] else None)
    dproj = dqkv.reshape(8, T, 256)
    dw_mi, _ = mm_tn(h2, dproj, False, True, 8, "mix_dwin")
    (dx1, dg_mpre, do1, dg_f1post), (s_wmi,) = dh_pre_bwd(
        dproj, w_mi, x1, sp["mix_pre_g"], dx2, "mix_dh", comm=([dw_mi], [True]), post=(o1, sp["ffn1_post_g"], 0.5))
    (dx0, dg_f1pre), (s_w1in, s_w1out) = _ffn_bwd(
        dx1, do1, (h1, z1, a1), x, sp["ffn1_pre_g"], w1_in, w1_out4, "ffn1")

    da4 = da.reshape(2, 2, HALF_STATES)
    d_ar, d_ai = da4[:, 0].reshape(32, N_STATE), da4[:, 1].reshape(32, N_STATE)
    d_bbr = _diag_blocks(dbh[:, :, :HALF_STATES], 16, N_STATE).transpose(0, 2, 1)
    d_bbi = _diag_blocks(dbh[:, :, HALF_STATES:], 16, N_STATE).transpose(0, 2, 1)
    _, disc_vjp = jax.vjp(_discretise, sp["a_re"], sp["a_im"], sp["log_dt"], sp["b_re"], sp["b_im"])
    g_are, g_aim, g_ldt, g_bre, g_bim = disc_vjp((d_ar, d_ai, d_bbr, d_bbi))
    g_cre = _diag_blocks(dch[:, :HALF_STATES], N_STATE, 16).transpose(0, 2, 1)
    g_cim = -_diag_blocks(dch[:, HALF_STATES:], N_STATE, 16).transpose(0, 2, 1)
    small = {
        "ffn1_pre_g": dg_f1pre, "ffn1_post_g": dg_f1post, "mix_pre_g": dg_mpre, "a_re": g_are, "a_im": g_aim,
        "log_dt": g_ldt, "b_re": g_bre, "b_im": g_bim, "c_re": g_cre, "c_im": g_cim, "d_skip": dd.reshape(1, 512),
        "b_glu": db_glu, "mix_post_g": dg_mpost, "ffn2_pre_g": dg_f2pre, "ffn2_post_g": dg_f2post,
    }
    small_slots = gather_two_level(_pack([small[n] for n in SMALL]), "exchange_small")
    big_slots = {"ffn1_w_in": s_w1in, "ffn1_w_out": s_w1out, "w_mix_in": s_wmi, "w_glu": s_wglu, "w_mix_out": s_wmo,
                 "ffn2_w_in": s_w2in, "ffn2_w_out": s_w2out}
    return sq, dx0, big_slots, small_slots


def kernel(x, ffn1_pre_g, ffn1_w_in, ffn1_w_out, ffn1_post_g, mix_pre_g, w_mix_in, a_re, a_im, log_dt, b_re, b_im, c_re, c_im, d_skip, w_glu, b_glu, w_mix_out, mix_post_g, ffn2_pre_g, ffn2_w_in, ffn2_w_out, ffn2_post_g, loss_target, m_ffn1_pre_g, m_ffn1_w_in, m_ffn1_w_out, m_ffn1_post_g, m_mix_pre_g, m_w_mix_in, m_a_re, m_a_im, m_log_dt, m_b_re, m_b_im, m_c_re, m_c_im, m_d_skip, m_w_glu, m_b_glu, m_w_mix_out, m_mix_post_g, m_ffn2_pre_g, m_ffn2_w_in, m_ffn2_w_out, m_ffn2_post_g, v_ffn1_pre_g, v_ffn1_w_in, v_ffn1_w_out, v_ffn1_post_g, v_mix_pre_g, v_w_mix_in, v_a_re, v_a_im, v_log_dt, v_b_re, v_b_im, v_c_re, v_c_im, v_d_skip, v_w_glu, v_b_glu, v_w_mix_out, v_mix_post_g, v_ffn2_pre_g, v_ffn2_w_in, v_ffn2_w_out, v_ffn2_post_g):
    args = dict(locals())
    w = {n: args[n][0] for n in WEIGHTS}
    m = {n: args["m_" + n][0] for n in WEIGHTS}
    v = {n: args["v_" + n][0] for n in WEIGHTS}

    for d in (w, m, v):
        for n in TRANSPOSED:
            d[n] = jnp.swapaxes(d[n], 0, 1)
    wb = {n: w[n].astype(BF16) for n in BIG}
    sp = {n: w[n] for n in SMALL}
    for n in ("ffn1_pre_g", "ffn1_post_g", "mix_pre_g", "mix_post_g", "ffn2_pre_g", "ffn2_post_g", "b_glu", "d_skip"):
        sp[n] = w[n].reshape(1, -1)

    sq, grad_x, big_slots, small_slots = local_step(x[0], loss_target[0], sp, wb)
    loss = lax.psum(0.5 / D_MODEL * jnp.sum(sq), ("x", "y", "c"))

    outs = {}
    for n in BIG:
        shp = w[n].shape
        r2 = lambda t: t.reshape(-1, shp[-1])
        res = adamw(r2(w[n]), r2(m[n]), r2(v[n]), big_slots[n].reshape(N_DEV, -1, shp[-1]), f"adamw_{n}")
        outs[n] = [(jnp.swapaxes(t, 0, 1) if n in TRANSPOSED else t.reshape(shp))[None] for t in res]
    res = adamw(_pack([w[n] for n in SMALL]), _pack([m[n] for n in SMALL]), _pack([v[n] for n in SMALL]),
                small_slots, "adamw_small")
    shapes = [(1,) + w[n].shape for n in SMALL]
    unpacked = [_unpack(t, shapes) for t in res]
    for j, n in enumerate(SMALL):
        outs[n] = [unpacked[k][j] for k in range(4)]

    result = [loss, grad_x[None]]
    for k in range(4):
        result += [outs[n][k] for n in WEIGHTS]
    return tuple(result)
```

```python
import functools
import math

import jax
import jax.numpy as jnp
from jax import lax
from jax.experimental import pallas as pl
from jax.experimental.pallas import tpu as pltpu

F32, BF16 = jnp.float32, jnp.bfloat16
SDS = jax.ShapeDtypeStruct

D_MODEL = 1024
N_DEV = 8
HEAD_DIM = 64
PAIR_W = 128
QBLK = 128
DILATIONS = (1, 4, 16)
N_STATE = 64
HALF_STATES = 1024
NORM_EPS = 1e-6
NEG = -1e30
VMEM_LIMIT = 56 * 1024 * 1024
ADAM_LR, ADAM_B1, ADAM_B2, ADAM_EPS, ADAM_WD, ADAM_STEP = 1e-3, 0.9, 0.999, 1e-8, 0.01, 10
SCAN_TM = 256
SCAN_CW = 512


def _params(n_grid):
    return pltpu.CompilerParams(dimension_semantics=("arbitrary",) * n_grid, vmem_limit_bytes=VMEM_LIMIT)


def _dot(a, b):
    return jnp.dot(a, b, preferred_element_type=F32)


def _dot_nt(a, b):
    return lax.dot_general(a, b, (((1,), (1,)), ((), ())), preferred_element_type=F32)


def _dot_tn(a, b):
    return lax.dot_general(a, b, (((0,), (0,)), ((), ())), preferred_element_type=F32)


def _sigmoid(v):
    return 0.5 * jnp.tanh(0.5 * v) + 0.5


def _resident(shape):
    return pl.BlockSpec(shape, lambda i: (0,) * len(shape), pipeline_mode=pl.Buffered(1))


ROW_SPLIT = 2


def _exchange_phase(ins, outs, scatter, sems, start):
    send_sems, recv_sems, loc_sems = sems
    x, y, c = lax.axis_index("x"), lax.axis_index("y"), lax.axis_index("c")
    me = 4 * x + 2 * y + c
    own_copies, sends, arrivals = [], [], []
    for i in range(len(ins)):
        own = ins[i].at[me] if scatter[i] else ins[i]
        own_copies.append(pltpu.make_async_copy(own, outs[i].at[me], loc_sems.at[i]))
        for k in range(1, N_DEV):
            px = 1 - x if k & 4 else x
            py = 1 - y if k & 2 else y
            pc = 1 - c if k & 1 else c
            peer = 4 * px + 2 * py + pc
            src = ins[i].at[peer] if scatter[i] else ins[i]
            common = dict(src_ref=src, send_sem=send_sems.at[i, k - 1], recv_sem=recv_sems.at[i, k - 1],
                          device_id=(px, py, pc), device_id_type=pl.DeviceIdType.MESH)
            sends.append(pltpu.make_async_remote_copy(dst_ref=outs[i].at[me], **common))
            if not start:
                arrivals.append(pltpu.make_async_remote_copy(dst_ref=outs[i].at[peer], **common))
    if start:
        for cp in own_copies + sends:
            cp.start()
    else:
        for cp in arrivals:
            cp.wait_recv()
        for cp in sends:
            cp.wait_send()
        for cp in own_copies:
            cp.wait()


def _comm_shapes(arrs, scatter):
    n = len(arrs)
    out_shapes = [SDS(a.shape if scatter[i] else (N_DEV,) + a.shape, a.dtype) for i, a in enumerate(arrs)]
    sems = [pltpu.SemaphoreType.DMA((n, N_DEV - 1)), pltpu.SemaphoreType.DMA((n, N_DEV - 1)),
            pltpu.SemaphoreType.DMA((n,))]
    return out_shapes, sems


def gather_two_level(arr, name):
    def body(x_ref, out_ref, send_sems, recv_sems, local_sem):
        x, y, c = lax.axis_index("x"), lax.axis_index("y"), lax.axis_index("c")
        sibling = (x, y, 1 - c)
        chips = [(1 - x, y), (x, 1 - y), (1 - x, 1 - y)]

        def slot(px, py, pc):
            return out_ref.at[4 * px + 2 * py + pc]

        def copy(k, block, to, src=None):
            return pltpu.make_async_remote_copy(
                src_ref=slot(*block) if src is None else src, dst_ref=slot(*block),
                send_sem=send_sems.at[k], recv_sem=recv_sems.at[k], device_id=to, device_id_type=pl.DeviceIdType.MESH)

        mine = pltpu.make_async_copy(x_ref, slot(x, y, c), local_sem)
        mine.start()
        first = [copy(0, (x, y, c), sibling, src=x_ref)]
        first += [copy(1 + j, (x, y, c), (*chip, c), src=x_ref) for j, chip in enumerate(chips)]
        for cp in first:
            cp.start()
        passed = [copy(4 + j, (*chip, c), sibling) for j, chip in enumerate(chips)]
        for j, chip in enumerate(chips):
            copy(1 + j, (*chip, c), (x, y, c)).wait_recv()
            passed[j].start()
        copy(0, sibling, (x, y, c)).wait_recv()
        for j, chip in enumerate(chips):
            copy(4 + j, (*chip, 1 - c), (x, y, c)).wait_recv()
        for cp in first + passed:
            cp.wait_send()
        mine.wait()

    anyspec = pl.BlockSpec(memory_space=pl.ANY)
    return pl.pallas_call(
        body, in_specs=[anyspec], out_specs=anyspec, out_shape=SDS((N_DEV,) + arr.shape, arr.dtype),
        scratch_shapes=[pltpu.SemaphoreType.DMA((N_DEV - 1,)), pltpu.SemaphoreType.DMA((N_DEV - 1,)),
                        pltpu.SemaphoreType.DMA],
        compiler_params=pltpu.CompilerParams(has_side_effects=True), name=name)(arr)


def _call(body, *, grid, in_specs, out_specs, out_shape, args, name, scratch_shapes=(), comm=None):
    n_grid, scratch_shapes = len(grid), list(scratch_shapes)
    if comm is None:
        outs = pl.pallas_call(body, grid=grid, in_specs=in_specs, out_specs=out_specs, out_shape=out_shape,
                              scratch_shapes=scratch_shapes, compiler_params=_params(n_grid), name=name)(*args)
        return outs, []
    arrs, scatter = comm
    nc, n_in, n_out, n_sc = len(arrs), len(in_specs), len(out_specs), len(scratch_shapes)
    comm_shapes, sems = _comm_shapes(arrs, scatter)

    def wrapped(*refs):
        ins, cins = refs[:n_in], refs[n_in:n_in + nc]
        o0 = n_in + nc
        outs, couts = refs[o0:o0 + n_out], refs[o0 + n_out:o0 + n_out + nc]
        s0 = o0 + n_out + nc
        scratch, sem_refs = refs[s0:s0 + n_sc], refs[s0 + n_sc:]
        first = functools.reduce(jnp.logical_and, [pl.program_id(k) == 0 for k in range(n_grid)])
        last = functools.reduce(jnp.logical_and, [pl.program_id(k) == grid[k] - 1 for k in range(n_grid)])

        @pl.when(first)
        def _():
            _exchange_phase(cins, couts, scatter, sem_refs, True)

        body(*ins, *outs, *scratch)

        @pl.when(last)
        def _():
            _exchange_phase(cins, couts, scatter, sem_refs, False)

    anyspec = pl.BlockSpec(memory_space=pl.ANY)
    res = pl.pallas_call(
        wrapped, grid=grid, in_specs=list(in_specs) + [anyspec] * nc, out_specs=list(out_specs) + [anyspec] * nc,
        out_shape=list(out_shape) + comm_shapes, scratch_shapes=scratch_shapes + sems,
        compiler_params=pltpu.CompilerParams(dimension_semantics=("arbitrary",) * n_grid,
                                             vmem_limit_bytes=VMEM_LIMIT, has_side_effects=True),
        name=name)(*args, *arrs)
    return res[:n_out], res[n_out:]


def _rms(xv, g):
    r = lax.rsqrt(jnp.mean(xv * xv, axis=-1, keepdims=True) + NORM_EPS)
    return (xv * r * g).astype(BF16)


def ffn_in(x, g, w, name, comm=None):
    T, D = x.shape
    F = w.shape[1]
    tm = 512

    def body(x_ref, g_ref, w_ref, h_ref, z_ref, a_ref):
        hv = _rms(x_ref[...], g_ref[...])
        h_ref[...] = hv
        pending = None
        for j in range(5):
            if j < 4:
                zs = (_dot_nt(hv, w_ref[j]), _dot_nt(hv, w_ref[j + 4]))
            if pending is not None:
                zg, zu = pending
                sg = _sigmoid(zg)
                silu = zg * sg
                z_ref[0, j - 1] = (zu * (sg + silu - silu * sg)).astype(BF16)
                z_ref[1, j - 1] = silu.astype(BF16)
                a_ref[j - 1] = (silu * zu).astype(BF16)
            pending = zs

    return _call(
        body, grid=(T // tm,),
        in_specs=[pl.BlockSpec((tm, D), lambda i: (i, 0)), pl.BlockSpec((1, D), lambda i: (0, 0)),
                  _resident((8, F, D))],
        out_specs=[pl.BlockSpec((tm, D), lambda i: (i, 0)), pl.BlockSpec((2, 4, tm, F), lambda i: (0, 0, i, 0)),
                   pl.BlockSpec((4, tm, F), lambda i: (0, i, 0))],
        out_shape=[SDS((T, D), BF16), SDS((2, 4, T, F), BF16), SDS((4, T, F), BF16)],
        args=(x, g, w), name=name, comm=comm)


def norm_proj(x, g, w, name):
    T, K = x.shape
    nb, _, N = w.shape
    tm = 512

    def body(x_ref, g_ref, w_ref, h_ref, o_ref):
        hv = _rms(x_ref[...], g_ref[...])
        h_ref[...] = hv
        for b in range(nb):
            o_ref[b] = _dot(hv, w_ref[b])

    return pl.pallas_call(
        body, grid=(T // tm,),
        in_specs=[pl.BlockSpec((tm, K), lambda i: (i, 0)), pl.BlockSpec((1, K), lambda i: (0, 0)),
                  _resident((nb, K, N))],
        out_specs=[pl.BlockSpec((tm, K), lambda i: (i, 0)), pl.BlockSpec((nb, tm, N), lambda i: (0, i, 0))],
        out_shape=[SDS((T, K), BF16), SDS((nb, T, N), F32)], compiler_params=_params(1), name=name)(x, g, w)


def mm_acc_norm(a, w, xres, g, scale, name, comm=None, tgt=None):
    nb, T, K = a.shape
    D = w.shape[2]
    tm = 512
    rc = tm // ROW_SPLIT
    with_loss = tgt is not None

    def body(a_ref, w_ref, x_ref, g_ref, *rest):
        if with_loss:
            t_ref, dy_ref, sq_ref, do_ref, dg_ref = rest

            @pl.when(pl.program_id(0) == 0)
            def _():
                sq_ref[...] = jnp.zeros_like(sq_ref)
                dg_ref[...] = jnp.zeros_like(dg_ref)
        else:
            o_ref, y_ref = rest
        accs = []
        for c in range(ROW_SPLIT):
            rows = pl.ds(c * rc, rc)
            o = _dot(a_ref[0, rows, :].astype(BF16), w_ref[0])
            for b in range(1, nb):
                o += _dot(a_ref[b, rows, :].astype(BF16), w_ref[b])
            accs.append(o)
        for c, o in enumerate(accs):
            rows = pl.ds(c * rc, rc)
            r = lax.rsqrt(jnp.mean(o * o, axis=-1, keepdims=True) + NORM_EPS)
            y = x_ref[rows, :] + scale * (o * r * g_ref[...])
            if with_loss:
                e = y - t_ref[rows, :]
                dy = e * (1.0 / D)
                dy_ref[rows, :] = dy
                sq_ref[...] += jnp.sum(e * e, axis=0, keepdims=True)
                do, dg = _post_bwd(dy, o, g_ref[...], scale)
                do_ref[rows, :] = do
                dg_ref[...] += dg
            else:
                o_ref[rows, :] = o
                y_ref[rows, :] = y

    tile = pl.BlockSpec((tm, D), lambda i: (i, 0))
    row = pl.BlockSpec((1, D), lambda i: (0, 0))
    in_specs = [pl.BlockSpec((nb, tm, K), lambda i: (0, i, 0)), _resident((nb, K, D)), tile, row]
    args = (a, w, xres, g)
    if with_loss:
        return _call(body, grid=(T // tm,), in_specs=in_specs + [tile], out_specs=[tile, row, tile, row],
                     out_shape=[SDS((T, D), F32), SDS((1, D), F32), SDS((T, D), BF16), SDS((1, D), F32)],
                     args=args + (tgt,), name=name, comm=comm)
    return _call(body, grid=(T // tm,), in_specs=in_specs, out_specs=[tile, tile],
                 out_shape=[SDS((T, D), F32), SDS((T, D), F32)], args=args, name=name, comm=comm)


def _post_bwd(dy, ov, g, scale):
    r = scale * dy
    rstd = lax.rsqrt(jnp.mean(ov * ov, axis=-1, keepdims=True) + NORM_EPS)
    oh = ov * rstd
    rg = r * g
    do = rstd * (rg - oh * jnp.mean(rg * oh, axis=-1, keepdims=True))
    return do.astype(BF16), jnp.sum(r * oh, axis=0, keepdims=True)


def mm_nt_b(gr, w, name):
    T, N = gr.shape
    nb, K, _ = w.shape
    tm = 512

    def body(g_ref, w_ref, o_ref):
        gv = g_ref[...]
        for b in range(nb):
            o_ref[b] = _dot_nt(gv, w_ref[b])

    return pl.pallas_call(
        body, grid=(T // tm,),
        in_specs=[pl.BlockSpec((tm, N), lambda i: (i, 0)), _resident((nb, K, N))],
        out_specs=pl.BlockSpec((nb, tm, K), lambda i: (0, i, 0)),
        out_shape=SDS((nb, T, K), F32), compiler_params=_params(1), name=name)(gr, w)


def ffn_dact(do, w_out, z, name):
    T, D = do.shape
    nb, F, _ = w_out.shape
    tm = 512

    rc = tm // ROW_SPLIT

    def body(g_ref, w_ref, z_ref, dz_ref):
        das = [_dot_nt(g_ref[pl.ds(c * rc, rc), :], w_ref[...]) for c in range(ROW_SPLIT)]
        for c, da in enumerate(das):
            rows = pl.ds(c * rc, rc)
            dz_ref[0, rows, :] = (da * z_ref[0, rows, :].astype(F32)).astype(BF16)
            dz_ref[1, rows, :] = (da * z_ref[1, rows, :].astype(F32)).astype(BF16)

    return pl.pallas_call(
        body, grid=(nb, T // tm),
        in_specs=[pl.BlockSpec((tm, D), lambda b, i: (i, 0)), pl.BlockSpec((None, F, D), lambda b, i: (b, 0, 0)),
                  pl.BlockSpec((2, None, tm, F), lambda b, i: (0, b, i, 0))],
        out_specs=pl.BlockSpec((2, None, tm, F), lambda b, i: (0, b, i, 0)),
        out_shape=SDS((2, nb, T, F), BF16), compiler_params=_params(2), name=name)(do, w_out, z)


def mm_tn(a, g, a_batched, g_batched, nb, name, comm=None):
    T = a.shape[-2]
    K, N = a.shape[-1], g.shape[-1]
    tk = 2048
    nk = T // tk

    def body(a_ref, g_ref, o_ref, acc):
        k = pl.program_id(1)

        @pl.when(k == 0)
        def _():
            acc[...] = jnp.zeros_like(acc)

        acc[...] += _dot_tn(a_ref[...].astype(BF16), g_ref[...].astype(BF16))

        @pl.when(k == nk - 1)
        def _():
            o_ref[...] = acc[...].astype(BF16)

    a_spec = (pl.BlockSpec((None, tk, K), lambda b, k: (b, k, 0)) if a_batched
              else pl.BlockSpec((tk, K), lambda b, k: (k, 0)))
    g_spec = (pl.BlockSpec((None, tk, N), lambda b, k: (b, k, 0)) if g_batched
              else pl.BlockSpec((tk, N), lambda b, k: (k, 0)))
    (out,), slots = _call(
        body, grid=(nb, nk), in_specs=[a_spec, g_spec],
        out_specs=[pl.BlockSpec((None, K, N), lambda b, k: (b, 0, 0))],
        out_shape=[SDS((nb, K, N), BF16)], scratch_shapes=[pltpu.VMEM((K, N), F32)],
        args=(a, g), name=name, comm=comm)
    return out, slots


def dh_pre_bwd(dz, w, x, g, dyres, name, comm=None, post=None, w_transposed=False):
    nb, T, F = dz.shape
    D = x.shape[1]
    tm = 512
    rc = tm // ROW_SPLIT
    mm = _dot if w_transposed else _dot_nt

    def body(dz_ref, w_ref, x_ref, g_ref, dy_ref, *rest):
        if post is None:
            dx_ref, dg_ref = rest
        else:
            o_ref, gp_ref, dx_ref, dg_ref, do_ref, dgp_ref = rest

        @pl.when(pl.program_id(0) == 0)
        def _():
            dg_ref[...] = jnp.zeros_like(dg_ref)
            if post is not None:
                dgp_ref[...] = jnp.zeros_like(dgp_ref)

        accs = []
        for c in range(ROW_SPLIT):
            rows = pl.ds(c * rc, rc)
            dh = mm(dz_ref[0, rows, :].astype(BF16), w_ref[0])
            for b in range(1, nb):
                dh += mm(dz_ref[b, rows, :].astype(BF16), w_ref[b])
            accs.append(dh)
        for c, dh in enumerate(accs):
            rows = pl.ds(c * rc, rc)
            xv = x_ref[rows, :]
            rstd = lax.rsqrt(jnp.mean(xv * xv, axis=-1, keepdims=True) + NORM_EPS)
            xh = xv * rstd
            dg_ref[...] += jnp.sum(dh * xh, axis=0, keepdims=True)
            dhg = dh * g_ref[...]
            dx = dy_ref[rows, :] + rstd * (dhg - xh * jnp.mean(dhg * xh, axis=-1, keepdims=True))
            dx_ref[rows, :] = dx
            if post is not None:
                do, dgp = _post_bwd(dx, o_ref[rows, :], gp_ref[...], post[2])
                do_ref[rows, :] = do
                dgp_ref[...] += dgp

    tile = pl.BlockSpec((tm, D), lambda i: (i, 0))
    row = pl.BlockSpec((1, D), lambda i: (0, 0))
    in_specs = [pl.BlockSpec((nb, tm, F), lambda i: (0, i, 0)), _resident(w.shape), tile, row, tile]
    out_specs, out_shape, args = [tile, row], [SDS((T, D), F32), SDS((1, D), F32)], (dz, w, x, g, dyres)
    if post is not None:
        in_specs += [tile, row]
        out_specs += [tile, row]
        out_shape += [SDS((T, D), BF16), SDS((1, D), F32)]
        args += (post[0], post[1])
    return _call(body, grid=(T // tm,), in_specs=in_specs, out_specs=out_specs, out_shape=out_shape,
                 args=args, name=name, comm=comm)


ATTN_GROUP = {1: 4, 4: 2, 16: 1}
ATTN_UNROLL = 4


def _attn_masks():
    qi = lax.broadcasted_iota(jnp.int32, (QBLK, QBLK), 0)
    kj = lax.broadcasted_iota(jnp.int32, (QBLK, QBLK), 1)
    cur_ok = kj <= qi
    prev_ok = kj >= qi
    dcur = (qi - kj).astype(F32)
    return cur_ok, prev_ok, dcur, dcur + float(QBLK)


def _head_slopes(p, d):
    out = []
    for hq in range(2):
        v = [float(d) * 2.0 ** -(2 * q + hq + 1) for q in range(4)]
        out.append(jnp.where(p == 0, v[0], jnp.where(p == 1, v[1], jnp.where(p == 2, v[2], v[3]))))
    return out


def _rows(start, d):
    return pl.ds(start, QBLK, stride=d) if d > 1 else pl.ds(start, QBLK)


def _pair_spec(rows, part, blk):
    return pl.BlockSpec((None, rows, PAIR_W), lambda p, n: (2 * part + p // 2, blk(n), p % 2))


def _for_query_blocks(d, groups, several):
    blocks = [(g, r) for r in range(d) for g in range(groups)]
    for s in range(0, len(blocks), ATTN_UNROLL):
        several(blocks[s:s + ATTN_UNROLL])


def attn_fwd(proj, d, name, comm=None):
    T = proj.shape[1]
    sb, groups = QBLK * d, ATTN_GROUP[d]
    rb = sb * groups
    nblk = T // rb

    def body(q_ref, kc_ref, kp_ref, vc_ref, vp_ref, o_ref, l_ref):
        p, n = pl.program_id(0), pl.program_id(1)
        cur_ok, prev_ok, dcur, dprev = _attn_masks()
        first_ok = jnp.logical_and(prev_ok, n > 0)
        lane_head = lax.broadcasted_iota(jnp.int32, (QBLK, PAIR_W), 1) // HEAD_DIM
        slopes = _head_slopes(p, d)

        def several(blocks):
            work = []
            for g, r in blocks:
                rows = _rows(g * sb + r, d)
                q = q_ref[rows, :]
                kc, vc = kc_ref[rows, :].astype(BF16), vc_ref[rows, :].astype(BF16)
                if g == 0:
                    prow, pok = _rows(r, d), first_ok
                    kp, vp = kp_ref[prow, :].astype(BF16), vp_ref[prow, :].astype(BF16)
                else:
                    prow, pok = _rows((g - 1) * sb + r, d), prev_ok
                    kp, vp = kc_ref[prow, :].astype(BF16), vc_ref[prow, :].astype(BF16)
                for hq in range(2):
                    qm = jnp.where(lane_head == hq, q, 0.0).astype(BF16)
                    work.append([rows, hq, pok, vc, vp, _dot_nt(qm, kc), _dot_nt(qm, kp)])
            for w in work:
                _, hq, pok, _, _, sc, sp = w
                sc = jnp.where(cur_ok, sc * 0.125 - slopes[hq] * dcur, NEG)
                sp = jnp.where(pok, sp * 0.125 - slopes[hq] * dprev, NEG)
                m = jnp.maximum(jnp.max(sc, axis=1, keepdims=True), jnp.max(sp, axis=1, keepdims=True))
                pc = jnp.exp(sc - m)
                pp = jnp.exp(sp - m)
                den = jnp.sum(pc, axis=1, keepdims=True) + jnp.sum(pp, axis=1, keepdims=True)
                w[5:] = [pc.astype(BF16), pp.astype(BF16), 1.0 / den, m + jnp.log(den)]
            for i in range(0, len(work), 2):
                o_acc = jnp.zeros((QBLK, PAIR_W), F32)
                l_acc = jnp.zeros((QBLK, PAIR_W), F32)
                for rows, hq, _, vc, vp, pc, pp, inv, lse in work[i:i + 2]:
                    hm = lane_head == hq
                    o_acc = jnp.where(hm, (_dot(pc, vc) + _dot(pp, vp)) * inv, o_acc)
                    l_acc = jnp.where(hm, lse, l_acc)
                o_ref[rows, :] = o_acc
                l_ref[rows, :] = l_acc

        _for_query_blocks(d, groups, several)

    cur = lambda part: _pair_spec(rb, part, lambda n: n)
    prv = lambda part: _pair_spec(sb, part, lambda n: jnp.maximum(n * groups - 1, 0))
    return _call(
        body, grid=(4, nblk), in_specs=[cur(0), cur(1), prv(1), cur(2), prv(2)], out_specs=[cur(0), cur(0)],
        out_shape=[SDS((2, T, 2 * PAIR_W), F32), SDS((2, T, 2 * PAIR_W), F32)],
        args=(proj, proj, proj, proj, proj), name=name, comm=comm)


def mix_out(os_, ls_, o_ssm, w, xres, g, name):
    _, T, HW = o_ssm.shape
    D = w.shape[2]
    tm = 512

    def body(o1, o2, o3, l1, l2, l3, s_ref, w_ref, x_ref, g_ref, cat_ref, l_ref, m_ref, y_ref):
        a, b, c = l1[...], l2[...], l3[...]
        m = jnp.maximum(jnp.maximum(a, b), c)
        ea, eb, ec = jnp.exp(a - m), jnp.exp(b - m), jnp.exp(c - m)
        s = ea + eb + ec
        att = (ea * o1[...] + eb * o2[...] + ec * o3[...]) * (1.0 / s)
        ssm = s_ref[...]
        cat_ref[pl.ds(0, 2)] = att
        cat_ref[pl.ds(2, 2)] = ssm
        l_ref[...] = m + jnp.log(s)
        o = _dot(att[0].astype(BF16), w_ref[0]) + _dot(att[1].astype(BF16), w_ref[1])
        o += _dot(ssm[0].astype(BF16), w_ref[2]) + _dot(ssm[1].astype(BF16), w_ref[3])
        r = lax.rsqrt(jnp.mean(o * o, axis=-1, keepdims=True) + NORM_EPS)
        m_ref[...] = o
        y_ref[...] = x_ref[...] + o * r * g_ref[...]

    spec = pl.BlockSpec((2, tm, HW), lambda i: (0, i, 0))
    tile = pl.BlockSpec((tm, D), lambda i: (i, 0))
    return pl.pallas_call(
        body, grid=(T // tm,),
        in_specs=[spec] * 7 + [_resident(w.shape), tile, pl.BlockSpec((1, D), lambda i: (0, 0))],
        out_specs=[pl.BlockSpec((4, tm, HW), lambda i: (0, i, 0)), spec, tile, tile],
        out_shape=[SDS((4, T, HW), F32), SDS((2, T, HW), F32), SDS((T, D), F32), SDS((T, D), F32)],
        compiler_params=_params(1), name=name)(*os_, *ls_, o_ssm, w, xres, g)


def attn_bwd(proj, dcat, o, lse, acc, d, name, du=None):
    T = proj.shape[1]
    sb, groups = QBLK * d, ATTN_GROUP[d]
    rb = sb * groups
    nblk = T // rb
    has_acc = acc is not None
    n_parts = 3 if du is None else 4

    def body(*refs):
        (qc_ref, qn_ref, kc_ref, kp_ref, vc_ref, vp_ref, dc_ref, dn_ref, oc_ref, on_ref, lc_ref, ln_ref) = refs[:12]
        acc_ref = refs[12] if has_acc else None
        out_ref = refs[-1]
        if du is not None:
            out_ref[3] = refs[-2][...]
        p, n = pl.program_id(0), pl.program_id(1)
        cur_ok, prev_ok, dcur, dprev = _attn_masks()
        first_ok = jnp.logical_and(prev_ok, n > 0)
        last_ok = jnp.logical_and(prev_ok, n < nblk - 1)
        lane_head = lax.broadcasted_iota(jnp.int32, (QBLK, PAIR_W), 1) // HEAD_DIM
        slopes = _head_slopes(p, d)

        def one(g, r, shared):
            rows = _rows(g * sb + r, d)
            q_c, do_c, o_c, l_c = qc_ref[rows, :], dc_ref[rows, :], oc_ref[rows, :], lc_ref[rows, :]
            k_c, v_c = kc_ref[rows, :].astype(BF16), vc_ref[rows, :].astype(BF16)
            if shared:
                pok_c, k_p, v_p = prev_ok, None, None
            elif g == 0:
                prow, pok_c = _rows(r, d), first_ok
                k_p, v_p = kp_ref[prow, :].astype(BF16), vp_ref[prow, :].astype(BF16)
            else:
                prow, pok_c = _rows((g - 1) * sb + r, d), prev_ok
                k_p, v_p = kc_ref[prow, :].astype(BF16), vc_ref[prow, :].astype(BF16)
            if g == groups - 1:
                nrow, pok_n = _rows(r, d), last_ok
                q_n, do_n, o_n, l_n = qn_ref[nrow, :], dn_ref[nrow, :], on_ref[nrow, :], ln_ref[nrow, :]
            else:
                nrow, pok_n = _rows((g + 1) * sb + r, d), prev_ok
                q_n, do_n, o_n, l_n = qc_ref[nrow, :], dc_ref[nrow, :], oc_ref[nrow, :], lc_ref[nrow, :]
            heads = []
            for hq in range(2):
                hm = lane_head == hq
                qm_c = jnp.where(hm, q_c, 0.0).astype(BF16)
                qm_n = jnp.where(hm, q_n, 0.0).astype(BF16)
                dom_c = jnp.where(hm, do_c, 0.0)
                dom_n = jnp.where(hm, do_n, 0.0)
                dd_c = jnp.sum(dom_c * o_c, axis=1, keepdims=True)
                dd_n = jnp.sum(dom_n * o_n, axis=1, keepdims=True)
                ls_c = jnp.max(jnp.where(hm, l_c, NEG), axis=1, keepdims=True)
                ls_n = jnp.max(jnp.where(hm, l_n, NEG), axis=1, keepdims=True)
                dob_c, dob_n = dom_c.astype(BF16), dom_n.astype(BF16)
                mm = [(_dot_nt(qm_c, k_c), _dot_nt(dob_c, v_c)),
                      None if shared else (_dot_nt(qm_c, k_p), _dot_nt(dob_c, v_p)),
                      (_dot_nt(qm_n, k_c), _dot_nt(dob_n, v_c))]
                heads.append(dict(hq=hq, qm_c=qm_c, qm_n=qm_n, dob_c=dob_c, dob_n=dob_n, mm=mm,
                                  dd=(dd_c, dd_c, dd_n), ls=(ls_c, ls_c, ls_n)))
            return dict(rows=rows, k_c=k_c, k_p=k_p, heads=heads, oks=(cur_ok, pok_c, pok_n), shared=shared)

        def several(blocks):
            work = []
            for i, (g, r) in enumerate(blocks):
                work.append(one(g, r, i > 0 and blocks[i - 1] == (g - 1, r)))
            for i, w in enumerate(work):
                if w["shared"]:
                    w["k_p"] = work[i - 1]["k_c"]
                for hi, h in enumerate(w["heads"]):
                    slope, dist = slopes[h["hq"]], (dcur, dprev, dprev)
                    h["pr"], h["ds"] = [], []
                    for j in range(3):
                        if h["mm"][j] is None:
                            h["pr"].append(work[i - 1]["heads"][hi]["pr"][2])
                            h["ds"].append(work[i - 1]["heads"][hi]["ds"][2])
                            continue
                        s = jnp.where(w["oks"][j], h["mm"][j][0] * 0.125 - slope * dist[j], NEG)
                        pr = jnp.exp(s - h["ls"][j])
                        h["pr"].append(pr.astype(BF16))
                        h["ds"].append((pr * (h["mm"][j][1] - h["dd"][j])).astype(BF16))
            for w in work:
                dq = jnp.zeros((QBLK, PAIR_W), F32)
                dk = jnp.zeros((QBLK, PAIR_W), F32)
                dv = jnp.zeros((QBLK, PAIR_W), F32)
                for h in w["heads"]:
                    ds, pr = h["ds"], h["pr"]
                    dq_h = _dot(ds[0], w["k_c"]) + _dot(ds[1], w["k_p"])
                    dk += (_dot_tn(ds[0], h["qm_c"]) + _dot_tn(ds[2], h["qm_n"])) * 0.125
                    dv += _dot_tn(pr[0], h["dob_c"]) + _dot_tn(pr[2], h["dob_n"])
                    dq = jnp.where(lane_head == h["hq"], dq_h * 0.125, dq)
                for part, val in enumerate((dq, dk, dv)):
                    if has_acc:
                        val = val + acc_ref.at[part][w["rows"], :]
                    out_ref.at[part][w["rows"], :] = val

        _for_query_blocks(d, groups, several)

    cur = lambda part: _pair_spec(rb, part, lambda n: n)
    prv = lambda part: _pair_spec(sb, part, lambda n: jnp.maximum(n * groups - 1, 0))
    nxt = lambda part: _pair_spec(sb, part, lambda n: jnp.minimum((n + 1) * groups, T // sb - 1))
    full = pl.BlockSpec((3, None, rb, PAIR_W), lambda p, n: (0, p // 2, n, p % 2))
    in_specs = [cur(0), nxt(0), cur(1), prv(1), cur(2), prv(2), cur(0), nxt(0), cur(0), nxt(0), cur(0), nxt(0)]
    args = [proj, proj, proj, proj, proj, proj, dcat, dcat, o, o, lse, lse]
    if has_acc:
        in_specs.append(full)
        args.append(acc)
    if du is not None:
        in_specs.append(cur(0))
        args.append(du)
    out_spec = pl.BlockSpec((n_parts, None, rb, PAIR_W), lambda p, n: (0, p // 2, n, p % 2))
    return pl.pallas_call(
        body, grid=(4, nblk), in_specs=in_specs, out_specs=out_spec,
        out_shape=SDS((n_parts, 2, T, 2 * PAIR_W), F32), compiler_params=_params(2), name=name)(*args)


def _scan_rows(buf, tab_ref, reverse, half):
    n_tiles = (buf.shape[0] - 8) // 8
    per_half = HALF_STATES // SCAN_CW
    row = lax.broadcasted_iota(jnp.int32, (8, SCAN_CW), 0)
    sgn = -1.0 if reverse else 1.0

    for j in range(per_half):
        c0 = half * 2 * HALF_STATES + j * SCAN_CW
        cre = pl.ds(c0, SCAN_CW)
        cim = pl.ds(c0 + HALF_STATES, SCAN_CW)
        steps = []
        for s, k in enumerate((1, 2, 4)):
            ok, shift = (row < 8 - k, 8 - k) if reverse else (row >= k, k)
            steps.append((shift, jnp.where(ok, tab_ref[pl.ds(s, 1), cre], 0.0),
                          jnp.where(ok, sgn * tab_ref[pl.ds(s, 1), cim], 0.0)))
        trow = 16 if reverse else 8
        pr, pi = tab_ref[pl.ds(trow, 8), cre], tab_ref[pl.ds(trow, 8), cim]
        for t in range(n_tiles):
            base = 8 * (n_tiles - 1 - t) if reverse else 8 + 8 * t
            rows = pl.ds(base, 8)
            re, im = buf[rows, cre], buf[rows, cim]
            for shift, ar, ai in steps:
                sre, sim = pltpu.roll(re, shift, 0), pltpu.roll(im, shift, 0)
                re, im = re + ar * sre - ai * sim, im + ar * sim + ai * sre
            crow = pl.ds(base + 8 if reverse else base - 1, 1)
            cr, ci = buf[crow, cre], buf[crow, cim]
            buf[rows, cre] = re + pr * cr - pi * ci
            buf[rows, cim] = im + pr * ci + pi * cr


def ssm_fwd(proj, bh, ch, apow, dskip, name, comm=None):
    _, T, C = proj.shape
    tm = SCAN_TM
    SW = 4 * HALF_STATES

    def body(u_ref, bh_ref, ch_ref, tab_ref, dsk_ref, y_ref, s_ref, buf):
        @pl.when(pl.program_id(0) == 0)
        def _():
            buf[pl.ds(0, 8), :] = jnp.zeros((8, SW), F32)

        for h in range(2):
            buf[pl.ds(8, tm), pl.ds(h * 2 * HALF_STATES, 2 * HALF_STATES)] = _dot(u_ref[h].astype(BF16), bh_ref[h])
        for h in range(2):
            cols = pl.ds(h * 2 * HALF_STATES, 2 * HALF_STATES)
            _scan_rows(buf, tab_ref, False, h)
            sv = buf[pl.ds(8, tm), cols]
            s_ref[:, cols] = sv
            y_ref[h] = _dot(sv.astype(BF16), ch_ref[h]) + dsk_ref[h] * u_ref[h]
        buf[pl.ds(0, 8), :] = buf[pl.ds(tm, 8), :]

    return _call(
        body, grid=(T // tm,),
        in_specs=[pl.BlockSpec((2, tm, C), lambda i: (3, i, 0)),
                  pl.BlockSpec((2, C, 2 * HALF_STATES), lambda i: (0, 0, 0)),
                  pl.BlockSpec((2, 2 * HALF_STATES, C), lambda i: (0, 0, 0)),
                  pl.BlockSpec((24, SW), lambda i: (0, 0)),
                  pl.BlockSpec((2, 1, C), lambda i: (0, 0, 0))],
        out_specs=[pl.BlockSpec((2, tm, C), lambda i: (0, i, 0)), pl.BlockSpec((tm, SW), lambda i: (i, 0))],
        out_shape=[SDS((2, T, C), F32), SDS((T, SW), F32)],
        scratch_shapes=[pltpu.VMEM((tm + 8, SW), F32)],
        args=(proj, bh, ch, apow, dskip), name=name, comm=comm)


def ssm_bwd(dy, proj, st, bh, ch, apow, dskip, name, comm=None):
    _, T, C = proj.shape
    tm = SCAN_TM
    nt = T // tm
    SW = 4 * HALF_STATES
    HS2 = 2 * HALF_STATES

    def body(dy_ref, u_ref, s_ref, sp_ref, bh_ref, ch_ref, tab_ref, dsk_ref,
             du_ref, da_ref, dbh_ref, dch_ref, dd_ref, lam):
        i = pl.program_id(0)

        @pl.when(i == 0)
        def _():
            lam[pl.ds(tm, 8), :] = jnp.zeros((8, SW), F32)
            da_ref[...] = jnp.zeros_like(da_ref)
            dbh_ref[...] = jnp.zeros_like(dbh_ref)
            dch_ref[...] = jnp.zeros_like(dch_ref)
            dd_ref[...] = jnp.zeros_like(dd_ref)

        for h in range(2):
            lam[pl.ds(0, tm), pl.ds(h * HS2, HS2)] = _dot_nt(dy_ref[h].astype(BF16), ch_ref[h])
        for h in range(2):
            dyv, uv = dy_ref[h], u_ref[h]
            dch_ref[h] += _dot_tn(s_ref[:, pl.ds(h * HS2, HS2)].astype(BF16), dyv.astype(BF16))
            dd_ref[h] += jnp.sum(dyv * uv, axis=0, keepdims=True)
        for h in range(2):
            _scan_rows(lam, tab_ref, True, h)
            lb = lam[pl.ds(0, tm), pl.ds(h * HS2, HS2)].astype(BF16)
            du_ref[h] = _dot_nt(lb, bh_ref[h]) + dsk_ref[h] * dy_ref[h]
            dbh_ref[h] += _dot_tn(u_ref[h].astype(BF16), lb)

        first = i == nt - 1
        per_half = HALF_STATES // SCAN_CW

        def chunk(j, _):
            c0 = pl.multiple_of((j // per_half) * HS2 + (j % per_half) * SCAN_CW, 128)
            cre, cim = pl.ds(c0, SCAN_CW), pl.ds(pl.multiple_of(c0 + HALF_STATES, 128), SCAN_CW)
            row = lax.broadcasted_iota(jnp.int32, (tm, SCAN_CW), 0)
            pre = jnp.where(first, 0.0, sp_ref[pl.ds(7, 1), cre])
            pim = jnp.where(first, 0.0, sp_ref[pl.ds(7, 1), cim])
            spr = jnp.where(row == 0, pre, pltpu.roll(s_ref[:, cre], 1, 0))
            spi = jnp.where(row == 0, pim, pltpu.roll(s_ref[:, cim], 1, 0))
            lr, li = lam[pl.ds(0, tm), cre], lam[pl.ds(0, tm), cim]
            da_ref[:, cre] += jnp.sum(lr * spr + li * spi, axis=0, keepdims=True)
            da_ref[:, cim] += jnp.sum(li * spr - lr * spi, axis=0, keepdims=True)
            return 0

        lax.fori_loop(0, 2 * per_half, chunk, 0)
        lam[pl.ds(tm, 8), :] = lam[pl.ds(0, 8), :]

    rev = lambda i: nt - 1 - i
    return _call(
        body, grid=(nt,),
        in_specs=[pl.BlockSpec((2, tm, C), lambda i: (0, rev(i), 0)),
                  pl.BlockSpec((2, tm, C), lambda i: (3, rev(i), 0)),
                  pl.BlockSpec((tm, SW), lambda i: (rev(i), 0)),
                  pl.BlockSpec((8, SW), lambda i: (jnp.maximum(rev(i) * (tm // 8) - 1, 0), 0)),
                  pl.BlockSpec((2, C, HS2), lambda i: (0, 0, 0)),
                  pl.BlockSpec((2, HS2, C), lambda i: (0, 0, 0)),
                  pl.BlockSpec((24, SW), lambda i: (0, 0)),
                  pl.BlockSpec((2, 1, C), lambda i: (0, 0, 0))],
        out_specs=[pl.BlockSpec((2, tm, C), lambda i: (0, rev(i), 0)),
                   pl.BlockSpec((1, SW), lambda i: (0, 0)),
                   pl.BlockSpec((2, C, HS2), lambda i: (0, 0, 0)),
                   pl.BlockSpec((2, HS2, C), lambda i: (0, 0, 0)),
                   pl.BlockSpec((2, 1, C), lambda i: (0, 0, 0))],
        out_shape=[SDS((2, T, C), F32), SDS((1, SW), F32), SDS((2, C, HS2), F32), SDS((2, HS2, C), F32),
                   SDS((2, 1, C), F32)],
        scratch_shapes=[pltpu.VMEM((tm + 8, SW), F32)],
        args=(dy, proj, st, st, bh, ch, apow, dskip), name=name, comm=comm)


_GELU_C = math.sqrt(2.0 / math.pi)


def _gelu(x):
    t = jnp.tanh(_GELU_C * (x + 0.044715 * x * x * x))
    return 0.5 * x * (1.0 + t), t


def glu_fwd(y, w, b, name):
    _, T, C = y.shape
    tm = 512

    def body(y_ref, w_ref, b_ref, o_ref, lg_ref):
        y0, _ = _gelu(y_ref[0])
        y1, _ = _gelu(y_ref[1])
        lg = _dot(y0.astype(BF16), w_ref[0]) + _dot(y1.astype(BF16), w_ref[1]) + b_ref[...]
        sg = _sigmoid(lg)
        o_ref[0] = y0 * sg[:, :C]
        o_ref[1] = y1 * sg[:, C:]
        lg_ref[0] = lg[:, :C]
        lg_ref[1] = lg[:, C:]

    return pl.pallas_call(
        body, grid=(T // tm,),
        in_specs=[pl.BlockSpec((2, tm, C), lambda i: (0, i, 0)), pl.BlockSpec((2, C, 2 * C), lambda i: (0, 0, 0)),
                  pl.BlockSpec((1, 2 * C), lambda i: (0, 0))],
        out_specs=[pl.BlockSpec((2, tm, C), lambda i: (0, i, 0)), pl.BlockSpec((2, tm, C), lambda i: (0, i, 0))],
        out_shape=[SDS((2, T, C), F32), SDS((2, T, C), F32)], compiler_params=_params(1), name=name)(y, w, b)


def glu_bwd(dcat, y, lg, w, name):
    _, T, C = y.shape
    tm = 512

    def body(d_ref, y_ref, lg_ref, w_ref, dy_ref, dw_ref, db_ref):
        @pl.when(pl.program_id(0) == 0)
        def _():
            dw_ref[...] = jnp.zeros_like(dw_ref)
            db_ref[...] = jnp.zeros_like(db_ref)

        y2, th, sg, dlg = [], [], [], []
        for h in range(2):
            yy, tt = _gelu(y_ref[h])
            ss = _sigmoid(lg_ref[h])
            y2.append(yy)
            th.append(tt)
            sg.append(ss)
            dlg.append(d_ref[h] * yy * ss * (1.0 - ss))
        dl = jnp.concatenate(dlg, axis=1)
        dlb = dl.astype(BF16)
        db_ref[...] += jnp.sum(dl, axis=0, keepdims=True)
        for h in range(2):
            dy2 = d_ref[h] * sg[h] + _dot_nt(dlb, w_ref[h])
            yv = y_ref[h]
            dgelu = 0.5 * (1.0 + th[h]) + 0.5 * yv * (1.0 - th[h] * th[h]) * _GELU_C * (1.0 + 3 * 0.044715 * yv * yv)
            dy_ref[h] = dy2 * dgelu
            dw_ref[h] += _dot_tn(y2[h].astype(BF16), dlb)

    return pl.pallas_call(
        body, grid=(T // tm,),
        in_specs=[pl.BlockSpec((2, tm, C), lambda i: (1, i, 0)), pl.BlockSpec((2, tm, C), lambda i: (0, i, 0)),
                  pl.BlockSpec((2, tm, C), lambda i: (0, i, 0)), pl.BlockSpec((2, C, 2 * C), lambda i: (0, 0, 0))],
        out_specs=[pl.BlockSpec((2, tm, C), lambda i: (0, i, 0)), pl.BlockSpec((2, C, 2 * C), lambda i: (0, 0, 0)),
                   pl.BlockSpec((1, 2 * C), lambda i: (0, 0))],
        out_shape=[SDS((2, T, C), F32), SDS((2, C, 2 * C), F32), SDS((1, 2 * C), F32)],
        compiler_params=_params(1), name=name)(dcat, y, lg, w)


def adamw(w, m, v, slots, name):
    R, C = w.shape
    tr = R
    for cand in (512, 256, 128, 64, 32, 16, 8):
        if R % cand == 0 and cand * C * 4 <= 2 * 1024 * 1024:
            tr = cand
            break
    c1 = 1.0 / (1.0 - ADAM_B1 ** ADAM_STEP)
    c2 = 1.0 / (1.0 - ADAM_B2 ** ADAM_STEP)

    def body(w_ref, m_ref, v_ref, s_ref, g_ref, d_ref, nm_ref, nv_ref):
        g = s_ref[0].astype(F32)
        for j in range(1, N_DEV):
            g = g + s_ref[j].astype(F32)
        nm = ADAM_B1 * m_ref[...] + (1.0 - ADAM_B1) * g
        nv = ADAM_B2 * v_ref[...] + (1.0 - ADAM_B2) * (g * g)
        g_ref[...] = g
        nm_ref[...] = nm
        nv_ref[...] = nv
        d_ref[...] = -ADAM_LR * ((nm * c1) / (jnp.sqrt(nv * c2) + ADAM_EPS) + ADAM_WD * w_ref[...])

    spec = pl.BlockSpec((tr, C), lambda i: (i, 0))
    return pl.pallas_call(
        body, grid=(R // tr,),
        in_specs=[spec, spec, spec, pl.BlockSpec((N_DEV, tr, C), lambda i: (0, i, 0))],
        out_specs=[spec] * 4, out_shape=[SDS((R, C), F32)] * 4, compiler_params=_params(1), name=name)(w, m, v, slots)


def _discretise(a_re, a_im, log_dt, b_re, b_im):
    dt = jnp.exp(log_dt)[:, None]
    e = jnp.exp(dt * a_re)
    ar, ai = e * jnp.cos(dt * a_im), e * jnp.sin(dt * a_im)
    den = a_re * a_re + a_im * a_im
    nr, ni = ar - 1.0, ai
    wr = (nr * a_re + ni * a_im) / den
    wi = (ni * a_re - nr * a_im) / den
    bbr = wr[..., None] * b_re - wi[..., None] * b_im
    bbi = wr[..., None] * b_im + wi[..., None] * b_re
    return ar, ai, bbr, bbi


def _block_diag(t):
    eye = jnp.eye(16, dtype=t.dtype).reshape(1, 16, 1, 16, 1)
    r, c = t.shape[1], t.shape[2]
    return (t.reshape(2, 16, r, 1, c) * eye).reshape(2, 16 * r, 16 * c)


def _diag_blocks(m, r, c):
    eye = jnp.eye(16, dtype=m.dtype).reshape(1, 16, 1, 16, 1)
    return jnp.sum(m.reshape(2, 16, r, 16, c) * eye, axis=3).reshape(32, r, c)


def _state_vec(re, im):
    return jnp.stack([re.reshape(2, HALF_STATES), im.reshape(2, HALF_STATES)], axis=1).reshape(-1)


BIG = ("ffn1_w_in", "ffn1_w_out", "w_mix_in", "w_glu", "w_mix_out", "ffn2_w_in", "ffn2_w_out")
WEIGHTS = ("ffn1_pre_g", "ffn1_w_in", "ffn1_w_out", "ffn1_post_g", "mix_pre_g", "w_mix_in", "a_re", "a_im", "log_dt",
           "b_re", "b_im", "c_re", "c_im", "d_skip", "w_glu", "b_glu", "w_mix_out", "mix_post_g", "ffn2_pre_g",
           "ffn2_w_in", "ffn2_w_out", "ffn2_post_g")
SMALL = tuple(n for n in WEIGHTS if n not in BIG)
TRANSPOSED = ("ffn1_w_in", "ffn2_w_in")
PACK_COLS = 1024


def _pack(parts):
    flat = jnp.concatenate([p.reshape(-1) for p in parts])
    rows = -(-flat.shape[0] // (8 * PACK_COLS)) * 8
    return jnp.pad(flat, (0, rows * PACK_COLS - flat.shape[0])).reshape(rows, PACK_COLS)


def _unpack(packed, shapes):
    flat, out, off = packed.reshape(-1), [], 0
    for s in shapes:
        n = math.prod(s)
        out.append(flat[off:off + n].reshape(s))
        off += n
    return out


def _gather(names, wb):
    return [wb[n] for n in names], [False] * len(names)


def _ffn_bwd(dy, do, saved, x, pre_g, w_in, w_out4, tag, post=None):
    h, z, a = saved
    T = x.shape[0]
    dz = ffn_dact(do, w_out4, z, f"{tag}_dact")
    dz8 = dz.reshape(8, T, dz.shape[-1])
    dw_out, _ = mm_tn(a, do, True, False, 4, f"{tag}_dwout")
    dw_in, (s_out,) = mm_tn(dz8, h, True, False, 8, f"{tag}_dwin", comm=([dw_out.reshape(8, -1, D_MODEL)], [True]))
    outs, (s_in,) = dh_pre_bwd(dz8, w_in, x, pre_g, dy, f"{tag}_dh", comm=([dw_in], [True]), post=post,
                               w_transposed=True)
    return outs, (s_in, s_out)


def local_step(x, tgt, sp, wb):
    T = x.shape[0]
    ar, ai, bbr, bbi = _discretise(sp["a_re"], sp["a_im"], sp["log_dt"], sp["b_re"], sp["b_im"])
    powers = [(ar, ai)]
    for _ in range(7):
        pr, pi = powers[-1]
        powers.append((pr * ar - pi * ai, pr * ai + pi * ar))
    zero = jnp.zeros_like(ar)
    rows = [_state_vec(*powers[k - 1]) for k in (1, 2, 4)] + [_state_vec(zero, zero)] * 5
    rows += [_state_vec(pr, pi) for pr, pi in powers]
    rows += [_state_vec(pr, -pi) for pr, pi in reversed(powers)]
    apow = jnp.stack(rows)
    bh = jnp.concatenate([_block_diag(bbr.transpose(0, 2, 1)), _block_diag(bbi.transpose(0, 2, 1))], axis=2)
    ch = jnp.concatenate([_block_diag(sp["c_re"].transpose(0, 2, 1)), _block_diag(-sp["c_im"].transpose(0, 2, 1))], axis=1)
    bh, ch = bh.astype(BF16), ch.astype(BF16)
    dskip = sp["d_skip"].reshape(2, 1, 256)

    w1_in = gather_two_level(wb["ffn1_w_in"], "gather_w1in")
    (h1, z1, a1), (w1_out, w_mi) = ffn_in(
        x, sp["ffn1_pre_g"], w1_in, "ffn1_in", comm=_gather(["ffn1_w_out", "w_mix_in"], wb))
    w1_out4 = w1_out.reshape(4, -1, D_MODEL)
    (o1, x1), (w_glu, w_mo) = mm_acc_norm(
        a1, w1_out4, x, sp["ffn1_post_g"], 0.5, "ffn1_out", comm=_gather(["w_glu", "w_mix_out"], wb))
    w_glu2, w_mo4 = w_glu.reshape(2, 256, 512), w_mo.reshape(4, 256, D_MODEL)
    h2, proj = norm_proj(x1, sp["mix_pre_g"], w_mi, "mix_proj")
    (y_ssm, states), (w2_in,) = ssm_fwd(proj, bh, ch, apow, dskip, "ssm_fwd", comm=_gather(["ffn2_w_in"], wb))
    os_, ls_ = [], []
    for d in DILATIONS:
        (o_d, l_d), got = attn_fwd(proj, d, f"attn_fwd_d{d}",
                                   comm=_gather(["ffn2_w_out"], wb) if d == DILATIONS[-1] else None)
        os_.append(o_d)
        ls_.append(l_d)
    w2_out4 = got[0].reshape(4, -1, D_MODEL)
    o_ssm, lg = glu_fwd(y_ssm, w_glu2, sp["b_glu"], "glu_fwd")
    cat, lse, mixed, x2 = mix_out(os_, ls_, o_ssm, w_mo4, x1, sp["mix_post_g"], "mix_out")
    (h3, z3, a3), _ = ffn_in(x2, sp["ffn2_pre_g"], w2_in, "ffn2_in")
    (dy3, sq, do3, dg_f2post), _ = mm_acc_norm(a3, w2_out4, x2, sp["ffn2_post_g"], 0.5, "ffn2_out", tgt=tgt)

    (dx2, dg_f2pre, dmixed, dg_mpost), (s_w2in, s_w2out) = _ffn_bwd(
        dy3, do3, (h3, z3, a3), x2, sp["ffn2_pre_g"], w2_in, w2_out4, "ffn2", post=(mixed, sp["mix_post_g"], 1.0))
    dcat = mm_nt_b(dmixed, w_mo4, "mix_dcat")
    dw_mo, _ = mm_tn(cat, dmixed, True, False, 4, "mix_dwout")
    dy_ssm, dw_glu, db_glu = glu_bwd(dcat, y_ssm, lg, w_glu2, "glu_bwd")
    (du, da, dbh, dch, dd), (s_wmo, s_wglu) = ssm_bwd(
        dy_ssm, proj, states, bh, ch, apow, dskip, "ssm_bwd",
        comm=([dw_mo.reshape(8, 128, D_MODEL), dw_glu.astype(BF16).reshape(8, 64, 512)], [True, True]))
    dqkv = None
    for d in DILATIONS:
        dqkv = attn_bwd(proj, dcat, cat, lse, dqkv, d, f"attn_bwd_d{d}", du=du if d == DILATIONS[-1] else None)
    dproj = dqkv.reshape(8, T, 256)
    dw_mi, _ = mm_tn(h2, dproj, False, True, 8, "mix_dwin")
    (dx1, dg_mpre, do1, dg_f1post), (s_wmi,) = dh_pre_bwd(
        dproj, w_mi, x1, sp["mix_pre_g"], dx2, "mix_dh", comm=([dw_mi], [True]), post=(o1, sp["ffn1_post_g"], 0.5))
    (dx0, dg_f1pre), (s_w1in, s_w1out) = _ffn_bwd(
        dx1, do1, (h1, z1, a1), x, sp["ffn1_pre_g"], w1_in, w1_out4, "ffn1")

    da4 = da.reshape(2, 2, HALF_STATES)
    d_ar, d_ai = da4[:, 0].reshape(32, N_STATE), da4[:, 1].reshape(32, N_STATE)
    d_bbr = _diag_blocks(dbh[:, :, :HALF_STATES], 16, N_STATE).transpose(0, 2, 1)
    d_bbi = _diag_blocks(dbh[:, :, HALF_STATES:], 16, N_STATE).transpose(0, 2, 1)
    _, disc_vjp = jax.vjp(_discretise, sp["a_re"], sp["a_im"], sp["log_dt"], sp["b_re"], sp["b_im"])
    g_are, g_aim, g_ldt, g_bre, g_bim = disc_vjp((d_ar, d_ai, d_bbr, d_bbi))
    g_cre = _diag_blocks(dch[:, :HALF_STATES], N_STATE, 16).transpose(0, 2, 1)
    g_cim = -_diag_blocks(dch[:, HALF_STATES:], N_STATE, 16).transpose(0, 2, 1)
    small = {
        "ffn1_pre_g": dg_f1pre, "ffn1_post_g": dg_f1post, "mix_pre_g": dg_mpre, "a_re": g_are, "a_im": g_aim,
        "log_dt": g_ldt, "b_re": g_bre, "b_im": g_bim, "c_re": g_cre, "c_im": g_cim, "d_skip": dd.reshape(1, 512),
        "b_glu": db_glu, "mix_post_g": dg_mpost, "ffn2_pre_g": dg_f2pre, "ffn2_post_g": dg_f2post,
    }
    small_slots = gather_two_level(_pack([small[n] for n in SMALL]), "exchange_small")
    big_slots = {"ffn1_w_in": s_w1in, "ffn1_w_out": s_w1out, "w_mix_in": s_wmi, "w_glu": s_wglu, "w_mix_out": s_wmo,
                 "ffn2_w_in": s_w2in, "ffn2_w_out": s_w2out}
    return sq, dx0, big_slots, small_slots


def kernel(x, ffn1_pre_g, ffn1_w_in, ffn1_w_out, ffn1_post_g, mix_pre_g, w_mix_in, a_re, a_im, log_dt, b_re, b_im, c_re, c_im, d_skip, w_glu, b_glu, w_mix_out, mix_post_g, ffn2_pre_g, ffn2_w_in, ffn2_w_out, ffn2_post_g, loss_target, m_ffn1_pre_g, m_ffn1_w_in, m_ffn1_w_out, m_ffn1_post_g, m_mix_pre_g, m_w_mix_in, m_a_re, m_a_im, m_log_dt, m_b_re, m_b_im, m_c_re, m_c_im, m_d_skip, m_w_glu, m_b_glu, m_w_mix_out, m_mix_post_g, m_ffn2_pre_g, m_ffn2_w_in, m_ffn2_w_out, m_ffn2_post_g, v_ffn1_pre_g, v_ffn1_w_in, v_ffn1_w_out, v_ffn1_post_g, v_mix_pre_g, v_w_mix_in, v_a_re, v_a_im, v_log_dt, v_b_re, v_b_im, v_c_re, v_c_im, v_d_skip, v_w_glu, v_b_glu, v_w_mix_out, v_mix_post_g, v_ffn2_pre_g, v_ffn2_w_in, v_ffn2_w_out, v_ffn2_post_g):
    args = dict(locals())
    w = {n: args[n][0] for n in WEIGHTS}
    m = {n: args["m_" + n][0] for n in WEIGHTS}
    v = {n: args["v_" + n][0] for n in WEIGHTS}

    for d in (w, m, v):
        for n in TRANSPOSED:
            d[n] = jnp.swapaxes(d[n], 0, 1)
    wb = {n: w[n].astype(BF16) for n in BIG}
    sp = {n: w[n] for n in SMALL}
    for n in ("ffn1_pre_g", "ffn1_post_g", "mix_pre_g", "mix_post_g", "ffn2_pre_g", "ffn2_post_g", "b_glu", "d_skip"):
        sp[n] = w[n].reshape(1, -1)

    sq, grad_x, big_slots, small_slots = local_step(x[0], loss_target[0], sp, wb)
    loss = lax.psum(0.5 / D_MODEL * jnp.sum(sq), ("x", "y", "c"))

    outs = {}
    for n in BIG:
        shp = w[n].shape
        r2 = lambda t: t.reshape(-1, shp[-1])
        res = adamw(r2(w[n]), r2(m[n]), r2(v[n]), big_slots[n].reshape(N_DEV, -1, shp[-1]), f"adamw_{n}")
        outs[n] = [(jnp.swapaxes(t, 0, 1) if n in TRANSPOSED else t.reshape(shp))[None] for t in res]
    res = adamw(_pack([w[n] for n in SMALL]), _pack([m[n] for n in SMALL]), _pack([v[n] for n in SMALL]),
                small_slots, "adamw_small")
    shapes = [(1,) + w[n].shape for n in SMALL]
    unpacked = [_unpack(t, shapes) for t in res]
    for j, n in enumerate(SMALL):
        outs[n] = [unpacked[k][j] for k in range(4)]

    result = [loss, grad_x[None]]
    for k in range(4):
        result += [outs[n][k] for n in WEIGHTS]
    return tuple(result)
```

```python
import functools
import math

import jax
import jax.numpy as jnp
from jax import lax
from jax.experimental import pallas as pl
from jax.experimental.pallas import tpu as pltpu

F32, BF16 = jnp.float32, jnp.bfloat16
SDS = jax.ShapeDtypeStruct

D_MODEL = 1024
N_DEV = 8
HEAD_DIM = 64
PAIR_W = 128
QBLK = 128
DILATIONS = (1, 4, 16)
N_STATE = 64
HALF_STATES = 1024
NORM_EPS = 1e-6
NEG = -1e30
VMEM_LIMIT = 56 * 1024 * 1024
ADAM_LR, ADAM_B1, ADAM_B2, ADAM_EPS, ADAM_WD, ADAM_STEP = 1e-3, 0.9, 0.999, 1e-8, 0.01, 10
SCAN_TM = 256
SCAN_TM_BWD = 512
SCAN_CW = 512


def _params(n_grid):
    return pltpu.CompilerParams(dimension_semantics=("arbitrary",) * n_grid, vmem_limit_bytes=VMEM_LIMIT)


def _dot(a, b):
    return jnp.dot(a, b, preferred_element_type=F32)


def _dot_nt(a, b):
    return lax.dot_general(a, b, (((1,), (1,)), ((), ())), preferred_element_type=F32)


def _dot_tn(a, b):
    return lax.dot_general(a, b, (((0,), (0,)), ((), ())), preferred_element_type=F32)


def _sigmoid(v):
    return 0.5 * jnp.tanh(0.5 * v) + 0.5


def _resident(shape):
    return pl.BlockSpec(shape, lambda i: (0,) * len(shape), pipeline_mode=pl.Buffered(1))


ROW_SPLIT = 2


def _exchange_phase(ins, outs, scatter, sems, start):
    send_sems, recv_sems, loc_sems = sems
    x, y, c = lax.axis_index("x"), lax.axis_index("y"), lax.axis_index("c")
    me = 4 * x + 2 * y + c
    own_copies, sends, arrivals = [], [], []
    for i in range(len(ins)):
        own = ins[i].at[me] if scatter[i] else ins[i]
        own_copies.append(pltpu.make_async_copy(own, outs[i].at[me], loc_sems.at[i]))
        for k in range(1, N_DEV):
            px = 1 - x if k & 4 else x
            py = 1 - y if k & 2 else y
            pc = 1 - c if k & 1 else c
            peer = 4 * px + 2 * py + pc
            src = ins[i].at[peer] if scatter[i] else ins[i]
            common = dict(src_ref=src, send_sem=send_sems.at[i, k - 1], recv_sem=recv_sems.at[i, k - 1],
                          device_id=(px, py, pc), device_id_type=pl.DeviceIdType.MESH)
            sends.append(pltpu.make_async_remote_copy(dst_ref=outs[i].at[me], **common))
            if not start:
                arrivals.append(pltpu.make_async_remote_copy(dst_ref=outs[i].at[peer], **common))
    if start:
        for cp in own_copies + sends:
            cp.start()
    else:
        for cp in arrivals:
            cp.wait_recv()
        for cp in sends:
            cp.wait_send()
        for cp in own_copies:
            cp.wait()


def _comm_shapes(arrs, scatter):
    n = len(arrs)
    out_shapes = [SDS(a.shape if scatter[i] else (N_DEV,) + a.shape, a.dtype) for i, a in enumerate(arrs)]
    sems = [pltpu.SemaphoreType.DMA((n, N_DEV - 1)), pltpu.SemaphoreType.DMA((n, N_DEV - 1)),
            pltpu.SemaphoreType.DMA((n,))]
    return out_shapes, sems


def gather_two_level(arr, name):
    def body(x_ref, out_ref, send_sems, recv_sems, local_sem):
        x, y, c = lax.axis_index("x"), lax.axis_index("y"), lax.axis_index("c")
        sibling = (x, y, 1 - c)
        chips = [(1 - x, y), (x, 1 - y), (1 - x, 1 - y)]

        def slot(px, py, pc):
            return out_ref.at[4 * px + 2 * py + pc]

        def copy(k, block, to, src=None):
            return pltpu.make_async_remote_copy(
                src_ref=slot(*block) if src is None else src, dst_ref=slot(*block),
                send_sem=send_sems.at[k], recv_sem=recv_sems.at[k], device_id=to, device_id_type=pl.DeviceIdType.MESH)

        mine = pltpu.make_async_copy(x_ref, slot(x, y, c), local_sem)
        mine.start()
        first = [copy(0, (x, y, c), sibling, src=x_ref)]
        first += [copy(1 + j, (x, y, c), (*chip, c), src=x_ref) for j, chip in enumerate(chips)]
        for cp in first:
            cp.start()
        passed = [copy(4 + j, (*chip, c), sibling) for j, chip in enumerate(chips)]
        for j, chip in enumerate(chips):
            copy(1 + j, (*chip, c), (x, y, c)).wait_recv()
            passed[j].start()
        copy(0, sibling, (x, y, c)).wait_recv()
        for j, chip in enumerate(chips):
            copy(4 + j, (*chip, 1 - c), (x, y, c)).wait_recv()
        for cp in first + passed:
            cp.wait_send()
        mine.wait()

    anyspec = pl.BlockSpec(memory_space=pl.ANY)
    return pl.pallas_call(
        body, in_specs=[anyspec], out_specs=anyspec, out_shape=SDS((N_DEV,) + arr.shape, arr.dtype),
        scratch_shapes=[pltpu.SemaphoreType.DMA((N_DEV - 1,)), pltpu.SemaphoreType.DMA((N_DEV - 1,)),
                        pltpu.SemaphoreType.DMA],
        compiler_params=pltpu.CompilerParams(has_side_effects=True), name=name)(arr)


def _call(body, *, grid, in_specs, out_specs, out_shape, args, name, scratch_shapes=(), comm=None):
    n_grid, scratch_shapes = len(grid), list(scratch_shapes)
    if comm is None:
        outs = pl.pallas_call(body, grid=grid, in_specs=in_specs, out_specs=out_specs, out_shape=out_shape,
                              scratch_shapes=scratch_shapes, compiler_params=_params(n_grid), name=name)(*args)
        return outs, []
    arrs, scatter = comm
    nc, n_in, n_out, n_sc = len(arrs), len(in_specs), len(out_specs), len(scratch_shapes)
    comm_shapes, sems = _comm_shapes(arrs, scatter)

    def wrapped(*refs):
        ins, cins = refs[:n_in], refs[n_in:n_in + nc]
        o0 = n_in + nc
        outs, couts = refs[o0:o0 + n_out], refs[o0 + n_out:o0 + n_out + nc]
        s0 = o0 + n_out + nc
        scratch, sem_refs = refs[s0:s0 + n_sc], refs[s0 + n_sc:]
        first = functools.reduce(jnp.logical_and, [pl.program_id(k) == 0 for k in range(n_grid)])
        last = functools.reduce(jnp.logical_and, [pl.program_id(k) == grid[k] - 1 for k in range(n_grid)])

        @pl.when(first)
        def _():
            _exchange_phase(cins, couts, scatter, sem_refs, True)

        body(*ins, *outs, *scratch)

        @pl.when(last)
        def _():
            _exchange_phase(cins, couts, scatter, sem_refs, False)

    anyspec = pl.BlockSpec(memory_space=pl.ANY)
    res = pl.pallas_call(
        wrapped, grid=grid, in_specs=list(in_specs) + [anyspec] * nc, out_specs=list(out_specs) + [anyspec] * nc,
        out_shape=list(out_shape) + comm_shapes, scratch_shapes=scratch_shapes + sems,
        compiler_params=pltpu.CompilerParams(dimension_semantics=("arbitrary",) * n_grid,
                                             vmem_limit_bytes=VMEM_LIMIT, has_side_effects=True),
        name=name)(*args, *arrs)
    return res[:n_out], res[n_out:]


def _rms(xv, g):
    r = lax.rsqrt(jnp.mean(xv * xv, axis=-1, keepdims=True) + NORM_EPS)
    return (xv * r * g).astype(BF16)


def ffn_in(x, g, w, name, comm=None):
    T, D = x.shape
    F = w.shape[1]
    tm = 512

    def body(x_ref, g_ref, w_ref, h_ref, z_ref, a_ref):
        hv = _rms(x_ref[...], g_ref[...])
        h_ref[...] = hv
        pending = None
        for j in range(5):
            if j < 4:
                zs = (_dot_nt(hv, w_ref[j]), _dot_nt(hv, w_ref[j + 4]))
            if pending is not None:
                zg, zu = pending
                sg = _sigmoid(zg)
                silu = zg * sg
                z_ref[0, j - 1] = (zu * (sg + silu - silu * sg)).astype(BF16)
                z_ref[1, j - 1] = silu.astype(BF16)
                a_ref[j - 1] = (silu * zu).astype(BF16)
            pending = zs

    return _call(
        body, grid=(T // tm,),
        in_specs=[pl.BlockSpec((tm, D), lambda i: (i, 0)), pl.BlockSpec((1, D), lambda i: (0, 0)),
                  _resident((8, F, D))],
        out_specs=[pl.BlockSpec((tm, D), lambda i: (i, 0)), pl.BlockSpec((2, 4, tm, F), lambda i: (0, 0, i, 0)),
                   pl.BlockSpec((4, tm, F), lambda i: (0, i, 0))],
        out_shape=[SDS((T, D), BF16), SDS((2, 4, T, F), BF16), SDS((4, T, F), BF16)],
        args=(x, g, w), name=name, comm=comm)


def norm_proj(x, g, w, name):
    T, K = x.shape
    nb, _, N = w.shape
    tm = 512

    def body(x_ref, g_ref, w_ref, h_ref, o_ref):
        hv = _rms(x_ref[...], g_ref[...])
        h_ref[...] = hv
        for b in range(nb):
            o_ref[b] = _dot(hv, w_ref[b])

    return pl.pallas_call(
        body, grid=(T // tm,),
        in_specs=[pl.BlockSpec((tm, K), lambda i: (i, 0)), pl.BlockSpec((1, K), lambda i: (0, 0)),
                  _resident((nb, K, N))],
        out_specs=[pl.BlockSpec((tm, K), lambda i: (i, 0)), pl.BlockSpec((nb, tm, N), lambda i: (0, i, 0))],
        out_shape=[SDS((T, K), BF16), SDS((nb, T, N), F32)], compiler_params=_params(1), name=name)(x, g, w)


def mm_acc_norm(a, w, xres, g, scale, name, comm=None, tgt=None):
    nb, T, K = a.shape
    D = w.shape[2]
    tm = 512
    rc = tm // ROW_SPLIT
    with_loss = tgt is not None

    def body(a_ref, w_ref, x_ref, g_ref, *rest):
        if with_loss:
            t_ref, dy_ref, sq_ref, do_ref, dg_ref = rest

            @pl.when(pl.program_id(0) == 0)
            def _():
                sq_ref[...] = jnp.zeros_like(sq_ref)
                dg_ref[...] = jnp.zeros_like(dg_ref)
        else:
            o_ref, y_ref = rest
        accs = []
        for c in range(ROW_SPLIT):
            rows = pl.ds(c * rc, rc)
            o = _dot(a_ref[0, rows, :].astype(BF16), w_ref[0])
            for b in range(1, nb):
                o += _dot(a_ref[b, rows, :].astype(BF16), w_ref[b])
            accs.append(o)
        for c, o in enumerate(accs):
            rows = pl.ds(c * rc, rc)
            r = lax.rsqrt(jnp.mean(o * o, axis=-1, keepdims=True) + NORM_EPS)
            y = x_ref[rows, :] + scale * (o * r * g_ref[...])
            if with_loss:
                e = y - t_ref[rows, :]
                dy = e * (1.0 / D)
                dy_ref[rows, :] = dy
                sq_ref[...] += jnp.sum(e * e, axis=0, keepdims=True)
                do, dg = _post_bwd(dy, o, g_ref[...], scale)
                do_ref[rows, :] = do
                dg_ref[...] += dg
            else:
                o_ref[rows, :] = o
                y_ref[rows, :] = y

    tile = pl.BlockSpec((tm, D), lambda i: (i, 0))
    row = pl.BlockSpec((1, D), lambda i: (0, 0))
    in_specs = [pl.BlockSpec((nb, tm, K), lambda i: (0, i, 0)), _resident((nb, K, D)), tile, row]
    args = (a, w, xres, g)
    if with_loss:
        return _call(body, grid=(T // tm,), in_specs=in_specs + [tile], out_specs=[tile, row, tile, row],
                     out_shape=[SDS((T, D), F32), SDS((1, D), F32), SDS((T, D), BF16), SDS((1, D), F32)],
                     args=args + (tgt,), name=name, comm=comm)
    return _call(body, grid=(T // tm,), in_specs=in_specs, out_specs=[tile, tile],
                 out_shape=[SDS((T, D), F32), SDS((T, D), F32)], args=args, name=name, comm=comm)


def _post_bwd(dy, ov, g, scale):
    r = scale * dy
    rstd = lax.rsqrt(jnp.mean(ov * ov, axis=-1, keepdims=True) + NORM_EPS)
    oh = ov * rstd
    rg = r * g
    do = rstd * (rg - oh * jnp.mean(rg * oh, axis=-1, keepdims=True))
    return do.astype(BF16), jnp.sum(r * oh, axis=0, keepdims=True)


def mm_nt_b(gr, w, name):
    T, N = gr.shape
    nb, K, _ = w.shape
    tm = 512

    def body(g_ref, w_ref, o_ref):
        gv = g_ref[...]
        for b in range(nb):
            o_ref[b] = _dot_nt(gv, w_ref[b])

    return pl.pallas_call(
        body, grid=(T // tm,),
        in_specs=[pl.BlockSpec((tm, N), lambda i: (i, 0)), _resident((nb, K, N))],
        out_specs=pl.BlockSpec((nb, tm, K), lambda i: (0, i, 0)),
        out_shape=SDS((nb, T, K), F32), compiler_params=_params(1), name=name)(gr, w)


def ffn_dact(do, w_out, z, name):
    T, D = do.shape
    nb, F, _ = w_out.shape
    tm = 512

    rc = tm // ROW_SPLIT

    def body(g_ref, w_ref, z_ref, dz_ref):
        das = [_dot_nt(g_ref[pl.ds(c * rc, rc), :], w_ref[...]) for c in range(ROW_SPLIT)]
        for c, da in enumerate(das):
            rows = pl.ds(c * rc, rc)
            dz_ref[0, rows, :] = (da * z_ref[0, rows, :].astype(F32)).astype(BF16)
            dz_ref[1, rows, :] = (da * z_ref[1, rows, :].astype(F32)).astype(BF16)

    return pl.pallas_call(
        body, grid=(nb, T // tm),
        in_specs=[pl.BlockSpec((tm, D), lambda b, i: (i, 0)), pl.BlockSpec((None, F, D), lambda b, i: (b, 0, 0)),
                  pl.BlockSpec((2, None, tm, F), lambda b, i: (0, b, i, 0))],
        out_specs=pl.BlockSpec((2, None, tm, F), lambda b, i: (0, b, i, 0)),
        out_shape=SDS((2, nb, T, F), BF16), compiler_params=_params(2), name=name)(do, w_out, z)


def mm_tn(a, g, a_batched, g_batched, nb, name, comm=None):
    T = a.shape[-2]
    K, N = a.shape[-1], g.shape[-1]
    tk = 2048
    nk = T // tk

    def body(a_ref, g_ref, o_ref, acc):
        k = pl.program_id(1)

        @pl.when(k == 0)
        def _():
            acc[...] = jnp.zeros_like(acc)

        acc[...] += _dot_tn(a_ref[...].astype(BF16), g_ref[...].astype(BF16))

        @pl.when(k == nk - 1)
        def _():
            o_ref[...] = acc[...].astype(BF16)

    a_spec = (pl.BlockSpec((None, tk, K), lambda b, k: (b, k, 0)) if a_batched
              else pl.BlockSpec((tk, K), lambda b, k: (k, 0)))
    g_spec = (pl.BlockSpec((None, tk, N), lambda b, k: (b, k, 0)) if g_batched
              else pl.BlockSpec((tk, N), lambda b, k: (k, 0)))
    (out,), slots = _call(
        body, grid=(nb, nk), in_specs=[a_spec, g_spec],
        out_specs=[pl.BlockSpec((None, K, N), lambda b, k: (b, 0, 0))],
        out_shape=[SDS((nb, K, N), BF16)], scratch_shapes=[pltpu.VMEM((K, N), F32)],
        args=(a, g), name=name, comm=comm)
    return out, slots


def dh_pre_bwd(dz, w, x, g, dyres, name, comm=None, post=None, w_transposed=False):
    nb, T, F = dz.shape
    D = x.shape[1]
    tm = 512
    rc = tm // ROW_SPLIT
    mm = _dot if w_transposed else _dot_nt

    def body(dz_ref, w_ref, x_ref, g_ref, dy_ref, *rest):
        if post is None:
            dx_ref, dg_ref = rest
        else:
            o_ref, gp_ref, dx_ref, dg_ref, do_ref, dgp_ref = rest

        @pl.when(pl.program_id(0) == 0)
        def _():
            dg_ref[...] = jnp.zeros_like(dg_ref)
            if post is not None:
                dgp_ref[...] = jnp.zeros_like(dgp_ref)

        accs = []
        for c in range(ROW_SPLIT):
            rows = pl.ds(c * rc, rc)
            dh = mm(dz_ref[0, rows, :].astype(BF16), w_ref[0])
            for b in range(1, nb):
                dh += mm(dz_ref[b, rows, :].astype(BF16), w_ref[b])
            accs.append(dh)
        for c, dh in enumerate(accs):
            rows = pl.ds(c * rc, rc)
            xv = x_ref[rows, :]
            rstd = lax.rsqrt(jnp.mean(xv * xv, axis=-1, keepdims=True) + NORM_EPS)
            xh = xv * rstd
            dg_ref[...] += jnp.sum(dh * xh, axis=0, keepdims=True)
            dhg = dh * g_ref[...]
            dx = dy_ref[rows, :] + rstd * (dhg - xh * jnp.mean(dhg * xh, axis=-1, keepdims=True))
            dx_ref[rows, :] = dx
            if post is not None:
                do, dgp = _post_bwd(dx, o_ref[rows, :], gp_ref[...], post[2])
                do_ref[rows, :] = do
                dgp_ref[...] += dgp

    tile = pl.BlockSpec((tm, D), lambda i: (i, 0))
    row = pl.BlockSpec((1, D), lambda i: (0, 0))
    in_specs = [pl.BlockSpec((nb, tm, F), lambda i: (0, i, 0)), _resident(w.shape), tile, row, tile]
    out_specs, out_shape, args = [tile, row], [SDS((T, D), F32), SDS((1, D), F32)], (dz, w, x, g, dyres)
    if post is not None:
        in_specs += [tile, row]
        out_specs += [tile, row]
        out_shape += [SDS((T, D), BF16), SDS((1, D), F32)]
        args += (post[0], post[1])
    return _call(body, grid=(T // tm,), in_specs=in_specs, out_specs=out_specs, out_shape=out_shape,
                 args=args, name=name, comm=comm)


ATTN_GROUP = {1: 4, 4: 2, 16: 1}
ATTN_UNROLL = 4


def _attn_masks():
    qi = lax.broadcasted_iota(jnp.int32, (QBLK, QBLK), 0)
    kj = lax.broadcasted_iota(jnp.int32, (QBLK, QBLK), 1)
    cur_ok = kj <= qi
    prev_ok = kj >= qi
    dcur = (qi - kj).astype(F32)
    return cur_ok, prev_ok, dcur, dcur + float(QBLK)


def _head_slopes(p, d):
    out = []
    for hq in range(2):
        v = [float(d) * 2.0 ** -(2 * q + hq + 1) for q in range(4)]
        out.append(jnp.where(p == 0, v[0], jnp.where(p == 1, v[1], jnp.where(p == 2, v[2], v[3]))))
    return out


def _rows(start, d):
    return pl.ds(start, QBLK, stride=d) if d > 1 else pl.ds(start, QBLK)


def _pair_spec(rows, part, blk):
    return pl.BlockSpec((None, rows, PAIR_W), lambda p, n: (2 * part + p // 2, blk(n), p % 2))


def _for_query_blocks(d, groups, several):
    blocks = [(g, r) for r in range(d) for g in range(groups)]
    for s in range(0, len(blocks), ATTN_UNROLL):
        several(blocks[s:s + ATTN_UNROLL])


def attn_fwd(proj, d, name, comm=None):
    T = proj.shape[1]
    sb, groups = QBLK * d, ATTN_GROUP[d]
    rb = sb * groups
    nblk = T // rb

    def body(q_ref, kc_ref, kp_ref, vc_ref, vp_ref, o_ref, l_ref):
        p, n = pl.program_id(0), pl.program_id(1)
        cur_ok, prev_ok, dcur, dprev = _attn_masks()
        first_ok = jnp.logical_and(prev_ok, n > 0)
        lane_head = lax.broadcasted_iota(jnp.int32, (QBLK, PAIR_W), 1) // HEAD_DIM
        slopes = _head_slopes(p, d)

        def several(blocks):
            work = []
            for g, r in blocks:
                rows = _rows(g * sb + r, d)
                q = q_ref[rows, :]
                kc, vc = kc_ref[rows, :].astype(BF16), vc_ref[rows, :].astype(BF16)
                if g == 0:
                    prow, pok = _rows(r, d), first_ok
                    kp, vp = kp_ref[prow, :].astype(BF16), vp_ref[prow, :].astype(BF16)
                else:
                    prow, pok = _rows((g - 1) * sb + r, d), prev_ok
                    kp, vp = kc_ref[prow, :].astype(BF16), vc_ref[prow, :].astype(BF16)
                for hq in range(2):
                    qm = jnp.where(lane_head == hq, q, 0.0).astype(BF16)
                    work.append([rows, hq, pok, vc, vp, _dot_nt(qm, kc), _dot_nt(qm, kp)])
            for w in work:
                _, hq, pok, _, _, sc, sp = w
                sc = jnp.where(cur_ok, sc * 0.125 - slopes[hq] * dcur, NEG)
                sp = jnp.where(pok, sp * 0.125 - slopes[hq] * dprev, NEG)
                m = jnp.maximum(jnp.max(sc, axis=1, keepdims=True), jnp.max(sp, axis=1, keepdims=True))
                pc = jnp.exp(sc - m)
                pp = jnp.exp(sp - m)
                den = jnp.sum(pc, axis=1, keepdims=True) + jnp.sum(pp, axis=1, keepdims=True)
                w[5:] = [pc.astype(BF16), pp.astype(BF16), 1.0 / den, m + jnp.log(den)]
            for i in range(0, len(work), 2):
                o_acc = jnp.zeros((QBLK, PAIR_W), F32)
                l_acc = jnp.zeros((QBLK, PAIR_W), F32)
                for rows, hq, _, vc, vp, pc, pp, inv, lse in work[i:i + 2]:
                    hm = lane_head == hq
                    o_acc = jnp.where(hm, (_dot(pc, vc) + _dot(pp, vp)) * inv, o_acc)
                    l_acc = jnp.where(hm, lse, l_acc)
                o_ref[rows, :] = o_acc
                l_ref[rows, :] = l_acc

        _for_query_blocks(d, groups, several)

    cur = lambda part: _pair_spec(rb, part, lambda n: n)
    prv = lambda part: _pair_spec(sb, part, lambda n: jnp.maximum(n * groups - 1, 0))
    return _call(
        body, grid=(4, nblk), in_specs=[cur(0), cur(1), prv(1), cur(2), prv(2)], out_specs=[cur(0), cur(0)],
        out_shape=[SDS((2, T, 2 * PAIR_W), F32), SDS((2, T, 2 * PAIR_W), F32)],
        args=(proj, proj, proj, proj, proj), name=name, comm=comm)


def mix_out(os_, ls_, o_ssm, w, xres, g, name):
    _, T, HW = o_ssm.shape
    D = w.shape[2]
    tm = 512

    def body(o1, o2, o3, l1, l2, l3, s_ref, w_ref, x_ref, g_ref, cat_ref, l_ref, m_ref, y_ref):
        a, b, c = l1[...], l2[...], l3[...]
        m = jnp.maximum(jnp.maximum(a, b), c)
        ea, eb, ec = jnp.exp(a - m), jnp.exp(b - m), jnp.exp(c - m)
        s = ea + eb + ec
        att = (ea * o1[...] + eb * o2[...] + ec * o3[...]) * (1.0 / s)
        ssm = s_ref[...]
        cat_ref[pl.ds(0, 2)] = att
        cat_ref[pl.ds(2, 2)] = ssm
        l_ref[...] = m + jnp.log(s)
        o = _dot(att[0].astype(BF16), w_ref[0]) + _dot(att[1].astype(BF16), w_ref[1])
        o += _dot(ssm[0].astype(BF16), w_ref[2]) + _dot(ssm[1].astype(BF16), w_ref[3])
        r = lax.rsqrt(jnp.mean(o * o, axis=-1, keepdims=True) + NORM_EPS)
        m_ref[...] = o
        y_ref[...] = x_ref[...] + o * r * g_ref[...]

    spec = pl.BlockSpec((2, tm, HW), lambda i: (0, i, 0))
    tile = pl.BlockSpec((tm, D), lambda i: (i, 0))
    return pl.pallas_call(
        body, grid=(T // tm,),
        in_specs=[spec] * 7 + [_resident(w.shape), tile, pl.BlockSpec((1, D), lambda i: (0, 0))],
        out_specs=[pl.BlockSpec((4, tm, HW), lambda i: (0, i, 0)), spec, tile, tile],
        out_shape=[SDS((4, T, HW), F32), SDS((2, T, HW), F32), SDS((T, D), F32), SDS((T, D), F32)],
        compiler_params=_params(1), name=name)(*os_, *ls_, o_ssm, w, xres, g)


def attn_bwd(proj, dcat, o, lse, acc, d, name, du=None):
    T = proj.shape[1]
    sb, groups = QBLK * d, ATTN_GROUP[d]
    rb = sb * groups
    nblk = T // rb
    has_acc = acc is not None
    n_parts = 3 if du is None else 4

    def body(*refs):
        (qc_ref, qn_ref, kc_ref, kp_ref, vc_ref, vp_ref, dc_ref, dn_ref, oc_ref, on_ref, lc_ref, ln_ref) = refs[:12]
        acc_ref = refs[12] if has_acc else None
        out_ref = refs[-1]
        if du is not None:
            out_ref[3] = refs[-2][...]
        p, n = pl.program_id(0), pl.program_id(1)
        cur_ok, prev_ok, dcur, dprev = _attn_masks()
        first_ok = jnp.logical_and(prev_ok, n > 0)
        last_ok = jnp.logical_and(prev_ok, n < nblk - 1)
        lane_head = lax.broadcasted_iota(jnp.int32, (QBLK, PAIR_W), 1) // HEAD_DIM
        slopes = _head_slopes(p, d)

        def one(g, r, shared):
            rows = _rows(g * sb + r, d)
            q_c, do_c, o_c, l_c = qc_ref[rows, :], dc_ref[rows, :], oc_ref[rows, :], lc_ref[rows, :]
            k_c, v_c = kc_ref[rows, :].astype(BF16), vc_ref[rows, :].astype(BF16)
            if shared:
                pok_c, k_p, v_p = prev_ok, None, None
            elif g == 0:
                prow, pok_c = _rows(r, d), first_ok
                k_p, v_p = kp_ref[prow, :].astype(BF16), vp_ref[prow, :].astype(BF16)
            else:
                prow, pok_c = _rows((g - 1) * sb + r, d), prev_ok
                k_p, v_p = kc_ref[prow, :].astype(BF16), vc_ref[prow, :].astype(BF16)
            if g == groups - 1:
                nrow, pok_n = _rows(r, d), last_ok
                q_n, do_n, o_n, l_n = qn_ref[nrow, :], dn_ref[nrow, :], on_ref[nrow, :], ln_ref[nrow, :]
            else:
                nrow, pok_n = _rows((g + 1) * sb + r, d), prev_ok
                q_n, do_n, o_n, l_n = qc_ref[nrow, :], dc_ref[nrow, :], oc_ref[nrow, :], lc_ref[nrow, :]
            heads = []
            for hq in range(2):
                hm = lane_head == hq
                qm_c = jnp.where(hm, q_c, 0.0).astype(BF16)
                qm_n = jnp.where(hm, q_n, 0.0).astype(BF16)
                dom_c = jnp.where(hm, do_c, 0.0)
                dom_n = jnp.where(hm, do_n, 0.0)
                dd_c = jnp.sum(dom_c * o_c, axis=1, keepdims=True)
                dd_n = jnp.sum(dom_n * o_n, axis=1, keepdims=True)
                ls_c = jnp.max(jnp.where(hm, l_c, NEG), axis=1, keepdims=True)
                ls_n = jnp.max(jnp.where(hm, l_n, NEG), axis=1, keepdims=True)
                dob_c, dob_n = dom_c.astype(BF16), dom_n.astype(BF16)
                mm = [(_dot_nt(qm_c, k_c), _dot_nt(dob_c, v_c)),
                      None if shared else (_dot_nt(qm_c, k_p), _dot_nt(dob_c, v_p)),
                      (_dot_nt(qm_n, k_c), _dot_nt(dob_n, v_c))]
                heads.append(dict(hq=hq, qm_c=qm_c, qm_n=qm_n, dob_c=dob_c, dob_n=dob_n, mm=mm,
                                  dd=(dd_c, dd_c, dd_n), ls=(ls_c, ls_c, ls_n)))
            return dict(rows=rows, k_c=k_c, k_p=k_p, heads=heads, oks=(cur_ok, pok_c, pok_n), shared=shared)

        def several(blocks):
            work = []
            for i, (g, r) in enumerate(blocks):
                work.append(one(g, r, i > 0 and blocks[i - 1] == (g - 1, r)))
            for i, w in enumerate(work):
                if w["shared"]:
                    w["k_p"] = work[i - 1]["k_c"]
                for hi, h in enumerate(w["heads"]):
                    slope, dist = slopes[h["hq"]], (dcur, dprev, dprev)
                    h["pr"], h["ds"] = [], []
                    for j in range(3):
                        if h["mm"][j] is None:
                            h["pr"].append(work[i - 1]["heads"][hi]["pr"][2])
                            h["ds"].append(work[i - 1]["heads"][hi]["ds"][2])
                            continue
                        s = jnp.where(w["oks"][j], h["mm"][j][0] * 0.125 - slope * dist[j], NEG)
                        pr = jnp.exp(s - h["ls"][j])
                        h["pr"].append(pr.astype(BF16))
                        h["ds"].append((pr * (h["mm"][j][1] - h["dd"][j])).astype(BF16))
            for w in work:
                dq = jnp.zeros((QBLK, PAIR_W), F32)
                dk = jnp.zeros((QBLK, PAIR_W), F32)
                dv = jnp.zeros((QBLK, PAIR_W), F32)
                for h in w["heads"]:
                    ds, pr = h["ds"], h["pr"]
                    dq_h = _dot(ds[0], w["k_c"]) + _dot(ds[1], w["k_p"])
                    dk += (_dot_tn(ds[0], h["qm_c"]) + _dot_tn(ds[2], h["qm_n"])) * 0.125
                    dv += _dot_tn(pr[0], h["dob_c"]) + _dot_tn(pr[2], h["dob_n"])
                    dq = jnp.where(lane_head == h["hq"], dq_h * 0.125, dq)
                for part, val in enumerate((dq, dk, dv)):
                    if has_acc:
                        val = val + acc_ref.at[part][w["rows"], :]
                    out_ref.at[part][w["rows"], :] = val

        _for_query_blocks(d, groups, several)

    cur = lambda part: _pair_spec(rb, part, lambda n: n)
    prv = lambda part: _pair_spec(sb, part, lambda n: jnp.maximum(n * groups - 1, 0))
    nxt = lambda part: _pair_spec(sb, part, lambda n: jnp.minimum((n + 1) * groups, T // sb - 1))
    full = pl.BlockSpec((3, None, rb, PAIR_W), lambda p, n: (0, p // 2, n, p % 2))
    in_specs = [cur(0), nxt(0), cur(1), prv(1), cur(2), prv(2), cur(0), nxt(0), cur(0), nxt(0), cur(0), nxt(0)]
    args = [proj, proj, proj, proj, proj, proj, dcat, dcat, o, o, lse, lse]
    if has_acc:
        in_specs.append(full)
        args.append(acc)
    if du is not None:
        in_specs.append(cur(0))
        args.append(du)
    out_spec = pl.BlockSpec((n_parts, None, rb, PAIR_W), lambda p, n: (0, p // 2, n, p % 2))
    return pl.pallas_call(
        body, grid=(4, nblk), in_specs=in_specs, out_specs=out_spec,
        out_shape=SDS((n_parts, 2, T, 2 * PAIR_W), F32), compiler_params=_params(2), name=name)(*args)


def _scan_rows(buf, tab_ref, reverse, half):
    n_tiles = (buf.shape[0] - 8) // 8
    per_half = HALF_STATES // SCAN_CW
    row = lax.broadcasted_iota(jnp.int32, (8, SCAN_CW), 0)
    sgn = -1.0 if reverse else 1.0

    for j in range(per_half):
        c0 = half * 2 * HALF_STATES + j * SCAN_CW
        cre = pl.ds(c0, SCAN_CW)
        cim = pl.ds(c0 + HALF_STATES, SCAN_CW)
        steps = []
        for s, k in enumerate((1, 2, 4)):
            ok, shift = (row < 8 - k, 8 - k) if reverse else (row >= k, k)
            steps.append((shift, jnp.where(ok, tab_ref[pl.ds(s, 1), cre], 0.0),
                          jnp.where(ok, sgn * tab_ref[pl.ds(s, 1), cim], 0.0)))
        trow = 16 if reverse else 8
        pr, pi = tab_ref[pl.ds(trow, 8), cre], tab_ref[pl.ds(trow, 8), cim]
        for t in range(n_tiles):
            base = 8 * (n_tiles - 1 - t) if reverse else 8 + 8 * t
            rows = pl.ds(base, 8)
            re, im = buf[rows, cre], buf[rows, cim]
            for shift, ar, ai in steps:
                sre, sim = pltpu.roll(re, shift, 0), pltpu.roll(im, shift, 0)
                re, im = re + ar * sre - ai * sim, im + ar * sim + ai * sre
            crow = pl.ds(base + 8 if reverse else base - 1, 1)
            cr, ci = buf[crow, cre], buf[crow, cim]
            buf[rows, cre] = re + pr * cr - pi * ci
            buf[rows, cim] = im + pr * ci + pi * cr


def ssm_fwd(proj, bh, ch, apow, dskip, name, comm=None):
    _, T, C = proj.shape
    tm = SCAN_TM
    SW = 4 * HALF_STATES

    def body(u_ref, bh_ref, ch_ref, tab_ref, dsk_ref, y_ref, s_ref, buf):
        @pl.when(pl.program_id(0) == 0)
        def _():
            buf[pl.ds(0, 8), :] = jnp.zeros((8, SW), F32)

        for h in range(2):
            buf[pl.ds(8, tm), pl.ds(h * 2 * HALF_STATES, 2 * HALF_STATES)] = _dot(u_ref[h].astype(BF16), bh_ref[h])
        for h in range(2):
            cols = pl.ds(h * 2 * HALF_STATES, 2 * HALF_STATES)
            _scan_rows(buf, tab_ref, False, h)
            sv = buf[pl.ds(8, tm), cols]
            s_ref[:, cols] = sv
            y_ref[h] = _dot(sv.astype(BF16), ch_ref[h]) + dsk_ref[h] * u_ref[h]
        buf[pl.ds(0, 8), :] = buf[pl.ds(tm, 8), :]

    return _call(
        body, grid=(T // tm,),
        in_specs=[pl.BlockSpec((2, tm, C), lambda i: (3, i, 0)),
                  pl.BlockSpec((2, C, 2 * HALF_STATES), lambda i: (0, 0, 0)),
                  pl.BlockSpec((2, 2 * HALF_STATES, C), lambda i: (0, 0, 0)),
                  pl.BlockSpec((24, SW), lambda i: (0, 0)),
                  pl.BlockSpec((2, 1, C), lambda i: (0, 0, 0))],
        out_specs=[pl.BlockSpec((2, tm, C), lambda i: (0, i, 0)), pl.BlockSpec((tm, SW), lambda i: (i, 0))],
        out_shape=[SDS((2, T, C), F32), SDS((T, SW), F32)],
        scratch_shapes=[pltpu.VMEM((tm + 8, SW), F32)],
        args=(proj, bh, ch, apow, dskip), name=name, comm=comm)


def ssm_bwd(dy, proj, st, bh, ch, apow, dskip, name, comm=None):
    _, T, C = proj.shape
    tm = SCAN_TM_BWD
    nt = T // tm
    SW = 4 * HALF_STATES
    HS2 = 2 * HALF_STATES

    def body(dy_ref, u_ref, s_ref, sp_ref, bh_ref, ch_ref, tab_ref, dsk_ref,
             du_ref, da_ref, dbh_ref, dch_ref, dd_ref, lam):
        i = pl.program_id(0)

        @pl.when(i == 0)
        def _():
            lam[pl.ds(tm, 8), :] = jnp.zeros((8, SW), F32)
            da_ref[...] = jnp.zeros_like(da_ref)
            dbh_ref[...] = jnp.zeros_like(dbh_ref)
            dch_ref[...] = jnp.zeros_like(dch_ref)
            dd_ref[...] = jnp.zeros_like(dd_ref)

        for h in range(2):
            lam[pl.ds(0, tm), pl.ds(h * HS2, HS2)] = _dot_nt(dy_ref[h].astype(BF16), ch_ref[h])
        for h in range(2):
            dyv, uv = dy_ref[h], u_ref[h]
            dch_ref[h] += _dot_tn(s_ref[:, pl.ds(h * HS2, HS2)].astype(BF16), dyv.astype(BF16))
            dd_ref[h] += jnp.sum(dyv * uv, axis=0, keepdims=True)
        for h in range(2):
            _scan_rows(lam, tab_ref, True, h)
            lb = lam[pl.ds(0, tm), pl.ds(h * HS2, HS2)].astype(BF16)
            du_ref[h] = _dot_nt(lb, bh_ref[h]) + dsk_ref[h] * dy_ref[h]
            dbh_ref[h] += _dot_tn(u_ref[h].astype(BF16), lb)

        first = i == nt - 1
        per_half = HALF_STATES // SCAN_CW

        def chunk(j, _):
            c0 = pl.multiple_of((j // per_half) * HS2 + (j % per_half) * SCAN_CW, 128)
            cre, cim = pl.ds(c0, SCAN_CW), pl.ds(pl.multiple_of(c0 + HALF_STATES, 128), SCAN_CW)
            row0 = lax.broadcasted_iota(jnp.int32, (8, SCAN_CW), 0) == 0
            acc_r = jnp.zeros((8, SCAN_CW), F32)
            acc_i = jnp.zeros((8, SCAN_CW), F32)
            for t in range(tm // 8):
                rows = pl.ds(8 * t, 8)
                if t == 0:
                    pre = jnp.where(first, 0.0, sp_ref[pl.ds(7, 1), cre])
                    pim = jnp.where(first, 0.0, sp_ref[pl.ds(7, 1), cim])
                else:
                    pre, pim = s_ref[pl.ds(8 * t - 1, 1), cre], s_ref[pl.ds(8 * t - 1, 1), cim]
                spr = jnp.where(row0, pre, pltpu.roll(s_ref[rows, cre], 1, 0))
                spi = jnp.where(row0, pim, pltpu.roll(s_ref[rows, cim], 1, 0))
                lr, li = lam[rows, cre], lam[rows, cim]
                acc_r += lr * spr + li * spi
                acc_i += li * spr - lr * spi
            da_ref[:, cre] += jnp.sum(acc_r, axis=0, keepdims=True)
            da_ref[:, cim] += jnp.sum(acc_i, axis=0, keepdims=True)
            return 0

        lax.fori_loop(0, 2 * per_half, chunk, 0)
        lam[pl.ds(tm, 8), :] = lam[pl.ds(0, 8), :]

    rev = lambda i: nt - 1 - i
    return _call(
        body, grid=(nt,),
        in_specs=[pl.BlockSpec((2, tm, C), lambda i: (0, rev(i), 0)),
                  pl.BlockSpec((2, tm, C), lambda i: (3, rev(i), 0)),
                  pl.BlockSpec((tm, SW), lambda i: (rev(i), 0)),
                  pl.BlockSpec((8, SW), lambda i: (jnp.maximum(rev(i) * (tm // 8) - 1, 0), 0)),
                  _resident((2, C, HS2)), _resident((2, HS2, C)), _resident((24, SW)), _resident((2, 1, C))],
        out_specs=[pl.BlockSpec((2, tm, C), lambda i: (0, rev(i), 0)),
                   _resident((1, SW)), _resident((2, C, HS2)), _resident((2, HS2, C)), _resident((2, 1, C))],
        out_shape=[SDS((2, T, C), F32), SDS((1, SW), F32), SDS((2, C, HS2), F32), SDS((2, HS2, C), F32),
                   SDS((2, 1, C), F32)],
        scratch_shapes=[pltpu.VMEM((tm + 8, SW), F32)],
        args=(dy, proj, st, st, bh, ch, apow, dskip), name=name, comm=comm)


_GELU_C = math.sqrt(2.0 / math.pi)


def _gelu(x):
    t = jnp.tanh(_GELU_C * (x + 0.044715 * x * x * x))
    return 0.5 * x * (1.0 + t), t


def glu_fwd(y, w, b, name):
    _, T, C = y.shape
    tm = 512

    def body(y_ref, w_ref, b_ref, o_ref, lg_ref):
        y0, _ = _gelu(y_ref[0])
        y1, _ = _gelu(y_ref[1])
        lg = _dot(y0.astype(BF16), w_ref[0]) + _dot(y1.astype(BF16), w_ref[1]) + b_ref[...]
        sg = _sigmoid(lg)
        o_ref[0] = y0 * sg[:, :C]
        o_ref[1] = y1 * sg[:, C:]
        lg_ref[0] = lg[:, :C]
        lg_ref[1] = lg[:, C:]

    return pl.pallas_call(
        body, grid=(T // tm,),
        in_specs=[pl.BlockSpec((2, tm, C), lambda i: (0, i, 0)), pl.BlockSpec((2, C, 2 * C), lambda i: (0, 0, 0)),
                  pl.BlockSpec((1, 2 * C), lambda i: (0, 0))],
        out_specs=[pl.BlockSpec((2, tm, C), lambda i: (0, i, 0)), pl.BlockSpec((2, tm, C), lambda i: (0, i, 0))],
        out_shape=[SDS((2, T, C), F32), SDS((2, T, C), F32)], compiler_params=_params(1), name=name)(y, w, b)


def glu_bwd(dcat, y, lg, w, name):
    _, T, C = y.shape
    tm = 512

    def body(d_ref, y_ref, lg_ref, w_ref, dy_ref, dw_ref, db_ref):
        @pl.when(pl.program_id(0) == 0)
        def _():
            dw_ref[...] = jnp.zeros_like(dw_ref)
            db_ref[...] = jnp.zeros_like(db_ref)

        y2, th, sg, dlg = [], [], [], []
        for h in range(2):
            yy, tt = _gelu(y_ref[h])
            ss = _sigmoid(lg_ref[h])
            y2.append(yy)
            th.append(tt)
            sg.append(ss)
            dlg.append(d_ref[h] * yy * ss * (1.0 - ss))
        dl = jnp.concatenate(dlg, axis=1)
        dlb = dl.astype(BF16)
        db_ref[...] += jnp.sum(dl, axis=0, keepdims=True)
        for h in range(2):
            dy2 = d_ref[h] * sg[h] + _dot_nt(dlb, w_ref[h])
            yv = y_ref[h]
            dgelu = 0.5 * (1.0 + th[h]) + 0.5 * yv * (1.0 - th[h] * th[h]) * _GELU_C * (1.0 + 3 * 0.044715 * yv * yv)
            dy_ref[h] = dy2 * dgelu
            dw_ref[h] += _dot_tn(y2[h].astype(BF16), dlb)

    return pl.pallas_call(
        body, grid=(T // tm,),
        in_specs=[pl.BlockSpec((2, tm, C), lambda i: (1, i, 0)), pl.BlockSpec((2, tm, C), lambda i: (0, i, 0)),
                  pl.BlockSpec((2, tm, C), lambda i: (0, i, 0)), pl.BlockSpec((2, C, 2 * C), lambda i: (0, 0, 0))],
        out_specs=[pl.BlockSpec((2, tm, C), lambda i: (0, i, 0)), pl.BlockSpec((2, C, 2 * C), lambda i: (0, 0, 0)),
                   pl.BlockSpec((1, 2 * C), lambda i: (0, 0))],
        out_shape=[SDS((2, T, C), F32), SDS((2, C, 2 * C), F32), SDS((1, 2 * C), F32)],
        compiler_params=_params(1), name=name)(dcat, y, lg, w)


def adamw(w, m, v, slots, name):
    R, C = w.shape
    tr = R
    for cand in (512, 256, 128, 64, 32, 16, 8):
        if R % cand == 0 and cand * C * 4 <= 2 * 1024 * 1024:
            tr = cand
            break
    c1 = 1.0 / (1.0 - ADAM_B1 ** ADAM_STEP)
    c2 = 1.0 / (1.0 - ADAM_B2 ** ADAM_STEP)

    def body(w_ref, m_ref, v_ref, s_ref, g_ref, d_ref, nm_ref, nv_ref):
        g = s_ref[0].astype(F32)
        for j in range(1, N_DEV):
            g = g + s_ref[j].astype(F32)
        nm = ADAM_B1 * m_ref[...] + (1.0 - ADAM_B1) * g
        nv = ADAM_B2 * v_ref[...] + (1.0 - ADAM_B2) * (g * g)
        g_ref[...] = g
        nm_ref[...] = nm
        nv_ref[...] = nv
        d_ref[...] = -ADAM_LR * ((nm * c1) / (jnp.sqrt(nv * c2) + ADAM_EPS) + ADAM_WD * w_ref[...])

    spec = pl.BlockSpec((tr, C), lambda i: (i, 0))
    return pl.pallas_call(
        body, grid=(R // tr,),
        in_specs=[spec, spec, spec, pl.BlockSpec((N_DEV, tr, C), lambda i: (0, i, 0))],
        out_specs=[spec] * 4, out_shape=[SDS((R, C), F32)] * 4, compiler_params=_params(1), name=name)(w, m, v, slots)


def _discretise(a_re, a_im, log_dt, b_re, b_im):
    dt = jnp.exp(log_dt)[:, None]
    e = jnp.exp(dt * a_re)
    ar, ai = e * jnp.cos(dt * a_im), e * jnp.sin(dt * a_im)
    den = a_re * a_re + a_im * a_im
    nr, ni = ar - 1.0, ai
    wr = (nr * a_re + ni * a_im) / den
    wi = (ni * a_re - nr * a_im) / den
    bbr = wr[..., None] * b_re - wi[..., None] * b_im
    bbi = wr[..., None] * b_im + wi[..., None] * b_re
    return ar, ai, bbr, bbi


def _block_diag(t):
    eye = jnp.eye(16, dtype=t.dtype).reshape(1, 16, 1, 16, 1)
    r, c = t.shape[1], t.shape[2]
    return (t.reshape(2, 16, r, 1, c) * eye).reshape(2, 16 * r, 16 * c)


def _diag_blocks(m, r, c):
    eye = jnp.eye(16, dtype=m.dtype).reshape(1, 16, 1, 16, 1)
    return jnp.sum(m.reshape(2, 16, r, 16, c) * eye, axis=3).reshape(32, r, c)


def _state_vec(re, im):
    return jnp.stack([re.reshape(2, HALF_STATES), im.reshape(2, HALF_STATES)], axis=1).reshape(-1)


BIG = ("ffn1_w_in", "ffn1_w_out", "w_mix_in", "w_glu", "w_mix_out", "ffn2_w_in", "ffn2_w_out")
WEIGHTS = ("ffn1_pre_g", "ffn1_w_in", "ffn1_w_out", "ffn1_post_g", "mix_pre_g", "w_mix_in", "a_re", "a_im", "log_dt",
           "b_re", "b_im", "c_re", "c_im", "d_skip", "w_glu", "b_glu", "w_mix_out", "mix_post_g", "ffn2_pre_g",
           "ffn2_w_in", "ffn2_w_out", "ffn2_post_g")
SMALL = tuple(n for n in WEIGHTS if n not in BIG)
TRANSPOSED = ("ffn1_w_in", "ffn2_w_in")
PACK_COLS = 1024


def _pack(parts):
    flat = jnp.concatenate([p.reshape(-1) for p in parts])
    rows = -(-flat.shape[0] // (8 * PACK_COLS)) * 8
    return jnp.pad(flat, (0, rows * PACK_COLS - flat.shape[0])).reshape(rows, PACK_COLS)


def _unpack(packed, shapes):
    flat, out, off = packed.reshape(-1), [], 0
    for s in shapes:
        n = math.prod(s)
        out.append(flat[off:off + n].reshape(s))
        off += n
    return out


def _gather(names, wb):
    return [wb[n] for n in names], [False] * len(names)


def _ffn_bwd(dy, do, saved, x, pre_g, w_in, w_out4, tag, post=None):
    h, z, a = saved
    T = x.shape[0]
    dz = ffn_dact(do, w_out4, z, f"{tag}_dact")
    dz8 = dz.reshape(8, T, dz.shape[-1])
    dw_out, _ = mm_tn(a, do, True, False, 4, f"{tag}_dwout")
    dw_in, (s_out,) = mm_tn(dz8, h, True, False, 8, f"{tag}_dwin", comm=([dw_out.reshape(8, -1, D_MODEL)], [True]))
    outs, (s_in,) = dh_pre_bwd(dz8, w_in, x, pre_g, dy, f"{tag}_dh", comm=([dw_in], [True]), post=post,
                               w_transposed=True)
    return outs, (s_in, s_out)


def local_step(x, tgt, sp, wb):
    T = x.shape[0]
    ar, ai, bbr, bbi = _discretise(sp["a_re"], sp["a_im"], sp["log_dt"], sp["b_re"], sp["b_im"])
    powers = [(ar, ai)]
    for _ in range(7):
        pr, pi = powers[-1]
        powers.append((pr * ar - pi * ai, pr * ai + pi * ar))
    zero = jnp.zeros_like(ar)
    rows = [_state_vec(*powers[k - 1]) for k in (1, 2, 4)] + [_state_vec(zero, zero)] * 5
    rows += [_state_vec(pr, pi) for pr, pi in powers]
    rows += [_state_vec(pr, -pi) for pr, pi in reversed(powers)]
    apow = jnp.stack(rows)
    bh = jnp.concatenate([_block_diag(bbr.transpose(0, 2, 1)), _block_diag(bbi.transpose(0, 2, 1))], axis=2)
    ch = jnp.concatenate([_block_diag(sp["c_re"].transpose(0, 2, 1)), _block_diag(-sp["c_im"].transpose(0, 2, 1))], axis=1)
    bh, ch = bh.astype(BF16), ch.astype(BF16)
    dskip = sp["d_skip"].reshape(2, 1, 256)

    w1_in = gather_two_level(wb["ffn1_w_in"], "gather_w1in")
    (h1, z1, a1), (w1_out, w_mi) = ffn_in(
        x, sp["ffn1_pre_g"], w1_in, "ffn1_in", comm=_gather(["ffn1_w_out", "w_mix_in"], wb))
    w1_out4 = w1_out.reshape(4, -1, D_MODEL)
    (o1, x1), (w_glu, w_mo) = mm_acc_norm(
        a1, w1_out4, x, sp["ffn1_post_g"], 0.5, "ffn1_out", comm=_gather(["w_glu", "w_mix_out"], wb))
    w_glu2, w_mo4 = w_glu.reshape(2, 256, 512), w_mo.reshape(4, 256, D_MODEL)
    h2, proj = norm_proj(x1, sp["mix_pre_g"], w_mi, "mix_proj")
    (y_ssm, states), (w2_in,) = ssm_fwd(proj, bh, ch, apow, dskip, "ssm_fwd", comm=_gather(["ffn2_w_in"], wb))
    os_, ls_ = [], []
    for d in DILATIONS:
        (o_d, l_d), got = attn_fwd(proj, d, f"attn_fwd_d{d}",
                                   comm=_gather(["ffn2_w_out"], wb) if d == DILATIONS[-1] else None)
        os_.append(o_d)
        ls_.append(l_d)
    w2_out4 = got[0].reshape(4, -1, D_MODEL)
    o_ssm, lg = glu_fwd(y_ssm, w_glu2, sp["b_glu"], "glu_fwd")
    cat, lse, mixed, x2 = mix_out(os_, ls_, o_ssm, w_mo4, x1, sp["mix_post_g"], "mix_out")
    (h3, z3, a3), _ = ffn_in(x2, sp["ffn2_pre_g"], w2_in, "ffn2_in")
    (dy3, sq, do3, dg_f2post), _ = mm_acc_norm(a3, w2_out4, x2, sp["ffn2_post_g"], 0.5, "ffn2_out", tgt=tgt)

    (dx2, dg_f2pre, dmixed, dg_mpost), (s_w2in, s_w2out) = _ffn_bwd(
        dy3, do3, (h3, z3, a3), x2, sp["ffn2_pre_g"], w2_in, w2_out4, "ffn2", post=(mixed, sp["mix_post_g"], 1.0))
    dcat = mm_nt_b(dmixed, w_mo4, "mix_dcat")
    dw_mo, _ = mm_tn(cat, dmixed, True, False, 4, "mix_dwout")
    dy_ssm, dw_glu, db_glu = glu_bwd(dcat, y_ssm, lg, w_glu2, "glu_bwd")
    (du, da, dbh, dch, dd), (s_wmo, s_wglu) = ssm_bwd(
        dy_ssm, proj, states, bh, ch, apow, dskip, "ssm_bwd",
        comm=([dw_mo.reshape(8, 128, D_MODEL), dw_glu.astype(BF16).reshape(8, 64, 512)], [True, True]))
    dqkv = None
    for d in DILATIONS:
        dqkv = attn_bwd(proj, dcat, cat, lse, dqkv, d, f"attn_bwd_d{d}", du=du if d == DILATIONS[-1] else None)
    dproj = dqkv.reshape(8, T, 256)
    dw_mi, _ = mm_tn(h2, dproj, False, True, 8, "mix_dwin")
    (dx1, dg_mpre, do1, dg_f1post), (s_wmi,) = dh_pre_bwd(
        dproj, w_mi, x1, sp["mix_pre_g"], dx2, "mix_dh", comm=([dw_mi], [True]), post=(o1, sp["ffn1_post_g"], 0.5))
    (dx0, dg_f1pre), (s_w1in, s_w1out) = _ffn_bwd(
        dx1, do1, (h1, z1, a1), x, sp["ffn1_pre_g"], w1_in, w1_out4, "ffn1")

    da4 = da.reshape(2, 2, HALF_STATES)
    d_ar, d_ai = da4[:, 0].reshape(32, N_STATE), da4[:, 1].reshape(32, N_STATE)
    d_bbr = _diag_blocks(dbh[:, :, :HALF_STATES], 16, N_STATE).transpose(0, 2, 1)
    d_bbi = _diag_blocks(dbh[:, :, HALF_STATES:], 16, N_STATE).transpose(0, 2, 1)
    _, disc_vjp = jax.vjp(_discretise, sp["a_re"], sp["a_im"], sp["log_dt"], sp["b_re"], sp["b_im"])
    g_are, g_aim, g_ldt, g_bre, g_bim = disc_vjp((d_ar, d_ai, d_bbr, d_bbi))
    g_cre = _diag_blocks(dch[:, :HALF_STATES], N_STATE, 16).transpose(0, 2, 1)
    g_cim = -_diag_blocks(dch[:, HALF_STATES:], N_STATE, 16).transpose(0, 2, 1)
    small = {
        "ffn1_pre_g": dg_f1pre, "ffn1_post_g": dg_f1post, "mix_pre_g": dg_mpre, "a_re": g_are, "a_im": g_aim,
        "log_dt": g_ldt, "b_re": g_bre, "b_im": g_bim, "c_re": g_cre, "c_im": g_cim, "d_skip": dd.reshape(1, 512),
        "b_glu": db_glu, "mix_post_g": dg_mpost, "ffn2_pre_g": dg_f2pre, "ffn2_post_g": dg_f2post,
    }
    small_slots = gather_two_level(_pack([small[n] for n in SMALL]), "exchange_small")
    big_slots = {"ffn1_w_in": s_w1in, "ffn1_w_out": s_w1out, "w_mix_in": s_wmi, "w_glu": s_wglu, "w_mix_out": s_wmo,
                 "ffn2_w_in": s_w2in, "ffn2_w_out": s_w2out}
    return sq, dx0, big_slots, small_slots


def kernel(x, ffn1_pre_g, ffn1_w_in, ffn1_w_out, ffn1_post_g, mix_pre_g, w_mix_in, a_re, a_im, log_dt, b_re, b_im, c_re, c_im, d_skip, w_glu, b_glu, w_mix_out, mix_post_g, ffn2_pre_g, ffn2_w_in, ffn2_w_out, ffn2_post_g, loss_target, m_ffn1_pre_g, m_ffn1_w_in, m_ffn1_w_out, m_ffn1_post_g, m_mix_pre_g, m_w_mix_in, m_a_re, m_a_im, m_log_dt, m_b_re, m_b_im, m_c_re, m_c_im, m_d_skip, m_w_glu, m_b_glu, m_w_mix_out, m_mix_post_g, m_ffn2_pre_g, m_ffn2_w_in, m_ffn2_w_out, m_ffn2_post_g, v_ffn1_pre_g, v_ffn1_w_in, v_ffn1_w_out, v_ffn1_post_g, v_mix_pre_g, v_w_mix_in, v_a_re, v_a_im, v_log_dt, v_b_re, v_b_im, v_c_re, v_c_im, v_d_skip, v_w_glu, v_b_glu, v_w_mix_out, v_mix_post_g, v_ffn2_pre_g, v_ffn2_w_in, v_ffn2_w_out, v_ffn2_post_g):
    args = dict(locals())
    w = {n: args[n][0] for n in WEIGHTS}
    m = {n: args["m_" + n][0] for n in WEIGHTS}
    v = {n: args["v_" + n][0] for n in WEIGHTS}

    for d in (w, m, v):
        for n in TRANSPOSED:
            d[n] = jnp.swapaxes(d[n], 0, 1)
    wb = {n: w[n].astype(BF16) for n in BIG}
    sp = {n: w[n] for n in SMALL}
    for n in ("ffn1_pre_g", "ffn1_post_g", "mix_pre_g", "mix_post_g", "ffn2_pre_g", "ffn2_post_g", "b_glu", "d_skip"):
        sp[n] = w[n].reshape(1, -1)

    sq, grad_x, big_slots, small_slots = local_step(x[0], loss_target[0], sp, wb)
    loss = lax.psum(0.5 / D_MODEL * jnp.sum(sq), ("x", "y", "c"))

    outs = {}
    for n in BIG:
        shp = w[n].shape
        r2 = lambda t: t.reshape(-1, shp[-1])
        res = adamw(r2(w[n]), r2(m[n]), r2(v[n]), big_slots[n].reshape(N_DEV, -1, shp[-1]), f"adamw_{n}")
        outs[n] = [(jnp.swapaxes(t, 0, 1) if n in TRANSPOSED else t.reshape(shp))[None] for t in res]
    res = adamw(_pack([w[n] for n in SMALL]), _pack([m[n] for n in SMALL]), _pack([v[n] for n in SMALL]),
                small_slots, "adamw_small")
    shapes = [(1,) + w[n].shape for n in SMALL]
    unpacked = [_unpack(t, shapes) for t in res]
    for j, n in enumerate(SMALL):
        outs[n] = [unpacked[k][j] for k in range(4)]

    result = [loss, grad_x[None]]
    for k in range(4):
        result += [outs[n][k] for n in WEIGHTS]
    return tuple(result)
```

```python
import functools
import math

import jax
import jax.numpy as jnp
from jax import lax
from jax.experimental import pallas as pl
from jax.experimental.pallas import tpu as pltpu

F32, BF16 = jnp.float32, jnp.bfloat16
SDS = jax.ShapeDtypeStruct

D_MODEL = 1024
N_DEV = 8
HEAD_DIM = 64
PAIR_W = 128
QBLK = 128
DILATIONS = (1, 4, 16)
N_STATE = 64
HALF_STATES = 1024
NORM_EPS = 1e-6
NEG = -1e30
VMEM_LIMIT = 56 * 1024 * 1024
ADAM_LR, ADAM_B1, ADAM_B2, ADAM_EPS, ADAM_WD, ADAM_STEP = 1e-3, 0.9, 0.999, 1e-8, 0.01, 10
SCAN_TM = 256
SCAN_TM_BWD = 512
SCAN_CW = 512


def _params(n_grid):
    return pltpu.CompilerParams(dimension_semantics=("arbitrary",) * n_grid, vmem_limit_bytes=VMEM_LIMIT)


def _dot(a, b):
    return jnp.dot(a, b, preferred_element_type=F32)


def _dot_nt(a, b):
    return lax.dot_general(a, b, (((1,), (1,)), ((), ())), preferred_element_type=F32)


def _dot_tn(a, b):
    return lax.dot_general(a, b, (((0,), (0,)), ((), ())), preferred_element_type=F32)


def _sigmoid(v):
    return 0.5 * jnp.tanh(0.5 * v) + 0.5


def _resident(shape):
    return pl.BlockSpec(shape, lambda i: (0,) * len(shape), pipeline_mode=pl.Buffered(1))


ROW_SPLIT = 2


def _exchange_phase(ins, outs, scatter, sems, start):
    send_sems, recv_sems, loc_sems = sems
    x, y, c = lax.axis_index("x"), lax.axis_index("y"), lax.axis_index("c")
    me = 4 * x + 2 * y + c
    own_copies, sends, arrivals = [], [], []
    for i in range(len(ins)):
        own = ins[i].at[me] if scatter[i] else ins[i]
        own_copies.append(pltpu.make_async_copy(own, outs[i].at[me], loc_sems.at[i]))
        for k in range(1, N_DEV):
            px = 1 - x if k & 4 else x
            py = 1 - y if k & 2 else y
            pc = 1 - c if k & 1 else c
            peer = 4 * px + 2 * py + pc
            src = ins[i].at[peer] if scatter[i] else ins[i]
            common = dict(src_ref=src, send_sem=send_sems.at[i, k - 1], recv_sem=recv_sems.at[i, k - 1],
                          device_id=(px, py, pc), device_id_type=pl.DeviceIdType.MESH)
            sends.append(pltpu.make_async_remote_copy(dst_ref=outs[i].at[me], **common))
            if not start:
                arrivals.append(pltpu.make_async_remote_copy(dst_ref=outs[i].at[peer], **common))
    if start:
        for cp in own_copies + sends:
            cp.start()
    else:
        for cp in arrivals:
            cp.wait_recv()
        for cp in sends:
            cp.wait_send()
        for cp in own_copies:
            cp.wait()


def _comm_shapes(arrs, scatter):
    n = len(arrs)
    out_shapes = [SDS(a.shape if scatter[i] else (N_DEV,) + a.shape, a.dtype) for i, a in enumerate(arrs)]
    sems = [pltpu.SemaphoreType.DMA((n, N_DEV - 1)), pltpu.SemaphoreType.DMA((n, N_DEV - 1)),
            pltpu.SemaphoreType.DMA((n,))]
    return out_shapes, sems


def gather_two_level(arr, name):
    def body(x_ref, out_ref, send_sems, recv_sems, local_sem):
        x, y, c = lax.axis_index("x"), lax.axis_index("y"), lax.axis_index("c")
        sibling = (x, y, 1 - c)
        chips = [(1 - x, y), (x, 1 - y), (1 - x, 1 - y)]

        def slot(px, py, pc):
            return out_ref.at[4 * px + 2 * py + pc]

        def copy(k, block, to, src=None):
            return pltpu.make_async_remote_copy(
                src_ref=slot(*block) if src is None else src, dst_ref=slot(*block),
                send_sem=send_sems.at[k], recv_sem=recv_sems.at[k], device_id=to, device_id_type=pl.DeviceIdType.MESH)

        mine = pltpu.make_async_copy(x_ref, slot(x, y, c), local_sem)
        mine.start()
        first = [copy(0, (x, y, c), sibling, src=x_ref)]
        first += [copy(1 + j, (x, y, c), (*chip, c), src=x_ref) for j, chip in enumerate(chips)]
        for cp in first:
            cp.start()
        passed = [copy(4 + j, (*chip, c), sibling) for j, chip in enumerate(chips)]
        for j, chip in enumerate(chips):
            copy(1 + j, (*chip, c), (x, y, c)).wait_recv()
            passed[j].start()
        copy(0, sibling, (x, y, c)).wait_recv()
        for j, chip in enumerate(chips):
            copy(4 + j, (*chip, 1 - c), (x, y, c)).wait_recv()
        for cp in first + passed:
            cp.wait_send()
        mine.wait()

    anyspec = pl.BlockSpec(memory_space=pl.ANY)
    return pl.pallas_call(
        body, in_specs=[anyspec], out_specs=anyspec, out_shape=SDS((N_DEV,) + arr.shape, arr.dtype),
        scratch_shapes=[pltpu.SemaphoreType.DMA((N_DEV - 1,)), pltpu.SemaphoreType.DMA((N_DEV - 1,)),
                        pltpu.SemaphoreType.DMA],
        compiler_params=pltpu.CompilerParams(has_side_effects=True), name=name)(arr)


def _call(body, *, grid, in_specs, out_specs, out_shape, args, name, scratch_shapes=(), comm=None):
    n_grid, scratch_shapes = len(grid), list(scratch_shapes)
    if comm is None:
        outs = pl.pallas_call(body, grid=grid, in_specs=in_specs, out_specs=out_specs, out_shape=out_shape,
                              scratch_shapes=scratch_shapes, compiler_params=_params(n_grid), name=name)(*args)
        return outs, []
    arrs, scatter = comm
    nc, n_in, n_out, n_sc = len(arrs), len(in_specs), len(out_specs), len(scratch_shapes)
    comm_shapes, sems = _comm_shapes(arrs, scatter)

    def wrapped(*refs):
        ins, cins = refs[:n_in], refs[n_in:n_in + nc]
        o0 = n_in + nc
        outs, couts = refs[o0:o0 + n_out], refs[o0 + n_out:o0 + n_out + nc]
        s0 = o0 + n_out + nc
        scratch, sem_refs = refs[s0:s0 + n_sc], refs[s0 + n_sc:]
        first = functools.reduce(jnp.logical_and, [pl.program_id(k) == 0 for k in range(n_grid)])
        last = functools.reduce(jnp.logical_and, [pl.program_id(k) == grid[k] - 1 for k in range(n_grid)])

        @pl.when(first)
        def _():
            _exchange_phase(cins, couts, scatter, sem_refs, True)

        body(*ins, *outs, *scratch)

        @pl.when(last)
        def _():
            _exchange_phase(cins, couts, scatter, sem_refs, False)

    anyspec = pl.BlockSpec(memory_space=pl.ANY)
    res = pl.pallas_call(
        wrapped, grid=grid, in_specs=list(in_specs) + [anyspec] * nc, out_specs=list(out_specs) + [anyspec] * nc,
        out_shape=list(out_shape) + comm_shapes, scratch_shapes=scratch_shapes + sems,
        compiler_params=pltpu.CompilerParams(dimension_semantics=("arbitrary",) * n_grid,
                                             vmem_limit_bytes=VMEM_LIMIT, has_side_effects=True),
        name=name)(*args, *arrs)
    return res[:n_out], res[n_out:]


def _rms(xv, g):
    r = lax.rsqrt(jnp.mean(xv * xv, axis=-1, keepdims=True) + NORM_EPS)
    return (xv * r * g).astype(BF16)


def ffn_in(x, g, w, name, comm=None):
    T, D = x.shape
    F = w.shape[1]
    tm = 512

    def body(x_ref, g_ref, w_ref, h_ref, z_ref, a_ref):
        hv = _rms(x_ref[...], g_ref[...])
        h_ref[...] = hv
        pending = None
        for j in range(5):
            if j < 4:
                zs = (_dot_nt(hv, w_ref[j]), _dot_nt(hv, w_ref[j + 4]))
            if pending is not None:
                zg, zu = pending
                sg = _sigmoid(zg)
                silu = zg * sg
                z_ref[0, j - 1] = (zu * (sg + silu - silu * sg)).astype(BF16)
                z_ref[1, j - 1] = silu.astype(BF16)
                a_ref[j - 1] = (silu * zu).astype(BF16)
            pending = zs

    return _call(
        body, grid=(T // tm,),
        in_specs=[pl.BlockSpec((tm, D), lambda i: (i, 0)), pl.BlockSpec((1, D), lambda i: (0, 0)),
                  _resident((8, F, D))],
        out_specs=[pl.BlockSpec((tm, D), lambda i: (i, 0)), pl.BlockSpec((2, 4, tm, F), lambda i: (0, 0, i, 0)),
                   pl.BlockSpec((4, tm, F), lambda i: (0, i, 0))],
        out_shape=[SDS((T, D), BF16), SDS((2, 4, T, F), BF16), SDS((4, T, F), BF16)],
        args=(x, g, w), name=name, comm=comm)


def norm_proj(x, g, w, name):
    T, K = x.shape
    nb, _, N = w.shape
    tm = 512

    def body(x_ref, g_ref, w_ref, h_ref, o_ref):
        hv = _rms(x_ref[...], g_ref[...])
        h_ref[...] = hv
        for b in range(nb):
            o_ref[b] = _dot(hv, w_ref[b])

    return pl.pallas_call(
        body, grid=(T // tm,),
        in_specs=[pl.BlockSpec((tm, K), lambda i: (i, 0)), pl.BlockSpec((1, K), lambda i: (0, 0)),
                  _resident((nb, K, N))],
        out_specs=[pl.BlockSpec((tm, K), lambda i: (i, 0)), pl.BlockSpec((nb, tm, N), lambda i: (0, i, 0))],
        out_shape=[SDS((T, K), BF16), SDS((nb, T, N), F32)], compiler_params=_params(1), name=name)(x, g, w)


def mm_acc_norm(a, w, xres, g, scale, name, comm=None, tgt=None):
    nb, T, K = a.shape
    D = w.shape[2]
    tm = 512
    rc = tm // ROW_SPLIT
    with_loss = tgt is not None

    def body(a_ref, w_ref, x_ref, g_ref, *rest):
        if with_loss:
            t_ref, dy_ref, sq_ref, do_ref, dg_ref = rest

            @pl.when(pl.program_id(0) == 0)
            def _():
                sq_ref[...] = jnp.zeros_like(sq_ref)
                dg_ref[...] = jnp.zeros_like(dg_ref)
        else:
            o_ref, y_ref = rest
        accs = []
        for c in range(ROW_SPLIT):
            rows = pl.ds(c * rc, rc)
            o = _dot(a_ref[0, rows, :].astype(BF16), w_ref[0])
            for b in range(1, nb):
                o += _dot(a_ref[b, rows, :].astype(BF16), w_ref[b])
            accs.append(o)
        for c, o in enumerate(accs):
            rows = pl.ds(c * rc, rc)
            r = lax.rsqrt(jnp.mean(o * o, axis=-1, keepdims=True) + NORM_EPS)
            y = x_ref[rows, :] + scale * (o * r * g_ref[...])
            if with_loss:
                e = y - t_ref[rows, :]
                dy = e * (1.0 / D)
                dy_ref[rows, :] = dy
                sq_ref[...] += jnp.sum(e * e, axis=0, keepdims=True)
                do, dg = _post_bwd(dy, o, g_ref[...], scale)
                do_ref[rows, :] = do
                dg_ref[...] += dg
            else:
                o_ref[rows, :] = o
                y_ref[rows, :] = y

    tile = pl.BlockSpec((tm, D), lambda i: (i, 0))
    row = pl.BlockSpec((1, D), lambda i: (0, 0))
    in_specs = [pl.BlockSpec((nb, tm, K), lambda i: (0, i, 0)), _resident((nb, K, D)), tile, row]
    args = (a, w, xres, g)
    if with_loss:
        return _call(body, grid=(T // tm,), in_specs=in_specs + [tile], out_specs=[tile, row, tile, row],
                     out_shape=[SDS((T, D), F32), SDS((1, D), F32), SDS((T, D), BF16), SDS((1, D), F32)],
                     args=args + (tgt,), name=name, comm=comm)
    return _call(body, grid=(T // tm,), in_specs=in_specs, out_specs=[tile, tile],
                 out_shape=[SDS((T, D), F32), SDS((T, D), F32)], args=args, name=name, comm=comm)


def _post_bwd(dy, ov, g, scale):
    r = scale * dy
    rstd = lax.rsqrt(jnp.mean(ov * ov, axis=-1, keepdims=True) + NORM_EPS)
    oh = ov * rstd
    rg = r * g
    do = rstd * (rg - oh * jnp.mean(rg * oh, axis=-1, keepdims=True))
    return do.astype(BF16), jnp.sum(r * oh, axis=0, keepdims=True)


def mm_nt_b(gr, w, name):
    T, N = gr.shape
    nb, K, _ = w.shape
    tm = 512

    def body(g_ref, w_ref, o_ref):
        gv = g_ref[...]
        for b in range(nb):
            o_ref[b] = _dot_nt(gv, w_ref[b])

    return pl.pallas_call(
        body, grid=(T // tm,),
        in_specs=[pl.BlockSpec((tm, N), lambda i: (i, 0)), _resident((nb, K, N))],
        out_specs=pl.BlockSpec((nb, tm, K), lambda i: (0, i, 0)),
        out_shape=SDS((nb, T, K), F32), compiler_params=_params(1), name=name)(gr, w)


def ffn_dact(do, w_out, z, name):
    T, D = do.shape
    nb, F, _ = w_out.shape
    tm = 512

    def body(g_ref, w_ref, z_ref, dz_ref):
        gv = g_ref[...]
        pending = None
        for b in range(nb + 1):
            da = _dot_nt(gv, w_ref[b]) if b < nb else None
            if pending is not None:
                dz_ref[0, b - 1] = (pending * z_ref[0, b - 1].astype(F32)).astype(BF16)
                dz_ref[1, b - 1] = (pending * z_ref[1, b - 1].astype(F32)).astype(BF16)
            pending = da

    blk = pl.BlockSpec((2, nb, tm, F), lambda i: (0, 0, i, 0))
    return pl.pallas_call(
        body, grid=(T // tm,),
        in_specs=[pl.BlockSpec((tm, D), lambda i: (i, 0)), _resident((nb, F, D)), blk],
        out_specs=blk, out_shape=SDS((2, nb, T, F), BF16), compiler_params=_params(1), name=name)(do, w_out, z)


def mm_tn(a, g, a_batched, g_batched, nb, name, comm=None):
    T = a.shape[-2]
    K, N = a.shape[-1], g.shape[-1]
    tk = 2048
    nk = T // tk

    def body(a_ref, g_ref, o_ref, acc):
        k = pl.program_id(1)

        @pl.when(k == 0)
        def _():
            acc[...] = jnp.zeros_like(acc)

        acc[...] += _dot_tn(a_ref[...].astype(BF16), g_ref[...].astype(BF16))

        @pl.when(k == nk - 1)
        def _():
            o_ref[...] = acc[...].astype(BF16)

    a_spec = (pl.BlockSpec((None, tk, K), lambda b, k: (b, k, 0)) if a_batched
              else pl.BlockSpec((tk, K), lambda b, k: (k, 0)))
    g_spec = (pl.BlockSpec((None, tk, N), lambda b, k: (b, k, 0)) if g_batched
              else pl.BlockSpec((tk, N), lambda b, k: (k, 0)))
    (out,), slots = _call(
        body, grid=(nb, nk), in_specs=[a_spec, g_spec],
        out_specs=[pl.BlockSpec((None, K, N), lambda b, k: (b, 0, 0))],
        out_shape=[SDS((nb, K, N), BF16)], scratch_shapes=[pltpu.VMEM((K, N), F32)],
        args=(a, g), name=name, comm=comm)
    return out, slots


def mm_tn_shared_a(a, g, name):
    T, K = a.shape
    nb, _, N = g.shape
    tk = 1024
    nk = T // tk

    def body(a_ref, g_ref, o_ref, acc):
        k = pl.program_id(0)

        @pl.when(k == 0)
        def _():
            acc[...] = jnp.zeros_like(acc)

        av = a_ref[...]
        for b in range(nb):
            acc[b] += _dot_tn(av, g_ref[b].astype(BF16))

        @pl.when(k == nk - 1)
        def _():
            o_ref[...] = acc[...].astype(BF16)

    return pl.pallas_call(
        body, grid=(nk,),
        in_specs=[pl.BlockSpec((tk, K), lambda k: (k, 0)), pl.BlockSpec((nb, tk, N), lambda k: (0, k, 0))],
        out_specs=_resident((nb, K, N)), out_shape=SDS((nb, K, N), BF16),
        scratch_shapes=[pltpu.VMEM((nb, K, N), F32)], compiler_params=_params(1), name=name)(a, g)


def dh_pre_bwd(dz, w, x, g, dyres, name, comm=None, post=None, w_transposed=False):
    nb, T, F = dz.shape
    D = x.shape[1]
    tm = 512
    rc = tm // ROW_SPLIT
    mm = _dot if w_transposed else _dot_nt

    def body(dz_ref, w_ref, x_ref, g_ref, dy_ref, *rest):
        if post is None:
            dx_ref, dg_ref = rest
        else:
            o_ref, gp_ref, dx_ref, dg_ref, do_ref, dgp_ref = rest

        @pl.when(pl.program_id(0) == 0)
        def _():
            dg_ref[...] = jnp.zeros_like(dg_ref)
            if post is not None:
                dgp_ref[...] = jnp.zeros_like(dgp_ref)

        accs = []
        for c in range(ROW_SPLIT):
            rows = pl.ds(c * rc, rc)
            dh = mm(dz_ref[0, rows, :].astype(BF16), w_ref[0])
            for b in range(1, nb):
                dh += mm(dz_ref[b, rows, :].astype(BF16), w_ref[b])
            accs.append(dh)
        for c, dh in enumerate(accs):
            rows = pl.ds(c * rc, rc)
            xv = x_ref[rows, :]
            rstd = lax.rsqrt(jnp.mean(xv * xv, axis=-1, keepdims=True) + NORM_EPS)
            xh = xv * rstd
            dg_ref[...] += jnp.sum(dh * xh, axis=0, keepdims=True)
            dhg = dh * g_ref[...]
            dx = dy_ref[rows, :] + rstd * (dhg - xh * jnp.mean(dhg * xh, axis=-1, keepdims=True))
            dx_ref[rows, :] = dx
            if post is not None:
                do, dgp = _post_bwd(dx, o_ref[rows, :], gp_ref[...], post[2])
                do_ref[rows, :] = do
                dgp_ref[...] += dgp

    tile = pl.BlockSpec((tm, D), lambda i: (i, 0))
    row = pl.BlockSpec((1, D), lambda i: (0, 0))
    in_specs = [pl.BlockSpec((nb, tm, F), lambda i: (0, i, 0)), _resident(w.shape), tile, row, tile]
    out_specs, out_shape, args = [tile, row], [SDS((T, D), F32), SDS((1, D), F32)], (dz, w, x, g, dyres)
    if post is not None:
        in_specs += [tile, row]
        out_specs += [tile, row]
        out_shape += [SDS((T, D), BF16), SDS((1, D), F32)]
        args += (post[0], post[1])
    return _call(body, grid=(T // tm,), in_specs=in_specs, out_specs=out_specs, out_shape=out_shape,
                 args=args, name=name, comm=comm)


ATTN_GROUP = {1: 4, 4: 2, 16: 1}
ATTN_UNROLL = 4


def _attn_masks():
    qi = lax.broadcasted_iota(jnp.int32, (QBLK, QBLK), 0)
    kj = lax.broadcasted_iota(jnp.int32, (QBLK, QBLK), 1)
    cur_ok = kj <= qi
    prev_ok = kj >= qi
    dcur = (qi - kj).astype(F32)
    return cur_ok, prev_ok, dcur, dcur + float(QBLK)


def _head_slopes(p, d):
    out = []
    for hq in range(2):
        v = [float(d) * 2.0 ** -(2 * q + hq + 1) for q in range(4)]
        out.append(jnp.where(p == 0, v[0], jnp.where(p == 1, v[1], jnp.where(p == 2, v[2], v[3]))))
    return out


def _rows(start, d):
    return pl.ds(start, QBLK, stride=d) if d > 1 else pl.ds(start, QBLK)


def _pair_spec(rows, part, blk):
    return pl.BlockSpec((None, rows, PAIR_W), lambda p, n: (2 * part + p // 2, blk(n), p % 2))


def _for_query_blocks(d, groups, several):
    blocks = [(g, r) for r in range(d) for g in range(groups)]
    for s in range(0, len(blocks), ATTN_UNROLL):
        several(blocks[s:s + ATTN_UNROLL])


def attn_fwd(proj, d, name, comm=None):
    T = proj.shape[1]
    sb, groups = QBLK * d, ATTN_GROUP[d]
    rb = sb * groups
    nblk = T // rb

    def body(q_ref, kc_ref, kp_ref, vc_ref, vp_ref, o_ref, l_ref):
        p, n = pl.program_id(0), pl.program_id(1)
        cur_ok, prev_ok, dcur, dprev = _attn_masks()
        first_ok = jnp.logical_and(prev_ok, n > 0)
        lane_head = lax.broadcasted_iota(jnp.int32, (QBLK, PAIR_W), 1) // HEAD_DIM
        slopes = _head_slopes(p, d)

        def several(blocks):
            work = []
            for g, r in blocks:
                rows = _rows(g * sb + r, d)
                q = q_ref[rows, :]
                kc, vc = kc_ref[rows, :].astype(BF16), vc_ref[rows, :].astype(BF16)
                if g == 0:
                    prow, pok = _rows(r, d), first_ok
                    kp, vp = kp_ref[prow, :].astype(BF16), vp_ref[prow, :].astype(BF16)
                else:
                    prow, pok = _rows((g - 1) * sb + r, d), prev_ok
                    kp, vp = kc_ref[prow, :].astype(BF16), vc_ref[prow, :].astype(BF16)
                for hq in range(2):
                    qm = jnp.where(lane_head == hq, q, 0.0).astype(BF16)
                    work.append([rows, hq, pok, vc, vp, _dot_nt(qm, kc), _dot_nt(qm, kp)])
            for w in work:
                _, hq, pok, _, _, sc, sp = w
                sc = jnp.where(cur_ok, sc * 0.125 - slopes[hq] * dcur, NEG)
                sp = jnp.where(pok, sp * 0.125 - slopes[hq] * dprev, NEG)
                m = jnp.maximum(jnp.max(sc, axis=1, keepdims=True), jnp.max(sp, axis=1, keepdims=True))
                pc = jnp.exp(sc - m)
                pp = jnp.exp(sp - m)
                den = jnp.sum(pc, axis=1, keepdims=True) + jnp.sum(pp, axis=1, keepdims=True)
                w[5:] = [pc.astype(BF16), pp.astype(BF16), 1.0 / den, m + jnp.log(den)]
            for i in range(0, len(work), 2):
                o_acc = jnp.zeros((QBLK, PAIR_W), F32)
                l_acc = jnp.zeros((QBLK, PAIR_W), F32)
                for rows, hq, _, vc, vp, pc, pp, inv, lse in work[i:i + 2]:
                    hm = lane_head == hq
                    o_acc = jnp.where(hm, (_dot(pc, vc) + _dot(pp, vp)) * inv, o_acc)
                    l_acc = jnp.where(hm, lse, l_acc)
                o_ref[rows, :] = o_acc
                l_ref[rows, :] = l_acc

        _for_query_blocks(d, groups, several)

    cur = lambda part: _pair_spec(rb, part, lambda n: n)
    prv = lambda part: _pair_spec(sb, part, lambda n: jnp.maximum(n * groups - 1, 0))
    return _call(
        body, grid=(4, nblk), in_specs=[cur(0), cur(1), prv(1), cur(2), prv(2)], out_specs=[cur(0), cur(0)],
        out_shape=[SDS((2, T, 2 * PAIR_W), F32), SDS((2, T, 2 * PAIR_W), F32)],
        args=(proj, proj, proj, proj, proj), name=name, comm=comm)


def mix_out(os_, ls_, o_ssm, w, xres, g, name):
    _, T, HW = o_ssm.shape
    D = w.shape[2]
    tm = 512

    def body(o1, o2, o3, l1, l2, l3, s_ref, w_ref, x_ref, g_ref, cat_ref, l_ref, m_ref, y_ref):
        a, b, c = l1[...], l2[...], l3[...]
        m = jnp.maximum(jnp.maximum(a, b), c)
        ea, eb, ec = jnp.exp(a - m), jnp.exp(b - m), jnp.exp(c - m)
        s = ea + eb + ec
        att = (ea * o1[...] + eb * o2[...] + ec * o3[...]) * (1.0 / s)
        ssm = s_ref[...]
        cat_ref[pl.ds(0, 2)] = att
        cat_ref[pl.ds(2, 2)] = ssm
        l_ref[...] = m + jnp.log(s)
        o = _dot(att[0].astype(BF16), w_ref[0]) + _dot(att[1].astype(BF16), w_ref[1])
        o += _dot(ssm[0].astype(BF16), w_ref[2]) + _dot(ssm[1].astype(BF16), w_ref[3])
        r = lax.rsqrt(jnp.mean(o * o, axis=-1, keepdims=True) + NORM_EPS)
        m_ref[...] = o
        y_ref[...] = x_ref[...] + o * r * g_ref[...]

    spec = pl.BlockSpec((2, tm, HW), lambda i: (0, i, 0))
    tile = pl.BlockSpec((tm, D), lambda i: (i, 0))
    return pl.pallas_call(
        body, grid=(T // tm,),
        in_specs=[spec] * 7 + [_resident(w.shape), tile, pl.BlockSpec((1, D), lambda i: (0, 0))],
        out_specs=[pl.BlockSpec((4, tm, HW), lambda i: (0, i, 0)), spec, tile, tile],
        out_shape=[SDS((4, T, HW), F32), SDS((2, T, HW), F32), SDS((T, D), F32), SDS((T, D), F32)],
        compiler_params=_params(1), name=name)(*os_, *ls_, o_ssm, w, xres, g)


def attn_bwd(proj, dcat, o, lse, acc, d, name, du=None):
    T = proj.shape[1]
    sb, groups = QBLK * d, ATTN_GROUP[d]
    rb = sb * groups
    nblk = T // rb
    has_acc = acc is not None
    n_parts = 3 if du is None else 4

    def body(*refs):
        (qc_ref, qn_ref, kc_ref, kp_ref, vc_ref, vp_ref, dc_ref, dn_ref, oc_ref, on_ref, lc_ref, ln_ref) = refs[:12]
        acc_ref = refs[12] if has_acc else None
        out_ref = refs[-1]
        if du is not None:
            out_ref[3] = refs[-2][...]
        p, n = pl.program_id(0), pl.program_id(1)
        cur_ok, prev_ok, dcur, dprev = _attn_masks()
        first_ok = jnp.logical_and(prev_ok, n > 0)
        last_ok = jnp.logical_and(prev_ok, n < nblk - 1)
        lane_head = lax.broadcasted_iota(jnp.int32, (QBLK, PAIR_W), 1) // HEAD_DIM
        slopes = _head_slopes(p, d)

        def one(g, r, shared):
            rows = _rows(g * sb + r, d)
            q_c, do_c, o_c, l_c = qc_ref[rows, :], dc_ref[rows, :], oc_ref[rows, :], lc_ref[rows, :]
            k_c, v_c = kc_ref[rows, :].astype(BF16), vc_ref[rows, :].astype(BF16)
            if shared:
                pok_c, k_p, v_p = prev_ok, None, None
            elif g == 0:
                prow, pok_c = _rows(r, d), first_ok
                k_p, v_p = kp_ref[prow, :].astype(BF16), vp_ref[prow, :].astype(BF16)
            else:
                prow, pok_c = _rows((g - 1) * sb + r, d), prev_ok
                k_p, v_p = kc_ref[prow, :].astype(BF16), vc_ref[prow, :].astype(BF16)
            if g == groups - 1:
                nrow, pok_n = _rows(r, d), last_ok
                q_n, do_n, o_n, l_n = qn_ref[nrow, :], dn_ref[nrow, :], on_ref[nrow, :], ln_ref[nrow, :]
            else:
                nrow, pok_n = _rows((g + 1) * sb + r, d), prev_ok
                q_n, do_n, o_n, l_n = qc_ref[nrow, :], dc_ref[nrow, :], oc_ref[nrow, :], lc_ref[nrow, :]
            heads = []
            for hq in range(2):
                hm = lane_head == hq
                qm_c = jnp.where(hm, q_c, 0.0).astype(BF16)
                qm_n = jnp.where(hm, q_n, 0.0).astype(BF16)
                dom_c = jnp.where(hm, do_c, 0.0)
                dom_n = jnp.where(hm, do_n, 0.0)
                dd_c = jnp.sum(dom_c * o_c, axis=1, keepdims=True)
                dd_n = jnp.sum(dom_n * o_n, axis=1, keepdims=True)
                ls_c = jnp.max(jnp.where(hm, l_c, NEG), axis=1, keepdims=True)
                ls_n = jnp.max(jnp.where(hm, l_n, NEG), axis=1, keepdims=True)
                dob_c, dob_n = dom_c.astype(BF16), dom_n.astype(BF16)
                mm = [(_dot_nt(qm_c, k_c), _dot_nt(dob_c, v_c)),
                      None if shared else (_dot_nt(qm_c, k_p), _dot_nt(dob_c, v_p)),
                      (_dot_nt(qm_n, k_c), _dot_nt(dob_n, v_c))]
                heads.append(dict(hq=hq, qm_c=qm_c, qm_n=qm_n, dob_c=dob_c, dob_n=dob_n, mm=mm,
                                  dd=(dd_c, dd_c, dd_n), ls=(ls_c, ls_c, ls_n)))
            return dict(rows=rows, k_c=k_c, k_p=k_p, heads=heads, oks=(cur_ok, pok_c, pok_n), shared=shared)

        def several(blocks):
            work = []
            for i, (g, r) in enumerate(blocks):
                work.append(one(g, r, i > 0 and blocks[i - 1] == (g - 1, r)))
            for i, w in enumerate(work):
                if w["shared"]:
                    w["k_p"] = work[i - 1]["k_c"]
                for hi, h in enumerate(w["heads"]):
                    slope, dist = slopes[h["hq"]], (dcur, dprev, dprev)
                    h["pr"], h["ds"] = [], []
                    for j in range(3):
                        if h["mm"][j] is None:
                            h["pr"].append(work[i - 1]["heads"][hi]["pr"][2])
                            h["ds"].append(work[i - 1]["heads"][hi]["ds"][2])
                            continue
                        s = jnp.where(w["oks"][j], h["mm"][j][0] * 0.125 - slope * dist[j], NEG)
                        pr = jnp.exp(s - h["ls"][j])
                        h["pr"].append(pr.astype(BF16))
                        h["ds"].append((pr * (h["mm"][j][1] - h["dd"][j])).astype(BF16))
            for w in work:
                dq = jnp.zeros((QBLK, PAIR_W), F32)
                dk = jnp.zeros((QBLK, PAIR_W), F32)
                dv = jnp.zeros((QBLK, PAIR_W), F32)
                for h in w["heads"]:
                    ds, pr = h["ds"], h["pr"]
                    dq_h = _dot(ds[0], w["k_c"]) + _dot(ds[1], w["k_p"])
                    dk += (_dot_tn(ds[0], h["qm_c"]) + _dot_tn(ds[2], h["qm_n"])) * 0.125
                    dv += _dot_tn(pr[0], h["dob_c"]) + _dot_tn(pr[2], h["dob_n"])
                    dq = jnp.where(lane_head == h["hq"], dq_h * 0.125, dq)
                for part, val in enumerate((dq, dk, dv)):
                    if has_acc:
                        val = val + acc_ref.at[part][w["rows"], :]
                    out_ref.at[part][w["rows"], :] = val

        _for_query_blocks(d, groups, several)

    cur = lambda part: _pair_spec(rb, part, lambda n: n)
    prv = lambda part: _pair_spec(sb, part, lambda n: jnp.maximum(n * groups - 1, 0))
    nxt = lambda part: _pair_spec(sb, part, lambda n: jnp.minimum((n + 1) * groups, T // sb - 1))
    full = pl.BlockSpec((3, None, rb, PAIR_W), lambda p, n: (0, p // 2, n, p % 2))
    in_specs = [cur(0), nxt(0), cur(1), prv(1), cur(2), prv(2), cur(0), nxt(0), cur(0), nxt(0), cur(0), nxt(0)]
    args = [proj, proj, proj, proj, proj, proj, dcat, dcat, o, o, lse, lse]
    if has_acc:
        in_specs.append(full)
        args.append(acc)
    if du is not None:
        in_specs.append(cur(0))
        args.append(du)
    out_spec = pl.BlockSpec((n_parts, None, rb, PAIR_W), lambda p, n: (0, p // 2, n, p % 2))
    return pl.pallas_call(
        body, grid=(4, nblk), in_specs=in_specs, out_specs=out_spec,
        out_shape=SDS((n_parts, 2, T, 2 * PAIR_W), F32), compiler_params=_params(2), name=name)(*args)


def _scan_rows(buf, tab_ref, reverse, half):
    n_tiles = (buf.shape[0] - 8) // 8
    per_half = HALF_STATES // SCAN_CW
    row = lax.broadcasted_iota(jnp.int32, (8, SCAN_CW), 0)
    sgn = -1.0 if reverse else 1.0

    for j in range(per_half):
        c0 = half * 2 * HALF_STATES + j * SCAN_CW
        cre = pl.ds(c0, SCAN_CW)
        cim = pl.ds(c0 + HALF_STATES, SCAN_CW)
        steps = []
        for s, k in enumerate((1, 2, 4)):
            ok, shift = (row < 8 - k, 8 - k) if reverse else (row >= k, k)
            steps.append((shift, jnp.where(ok, tab_ref[pl.ds(s, 1), cre], 0.0),
                          jnp.where(ok, sgn * tab_ref[pl.ds(s, 1), cim], 0.0)))
        trow = 16 if reverse else 8
        pr, pi = tab_ref[pl.ds(trow, 8), cre], tab_ref[pl.ds(trow, 8), cim]
        for t in range(n_tiles):
            base = 8 * (n_tiles - 1 - t) if reverse else 8 + 8 * t
            rows = pl.ds(base, 8)
            re, im = buf[rows, cre], buf[rows, cim]
            for shift, ar, ai in steps:
                sre, sim = pltpu.roll(re, shift, 0), pltpu.roll(im, shift, 0)
                re, im = re + ar * sre - ai * sim, im + ar * sim + ai * sre
            crow = pl.ds(base + 8 if reverse else base - 1, 1)
            cr, ci = buf[crow, cre], buf[crow, cim]
            buf[rows, cre] = re + pr * cr - pi * ci
            buf[rows, cim] = im + pr * ci + pi * cr


def ssm_fwd(proj, bh, ch, apow, dskip, name, comm=None):
    _, T, C = proj.shape
    tm = SCAN_TM
    SW = 4 * HALF_STATES

    def body(u_ref, bh_ref, ch_ref, tab_ref, dsk_ref, y_ref, s_ref, buf):
        @pl.when(pl.program_id(0) == 0)
        def _():
            buf[pl.ds(0, 8), :] = jnp.zeros((8, SW), F32)

        for h in range(2):
            buf[pl.ds(8, tm), pl.ds(h * 2 * HALF_STATES, 2 * HALF_STATES)] = _dot(u_ref[h].astype(BF16), bh_ref[h])
        for h in range(2):
            cols = pl.ds(h * 2 * HALF_STATES, 2 * HALF_STATES)
            _scan_rows(buf, tab_ref, False, h)
            sv = buf[pl.ds(8, tm), cols]
            s_ref[:, cols] = sv
            y_ref[h] = _dot(sv.astype(BF16), ch_ref[h]) + dsk_ref[h] * u_ref[h]
        buf[pl.ds(0, 8), :] = buf[pl.ds(tm, 8), :]

    return _call(
        body, grid=(T // tm,),
        in_specs=[pl.BlockSpec((2, tm, C), lambda i: (3, i, 0)),
                  pl.BlockSpec((2, C, 2 * HALF_STATES), lambda i: (0, 0, 0)),
                  pl.BlockSpec((2, 2 * HALF_STATES, C), lambda i: (0, 0, 0)),
                  pl.BlockSpec((24, SW), lambda i: (0, 0)),
                  pl.BlockSpec((2, 1, C), lambda i: (0, 0, 0))],
        out_specs=[pl.BlockSpec((2, tm, C), lambda i: (0, i, 0)), pl.BlockSpec((tm, SW), lambda i: (i, 0))],
        out_shape=[SDS((2, T, C), F32), SDS((T, SW), F32)],
        scratch_shapes=[pltpu.VMEM((tm + 8, SW), F32)],
        args=(proj, bh, ch, apow, dskip), name=name, comm=comm)


def ssm_bwd(dy, proj, st, bh, ch, apow, dskip, name, comm=None):
    _, T, C = proj.shape
    tm = SCAN_TM_BWD
    nt = T // tm
    SW = 4 * HALF_STATES
    HS2 = 2 * HALF_STATES

    def body(dy_ref, u_ref, s_ref, sp_ref, bh_ref, ch_ref, tab_ref, dsk_ref,
             du_ref, da_ref, dbh_ref, dch_ref, dd_ref, lam):
        i = pl.program_id(0)

        @pl.when(i == 0)
        def _():
            lam[pl.ds(tm, 8), :] = jnp.zeros((8, SW), F32)
            da_ref[...] = jnp.zeros_like(da_ref)
            dbh_ref[...] = jnp.zeros_like(dbh_ref)
            dch_ref[...] = jnp.zeros_like(dch_ref)
            dd_ref[...] = jnp.zeros_like(dd_ref)

        for h in range(2):
            lam[pl.ds(0, tm), pl.ds(h * HS2, HS2)] = _dot_nt(dy_ref[h].astype(BF16), ch_ref[h])
        for h in range(2):
            dyv, uv = dy_ref[h], u_ref[h]
            dch_ref[h] += _dot_tn(s_ref[:, pl.ds(h * HS2, HS2)].astype(BF16), dyv.astype(BF16))
            dd_ref[h] += jnp.sum(dyv * uv, axis=0, keepdims=True)
        for h in range(2):
            _scan_rows(lam, tab_ref, True, h)
            lb = lam[pl.ds(0, tm), pl.ds(h * HS2, HS2)].astype(BF16)
            du_ref[h] = _dot_nt(lb, bh_ref[h]) + dsk_ref[h] * dy_ref[h]
            dbh_ref[h] += _dot_tn(u_ref[h].astype(BF16), lb)

        first = i == nt - 1
        per_half = HALF_STATES // SCAN_CW

        def chunk(j, _):
            c0 = pl.multiple_of((j // per_half) * HS2 + (j % per_half) * SCAN_CW, 128)
            cre, cim = pl.ds(c0, SCAN_CW), pl.ds(pl.multiple_of(c0 + HALF_STATES, 128), SCAN_CW)
            row0 = lax.broadcasted_iota(jnp.int32, (8, SCAN_CW), 0) == 0
            acc_r = jnp.zeros((8, SCAN_CW), F32)
            acc_i = jnp.zeros((8, SCAN_CW), F32)
            for t in range(tm // 8):
                rows = pl.ds(8 * t, 8)
                if t == 0:
                    pre = jnp.where(first, 0.0, sp_ref[pl.ds(7, 1), cre])
                    pim = jnp.where(first, 0.0, sp_ref[pl.ds(7, 1), cim])
                else:
                    pre, pim = s_ref[pl.ds(8 * t - 1, 1), cre], s_ref[pl.ds(8 * t - 1, 1), cim]
                spr = jnp.where(row0, pre, pltpu.roll(s_ref[rows, cre], 1, 0))
                spi = jnp.where(row0, pim, pltpu.roll(s_ref[rows, cim], 1, 0))
                lr, li = lam[rows, cre], lam[rows, cim]
                acc_r += lr * spr + li * spi
                acc_i += li * spr - lr * spi
            da_ref[:, cre] += jnp.sum(acc_r, axis=0, keepdims=True)
            da_ref[:, cim] += jnp.sum(acc_i, axis=0, keepdims=True)
            return 0

        lax.fori_loop(0, 2 * per_half, chunk, 0)
        lam[pl.ds(tm, 8), :] = lam[pl.ds(0, 8), :]

    rev = lambda i: nt - 1 - i
    return _call(
        body, grid=(nt,),
        in_specs=[pl.BlockSpec((2, tm, C), lambda i: (0, rev(i), 0)),
                  pl.BlockSpec((2, tm, C), lambda i: (3, rev(i), 0)),
                  pl.BlockSpec((tm, SW), lambda i: (rev(i), 0)),
                  pl.BlockSpec((8, SW), lambda i: (jnp.maximum(rev(i) * (tm // 8) - 1, 0), 0)),
                  _resident((2, C, HS2)), _resident((2, HS2, C)), _resident((24, SW)), _resident((2, 1, C))],
        out_specs=[pl.BlockSpec((2, tm, C), lambda i: (0, rev(i), 0)),
                   _resident((1, SW)), _resident((2, C, HS2)), _resident((2, HS2, C)), _resident((2, 1, C))],
        out_shape=[SDS((2, T, C), F32), SDS((1, SW), F32), SDS((2, C, HS2), F32), SDS((2, HS2, C), F32),
                   SDS((2, 1, C), F32)],
        scratch_shapes=[pltpu.VMEM((tm + 8, SW), F32)],
        args=(dy, proj, st, st, bh, ch, apow, dskip), name=name, comm=comm)


_GELU_C = math.sqrt(2.0 / math.pi)


def _gelu(x):
    t = jnp.tanh(_GELU_C * (x + 0.044715 * x * x * x))
    return 0.5 * x * (1.0 + t), t


def glu_fwd(y, w, b, name):
    _, T, C = y.shape
    tm = 512

    def body(y_ref, w_ref, b_ref, o_ref, lg_ref):
        y0, _ = _gelu(y_ref[0])
        y1, _ = _gelu(y_ref[1])
        lg = _dot(y0.astype(BF16), w_ref[0]) + _dot(y1.astype(BF16), w_ref[1]) + b_ref[...]
        sg = _sigmoid(lg)
        o_ref[0] = y0 * sg[:, :C]
        o_ref[1] = y1 * sg[:, C:]
        lg_ref[0] = lg[:, :C]
        lg_ref[1] = lg[:, C:]

    return pl.pallas_call(
        body, grid=(T // tm,),
        in_specs=[pl.BlockSpec((2, tm, C), lambda i: (0, i, 0)), pl.BlockSpec((2, C, 2 * C), lambda i: (0, 0, 0)),
                  pl.BlockSpec((1, 2 * C), lambda i: (0, 0))],
        out_specs=[pl.BlockSpec((2, tm, C), lambda i: (0, i, 0)), pl.BlockSpec((2, tm, C), lambda i: (0, i, 0))],
        out_shape=[SDS((2, T, C), F32), SDS((2, T, C), F32)], compiler_params=_params(1), name=name)(y, w, b)


def glu_bwd(dcat, y, lg, w, name):
    _, T, C = y.shape
    tm = 512

    def body(d_ref, y_ref, lg_ref, w_ref, dy_ref, dw_ref, db_ref):
        @pl.when(pl.program_id(0) == 0)
        def _():
            dw_ref[...] = jnp.zeros_like(dw_ref)
            db_ref[...] = jnp.zeros_like(db_ref)

        y2, th, sg, dlg = [], [], [], []
        for h in range(2):
            yy, tt = _gelu(y_ref[h])
            ss = _sigmoid(lg_ref[h])
            y2.append(yy)
            th.append(tt)
            sg.append(ss)
            dlg.append(d_ref[h] * yy * ss * (1.0 - ss))
        dl = jnp.concatenate(dlg, axis=1)
        dlb = dl.astype(BF16)
        db_ref[...] += jnp.sum(dl, axis=0, keepdims=True)
        for h in range(2):
            dy2 = d_ref[h] * sg[h] + _dot_nt(dlb, w_ref[h])
            yv = y_ref[h]
            dgelu = 0.5 * (1.0 + th[h]) + 0.5 * yv * (1.0 - th[h] * th[h]) * _GELU_C * (1.0 + 3 * 0.044715 * yv * yv)
            dy_ref[h] = dy2 * dgelu
            dw_ref[h] += _dot_tn(y2[h].astype(BF16), dlb)

    return pl.pallas_call(
        body, grid=(T // tm,),
        in_specs=[pl.BlockSpec((2, tm, C), lambda i: (1, i, 0)), pl.BlockSpec((2, tm, C), lambda i: (0, i, 0)),
                  pl.BlockSpec((2, tm, C), lambda i: (0, i, 0)), pl.BlockSpec((2, C, 2 * C), lambda i: (0, 0, 0))],
        out_specs=[pl.BlockSpec((2, tm, C), lambda i: (0, i, 0)), pl.BlockSpec((2, C, 2 * C), lambda i: (0, 0, 0)),
                   pl.BlockSpec((1, 2 * C), lambda i: (0, 0))],
        out_shape=[SDS((2, T, C), F32), SDS((2, C, 2 * C), F32), SDS((1, 2 * C), F32)],
        compiler_params=_params(1), name=name)(dcat, y, lg, w)


def adamw(w, m, v, slots, name):
    R, C = w.shape
    tr = R
    for cand in (512, 256, 128, 64, 32, 16, 8):
        if R % cand == 0 and cand * C * 4 <= 2 * 1024 * 1024:
            tr = cand
            break
    c1 = 1.0 / (1.0 - ADAM_B1 ** ADAM_STEP)
    c2 = 1.0 / (1.0 - ADAM_B2 ** ADAM_STEP)

    def body(w_ref, m_ref, v_ref, s_ref, g_ref, d_ref, nm_ref, nv_ref):
        g = s_ref[0].astype(F32)
        for j in range(1, N_DEV):
            g = g + s_ref[j].astype(F32)
        nm = ADAM_B1 * m_ref[...] + (1.0 - ADAM_B1) * g
        nv = ADAM_B2 * v_ref[...] + (1.0 - ADAM_B2) * (g * g)
        g_ref[...] = g
        nm_ref[...] = nm
        nv_ref[...] = nv
        d_ref[...] = -ADAM_LR * ((nm * c1) / (jnp.sqrt(nv * c2) + ADAM_EPS) + ADAM_WD * w_ref[...])

    spec = pl.BlockSpec((tr, C), lambda i: (i, 0))
    return pl.pallas_call(
        body, grid=(R // tr,),
        in_specs=[spec, spec, spec, pl.BlockSpec((N_DEV, tr, C), lambda i: (0, i, 0))],
        out_specs=[spec] * 4, out_shape=[SDS((R, C), F32)] * 4, compiler_params=_params(1), name=name)(w, m, v, slots)


def _discretise(a_re, a_im, log_dt, b_re, b_im):
    dt = jnp.exp(log_dt)[:, None]
    e = jnp.exp(dt * a_re)
    ar, ai = e * jnp.cos(dt * a_im), e * jnp.sin(dt * a_im)
    den = a_re * a_re + a_im * a_im
    nr, ni = ar - 1.0, ai
    wr = (nr * a_re + ni * a_im) / den
    wi = (ni * a_re - nr * a_im) / den
    bbr = wr[..., None] * b_re - wi[..., None] * b_im
    bbi = wr[..., None] * b_im + wi[..., None] * b_re
    return ar, ai, bbr, bbi


def _block_diag(t):
    eye = jnp.eye(16, dtype=t.dtype).reshape(1, 16, 1, 16, 1)
    r, c = t.shape[1], t.shape[2]
    return (t.reshape(2, 16, r, 1, c) * eye).reshape(2, 16 * r, 16 * c)


def _diag_blocks(m, r, c):
    eye = jnp.eye(16, dtype=m.dtype).reshape(1, 16, 1, 16, 1)
    return jnp.sum(m.reshape(2, 16, r, 16, c) * eye, axis=3).reshape(32, r, c)


def _state_vec(re, im):
    return jnp.stack([re.reshape(2, HALF_STATES), im.reshape(2, HALF_STATES)], axis=1).reshape(-1)


BIG = ("ffn1_w_in", "ffn1_w_out", "w_mix_in", "w_glu", "w_mix_out", "ffn2_w_in", "ffn2_w_out")
WEIGHTS = ("ffn1_pre_g", "ffn1_w_in", "ffn1_w_out", "ffn1_post_g", "mix_pre_g", "w_mix_in", "a_re", "a_im", "log_dt",
           "b_re", "b_im", "c_re", "c_im", "d_skip", "w_glu", "b_glu", "w_mix_out", "mix_post_g", "ffn2_pre_g",
           "ffn2_w_in", "ffn2_w_out", "ffn2_post_g")
SMALL = tuple(n for n in WEIGHTS if n not in BIG)
TRANSPOSED = ("ffn1_w_in", "ffn2_w_in")
PACK_COLS = 1024


def _pack(parts):
    flat = jnp.concatenate([p.reshape(-1) for p in parts])
    rows = -(-flat.shape[0] // (8 * PACK_COLS)) * 8
    return jnp.pad(flat, (0, rows * PACK_COLS - flat.shape[0])).reshape(rows, PACK_COLS)


def _unpack(packed, shapes):
    flat, out, off = packed.reshape(-1), [], 0
    for s in shapes:
        n = math.prod(s)
        out.append(flat[off:off + n].reshape(s))
        off += n
    return out


def _gather(names, wb):
    return [wb[n] for n in names], [False] * len(names)


def _ffn_bwd(dy, do, saved, x, pre_g, w_in, w_out4, tag, post=None, dwout_comm=None):
    h, z, a = saved
    T = x.shape[0]
    dz = ffn_dact(do, w_out4, z, f"{tag}_dact")
    dz8 = dz.reshape(8, T, dz.shape[-1])
    dw_out, extra = mm_tn(a, do, True, False, 4, f"{tag}_dwout", comm=dwout_comm)
    dw_in, (s_out,) = mm_tn(dz8, h, True, False, 8, f"{tag}_dwin", comm=([dw_out.reshape(8, -1, D_MODEL)], [True]))
    outs, (s_in,) = dh_pre_bwd(dz8, w_in, x, pre_g, dy, f"{tag}_dh", comm=([dw_in], [True]), post=post,
                               w_transposed=True)
    return outs, (s_in, s_out), extra


def local_step(x, tgt, sp, wb):
    T = x.shape[0]
    ar, ai, bbr, bbi = _discretise(sp["a_re"], sp["a_im"], sp["log_dt"], sp["b_re"], sp["b_im"])
    powers = [(ar, ai)]
    for _ in range(7):
        pr, pi = powers[-1]
        powers.append((pr * ar - pi * ai, pr * ai + pi * ar))
    zero = jnp.zeros_like(ar)
    rows = [_state_vec(*powers[k - 1]) for k in (1, 2, 4)] + [_state_vec(zero, zero)] * 5
    rows += [_state_vec(pr, pi) for pr, pi in powers]
    rows += [_state_vec(pr, -pi) for pr, pi in reversed(powers)]
    apow = jnp.stack(rows)
    bh = jnp.concatenate([_block_diag(bbr.transpose(0, 2, 1)), _block_diag(bbi.transpose(0, 2, 1))], axis=2)
    ch = jnp.concatenate([_block_diag(sp["c_re"].transpose(0, 2, 1)), _block_diag(-sp["c_im"].transpose(0, 2, 1))], axis=1)
    bh, ch = bh.astype(BF16), ch.astype(BF16)
    dskip = sp["d_skip"].reshape(2, 1, 256)

    w1_in = gather_two_level(wb["ffn1_w_in"], "gather_w1in")
    (h1, z1, a1), (w1_out, w_mi) = ffn_in(
        x, sp["ffn1_pre_g"], w1_in, "ffn1_in", comm=_gather(["ffn1_w_out", "w_mix_in"], wb))
    w1_out4 = w1_out.reshape(4, -1, D_MODEL)
    (o1, x1), (w_glu, w_mo) = mm_acc_norm(
        a1, w1_out4, x, sp["ffn1_post_g"], 0.5, "ffn1_out", comm=_gather(["w_glu", "w_mix_out"], wb))
    w_glu2, w_mo4 = w_glu.reshape(2, 256, 512), w_mo.reshape(4, 256, D_MODEL)
    h2, proj = norm_proj(x1, sp["mix_pre_g"], w_mi, "mix_proj")
    (y_ssm, states), (w2_in,) = ssm_fwd(proj, bh, ch, apow, dskip, "ssm_fwd", comm=_gather(["ffn2_w_in"], wb))
    os_, ls_ = [], []
    for d in DILATIONS:
        (o_d, l_d), got = attn_fwd(proj, d, f"attn_fwd_d{d}",
                                   comm=_gather(["ffn2_w_out"], wb) if d == DILATIONS[-1] else None)
        os_.append(o_d)
        ls_.append(l_d)
    w2_out4 = got[0].reshape(4, -1, D_MODEL)
    o_ssm, lg = glu_fwd(y_ssm, w_glu2, sp["b_glu"], "glu_fwd")
    cat, lse, mixed, x2 = mix_out(os_, ls_, o_ssm, w_mo4, x1, sp["mix_post_g"], "mix_out")
    (h3, z3, a3), _ = ffn_in(x2, sp["ffn2_pre_g"], w2_in, "ffn2_in")
    (dy3, sq, do3, dg_f2post), _ = mm_acc_norm(a3, w2_out4, x2, sp["ffn2_post_g"], 0.5, "ffn2_out", tgt=tgt)

    (dx2, dg_f2pre, dmixed, dg_mpost), (s_w2in, s_w2out), _ = _ffn_bwd(
        dy3, do3, (h3, z3, a3), x2, sp["ffn2_pre_g"], w2_in, w2_out4, "ffn2", post=(mixed, sp["mix_post_g"], 1.0))
    dcat = mm_nt_b(dmixed, w_mo4, "mix_dcat")
    dw_mo, _ = mm_tn(cat, dmixed, True, False, 4, "mix_dwout")
    dy_ssm, dw_glu, db_glu = glu_bwd(dcat, y_ssm, lg, w_glu2, "glu_bwd")
    (du, da, dbh, dch, dd), (s_wmo, s_wglu) = ssm_bwd(
        dy_ssm, proj, states, bh, ch, apow, dskip, "ssm_bwd",
        comm=([dw_mo.reshape(8, 128, D_MODEL), dw_glu.astype(BF16).reshape(8, 64, 512)], [True, True]))
    dqkv = None
    for d in DILATIONS:
        dqkv = attn_bwd(proj, dcat, cat, lse, dqkv, d, f"attn_bwd_d{d}", du=du if d == DILATIONS[-1] else None)
    dproj = dqkv.reshape(8, T, 256)
    dw_mi = mm_tn_shared_a(h2, dproj, "mix_dwin")
    (dx1, dg_mpre, do1, dg_f1post), (s_wmi,) = dh_pre_bwd(
        dproj, w_mi, x1, sp["mix_pre_g"], dx2, "mix_dh", comm=([dw_mi], [True]), post=(o1, sp["ffn1_post_g"], 0.5))

    da4 = da.reshape(2, 2, HALF_STATES)
    d_ar, d_ai = da4[:, 0].reshape(32, N_STATE), da4[:, 1].reshape(32, N_STATE)
    d_bbr = _diag_blocks(dbh[:, :, :HALF_STATES], 16, N_STATE).transpose(0, 2, 1)
    d_bbi = _diag_blocks(dbh[:, :, HALF_STATES:], 16, N_STATE).transpose(0, 2, 1)
    _, disc_vjp = jax.vjp(_discretise, sp["a_re"], sp["a_im"], sp["log_dt"], sp["b_re"], sp["b_im"])
    g_are, g_aim, g_ldt, g_bre, g_bim = disc_vjp((d_ar, d_ai, d_bbr, d_bbi))
    g_cre = _diag_blocks(dch[:, :HALF_STATES], N_STATE, 16).transpose(0, 2, 1)
    g_cim = -_diag_blocks(dch[:, HALF_STATES:], N_STATE, 16).transpose(0, 2, 1)
    small = {
        "ffn1_pre_g": jnp.zeros((1, D_MODEL), F32), "ffn1_post_g": dg_f1post, "mix_pre_g": dg_mpre, "a_re": g_are,
        "a_im": g_aim, "log_dt": g_ldt, "b_re": g_bre, "b_im": g_bim, "c_re": g_cre, "c_im": g_cim,
        "d_skip": dd.reshape(1, 512), "b_glu": db_glu, "mix_post_g": dg_mpost, "ffn2_pre_g": dg_f2pre,
        "ffn2_post_g": dg_f2post,
    }
    (dx0, dg_f1pre), (s_w1in, s_w1out), (early,) = _ffn_bwd(
        dx1, do1, (h1, z1, a1), x, sp["ffn1_pre_g"], w1_in, w1_out4, "ffn1",
        dwout_comm=([_pack([small[n] for n in SMALL])], [False]))
    late = gather_two_level(dg_f1pre, "exchange_small")
    small_slots = lax.dynamic_update_slice(early, late, (0, 0, 0))
    big_slots = {"ffn1_w_in": s_w1in, "ffn1_w_out": s_w1out, "w_mix_in": s_wmi, "w_glu": s_wglu, "w_mix_out": s_wmo,
                 "ffn2_w_in": s_w2in, "ffn2_w_out": s_w2out}
    return sq, dx0, big_slots, small_slots


def kernel(x, ffn1_pre_g, ffn1_w_in, ffn1_w_out, ffn1_post_g, mix_pre_g, w_mix_in, a_re, a_im, log_dt, b_re, b_im, c_re, c_im, d_skip, w_glu, b_glu, w_mix_out, mix_post_g, ffn2_pre_g, ffn2_w_in, ffn2_w_out, ffn2_post_g, loss_target, m_ffn1_pre_g, m_ffn1_w_in, m_ffn1_w_out, m_ffn1_post_g, m_mix_pre_g, m_w_mix_in, m_a_re, m_a_im, m_log_dt, m_b_re, m_b_im, m_c_re, m_c_im, m_d_skip, m_w_glu, m_b_glu, m_w_mix_out, m_mix_post_g, m_ffn2_pre_g, m_ffn2_w_in, m_ffn2_w_out, m_ffn2_post_g, v_ffn1_pre_g, v_ffn1_w_in, v_ffn1_w_out, v_ffn1_post_g, v_mix_pre_g, v_w_mix_in, v_a_re, v_a_im, v_log_dt, v_b_re, v_b_im, v_c_re, v_c_im, v_d_skip, v_w_glu, v_b_glu, v_w_mix_out, v_mix_post_g, v_ffn2_pre_g, v_ffn2_w_in, v_ffn2_w_out, v_ffn2_post_g):
    args = dict(locals())
    w = {n: args[n][0] for n in WEIGHTS}
    m = {n: args["m_" + n][0] for n in WEIGHTS}
    v = {n: args["v_" + n][0] for n in WEIGHTS}

    for d in (w, m, v):
        for n in TRANSPOSED:
            d[n] = jnp.swapaxes(d[n], 0, 1)
    wb = {n: w[n].astype(BF16) for n in BIG}
    sp = {n: w[n] for n in SMALL}
    for n in ("ffn1_pre_g", "ffn1_post_g", "mix_pre_g", "mix_post_g", "ffn2_pre_g", "ffn2_post_g", "b_glu", "d_skip"):
        sp[n] = w[n].reshape(1, -1)

    sq, grad_x, big_slots, small_slots = local_step(x[0], loss_target[0], sp, wb)
    loss = lax.psum(0.5 / D_MODEL * jnp.sum(sq), ("x", "y", "c"))

    outs = {}
    for n in BIG:
        shp = w[n].shape
        r2 = lambda t: t.reshape(-1, shp[-1])
        res = adamw(r2(w[n]), r2(m[n]), r2(v[n]), big_slots[n].reshape(N_DEV, -1, shp[-1]), f"adamw_{n}")
        outs[n] = [(jnp.swapaxes(t, 0, 1) if n in TRANSPOSED else t.reshape(shp))[None] for t in res]
    res = adamw(_pack([w[n] for n in SMALL]), _pack([m[n] for n in SMALL]), _pack([v[n] for n in SMALL]),
                small_slots, "adamw_small")
    shapes = [(1,) + w[n].shape for n in SMALL]
    unpacked = [_unpack(t, shapes) for t in res]
    for j, n in enumerate(SMALL):
        outs[n] = [unpacked[k][j] for k in range(4)]

    result = [loss, grad_x[None]]
    for k in range(4):
        result += [outs[n][k] for n in WEIGHTS]
    return tuple(result)
```

```python
import functools
import math

import jax
import jax.numpy as jnp
from jax import lax
from jax.experimental import pallas as pl
from jax.experimental.pallas import tpu as pltpu

F32, BF16 = jnp.float32, jnp.bfloat16
SDS = jax.ShapeDtypeStruct

D_MODEL = 1024
N_DEV = 8
HEAD_DIM = 64
PAIR_W = 128
QBLK = 128
DILATIONS = (1, 4, 16)
N_STATE = 64
HALF_STATES = 1024
NORM_EPS = 1e-6
NEG = -1e30
VMEM_LIMIT = 56 * 1024 * 1024
ADAM_LR, ADAM_B1, ADAM_B2, ADAM_EPS, ADAM_WD, ADAM_STEP = 1e-3, 0.9, 0.999, 1e-8, 0.01, 10
SCAN_TM = 256
SCAN_TM_BWD = 512
SCAN_CW = 512


def _params(n_grid):
    return pltpu.CompilerParams(dimension_semantics=("arbitrary",) * n_grid, vmem_limit_bytes=VMEM_LIMIT)


def _dot(a, b):
    return jnp.dot(a, b, preferred_element_type=F32)


def _dot_nt(a, b):
    return lax.dot_general(a, b, (((1,), (1,)), ((), ())), preferred_element_type=F32)


def _dot_tn(a, b):
    return lax.dot_general(a, b, (((0,), (0,)), ((), ())), preferred_element_type=F32)


def _sigmoid(v):
    return 0.5 * jnp.tanh(0.5 * v) + 0.5


def _resident(shape):
    return pl.BlockSpec(shape, lambda i: (0,) * len(shape), pipeline_mode=pl.Buffered(1))


ROW_SPLIT = 2


def _exchange_phase(ins, outs, scatter, sems, start):
    send_sems, recv_sems, loc_sems = sems
    x, y, c = lax.axis_index("x"), lax.axis_index("y"), lax.axis_index("c")
    me = 4 * x + 2 * y + c
    own_copies, sends, arrivals = [], [], []
    for i in range(len(ins)):
        own = ins[i].at[me] if scatter[i] else ins[i]
        own_copies.append(pltpu.make_async_copy(own, outs[i].at[me], loc_sems.at[i]))
        for k in range(1, N_DEV):
            px = 1 - x if k & 4 else x
            py = 1 - y if k & 2 else y
            pc = 1 - c if k & 1 else c
            peer = 4 * px + 2 * py + pc
            src = ins[i].at[peer] if scatter[i] else ins[i]
            common = dict(src_ref=src, send_sem=send_sems.at[i, k - 1], recv_sem=recv_sems.at[i, k - 1],
                          device_id=(px, py, pc), device_id_type=pl.DeviceIdType.MESH)
            sends.append(pltpu.make_async_remote_copy(dst_ref=outs[i].at[me], **common))
            if not start:
                arrivals.append(pltpu.make_async_remote_copy(dst_ref=outs[i].at[peer], **common))
    if start:
        for cp in own_copies + sends:
            cp.start()
    else:
        for cp in arrivals:
            cp.wait_recv()
        for cp in sends:
            cp.wait_send()
        for cp in own_copies:
            cp.wait()


def _comm_shapes(arrs, scatter):
    n = len(arrs)
    out_shapes = [SDS(a.shape if scatter[i] else (N_DEV,) + a.shape, a.dtype) for i, a in enumerate(arrs)]
    sems = [pltpu.SemaphoreType.DMA((n, N_DEV - 1)), pltpu.SemaphoreType.DMA((n, N_DEV - 1)),
            pltpu.SemaphoreType.DMA((n,))]
    return out_shapes, sems


def gather_two_level(arr, name):
    def body(x_ref, out_ref, send_sems, recv_sems, local_sem):
        x, y, c = lax.axis_index("x"), lax.axis_index("y"), lax.axis_index("c")
        sibling = (x, y, 1 - c)
        chips = [(1 - x, y), (x, 1 - y), (1 - x, 1 - y)]

        def slot(px, py, pc):
            return out_ref.at[4 * px + 2 * py + pc]

        def copy(k, block, to, src=None):
            return pltpu.make_async_remote_copy(
                src_ref=slot(*block) if src is None else src, dst_ref=slot(*block),
                send_sem=send_sems.at[k], recv_sem=recv_sems.at[k], device_id=to, device_id_type=pl.DeviceIdType.MESH)

        mine = pltpu.make_async_copy(x_ref, slot(x, y, c), local_sem)
        mine.start()
        first = [copy(0, (x, y, c), sibling, src=x_ref)]
        first += [copy(1 + j, (x, y, c), (*chip, c), src=x_ref) for j, chip in enumerate(chips)]
        for cp in first:
            cp.start()
        passed = [copy(4 + j, (*chip, c), sibling) for j, chip in enumerate(chips)]
        for j, chip in enumerate(chips):
            copy(1 + j, (*chip, c), (x, y, c)).wait_recv()
            passed[j].start()
        copy(0, sibling, (x, y, c)).wait_recv()
        for j, chip in enumerate(chips):
            copy(4 + j, (*chip, 1 - c), (x, y, c)).wait_recv()
        for cp in first + passed:
            cp.wait_send()
        mine.wait()

    anyspec = pl.BlockSpec(memory_space=pl.ANY)
    return pl.pallas_call(
        body, in_specs=[anyspec], out_specs=anyspec, out_shape=SDS((N_DEV,) + arr.shape, arr.dtype),
        scratch_shapes=[pltpu.SemaphoreType.DMA((N_DEV - 1,)), pltpu.SemaphoreType.DMA((N_DEV - 1,)),
                        pltpu.SemaphoreType.DMA],
        compiler_params=pltpu.CompilerParams(has_side_effects=True), name=name)(arr)


def _call(body, *, grid, in_specs, out_specs, out_shape, args, name, scratch_shapes=(), comm=None):
    n_grid, scratch_shapes = len(grid), list(scratch_shapes)
    if comm is None:
        outs = pl.pallas_call(body, grid=grid, in_specs=in_specs, out_specs=out_specs, out_shape=out_shape,
                              scratch_shapes=scratch_shapes, compiler_params=_params(n_grid), name=name)(*args)
        return outs, []
    arrs, scatter = comm
    nc, n_in, n_out, n_sc = len(arrs), len(in_specs), len(out_specs), len(scratch_shapes)
    comm_shapes, sems = _comm_shapes(arrs, scatter)

    def wrapped(*refs):
        ins, cins = refs[:n_in], refs[n_in:n_in + nc]
        o0 = n_in + nc
        outs, couts = refs[o0:o0 + n_out], refs[o0 + n_out:o0 + n_out + nc]
        s0 = o0 + n_out + nc
        scratch, sem_refs = refs[s0:s0 + n_sc], refs[s0 + n_sc:]
        first = functools.reduce(jnp.logical_and, [pl.program_id(k) == 0 for k in range(n_grid)])
        last = functools.reduce(jnp.logical_and, [pl.program_id(k) == grid[k] - 1 for k in range(n_grid)])

        @pl.when(first)
        def _():
            _exchange_phase(cins, couts, scatter, sem_refs, True)

        body(*ins, *outs, *scratch)

        @pl.when(last)
        def _():
            _exchange_phase(cins, couts, scatter, sem_refs, False)

    anyspec = pl.BlockSpec(memory_space=pl.ANY)
    res = pl.pallas_call(
        wrapped, grid=grid, in_specs=list(in_specs) + [anyspec] * nc, out_specs=list(out_specs) + [anyspec] * nc,
        out_shape=list(out_shape) + comm_shapes, scratch_shapes=scratch_shapes + sems,
        compiler_params=pltpu.CompilerParams(dimension_semantics=("arbitrary",) * n_grid,
                                             vmem_limit_bytes=VMEM_LIMIT, has_side_effects=True),
        name=name)(*args, *arrs)
    return res[:n_out], res[n_out:]


def _rms(xv, g):
    r = lax.rsqrt(jnp.mean(xv * xv, axis=-1, keepdims=True) + NORM_EPS)
    return (xv * r * g).astype(BF16)


def ffn_in(x, g, w, name, comm=None):
    T, D = x.shape
    F = w.shape[1]
    tm = 512

    def body(x_ref, g_ref, w_ref, h_ref, z_ref, a_ref):
        hv = _rms(x_ref[...], g_ref[...])
        h_ref[...] = hv
        pending = None
        for j in range(5):
            if j < 4:
                zs = (_dot_nt(hv, w_ref[j]), _dot_nt(hv, w_ref[j + 4]))
            if pending is not None:
                zg, zu = pending
                sg = _sigmoid(zg)
                silu = zg * sg
                z_ref[0, j - 1] = (zu * (sg + silu - silu * sg)).astype(BF16)
                z_ref[1, j - 1] = silu.astype(BF16)
                a_ref[j - 1] = (silu * zu).astype(BF16)
            pending = zs

    return _call(
        body, grid=(T // tm,),
        in_specs=[pl.BlockSpec((tm, D), lambda i: (i, 0)), pl.BlockSpec((1, D), lambda i: (0, 0)),
                  _resident((8, F, D))],
        out_specs=[pl.BlockSpec((tm, D), lambda i: (i, 0)), pl.BlockSpec((2, 4, tm, F), lambda i: (0, 0, i, 0)),
                   pl.BlockSpec((4, tm, F), lambda i: (0, i, 0))],
        out_shape=[SDS((T, D), BF16), SDS((2, 4, T, F), BF16), SDS((4, T, F), BF16)],
        args=(x, g, w), name=name, comm=comm)


def norm_proj(x, g, w, name):
    T, K = x.shape
    nb, _, N = w.shape
    tm = 512

    def body(x_ref, g_ref, w_ref, h_ref, o_ref):
        hv = _rms(x_ref[...], g_ref[...])
        h_ref[...] = hv
        for b in range(nb):
            o_ref[b] = _dot(hv, w_ref[b])

    return pl.pallas_call(
        body, grid=(T // tm,),
        in_specs=[pl.BlockSpec((tm, K), lambda i: (i, 0)), pl.BlockSpec((1, K), lambda i: (0, 0)),
                  _resident((nb, K, N))],
        out_specs=[pl.BlockSpec((tm, K), lambda i: (i, 0)), pl.BlockSpec((nb, tm, N), lambda i: (0, i, 0))],
        out_shape=[SDS((T, K), BF16), SDS((nb, T, N), F32)], compiler_params=_params(1), name=name)(x, g, w)


def mm_acc_norm(a, w, xres, g, scale, name, comm=None, tgt=None):
    nb, T, K = a.shape
    D = w.shape[2]
    tm = 512
    rc = tm // ROW_SPLIT
    with_loss = tgt is not None

    def body(a_ref, w_ref, x_ref, g_ref, *rest):
        if with_loss:
            t_ref, dy_ref, sq_ref, do_ref, dg_ref = rest

            @pl.when(pl.program_id(0) == 0)
            def _():
                sq_ref[...] = jnp.zeros_like(sq_ref)
                dg_ref[...] = jnp.zeros_like(dg_ref)
        else:
            o_ref, y_ref = rest
        accs = []
        for c in range(ROW_SPLIT):
            rows = pl.ds(c * rc, rc)
            o = _dot(a_ref[0, rows, :].astype(BF16), w_ref[0])
            for b in range(1, nb):
                o += _dot(a_ref[b, rows, :].astype(BF16), w_ref[b])
            accs.append(o)
        for c, o in enumerate(accs):
            rows = pl.ds(c * rc, rc)
            r = lax.rsqrt(jnp.mean(o * o, axis=-1, keepdims=True) + NORM_EPS)
            y = x_ref[rows, :] + scale * (o * r * g_ref[...])
            if with_loss:
                e = y - t_ref[rows, :]
                dy = e * (1.0 / D)
                dy_ref[rows, :] = dy
                sq_ref[...] += jnp.sum(e * e, axis=0, keepdims=True)
                do, dg = _post_bwd(dy, o, g_ref[...], scale)
                do_ref[rows, :] = do
                dg_ref[...] += dg
            else:
                o_ref[rows, :] = o
                y_ref[rows, :] = y

    tile = pl.BlockSpec((tm, D), lambda i: (i, 0))
    row = pl.BlockSpec((1, D), lambda i: (0, 0))
    in_specs = [pl.BlockSpec((nb, tm, K), lambda i: (0, i, 0)), _resident((nb, K, D)), tile, row]
    args = (a, w, xres, g)
    if with_loss:
        return _call(body, grid=(T // tm,), in_specs=in_specs + [tile], out_specs=[tile, row, tile, row],
                     out_shape=[SDS((T, D), F32), SDS((1, D), F32), SDS((T, D), BF16), SDS((1, D), F32)],
                     args=args + (tgt,), name=name, comm=comm)
    return _call(body, grid=(T // tm,), in_specs=in_specs, out_specs=[tile, tile],
                 out_shape=[SDS((T, D), F32), SDS((T, D), F32)], args=args, name=name, comm=comm)


def _post_bwd(dy, ov, g, scale):
    r = scale * dy
    rstd = lax.rsqrt(jnp.mean(ov * ov, axis=-1, keepdims=True) + NORM_EPS)
    oh = ov * rstd
    rg = r * g
    do = rstd * (rg - oh * jnp.mean(rg * oh, axis=-1, keepdims=True))
    return do.astype(BF16), jnp.sum(r * oh, axis=0, keepdims=True)


def mm_nt_b(gr, w, name):
    T, N = gr.shape
    nb, K, _ = w.shape
    tm = 512

    def body(g_ref, w_ref, o_ref):
        gv = g_ref[...]
        for b in range(nb):
            o_ref[b] = _dot_nt(gv, w_ref[b])

    return pl.pallas_call(
        body, grid=(T // tm,),
        in_specs=[pl.BlockSpec((tm, N), lambda i: (i, 0)), _resident((nb, K, N))],
        out_specs=pl.BlockSpec((nb, tm, K), lambda i: (0, i, 0)),
        out_shape=SDS((nb, T, K), F32), compiler_params=_params(1), name=name)(gr, w)


def ffn_dact(do, w_out, z, name):
    T, D = do.shape
    nb, F, _ = w_out.shape
    tm = 512

    def body(g_ref, w_ref, z_ref, dz_ref):
        gv = g_ref[...]
        pending = None
        for b in range(nb + 1):
            da = _dot_nt(gv, w_ref[b]) if b < nb else None
            if pending is not None:
                dz_ref[0, b - 1] = (pending * z_ref[0, b - 1].astype(F32)).astype(BF16)
                dz_ref[1, b - 1] = (pending * z_ref[1, b - 1].astype(F32)).astype(BF16)
            pending = da

    blk = pl.BlockSpec((2, nb, tm, F), lambda i: (0, 0, i, 0))
    return pl.pallas_call(
        body, grid=(T // tm,),
        in_specs=[pl.BlockSpec((tm, D), lambda i: (i, 0)), _resident((nb, F, D)), blk],
        out_specs=blk, out_shape=SDS((2, nb, T, F), BF16), compiler_params=_params(1), name=name)(do, w_out, z)


def mm_tn(a, g, a_batched, g_batched, nb, name, comm=None):
    T = a.shape[-2]
    K, N = a.shape[-1], g.shape[-1]
    tk = 2048
    nk = T // tk

    def body(a_ref, g_ref, o_ref, acc):
        k = pl.program_id(1)

        @pl.when(k == 0)
        def _():
            acc[...] = jnp.zeros_like(acc)

        acc[...] += _dot_tn(a_ref[...].astype(BF16), g_ref[...].astype(BF16))

        @pl.when(k == nk - 1)
        def _():
            o_ref[...] = acc[...].astype(BF16)

    a_spec = (pl.BlockSpec((None, tk, K), lambda b, k: (b, k, 0)) if a_batched
              else pl.BlockSpec((tk, K), lambda b, k: (k, 0)))
    g_spec = (pl.BlockSpec((None, tk, N), lambda b, k: (b, k, 0)) if g_batched
              else pl.BlockSpec((tk, N), lambda b, k: (k, 0)))
    (out,), slots = _call(
        body, grid=(nb, nk), in_specs=[a_spec, g_spec],
        out_specs=[pl.BlockSpec((None, K, N), lambda b, k: (b, 0, 0))],
        out_shape=[SDS((nb, K, N), BF16)], scratch_shapes=[pltpu.VMEM((K, N), F32)],
        args=(a, g), name=name, comm=comm)
    return out, slots


def mm_tn_shared_a(a, g, name):
    T, K = a.shape
    nb, _, N = g.shape
    tk = 1024
    nk = T // tk

    def body(a_ref, g_ref, o_ref, acc):
        k = pl.program_id(0)

        @pl.when(k == 0)
        def _():
            acc[...] = jnp.zeros_like(acc)

        av = a_ref[...]
        for b in range(nb):
            acc[b] += _dot_tn(av, g_ref[b].astype(BF16))

        @pl.when(k == nk - 1)
        def _():
            o_ref[...] = acc[...].astype(BF16)

    return pl.pallas_call(
        body, grid=(nk,),
        in_specs=[pl.BlockSpec((tk, K), lambda k: (k, 0)), pl.BlockSpec((nb, tk, N), lambda k: (0, k, 0))],
        out_specs=_resident((nb, K, N)), out_shape=SDS((nb, K, N), BF16),
        scratch_shapes=[pltpu.VMEM((nb, K, N), F32)], compiler_params=_params(1), name=name)(a, g)


def dh_pre_bwd(dz, w, x, g, dyres, name, comm=None, post=None, w_transposed=False):
    nb, T, F = dz.shape
    D = x.shape[1]
    tm = 512
    rc = tm // ROW_SPLIT
    mm = _dot if w_transposed else _dot_nt

    def body(dz_ref, w_ref, x_ref, g_ref, dy_ref, *rest):
        if post is None:
            dx_ref, dg_ref = rest
        else:
            o_ref, gp_ref, dx_ref, dg_ref, do_ref, dgp_ref = rest

        @pl.when(pl.program_id(0) == 0)
        def _():
            dg_ref[...] = jnp.zeros_like(dg_ref)
            if post is not None:
                dgp_ref[...] = jnp.zeros_like(dgp_ref)

        accs = []
        for c in range(ROW_SPLIT):
            rows = pl.ds(c * rc, rc)
            dh = mm(dz_ref[0, rows, :].astype(BF16), w_ref[0])
            for b in range(1, nb):
                dh += mm(dz_ref[b, rows, :].astype(BF16), w_ref[b])
            accs.append(dh)
        for c, dh in enumerate(accs):
            rows = pl.ds(c * rc, rc)
            xv = x_ref[rows, :]
            rstd = lax.rsqrt(jnp.mean(xv * xv, axis=-1, keepdims=True) + NORM_EPS)
            xh = xv * rstd
            dg_ref[...] += jnp.sum(dh * xh, axis=0, keepdims=True)
            dhg = dh * g_ref[...]
            dx = dy_ref[rows, :] + rstd * (dhg - xh * jnp.mean(dhg * xh, axis=-1, keepdims=True))
            dx_ref[rows, :] = dx
            if post is not None:
                do, dgp = _post_bwd(dx, o_ref[rows, :], gp_ref[...], post[2])
                do_ref[rows, :] = do
                dgp_ref[...] += dgp

    tile = pl.BlockSpec((tm, D), lambda i: (i, 0))
    row = pl.BlockSpec((1, D), lambda i: (0, 0))
    in_specs = [pl.BlockSpec((nb, tm, F), lambda i: (0, i, 0)), _resident(w.shape), tile, row, tile]
    out_specs, out_shape, args = [tile, row], [SDS((T, D), F32), SDS((1, D), F32)], (dz, w, x, g, dyres)
    if post is not None:
        in_specs += [tile, row]
        out_specs += [tile, row]
        out_shape += [SDS((T, D), BF16), SDS((1, D), F32)]
        args += (post[0], post[1])
    return _call(body, grid=(T // tm,), in_specs=in_specs, out_specs=out_specs, out_shape=out_shape,
                 args=args, name=name, comm=comm)


ATTN_GROUP = {1: 4, 4: 2, 16: 1}
ATTN_UNROLL = 4


def _attn_masks():
    qi = lax.broadcasted_iota(jnp.int32, (QBLK, QBLK), 0)
    kj = lax.broadcasted_iota(jnp.int32, (QBLK, QBLK), 1)
    cur_ok = kj <= qi
    prev_ok = kj >= qi
    dcur = (qi - kj).astype(F32)
    return cur_ok, prev_ok, dcur, dcur + float(QBLK)


def _head_slopes(p, d):
    out = []
    for hq in range(2):
        v = [float(d) * 2.0 ** -(2 * q + hq + 1) for q in range(4)]
        out.append(jnp.where(p == 0, v[0], jnp.where(p == 1, v[1], jnp.where(p == 2, v[2], v[3]))))
    return out


def _rows(start, d):
    return pl.ds(start, QBLK, stride=d) if d > 1 else pl.ds(start, QBLK)


def _pair_spec(rows, part, blk):
    return pl.BlockSpec((None, rows, PAIR_W), lambda p, n: (2 * part + p // 2, blk(n), p % 2))


def _for_query_blocks(d, groups, several):
    blocks = [(g, r) for r in range(d) for g in range(groups)]
    for s in range(0, len(blocks), ATTN_UNROLL):
        several(blocks[s:s + ATTN_UNROLL])


def attn_fwd(proj, d, name, comm=None):
    T = proj.shape[1]
    sb, groups = QBLK * d, ATTN_GROUP[d]
    rb = sb * groups
    nblk = T // rb

    def body(q_ref, kc_ref, kp_ref, vc_ref, vp_ref, o_ref, l_ref):
        p, n = pl.program_id(0), pl.program_id(1)
        cur_ok, prev_ok, dcur, dprev = _attn_masks()
        first_ok = jnp.logical_and(prev_ok, n > 0)
        lane_head = lax.broadcasted_iota(jnp.int32, (QBLK, PAIR_W), 1) // HEAD_DIM
        slopes = _head_slopes(p, d)

        def several(blocks):
            work = []
            for g, r in blocks:
                rows = _rows(g * sb + r, d)
                q = q_ref[rows, :]
                kc, vc = kc_ref[rows, :].astype(BF16), vc_ref[rows, :].astype(BF16)
                if g == 0:
                    prow, pok = _rows(r, d), first_ok
                    kp, vp = kp_ref[prow, :].astype(BF16), vp_ref[prow, :].astype(BF16)
                else:
                    prow, pok = _rows((g - 1) * sb + r, d), prev_ok
                    kp, vp = kc_ref[prow, :].astype(BF16), vc_ref[prow, :].astype(BF16)
                for hq in range(2):
                    qm = jnp.where(lane_head == hq, q, 0.0).astype(BF16)
                    work.append([rows, hq, pok, vc, vp, _dot_nt(qm, kc), _dot_nt(qm, kp)])
            for w in work:
                _, hq, pok, _, _, sc, sp = w
                sc = jnp.where(cur_ok, sc * 0.125 - slopes[hq] * dcur, NEG)
                sp = jnp.where(pok, sp * 0.125 - slopes[hq] * dprev, NEG)
                m = jnp.maximum(jnp.max(sc, axis=1, keepdims=True), jnp.max(sp, axis=1, keepdims=True))
                pc = jnp.exp(sc - m)
                pp = jnp.exp(sp - m)
                den = jnp.sum(pc, axis=1, keepdims=True) + jnp.sum(pp, axis=1, keepdims=True)
                w[5:] = [pc.astype(BF16), pp.astype(BF16), 1.0 / den, m + jnp.log(den)]
            for i in range(0, len(work), 2):
                o_acc = jnp.zeros((QBLK, PAIR_W), F32)
                l_acc = jnp.zeros((QBLK, PAIR_W), F32)
                for rows, hq, _, vc, vp, pc, pp, inv, lse in work[i:i + 2]:
                    hm = lane_head == hq
                    o_acc = jnp.where(hm, (_dot(pc, vc) + _dot(pp, vp)) * inv, o_acc)
                    l_acc = jnp.where(hm, lse, l_acc)
                o_ref[rows, :] = o_acc
                l_ref[rows, :] = l_acc

        _for_query_blocks(d, groups, several)

    cur = lambda part: _pair_spec(rb, part, lambda n: n)
    prv = lambda part: _pair_spec(sb, part, lambda n: jnp.maximum(n * groups - 1, 0))
    return _call(
        body, grid=(4, nblk), in_specs=[cur(0), cur(1), prv(1), cur(2), prv(2)], out_specs=[cur(0), cur(0)],
        out_shape=[SDS((2, T, 2 * PAIR_W), F32), SDS((2, T, 2 * PAIR_W), F32)],
        args=(proj, proj, proj, proj, proj), name=name, comm=comm)


def mix_out(os_, ls_, o_ssm, w, xres, g, name):
    _, T, HW = o_ssm.shape
    D = w.shape[2]
    tm = 512

    def body(o1, o2, o3, l1, l2, l3, s_ref, w_ref, x_ref, g_ref, cat_ref, l_ref, m_ref, y_ref):
        a, b, c = l1[...], l2[...], l3[...]
        m = jnp.maximum(jnp.maximum(a, b), c)
        ea, eb, ec = jnp.exp(a - m), jnp.exp(b - m), jnp.exp(c - m)
        s = ea + eb + ec
        att = (ea * o1[...] + eb * o2[...] + ec * o3[...]) * (1.0 / s)
        ssm = s_ref[...]
        cat_ref[pl.ds(0, 2)] = att
        cat_ref[pl.ds(2, 2)] = ssm
        l_ref[...] = m + jnp.log(s)
        o = _dot(att[0].astype(BF16), w_ref[0]) + _dot(att[1].astype(BF16), w_ref[1])
        o += _dot(ssm[0].astype(BF16), w_ref[2]) + _dot(ssm[1].astype(BF16), w_ref[3])
        r = lax.rsqrt(jnp.mean(o * o, axis=-1, keepdims=True) + NORM_EPS)
        m_ref[...] = o
        y_ref[...] = x_ref[...] + o * r * g_ref[...]

    spec = pl.BlockSpec((2, tm, HW), lambda i: (0, i, 0))
    tile = pl.BlockSpec((tm, D), lambda i: (i, 0))
    return pl.pallas_call(
        body, grid=(T // tm,),
        in_specs=[spec] * 7 + [_resident(w.shape), tile, pl.BlockSpec((1, D), lambda i: (0, 0))],
        out_specs=[pl.BlockSpec((4, tm, HW), lambda i: (0, i, 0)), spec, tile, tile],
        out_shape=[SDS((4, T, HW), F32), SDS((2, T, HW), F32), SDS((T, D), F32), SDS((T, D), F32)],
        compiler_params=_params(1), name=name)(*os_, *ls_, o_ssm, w, xres, g)


def attn_bwd(proj, dcat, o, lse, acc, d, name, du=None, acc_b=None):
    T = proj.shape[1]
    sb, groups = QBLK * d, ATTN_GROUP[d]
    rb = sb * groups
    nblk = T // rb
    has_acc = acc is not None
    n_parts = 3 if du is None else 4

    def body(*refs):
        (qc_ref, qn_ref, kc_ref, kp_ref, vc_ref, vp_ref, dc_ref, dn_ref, oc_ref, on_ref, lc_ref, ln_ref) = refs[:12]
        acc_ref = refs[12] if has_acc else None
        accb_ref = refs[13] if acc_b is not None else None
        out_ref = refs[-1]
        if du is not None:
            out_ref[3] = refs[-2][...]
        p, n = pl.program_id(0), pl.program_id(1)
        cur_ok, prev_ok, dcur, dprev = _attn_masks()
        first_ok = jnp.logical_and(prev_ok, n > 0)
        last_ok = jnp.logical_and(prev_ok, n < nblk - 1)
        lane_head = lax.broadcasted_iota(jnp.int32, (QBLK, PAIR_W), 1) // HEAD_DIM
        slopes = _head_slopes(p, d)

        def one(g, r, shared):
            rows = _rows(g * sb + r, d)
            q_c, do_c, o_c, l_c = qc_ref[rows, :], dc_ref[rows, :], oc_ref[rows, :], lc_ref[rows, :]
            k_c, v_c = kc_ref[rows, :].astype(BF16), vc_ref[rows, :].astype(BF16)
            if shared:
                pok_c, k_p, v_p = prev_ok, None, None
            elif g == 0:
                prow, pok_c = _rows(r, d), first_ok
                k_p, v_p = kp_ref[prow, :].astype(BF16), vp_ref[prow, :].astype(BF16)
            else:
                prow, pok_c = _rows((g - 1) * sb + r, d), prev_ok
                k_p, v_p = kc_ref[prow, :].astype(BF16), vc_ref[prow, :].astype(BF16)
            if g == groups - 1:
                nrow, pok_n = _rows(r, d), last_ok
                q_n, do_n, o_n, l_n = qn_ref[nrow, :], dn_ref[nrow, :], on_ref[nrow, :], ln_ref[nrow, :]
            else:
                nrow, pok_n = _rows((g + 1) * sb + r, d), prev_ok
                q_n, do_n, o_n, l_n = qc_ref[nrow, :], dc_ref[nrow, :], oc_ref[nrow, :], lc_ref[nrow, :]
            heads = []
            for hq in range(2):
                hm = lane_head == hq
                qm_c = jnp.where(hm, q_c, 0.0).astype(BF16)
                qm_n = jnp.where(hm, q_n, 0.0).astype(BF16)
                dom_c = jnp.where(hm, do_c, 0.0)
                dom_n = jnp.where(hm, do_n, 0.0)
                dd_c = jnp.sum(dom_c * o_c, axis=1, keepdims=True)
                dd_n = jnp.sum(dom_n * o_n, axis=1, keepdims=True)
                ls_c = jnp.max(jnp.where(hm, l_c, NEG), axis=1, keepdims=True)
                ls_n = jnp.max(jnp.where(hm, l_n, NEG), axis=1, keepdims=True)
                dob_c, dob_n = dom_c.astype(BF16), dom_n.astype(BF16)
                mm = [(_dot_nt(qm_c, k_c), _dot_nt(dob_c, v_c)),
                      None if shared else (_dot_nt(qm_c, k_p), _dot_nt(dob_c, v_p)),
                      (_dot_nt(qm_n, k_c), _dot_nt(dob_n, v_c))]
                heads.append(dict(hq=hq, qm_c=qm_c, qm_n=qm_n, dob_c=dob_c, dob_n=dob_n, mm=mm,
                                  dd=(dd_c, dd_c, dd_n), ls=(ls_c, ls_c, ls_n)))
            return dict(rows=rows, k_c=k_c, k_p=k_p, heads=heads, oks=(cur_ok, pok_c, pok_n), shared=shared)

        def several(blocks):
            work = []
            for i, (g, r) in enumerate(blocks):
                work.append(one(g, r, i > 0 and blocks[i - 1] == (g - 1, r)))
            for i, w in enumerate(work):
                if w["shared"]:
                    w["k_p"] = work[i - 1]["k_c"]
                for hi, h in enumerate(w["heads"]):
                    slope, dist = slopes[h["hq"]], (dcur, dprev, dprev)
                    h["pr"], h["ds"] = [], []
                    for j in range(3):
                        if h["mm"][j] is None:
                            h["pr"].append(work[i - 1]["heads"][hi]["pr"][2])
                            h["ds"].append(work[i - 1]["heads"][hi]["ds"][2])
                            continue
                        s = jnp.where(w["oks"][j], h["mm"][j][0] * 0.125 - slope * dist[j], NEG)
                        pr = jnp.exp(s - h["ls"][j])
                        h["pr"].append(pr.astype(BF16))
                        h["ds"].append((pr * (h["mm"][j][1] - h["dd"][j])).astype(BF16))
            for w in work:
                dq = jnp.zeros((QBLK, PAIR_W), F32)
                dk = jnp.zeros((QBLK, PAIR_W), F32)
                dv = jnp.zeros((QBLK, PAIR_W), F32)
                for h in w["heads"]:
                    ds, pr = h["ds"], h["pr"]
                    dq_h = _dot(ds[0], w["k_c"]) + _dot(ds[1], w["k_p"])
                    dk += (_dot_tn(ds[0], h["qm_c"]) + _dot_tn(ds[2], h["qm_n"])) * 0.125
                    dv += _dot_tn(pr[0], h["dob_c"]) + _dot_tn(pr[2], h["dob_n"])
                    dq = jnp.where(lane_head == h["hq"], dq_h * 0.125, dq)
                for part, val in enumerate((dq, dk, dv)):
                    if has_acc:
                        val = val + acc_ref.at[part][w["rows"], :]
                    if accb_ref is not None and part > 0:
                        val = val + accb_ref.at[part - 1][w["rows"], :]
                    out_ref.at[part][w["rows"], :] = val

        _for_query_blocks(d, groups, several)

    cur = lambda part: _pair_spec(rb, part, lambda n: n)
    prv = lambda part: _pair_spec(sb, part, lambda n: jnp.maximum(n * groups - 1, 0))
    nxt = lambda part: _pair_spec(sb, part, lambda n: jnp.minimum((n + 1) * groups, T // sb - 1))
    full = pl.BlockSpec((3, None, rb, PAIR_W), lambda p, n: (0, p // 2, n, p % 2))
    in_specs = [cur(0), nxt(0), cur(1), prv(1), cur(2), prv(2), cur(0), nxt(0), cur(0), nxt(0), cur(0), nxt(0)]
    args = [proj, proj, proj, proj, proj, proj, dcat, dcat, o, o, lse, lse]
    if has_acc:
        in_specs.append(full)
        args.append(acc)
    if acc_b is not None:
        in_specs.append(pl.BlockSpec((2, None, rb, PAIR_W), lambda p, n: (0, p // 2, n, p % 2)))
        args.append(acc_b)
    if du is not None:
        in_specs.append(cur(0))
        args.append(du)
    out_spec = pl.BlockSpec((n_parts, None, rb, PAIR_W), lambda p, n: (0, p // 2, n, p % 2))
    return pl.pallas_call(
        body, grid=(4, nblk), in_specs=in_specs, out_specs=out_spec,
        out_shape=SDS((n_parts, 2, T, 2 * PAIR_W), F32), compiler_params=_params(2), name=name)(*args)


def attn_bwd_split(proj, dcat, o, lse, acc, acc_b, d, name):
    T = proj.shape[1]
    sb, groups = QBLK * d, ATTN_GROUP[d]
    rb = sb * groups
    nblk = T // rb
    has_acc = acc is not None
    assert ATTN_UNROLL % groups == 0

    def body(*refs):
        qc_ref, kc_ref, kp_ref, vc_ref, vp_ref, dc_ref, oc_ref, lc_ref = refs[:8]
        acc_ref, accb_ref = (refs[8], refs[9]) if has_acc else (None, None)
        a_ref, b_ref = refs[-2:]
        p, n = pl.program_id(0), pl.program_id(1)
        b_ref[...] = jnp.zeros_like(b_ref)

        @pl.when(n < nblk)
        def _():
            cur_ok, prev_ok, dcur, dprev = _attn_masks()
            first_ok = jnp.logical_and(prev_ok, n > 0)
            lane_head = lax.broadcasted_iota(jnp.int32, (QBLK, PAIR_W), 1) // HEAD_DIM
            slopes = _head_slopes(p, d)

            def one(g, r):
                rows = _rows(g * sb + r, d)
                q_c, do_c, o_c, l_c = qc_ref[rows, :], dc_ref[rows, :], oc_ref[rows, :], lc_ref[rows, :]
                k_c, v_c = kc_ref[rows, :].astype(BF16), vc_ref[rows, :].astype(BF16)
                if g == 0:
                    prow, pok = _rows(r, d), first_ok
                    k_p, v_p = kp_ref[prow, :].astype(BF16), vp_ref[prow, :].astype(BF16)
                else:
                    prow, pok = _rows((g - 1) * sb + r, d), prev_ok
                    k_p, v_p = kc_ref[prow, :].astype(BF16), vc_ref[prow, :].astype(BF16)
                heads = []
                for hq in range(2):
                    hm = lane_head == hq
                    qm = jnp.where(hm, q_c, 0.0).astype(BF16)
                    dom = jnp.where(hm, do_c, 0.0)
                    dd = jnp.sum(dom * o_c, axis=1, keepdims=True)
                    ls = jnp.max(jnp.where(hm, l_c, NEG), axis=1, keepdims=True)
                    dob = dom.astype(BF16)
                    mm = [(_dot_nt(qm, k_c), _dot_nt(dob, v_c)), (_dot_nt(qm, k_p), _dot_nt(dob, v_p))]
                    heads.append(dict(hq=hq, qm=qm, dob=dob, mm=mm, dd=dd, ls=ls))
                return dict(g=g, r=r, rows=rows, k_c=k_c, k_p=k_p, heads=heads, oks=(cur_ok, pok))

            def several(blocks):
                work = [one(g, r) for g, r in blocks]
                for w in work:
                    for h in w["heads"]:
                        h["pr"], h["ds"] = [], []
                        for j, dist in enumerate((dcur, dprev)):
                            s = jnp.where(w["oks"][j], h["mm"][j][0] * 0.125 - slopes[h["hq"]] * dist, NEG)
                            pr = jnp.exp(s - h["ls"])
                            h["pr"].append(pr.astype(BF16))
                            h["ds"].append((pr * (h["mm"][j][1] - h["dd"])).astype(BF16))
                for w in work:
                    zero = jnp.zeros((QBLK, PAIR_W), F32)
                    dq, w["dk"], w["dv"], w["dk_prev"], w["dv_prev"] = zero, zero, zero, zero, zero
                    for h in w["heads"]:
                        ds, pr = h["ds"], h["pr"]
                        dq_h = _dot(ds[0], w["k_c"]) + _dot(ds[1], w["k_p"])
                        dq = jnp.where(lane_head == h["hq"], dq_h * 0.125, dq)
                        w["dk"] += _dot_tn(ds[0], h["qm"]) * 0.125
                        w["dv"] += _dot_tn(pr[0], h["dob"])
                        w["dk_prev"] += _dot_tn(ds[1], h["qm"]) * 0.125
                        w["dv_prev"] += _dot_tn(pr[1], h["dob"])
                    a_ref.at[0][w["rows"], :] = dq + acc_ref.at[0][w["rows"], :] if has_acc else dq
                for i, w in enumerate(work):
                    dk, dv = w["dk"], w["dv"]
                    if i + 1 < len(work) and (work[i + 1]["g"], work[i + 1]["r"]) == (w["g"] + 1, w["r"]):
                        dk, dv = dk + work[i + 1]["dk_prev"], dv + work[i + 1]["dv_prev"]
                    if has_acc:
                        dk = dk + acc_ref.at[1][w["rows"], :] + accb_ref.at[0][w["rows"], :]
                        dv = dv + acc_ref.at[2][w["rows"], :] + accb_ref.at[1][w["rows"], :]
                    a_ref.at[1][w["rows"], :] = dk
                    a_ref.at[2][w["rows"], :] = dv
                    if w["g"] == 0:
                        brow = _rows((groups - 1) * sb + w["r"], d)
                        b_ref.at[0][brow, :] = w["dk_prev"]
                        b_ref.at[1][brow, :] = w["dv_prev"]

            _for_query_blocks(d, groups, several)

    here = lambda n: jnp.minimum(n, nblk - 1)
    cur = lambda part: _pair_spec(rb, part, here)
    prv = lambda part: _pair_spec(sb, part, lambda n: jnp.maximum(here(n) * groups - 1, 0))
    a_spec = pl.BlockSpec((3, None, rb, PAIR_W), lambda p, n: (0, p // 2, here(n), p % 2))
    b_spec = lambda blk: pl.BlockSpec((2, None, rb, PAIR_W), lambda p, n: (0, p // 2, blk(n), p % 2))
    in_specs = [cur(0), cur(1), prv(1), cur(2), prv(2), cur(0), cur(0), cur(0)]
    args = [proj, proj, proj, proj, proj, dcat, o, lse]
    if has_acc:
        in_specs += [a_spec, b_spec(here)]
        args += [acc, acc_b]
    return pl.pallas_call(
        body, grid=(4, nblk + 1), in_specs=in_specs,
        out_specs=[a_spec, b_spec(lambda n: jnp.maximum(n - 1, 0))],
        out_shape=[SDS((3, 2, T, 2 * PAIR_W), F32), SDS((2, 2, T, 2 * PAIR_W), F32)],
        compiler_params=_params(2), name=name)(*args)


def _scan_rows(buf, tab_ref, reverse, half):
    n_tiles = (buf.shape[0] - 8) // 8
    per_half = HALF_STATES // SCAN_CW
    row = lax.broadcasted_iota(jnp.int32, (8, SCAN_CW), 0)
    sgn = -1.0 if reverse else 1.0

    for j in range(per_half):
        c0 = half * 2 * HALF_STATES + j * SCAN_CW
        cre = pl.ds(c0, SCAN_CW)
        cim = pl.ds(c0 + HALF_STATES, SCAN_CW)
        steps = []
        for s, k in enumerate((1, 2, 4)):
            ok, shift = (row < 8 - k, 8 - k) if reverse else (row >= k, k)
            steps.append((shift, jnp.where(ok, tab_ref[pl.ds(s, 1), cre], 0.0),
                          jnp.where(ok, sgn * tab_ref[pl.ds(s, 1), cim], 0.0)))
        trow = 16 if reverse else 8
        pr, pi = tab_ref[pl.ds(trow, 8), cre], tab_ref[pl.ds(trow, 8), cim]
        for t in range(n_tiles):
            base = 8 * (n_tiles - 1 - t) if reverse else 8 + 8 * t
            rows = pl.ds(base, 8)
            re, im = buf[rows, cre], buf[rows, cim]
            for shift, ar, ai in steps:
                sre, sim = pltpu.roll(re, shift, 0), pltpu.roll(im, shift, 0)
                re, im = re + ar * sre - ai * sim, im + ar * sim + ai * sre
            crow = pl.ds(base + 8 if reverse else base - 1, 1)
            cr, ci = buf[crow, cre], buf[crow, cim]
            buf[rows, cre] = re + pr * cr - pi * ci
            buf[rows, cim] = im + pr * ci + pi * cr


def ssm_fwd(proj, bh, ch, apow, dskip, name, comm=None):
    _, T, C = proj.shape
    tm = SCAN_TM
    SW = 4 * HALF_STATES

    def body(u_ref, bh_ref, ch_ref, tab_ref, dsk_ref, y_ref, s_ref, buf):
        @pl.when(pl.program_id(0) == 0)
        def _():
            buf[pl.ds(0, 8), :] = jnp.zeros((8, SW), F32)

        for h in range(2):
            buf[pl.ds(8, tm), pl.ds(h * 2 * HALF_STATES, 2 * HALF_STATES)] = _dot(u_ref[h].astype(BF16), bh_ref[h])
        for h in range(2):
            cols = pl.ds(h * 2 * HALF_STATES, 2 * HALF_STATES)
            _scan_rows(buf, tab_ref, False, h)
            sv = buf[pl.ds(8, tm), cols]
            s_ref[:, cols] = sv
            y_ref[h] = _dot(sv.astype(BF16), ch_ref[h]) + dsk_ref[h] * u_ref[h]
        buf[pl.ds(0, 8), :] = buf[pl.ds(tm, 8), :]

    return _call(
        body, grid=(T // tm,),
        in_specs=[pl.BlockSpec((2, tm, C), lambda i: (3, i, 0)),
                  pl.BlockSpec((2, C, 2 * HALF_STATES), lambda i: (0, 0, 0)),
                  pl.BlockSpec((2, 2 * HALF_STATES, C), lambda i: (0, 0, 0)),
                  pl.BlockSpec((24, SW), lambda i: (0, 0)),
                  pl.BlockSpec((2, 1, C), lambda i: (0, 0, 0))],
        out_specs=[pl.BlockSpec((2, tm, C), lambda i: (0, i, 0)), pl.BlockSpec((tm, SW), lambda i: (i, 0))],
        out_shape=[SDS((2, T, C), F32), SDS((T, SW), F32)],
        scratch_shapes=[pltpu.VMEM((tm + 8, SW), F32)],
        args=(proj, bh, ch, apow, dskip), name=name, comm=comm)


def ssm_bwd(dy, proj, st, bh, ch, apow, dskip, name, comm=None):
    _, T, C = proj.shape
    tm = SCAN_TM_BWD
    nt = T // tm
    SW = 4 * HALF_STATES
    HS2 = 2 * HALF_STATES

    def body(dy_ref, u_ref, s_ref, sp_ref, bh_ref, ch_ref, tab_ref, dsk_ref,
             du_ref, da_ref, dbh_ref, dch_ref, dd_ref, lam):
        i = pl.program_id(0)

        @pl.when(i == 0)
        def _():
            lam[pl.ds(tm, 8), :] = jnp.zeros((8, SW), F32)
            da_ref[...] = jnp.zeros_like(da_ref)
            dbh_ref[...] = jnp.zeros_like(dbh_ref)
            dch_ref[...] = jnp.zeros_like(dch_ref)
            dd_ref[...] = jnp.zeros_like(dd_ref)

        for h in range(2):
            lam[pl.ds(0, tm), pl.ds(h * HS2, HS2)] = _dot_nt(dy_ref[h].astype(BF16), ch_ref[h])
        for h in range(2):
            dyv, uv = dy_ref[h], u_ref[h]
            dch_ref[h] += _dot_tn(s_ref[:, pl.ds(h * HS2, HS2)].astype(BF16), dyv.astype(BF16))
            dd_ref[h] += jnp.sum(dyv * uv, axis=0, keepdims=True)
        for h in range(2):
            _scan_rows(lam, tab_ref, True, h)
            lb = lam[pl.ds(0, tm), pl.ds(h * HS2, HS2)].astype(BF16)
            du_ref[h] = _dot_nt(lb, bh_ref[h]) + dsk_ref[h] * dy_ref[h]
            dbh_ref[h] += _dot_tn(u_ref[h].astype(BF16), lb)

        first = i == nt - 1
        per_half = HALF_STATES // SCAN_CW

        def chunk(j, _):
            c0 = pl.multiple_of((j // per_half) * HS2 + (j % per_half) * SCAN_CW, 128)
            cre, cim = pl.ds(c0, SCAN_CW), pl.ds(pl.multiple_of(c0 + HALF_STATES, 128), SCAN_CW)
            row0 = lax.broadcasted_iota(jnp.int32, (8, SCAN_CW), 0) == 0
            acc_r = jnp.zeros((8, SCAN_CW), F32)
            acc_i = jnp.zeros((8, SCAN_CW), F32)
            for t in range(tm // 8):
                rows = pl.ds(8 * t, 8)
                if t == 0:
                    pre = jnp.where(first, 0.0, sp_ref[pl.ds(7, 1), cre])
                    pim = jnp.where(first, 0.0, sp_ref[pl.ds(7, 1), cim])
                else:
                    pre, pim = s_ref[pl.ds(8 * t - 1, 1), cre], s_ref[pl.ds(8 * t - 1, 1), cim]
                spr = jnp.where(row0, pre, pltpu.roll(s_ref[rows, cre], 1, 0))
                spi = jnp.where(row0, pim, pltpu.roll(s_ref[rows, cim], 1, 0))
                lr, li = lam[rows, cre], lam[rows, cim]
                acc_r += lr * spr + li * spi
                acc_i += li * spr - lr * spi
            da_ref[:, cre] += jnp.sum(acc_r, axis=0, keepdims=True)
            da_ref[:, cim] += jnp.sum(acc_i, axis=0, keepdims=True)
            return 0

        lax.fori_loop(0, 2 * per_half, chunk, 0)
        lam[pl.ds(tm, 8), :] = lam[pl.ds(0, 8), :]

    rev = lambda i: nt - 1 - i
    return _call(
        body, grid=(nt,),
        in_specs=[pl.BlockSpec((2, tm, C), lambda i: (0, rev(i), 0)),
                  pl.BlockSpec((2, tm, C), lambda i: (3, rev(i), 0)),
                  pl.BlockSpec((tm, SW), lambda i: (rev(i), 0)),
                  pl.BlockSpec((8, SW), lambda i: (jnp.maximum(rev(i) * (tm // 8) - 1, 0), 0)),
                  _resident((2, C, HS2)), _resident((2, HS2, C)), _resident((24, SW)), _resident((2, 1, C))],
        out_specs=[pl.BlockSpec((2, tm, C), lambda i: (0, rev(i), 0)),
                   _resident((1, SW)), _resident((2, C, HS2)), _resident((2, HS2, C)), _resident((2, 1, C))],
        out_shape=[SDS((2, T, C), F32), SDS((1, SW), F32), SDS((2, C, HS2), F32), SDS((2, HS2, C), F32),
                   SDS((2, 1, C), F32)],
        scratch_shapes=[pltpu.VMEM((tm + 8, SW), F32)],
        args=(dy, proj, st, st, bh, ch, apow, dskip), name=name, comm=comm)


_GELU_C = math.sqrt(2.0 / math.pi)


def _gelu(x):
    t = jnp.tanh(_GELU_C * (x + 0.044715 * x * x * x))
    return 0.5 * x * (1.0 + t), t


def glu_fwd(y, w, b, name):
    _, T, C = y.shape
    tm = 512

    def body(y_ref, w_ref, b_ref, o_ref, lg_ref):
        y0, _ = _gelu(y_ref[0])
        y1, _ = _gelu(y_ref[1])
        lg = _dot(y0.astype(BF16), w_ref[0]) + _dot(y1.astype(BF16), w_ref[1]) + b_ref[...]
        sg = _sigmoid(lg)
        o_ref[0] = y0 * sg[:, :C]
        o_ref[1] = y1 * sg[:, C:]
        lg_ref[0] = lg[:, :C]
        lg_ref[1] = lg[:, C:]

    return pl.pallas_call(
        body, grid=(T // tm,),
        in_specs=[pl.BlockSpec((2, tm, C), lambda i: (0, i, 0)), pl.BlockSpec((2, C, 2 * C), lambda i: (0, 0, 0)),
                  pl.BlockSpec((1, 2 * C), lambda i: (0, 0))],
        out_specs=[pl.BlockSpec((2, tm, C), lambda i: (0, i, 0)), pl.BlockSpec((2, tm, C), lambda i: (0, i, 0))],
        out_shape=[SDS((2, T, C), F32), SDS((2, T, C), F32)], compiler_params=_params(1), name=name)(y, w, b)


def glu_bwd(dcat, y, lg, w, name):
    _, T, C = y.shape
    tm = 512

    def body(d_ref, y_ref, lg_ref, w_ref, dy_ref, dw_ref, db_ref):
        @pl.when(pl.program_id(0) == 0)
        def _():
            dw_ref[...] = jnp.zeros_like(dw_ref)
            db_ref[...] = jnp.zeros_like(db_ref)

        y2, th, sg, dlg = [], [], [], []
        for h in range(2):
            yy, tt = _gelu(y_ref[h])
            ss = _sigmoid(lg_ref[h])
            y2.append(yy)
            th.append(tt)
            sg.append(ss)
            dlg.append(d_ref[h] * yy * ss * (1.0 - ss))
        dl = jnp.concatenate(dlg, axis=1)
        dlb = dl.astype(BF16)
        db_ref[...] += jnp.sum(dl, axis=0, keepdims=True)
        for h in range(2):
            dy2 = d_ref[h] * sg[h] + _dot_nt(dlb, w_ref[h])
            yv = y_ref[h]
            dgelu = 0.5 * (1.0 + th[h]) + 0.5 * yv * (1.0 - th[h] * th[h]) * _GELU_C * (1.0 + 3 * 0.044715 * yv * yv)
            dy_ref[h] = dy2 * dgelu
            dw_ref[h] += _dot_tn(y2[h].astype(BF16), dlb)

    return pl.pallas_call(
        body, grid=(T // tm,),
        in_specs=[pl.BlockSpec((2, tm, C), lambda i: (1, i, 0)), pl.BlockSpec((2, tm, C), lambda i: (0, i, 0)),
                  pl.BlockSpec((2, tm, C), lambda i: (0, i, 0)), pl.BlockSpec((2, C, 2 * C), lambda i: (0, 0, 0))],
        out_specs=[pl.BlockSpec((2, tm, C), lambda i: (0, i, 0)), pl.BlockSpec((2, C, 2 * C), lambda i: (0, 0, 0)),
                   pl.BlockSpec((1, 2 * C), lambda i: (0, 0))],
        out_shape=[SDS((2, T, C), F32), SDS((2, C, 2 * C), F32), SDS((1, 2 * C), F32)],
        compiler_params=_params(1), name=name)(dcat, y, lg, w)


def adamw(w, m, v, slots, name):
    R, C = w.shape
    tr = R
    for cand in (512, 256, 128, 64, 32, 16, 8):
        if R % cand == 0 and cand * C * 4 <= 2 * 1024 * 1024:
            tr = cand
            break
    c1 = 1.0 / (1.0 - ADAM_B1 ** ADAM_STEP)
    c2 = 1.0 / (1.0 - ADAM_B2 ** ADAM_STEP)

    def body(w_ref, m_ref, v_ref, s_ref, g_ref, d_ref, nm_ref, nv_ref):
        g = s_ref[0].astype(F32)
        for j in range(1, N_DEV):
            g = g + s_ref[j].astype(F32)
        nm = ADAM_B1 * m_ref[...] + (1.0 - ADAM_B1) * g
        nv = ADAM_B2 * v_ref[...] + (1.0 - ADAM_B2) * (g * g)
        g_ref[...] = g
        nm_ref[...] = nm
        nv_ref[...] = nv
        d_ref[...] = -ADAM_LR * ((nm * c1) / (jnp.sqrt(nv * c2) + ADAM_EPS) + ADAM_WD * w_ref[...])

    spec = pl.BlockSpec((tr, C), lambda i: (i, 0))
    return pl.pallas_call(
        body, grid=(R // tr,),
        in_specs=[spec, spec, spec, pl.BlockSpec((N_DEV, tr, C), lambda i: (0, i, 0))],
        out_specs=[spec] * 4, out_shape=[SDS((R, C), F32)] * 4, compiler_params=_params(1), name=name)(w, m, v, slots)


def _discretise(a_re, a_im, log_dt, b_re, b_im):
    dt = jnp.exp(log_dt)[:, None]
    e = jnp.exp(dt * a_re)
    ar, ai = e * jnp.cos(dt * a_im), e * jnp.sin(dt * a_im)
    den = a_re * a_re + a_im * a_im
    nr, ni = ar - 1.0, ai
    wr = (nr * a_re + ni * a_im) / den
    wi = (ni * a_re - nr * a_im) / den
    bbr = wr[..., None] * b_re - wi[..., None] * b_im
    bbi = wr[..., None] * b_im + wi[..., None] * b_re
    return ar, ai, bbr, bbi


def _block_diag(t):
    eye = jnp.eye(16, dtype=t.dtype).reshape(1, 16, 1, 16, 1)
    r, c = t.shape[1], t.shape[2]
    return (t.reshape(2, 16, r, 1, c) * eye).reshape(2, 16 * r, 16 * c)


def _diag_blocks(m, r, c):
    eye = jnp.eye(16, dtype=m.dtype).reshape(1, 16, 1, 16, 1)
    return jnp.sum(m.reshape(2, 16, r, 16, c) * eye, axis=3).reshape(32, r, c)


def _state_vec(re, im):
    return jnp.stack([re.reshape(2, HALF_STATES), im.reshape(2, HALF_STATES)], axis=1).reshape(-1)


BIG = ("ffn1_w_in", "ffn1_w_out", "w_mix_in", "w_glu", "w_mix_out", "ffn2_w_in", "ffn2_w_out")
WEIGHTS = ("ffn1_pre_g", "ffn1_w_in", "ffn1_w_out", "ffn1_post_g", "mix_pre_g", "w_mix_in", "a_re", "a_im", "log_dt",
           "b_re", "b_im", "c_re", "c_im", "d_skip", "w_glu", "b_glu", "w_mix_out", "mix_post_g", "ffn2_pre_g",
           "ffn2_w_in", "ffn2_w_out", "ffn2_post_g")
SMALL = tuple(n for n in WEIGHTS if n not in BIG)
TRANSPOSED = ("ffn1_w_in", "ffn2_w_in")
PACK_COLS = 1024


def _pack(parts):
    flat = jnp.concatenate([p.reshape(-1) for p in parts])
    rows = -(-flat.shape[0] // (8 * PACK_COLS)) * 8
    return jnp.pad(flat, (0, rows * PACK_COLS - flat.shape[0])).reshape(rows, PACK_COLS)


def _unpack(packed, shapes):
    flat, out, off = packed.reshape(-1), [], 0
    for s in shapes:
        n = math.prod(s)
        out.append(flat[off:off + n].reshape(s))
        off += n
    return out


def _gather(names, wb):
    return [wb[n] for n in names], [False] * len(names)


def _ffn_bwd(dy, do, saved, x, pre_g, w_in, w_out4, tag, post=None, dwout_comm=None):
    h, z, a = saved
    T = x.shape[0]
    dz = ffn_dact(do, w_out4, z, f"{tag}_dact")
    dz8 = dz.reshape(8, T, dz.shape[-1])
    dw_out, extra = mm_tn(a, do, True, False, 4, f"{tag}_dwout", comm=dwout_comm)
    dw_in, (s_out,) = mm_tn(dz8, h, True, False, 8, f"{tag}_dwin", comm=([dw_out.reshape(8, -1, D_MODEL)], [True]))
    outs, (s_in,) = dh_pre_bwd(dz8, w_in, x, pre_g, dy, f"{tag}_dh", comm=([dw_in], [True]), post=post,
                               w_transposed=True)
    return outs, (s_in, s_out), extra


def local_step(x, tgt, sp, wb):
    T = x.shape[0]
    ar, ai, bbr, bbi = _discretise(sp["a_re"], sp["a_im"], sp["log_dt"], sp["b_re"], sp["b_im"])
    powers = [(ar, ai)]
    for _ in range(7):
        pr, pi = powers[-1]
        powers.append((pr * ar - pi * ai, pr * ai + pi * ar))
    zero = jnp.zeros_like(ar)
    rows = [_state_vec(*powers[k - 1]) for k in (1, 2, 4)] + [_state_vec(zero, zero)] * 5
    rows += [_state_vec(pr, pi) for pr, pi in powers]
    rows += [_state_vec(pr, -pi) for pr, pi in reversed(powers)]
    apow = jnp.stack(rows)
    bh = jnp.concatenate([_block_diag(bbr.transpose(0, 2, 1)), _block_diag(bbi.transpose(0, 2, 1))], axis=2)
    ch = jnp.concatenate([_block_diag(sp["c_re"].transpose(0, 2, 1)), _block_diag(-sp["c_im"].transpose(0, 2, 1))], axis=1)
    bh, ch = bh.astype(BF16), ch.astype(BF16)
    dskip = sp["d_skip"].reshape(2, 1, 256)

    w1_in = gather_two_level(wb["ffn1_w_in"], "gather_w1in")
    (h1, z1, a1), (w1_out, w_mi) = ffn_in(
        x, sp["ffn1_pre_g"], w1_in, "ffn1_in", comm=_gather(["ffn1_w_out", "w_mix_in"], wb))
    w1_out4 = w1_out.reshape(4, -1, D_MODEL)
    (o1, x1), (w_glu, w_mo) = mm_acc_norm(
        a1, w1_out4, x, sp["ffn1_post_g"], 0.5, "ffn1_out", comm=_gather(["w_glu", "w_mix_out"], wb))
    w_glu2, w_mo4 = w_glu.reshape(2, 256, 512), w_mo.reshape(4, 256, D_MODEL)
    h2, proj = norm_proj(x1, sp["mix_pre_g"], w_mi, "mix_proj")
    (y_ssm, states), (w2_in,) = ssm_fwd(proj, bh, ch, apow, dskip, "ssm_fwd", comm=_gather(["ffn2_w_in"], wb))
    os_, ls_ = [], []
    for d in DILATIONS:
        (o_d, l_d), got = attn_fwd(proj, d, f"attn_fwd_d{d}",
                                   comm=_gather(["ffn2_w_out"], wb) if d == DILATIONS[-1] else None)
        os_.append(o_d)
        ls_.append(l_d)
    w2_out4 = got[0].reshape(4, -1, D_MODEL)
    o_ssm, lg = glu_fwd(y_ssm, w_glu2, sp["b_glu"], "glu_fwd")
    cat, lse, mixed, x2 = mix_out(os_, ls_, o_ssm, w_mo4, x1, sp["mix_post_g"], "mix_out")
    (h3, z3, a3), _ = ffn_in(x2, sp["ffn2_pre_g"], w2_in, "ffn2_in")
    (dy3, sq, do3, dg_f2post), _ = mm_acc_norm(a3, w2_out4, x2, sp["ffn2_post_g"], 0.5, "ffn2_out", tgt=tgt)

    (dx2, dg_f2pre, dmixed, dg_mpost), (s_w2in, s_w2out), _ = _ffn_bwd(
        dy3, do3, (h3, z3, a3), x2, sp["ffn2_pre_g"], w2_in, w2_out4, "ffn2", post=(mixed, sp["mix_post_g"], 1.0))
    dcat = mm_nt_b(dmixed, w_mo4, "mix_dcat")
    dw_mo, _ = mm_tn(cat, dmixed, True, False, 4, "mix_dwout")
    dy_ssm, dw_glu, db_glu = glu_bwd(dcat, y_ssm, lg, w_glu2, "glu_bwd")
    (du, da, dbh, dch, dd), (s_wmo, s_wglu) = ssm_bwd(
        dy_ssm, proj, states, bh, ch, apow, dskip, "ssm_bwd",
        comm=([dw_mo.reshape(8, 128, D_MODEL), dw_glu.astype(BF16).reshape(8, 64, 512)], [True, True]))
    acc_a, acc_b = None, None
    for d in DILATIONS[:0:-1]:
        acc_a, acc_b = attn_bwd_split(proj, dcat, cat, lse, acc_a, acc_b, d, f"attn_bwd_d{d}")
    dqkv = attn_bwd(proj, dcat, cat, lse, acc_a, DILATIONS[0], f"attn_bwd_d{DILATIONS[0]}", du=du, acc_b=acc_b)
    dproj = dqkv.reshape(8, T, 256)
    dw_mi = mm_tn_shared_a(h2, dproj, "mix_dwin")
    (dx1, dg_mpre, do1, dg_f1post), (s_wmi,) = dh_pre_bwd(
        dproj, w_mi, x1, sp["mix_pre_g"], dx2, "mix_dh", comm=([dw_mi], [True]), post=(o1, sp["ffn1_post_g"], 0.5))

    da4 = da.reshape(2, 2, HALF_STATES)
    d_ar, d_ai = da4[:, 0].reshape(32, N_STATE), da4[:, 1].reshape(32, N_STATE)
    d_bbr = _diag_blocks(dbh[:, :, :HALF_STATES], 16, N_STATE).transpose(0, 2, 1)
    d_bbi = _diag_blocks(dbh[:, :, HALF_STATES:], 16, N_STATE).transpose(0, 2, 1)
    _, disc_vjp = jax.vjp(_discretise, sp["a_re"], sp["a_im"], sp["log_dt"], sp["b_re"], sp["b_im"])
    g_are, g_aim, g_ldt, g_bre, g_bim = disc_vjp((d_ar, d_ai, d_bbr, d_bbi))
    g_cre = _diag_blocks(dch[:, :HALF_STATES], N_STATE, 16).transpose(0, 2, 1)
    g_cim = -_diag_blocks(dch[:, HALF_STATES:], N_STATE, 16).transpose(0, 2, 1)
    small = {
        "ffn1_pre_g": jnp.zeros((1, D_MODEL), F32), "ffn1_post_g": dg_f1post, "mix_pre_g": dg_mpre, "a_re": g_are,
        "a_im": g_aim, "log_dt": g_ldt, "b_re": g_bre, "b_im": g_bim, "c_re": g_cre, "c_im": g_cim,
        "d_skip": dd.reshape(1, 512), "b_glu": db_glu, "mix_post_g": dg_mpost, "ffn2_pre_g": dg_f2pre,
        "ffn2_post_g": dg_f2post,
    }
    (dx0, dg_f1pre), (s_w1in, s_w1out), (early,) = _ffn_bwd(
        dx1, do1, (h1, z1, a1), x, sp["ffn1_pre_g"], w1_in, w1_out4, "ffn1",
        dwout_comm=([_pack([small[n] for n in SMALL])], [False]))
    late = gather_two_level(dg_f1pre, "exchange_small")
    small_slots = lax.dynamic_update_slice(early, late, (0, 0, 0))
    big_slots = {"ffn1_w_in": s_w1in, "ffn1_w_out": s_w1out, "w_mix_in": s_wmi, "w_glu": s_wglu, "w_mix_out": s_wmo,
                 "ffn2_w_in": s_w2in, "ffn2_w_out": s_w2out}
    return sq, dx0, big_slots, small_slots


def kernel(x, ffn1_pre_g, ffn1_w_in, ffn1_w_out, ffn1_post_g, mix_pre_g, w_mix_in, a_re, a_im, log_dt, b_re, b_im, c_re, c_im, d_skip, w_glu, b_glu, w_mix_out, mix_post_g, ffn2_pre_g, ffn2_w_in, ffn2_w_out, ffn2_post_g, loss_target, m_ffn1_pre_g, m_ffn1_w_in, m_ffn1_w_out, m_ffn1_post_g, m_mix_pre_g, m_w_mix_in, m_a_re, m_a_im, m_log_dt, m_b_re, m_b_im, m_c_re, m_c_im, m_d_skip, m_w_glu, m_b_glu, m_w_mix_out, m_mix_post_g, m_ffn2_pre_g, m_ffn2_w_in, m_ffn2_w_out, m_ffn2_post_g, v_ffn1_pre_g, v_ffn1_w_in, v_ffn1_w_out, v_ffn1_post_g, v_mix_pre_g, v_w_mix_in, v_a_re, v_a_im, v_log_dt, v_b_re, v_b_im, v_c_re, v_c_im, v_d_skip, v_w_glu, v_b_glu, v_w_mix_out, v_mix_post_g, v_ffn2_pre_g, v_ffn2_w_in, v_ffn2_w_out, v_ffn2_post_g):
    args = dict(locals())
    w = {n: args[n][0] for n in WEIGHTS}
    m = {n: args["m_" + n][0] for n in WEIGHTS}
    v = {n: args["v_" + n][0] for n in WEIGHTS}

    for d in (w, m, v):
        for n in TRANSPOSED:
            d[n] = jnp.swapaxes(d[n], 0, 1)
    wb = {n: w[n].astype(BF16) for n in BIG}
    sp = {n: w[n] for n in SMALL}
    for n in ("ffn1_pre_g", "ffn1_post_g", "mix_pre_g", "mix_post_g", "ffn2_pre_g", "ffn2_post_g", "b_glu", "d_skip"):
        sp[n] = w[n].reshape(1, -1)

    sq, grad_x, big_slots, small_slots = local_step(x[0], loss_target[0], sp, wb)
    loss = lax.psum(0.5 / D_MODEL * jnp.sum(sq), ("x", "y", "c"))

    outs = {}
    for n in BIG:
        shp = w[n].shape
        r2 = lambda t: t.reshape(-1, shp[-1])
        res = adamw(r2(w[n]), r2(m[n]), r2(v[n]), big_slots[n].reshape(N_DEV, -1, shp[-1]), f"adamw_{n}")
        outs[n] = [(jnp.swapaxes(t, 0, 1) if n in TRANSPOSED else t.reshape(shp))[None] for t in res]
    res = adamw(_pack([w[n] for n in SMALL]), _pack([m[n] for n in SMALL]), _pack([v[n] for n in SMALL]),
                small_slots, "adamw_small")
    shapes = [(1,) + w[n].shape for n in SMALL]
    unpacked = [_unpack(t, shapes) for t in res]
    for j, n in enumerate(SMALL):
        outs[n] = [unpacked[k][j] for k in range(4)]

    result = [loss, grad_x[None]]
    for k in range(4):
        result += [outs[n][k] for n in WEIGHTS]
    return tuple(result)
```

```python
import functools
import math

import jax
import jax.numpy as jnp
from jax import lax
from jax.experimental import pallas as pl
from jax.experimental.pallas import tpu as pltpu

F32, BF16 = jnp.float32, jnp.bfloat16
SDS = jax.ShapeDtypeStruct

D_MODEL = 1024
N_DEV = 8
HEAD_DIM = 64
PAIR_W = 128
QBLK = 128
DILATIONS = (1, 4, 16)
N_STATE = 64
HALF_STATES = 1024
NORM_EPS = 1e-6
NEG = -1e30
VMEM_LIMIT = 56 * 1024 * 1024
ADAM_LR, ADAM_B1, ADAM_B2, ADAM_EPS, ADAM_WD, ADAM_STEP = 1e-3, 0.9, 0.999, 1e-8, 0.01, 10
SCAN_TM = 256
SCAN_TM_BWD = 512
SCAN_CW = 512


def _params(n_grid):
    return pltpu.CompilerParams(dimension_semantics=("arbitrary",) * n_grid, vmem_limit_bytes=VMEM_LIMIT)


def _dot(a, b):
    return jnp.dot(a, b, preferred_element_type=F32)


def _dot_nt(a, b):
    return lax.dot_general(a, b, (((1,), (1,)), ((), ())), preferred_element_type=F32)


def _dot_tn(a, b):
    return lax.dot_general(a, b, (((0,), (0,)), ((), ())), preferred_element_type=F32)


def _sigmoid(v):
    return 0.5 * jnp.tanh(0.5 * v) + 0.5


def _resident(shape):
    return pl.BlockSpec(shape, lambda i: (0,) * len(shape), pipeline_mode=pl.Buffered(1))


ROW_SPLIT = 2


def _exchange_phase(ins, outs, scatter, sems, start):
    send_sems, recv_sems, loc_sems = sems
    x, y, c = lax.axis_index("x"), lax.axis_index("y"), lax.axis_index("c")
    me = 4 * x + 2 * y + c
    own_copies, sends, arrivals = [], [], []
    for i in range(len(ins)):
        own = ins[i].at[me] if scatter[i] else ins[i]
        own_copies.append(pltpu.make_async_copy(own, outs[i].at[me], loc_sems.at[i]))
        for k in range(1, N_DEV):
            px = 1 - x if k & 4 else x
            py = 1 - y if k & 2 else y
            pc = 1 - c if k & 1 else c
            peer = 4 * px + 2 * py + pc
            src = ins[i].at[peer] if scatter[i] else ins[i]
            common = dict(src_ref=src, send_sem=send_sems.at[i, k - 1], recv_sem=recv_sems.at[i, k - 1],
                          device_id=(px, py, pc), device_id_type=pl.DeviceIdType.MESH)
            sends.append(pltpu.make_async_remote_copy(dst_ref=outs[i].at[me], **common))
            if not start:
                arrivals.append(pltpu.make_async_remote_copy(dst_ref=outs[i].at[peer], **common))
    if start:
        for cp in own_copies + sends:
            cp.start()
    else:
        for cp in arrivals:
            cp.wait_recv()
        for cp in sends:
            cp.wait_send()
        for cp in own_copies:
            cp.wait()


def _comm_shapes(arrs, scatter):
    n = len(arrs)
    out_shapes = [SDS(a.shape if scatter[i] else (N_DEV,) + a.shape, a.dtype) for i, a in enumerate(arrs)]
    sems = [pltpu.SemaphoreType.DMA((n, N_DEV - 1)), pltpu.SemaphoreType.DMA((n, N_DEV - 1)),
            pltpu.SemaphoreType.DMA((n,))]
    return out_shapes, sems


def gather_two_level(arr, name):
    def body(x_ref, out_ref, send_sems, recv_sems, local_sem):
        x, y, c = lax.axis_index("x"), lax.axis_index("y"), lax.axis_index("c")
        sibling = (x, y, 1 - c)
        chips = [(1 - x, y), (x, 1 - y), (1 - x, 1 - y)]

        def slot(px, py, pc):
            return out_ref.at[4 * px + 2 * py + pc]

        def copy(k, block, to, src=None):
            return pltpu.make_async_remote_copy(
                src_ref=slot(*block) if src is None else src, dst_ref=slot(*block),
                send_sem=send_sems.at[k], recv_sem=recv_sems.at[k], device_id=to, device_id_type=pl.DeviceIdType.MESH)

        mine = pltpu.make_async_copy(x_ref, slot(x, y, c), local_sem)
        mine.start()
        first = [copy(0, (x, y, c), sibling, src=x_ref)]
        first += [copy(1 + j, (x, y, c), (*chip, c), src=x_ref) for j, chip in enumerate(chips)]
        for cp in first:
            cp.start()
        passed = [copy(4 + j, (*chip, c), sibling) for j, chip in enumerate(chips)]
        for j, chip in enumerate(chips):
            copy(1 + j, (*chip, c), (x, y, c)).wait_recv()
            passed[j].start()
        copy(0, sibling, (x, y, c)).wait_recv()
        for j, chip in enumerate(chips):
            copy(4 + j, (*chip, 1 - c), (x, y, c)).wait_recv()
        for cp in first + passed:
            cp.wait_send()
        mine.wait()

    anyspec = pl.BlockSpec(memory_space=pl.ANY)
    return pl.pallas_call(
        body, in_specs=[anyspec], out_specs=anyspec, out_shape=SDS((N_DEV,) + arr.shape, arr.dtype),
        scratch_shapes=[pltpu.SemaphoreType.DMA((N_DEV - 1,)), pltpu.SemaphoreType.DMA((N_DEV - 1,)),
                        pltpu.SemaphoreType.DMA],
        compiler_params=pltpu.CompilerParams(has_side_effects=True), name=name)(arr)


def _call(body, *, grid, in_specs, out_specs, out_shape, args, name, scratch_shapes=(), comm=None):
    n_grid, scratch_shapes = len(grid), list(scratch_shapes)
    if comm is None:
        outs = pl.pallas_call(body, grid=grid, in_specs=in_specs, out_specs=out_specs, out_shape=out_shape,
                              scratch_shapes=scratch_shapes, compiler_params=_params(n_grid), name=name)(*args)
        return outs, []
    arrs, scatter = comm
    nc, n_in, n_out, n_sc = len(arrs), len(in_specs), len(out_specs), len(scratch_shapes)
    comm_shapes, sems = _comm_shapes(arrs, scatter)

    def wrapped(*refs):
        ins, cins = refs[:n_in], refs[n_in:n_in + nc]
        o0 = n_in + nc
        outs, couts = refs[o0:o0 + n_out], refs[o0 + n_out:o0 + n_out + nc]
        s0 = o0 + n_out + nc
        scratch, sem_refs = refs[s0:s0 + n_sc], refs[s0 + n_sc:]
        first = functools.reduce(jnp.logical_and, [pl.program_id(k) == 0 for k in range(n_grid)])
        last = functools.reduce(jnp.logical_and, [pl.program_id(k) == grid[k] - 1 for k in range(n_grid)])

        @pl.when(first)
        def _():
            _exchange_phase(cins, couts, scatter, sem_refs, True)

        body(*ins, *outs, *scratch)

        @pl.when(last)
        def _():
            _exchange_phase(cins, couts, scatter, sem_refs, False)

    anyspec = pl.BlockSpec(memory_space=pl.ANY)
    res = pl.pallas_call(
        wrapped, grid=grid, in_specs=list(in_specs) + [anyspec] * nc, out_specs=list(out_specs) + [anyspec] * nc,
        out_shape=list(out_shape) + comm_shapes, scratch_shapes=scratch_shapes + sems,
        compiler_params=pltpu.CompilerParams(dimension_semantics=("arbitrary",) * n_grid,
                                             vmem_limit_bytes=VMEM_LIMIT, has_side_effects=True),
        name=name)(*args, *arrs)
    return res[:n_out], res[n_out:]


def _rms(xv, g):
    r = lax.rsqrt(jnp.mean(xv * xv, axis=-1, keepdims=True) + NORM_EPS)
    return (xv * r * g).astype(BF16)


def ffn_in(x, g, w, name, comm=None):
    T, D = x.shape
    F = w.shape[1]
    tm = 512

    def body(x_ref, g_ref, w_ref, h_ref, z_ref, a_ref):
        hv = _rms(x_ref[...], g_ref[...])
        h_ref[...] = hv
        pending = None
        for j in range(5):
            if j < 4:
                zs = (_dot_nt(hv, w_ref[j]), _dot_nt(hv, w_ref[j + 4]))
            if pending is not None:
                zg, zu = pending
                sg = _sigmoid(zg)
                silu = zg * sg
                z_ref[0, j - 1] = (zu * (sg + silu - silu * sg)).astype(BF16)
                z_ref[1, j - 1] = silu.astype(BF16)
                a_ref[j - 1] = (silu * zu).astype(BF16)
            pending = zs

    return _call(
        body, grid=(T // tm,),
        in_specs=[pl.BlockSpec((tm, D), lambda i: (i, 0)), pl.BlockSpec((1, D), lambda i: (0, 0)),
                  _resident((8, F, D))],
        out_specs=[pl.BlockSpec((tm, D), lambda i: (i, 0)), pl.BlockSpec((2, 4, tm, F), lambda i: (0, 0, i, 0)),
                   pl.BlockSpec((4, tm, F), lambda i: (0, i, 0))],
        out_shape=[SDS((T, D), BF16), SDS((2, 4, T, F), BF16), SDS((4, T, F), BF16)],
        args=(x, g, w), name=name, comm=comm)


def norm_proj(x, g, w, name):
    T, K = x.shape
    nb, _, N = w.shape
    tm = 512

    def body(x_ref, g_ref, w_ref, h_ref, o_ref):
        hv = _rms(x_ref[...], g_ref[...])
        h_ref[...] = hv
        for b in range(nb):
            o_ref[b] = _dot(hv, w_ref[b])

    return pl.pallas_call(
        body, grid=(T // tm,),
        in_specs=[pl.BlockSpec((tm, K), lambda i: (i, 0)), pl.BlockSpec((1, K), lambda i: (0, 0)),
                  _resident((nb, K, N))],
        out_specs=[pl.BlockSpec((tm, K), lambda i: (i, 0)), pl.BlockSpec((nb, tm, N), lambda i: (0, i, 0))],
        out_shape=[SDS((T, K), BF16), SDS((nb, T, N), F32)], compiler_params=_params(1), name=name)(x, g, w)


def mm_acc_norm(a, w, xres, g, scale, name, comm=None, tgt=None):
    nb, T, K = a.shape
    D = w.shape[2]
    tm = 512
    rc = tm // ROW_SPLIT
    with_loss = tgt is not None

    def body(a_ref, w_ref, x_ref, g_ref, *rest):
        if with_loss:
            t_ref, dy_ref, sq_ref, do_ref, dg_ref = rest

            @pl.when(pl.program_id(0) == 0)
            def _():
                sq_ref[...] = jnp.zeros_like(sq_ref)
                dg_ref[...] = jnp.zeros_like(dg_ref)
        else:
            o_ref, y_ref = rest
        accs = []
        for c in range(ROW_SPLIT):
            rows = pl.ds(c * rc, rc)
            o = _dot(a_ref[0, rows, :].astype(BF16), w_ref[0])
            for b in range(1, nb):
                o += _dot(a_ref[b, rows, :].astype(BF16), w_ref[b])
            accs.append(o)
        for c, o in enumerate(accs):
            rows = pl.ds(c * rc, rc)
            r = lax.rsqrt(jnp.mean(o * o, axis=-1, keepdims=True) + NORM_EPS)
            y = x_ref[rows, :] + scale * (o * r * g_ref[...])
            if with_loss:
                e = y - t_ref[rows, :]
                dy = e * (1.0 / D)
                dy_ref[rows, :] = dy
                sq_ref[...] += jnp.sum(e * e, axis=0, keepdims=True)
                do, dg = _post_bwd(dy, o, g_ref[...], scale)
                do_ref[rows, :] = do
                dg_ref[...] += dg
            else:
                o_ref[rows, :] = o
                y_ref[rows, :] = y

    tile = pl.BlockSpec((tm, D), lambda i: (i, 0))
    row = pl.BlockSpec((1, D), lambda i: (0, 0))
    in_specs = [pl.BlockSpec((nb, tm, K), lambda i: (0, i, 0)), _resident((nb, K, D)), tile, row]
    args = (a, w, xres, g)
    if with_loss:
        return _call(body, grid=(T // tm,), in_specs=in_specs + [tile], out_specs=[tile, row, tile, row],
                     out_shape=[SDS((T, D), F32), SDS((1, D), F32), SDS((T, D), BF16), SDS((1, D), F32)],
                     args=args + (tgt,), name=name, comm=comm)
    return _call(body, grid=(T // tm,), in_specs=in_specs, out_specs=[tile, tile],
                 out_shape=[SDS((T, D), F32), SDS((T, D), F32)], args=args, name=name, comm=comm)


def _post_bwd(dy, ov, g, scale):
    r = scale * dy
    rstd = lax.rsqrt(jnp.mean(ov * ov, axis=-1, keepdims=True) + NORM_EPS)
    oh = ov * rstd
    rg = r * g
    do = rstd * (rg - oh * jnp.mean(rg * oh, axis=-1, keepdims=True))
    return do.astype(BF16), jnp.sum(r * oh, axis=0, keepdims=True)


def mm_nt_b(gr, w, name):
    T, N = gr.shape
    nb, K, _ = w.shape
    tm = 512

    def body(g_ref, w_ref, o_ref):
        gv = g_ref[...]
        for b in range(nb):
            o_ref[b] = _dot_nt(gv, w_ref[b])

    return pl.pallas_call(
        body, grid=(T // tm,),
        in_specs=[pl.BlockSpec((tm, N), lambda i: (i, 0)), _resident((nb, K, N))],
        out_specs=pl.BlockSpec((nb, tm, K), lambda i: (0, i, 0)),
        out_shape=SDS((nb, T, K), F32), compiler_params=_params(1), name=name)(gr, w)


def ffn_dact(do, w_out, z, name):
    T, D = do.shape
    nb, F, _ = w_out.shape
    tm = 512

    def body(g_ref, w_ref, z_ref, dz_ref):
        gv = g_ref[...]
        pending = None
        for b in range(nb + 1):
            da = _dot_nt(gv, w_ref[b]) if b < nb else None
            if pending is not None:
                dz_ref[0, b - 1] = (pending * z_ref[0, b - 1].astype(F32)).astype(BF16)
                dz_ref[1, b - 1] = (pending * z_ref[1, b - 1].astype(F32)).astype(BF16)
            pending = da

    blk = pl.BlockSpec((2, nb, tm, F), lambda i: (0, 0, i, 0))
    return pl.pallas_call(
        body, grid=(T // tm,),
        in_specs=[pl.BlockSpec((tm, D), lambda i: (i, 0)), _resident((nb, F, D)), blk],
        out_specs=blk, out_shape=SDS((2, nb, T, F), BF16), compiler_params=_params(1), name=name)(do, w_out, z)


def mm_tn(a, g, a_batched, g_batched, nb, name, comm=None):
    T = a.shape[-2]
    K, N = a.shape[-1], g.shape[-1]
    tk = 2048
    nk = T // tk

    def body(a_ref, g_ref, o_ref, acc):
        k = pl.program_id(1)

        @pl.when(k == 0)
        def _():
            acc[...] = jnp.zeros_like(acc)

        acc[...] += _dot_tn(a_ref[...].astype(BF16), g_ref[...].astype(BF16))

        @pl.when(k == nk - 1)
        def _():
            o_ref[...] = acc[...].astype(BF16)

    a_spec = (pl.BlockSpec((None, tk, K), lambda b, k: (b, k, 0)) if a_batched
              else pl.BlockSpec((tk, K), lambda b, k: (k, 0)))
    g_spec = (pl.BlockSpec((None, tk, N), lambda b, k: (b, k, 0)) if g_batched
              else pl.BlockSpec((tk, N), lambda b, k: (k, 0)))
    (out,), slots = _call(
        body, grid=(nb, nk), in_specs=[a_spec, g_spec],
        out_specs=[pl.BlockSpec((None, K, N), lambda b, k: (b, 0, 0))],
        out_shape=[SDS((nb, K, N), BF16)], scratch_shapes=[pltpu.VMEM((K, N), F32)],
        args=(a, g), name=name, comm=comm)
    return out, slots


def mm_tn_shared_a(a, g, name):
    T, K = a.shape
    nb, _, N = g.shape
    tk = 1024
    nk = T // tk

    def body(a_ref, g_ref, o_ref, acc):
        k = pl.program_id(0)

        @pl.when(k == 0)
        def _():
            acc[...] = jnp.zeros_like(acc)

        av = a_ref[...]
        for b in range(nb):
            acc[b] += _dot_tn(av, g_ref[b].astype(BF16))

        @pl.when(k == nk - 1)
        def _():
            o_ref[...] = acc[...].astype(BF16)

    return pl.pallas_call(
        body, grid=(nk,),
        in_specs=[pl.BlockSpec((tk, K), lambda k: (k, 0)), pl.BlockSpec((nb, tk, N), lambda k: (0, k, 0))],
        out_specs=_resident((nb, K, N)), out_shape=SDS((nb, K, N), BF16),
        scratch_shapes=[pltpu.VMEM((nb, K, N), F32)], compiler_params=_params(1), name=name)(a, g)


def dh_pre_bwd(dz, w, x, g, dyres, name, comm=None, post=None, w_transposed=False):
    nb, T, F = dz.shape
    D = x.shape[1]
    tm = 512
    rc = tm // ROW_SPLIT
    mm = _dot if w_transposed else _dot_nt

    def body(dz_ref, w_ref, x_ref, g_ref, dy_ref, *rest):
        if post is None:
            dx_ref, dg_ref = rest
        else:
            o_ref, gp_ref, dx_ref, dg_ref, do_ref, dgp_ref = rest

        @pl.when(pl.program_id(0) == 0)
        def _():
            dg_ref[...] = jnp.zeros_like(dg_ref)
            if post is not None:
                dgp_ref[...] = jnp.zeros_like(dgp_ref)

        accs = []
        for c in range(ROW_SPLIT):
            rows = pl.ds(c * rc, rc)
            dh = mm(dz_ref[0, rows, :].astype(BF16), w_ref[0])
            for b in range(1, nb):
                dh += mm(dz_ref[b, rows, :].astype(BF16), w_ref[b])
            accs.append(dh)
        for c, dh in enumerate(accs):
            rows = pl.ds(c * rc, rc)
            xv = x_ref[rows, :]
            rstd = lax.rsqrt(jnp.mean(xv * xv, axis=-1, keepdims=True) + NORM_EPS)
            xh = xv * rstd
            dg_ref[...] += jnp.sum(dh * xh, axis=0, keepdims=True)
            dhg = dh * g_ref[...]
            dx = dy_ref[rows, :] + rstd * (dhg - xh * jnp.mean(dhg * xh, axis=-1, keepdims=True))
            dx_ref[rows, :] = dx
            if post is not None:
                do, dgp = _post_bwd(dx, o_ref[rows, :], gp_ref[...], post[2])
                do_ref[rows, :] = do
                dgp_ref[...] += dgp

    tile = pl.BlockSpec((tm, D), lambda i: (i, 0))
    row = pl.BlockSpec((1, D), lambda i: (0, 0))
    in_specs = [pl.BlockSpec((nb, tm, F), lambda i: (0, i, 0)), _resident(w.shape), tile, row, tile]
    out_specs, out_shape, args = [tile, row], [SDS((T, D), F32), SDS((1, D), F32)], (dz, w, x, g, dyres)
    if post is not None:
        in_specs += [tile, row]
        out_specs += [tile, row]
        out_shape += [SDS((T, D), BF16), SDS((1, D), F32)]
        args += (post[0], post[1])
    return _call(body, grid=(T // tm,), in_specs=in_specs, out_specs=out_specs, out_shape=out_shape,
                 args=args, name=name, comm=comm)


ATTN_GROUP = {1: 8, 4: 2, 16: 1}
ATTN_GROUP_BWD = {1: 8, 4: 4, 16: 1}
ATTN_UNROLL = 4


def _attn_masks():
    qi = lax.broadcasted_iota(jnp.int32, (QBLK, QBLK), 0)
    kj = lax.broadcasted_iota(jnp.int32, (QBLK, QBLK), 1)
    cur_ok = kj <= qi
    prev_ok = kj >= qi
    dcur = (qi - kj).astype(F32)
    return cur_ok, prev_ok, dcur, dcur + float(QBLK)


def _head_slopes(p, d):
    out = []
    for hq in range(2):
        v = [float(d) * 2.0 ** -(2 * q + hq + 1) for q in range(4)]
        out.append(jnp.where(p == 0, v[0], jnp.where(p == 1, v[1], jnp.where(p == 2, v[2], v[3]))))
    return out


def _rows(start, d):
    return pl.ds(start, QBLK, stride=d) if d > 1 else pl.ds(start, QBLK)


def _pair_spec(rows, part, blk):
    return pl.BlockSpec((None, rows, PAIR_W), lambda p, n: (2 * part + p // 2, blk(n), p % 2))


def _for_query_blocks(d, groups, several):
    blocks = [(g, r) for r in range(d) for g in range(groups)]
    for s in range(0, len(blocks), ATTN_UNROLL):
        several(blocks[s:s + ATTN_UNROLL])


def attn_fwd(proj, d, name, comm=None):
    T = proj.shape[1]
    sb, groups = QBLK * d, ATTN_GROUP[d]
    rb = sb * groups
    nblk = T // rb

    def body(q_ref, kc_ref, kp_ref, vc_ref, vp_ref, o_ref, l_ref):
        p, n = pl.program_id(0), pl.program_id(1)
        cur_ok, prev_ok, dcur, dprev = _attn_masks()
        first_ok = jnp.logical_and(prev_ok, n > 0)
        lane_head = lax.broadcasted_iota(jnp.int32, (QBLK, PAIR_W), 1) // HEAD_DIM
        slopes = _head_slopes(p, d)

        def several(blocks):
            work = []
            for g, r in blocks:
                rows = _rows(g * sb + r, d)
                q = q_ref[rows, :]
                kc, vc = kc_ref[rows, :].astype(BF16), vc_ref[rows, :].astype(BF16)
                if g == 0:
                    prow, pok = _rows(r, d), first_ok
                    kp, vp = kp_ref[prow, :].astype(BF16), vp_ref[prow, :].astype(BF16)
                else:
                    prow, pok = _rows((g - 1) * sb + r, d), prev_ok
                    kp, vp = kc_ref[prow, :].astype(BF16), vc_ref[prow, :].astype(BF16)
                for hq in range(2):
                    qm = jnp.where(lane_head == hq, q, 0.0).astype(BF16)
                    work.append([rows, hq, pok, vc, vp, _dot_nt(qm, kc), _dot_nt(qm, kp)])
            for w in work:
                _, hq, pok, _, _, sc, sp = w
                sc = jnp.where(cur_ok, sc * 0.125 - slopes[hq] * dcur, NEG)
                sp = jnp.where(pok, sp * 0.125 - slopes[hq] * dprev, NEG)
                m = jnp.maximum(jnp.max(sc, axis=1, keepdims=True), jnp.max(sp, axis=1, keepdims=True))
                pc = jnp.exp(sc - m)
                pp = jnp.exp(sp - m)
                den = jnp.sum(pc, axis=1, keepdims=True) + jnp.sum(pp, axis=1, keepdims=True)
                w[5:] = [pc.astype(BF16), pp.astype(BF16), 1.0 / den, m + jnp.log(den)]
            for i in range(0, len(work), 2):
                o_acc = jnp.zeros((QBLK, PAIR_W), F32)
                l_acc = jnp.zeros((QBLK, PAIR_W), F32)
                for rows, hq, _, vc, vp, pc, pp, inv, lse in work[i:i + 2]:
                    hm = lane_head == hq
                    o_acc = jnp.where(hm, (_dot(pc, vc) + _dot(pp, vp)) * inv, o_acc)
                    l_acc = jnp.where(hm, lse, l_acc)
                o_ref[rows, :] = o_acc
                l_ref[rows, :] = l_acc

        _for_query_blocks(d, groups, several)

    cur = lambda part: _pair_spec(rb, part, lambda n: n)
    prv = lambda part: _pair_spec(sb, part, lambda n: jnp.maximum(n * groups - 1, 0))
    return _call(
        body, grid=(4, nblk), in_specs=[cur(0), cur(1), prv(1), cur(2), prv(2)], out_specs=[cur(0), cur(0)],
        out_shape=[SDS((2, T, 2 * PAIR_W), F32), SDS((2, T, 2 * PAIR_W), F32)],
        args=(proj, proj, proj, proj, proj), name=name, comm=comm)


def mix_out(os_, ls_, o_ssm, w, xres, g, name):
    _, T, HW = o_ssm.shape
    D = w.shape[2]
    tm = 512

    def body(o1, o2, o3, l1, l2, l3, s_ref, w_ref, x_ref, g_ref, cat_ref, l_ref, m_ref, y_ref):
        a, b, c = l1[...], l2[...], l3[...]
        m = jnp.maximum(jnp.maximum(a, b), c)
        ea, eb, ec = jnp.exp(a - m), jnp.exp(b - m), jnp.exp(c - m)
        s = ea + eb + ec
        att = (ea * o1[...] + eb * o2[...] + ec * o3[...]) * (1.0 / s)
        ssm = s_ref[...]
        cat_ref[pl.ds(0, 2)] = att
        cat_ref[pl.ds(2, 2)] = ssm
        l_ref[...] = m + jnp.log(s)
        o = _dot(att[0].astype(BF16), w_ref[0]) + _dot(att[1].astype(BF16), w_ref[1])
        o += _dot(ssm[0].astype(BF16), w_ref[2]) + _dot(ssm[1].astype(BF16), w_ref[3])
        r = lax.rsqrt(jnp.mean(o * o, axis=-1, keepdims=True) + NORM_EPS)
        m_ref[...] = o
        y_ref[...] = x_ref[...] + o * r * g_ref[...]

    spec = pl.BlockSpec((2, tm, HW), lambda i: (0, i, 0))
    tile = pl.BlockSpec((tm, D), lambda i: (i, 0))
    return pl.pallas_call(
        body, grid=(T // tm,),
        in_specs=[spec] * 7 + [_resident(w.shape), tile, pl.BlockSpec((1, D), lambda i: (0, 0))],
        out_specs=[pl.BlockSpec((4, tm, HW), lambda i: (0, i, 0)), spec, tile, tile],
        out_shape=[SDS((4, T, HW), F32), SDS((2, T, HW), F32), SDS((T, D), F32), SDS((T, D), F32)],
        compiler_params=_params(1), name=name)(*os_, *ls_, o_ssm, w, xres, g)


def attn_bwd(proj, dcat, o, lse, acc, d, name, du=None):
    T = proj.shape[1]
    sb, groups = QBLK * d, ATTN_GROUP_BWD[d]
    rb = sb * groups
    nblk = T // rb
    has_acc = acc is not None
    n_parts = 3 if du is None else 4

    def body(*refs):
        (qc_ref, qn_ref, kc_ref, kp_ref, vc_ref, vp_ref, dc_ref, dn_ref, oc_ref, on_ref, lc_ref, ln_ref) = refs[:12]
        acc_ref = refs[12] if has_acc else None
        out_ref = refs[-1]
        if du is not None:
            out_ref[3] = refs[-2][...]
        p, n = pl.program_id(0), pl.program_id(1)
        cur_ok, prev_ok, dcur, dprev = _attn_masks()
        first_ok = jnp.logical_and(prev_ok, n > 0)
        last_ok = jnp.logical_and(prev_ok, n < nblk - 1)
        lane_head = lax.broadcasted_iota(jnp.int32, (QBLK, PAIR_W), 1) // HEAD_DIM
        slopes = _head_slopes(p, d)

        def one(g, r, shared):
            rows = _rows(g * sb + r, d)
            q_c, do_c, o_c, l_c = qc_ref[rows, :], dc_ref[rows, :], oc_ref[rows, :], lc_ref[rows, :]
            k_c, v_c = kc_ref[rows, :].astype(BF16), vc_ref[rows, :].astype(BF16)
            if shared:
                pok_c, k_p, v_p = prev_ok, None, None
            elif g == 0:
                prow, pok_c = _rows(r, d), first_ok
                k_p, v_p = kp_ref[prow, :].astype(BF16), vp_ref[prow, :].astype(BF16)
            else:
                prow, pok_c = _rows((g - 1) * sb + r, d), prev_ok
                k_p, v_p = kc_ref[prow, :].astype(BF16), vc_ref[prow, :].astype(BF16)
            if g == groups - 1:
                nrow, pok_n = _rows(r, d), last_ok
                q_n, do_n, o_n, l_n = qn_ref[nrow, :], dn_ref[nrow, :], on_ref[nrow, :], ln_ref[nrow, :]
            else:
                nrow, pok_n = _rows((g + 1) * sb + r, d), prev_ok
                q_n, do_n, o_n, l_n = qc_ref[nrow, :], dc_ref[nrow, :], oc_ref[nrow, :], lc_ref[nrow, :]
            heads = []
            for hq in range(2):
                hm = lane_head == hq
                qm_c = jnp.where(hm, q_c, 0.0).astype(BF16)
                qm_n = jnp.where(hm, q_n, 0.0).astype(BF16)
                dom_c = jnp.where(hm, do_c, 0.0)
                dom_n = jnp.where(hm, do_n, 0.0)
                dd_c = jnp.sum(dom_c * o_c, axis=1, keepdims=True)
                dd_n = jnp.sum(dom_n * o_n, axis=1, keepdims=True)
                ls_c = jnp.max(jnp.where(hm, l_c, NEG), axis=1, keepdims=True)
                ls_n = jnp.max(jnp.where(hm, l_n, NEG), axis=1, keepdims=True)
                dob_c, dob_n = dom_c.astype(BF16), dom_n.astype(BF16)
                mm = [(_dot_nt(qm_c, k_c), _dot_nt(dob_c, v_c)),
                      None if shared else (_dot_nt(qm_c, k_p), _dot_nt(dob_c, v_p)),
                      (_dot_nt(qm_n, k_c), _dot_nt(dob_n, v_c))]
                heads.append(dict(hq=hq, qm_c=qm_c, qm_n=qm_n, dob_c=dob_c, dob_n=dob_n, mm=mm,
                                  dd=(dd_c, dd_c, dd_n), ls=(ls_c, ls_c, ls_n)))
            return dict(rows=rows, k_c=k_c, k_p=k_p, heads=heads, oks=(cur_ok, pok_c, pok_n), shared=shared)

        def several(blocks):
            work = []
            for i, (g, r) in enumerate(blocks):
                work.append(one(g, r, i > 0 and blocks[i - 1] == (g - 1, r)))
            for i, w in enumerate(work):
                if w["shared"]:
                    w["k_p"] = work[i - 1]["k_c"]
                for hi, h in enumerate(w["heads"]):
                    slope, dist = slopes[h["hq"]], (dcur, dprev, dprev)
                    h["pr"], h["ds"] = [], []
                    for j in range(3):
                        if h["mm"][j] is None:
                            h["pr"].append(work[i - 1]["heads"][hi]["pr"][2])
                            h["ds"].append(work[i - 1]["heads"][hi]["ds"][2])
                            continue
                        s = jnp.where(w["oks"][j], h["mm"][j][0] * 0.125 - slope * dist[j], NEG)
                        pr = jnp.exp(s - h["ls"][j])
                        h["pr"].append(pr.astype(BF16))
                        h["ds"].append((pr * (h["mm"][j][1] - h["dd"][j])).astype(BF16))
            for w in work:
                dq = jnp.zeros((QBLK, PAIR_W), F32)
                dk = jnp.zeros((QBLK, PAIR_W), F32)
                dv = jnp.zeros((QBLK, PAIR_W), F32)
                for h in w["heads"]:
                    ds, pr = h["ds"], h["pr"]
                    dq_h = _dot(ds[0], w["k_c"]) + _dot(ds[1], w["k_p"])
                    dk += (_dot_tn(ds[0], h["qm_c"]) + _dot_tn(ds[2], h["qm_n"])) * 0.125
                    dv += _dot_tn(pr[0], h["dob_c"]) + _dot_tn(pr[2], h["dob_n"])
                    dq = jnp.where(lane_head == h["hq"], dq_h * 0.125, dq)
                for part, val in enumerate((dq, dk, dv)):
                    if has_acc:
                        val = val + acc_ref.at[part][w["rows"], :]
                    out_ref.at[part][w["rows"], :] = val

        _for_query_blocks(d, groups, several)

    cur = lambda part: _pair_spec(rb, part, lambda n: n)
    prv = lambda part: _pair_spec(sb, part, lambda n: jnp.maximum(n * groups - 1, 0))
    nxt = lambda part: _pair_spec(sb, part, lambda n: jnp.minimum((n + 1) * groups, T // sb - 1))
    full = pl.BlockSpec((3, None, rb, PAIR_W), lambda p, n: (0, p // 2, n, p % 2))
    in_specs = [cur(0), nxt(0), cur(1), prv(1), cur(2), prv(2), cur(0), nxt(0), cur(0), nxt(0), cur(0), nxt(0)]
    args = [proj, proj, proj, proj, proj, proj, dcat, dcat, o, o, lse, lse]
    if has_acc:
        in_specs.append(full)
        args.append(acc)
    if du is not None:
        in_specs.append(cur(0))
        args.append(du)
    out_spec = pl.BlockSpec((n_parts, None, rb, PAIR_W), lambda p, n: (0, p // 2, n, p % 2))
    return pl.pallas_call(
        body, grid=(4, nblk), in_specs=in_specs, out_specs=out_spec,
        out_shape=SDS((n_parts, 2, T, 2 * PAIR_W), F32), compiler_params=_params(2), name=name)(*args)


def _scan_rows(buf, tab_ref, reverse, half):
    n_tiles = (buf.shape[0] - 8) // 8
    per_half = HALF_STATES // SCAN_CW
    row = lax.broadcasted_iota(jnp.int32, (8, SCAN_CW), 0)
    sgn = -1.0 if reverse else 1.0

    for j in range(per_half):
        c0 = half * 2 * HALF_STATES + j * SCAN_CW
        cre = pl.ds(c0, SCAN_CW)
        cim = pl.ds(c0 + HALF_STATES, SCAN_CW)
        steps = []
        for s, k in enumerate((1, 2, 4)):
            ok, shift = (row < 8 - k, 8 - k) if reverse else (row >= k, k)
            steps.append((shift, jnp.where(ok, tab_ref[pl.ds(s, 1), cre], 0.0),
                          jnp.where(ok, sgn * tab_ref[pl.ds(s, 1), cim], 0.0)))
        trow = 16 if reverse else 8
        pr, pi = tab_ref[pl.ds(trow, 8), cre], tab_ref[pl.ds(trow, 8), cim]
        for t in range(n_tiles):
            base = 8 * (n_tiles - 1 - t) if reverse else 8 + 8 * t
            rows = pl.ds(base, 8)
            re, im = buf[rows, cre], buf[rows, cim]
            for shift, ar, ai in steps:
                sre, sim = pltpu.roll(re, shift, 0), pltpu.roll(im, shift, 0)
                re, im = re + ar * sre - ai * sim, im + ar * sim + ai * sre
            crow = pl.ds(base + 8 if reverse else base - 1, 1)
            cr, ci = buf[crow, cre], buf[crow, cim]
            buf[rows, cre] = re + pr * cr - pi * ci
            buf[rows, cim] = im + pr * ci + pi * cr


def ssm_fwd(proj, bh, ch, apow, dskip, name, comm=None):
    _, T, C = proj.shape
    tm = SCAN_TM
    SW = 4 * HALF_STATES

    def body(u_ref, bh_ref, ch_ref, tab_ref, dsk_ref, y_ref, s_ref, buf):
        @pl.when(pl.program_id(0) == 0)
        def _():
            buf[pl.ds(0, 8), :] = jnp.zeros((8, SW), F32)

        for h in range(2):
            buf[pl.ds(8, tm), pl.ds(h * 2 * HALF_STATES, 2 * HALF_STATES)] = _dot(u_ref[h].astype(BF16), bh_ref[h])
        for h in range(2):
            cols = pl.ds(h * 2 * HALF_STATES, 2 * HALF_STATES)
            _scan_rows(buf, tab_ref, False, h)
            sv = buf[pl.ds(8, tm), cols]
            s_ref[:, cols] = sv
            y_ref[h] = _dot(sv.astype(BF16), ch_ref[h]) + dsk_ref[h] * u_ref[h]
        buf[pl.ds(0, 8), :] = buf[pl.ds(tm, 8), :]

    return _call(
        body, grid=(T // tm,),
        in_specs=[pl.BlockSpec((2, tm, C), lambda i: (3, i, 0)),
                  pl.BlockSpec((2, C, 2 * HALF_STATES), lambda i: (0, 0, 0)),
                  pl.BlockSpec((2, 2 * HALF_STATES, C), lambda i: (0, 0, 0)),
                  pl.BlockSpec((24, SW), lambda i: (0, 0)),
                  pl.BlockSpec((2, 1, C), lambda i: (0, 0, 0))],
        out_specs=[pl.BlockSpec((2, tm, C), lambda i: (0, i, 0)), pl.BlockSpec((tm, SW), lambda i: (i, 0))],
        out_shape=[SDS((2, T, C), F32), SDS((T, SW), F32)],
        scratch_shapes=[pltpu.VMEM((tm + 8, SW), F32)],
        args=(proj, bh, ch, apow, dskip), name=name, comm=comm)


def ssm_bwd(dy, proj, st, bh, ch, apow, dskip, name, comm=None):
    _, T, C = proj.shape
    tm = SCAN_TM_BWD
    nt = T // tm
    SW = 4 * HALF_STATES
    HS2 = 2 * HALF_STATES

    def body(dy_ref, u_ref, s_ref, sp_ref, bh_ref, ch_ref, tab_ref, dsk_ref,
             du_ref, da_ref, dbh_ref, dch_ref, dd_ref, lam):
        i = pl.program_id(0)

        @pl.when(i == 0)
        def _():
            lam[pl.ds(tm, 8), :] = jnp.zeros((8, SW), F32)
            da_ref[...] = jnp.zeros_like(da_ref)
            dbh_ref[...] = jnp.zeros_like(dbh_ref)
            dch_ref[...] = jnp.zeros_like(dch_ref)
            dd_ref[...] = jnp.zeros_like(dd_ref)

        for h in range(2):
            lam[pl.ds(0, tm), pl.ds(h * HS2, HS2)] = _dot_nt(dy_ref[h].astype(BF16), ch_ref[h])
        for h in range(2):
            dyv, uv = dy_ref[h], u_ref[h]
            dch_ref[h] += _dot_tn(s_ref[:, pl.ds(h * HS2, HS2)].astype(BF16), dyv.astype(BF16))
            dd_ref[h] += jnp.sum(dyv * uv, axis=0, keepdims=True)
        for h in range(2):
            _scan_rows(lam, tab_ref, True, h)
            lb = lam[pl.ds(0, tm), pl.ds(h * HS2, HS2)].astype(BF16)
            du_ref[h] = _dot_nt(lb, bh_ref[h]) + dsk_ref[h] * dy_ref[h]
            dbh_ref[h] += _dot_tn(u_ref[h].astype(BF16), lb)

        first = i == nt - 1
        per_half = HALF_STATES // SCAN_CW

        def chunk(j, _):
            c0 = pl.multiple_of((j // per_half) * HS2 + (j % per_half) * SCAN_CW, 128)
            cre, cim = pl.ds(c0, SCAN_CW), pl.ds(pl.multiple_of(c0 + HALF_STATES, 128), SCAN_CW)
            row0 = lax.broadcasted_iota(jnp.int32, (8, SCAN_CW), 0) == 0
            acc_r = jnp.zeros((8, SCAN_CW), F32)
            acc_i = jnp.zeros((8, SCAN_CW), F32)
            for t in range(tm // 8):
                rows = pl.ds(8 * t, 8)
                if t == 0:
                    pre = jnp.where(first, 0.0, sp_ref[pl.ds(7, 1), cre])
                    pim = jnp.where(first, 0.0, sp_ref[pl.ds(7, 1), cim])
                else:
                    pre, pim = s_ref[pl.ds(8 * t - 1, 1), cre], s_ref[pl.ds(8 * t - 1, 1), cim]
                spr = jnp.where(row0, pre, pltpu.roll(s_ref[rows, cre], 1, 0))
                spi = jnp.where(row0, pim, pltpu.roll(s_ref[rows, cim], 1, 0))
                lr, li = lam[rows, cre], lam[rows, cim]
                acc_r += lr * spr + li * spi
                acc_i += li * spr - lr * spi
            da_ref[:, cre] += jnp.sum(acc_r, axis=0, keepdims=True)
            da_ref[:, cim] += jnp.sum(acc_i, axis=0, keepdims=True)
            return 0

        lax.fori_loop(0, 2 * per_half, chunk, 0)
        lam[pl.ds(tm, 8), :] = lam[pl.ds(0, 8), :]

    rev = lambda i: nt - 1 - i
    return _call(
        body, grid=(nt,),
        in_specs=[pl.BlockSpec((2, tm, C), lambda i: (0, rev(i), 0)),
                  pl.BlockSpec((2, tm, C), lambda i: (3, rev(i), 0)),
                  pl.BlockSpec((tm, SW), lambda i: (rev(i), 0)),
                  pl.BlockSpec((8, SW), lambda i: (jnp.maximum(rev(i) * (tm // 8) - 1, 0), 0)),
                  _resident((2, C, HS2)), _resident((2, HS2, C)), _resident((24, SW)), _resident((2, 1, C))],
        out_specs=[pl.BlockSpec((2, tm, C), lambda i: (0, rev(i), 0)),
                   _resident((1, SW)), _resident((2, C, HS2)), _resident((2, HS2, C)), _resident((2, 1, C))],
        out_shape=[SDS((2, T, C), F32), SDS((1, SW), F32), SDS((2, C, HS2), F32), SDS((2, HS2, C), F32),
                   SDS((2, 1, C), F32)],
        scratch_shapes=[pltpu.VMEM((tm + 8, SW), F32)],
        args=(dy, proj, st, st, bh, ch, apow, dskip), name=name, comm=comm)


_GELU_C = math.sqrt(2.0 / math.pi)


def _gelu(x):
    t = jnp.tanh(_GELU_C * (x + 0.044715 * x * x * x))
    return 0.5 * x * (1.0 + t), t


def glu_fwd(y, w, b, name):
    _, T, C = y.shape
    tm = 512

    def body(y_ref, w_ref, b_ref, o_ref, lg_ref):
        y0, _ = _gelu(y_ref[0])
        y1, _ = _gelu(y_ref[1])
        lg = _dot(y0.astype(BF16), w_ref[0]) + _dot(y1.astype(BF16), w_ref[1]) + b_ref[...]
        sg = _sigmoid(lg)
        o_ref[0] = y0 * sg[:, :C]
        o_ref[1] = y1 * sg[:, C:]
        lg_ref[0] = lg[:, :C]
        lg_ref[1] = lg[:, C:]

    return pl.pallas_call(
        body, grid=(T // tm,),
        in_specs=[pl.BlockSpec((2, tm, C), lambda i: (0, i, 0)), pl.BlockSpec((2, C, 2 * C), lambda i: (0, 0, 0)),
                  pl.BlockSpec((1, 2 * C), lambda i: (0, 0))],
        out_specs=[pl.BlockSpec((2, tm, C), lambda i: (0, i, 0)), pl.BlockSpec((2, tm, C), lambda i: (0, i, 0))],
        out_shape=[SDS((2, T, C), F32), SDS((2, T, C), F32)], compiler_params=_params(1), name=name)(y, w, b)


def glu_bwd(dcat, y, lg, w, name):
    _, T, C = y.shape
    tm = 512

    def body(d_ref, y_ref, lg_ref, w_ref, dy_ref, dw_ref, db_ref):
        @pl.when(pl.program_id(0) == 0)
        def _():
            dw_ref[...] = jnp.zeros_like(dw_ref)
            db_ref[...] = jnp.zeros_like(db_ref)

        y2, th, sg, dlg = [], [], [], []
        for h in range(2):
            yy, tt = _gelu(y_ref[h])
            ss = _sigmoid(lg_ref[h])
            y2.append(yy)
            th.append(tt)
            sg.append(ss)
            dlg.append(d_ref[h] * yy * ss * (1.0 - ss))
        dl = jnp.concatenate(dlg, axis=1)
        dlb = dl.astype(BF16)
        db_ref[...] += jnp.sum(dl, axis=0, keepdims=True)
        for h in range(2):
            dy2 = d_ref[h] * sg[h] + _dot_nt(dlb, w_ref[h])
            yv = y_ref[h]
            dgelu = 0.5 * (1.0 + th[h]) + 0.5 * yv * (1.0 - th[h] * th[h]) * _GELU_C * (1.0 + 3 * 0.044715 * yv * yv)
            dy_ref[h] = dy2 * dgelu
            dw_ref[h] += _dot_tn(y2[h].astype(BF16), dlb)

    return pl.pallas_call(
        body, grid=(T // tm,),
        in_specs=[pl.BlockSpec((2, tm, C), lambda i: (1, i, 0)), pl.BlockSpec((2, tm, C), lambda i: (0, i, 0)),
                  pl.BlockSpec((2, tm, C), lambda i: (0, i, 0)), pl.BlockSpec((2, C, 2 * C), lambda i: (0, 0, 0))],
        out_specs=[pl.BlockSpec((2, tm, C), lambda i: (0, i, 0)), pl.BlockSpec((2, C, 2 * C), lambda i: (0, 0, 0)),
                   pl.BlockSpec((1, 2 * C), lambda i: (0, 0))],
        out_shape=[SDS((2, T, C), F32), SDS((2, C, 2 * C), F32), SDS((1, 2 * C), F32)],
        compiler_params=_params(1), name=name)(dcat, y, lg, w)


def adamw(w, m, v, slots, name):
    R, C = w.shape
    tr = R
    for cand in (512, 256, 128, 64, 32, 16, 8):
        if R % cand == 0 and cand * C * 4 <= 2 * 1024 * 1024:
            tr = cand
            break
    c1 = 1.0 / (1.0 - ADAM_B1 ** ADAM_STEP)
    c2 = 1.0 / (1.0 - ADAM_B2 ** ADAM_STEP)

    def body(w_ref, m_ref, v_ref, s_ref, g_ref, d_ref, nm_ref, nv_ref):
        g = s_ref[0].astype(F32)
        for j in range(1, N_DEV):
            g = g + s_ref[j].astype(F32)
        nm = ADAM_B1 * m_ref[...] + (1.0 - ADAM_B1) * g
        nv = ADAM_B2 * v_ref[...] + (1.0 - ADAM_B2) * (g * g)
        g_ref[...] = g
        nm_ref[...] = nm
        nv_ref[...] = nv
        d_ref[...] = -ADAM_LR * ((nm * c1) / (jnp.sqrt(nv * c2) + ADAM_EPS) + ADAM_WD * w_ref[...])

    spec = pl.BlockSpec((tr, C), lambda i: (i, 0))
    return pl.pallas_call(
        body, grid=(R // tr,),
        in_specs=[spec, spec, spec, pl.BlockSpec((N_DEV, tr, C), lambda i: (0, i, 0))],
        out_specs=[spec] * 4, out_shape=[SDS((R, C), F32)] * 4, compiler_params=_params(1), name=name)(w, m, v, slots)


def _discretise(a_re, a_im, log_dt, b_re, b_im):
    dt = jnp.exp(log_dt)[:, None]
    e = jnp.exp(dt * a_re)
    ar, ai = e * jnp.cos(dt * a_im), e * jnp.sin(dt * a_im)
    den = a_re * a_re + a_im * a_im
    nr, ni = ar - 1.0, ai
    wr = (nr * a_re + ni * a_im) / den
    wi = (ni * a_re - nr * a_im) / den
    bbr = wr[..., None] * b_re - wi[..., None] * b_im
    bbi = wr[..., None] * b_im + wi[..., None] * b_re
    return ar, ai, bbr, bbi


def _block_diag(t):
    eye = jnp.eye(16, dtype=t.dtype).reshape(1, 16, 1, 16, 1)
    r, c = t.shape[1], t.shape[2]
    return (t.reshape(2, 16, r, 1, c) * eye).reshape(2, 16 * r, 16 * c)


def _diag_blocks(m, r, c):
    eye = jnp.eye(16, dtype=m.dtype).reshape(1, 16, 1, 16, 1)
    return jnp.sum(m.reshape(2, 16, r, 16, c) * eye, axis=3).reshape(32, r, c)


def _state_vec(re, im):
    return jnp.stack([re.reshape(2, HALF_STATES), im.reshape(2, HALF_STATES)], axis=1).reshape(-1)


BIG = ("ffn1_w_in", "ffn1_w_out", "w_mix_in", "w_glu", "w_mix_out", "ffn2_w_in", "ffn2_w_out")
WEIGHTS = ("ffn1_pre_g", "ffn1_w_in", "ffn1_w_out", "ffn1_post_g", "mix_pre_g", "w_mix_in", "a_re", "a_im", "log_dt",
           "b_re", "b_im", "c_re", "c_im", "d_skip", "w_glu", "b_glu", "w_mix_out", "mix_post_g", "ffn2_pre_g",
           "ffn2_w_in", "ffn2_w_out", "ffn2_post_g")
SMALL = tuple(n for n in WEIGHTS if n not in BIG)
TRANSPOSED = ("ffn1_w_in", "ffn2_w_in")
PACK_COLS = 1024


def _pack(parts):
    flat = jnp.concatenate([p.reshape(-1) for p in parts])
    rows = -(-flat.shape[0] // (8 * PACK_COLS)) * 8
    return jnp.pad(flat, (0, rows * PACK_COLS - flat.shape[0])).reshape(rows, PACK_COLS)


def _unpack(packed, shapes):
    flat, out, off = packed.reshape(-1), [], 0
    for s in shapes:
        n = math.prod(s)
        out.append(flat[off:off + n].reshape(s))
        off += n
    return out


def _gather(names, wb):
    return [wb[n] for n in names], [False] * len(names)


def _ffn_bwd(dy, do, saved, x, pre_g, w_in, w_out4, tag, post=None, dwout_comm=None):
    h, z, a = saved
    T = x.shape[0]
    dz = ffn_dact(do, w_out4, z, f"{tag}_dact")
    dz8 = dz.reshape(8, T, dz.shape[-1])
    dw_out, extra = mm_tn(a, do, True, False, 4, f"{tag}_dwout", comm=dwout_comm)
    dw_in, (s_out,) = mm_tn(dz8, h, True, False, 8, f"{tag}_dwin", comm=([dw_out.reshape(8, -1, D_MODEL)], [True]))
    outs, (s_in,) = dh_pre_bwd(dz8, w_in, x, pre_g, dy, f"{tag}_dh", comm=([dw_in], [True]), post=post,
                               w_transposed=True)
    return outs, (s_in, s_out), extra


def local_step(x, tgt, sp, wb):
    T = x.shape[0]
    ar, ai, bbr, bbi = _discretise(sp["a_re"], sp["a_im"], sp["log_dt"], sp["b_re"], sp["b_im"])
    powers = [(ar, ai)]
    for _ in range(7):
        pr, pi = powers[-1]
        powers.append((pr * ar - pi * ai, pr * ai + pi * ar))
    zero = jnp.zeros_like(ar)
    rows = [_state_vec(*powers[k - 1]) for k in (1, 2, 4)] + [_state_vec(zero, zero)] * 5
    rows += [_state_vec(pr, pi) for pr, pi in powers]
    rows += [_state_vec(pr, -pi) for pr, pi in reversed(powers)]
    apow = jnp.stack(rows)
    bh = jnp.concatenate([_block_diag(bbr.transpose(0, 2, 1)), _block_diag(bbi.transpose(0, 2, 1))], axis=2)
    ch = jnp.concatenate([_block_diag(sp["c_re"].transpose(0, 2, 1)), _block_diag(-sp["c_im"].transpose(0, 2, 1))], axis=1)
    bh, ch = bh.astype(BF16), ch.astype(BF16)
    dskip = sp["d_skip"].reshape(2, 1, 256)

    w1_in = gather_two_level(wb["ffn1_w_in"], "gather_w1in")
    (h1, z1, a1), (w1_out, w_mi) = ffn_in(
        x, sp["ffn1_pre_g"], w1_in, "ffn1_in", comm=_gather(["ffn1_w_out", "w_mix_in"], wb))
    w1_out4 = w1_out.reshape(4, -1, D_MODEL)
    (o1, x1), (w_glu, w_mo) = mm_acc_norm(
        a1, w1_out4, x, sp["ffn1_post_g"], 0.5, "ffn1_out", comm=_gather(["w_glu", "w_mix_out"], wb))
    w_glu2, w_mo4 = w_glu.reshape(2, 256, 512), w_mo.reshape(4, 256, D_MODEL)
    h2, proj = norm_proj(x1, sp["mix_pre_g"], w_mi, "mix_proj")
    (y_ssm, states), (w2_in,) = ssm_fwd(proj, bh, ch, apow, dskip, "ssm_fwd", comm=_gather(["ffn2_w_in"], wb))
    os_, ls_ = [], []
    for d in DILATIONS:
        (o_d, l_d), got = attn_fwd(proj, d, f"attn_fwd_d{d}",
                                   comm=_gather(["ffn2_w_out"], wb) if d == DILATIONS[-1] else None)
        os_.append(o_d)
        ls_.append(l_d)
    w2_out4 = got[0].reshape(4, -1, D_MODEL)
    o_ssm, lg = glu_fwd(y_ssm, w_glu2, sp["b_glu"], "glu_fwd")
    cat, lse, mixed, x2 = mix_out(os_, ls_, o_ssm, w_mo4, x1, sp["mix_post_g"], "mix_out")
    (h3, z3, a3), _ = ffn_in(x2, sp["ffn2_pre_g"], w2_in, "ffn2_in")
    (dy3, sq, do3, dg_f2post), _ = mm_acc_norm(a3, w2_out4, x2, sp["ffn2_post_g"], 0.5, "ffn2_out", tgt=tgt)

    (dx2, dg_f2pre, dmixed, dg_mpost), (s_w2in, s_w2out), _ = _ffn_bwd(
        dy3, do3, (h3, z3, a3), x2, sp["ffn2_pre_g"], w2_in, w2_out4, "ffn2", post=(mixed, sp["mix_post_g"], 1.0))
    dcat = mm_nt_b(dmixed, w_mo4, "mix_dcat")
    dw_mo, _ = mm_tn(cat, dmixed, True, False, 4, "mix_dwout")
    dy_ssm, dw_glu, db_glu = glu_bwd(dcat, y_ssm, lg, w_glu2, "glu_bwd")
    (du, da, dbh, dch, dd), (s_wmo, s_wglu) = ssm_bwd(
        dy_ssm, proj, states, bh, ch, apow, dskip, "ssm_bwd",
        comm=([dw_mo.reshape(8, 128, D_MODEL), dw_glu.astype(BF16).reshape(8, 64, 512)], [True, True]))
    dqkv = None
    for d in DILATIONS:
        dqkv = attn_bwd(proj, dcat, cat, lse, dqkv, d, f"attn_bwd_d{d}", du=du if d == DILATIONS[-1] else None)
    dproj = dqkv.reshape(8, T, 256)
    dw_mi = mm_tn_shared_a(h2, dproj, "mix_dwin")
    (dx1, dg_mpre, do1, dg_f1post), (s_wmi,) = dh_pre_bwd(
        dproj, w_mi, x1, sp["mix_pre_g"], dx2, "mix_dh", comm=([dw_mi], [True]), post=(o1, sp["ffn1_post_g"], 0.5))

    da4 = da.reshape(2, 2, HALF_STATES)
    d_ar, d_ai = da4[:, 0].reshape(32, N_STATE), da4[:, 1].reshape(32, N_STATE)
    d_bbr = _diag_blocks(dbh[:, :, :HALF_STATES], 16, N_STATE).transpose(0, 2, 1)
    d_bbi = _diag_blocks(dbh[:, :, HALF_STATES:], 16, N_STATE).transpose(0, 2, 1)
    _, disc_vjp = jax.vjp(_discretise, sp["a_re"], sp["a_im"], sp["log_dt"], sp["b_re"], sp["b_im"])
    g_are, g_aim, g_ldt, g_bre, g_bim = disc_vjp((d_ar, d_ai, d_bbr, d_bbi))
    g_cre = _diag_blocks(dch[:, :HALF_STATES], N_STATE, 16).transpose(0, 2, 1)
    g_cim = -_diag_blocks(dch[:, HALF_STATES:], N_STATE, 16).transpose(0, 2, 1)
    small = {
        "ffn1_pre_g": jnp.zeros((1, D_MODEL), F32), "ffn1_post_g": dg_f1post, "mix_pre_g": dg_mpre, "a_re": g_are,
        "a_im": g_aim, "log_dt": g_ldt, "b_re": g_bre, "b_im": g_bim, "c_re": g_cre, "c_im": g_cim,
        "d_skip": dd.reshape(1, 512), "b_glu": db_glu, "mix_post_g": dg_mpost, "ffn2_pre_g": dg_f2pre,
        "ffn2_post_g": dg_f2post,
    }
    (dx0, dg_f1pre), (s_w1in, s_w1out), (early,) = _ffn_bwd(
        dx1, do1, (h1, z1, a1), x, sp["ffn1_pre_g"], w1_in, w1_out4, "ffn1",
        dwout_comm=([_pack([small[n] for n in SMALL])], [False]))
    late = gather_two_level(dg_f1pre, "exchange_small")
    small_slots = lax.dynamic_update_slice(early, late, (0, 0, 0))
    big_slots = {"ffn1_w_in": s_w1in, "ffn1_w_out": s_w1out, "w_mix_in": s_wmi, "w_glu": s_wglu, "w_mix_out": s_wmo,
                 "ffn2_w_in": s_w2in, "ffn2_w_out": s_w2out}
    return sq, dx0, big_slots, small_slots


def kernel(x, ffn1_pre_g, ffn1_w_in, ffn1_w_out, ffn1_post_g, mix_pre_g, w_mix_in, a_re, a_im, log_dt, b_re, b_im, c_re, c_im, d_skip, w_glu, b_glu, w_mix_out, mix_post_g, ffn2_pre_g, ffn2_w_in, ffn2_w_out, ffn2_post_g, loss_target, m_ffn1_pre_g, m_ffn1_w_in, m_ffn1_w_out, m_ffn1_post_g, m_mix_pre_g, m_w_mix_in, m_a_re, m_a_im, m_log_dt, m_b_re, m_b_im, m_c_re, m_c_im, m_d_skip, m_w_glu, m_b_glu, m_w_mix_out, m_mix_post_g, m_ffn2_pre_g, m_ffn2_w_in, m_ffn2_w_out, m_ffn2_post_g, v_ffn1_pre_g, v_ffn1_w_in, v_ffn1_w_out, v_ffn1_post_g, v_mix_pre_g, v_w_mix_in, v_a_re, v_a_im, v_log_dt, v_b_re, v_b_im, v_c_re, v_c_im, v_d_skip, v_w_glu, v_b_glu, v_w_mix_out, v_mix_post_g, v_ffn2_pre_g, v_ffn2_w_in, v_ffn2_w_out, v_ffn2_post_g):
    args = dict(locals())
    w = {n: args[n][0] for n in WEIGHTS}
    m = {n: args["m_" + n][0] for n in WEIGHTS}
    v = {n: args["v_" + n][0] for n in WEIGHTS}

    for d in (w, m, v):
        for n in TRANSPOSED:
            d[n] = jnp.swapaxes(d[n], 0, 1)
    wb = {n: w[n].astype(BF16) for n in BIG}
    sp = {n: w[n] for n in SMALL}
    for n in ("ffn1_pre_g", "ffn1_post_g", "mix_pre_g", "mix_post_g", "ffn2_pre_g", "ffn2_post_g", "b_glu", "d_skip"):
        sp[n] = w[n].reshape(1, -1)

    sq, grad_x, big_slots, small_slots = local_step(x[0], loss_target[0], sp, wb)
    loss = lax.psum(0.5 / D_MODEL * jnp.sum(sq), ("x", "y", "c"))

    outs = {}
    for n in BIG:
        shp = w[n].shape
        r2 = lambda t: t.reshape(-1, shp[-1])
        res = adamw(r2(w[n]), r2(m[n]), r2(v[n]), big_slots[n].reshape(N_DEV, -1, shp[-1]), f"adamw_{n}")
        outs[n] = [(jnp.swapaxes(t, 0, 1) if n in TRANSPOSED else t.reshape(shp))[None] for t in res]
    res = adamw(_pack([w[n] for n in SMALL]), _pack([m[n] for n in SMALL]), _pack([v[n] for n in SMALL]),
                small_slots, "adamw_small")
    shapes = [(1,) + w[n].shape for n in SMALL]
    unpacked = [_unpack(t, shapes) for t in res]
    for j, n in enumerate(SMALL):
        outs[n] = [unpacked[k][j] for k in range(4)]

    result = [loss, grad_x[None]]
    for k in range(4):
        result += [outs[n][k] for n in WEIGHTS]
    return tuple(result)
```

```python
import functools
import math

import jax
import jax.numpy as jnp
from jax import lax
from jax.experimental import pallas as pl
from jax.experimental.pallas import tpu as pltpu

F32, BF16 = jnp.float32, jnp.bfloat16
SDS = jax.ShapeDtypeStruct

D_MODEL = 1024
N_DEV = 8
HEAD_DIM = 64
PAIR_W = 128
QBLK = 128
DILATIONS = (1, 4, 16)
N_STATE = 64
HALF_STATES = 1024
NORM_EPS = 1e-6
NEG = -1e30
VMEM_LIMIT = 56 * 1024 * 1024
ADAM_LR, ADAM_B1, ADAM_B2, ADAM_EPS, ADAM_WD, ADAM_STEP = 1e-3, 0.9, 0.999, 1e-8, 0.01, 10
SCAN_TM = 256
SCAN_TM_BWD = 512
SCAN_CW = 512


def _params(n_grid):
    return pltpu.CompilerParams(dimension_semantics=("arbitrary",) * n_grid, vmem_limit_bytes=VMEM_LIMIT)


def _dot(a, b):
    return jnp.dot(a, b, preferred_element_type=F32)


def _dot_nt(a, b):
    return lax.dot_general(a, b, (((1,), (1,)), ((), ())), preferred_element_type=F32)


def _dot_tn(a, b):
    return lax.dot_general(a, b, (((0,), (0,)), ((), ())), preferred_element_type=F32)


def _sigmoid(v):
    return 0.5 * jnp.tanh(0.5 * v) + 0.5


def _resident(shape):
    return pl.BlockSpec(shape, lambda i: (0,) * len(shape), pipeline_mode=pl.Buffered(1))


ROW_SPLIT = 2


def _exchange_phase(ins, outs, scatter, sems, start):
    send_sems, recv_sems, loc_sems = sems
    x, y, c = lax.axis_index("x"), lax.axis_index("y"), lax.axis_index("c")
    me = 4 * x + 2 * y + c
    own_copies, sends, arrivals = [], [], []
    for i in range(len(ins)):
        own = ins[i].at[me] if scatter[i] else ins[i]
        own_copies.append(pltpu.make_async_copy(own, outs[i].at[me], loc_sems.at[i]))
        for k in range(1, N_DEV):
            px = 1 - x if k & 4 else x
            py = 1 - y if k & 2 else y
            pc = 1 - c if k & 1 else c
            peer = 4 * px + 2 * py + pc
            src = ins[i].at[peer] if scatter[i] else ins[i]
            common = dict(src_ref=src, send_sem=send_sems.at[i, k - 1], recv_sem=recv_sems.at[i, k - 1],
                          device_id=(px, py, pc), device_id_type=pl.DeviceIdType.MESH)
            sends.append(pltpu.make_async_remote_copy(dst_ref=outs[i].at[me], **common))
            if not start:
                arrivals.append(pltpu.make_async_remote_copy(dst_ref=outs[i].at[peer], **common))
    if start:
        for cp in own_copies + sends:
            cp.start()
    else:
        for cp in arrivals:
            cp.wait_recv()
        for cp in sends:
            cp.wait_send()
        for cp in own_copies:
            cp.wait()


def _comm_shapes(arrs, scatter):
    n = len(arrs)
    out_shapes = [SDS(a.shape if scatter[i] else (N_DEV,) + a.shape, a.dtype) for i, a in enumerate(arrs)]
    sems = [pltpu.SemaphoreType.DMA((n, N_DEV - 1)), pltpu.SemaphoreType.DMA((n, N_DEV - 1)),
            pltpu.SemaphoreType.DMA((n,))]
    return out_shapes, sems


def gather_two_level(arr, name):
    def body(x_ref, out_ref, send_sems, recv_sems, local_sem):
        x, y, c = lax.axis_index("x"), lax.axis_index("y"), lax.axis_index("c")
        sibling = (x, y, 1 - c)
        chips = [(1 - x, y), (x, 1 - y), (1 - x, 1 - y)]

        def slot(px, py, pc):
            return out_ref.at[4 * px + 2 * py + pc]

        def copy(k, block, to, src=None):
            return pltpu.make_async_remote_copy(
                src_ref=slot(*block) if src is None else src, dst_ref=slot(*block),
                send_sem=send_sems.at[k], recv_sem=recv_sems.at[k], device_id=to, device_id_type=pl.DeviceIdType.MESH)

        mine = pltpu.make_async_copy(x_ref, slot(x, y, c), local_sem)
        mine.start()
        first = [copy(0, (x, y, c), sibling, src=x_ref)]
        first += [copy(1 + j, (x, y, c), (*chip, c), src=x_ref) for j, chip in enumerate(chips)]
        for cp in first:
            cp.start()
        passed = [copy(4 + j, (*chip, c), sibling) for j, chip in enumerate(chips)]
        for j, chip in enumerate(chips):
            copy(1 + j, (*chip, c), (x, y, c)).wait_recv()
            passed[j].start()
        copy(0, sibling, (x, y, c)).wait_recv()
        for j, chip in enumerate(chips):
            copy(4 + j, (*chip, 1 - c), (x, y, c)).wait_recv()
        for cp in first + passed:
            cp.wait_send()
        mine.wait()

    anyspec = pl.BlockSpec(memory_space=pl.ANY)
    return pl.pallas_call(
        body, in_specs=[anyspec], out_specs=anyspec, out_shape=SDS((N_DEV,) + arr.shape, arr.dtype),
        scratch_shapes=[pltpu.SemaphoreType.DMA((N_DEV - 1,)), pltpu.SemaphoreType.DMA((N_DEV - 1,)),
                        pltpu.SemaphoreType.DMA],
        compiler_params=pltpu.CompilerParams(has_side_effects=True), name=name)(arr)


def _call(body, *, grid, in_specs, out_specs, out_shape, args, name, scratch_shapes=(), comm=None):
    n_grid, scratch_shapes = len(grid), list(scratch_shapes)
    if comm is None:
        outs = pl.pallas_call(body, grid=grid, in_specs=in_specs, out_specs=out_specs, out_shape=out_shape,
                              scratch_shapes=scratch_shapes, compiler_params=_params(n_grid), name=name)(*args)
        return outs, []
    arrs, scatter = comm
    nc, n_in, n_out, n_sc = len(arrs), len(in_specs), len(out_specs), len(scratch_shapes)
    comm_shapes, sems = _comm_shapes(arrs, scatter)

    def wrapped(*refs):
        ins, cins = refs[:n_in], refs[n_in:n_in + nc]
        o0 = n_in + nc
        outs, couts = refs[o0:o0 + n_out], refs[o0 + n_out:o0 + n_out + nc]
        s0 = o0 + n_out + nc
        scratch, sem_refs = refs[s0:s0 + n_sc], refs[s0 + n_sc:]
        first = functools.reduce(jnp.logical_and, [pl.program_id(k) == 0 for k in range(n_grid)])
        last = functools.reduce(jnp.logical_and, [pl.program_id(k) == grid[k] - 1 for k in range(n_grid)])

        @pl.when(first)
        def _():
            _exchange_phase(cins, couts, scatter, sem_refs, True)

        body(*ins, *outs, *scratch)

        @pl.when(last)
        def _():
            _exchange_phase(cins, couts, scatter, sem_refs, False)

    anyspec = pl.BlockSpec(memory_space=pl.ANY)
    res = pl.pallas_call(
        wrapped, grid=grid, in_specs=list(in_specs) + [anyspec] * nc, out_specs=list(out_specs) + [anyspec] * nc,
        out_shape=list(out_shape) + comm_shapes, scratch_shapes=scratch_shapes + sems,
        compiler_params=pltpu.CompilerParams(dimension_semantics=("arbitrary",) * n_grid,
                                             vmem_limit_bytes=VMEM_LIMIT, has_side_effects=True),
        name=name)(*args, *arrs)
    return res[:n_out], res[n_out:]


def _rms(xv, g):
    r = lax.rsqrt(jnp.mean(xv * xv, axis=-1, keepdims=True) + NORM_EPS)
    return (xv * r * g).astype(BF16)


def ffn_in(x, g, w, name, comm=None):
    T, D = x.shape
    F = w.shape[1]
    tm = 512

    def body(x_ref, g_ref, w_ref, h_ref, z_ref, a_ref):
        hv = _rms(x_ref[...], g_ref[...])
        h_ref[...] = hv
        pending = None
        for j in range(5):
            if j < 4:
                zs = (_dot_nt(hv, w_ref[j]), _dot_nt(hv, w_ref[j + 4]))
            if pending is not None:
                zg, zu = pending
                sg = _sigmoid(zg)
                silu = zg * sg
                z_ref[0, j - 1] = (zu * (sg + silu - silu * sg)).astype(BF16)
                z_ref[1, j - 1] = silu.astype(BF16)
                a_ref[j - 1] = (silu * zu).astype(BF16)
            pending = zs

    return _call(
        body, grid=(T // tm,),
        in_specs=[pl.BlockSpec((tm, D), lambda i: (i, 0)), pl.BlockSpec((1, D), lambda i: (0, 0)),
                  _resident((8, F, D))],
        out_specs=[pl.BlockSpec((tm, D), lambda i: (i, 0)), pl.BlockSpec((2, 4, tm, F), lambda i: (0, 0, i, 0)),
                   pl.BlockSpec((4, tm, F), lambda i: (0, i, 0))],
        out_shape=[SDS((T, D), BF16), SDS((2, 4, T, F), BF16), SDS((4, T, F), BF16)],
        args=(x, g, w), name=name, comm=comm)


def norm_proj(x, g, w, name):
    T, K = x.shape
    nb, _, N = w.shape
    tm = 512

    def body(x_ref, g_ref, w_ref, h_ref, o_ref):
        hv = _rms(x_ref[...], g_ref[...])
        h_ref[...] = hv
        for b in range(nb):
            o_ref[b] = _dot(hv, w_ref[b])

    return pl.pallas_call(
        body, grid=(T // tm,),
        in_specs=[pl.BlockSpec((tm, K), lambda i: (i, 0)), pl.BlockSpec((1, K), lambda i: (0, 0)),
                  _resident((nb, K, N))],
        out_specs=[pl.BlockSpec((tm, K), lambda i: (i, 0)), pl.BlockSpec((nb, tm, N), lambda i: (0, i, 0))],
        out_shape=[SDS((T, K), BF16), SDS((nb, T, N), F32)], compiler_params=_params(1), name=name)(x, g, w)


def mm_acc_norm(a, w, xres, g, scale, name, comm=None, tgt=None):
    nb, T, K = a.shape
    D = w.shape[2]
    tm = 512
    rc = tm // ROW_SPLIT
    with_loss = tgt is not None

    def body(a_ref, w_ref, x_ref, g_ref, *rest):
        if with_loss:
            t_ref, dy_ref, sq_ref, do_ref, dg_ref = rest

            @pl.when(pl.program_id(0) == 0)
            def _():
                sq_ref[...] = jnp.zeros_like(sq_ref)
                dg_ref[...] = jnp.zeros_like(dg_ref)
        else:
            o_ref, y_ref = rest
        accs = []
        for c in range(ROW_SPLIT):
            rows = pl.ds(c * rc, rc)
            o = _dot(a_ref[0, rows, :].astype(BF16), w_ref[0])
            for b in range(1, nb):
                o += _dot(a_ref[b, rows, :].astype(BF16), w_ref[b])
            accs.append(o)
        for c, o in enumerate(accs):
            rows = pl.ds(c * rc, rc)
            r = lax.rsqrt(jnp.mean(o * o, axis=-1, keepdims=True) + NORM_EPS)
            y = x_ref[rows, :] + scale * (o * r * g_ref[...])
            if with_loss:
                e = y - t_ref[rows, :]
                dy = e * (1.0 / D)
                dy_ref[rows, :] = dy
                sq_ref[...] += jnp.sum(e * e, axis=0, keepdims=True)
                do, dg = _post_bwd(dy, o, g_ref[...], scale)
                do_ref[rows, :] = do
                dg_ref[...] += dg
            else:
                o_ref[rows, :] = o
                y_ref[rows, :] = y

    tile = pl.BlockSpec((tm, D), lambda i: (i, 0))
    row = pl.BlockSpec((1, D), lambda i: (0, 0))
    in_specs = [pl.BlockSpec((nb, tm, K), lambda i: (0, i, 0)), _resident((nb, K, D)), tile, row]
    args = (a, w, xres, g)
    if with_loss:
        return _call(body, grid=(T // tm,), in_specs=in_specs + [tile], out_specs=[tile, row, tile, row],
                     out_shape=[SDS((T, D), F32), SDS((1, D), F32), SDS((T, D), BF16), SDS((1, D), F32)],
                     args=args + (tgt,), name=name, comm=comm)
    return _call(body, grid=(T // tm,), in_specs=in_specs, out_specs=[tile, tile],
                 out_shape=[SDS((T, D), F32), SDS((T, D), F32)], args=args, name=name, comm=comm)


def _post_bwd(dy, ov, g, scale):
    r = scale * dy
    rstd = lax.rsqrt(jnp.mean(ov * ov, axis=-1, keepdims=True) + NORM_EPS)
    oh = ov * rstd
    rg = r * g
    do = rstd * (rg - oh * jnp.mean(rg * oh, axis=-1, keepdims=True))
    return do.astype(BF16), jnp.sum(r * oh, axis=0, keepdims=True)


def mm_nt_b(gr, w, name):
    T, N = gr.shape
    nb, K, _ = w.shape
    tm = 512

    def body(g_ref, w_ref, o_ref):
        gv = g_ref[...]
        for b in range(nb):
            o_ref[b] = _dot_nt(gv, w_ref[b])

    return pl.pallas_call(
        body, grid=(T // tm,),
        in_specs=[pl.BlockSpec((tm, N), lambda i: (i, 0)), _resident((nb, K, N))],
        out_specs=pl.BlockSpec((nb, tm, K), lambda i: (0, i, 0)),
        out_shape=SDS((nb, T, K), F32), compiler_params=_params(1), name=name)(gr, w)


def ffn_dact(do, w_out, z, name):
    T, D = do.shape
    nb, F, _ = w_out.shape
    tm = 512

    def body(g_ref, w_ref, z_ref, dz_ref):
        gv = g_ref[...]
        pending = None
        for b in range(nb + 1):
            da = _dot_nt(gv, w_ref[b]) if b < nb else None
            if pending is not None:
                dz_ref[0, b - 1] = (pending * z_ref[0, b - 1].astype(F32)).astype(BF16)
                dz_ref[1, b - 1] = (pending * z_ref[1, b - 1].astype(F32)).astype(BF16)
            pending = da

    blk = pl.BlockSpec((2, nb, tm, F), lambda i: (0, 0, i, 0))
    return pl.pallas_call(
        body, grid=(T // tm,),
        in_specs=[pl.BlockSpec((tm, D), lambda i: (i, 0)), _resident((nb, F, D)), blk],
        out_specs=blk, out_shape=SDS((2, nb, T, F), BF16), compiler_params=_params(1), name=name)(do, w_out, z)


def mm_tn(a, g, a_batched, g_batched, nb, name, comm=None):
    T = a.shape[-2]
    K, N = a.shape[-1], g.shape[-1]
    tk = 2048
    nk = T // tk

    def body(a_ref, g_ref, o_ref, acc):
        k = pl.program_id(1)

        @pl.when(k == 0)
        def _():
            acc[...] = jnp.zeros_like(acc)

        acc[...] += _dot_tn(a_ref[...].astype(BF16), g_ref[...].astype(BF16))

        @pl.when(k == nk - 1)
        def _():
            o_ref[...] = acc[...].astype(BF16)

    a_spec = (pl.BlockSpec((None, tk, K), lambda b, k: (b, k, 0)) if a_batched
              else pl.BlockSpec((tk, K), lambda b, k: (k, 0)))
    g_spec = (pl.BlockSpec((None, tk, N), lambda b, k: (b, k, 0)) if g_batched
              else pl.BlockSpec((tk, N), lambda b, k: (k, 0)))
    (out,), slots = _call(
        body, grid=(nb, nk), in_specs=[a_spec, g_spec],
        out_specs=[pl.BlockSpec((None, K, N), lambda b, k: (b, 0, 0))],
        out_shape=[SDS((nb, K, N), BF16)], scratch_shapes=[pltpu.VMEM((K, N), F32)],
        args=(a, g), name=name, comm=comm)
    return out, slots


def mm_tn_shared_a(a, g, name):
    T, K = a.shape
    nb, _, N = g.shape
    tk = 1024
    nk = T // tk

    def body(a_ref, g_ref, o_ref, acc):
        k = pl.program_id(0)

        @pl.when(k == 0)
        def _():
            acc[...] = jnp.zeros_like(acc)

        av = a_ref[...]
        for b in range(nb):
            acc[b] += _dot_tn(av, g_ref[b].astype(BF16))

        @pl.when(k == nk - 1)
        def _():
            o_ref[...] = acc[...].astype(BF16)

    return pl.pallas_call(
        body, grid=(nk,),
        in_specs=[pl.BlockSpec((tk, K), lambda k: (k, 0)), pl.BlockSpec((nb, tk, N), lambda k: (0, k, 0))],
        out_specs=_resident((nb, K, N)), out_shape=SDS((nb, K, N), BF16),
        scratch_shapes=[pltpu.VMEM((nb, K, N), F32)], compiler_params=_params(1), name=name)(a, g)


def dh_pre_bwd(dz, w, x, g, dyres, name, comm=None, post=None, w_transposed=False):
    nb, T, F = dz.shape
    D = x.shape[1]
    tm = 512
    rc = tm // ROW_SPLIT
    mm = _dot if w_transposed else _dot_nt

    def body(dz_ref, w_ref, x_ref, g_ref, dy_ref, *rest):
        if post is None:
            dx_ref, dg_ref = rest
        else:
            o_ref, gp_ref, dx_ref, dg_ref, do_ref, dgp_ref = rest

        @pl.when(pl.program_id(0) == 0)
        def _():
            dg_ref[...] = jnp.zeros_like(dg_ref)
            if post is not None:
                dgp_ref[...] = jnp.zeros_like(dgp_ref)

        accs = []
        for c in range(ROW_SPLIT):
            rows = pl.ds(c * rc, rc)
            dh = mm(dz_ref[0, rows, :].astype(BF16), w_ref[0])
            for b in range(1, nb):
                dh += mm(dz_ref[b, rows, :].astype(BF16), w_ref[b])
            accs.append(dh)
        for c, dh in enumerate(accs):
            rows = pl.ds(c * rc, rc)
            xv = x_ref[rows, :]
            rstd = lax.rsqrt(jnp.mean(xv * xv, axis=-1, keepdims=True) + NORM_EPS)
            xh = xv * rstd
            dg_ref[...] += jnp.sum(dh * xh, axis=0, keepdims=True)
            dhg = dh * g_ref[...]
            dx = dy_ref[rows, :] + rstd * (dhg - xh * jnp.mean(dhg * xh, axis=-1, keepdims=True))
            dx_ref[rows, :] = dx
            if post is not None:
                do, dgp = _post_bwd(dx, o_ref[rows, :], gp_ref[...], post[2])
                do_ref[rows, :] = do
                dgp_ref[...] += dgp

    tile = pl.BlockSpec((tm, D), lambda i: (i, 0))
    row = pl.BlockSpec((1, D), lambda i: (0, 0))
    in_specs = [pl.BlockSpec((nb, tm, F), lambda i: (0, i, 0)), _resident(w.shape), tile, row, tile]
    out_specs, out_shape, args = [tile, row], [SDS((T, D), F32), SDS((1, D), F32)], (dz, w, x, g, dyres)
    if post is not None:
        in_specs += [tile, row]
        out_specs += [tile, row]
        out_shape += [SDS((T, D), BF16), SDS((1, D), F32)]
        args += (post[0], post[1])
    return _call(body, grid=(T // tm,), in_specs=in_specs, out_specs=out_specs, out_shape=out_shape,
                 args=args, name=name, comm=comm)


ATTN_GROUP = {1: 8, 4: 2, 16: 1}
ATTN_GROUP_BWD = {1: 16, 4: 4, 16: 1}
ATTN_UNROLL = 4


def _attn_masks():
    qi = lax.broadcasted_iota(jnp.int32, (QBLK, QBLK), 0)
    kj = lax.broadcasted_iota(jnp.int32, (QBLK, QBLK), 1)
    cur_ok = kj <= qi
    prev_ok = kj >= qi
    dcur = (qi - kj).astype(F32)
    return cur_ok, prev_ok, dcur, dcur + float(QBLK)


def _head_slopes(p, d):
    out = []
    for hq in range(2):
        v = [float(d) * 2.0 ** -(2 * q + hq + 1) for q in range(4)]
        out.append(jnp.where(p == 0, v[0], jnp.where(p == 1, v[1], jnp.where(p == 2, v[2], v[3]))))
    return out


def _rows(start, d):
    return pl.ds(start, QBLK, stride=d) if d > 1 else pl.ds(start, QBLK)


def _pair_spec(rows, part, blk):
    return pl.BlockSpec((None, rows, PAIR_W), lambda p, n: (2 * part + p // 2, blk(n), p % 2))


def _for_query_blocks(d, groups, several):
    blocks = [(g, r) for r in range(d) for g in range(groups)]
    for s in range(0, len(blocks), ATTN_UNROLL):
        several(blocks[s:s + ATTN_UNROLL])


def attn_fwd(proj, d, name, comm=None):
    T = proj.shape[1]
    sb, groups = QBLK * d, ATTN_GROUP[d]
    rb = sb * groups
    nblk = T // rb

    def body(q_ref, kc_ref, kp_ref, vc_ref, vp_ref, o_ref, l_ref):
        p, n = pl.program_id(0), pl.program_id(1)
        cur_ok, prev_ok, dcur, dprev = _attn_masks()
        first_ok = jnp.logical_and(prev_ok, n > 0)
        lane_head = lax.broadcasted_iota(jnp.int32, (QBLK, PAIR_W), 1) // HEAD_DIM
        slopes = _head_slopes(p, d)

        def several(blocks):
            work = []
            for g, r in blocks:
                rows = _rows(g * sb + r, d)
                q = q_ref[rows, :]
                kc, vc = kc_ref[rows, :].astype(BF16), vc_ref[rows, :].astype(BF16)
                if g == 0:
                    prow, pok = _rows(r, d), first_ok
                    kp, vp = kp_ref[prow, :].astype(BF16), vp_ref[prow, :].astype(BF16)
                else:
                    prow, pok = _rows((g - 1) * sb + r, d), prev_ok
                    kp, vp = kc_ref[prow, :].astype(BF16), vc_ref[prow, :].astype(BF16)
                for hq in range(2):
                    qm = jnp.where(lane_head == hq, q, 0.0).astype(BF16)
                    work.append([rows, hq, pok, vc, vp, _dot_nt(qm, kc), _dot_nt(qm, kp)])
            for w in work:
                _, hq, pok, _, _, sc, sp = w
                sc = jnp.where(cur_ok, sc * 0.125 - slopes[hq] * dcur, NEG)
                sp = jnp.where(pok, sp * 0.125 - slopes[hq] * dprev, NEG)
                m = jnp.maximum(jnp.max(sc, axis=1, keepdims=True), jnp.max(sp, axis=1, keepdims=True))
                pc = jnp.exp(sc - m)
                pp = jnp.exp(sp - m)
                den = jnp.sum(pc, axis=1, keepdims=True) + jnp.sum(pp, axis=1, keepdims=True)
                w[5:] = [pc.astype(BF16), pp.astype(BF16), 1.0 / den, m + jnp.log(den)]
            for i in range(0, len(work), 2):
                o_acc = jnp.zeros((QBLK, PAIR_W), F32)
                l_acc = jnp.zeros((QBLK, PAIR_W), F32)
                for rows, hq, _, vc, vp, pc, pp, inv, lse in work[i:i + 2]:
                    hm = lane_head == hq
                    o_acc = jnp.where(hm, (_dot(pc, vc) + _dot(pp, vp)) * inv, o_acc)
                    l_acc = jnp.where(hm, lse, l_acc)
                o_ref[rows, :] = o_acc
                l_ref[rows, :] = l_acc

        _for_query_blocks(d, groups, several)

    cur = lambda part: _pair_spec(rb, part, lambda n: n)
    prv = lambda part: _pair_spec(sb, part, lambda n: jnp.maximum(n * groups - 1, 0))
    return _call(
        body, grid=(4, nblk), in_specs=[cur(0), cur(1), prv(1), cur(2), prv(2)], out_specs=[cur(0), cur(0)],
        out_shape=[SDS((2, T, 2 * PAIR_W), F32), SDS((2, T, 2 * PAIR_W), F32)],
        args=(proj, proj, proj, proj, proj), name=name, comm=comm)


def mix_out(os_, ls_, y_ssm, w_glu, b_glu, w, xres, g, name):
    _, T, HW = y_ssm.shape
    D = w.shape[2]
    tm = 512

    def body(o1, o2, o3, l1, l2, l3, s_ref, wg_ref, bg_ref, w_ref, x_ref, g_ref, cat_ref, l_ref, lg_ref, m_ref, y_ref):
        y0, _ = _gelu(s_ref[0])
        y1, _ = _gelu(s_ref[1])
        lg = _dot(y0.astype(BF16), wg_ref[0]) + _dot(y1.astype(BF16), wg_ref[1]) + bg_ref[...]
        a, b, c = l1[...], l2[...], l3[...]
        m = jnp.maximum(jnp.maximum(a, b), c)
        ea, eb, ec = jnp.exp(a - m), jnp.exp(b - m), jnp.exp(c - m)
        s = ea + eb + ec
        att = (ea * o1[...] + eb * o2[...] + ec * o3[...]) * (1.0 / s)
        sg = _sigmoid(lg)
        ssm0, ssm1 = y0 * sg[:, :HW], y1 * sg[:, HW:]
        cat_ref[pl.ds(0, 2)] = att
        cat_ref[2] = ssm0
        cat_ref[3] = ssm1
        l_ref[...] = m + jnp.log(s)
        lg_ref[0] = lg[:, :HW]
        lg_ref[1] = lg[:, HW:]
        o = _dot(att[0].astype(BF16), w_ref[0]) + _dot(att[1].astype(BF16), w_ref[1])
        o += _dot(ssm0.astype(BF16), w_ref[2]) + _dot(ssm1.astype(BF16), w_ref[3])
        r = lax.rsqrt(jnp.mean(o * o, axis=-1, keepdims=True) + NORM_EPS)
        m_ref[...] = o
        y_ref[...] = x_ref[...] + o * r * g_ref[...]

    spec = pl.BlockSpec((2, tm, HW), lambda i: (0, i, 0))
    tile = pl.BlockSpec((tm, D), lambda i: (i, 0))
    return pl.pallas_call(
        body, grid=(T // tm,),
        in_specs=[spec] * 7 + [_resident(w_glu.shape), _resident(b_glu.shape), _resident(w.shape), tile,
                               pl.BlockSpec((1, D), lambda i: (0, 0))],
        out_specs=[pl.BlockSpec((4, tm, HW), lambda i: (0, i, 0)), spec, spec, tile, tile],
        out_shape=[SDS((4, T, HW), F32), SDS((2, T, HW), F32), SDS((2, T, HW), F32), SDS((T, D), F32),
                   SDS((T, D), F32)],
        compiler_params=_params(1), name=name)(*os_, *ls_, y_ssm, w_glu, b_glu, w, xres, g)


def attn_bwd(proj, dcat, o, lse, acc, d, name, du=None):
    T = proj.shape[1]
    sb, groups = QBLK * d, ATTN_GROUP_BWD[d]
    rb = sb * groups
    nblk = T // rb
    has_acc = acc is not None
    n_parts = 3 if du is None else 4

    def body(*refs):
        (qc_ref, qn_ref, kc_ref, kp_ref, vc_ref, vp_ref, dc_ref, dn_ref, oc_ref, on_ref, lc_ref, ln_ref) = refs[:12]
        acc_ref = refs[12] if has_acc else None
        out_ref = refs[-1]
        if du is not None:
            out_ref[3] = refs[-2][...]
        p, n = pl.program_id(0), pl.program_id(1)
        cur_ok, prev_ok, dcur, dprev = _attn_masks()
        first_ok = jnp.logical_and(prev_ok, n > 0)
        last_ok = jnp.logical_and(prev_ok, n < nblk - 1)
        lane_head = lax.broadcasted_iota(jnp.int32, (QBLK, PAIR_W), 1) // HEAD_DIM
        slopes = _head_slopes(p, d)

        def one(g, r, shared):
            rows = _rows(g * sb + r, d)
            q_c, do_c, o_c, l_c = qc_ref[rows, :], dc_ref[rows, :], oc_ref[rows, :], lc_ref[rows, :]
            k_c, v_c = kc_ref[rows, :].astype(BF16), vc_ref[rows, :].astype(BF16)
            if shared:
                pok_c, k_p, v_p = prev_ok, None, None
            elif g == 0:
                prow, pok_c = _rows(r, d), first_ok
                k_p, v_p = kp_ref[prow, :].astype(BF16), vp_ref[prow, :].astype(BF16)
            else:
                prow, pok_c = _rows((g - 1) * sb + r, d), prev_ok
                k_p, v_p = kc_ref[prow, :].astype(BF16), vc_ref[prow, :].astype(BF16)
            if g == groups - 1:
                nrow, pok_n = _rows(r, d), last_ok
                q_n, do_n, o_n, l_n = qn_ref[nrow, :], dn_ref[nrow, :], on_ref[nrow, :], ln_ref[nrow, :]
            else:
                nrow, pok_n = _rows((g + 1) * sb + r, d), prev_ok
                q_n, do_n, o_n, l_n = qc_ref[nrow, :], dc_ref[nrow, :], oc_ref[nrow, :], lc_ref[nrow, :]
            heads = []
            for hq in range(2):
                hm = lane_head == hq
                qm_c = jnp.where(hm, q_c, 0.0).astype(BF16)
                qm_n = jnp.where(hm, q_n, 0.0).astype(BF16)
                dom_c = jnp.where(hm, do_c, 0.0)
                dom_n = jnp.where(hm, do_n, 0.0)
                dd_c = jnp.sum(dom_c * o_c, axis=1, keepdims=True)
                dd_n = jnp.sum(dom_n * o_n, axis=1, keepdims=True)
                ls_c = jnp.max(jnp.where(hm, l_c, NEG), axis=1, keepdims=True)
                ls_n = jnp.max(jnp.where(hm, l_n, NEG), axis=1, keepdims=True)
                dob_c, dob_n = dom_c.astype(BF16), dom_n.astype(BF16)
                mm = [(_dot_nt(qm_c, k_c), _dot_nt(dob_c, v_c)),
                      None if shared else (_dot_nt(qm_c, k_p), _dot_nt(dob_c, v_p)),
                      (_dot_nt(qm_n, k_c), _dot_nt(dob_n, v_c))]
                heads.append(dict(hq=hq, qm_c=qm_c, qm_n=qm_n, dob_c=dob_c, dob_n=dob_n, mm=mm,
                                  dd=(dd_c, dd_c, dd_n), ls=(ls_c, ls_c, ls_n)))
            return dict(rows=rows, k_c=k_c, k_p=k_p, heads=heads, oks=(cur_ok, pok_c, pok_n), shared=shared)

        def several(blocks):
            work = []
            for i, (g, r) in enumerate(blocks):
                work.append(one(g, r, i > 0 and blocks[i - 1] == (g - 1, r)))
            for i, w in enumerate(work):
                if w["shared"]:
                    w["k_p"] = work[i - 1]["k_c"]
                for hi, h in enumerate(w["heads"]):
                    slope, dist = slopes[h["hq"]], (dcur, dprev, dprev)
                    h["pr"], h["ds"] = [], []
                    for j in range(3):
                        if h["mm"][j] is None:
                            h["pr"].append(work[i - 1]["heads"][hi]["pr"][2])
                            h["ds"].append(work[i - 1]["heads"][hi]["ds"][2])
                            continue
                        s = jnp.where(w["oks"][j], h["mm"][j][0] * 0.125 - slope * dist[j], NEG)
                        pr = jnp.exp(s - h["ls"][j])
                        h["pr"].append(pr.astype(BF16))
                        h["ds"].append((pr * (h["mm"][j][1] - h["dd"][j])).astype(BF16))
            for w in work:
                dq = jnp.zeros((QBLK, PAIR_W), F32)
                dk = jnp.zeros((QBLK, PAIR_W), F32)
                dv = jnp.zeros((QBLK, PAIR_W), F32)
                for h in w["heads"]:
                    ds, pr = h["ds"], h["pr"]
                    dq_h = _dot(ds[0], w["k_c"]) + _dot(ds[1], w["k_p"])
                    dk += (_dot_tn(ds[0], h["qm_c"]) + _dot_tn(ds[2], h["qm_n"])) * 0.125
                    dv += _dot_tn(pr[0], h["dob_c"]) + _dot_tn(pr[2], h["dob_n"])
                    dq = jnp.where(lane_head == h["hq"], dq_h * 0.125, dq)
                for part, val in enumerate((dq, dk, dv)):
                    if has_acc:
                        val = val + acc_ref.at[part][w["rows"], :]
                    out_ref.at[part][w["rows"], :] = val

        _for_query_blocks(d, groups, several)

    cur = lambda part: _pair_spec(rb, part, lambda n: n)
    prv = lambda part: _pair_spec(sb, part, lambda n: jnp.maximum(n * groups - 1, 0))
    nxt = lambda part: _pair_spec(sb, part, lambda n: jnp.minimum((n + 1) * groups, T // sb - 1))
    full = pl.BlockSpec((3, None, rb, PAIR_W), lambda p, n: (0, p // 2, n, p % 2))
    in_specs = [cur(0), nxt(0), cur(1), prv(1), cur(2), prv(2), cur(0), nxt(0), cur(0), nxt(0), cur(0), nxt(0)]
    args = [proj, proj, proj, proj, proj, proj, dcat, dcat, o, o, lse, lse]
    if has_acc:
        in_specs.append(full)
        args.append(acc)
    if du is not None:
        in_specs.append(cur(0))
        args.append(du)
    out_spec = pl.BlockSpec((n_parts, None, rb, PAIR_W), lambda p, n: (0, p // 2, n, p % 2))
    return pl.pallas_call(
        body, grid=(4, nblk), in_specs=in_specs, out_specs=out_spec,
        out_shape=SDS((n_parts, 2, T, 2 * PAIR_W), F32), compiler_params=_params(2), name=name)(*args)


def _scan_rows(buf, tab_ref, reverse, half):
    n_tiles = (buf.shape[0] - 8) // 8
    per_half = HALF_STATES // SCAN_CW
    row = lax.broadcasted_iota(jnp.int32, (8, SCAN_CW), 0)
    sgn = -1.0 if reverse else 1.0

    for j in range(per_half):
        c0 = half * 2 * HALF_STATES + j * SCAN_CW
        cre = pl.ds(c0, SCAN_CW)
        cim = pl.ds(c0 + HALF_STATES, SCAN_CW)
        steps = []
        for s, k in enumerate((1, 2, 4)):
            ok, shift = (row < 8 - k, 8 - k) if reverse else (row >= k, k)
            steps.append((shift, jnp.where(ok, tab_ref[pl.ds(s, 1), cre], 0.0),
                          jnp.where(ok, sgn * tab_ref[pl.ds(s, 1), cim], 0.0)))
        trow = 16 if reverse else 8
        pr, pi = tab_ref[pl.ds(trow, 8), cre], tab_ref[pl.ds(trow, 8), cim]
        for t in range(n_tiles):
            base = 8 * (n_tiles - 1 - t) if reverse else 8 + 8 * t
            rows = pl.ds(base, 8)
            re, im = buf[rows, cre], buf[rows, cim]
            for shift, ar, ai in steps:
                sre, sim = pltpu.roll(re, shift, 0), pltpu.roll(im, shift, 0)
                re, im = re + ar * sre - ai * sim, im + ar * sim + ai * sre
            crow = pl.ds(base + 8 if reverse else base - 1, 1)
            cr, ci = buf[crow, cre], buf[crow, cim]
            buf[rows, cre] = re + pr * cr - pi * ci
            buf[rows, cim] = im + pr * ci + pi * cr


def ssm_fwd(proj, bh, ch, apow, dskip, name, comm=None):
    _, T, C = proj.shape
    tm = SCAN_TM
    SW = 4 * HALF_STATES

    def body(u_ref, bh_ref, ch_ref, tab_ref, dsk_ref, y_ref, s_ref, buf):
        @pl.when(pl.program_id(0) == 0)
        def _():
            buf[pl.ds(0, 8), :] = jnp.zeros((8, SW), F32)

        for h in range(2):
            buf[pl.ds(8, tm), pl.ds(h * 2 * HALF_STATES, 2 * HALF_STATES)] = _dot(u_ref[h].astype(BF16), bh_ref[h])
        for h in range(2):
            cols = pl.ds(h * 2 * HALF_STATES, 2 * HALF_STATES)
            _scan_rows(buf, tab_ref, False, h)
            sv = buf[pl.ds(8, tm), cols]
            s_ref[:, cols] = sv
            y_ref[h] = _dot(sv.astype(BF16), ch_ref[h]) + dsk_ref[h] * u_ref[h]
        buf[pl.ds(0, 8), :] = buf[pl.ds(tm, 8), :]

    return _call(
        body, grid=(T // tm,),
        in_specs=[pl.BlockSpec((2, tm, C), lambda i: (3, i, 0)),
                  pl.BlockSpec((2, C, 2 * HALF_STATES), lambda i: (0, 0, 0)),
                  pl.BlockSpec((2, 2 * HALF_STATES, C), lambda i: (0, 0, 0)),
                  pl.BlockSpec((24, SW), lambda i: (0, 0)),
                  pl.BlockSpec((2, 1, C), lambda i: (0, 0, 0))],
        out_specs=[pl.BlockSpec((2, tm, C), lambda i: (0, i, 0)), pl.BlockSpec((tm, SW), lambda i: (i, 0))],
        out_shape=[SDS((2, T, C), F32), SDS((T, SW), F32)],
        scratch_shapes=[pltpu.VMEM((tm + 8, SW), F32)],
        args=(proj, bh, ch, apow, dskip), name=name, comm=comm)


def ssm_bwd(dy, proj, st, bh, ch, apow, dskip, name, comm=None):
    _, T, C = proj.shape
    tm = SCAN_TM_BWD
    nt = T // tm
    SW = 4 * HALF_STATES
    HS2 = 2 * HALF_STATES

    def body(dy_ref, u_ref, s_ref, sp_ref, bh_ref, ch_ref, tab_ref, dsk_ref,
             du_ref, da_ref, dbh_ref, dch_ref, dd_ref, lam):
        i = pl.program_id(0)

        @pl.when(i == 0)
        def _():
            lam[pl.ds(tm, 8), :] = jnp.zeros((8, SW), F32)
            da_ref[...] = jnp.zeros_like(da_ref)
            dbh_ref[...] = jnp.zeros_like(dbh_ref)
            dch_ref[...] = jnp.zeros_like(dch_ref)
            dd_ref[...] = jnp.zeros_like(dd_ref)

        for h in range(2):
            lam[pl.ds(0, tm), pl.ds(h * HS2, HS2)] = _dot_nt(dy_ref[h].astype(BF16), ch_ref[h])
        for h in range(2):
            dyv, uv = dy_ref[h], u_ref[h]
            dch_ref[h] += _dot_tn(s_ref[:, pl.ds(h * HS2, HS2)].astype(BF16), dyv.astype(BF16))
            dd_ref[h] += jnp.sum(dyv * uv, axis=0, keepdims=True)
        for h in range(2):
            _scan_rows(lam, tab_ref, True, h)
            lb = lam[pl.ds(0, tm), pl.ds(h * HS2, HS2)].astype(BF16)
            du_ref[h] = _dot_nt(lb, bh_ref[h]) + dsk_ref[h] * dy_ref[h]
            dbh_ref[h] += _dot_tn(u_ref[h].astype(BF16), lb)

        first = i == nt - 1
        per_half = HALF_STATES // SCAN_CW

        def chunk(j, _):
            c0 = pl.multiple_of((j // per_half) * HS2 + (j % per_half) * SCAN_CW, 128)
            cre, cim = pl.ds(c0, SCAN_CW), pl.ds(pl.multiple_of(c0 + HALF_STATES, 128), SCAN_CW)
            row0 = lax.broadcasted_iota(jnp.int32, (8, SCAN_CW), 0) == 0
            acc_r = jnp.zeros((8, SCAN_CW), F32)
            acc_i = jnp.zeros((8, SCAN_CW), F32)
            for t in range(tm // 8):
                rows = pl.ds(8 * t, 8)
                if t == 0:
                    pre = jnp.where(first, 0.0, sp_ref[pl.ds(7, 1), cre])
                    pim = jnp.where(first, 0.0, sp_ref[pl.ds(7, 1), cim])
                else:
                    pre, pim = s_ref[pl.ds(8 * t - 1, 1), cre], s_ref[pl.ds(8 * t - 1, 1), cim]
                spr = jnp.where(row0, pre, pltpu.roll(s_ref[rows, cre], 1, 0))
                spi = jnp.where(row0, pim, pltpu.roll(s_ref[rows, cim], 1, 0))
                lr, li = lam[rows, cre], lam[rows, cim]
                acc_r += lr * spr + li * spi
                acc_i += li * spr - lr * spi
            da_ref[:, cre] += jnp.sum(acc_r, axis=0, keepdims=True)
            da_ref[:, cim] += jnp.sum(acc_i, axis=0, keepdims=True)
            return 0

        lax.fori_loop(0, 2 * per_half, chunk, 0)
        lam[pl.ds(tm, 8), :] = lam[pl.ds(0, 8), :]

    rev = lambda i: nt - 1 - i
    return _call(
        body, grid=(nt,),
        in_specs=[pl.BlockSpec((2, tm, C), lambda i: (0, rev(i), 0)),
                  pl.BlockSpec((2, tm, C), lambda i: (3, rev(i), 0)),
                  pl.BlockSpec((tm, SW), lambda i: (rev(i), 0)),
                  pl.BlockSpec((8, SW), lambda i: (jnp.maximum(rev(i) * (tm // 8) - 1, 0), 0)),
                  _resident((2, C, HS2)), _resident((2, HS2, C)), _resident((24, SW)), _resident((2, 1, C))],
        out_specs=[pl.BlockSpec((2, tm, C), lambda i: (0, rev(i), 0)),
                   _resident((1, SW)), _resident((2, C, HS2)), _resident((2, HS2, C)), _resident((2, 1, C))],
        out_shape=[SDS((2, T, C), F32), SDS((1, SW), F32), SDS((2, C, HS2), F32), SDS((2, HS2, C), F32),
                   SDS((2, 1, C), F32)],
        scratch_shapes=[pltpu.VMEM((tm + 8, SW), F32)],
        args=(dy, proj, st, st, bh, ch, apow, dskip), name=name, comm=comm)


_GELU_C = math.sqrt(2.0 / math.pi)


def _gelu(x):
    t = jnp.tanh(_GELU_C * (x + 0.044715 * x * x * x))
    return 0.5 * x * (1.0 + t), t


def glu_bwd(dcat, y, lg, w, name):
    _, T, C = y.shape
    tm = 512

    def body(d_ref, y_ref, lg_ref, w_ref, dy_ref, dw_ref, db_ref):
        @pl.when(pl.program_id(0) == 0)
        def _():
            dw_ref[...] = jnp.zeros_like(dw_ref)
            db_ref[...] = jnp.zeros_like(db_ref)

        y2, th, sg, dlg = [], [], [], []
        for h in range(2):
            yy, tt = _gelu(y_ref[h])
            ss = _sigmoid(lg_ref[h])
            y2.append(yy)
            th.append(tt)
            sg.append(ss)
            dlg.append(d_ref[h] * yy * ss * (1.0 - ss))
        dl = jnp.concatenate(dlg, axis=1)
        dlb = dl.astype(BF16)
        db_ref[...] += jnp.sum(dl, axis=0, keepdims=True)
        for h in range(2):
            dy2 = d_ref[h] * sg[h] + _dot_nt(dlb, w_ref[h])
            yv = y_ref[h]
            dgelu = 0.5 * (1.0 + th[h]) + 0.5 * yv * (1.0 - th[h] * th[h]) * _GELU_C * (1.0 + 3 * 0.044715 * yv * yv)
            dy_ref[h] = dy2 * dgelu
            dw_ref[h] += _dot_tn(y2[h].astype(BF16), dlb)

    return pl.pallas_call(
        body, grid=(T // tm,),
        in_specs=[pl.BlockSpec((2, tm, C), lambda i: (1, i, 0)), pl.BlockSpec((2, tm, C), lambda i: (0, i, 0)),
                  pl.BlockSpec((2, tm, C), lambda i: (0, i, 0)), pl.BlockSpec((2, C, 2 * C), lambda i: (0, 0, 0))],
        out_specs=[pl.BlockSpec((2, tm, C), lambda i: (0, i, 0)), pl.BlockSpec((2, C, 2 * C), lambda i: (0, 0, 0)),
                   pl.BlockSpec((1, 2 * C), lambda i: (0, 0))],
        out_shape=[SDS((2, T, C), F32), SDS((2, C, 2 * C), F32), SDS((1, 2 * C), F32)],
        compiler_params=_params(1), name=name)(dcat, y, lg, w)


def adamw(w, m, v, slots, name):
    R, C = w.shape
    tr = R
    for cand in (512, 256, 128, 64, 32, 16, 8):
        if R % cand == 0 and cand * C * 4 <= 2 * 1024 * 1024:
            tr = cand
            break
    c1 = 1.0 / (1.0 - ADAM_B1 ** ADAM_STEP)
    c2 = 1.0 / (1.0 - ADAM_B2 ** ADAM_STEP)

    def body(w_ref, m_ref, v_ref, s_ref, g_ref, d_ref, nm_ref, nv_ref):
        g = s_ref[0].astype(F32)
        for j in range(1, N_DEV):
            g = g + s_ref[j].astype(F32)
        nm = ADAM_B1 * m_ref[...] + (1.0 - ADAM_B1) * g
        nv = ADAM_B2 * v_ref[...] + (1.0 - ADAM_B2) * (g * g)
        g_ref[...] = g
        nm_ref[...] = nm
        nv_ref[...] = nv
        d_ref[...] = -ADAM_LR * ((nm * c1) / (jnp.sqrt(nv * c2) + ADAM_EPS) + ADAM_WD * w_ref[...])

    spec = pl.BlockSpec((tr, C), lambda i: (i, 0))
    return pl.pallas_call(
        body, grid=(R // tr,),
        in_specs=[spec, spec, spec, pl.BlockSpec((N_DEV, tr, C), lambda i: (0, i, 0))],
        out_specs=[spec] * 4, out_shape=[SDS((R, C), F32)] * 4, compiler_params=_params(1), name=name)(w, m, v, slots)


def _discretise(a_re, a_im, log_dt, b_re, b_im):
    dt = jnp.exp(log_dt)[:, None]
    e = jnp.exp(dt * a_re)
    ar, ai = e * jnp.cos(dt * a_im), e * jnp.sin(dt * a_im)
    den = a_re * a_re + a_im * a_im
    nr, ni = ar - 1.0, ai
    wr = (nr * a_re + ni * a_im) / den
    wi = (ni * a_re - nr * a_im) / den
    bbr = wr[..., None] * b_re - wi[..., None] * b_im
    bbi = wr[..., None] * b_im + wi[..., None] * b_re
    return ar, ai, bbr, bbi


def _block_diag(t):
    eye = jnp.eye(16, dtype=t.dtype).reshape(1, 16, 1, 16, 1)
    r, c = t.shape[1], t.shape[2]
    return (t.reshape(2, 16, r, 1, c) * eye).reshape(2, 16 * r, 16 * c)


def _diag_blocks(m, r, c):
    eye = jnp.eye(16, dtype=m.dtype).reshape(1, 16, 1, 16, 1)
    return jnp.sum(m.reshape(2, 16, r, 16, c) * eye, axis=3).reshape(32, r, c)


def _state_vec(re, im):
    return jnp.stack([re.reshape(2, HALF_STATES), im.reshape(2, HALF_STATES)], axis=1).reshape(-1)


BIG = ("ffn1_w_in", "ffn1_w_out", "w_mix_in", "w_glu", "w_mix_out", "ffn2_w_in", "ffn2_w_out")
WEIGHTS = ("ffn1_pre_g", "ffn1_w_in", "ffn1_w_out", "ffn1_post_g", "mix_pre_g", "w_mix_in", "a_re", "a_im", "log_dt",
           "b_re", "b_im", "c_re", "c_im", "d_skip", "w_glu", "b_glu", "w_mix_out", "mix_post_g", "ffn2_pre_g",
           "ffn2_w_in", "ffn2_w_out", "ffn2_post_g")
SMALL = tuple(n for n in WEIGHTS if n not in BIG)
TRANSPOSED = ("ffn1_w_in", "ffn2_w_in")
PACK_COLS = 1024


def _pack(parts):
    flat = jnp.concatenate([p.reshape(-1) for p in parts])
    rows = -(-flat.shape[0] // (8 * PACK_COLS)) * 8
    return jnp.pad(flat, (0, rows * PACK_COLS - flat.shape[0])).reshape(rows, PACK_COLS)


def _unpack(packed, shapes):
    flat, out, off = packed.reshape(-1), [], 0
    for s in shapes:
        n = math.prod(s)
        out.append(flat[off:off + n].reshape(s))
        off += n
    return out


def _gather(names, wb):
    return [wb[n] for n in names], [False] * len(names)


def _ffn_bwd(dy, do, saved, x, pre_g, w_in, w_out4, tag, post=None, dwout_comm=None):
    h, z, a = saved
    T = x.shape[0]
    dz = ffn_dact(do, w_out4, z, f"{tag}_dact")
    dz8 = dz.reshape(8, T, dz.shape[-1])
    dw_out, extra = mm_tn(a, do, True, False, 4, f"{tag}_dwout", comm=dwout_comm)
    dw_in, (s_out,) = mm_tn(dz8, h, True, False, 8, f"{tag}_dwin", comm=([dw_out.reshape(8, -1, D_MODEL)], [True]))
    outs, (s_in,) = dh_pre_bwd(dz8, w_in, x, pre_g, dy, f"{tag}_dh", comm=([dw_in], [True]), post=post,
                               w_transposed=True)
    return outs, (s_in, s_out), extra


def local_step(x, tgt, sp, wb):
    T = x.shape[0]
    ar, ai, bbr, bbi = _discretise(sp["a_re"], sp["a_im"], sp["log_dt"], sp["b_re"], sp["b_im"])
    powers = [(ar, ai)]
    for _ in range(7):
        pr, pi = powers[-1]
        powers.append((pr * ar - pi * ai, pr * ai + pi * ar))
    zero = jnp.zeros_like(ar)
    rows = [_state_vec(*powers[k - 1]) for k in (1, 2, 4)] + [_state_vec(zero, zero)] * 5
    rows += [_state_vec(pr, pi) for pr, pi in powers]
    rows += [_state_vec(pr, -pi) for pr, pi in reversed(powers)]
    apow = jnp.stack(rows)
    bh = jnp.concatenate([_block_diag(bbr.transpose(0, 2, 1)), _block_diag(bbi.transpose(0, 2, 1))], axis=2)
    ch = jnp.concatenate([_block_diag(sp["c_re"].transpose(0, 2, 1)), _block_diag(-sp["c_im"].transpose(0, 2, 1))], axis=1)
    bh, ch = bh.astype(BF16), ch.astype(BF16)
    dskip = sp["d_skip"].reshape(2, 1, 256)

    w1_in = gather_two_level(wb["ffn1_w_in"], "gather_w1in")
    (h1, z1, a1), (w1_out, w_mi) = ffn_in(
        x, sp["ffn1_pre_g"], w1_in, "ffn1_in", comm=_gather(["ffn1_w_out", "w_mix_in"], wb))
    w1_out4 = w1_out.reshape(4, -1, D_MODEL)
    (o1, x1), (w_glu, w_mo) = mm_acc_norm(
        a1, w1_out4, x, sp["ffn1_post_g"], 0.5, "ffn1_out", comm=_gather(["w_glu", "w_mix_out"], wb))
    w_glu2, w_mo4 = w_glu.reshape(2, 256, 512), w_mo.reshape(4, 256, D_MODEL)
    h2, proj = norm_proj(x1, sp["mix_pre_g"], w_mi, "mix_proj")
    (y_ssm, states), (w2_in,) = ssm_fwd(proj, bh, ch, apow, dskip, "ssm_fwd", comm=_gather(["ffn2_w_in"], wb))
    os_, ls_ = [], []
    for d in DILATIONS:
        (o_d, l_d), got = attn_fwd(proj, d, f"attn_fwd_d{d}",
                                   comm=_gather(["ffn2_w_out"], wb) if d == DILATIONS[-1] else None)
        os_.append(o_d)
        ls_.append(l_d)
    w2_out4 = got[0].reshape(4, -1, D_MODEL)
    cat, lse, lg, mixed, x2 = mix_out(os_, ls_, y_ssm, w_glu2, sp["b_glu"], w_mo4, x1, sp["mix_post_g"], "mix_out")
    (h3, z3, a3), _ = ffn_in(x2, sp["ffn2_pre_g"], w2_in, "ffn2_in")
    (dy3, sq, do3, dg_f2post), _ = mm_acc_norm(a3, w2_out4, x2, sp["ffn2_post_g"], 0.5, "ffn2_out", tgt=tgt)

    (dx2, dg_f2pre, dmixed, dg_mpost), (s_w2in, s_w2out), _ = _ffn_bwd(
        dy3, do3, (h3, z3, a3), x2, sp["ffn2_pre_g"], w2_in, w2_out4, "ffn2", post=(mixed, sp["mix_post_g"], 1.0))
    dcat = mm_nt_b(dmixed, w_mo4, "mix_dcat")
    dw_mo, _ = mm_tn(cat, dmixed, True, False, 4, "mix_dwout")
    dy_ssm, dw_glu, db_glu = glu_bwd(dcat, y_ssm, lg, w_glu2, "glu_bwd")
    (du, da, dbh, dch, dd), (s_wmo, s_wglu) = ssm_bwd(
        dy_ssm, proj, states, bh, ch, apow, dskip, "ssm_bwd",
        comm=([dw_mo.reshape(8, 128, D_MODEL), dw_glu.astype(BF16).reshape(8, 64, 512)], [True, True]))
    dqkv = None
    for d in DILATIONS:
        dqkv = attn_bwd(proj, dcat, cat, lse, dqkv, d, f"attn_bwd_d{d}", du=du if d == DILATIONS[-1] else None)
    dproj = dqkv.reshape(8, T, 256)
    dw_mi = mm_tn_shared_a(h2, dproj, "mix_dwin")
    (dx1, dg_mpre, do1, dg_f1post), (s_wmi,) = dh_pre_bwd(
        dproj, w_mi, x1, sp["mix_pre_g"], dx2, "mix_dh", comm=([dw_mi], [True]), post=(o1, sp["ffn1_post_g"], 0.5))

    da4 = da.reshape(2, 2, HALF_STATES)
    d_ar, d_ai = da4[:, 0].reshape(32, N_STATE), da4[:, 1].reshape(32, N_STATE)
    d_bbr = _diag_blocks(dbh[:, :, :HALF_STATES], 16, N_STATE).transpose(0, 2, 1)
    d_bbi = _diag_blocks(dbh[:, :, HALF_STATES:], 16, N_STATE).transpose(0, 2, 1)
    _, disc_vjp = jax.vjp(_discretise, sp["a_re"], sp["a_im"], sp["log_dt"], sp["b_re"], sp["b_im"])
    g_are, g_aim, g_ldt, g_bre, g_bim = disc_vjp((d_ar, d_ai, d_bbr, d_bbi))
    g_cre = _diag_blocks(dch[:, :HALF_STATES], N_STATE, 16).transpose(0, 2, 1)
    g_cim = -_diag_blocks(dch[:, HALF_STATES:], N_STATE, 16).transpose(0, 2, 1)
    small = {
        "ffn1_pre_g": jnp.zeros((1, D_MODEL), F32), "ffn1_post_g": dg_f1post, "mix_pre_g": dg_mpre, "a_re": g_are,
        "a_im": g_aim, "log_dt": g_ldt, "b_re": g_bre, "b_im": g_bim, "c_re": g_cre, "c_im": g_cim,
        "d_skip": dd.reshape(1, 512), "b_glu": db_glu, "mix_post_g": dg_mpost, "ffn2_pre_g": dg_f2pre,
        "ffn2_post_g": dg_f2post,
    }
    (dx0, dg_f1pre), (s_w1in, s_w1out), (early,) = _ffn_bwd(
        dx1, do1, (h1, z1, a1), x, sp["ffn1_pre_g"], w1_in, w1_out4, "ffn1",
        dwout_comm=([_pack([small[n] for n in SMALL])], [False]))
    late = gather_two_level(dg_f1pre, "exchange_small")
    small_slots = lax.dynamic_update_slice(early, late, (0, 0, 0))
    big_slots = {"ffn1_w_in": s_w1in, "ffn1_w_out": s_w1out, "w_mix_in": s_wmi, "w_glu": s_wglu, "w_mix_out": s_wmo,
                 "ffn2_w_in": s_w2in, "ffn2_w_out": s_w2out}
    return sq, dx0, big_slots, small_slots


def kernel(x, ffn1_pre_g, ffn1_w_in, ffn1_w_out, ffn1_post_g, mix_pre_g, w_mix_in, a_re, a_im, log_dt, b_re, b_im, c_re, c_im, d_skip, w_glu, b_glu, w_mix_out, mix_post_g, ffn2_pre_g, ffn2_w_in, ffn2_w_out, ffn2_post_g, loss_target, m_ffn1_pre_g, m_ffn1_w_in, m_ffn1_w_out, m_ffn1_post_g, m_mix_pre_g, m_w_mix_in, m_a_re, m_a_im, m_log_dt, m_b_re, m_b_im, m_c_re, m_c_im, m_d_skip, m_w_glu, m_b_glu, m_w_mix_out, m_mix_post_g, m_ffn2_pre_g, m_ffn2_w_in, m_ffn2_w_out, m_ffn2_post_g, v_ffn1_pre_g, v_ffn1_w_in, v_ffn1_w_out, v_ffn1_post_g, v_mix_pre_g, v_w_mix_in, v_a_re, v_a_im, v_log_dt, v_b_re, v_b_im, v_c_re, v_c_im, v_d_skip, v_w_glu, v_b_glu, v_w_mix_out, v_mix_post_g, v_ffn2_pre_g, v_ffn2_w_in, v_ffn2_w_out, v_ffn2_post_g):
    args = dict(locals())
    w = {n: args[n][0] for n in WEIGHTS}
    m = {n: args["m_" + n][0] for n in WEIGHTS}
    v = {n: args["v_" + n][0] for n in WEIGHTS}

    for d in (w, m, v):
        for n in TRANSPOSED:
            d[n] = jnp.swapaxes(d[n], 0, 1)
    wb = {n: w[n].astype(BF16) for n in BIG}
    sp = {n: w[n] for n in SMALL}
    for n in ("ffn1_pre_g", "ffn1_post_g", "mix_pre_g", "mix_post_g", "ffn2_pre_g", "ffn2_post_g", "b_glu", "d_skip"):
        sp[n] = w[n].reshape(1, -1)

    sq, grad_x, big_slots, small_slots = local_step(x[0], loss_target[0], sp, wb)
    loss = lax.psum(0.5 / D_MODEL * jnp.sum(sq), ("x", "y", "c"))

    outs = {}
    for n in BIG:
        shp = w[n].shape
        r2 = lambda t: t.reshape(-1, shp[-1])
        res = adamw(r2(w[n]), r2(m[n]), r2(v[n]), big_slots[n].reshape(N_DEV, -1, shp[-1]), f"adamw_{n}")
        outs[n] = [(jnp.swapaxes(t, 0, 1) if n in TRANSPOSED else t.reshape(shp))[None] for t in res]
    res = adamw(_pack([w[n] for n in SMALL]), _pack([m[n] for n in SMALL]), _pack([v[n] for n in SMALL]),
                small_slots, "adamw_small")
    shapes = [(1,) + w[n].shape for n in SMALL]
    unpacked = [_unpack(t, shapes) for t in res]
    for j, n in enumerate(SMALL):
        outs[n] = [unpacked[k][j] for k in range(4)]

    result = [loss, grad_x[None]]
    for k in range(4):
        result += [outs[n][k] for n in WEIGHTS]
    return tuple(result)
```

```python
import functools
import math

import jax
import jax.numpy as jnp
from jax import lax
from jax.experimental import pallas as pl
from jax.experimental.pallas import tpu as pltpu

F32, BF16 = jnp.float32, jnp.bfloat16
SDS = jax.ShapeDtypeStruct

D_MODEL = 1024
N_DEV = 8
HEAD_DIM = 64
PAIR_W = 128
QBLK = 128
DILATIONS = (1, 4, 16)
N_STATE = 64
HALF_STATES = 1024
NORM_EPS = 1e-6
NEG = -1e30
VMEM_LIMIT = 56 * 1024 * 1024
ADAM_LR, ADAM_B1, ADAM_B2, ADAM_EPS, ADAM_WD, ADAM_STEP = 1e-3, 0.9, 0.999, 1e-8, 0.01, 10
SCAN_TM = 256
SCAN_TM_BWD = 512
SCAN_CW = 512


def _params(n_grid):
    return pltpu.CompilerParams(dimension_semantics=("arbitrary",) * n_grid, vmem_limit_bytes=VMEM_LIMIT)


def _dot(a, b):
    return jnp.dot(a, b, preferred_element_type=F32)


def _dot_nt(a, b):
    return lax.dot_general(a, b, (((1,), (1,)), ((), ())), preferred_element_type=F32)


def _dot_tn(a, b):
    return lax.dot_general(a, b, (((0,), (0,)), ((), ())), preferred_element_type=F32)


def _sigmoid(v):
    return 0.5 * jnp.tanh(0.5 * v) + 0.5


def _resident(shape):
    return pl.BlockSpec(shape, lambda i: (0,) * len(shape), pipeline_mode=pl.Buffered(1))


ROW_SPLIT = 2


def _exchange_phase(ins, outs, scatter, sems, start):
    send_sems, recv_sems, loc_sems = sems
    x, y, c = lax.axis_index("x"), lax.axis_index("y"), lax.axis_index("c")
    me = 4 * x + 2 * y + c
    own_copies, sends, arrivals = [], [], []
    for i in range(len(ins)):
        own = ins[i].at[me] if scatter[i] else ins[i]
        own_copies.append(pltpu.make_async_copy(own, outs[i].at[me], loc_sems.at[i]))
        for k in range(1, N_DEV):
            px = 1 - x if k & 4 else x
            py = 1 - y if k & 2 else y
            pc = 1 - c if k & 1 else c
            peer = 4 * px + 2 * py + pc
            src = ins[i].at[peer] if scatter[i] else ins[i]
            common = dict(src_ref=src, send_sem=send_sems.at[i, k - 1], recv_sem=recv_sems.at[i, k - 1],
                          device_id=(px, py, pc), device_id_type=pl.DeviceIdType.MESH)
            sends.append(pltpu.make_async_remote_copy(dst_ref=outs[i].at[me], **common))
            if not start:
                arrivals.append(pltpu.make_async_remote_copy(dst_ref=outs[i].at[peer], **common))
    if start:
        for cp in own_copies + sends:
            cp.start()
    else:
        for cp in arrivals:
            cp.wait_recv()
        for cp in sends:
            cp.wait_send()
        for cp in own_copies:
            cp.wait()


def _comm_shapes(arrs, scatter):
    n = len(arrs)
    out_shapes = [SDS(a.shape if scatter[i] else (N_DEV,) + a.shape, a.dtype) for i, a in enumerate(arrs)]
    sems = [pltpu.SemaphoreType.DMA((n, N_DEV - 1)), pltpu.SemaphoreType.DMA((n, N_DEV - 1)),
            pltpu.SemaphoreType.DMA((n,))]
    return out_shapes, sems


def gather_two_level(arr, name):
    def body(x_ref, out_ref, send_sems, recv_sems, local_sem):
        x, y, c = lax.axis_index("x"), lax.axis_index("y"), lax.axis_index("c")
        sibling = (x, y, 1 - c)
        chips = [(1 - x, y), (x, 1 - y), (1 - x, 1 - y)]

        def slot(px, py, pc):
            return out_ref.at[4 * px + 2 * py + pc]

        def copy(k, block, to, src=None):
            return pltpu.make_async_remote_copy(
                src_ref=slot(*block) if src is None else src, dst_ref=slot(*block),
                send_sem=send_sems.at[k], recv_sem=recv_sems.at[k], device_id=to, device_id_type=pl.DeviceIdType.MESH)

        mine = pltpu.make_async_copy(x_ref, slot(x, y, c), local_sem)
        mine.start()
        first = [copy(0, (x, y, c), sibling, src=x_ref)]
        first += [copy(1 + j, (x, y, c), (*chip, c), src=x_ref) for j, chip in enumerate(chips)]
        for cp in first:
            cp.start()
        passed = [copy(4 + j, (*chip, c), sibling) for j, chip in enumerate(chips)]
        for j, chip in enumerate(chips):
            copy(1 + j, (*chip, c), (x, y, c)).wait_recv()
            passed[j].start()
        copy(0, sibling, (x, y, c)).wait_recv()
        for j, chip in enumerate(chips):
            copy(4 + j, (*chip, 1 - c), (x, y, c)).wait_recv()
        for cp in first + passed:
            cp.wait_send()
        mine.wait()

    anyspec = pl.BlockSpec(memory_space=pl.ANY)
    return pl.pallas_call(
        body, in_specs=[anyspec], out_specs=anyspec, out_shape=SDS((N_DEV,) + arr.shape, arr.dtype),
        scratch_shapes=[pltpu.SemaphoreType.DMA((N_DEV - 1,)), pltpu.SemaphoreType.DMA((N_DEV - 1,)),
                        pltpu.SemaphoreType.DMA],
        compiler_params=pltpu.CompilerParams(has_side_effects=True), name=name)(arr)


def _call(body, *, grid, in_specs, out_specs, out_shape, args, name, scratch_shapes=(), comm=None):
    n_grid, scratch_shapes = len(grid), list(scratch_shapes)
    if comm is None:
        outs = pl.pallas_call(body, grid=grid, in_specs=in_specs, out_specs=out_specs, out_shape=out_shape,
                              scratch_shapes=scratch_shapes, compiler_params=_params(n_grid), name=name)(*args)
        return outs, []
    arrs, scatter = comm
    nc, n_in, n_out, n_sc = len(arrs), len(in_specs), len(out_specs), len(scratch_shapes)
    comm_shapes, sems = _comm_shapes(arrs, scatter)

    def wrapped(*refs):
        ins, cins = refs[:n_in], refs[n_in:n_in + nc]
        o0 = n_in + nc
        outs, couts = refs[o0:o0 + n_out], refs[o0 + n_out:o0 + n_out + nc]
        s0 = o0 + n_out + nc
        scratch, sem_refs = refs[s0:s0 + n_sc], refs[s0 + n_sc:]
        first = functools.reduce(jnp.logical_and, [pl.program_id(k) == 0 for k in range(n_grid)])
        last = functools.reduce(jnp.logical_and, [pl.program_id(k) == grid[k] - 1 for k in range(n_grid)])

        @pl.when(first)
        def _():
            _exchange_phase(cins, couts, scatter, sem_refs, True)

        body(*ins, *outs, *scratch)

        @pl.when(last)
        def _():
            _exchange_phase(cins, couts, scatter, sem_refs, False)

    anyspec = pl.BlockSpec(memory_space=pl.ANY)
    res = pl.pallas_call(
        wrapped, grid=grid, in_specs=list(in_specs) + [anyspec] * nc, out_specs=list(out_specs) + [anyspec] * nc,
        out_shape=list(out_shape) + comm_shapes, scratch_shapes=scratch_shapes + sems,
        compiler_params=pltpu.CompilerParams(dimension_semantics=("arbitrary",) * n_grid,
                                             vmem_limit_bytes=VMEM_LIMIT, has_side_effects=True),
        name=name)(*args, *arrs)
    return res[:n_out], res[n_out:]


def _rms(xv, g):
    r = lax.rsqrt(jnp.mean(xv * xv, axis=-1, keepdims=True) + NORM_EPS)
    return (xv * r * g).astype(BF16)


def ffn_in(x, g, w, name, comm=None):
    T, D = x.shape
    F = w.shape[1]
    tm = 512

    def body(x_ref, g_ref, w_ref, h_ref, z_ref, a_ref):
        hv = _rms(x_ref[...], g_ref[...])
        h_ref[...] = hv
        pending = None
        for j in range(5):
            if j < 4:
                zs = (_dot_nt(hv, w_ref[j]), _dot_nt(hv, w_ref[j + 4]))
            if pending is not None:
                zg, zu = pending
                sg = _sigmoid(zg)
                silu = zg * sg
                z_ref[0, j - 1] = (zu * (sg + silu - silu * sg)).astype(BF16)
                z_ref[1, j - 1] = silu.astype(BF16)
                a_ref[j - 1] = (silu * zu).astype(BF16)
            pending = zs

    return _call(
        body, grid=(T // tm,),
        in_specs=[pl.BlockSpec((tm, D), lambda i: (i, 0)), pl.BlockSpec((1, D), lambda i: (0, 0)),
                  _resident((8, F, D))],
        out_specs=[pl.BlockSpec((tm, D), lambda i: (i, 0)), pl.BlockSpec((2, 4, tm, F), lambda i: (0, 0, i, 0)),
                   pl.BlockSpec((4, tm, F), lambda i: (0, i, 0))],
        out_shape=[SDS((T, D), BF16), SDS((2, 4, T, F), BF16), SDS((4, T, F), BF16)],
        args=(x, g, w), name=name, comm=comm)


def norm_proj(x, g, w, name):
    T, K = x.shape
    nb, _, N = w.shape
    tm = 512

    def body(x_ref, g_ref, w_ref, h_ref, o_ref):
        hv = _rms(x_ref[...], g_ref[...])
        h_ref[...] = hv
        for b in range(nb):
            o_ref[b] = _dot(hv, w_ref[b])

    return pl.pallas_call(
        body, grid=(T // tm,),
        in_specs=[pl.BlockSpec((tm, K), lambda i: (i, 0)), pl.BlockSpec((1, K), lambda i: (0, 0)),
                  _resident((nb, K, N))],
        out_specs=[pl.BlockSpec((tm, K), lambda i: (i, 0)), pl.BlockSpec((nb, tm, N), lambda i: (0, i, 0))],
        out_shape=[SDS((T, K), BF16), SDS((nb, T, N), F32)], compiler_params=_params(1), name=name)(x, g, w)


def mm_acc_norm(a, w, xres, g, scale, name, comm=None, tgt=None):
    nb, T, K = a.shape
    D = w.shape[2]
    tm = 1024
    rc = tm // ROW_SPLIT
    with_loss = tgt is not None

    def body(a_ref, w_ref, x_ref, g_ref, *rest):
        if with_loss:
            t_ref, dy_ref, sq_ref, do_ref, dg_ref = rest

            @pl.when(pl.program_id(0) == 0)
            def _():
                sq_ref[...] = jnp.zeros_like(sq_ref)
                dg_ref[...] = jnp.zeros_like(dg_ref)
        else:
            o_ref, y_ref = rest
        accs = []
        for c in range(ROW_SPLIT):
            rows = pl.ds(c * rc, rc)
            o = _dot(a_ref[0, rows, :].astype(BF16), w_ref[0])
            for b in range(1, nb):
                o += _dot(a_ref[b, rows, :].astype(BF16), w_ref[b])
            accs.append(o)
        for c, o in enumerate(accs):
            rows = pl.ds(c * rc, rc)
            r = lax.rsqrt(jnp.mean(o * o, axis=-1, keepdims=True) + NORM_EPS)
            y = x_ref[rows, :] + scale * (o * r * g_ref[...])
            if with_loss:
                e = y - t_ref[rows, :]
                dy = e * (1.0 / D)
                dy_ref[rows, :] = dy
                sq_ref[...] += jnp.sum(e * e, axis=0, keepdims=True)
                do, dg = _post_bwd(dy, o, g_ref[...], scale)
                do_ref[rows, :] = do
                dg_ref[...] += dg
            else:
                o_ref[rows, :] = o
                y_ref[rows, :] = y

    tile = pl.BlockSpec((tm, D), lambda i: (i, 0))
    row = pl.BlockSpec((1, D), lambda i: (0, 0))
    in_specs = [pl.BlockSpec((nb, tm, K), lambda i: (0, i, 0)), _resident((nb, K, D)), tile, row]
    args = (a, w, xres, g)
    if with_loss:
        return _call(body, grid=(T // tm,), in_specs=in_specs + [tile], out_specs=[tile, row, tile, row],
                     out_shape=[SDS((T, D), F32), SDS((1, D), F32), SDS((T, D), BF16), SDS((1, D), F32)],
                     args=args + (tgt,), name=name, comm=comm)
    return _call(body, grid=(T // tm,), in_specs=in_specs, out_specs=[tile, tile],
                 out_shape=[SDS((T, D), F32), SDS((T, D), F32)], args=args, name=name, comm=comm)


def _post_bwd(dy, ov, g, scale):
    r = scale * dy
    rstd = lax.rsqrt(jnp.mean(ov * ov, axis=-1, keepdims=True) + NORM_EPS)
    oh = ov * rstd
    rg = r * g
    do = rstd * (rg - oh * jnp.mean(rg * oh, axis=-1, keepdims=True))
    return do.astype(BF16), jnp.sum(r * oh, axis=0, keepdims=True)


def mm_nt_b(gr, w, name):
    T, N = gr.shape
    nb, K, _ = w.shape
    tm = 512

    def body(g_ref, w_ref, o_ref):
        gv = g_ref[...]
        for b in range(nb):
            o_ref[b] = _dot_nt(gv, w_ref[b])

    return pl.pallas_call(
        body, grid=(T // tm,),
        in_specs=[pl.BlockSpec((tm, N), lambda i: (i, 0)), _resident((nb, K, N))],
        out_specs=pl.BlockSpec((nb, tm, K), lambda i: (0, i, 0)),
        out_shape=SDS((nb, T, K), F32), compiler_params=_params(1), name=name)(gr, w)


def ffn_dact(do, w_out, z, name):
    T, D = do.shape
    nb, F, _ = w_out.shape
    tm = 512

    def body(g_ref, w_ref, z_ref, dz_ref):
        gv = g_ref[...]
        pending = None
        for b in range(nb + 1):
            da = _dot_nt(gv, w_ref[b]) if b < nb else None
            if pending is not None:
                dz_ref[0, b - 1] = (pending * z_ref[0, b - 1].astype(F32)).astype(BF16)
                dz_ref[1, b - 1] = (pending * z_ref[1, b - 1].astype(F32)).astype(BF16)
            pending = da

    blk = pl.BlockSpec((2, nb, tm, F), lambda i: (0, 0, i, 0))
    return pl.pallas_call(
        body, grid=(T // tm,),
        in_specs=[pl.BlockSpec((tm, D), lambda i: (i, 0)), _resident((nb, F, D)), blk],
        out_specs=blk, out_shape=SDS((2, nb, T, F), BF16), compiler_params=_params(1), name=name)(do, w_out, z)


def mm_tn(a, g, a_batched, g_batched, nb, name, comm=None):
    T = a.shape[-2]
    K, N = a.shape[-1], g.shape[-1]
    tk = 2048
    nk = T // tk

    def body(a_ref, g_ref, o_ref, acc):
        k = pl.program_id(1)

        @pl.when(k == 0)
        def _():
            acc[...] = jnp.zeros_like(acc)

        acc[...] += _dot_tn(a_ref[...].astype(BF16), g_ref[...].astype(BF16))

        @pl.when(k == nk - 1)
        def _():
            o_ref[...] = acc[...].astype(BF16)

    a_spec = (pl.BlockSpec((None, tk, K), lambda b, k: (b, k, 0)) if a_batched
              else pl.BlockSpec((tk, K), lambda b, k: (k, 0)))
    g_spec = (pl.BlockSpec((None, tk, N), lambda b, k: (b, k, 0)) if g_batched
              else pl.BlockSpec((tk, N), lambda b, k: (k, 0)))
    (out,), slots = _call(
        body, grid=(nb, nk), in_specs=[a_spec, g_spec],
        out_specs=[pl.BlockSpec((None, K, N), lambda b, k: (b, 0, 0))],
        out_shape=[SDS((nb, K, N), BF16)], scratch_shapes=[pltpu.VMEM((K, N), F32)],
        args=(a, g), name=name, comm=comm)
    return out, slots


def mm_tn_shared_a(a, g, name):
    T, K = a.shape
    nb, _, N = g.shape
    tk = 1024
    nk = T // tk

    def body(a_ref, g_ref, o_ref, acc):
        k = pl.program_id(0)

        @pl.when(k == 0)
        def _():
            acc[...] = jnp.zeros_like(acc)

        av = a_ref[...]
        for b in range(nb):
            acc[b] += _dot_tn(av, g_ref[b].astype(BF16))

        @pl.when(k == nk - 1)
        def _():
            o_ref[...] = acc[...].astype(BF16)

    return pl.pallas_call(
        body, grid=(nk,),
        in_specs=[pl.BlockSpec((tk, K), lambda k: (k, 0)), pl.BlockSpec((nb, tk, N), lambda k: (0, k, 0))],
        out_specs=_resident((nb, K, N)), out_shape=SDS((nb, K, N), BF16),
        scratch_shapes=[pltpu.VMEM((nb, K, N), F32)], compiler_params=_params(1), name=name)(a, g)


def dh_pre_bwd(dz, w, x, g, dyres, name, comm=None, post=None, w_transposed=False):
    nb, T, F = dz.shape
    D = x.shape[1]
    tm = 512
    rc = tm // ROW_SPLIT
    mm = _dot if w_transposed else _dot_nt

    def body(dz_ref, w_ref, x_ref, g_ref, dy_ref, *rest):
        if post is None:
            dx_ref, dg_ref = rest
        else:
            o_ref, gp_ref, dx_ref, dg_ref, do_ref, dgp_ref = rest

        @pl.when(pl.program_id(0) == 0)
        def _():
            dg_ref[...] = jnp.zeros_like(dg_ref)
            if post is not None:
                dgp_ref[...] = jnp.zeros_like(dgp_ref)

        accs = []
        for c in range(ROW_SPLIT):
            rows = pl.ds(c * rc, rc)
            dh = mm(dz_ref[0, rows, :].astype(BF16), w_ref[0])
            for b in range(1, nb):
                dh += mm(dz_ref[b, rows, :].astype(BF16), w_ref[b])
            accs.append(dh)
        for c, dh in enumerate(accs):
            rows = pl.ds(c * rc, rc)
            xv = x_ref[rows, :]
            rstd = lax.rsqrt(jnp.mean(xv * xv, axis=-1, keepdims=True) + NORM_EPS)
            xh = xv * rstd
            dg_ref[...] += jnp.sum(dh * xh, axis=0, keepdims=True)
            dhg = dh * g_ref[...]
            dx = dy_ref[rows, :] + rstd * (dhg - xh * jnp.mean(dhg * xh, axis=-1, keepdims=True))
            dx_ref[rows, :] = dx
            if post is not None:
                do, dgp = _post_bwd(dx, o_ref[rows, :], gp_ref[...], post[2])
                do_ref[rows, :] = do
                dgp_ref[...] += dgp

    tile = pl.BlockSpec((tm, D), lambda i: (i, 0))
    row = pl.BlockSpec((1, D), lambda i: (0, 0))
    in_specs = [pl.BlockSpec((nb, tm, F), lambda i: (0, i, 0)), _resident(w.shape), tile, row, tile]
    out_specs, out_shape, args = [tile, row], [SDS((T, D), F32), SDS((1, D), F32)], (dz, w, x, g, dyres)
    if post is not None:
        in_specs += [tile, row]
        out_specs += [tile, row]
        out_shape += [SDS((T, D), BF16), SDS((1, D), F32)]
        args += (post[0], post[1])
    return _call(body, grid=(T // tm,), in_specs=in_specs, out_specs=out_specs, out_shape=out_shape,
                 args=args, name=name, comm=comm)


ATTN_GROUP = {1: 8, 4: 2, 16: 1}
ATTN_GROUP_BWD = {1: 16, 4: 4, 16: 1}
ATTN_UNROLL = 4


def _attn_masks():
    qi = lax.broadcasted_iota(jnp.int32, (QBLK, QBLK), 0)
    kj = lax.broadcasted_iota(jnp.int32, (QBLK, QBLK), 1)
    cur_ok = kj <= qi
    prev_ok = kj >= qi
    dcur = (qi - kj).astype(F32)
    return cur_ok, prev_ok, dcur, dcur + float(QBLK)


def _head_slopes(p, d):
    out = []
    for hq in range(2):
        v = [float(d) * 2.0 ** -(2 * q + hq + 1) for q in range(4)]
        out.append(jnp.where(p == 0, v[0], jnp.where(p == 1, v[1], jnp.where(p == 2, v[2], v[3]))))
    return out


def _rows(start, d):
    return pl.ds(start, QBLK, stride=d) if d > 1 else pl.ds(start, QBLK)


def _pair_spec(rows, part, blk):
    return pl.BlockSpec((None, rows, PAIR_W), lambda p, n: (2 * part + p // 2, blk(n), p % 2))


def _for_query_blocks(d, groups, several):
    blocks = [(g, r) for r in range(d) for g in range(groups)]
    for s in range(0, len(blocks), ATTN_UNROLL):
        several(blocks[s:s + ATTN_UNROLL])


def attn_fwd(proj, d, name, comm=None):
    T = proj.shape[1]
    sb, groups = QBLK * d, ATTN_GROUP[d]
    rb = sb * groups
    nblk = T // rb

    def body(q_ref, kc_ref, kp_ref, vc_ref, vp_ref, o_ref, l_ref):
        p, n = pl.program_id(0), pl.program_id(1)
        cur_ok, prev_ok, dcur, dprev = _attn_masks()
        first_ok = jnp.logical_and(prev_ok, n > 0)
        lane_head = lax.broadcasted_iota(jnp.int32, (QBLK, PAIR_W), 1) // HEAD_DIM
        slopes = _head_slopes(p, d)

        def several(blocks):
            work = []
            for g, r in blocks:
                rows = _rows(g * sb + r, d)
                q = q_ref[rows, :]
                kc, vc = kc_ref[rows, :].astype(BF16), vc_ref[rows, :].astype(BF16)
                if g == 0:
                    prow, pok = _rows(r, d), first_ok
                    kp, vp = kp_ref[prow, :].astype(BF16), vp_ref[prow, :].astype(BF16)
                else:
                    prow, pok = _rows((g - 1) * sb + r, d), prev_ok
                    kp, vp = kc_ref[prow, :].astype(BF16), vc_ref[prow, :].astype(BF16)
                for hq in range(2):
                    qm = jnp.where(lane_head == hq, q, 0.0).astype(BF16)
                    work.append([rows, hq, pok, vc, vp, _dot_nt(qm, kc), _dot_nt(qm, kp)])
            for w in work:
                _, hq, pok, _, _, sc, sp = w
                sc = jnp.where(cur_ok, sc * 0.125 - slopes[hq] * dcur, NEG)
                sp = jnp.where(pok, sp * 0.125 - slopes[hq] * dprev, NEG)
                m = jnp.maximum(jnp.max(sc, axis=1, keepdims=True), jnp.max(sp, axis=1, keepdims=True))
                pc = jnp.exp(sc - m)
                pp = jnp.exp(sp - m)
                den = jnp.sum(pc, axis=1, keepdims=True) + jnp.sum(pp, axis=1, keepdims=True)
                w[5:] = [pc.astype(BF16), pp.astype(BF16), 1.0 / den, m + jnp.log(den)]
            for i in range(0, len(work), 2):
                o_acc = jnp.zeros((QBLK, PAIR_W), F32)
                l_acc = jnp.zeros((QBLK, PAIR_W), F32)
                for rows, hq, _, vc, vp, pc, pp, inv, lse in work[i:i + 2]:
                    hm = lane_head == hq
                    o_acc = jnp.where(hm, (_dot(pc, vc) + _dot(pp, vp)) * inv, o_acc)
                    l_acc = jnp.where(hm, lse, l_acc)
                o_ref[rows, :] = o_acc
                l_ref[rows, :] = l_acc

        _for_query_blocks(d, groups, several)

    cur = lambda part: _pair_spec(rb, part, lambda n: n)
    prv = lambda part: _pair_spec(sb, part, lambda n: jnp.maximum(n * groups - 1, 0))
    return _call(
        body, grid=(4, nblk), in_specs=[cur(0), cur(1), prv(1), cur(2), prv(2)], out_specs=[cur(0), cur(0)],
        out_shape=[SDS((2, T, 2 * PAIR_W), F32), SDS((2, T, 2 * PAIR_W), F32)],
        args=(proj, proj, proj, proj, proj), name=name, comm=comm)


def mix_out(os_, ls_, y_ssm, w_glu, b_glu, w, xres, g, name):
    _, T, HW = y_ssm.shape
    D = w.shape[2]
    tm = 512

    def body(o1, o2, o3, l1, l2, l3, s_ref, wg_ref, bg_ref, w_ref, x_ref, g_ref, cat_ref, l_ref, lg_ref, m_ref, y_ref):
        y0, _ = _gelu(s_ref[0])
        y1, _ = _gelu(s_ref[1])
        lg = _dot(y0.astype(BF16), wg_ref[0]) + _dot(y1.astype(BF16), wg_ref[1]) + bg_ref[...]
        a, b, c = l1[...], l2[...], l3[...]
        m = jnp.maximum(jnp.maximum(a, b), c)
        ea, eb, ec = jnp.exp(a - m), jnp.exp(b - m), jnp.exp(c - m)
        s = ea + eb + ec
        att = (ea * o1[...] + eb * o2[...] + ec * o3[...]) * (1.0 / s)
        sg = _sigmoid(lg)
        ssm0, ssm1 = y0 * sg[:, :HW], y1 * sg[:, HW:]
        cat_ref[pl.ds(0, 2)] = att
        cat_ref[2] = ssm0
        cat_ref[3] = ssm1
        l_ref[...] = m + jnp.log(s)
        lg_ref[0] = lg[:, :HW]
        lg_ref[1] = lg[:, HW:]
        o = _dot(att[0].astype(BF16), w_ref[0]) + _dot(att[1].astype(BF16), w_ref[1])
        o += _dot(ssm0.astype(BF16), w_ref[2]) + _dot(ssm1.astype(BF16), w_ref[3])
        r = lax.rsqrt(jnp.mean(o * o, axis=-1, keepdims=True) + NORM_EPS)
        m_ref[...] = o
        y_ref[...] = x_ref[...] + o * r * g_ref[...]

    spec = pl.BlockSpec((2, tm, HW), lambda i: (0, i, 0))
    tile = pl.BlockSpec((tm, D), lambda i: (i, 0))
    return pl.pallas_call(
        body, grid=(T // tm,),
        in_specs=[spec] * 7 + [_resident(w_glu.shape), _resident(b_glu.shape), _resident(w.shape), tile,
                               pl.BlockSpec((1, D), lambda i: (0, 0))],
        out_specs=[pl.BlockSpec((4, tm, HW), lambda i: (0, i, 0)), spec, spec, tile, tile],
        out_shape=[SDS((4, T, HW), F32), SDS((2, T, HW), F32), SDS((2, T, HW), F32), SDS((T, D), F32),
                   SDS((T, D), F32)],
        compiler_params=_params(1), name=name)(*os_, *ls_, y_ssm, w_glu, b_glu, w, xres, g)


def attn_bwd(proj, dcat, o, lse, acc, d, name, du=None):
    T = proj.shape[1]
    sb, groups = QBLK * d, ATTN_GROUP_BWD[d]
    rb = sb * groups
    nblk = T // rb
    has_acc = acc is not None
    n_parts = 3 if du is None else 4

    def body(*refs):
        (qc_ref, qn_ref, kc_ref, kp_ref, vc_ref, vp_ref, dc_ref, dn_ref, oc_ref, on_ref, lc_ref, ln_ref) = refs[:12]
        acc_ref = refs[12] if has_acc else None
        out_ref = refs[-1]
        if du is not None:
            out_ref[3] = refs[-2][...]
        p, n = pl.program_id(0), pl.program_id(1)
        cur_ok, prev_ok, dcur, dprev = _attn_masks()
        first_ok = jnp.logical_and(prev_ok, n > 0)
        last_ok = jnp.logical_and(prev_ok, n < nblk - 1)
        lane_head = lax.broadcasted_iota(jnp.int32, (QBLK, PAIR_W), 1) // HEAD_DIM
        slopes = _head_slopes(p, d)

        def one(g, r, shared):
            rows = _rows(g * sb + r, d)
            q_c, do_c, o_c, l_c = qc_ref[rows, :], dc_ref[rows, :], oc_ref[rows, :], lc_ref[rows, :]
            k_c, v_c = kc_ref[rows, :].astype(BF16), vc_ref[rows, :].astype(BF16)
            if shared:
                pok_c, k_p, v_p = prev_ok, None, None
            elif g == 0:
                prow, pok_c = _rows(r, d), first_ok
                k_p, v_p = kp_ref[prow, :].astype(BF16), vp_ref[prow, :].astype(BF16)
            else:
                prow, pok_c = _rows((g - 1) * sb + r, d), prev_ok
                k_p, v_p = kc_ref[prow, :].astype(BF16), vc_ref[prow, :].astype(BF16)
            if g == groups - 1:
                nrow, pok_n = _rows(r, d), last_ok
                q_n, do_n, o_n, l_n = qn_ref[nrow, :], dn_ref[nrow, :], on_ref[nrow, :], ln_ref[nrow, :]
            else:
                nrow, pok_n = _rows((g + 1) * sb + r, d), prev_ok
                q_n, do_n, o_n, l_n = qc_ref[nrow, :], dc_ref[nrow, :], oc_ref[nrow, :], lc_ref[nrow, :]
            heads = []
            for hq in range(2):
                hm = lane_head == hq
                qm_c = jnp.where(hm, q_c, 0.0).astype(BF16)
                qm_n = jnp.where(hm, q_n, 0.0).astype(BF16)
                dom_c = jnp.where(hm, do_c, 0.0)
                dom_n = jnp.where(hm, do_n, 0.0)
                dd_c = jnp.sum(dom_c * o_c, axis=1, keepdims=True)
                dd_n = jnp.sum(dom_n * o_n, axis=1, keepdims=True)
                ls_c = jnp.max(jnp.where(hm, l_c, NEG), axis=1, keepdims=True)
                ls_n = jnp.max(jnp.where(hm, l_n, NEG), axis=1, keepdims=True)
                dob_c, dob_n = dom_c.astype(BF16), dom_n.astype(BF16)
                mm = [(_dot_nt(qm_c, k_c), _dot_nt(dob_c, v_c)),
                      None if shared else (_dot_nt(qm_c, k_p), _dot_nt(dob_c, v_p)),
                      (_dot_nt(qm_n, k_c), _dot_nt(dob_n, v_c))]
                heads.append(dict(hq=hq, qm_c=qm_c, qm_n=qm_n, dob_c=dob_c, dob_n=dob_n, mm=mm,
                                  dd=(dd_c, dd_c, dd_n), ls=(ls_c, ls_c, ls_n)))
            return dict(rows=rows, k_c=k_c, k_p=k_p, heads=heads, oks=(cur_ok, pok_c, pok_n), shared=shared)

        def several(blocks):
            work = []
            for i, (g, r) in enumerate(blocks):
                work.append(one(g, r, i > 0 and blocks[i - 1] == (g - 1, r)))
            for i, w in enumerate(work):
                if w["shared"]:
                    w["k_p"] = work[i - 1]["k_c"]
                for hi, h in enumerate(w["heads"]):
                    slope, dist = slopes[h["hq"]], (dcur, dprev, dprev)
                    h["pr"], h["ds"] = [], []
                    for j in range(3):
                        if h["mm"][j] is None:
                            h["pr"].append(work[i - 1]["heads"][hi]["pr"][2])
                            h["ds"].append(work[i - 1]["heads"][hi]["ds"][2])
                            continue
                        s = jnp.where(w["oks"][j], h["mm"][j][0] * 0.125 - slope * dist[j], NEG)
                        pr = jnp.exp(s - h["ls"][j])
                        h["pr"].append(pr.astype(BF16))
                        h["ds"].append((pr * (h["mm"][j][1] - h["dd"][j])).astype(BF16))
            for w in work:
                dq = jnp.zeros((QBLK, PAIR_W), F32)
                dk = jnp.zeros((QBLK, PAIR_W), F32)
                dv = jnp.zeros((QBLK, PAIR_W), F32)
                for h in w["heads"]:
                    ds, pr = h["ds"], h["pr"]
                    dq_h = _dot(ds[0], w["k_c"]) + _dot(ds[1], w["k_p"])
                    dk += (_dot_tn(ds[0], h["qm_c"]) + _dot_tn(ds[2], h["qm_n"])) * 0.125
                    dv += _dot_tn(pr[0], h["dob_c"]) + _dot_tn(pr[2], h["dob_n"])
                    dq = jnp.where(lane_head == h["hq"], dq_h * 0.125, dq)
                for part, val in enumerate((dq, dk, dv)):
                    if has_acc:
                        val = val + acc_ref.at[part][w["rows"], :]
                    out_ref.at[part][w["rows"], :] = val

        _for_query_blocks(d, groups, several)

    cur = lambda part: _pair_spec(rb, part, lambda n: n)
    prv = lambda part: _pair_spec(sb, part, lambda n: jnp.maximum(n * groups - 1, 0))
    nxt = lambda part: _pair_spec(sb, part, lambda n: jnp.minimum((n + 1) * groups, T // sb - 1))
    full = pl.BlockSpec((3, None, rb, PAIR_W), lambda p, n: (0, p // 2, n, p % 2))
    in_specs = [cur(0), nxt(0), cur(1), prv(1), cur(2), prv(2), cur(0), nxt(0), cur(0), nxt(0), cur(0), nxt(0)]
    args = [proj, proj, proj, proj, proj, proj, dcat, dcat, o, o, lse, lse]
    if has_acc:
        in_specs.append(full)
        args.append(acc)
    if du is not None:
        in_specs.append(cur(0))
        args.append(du)
    out_spec = pl.BlockSpec((n_parts, None, rb, PAIR_W), lambda p, n: (0, p // 2, n, p % 2))
    return pl.pallas_call(
        body, grid=(4, nblk), in_specs=in_specs, out_specs=out_spec,
        out_shape=SDS((n_parts, 2, T, 2 * PAIR_W), F32), compiler_params=_params(2), name=name)(*args)


def _scan_rows(buf, tab_ref, reverse, half):
    n_tiles = (buf.shape[0] - 8) // 8
    per_half = HALF_STATES // SCAN_CW
    row = lax.broadcasted_iota(jnp.int32, (8, SCAN_CW), 0)
    sgn = -1.0 if reverse else 1.0

    for j in range(per_half):
        c0 = half * 2 * HALF_STATES + j * SCAN_CW
        cre = pl.ds(c0, SCAN_CW)
        cim = pl.ds(c0 + HALF_STATES, SCAN_CW)
        steps = []
        for s, k in enumerate((1, 2, 4)):
            ok, shift = (row < 8 - k, 8 - k) if reverse else (row >= k, k)
            steps.append((shift, jnp.where(ok, tab_ref[pl.ds(s, 1), cre], 0.0),
                          jnp.where(ok, sgn * tab_ref[pl.ds(s, 1), cim], 0.0)))
        trow = 16 if reverse else 8
        pr, pi = tab_ref[pl.ds(trow, 8), cre], tab_ref[pl.ds(trow, 8), cim]
        for t in range(n_tiles):
            base = 8 * (n_tiles - 1 - t) if reverse else 8 + 8 * t
            rows = pl.ds(base, 8)
            re, im = buf[rows, cre], buf[rows, cim]
            for shift, ar, ai in steps:
                sre, sim = pltpu.roll(re, shift, 0), pltpu.roll(im, shift, 0)
                re, im = re + ar * sre - ai * sim, im + ar * sim + ai * sre
            crow = pl.ds(base + 8 if reverse else base - 1, 1)
            cr, ci = buf[crow, cre], buf[crow, cim]
            buf[rows, cre] = re + pr * cr - pi * ci
            buf[rows, cim] = im + pr * ci + pi * cr


def ssm_fwd(proj, bh, ch, apow, dskip, name, comm=None):
    _, T, C = proj.shape
    tm = SCAN_TM_BWD
    SW = 4 * HALF_STATES

    def body(u_ref, bh_ref, ch_ref, tab_ref, dsk_ref, y_ref, s_ref, buf):
        @pl.when(pl.program_id(0) == 0)
        def _():
            buf[pl.ds(0, 8), :] = jnp.zeros((8, SW), F32)

        for h in range(2):
            buf[pl.ds(8, tm), pl.ds(h * 2 * HALF_STATES, 2 * HALF_STATES)] = _dot(u_ref[h].astype(BF16), bh_ref[h])
        for h in range(2):
            cols = pl.ds(h * 2 * HALF_STATES, 2 * HALF_STATES)
            _scan_rows(buf, tab_ref, False, h)
            sv = buf[pl.ds(8, tm), cols]
            s_ref[:, cols] = sv
            y_ref[h] = _dot(sv.astype(BF16), ch_ref[h]) + dsk_ref[h] * u_ref[h]
        buf[pl.ds(0, 8), :] = buf[pl.ds(tm, 8), :]

    return _call(
        body, grid=(T // tm,),
        in_specs=[pl.BlockSpec((2, tm, C), lambda i: (3, i, 0)),
                  _resident((2, C, 2 * HALF_STATES)), _resident((2, 2 * HALF_STATES, C)), _resident((24, SW)),
                  _resident((2, 1, C))],
        out_specs=[pl.BlockSpec((2, tm, C), lambda i: (0, i, 0)), pl.BlockSpec((tm, SW), lambda i: (i, 0))],
        out_shape=[SDS((2, T, C), F32), SDS((T, SW), F32)],
        scratch_shapes=[pltpu.VMEM((tm + 8, SW), F32)],
        args=(proj, bh, ch, apow, dskip), name=name, comm=comm)


def ssm_bwd(dy, proj, st, bh, ch, apow, dskip, name, comm=None):
    _, T, C = proj.shape
    tm = SCAN_TM_BWD
    nt = T // tm
    SW = 4 * HALF_STATES
    HS2 = 2 * HALF_STATES

    def body(dy_ref, u_ref, s_ref, sp_ref, bh_ref, ch_ref, tab_ref, dsk_ref,
             du_ref, da_ref, dbh_ref, dch_ref, dd_ref, lam):
        i = pl.program_id(0)

        @pl.when(i == 0)
        def _():
            lam[pl.ds(tm, 8), :] = jnp.zeros((8, SW), F32)
            da_ref[...] = jnp.zeros_like(da_ref)
            dbh_ref[...] = jnp.zeros_like(dbh_ref)
            dch_ref[...] = jnp.zeros_like(dch_ref)
            dd_ref[...] = jnp.zeros_like(dd_ref)

        for h in range(2):
            lam[pl.ds(0, tm), pl.ds(h * HS2, HS2)] = _dot_nt(dy_ref[h].astype(BF16), ch_ref[h])
        for h in range(2):
            dyv, uv = dy_ref[h], u_ref[h]
            dch_ref[h] += _dot_tn(s_ref[:, pl.ds(h * HS2, HS2)].astype(BF16), dyv.astype(BF16))
            dd_ref[h] += jnp.sum(dyv * uv, axis=0, keepdims=True)
        for h in range(2):
            _scan_rows(lam, tab_ref, True, h)
            lb = lam[pl.ds(0, tm), pl.ds(h * HS2, HS2)].astype(BF16)
            du_ref[h] = _dot_nt(lb, bh_ref[h]) + dsk_ref[h] * dy_ref[h]
            dbh_ref[h] += _dot_tn(u_ref[h].astype(BF16), lb)

        first = i == nt - 1
        per_half = HALF_STATES // SCAN_CW

        def chunk(j, _):
            c0 = pl.multiple_of((j // per_half) * HS2 + (j % per_half) * SCAN_CW, 128)
            cre, cim = pl.ds(c0, SCAN_CW), pl.ds(pl.multiple_of(c0 + HALF_STATES, 128), SCAN_CW)
            row0 = lax.broadcasted_iota(jnp.int32, (8, SCAN_CW), 0) == 0
            acc_r = jnp.zeros((8, SCAN_CW), F32)
            acc_i = jnp.zeros((8, SCAN_CW), F32)
            for t in range(tm // 8):
                rows = pl.ds(8 * t, 8)
                if t == 0:
                    pre = jnp.where(first, 0.0, sp_ref[pl.ds(7, 1), cre])
                    pim = jnp.where(first, 0.0, sp_ref[pl.ds(7, 1), cim])
                else:
                    pre, pim = s_ref[pl.ds(8 * t - 1, 1), cre], s_ref[pl.ds(8 * t - 1, 1), cim]
                spr = jnp.where(row0, pre, pltpu.roll(s_ref[rows, cre], 1, 0))
                spi = jnp.where(row0, pim, pltpu.roll(s_ref[rows, cim], 1, 0))
                lr, li = lam[rows, cre], lam[rows, cim]
                acc_r += lr * spr + li * spi
                acc_i += li * spr - lr * spi
            da_ref[:, cre] += jnp.sum(acc_r, axis=0, keepdims=True)
            da_ref[:, cim] += jnp.sum(acc_i, axis=0, keepdims=True)
            return 0

        lax.fori_loop(0, 2 * per_half, chunk, 0)
        lam[pl.ds(tm, 8), :] = lam[pl.ds(0, 8), :]

    rev = lambda i: nt - 1 - i
    return _call(
        body, grid=(nt,),
        in_specs=[pl.BlockSpec((2, tm, C), lambda i: (0, rev(i), 0)),
                  pl.BlockSpec((2, tm, C), lambda i: (3, rev(i), 0)),
                  pl.BlockSpec((tm, SW), lambda i: (rev(i), 0)),
                  pl.BlockSpec((8, SW), lambda i: (jnp.maximum(rev(i) * (tm // 8) - 1, 0), 0)),
                  _resident((2, C, HS2)), _resident((2, HS2, C)), _resident((24, SW)), _resident((2, 1, C))],
        out_specs=[pl.BlockSpec((2, tm, C), lambda i: (0, rev(i), 0)),
                   _resident((1, SW)), _resident((2, C, HS2)), _resident((2, HS2, C)), _resident((2, 1, C))],
        out_shape=[SDS((2, T, C), F32), SDS((1, SW), F32), SDS((2, C, HS2), F32), SDS((2, HS2, C), F32),
                   SDS((2, 1, C), F32)],
        scratch_shapes=[pltpu.VMEM((tm + 8, SW), F32)],
        args=(dy, proj, st, st, bh, ch, apow, dskip), name=name, comm=comm)


_GELU_C = math.sqrt(2.0 / math.pi)


def _gelu(x):
    t = jnp.tanh(_GELU_C * (x + 0.044715 * x * x * x))
    return 0.5 * x * (1.0 + t), t


def glu_bwd(dcat, y, lg, w, name):
    _, T, C = y.shape
    tm = 512

    def body(d_ref, y_ref, lg_ref, w_ref, dy_ref, dw_ref, db_ref):
        @pl.when(pl.program_id(0) == 0)
        def _():
            dw_ref[...] = jnp.zeros_like(dw_ref)
            db_ref[...] = jnp.zeros_like(db_ref)

        y2, th, sg, dlg = [], [], [], []
        for h in range(2):
            yy, tt = _gelu(y_ref[h])
            ss = _sigmoid(lg_ref[h])
            y2.append(yy)
            th.append(tt)
            sg.append(ss)
            dlg.append(d_ref[h] * yy * ss * (1.0 - ss))
        dl = jnp.concatenate(dlg, axis=1)
        dlb = dl.astype(BF16)
        db_ref[...] += jnp.sum(dl, axis=0, keepdims=True)
        for h in range(2):
            dy2 = d_ref[h] * sg[h] + _dot_nt(dlb, w_ref[h])
            yv = y_ref[h]
            dgelu = 0.5 * (1.0 + th[h]) + 0.5 * yv * (1.0 - th[h] * th[h]) * _GELU_C * (1.0 + 3 * 0.044715 * yv * yv)
            dy_ref[h] = dy2 * dgelu
            dw_ref[h] += _dot_tn(y2[h].astype(BF16), dlb)

    return pl.pallas_call(
        body, grid=(T // tm,),
        in_specs=[pl.BlockSpec((2, tm, C), lambda i: (1, i, 0)), pl.BlockSpec((2, tm, C), lambda i: (0, i, 0)),
                  pl.BlockSpec((2, tm, C), lambda i: (0, i, 0)), pl.BlockSpec((2, C, 2 * C), lambda i: (0, 0, 0))],
        out_specs=[pl.BlockSpec((2, tm, C), lambda i: (0, i, 0)), pl.BlockSpec((2, C, 2 * C), lambda i: (0, 0, 0)),
                   pl.BlockSpec((1, 2 * C), lambda i: (0, 0))],
        out_shape=[SDS((2, T, C), F32), SDS((2, C, 2 * C), F32), SDS((1, 2 * C), F32)],
        compiler_params=_params(1), name=name)(dcat, y, lg, w)


def adamw(w, m, v, slots, name):
    R, C = w.shape
    tr = R
    for cand in (512, 256, 128, 64, 32, 16, 8):
        if R % cand == 0 and cand * C * 4 <= 2 * 1024 * 1024:
            tr = cand
            break
    c1 = 1.0 / (1.0 - ADAM_B1 ** ADAM_STEP)
    c2 = 1.0 / (1.0 - ADAM_B2 ** ADAM_STEP)

    def body(w_ref, m_ref, v_ref, s_ref, g_ref, d_ref, nm_ref, nv_ref):
        g = s_ref[0].astype(F32)
        for j in range(1, N_DEV):
            g = g + s_ref[j].astype(F32)
        nm = ADAM_B1 * m_ref[...] + (1.0 - ADAM_B1) * g
        nv = ADAM_B2 * v_ref[...] + (1.0 - ADAM_B2) * (g * g)
        g_ref[...] = g
        nm_ref[...] = nm
        nv_ref[...] = nv
        d_ref[...] = -ADAM_LR * ((nm * c1) / (jnp.sqrt(nv * c2) + ADAM_EPS) + ADAM_WD * w_ref[...])

    spec = pl.BlockSpec((tr, C), lambda i: (i, 0))
    return pl.pallas_call(
        body, grid=(R // tr,),
        in_specs=[spec, spec, spec, pl.BlockSpec((N_DEV, tr, C), lambda i: (0, i, 0))],
        out_specs=[spec] * 4, out_shape=[SDS((R, C), F32)] * 4, compiler_params=_params(1), name=name)(w, m, v, slots)


def _discretise(a_re, a_im, log_dt, b_re, b_im):
    dt = jnp.exp(log_dt)[:, None]
    e = jnp.exp(dt * a_re)
    ar, ai = e * jnp.cos(dt * a_im), e * jnp.sin(dt * a_im)
    den = a_re * a_re + a_im * a_im
    nr, ni = ar - 1.0, ai
    wr = (nr * a_re + ni * a_im) / den
    wi = (ni * a_re - nr * a_im) / den
    bbr = wr[..., None] * b_re - wi[..., None] * b_im
    bbi = wr[..., None] * b_im + wi[..., None] * b_re
    return ar, ai, bbr, bbi


def _block_diag(t):
    eye = jnp.eye(16, dtype=t.dtype).reshape(1, 16, 1, 16, 1)
    r, c = t.shape[1], t.shape[2]
    return (t.reshape(2, 16, r, 1, c) * eye).reshape(2, 16 * r, 16 * c)


def _diag_blocks(m, r, c):
    eye = jnp.eye(16, dtype=m.dtype).reshape(1, 16, 1, 16, 1)
    return jnp.sum(m.reshape(2, 16, r, 16, c) * eye, axis=3).reshape(32, r, c)


def _state_vec(re, im):
    return jnp.stack([re.reshape(2, HALF_STATES), im.reshape(2, HALF_STATES)], axis=1).reshape(-1)


BIG = ("ffn1_w_in", "ffn1_w_out", "w_mix_in", "w_glu", "w_mix_out", "ffn2_w_in", "ffn2_w_out")
WEIGHTS = ("ffn1_pre_g", "ffn1_w_in", "ffn1_w_out", "ffn1_post_g", "mix_pre_g", "w_mix_in", "a_re", "a_im", "log_dt",
           "b_re", "b_im", "c_re", "c_im", "d_skip", "w_glu", "b_glu", "w_mix_out", "mix_post_g", "ffn2_pre_g",
           "ffn2_w_in", "ffn2_w_out", "ffn2_post_g")
SMALL = tuple(n for n in WEIGHTS if n not in BIG)
TRANSPOSED = ("ffn1_w_in", "ffn2_w_in")
PACK_COLS = 1024


def _pack(parts):
    flat = jnp.concatenate([p.reshape(-1) for p in parts])
    rows = -(-flat.shape[0] // (8 * PACK_COLS)) * 8
    return jnp.pad(flat, (0, rows * PACK_COLS - flat.shape[0])).reshape(rows, PACK_COLS)


def _unpack(packed, shapes):
    flat, out, off = packed.reshape(-1), [], 0
    for s in shapes:
        n = math.prod(s)
        out.append(flat[off:off + n].reshape(s))
        off += n
    return out


def _gather(names, wb):
    return [wb[n] for n in names], [False] * len(names)


def _ffn_bwd(dy, do, saved, x, pre_g, w_in, w_out4, tag, post=None, dwout_comm=None):
    h, z, a = saved
    T = x.shape[0]
    dz = ffn_dact(do, w_out4, z, f"{tag}_dact")
    dz8 = dz.reshape(8, T, dz.shape[-1])
    dw_out, extra = mm_tn(a, do, True, False, 4, f"{tag}_dwout", comm=dwout_comm)
    dw_in, (s_out,) = mm_tn(dz8, h, True, False, 8, f"{tag}_dwin", comm=([dw_out.reshape(8, -1, D_MODEL)], [True]))
    outs, (s_in,) = dh_pre_bwd(dz8, w_in, x, pre_g, dy, f"{tag}_dh", comm=([dw_in], [True]), post=post,
                               w_transposed=True)
    return outs, (s_in, s_out), extra


def local_step(x, tgt, sp, wb):
    T = x.shape[0]
    ar, ai, bbr, bbi = _discretise(sp["a_re"], sp["a_im"], sp["log_dt"], sp["b_re"], sp["b_im"])
    powers = [(ar, ai)]
    for _ in range(7):
        pr, pi = powers[-1]
        powers.append((pr * ar - pi * ai, pr * ai + pi * ar))
    zero = jnp.zeros_like(ar)
    rows = [_state_vec(*powers[k - 1]) for k in (1, 2, 4)] + [_state_vec(zero, zero)] * 5
    rows += [_state_vec(pr, pi) for pr, pi in powers]
    rows += [_state_vec(pr, -pi) for pr, pi in reversed(powers)]
    apow = jnp.stack(rows)
    bh = jnp.concatenate([_block_diag(bbr.transpose(0, 2, 1)), _block_diag(bbi.transpose(0, 2, 1))], axis=2)
    ch = jnp.concatenate([_block_diag(sp["c_re"].transpose(0, 2, 1)), _block_diag(-sp["c_im"].transpose(0, 2, 1))], axis=1)
    bh, ch = bh.astype(BF16), ch.astype(BF16)
    dskip = sp["d_skip"].reshape(2, 1, 256)

    w1_in = gather_two_level(wb["ffn1_w_in"], "gather_w1in")
    (h1, z1, a1), (w1_out, w_mi) = ffn_in(
        x, sp["ffn1_pre_g"], w1_in, "ffn1_in", comm=_gather(["ffn1_w_out", "w_mix_in"], wb))
    w1_out4 = w1_out.reshape(4, -1, D_MODEL)
    (o1, x1), (w_glu, w_mo) = mm_acc_norm(
        a1, w1_out4, x, sp["ffn1_post_g"], 0.5, "ffn1_out", comm=_gather(["w_glu", "w_mix_out"], wb))
    w_glu2, w_mo4 = w_glu.reshape(2, 256, 512), w_mo.reshape(4, 256, D_MODEL)
    h2, proj = norm_proj(x1, sp["mix_pre_g"], w_mi, "mix_proj")
    (y_ssm, states), (w2_in,) = ssm_fwd(proj, bh, ch, apow, dskip, "ssm_fwd", comm=_gather(["ffn2_w_in"], wb))
    os_, ls_ = [], []
    for d in DILATIONS:
        (o_d, l_d), got = attn_fwd(proj, d, f"attn_fwd_d{d}",
                                   comm=_gather(["ffn2_w_out"], wb) if d == DILATIONS[-1] else None)
        os_.append(o_d)
        ls_.append(l_d)
    w2_out4 = got[0].reshape(4, -1, D_MODEL)
    cat, lse, lg, mixed, x2 = mix_out(os_, ls_, y_ssm, w_glu2, sp["b_glu"], w_mo4, x1, sp["mix_post_g"], "mix_out")
    (h3, z3, a3), _ = ffn_in(x2, sp["ffn2_pre_g"], w2_in, "ffn2_in")
    (dy3, sq, do3, dg_f2post), _ = mm_acc_norm(a3, w2_out4, x2, sp["ffn2_post_g"], 0.5, "ffn2_out", tgt=tgt)

    (dx2, dg_f2pre, dmixed, dg_mpost), (s_w2in, s_w2out), _ = _ffn_bwd(
        dy3, do3, (h3, z3, a3), x2, sp["ffn2_pre_g"], w2_in, w2_out4, "ffn2", post=(mixed, sp["mix_post_g"], 1.0))
    dcat = mm_nt_b(dmixed, w_mo4, "mix_dcat")
    dw_mo, _ = mm_tn(cat, dmixed, True, False, 4, "mix_dwout")
    dy_ssm, dw_glu, db_glu = glu_bwd(dcat, y_ssm, lg, w_glu2, "glu_bwd")
    (du, da, dbh, dch, dd), (s_wmo, s_wglu) = ssm_bwd(
        dy_ssm, proj, states, bh, ch, apow, dskip, "ssm_bwd",
        comm=([dw_mo.reshape(8, 128, D_MODEL), dw_glu.astype(BF16).reshape(8, 64, 512)], [True, True]))
    dqkv = None
    for d in DILATIONS:
        dqkv = attn_bwd(proj, dcat, cat, lse, dqkv, d, f"attn_bwd_d{d}", du=du if d == DILATIONS[-1] else None)
    dproj = dqkv.reshape(8, T, 256)
    dw_mi = mm_tn_shared_a(h2, dproj, "mix_dwin")
    (dx1, dg_mpre, do1, dg_f1post), (s_wmi,) = dh_pre_bwd(
        dproj, w_mi, x1, sp["mix_pre_g"], dx2, "mix_dh", comm=([dw_mi], [True]), post=(o1, sp["ffn1_post_g"], 0.5))

    da4 = da.reshape(2, 2, HALF_STATES)
    d_ar, d_ai = da4[:, 0].reshape(32, N_STATE), da4[:, 1].reshape(32, N_STATE)
    d_bbr = _diag_blocks(dbh[:, :, :HALF_STATES], 16, N_STATE).transpose(0, 2, 1)
    d_bbi = _diag_blocks(dbh[:, :, HALF_STATES:], 16, N_STATE).transpose(0, 2, 1)
    _, disc_vjp = jax.vjp(_discretise, sp["a_re"], sp["a_im"], sp["log_dt"], sp["b_re"], sp["b_im"])
    g_are, g_aim, g_ldt, g_bre, g_bim = disc_vjp((d_ar, d_ai, d_bbr, d_bbi))
    g_cre = _diag_blocks(dch[:, :HALF_STATES], N_STATE, 16).transpose(0, 2, 1)
    g_cim = -_diag_blocks(dch[:, HALF_STATES:], N_STATE, 16).transpose(0, 2, 1)
    small = {
        "ffn1_pre_g": jnp.zeros((1, D_MODEL), F32), "ffn1_post_g": dg_f1post, "mix_pre_g": dg_mpre, "a_re": g_are,
        "a_im": g_aim, "log_dt": g_ldt, "b_re": g_bre, "b_im": g_bim, "c_re": g_cre, "c_im": g_cim,
        "d_skip": dd.reshape(1, 512), "b_glu": db_glu, "mix_post_g": dg_mpost, "ffn2_pre_g": dg_f2pre,
        "ffn2_post_g": dg_f2post,
    }
    (dx0, dg_f1pre), (s_w1in, s_w1out), (early,) = _ffn_bwd(
        dx1, do1, (h1, z1, a1), x, sp["ffn1_pre_g"], w1_in, w1_out4, "ffn1",
        dwout_comm=([_pack([small[n] for n in SMALL])], [False]))
    late = gather_two_level(dg_f1pre, "exchange_small")
    small_slots = lax.dynamic_update_slice(early, late, (0, 0, 0))
    big_slots = {"ffn1_w_in": s_w1in, "ffn1_w_out": s_w1out, "w_mix_in": s_wmi, "w_glu": s_wglu, "w_mix_out": s_wmo,
                 "ffn2_w_in": s_w2in, "ffn2_w_out": s_w2out}
    return sq, dx0, big_slots, small_slots


def kernel(x, ffn1_pre_g, ffn1_w_in, ffn1_w_out, ffn1_post_g, mix_pre_g, w_mix_in, a_re, a_im, log_dt, b_re, b_im, c_re, c_im, d_skip, w_glu, b_glu, w_mix_out, mix_post_g, ffn2_pre_g, ffn2_w_in, ffn2_w_out, ffn2_post_g, loss_target, m_ffn1_pre_g, m_ffn1_w_in, m_ffn1_w_out, m_ffn1_post_g, m_mix_pre_g, m_w_mix_in, m_a_re, m_a_im, m_log_dt, m_b_re, m_b_im, m_c_re, m_c_im, m_d_skip, m_w_glu, m_b_glu, m_w_mix_out, m_mix_post_g, m_ffn2_pre_g, m_ffn2_w_in, m_ffn2_w_out, m_ffn2_post_g, v_ffn1_pre_g, v_ffn1_w_in, v_ffn1_w_out, v_ffn1_post_g, v_mix_pre_g, v_w_mix_in, v_a_re, v_a_im, v_log_dt, v_b_re, v_b_im, v_c_re, v_c_im, v_d_skip, v_w_glu, v_b_glu, v_w_mix_out, v_mix_post_g, v_ffn2_pre_g, v_ffn2_w_in, v_ffn2_w_out, v_ffn2_post_g):
    args = dict(locals())
    w = {n: args[n][0] for n in WEIGHTS}
    m = {n: args["m_" + n][0] for n in WEIGHTS}
    v = {n: args["v_" + n][0] for n in WEIGHTS}

    for d in (w, m, v):
        for n in TRANSPOSED:
            d[n] = jnp.swapaxes(d[n], 0, 1)
    wb = {n: w[n].astype(BF16) for n in BIG}
    sp = {n: w[n] for n in SMALL}
    for n in ("ffn1_pre_g", "ffn1_post_g", "mix_pre_g", "mix_post_g", "ffn2_pre_g", "ffn2_post_g", "b_glu", "d_skip"):
        sp[n] = w[n].reshape(1, -1)

    sq, grad_x, big_slots, small_slots = local_step(x[0], loss_target[0], sp, wb)
    loss = lax.psum(0.5 / D_MODEL * jnp.sum(sq), ("x", "y", "c"))

    outs = {}
    for n in BIG:
        shp = w[n].shape
        r2 = lambda t: t.reshape(-1, shp[-1])
        res = adamw(r2(w[n]), r2(m[n]), r2(v[n]), big_slots[n].reshape(N_DEV, -1, shp[-1]), f"adamw_{n}")
        outs[n] = [(jnp.swapaxes(t, 0, 1) if n in TRANSPOSED else t.reshape(shp))[None] for t in res]
    res = adamw(_pack([w[n] for n in SMALL]), _pack([m[n] for n in SMALL]), _pack([v[n] for n in SMALL]),
                small_slots, "adamw_small")
    shapes = [(1,) + w[n].shape for n in SMALL]
    unpacked = [_unpack(t, shapes) for t in res]
    for j, n in enumerate(SMALL):
        outs[n] = [unpacked[k][j] for k in range(4)]

    result = [loss, grad_x[None]]
    for k in range(4):
        result += [outs[n][k] for n in WEIGHTS]
    return tuple(result)
```

```python
import functools
import math

import jax
import jax.numpy as jnp
from jax import lax
from jax.experimental import pallas as pl
from jax.experimental.pallas import tpu as pltpu

F32, BF16 = jnp.float32, jnp.bfloat16
SDS = jax.ShapeDtypeStruct

D_MODEL = 1024
N_DEV = 8
HEAD_DIM = 64
PAIR_W = 128
QBLK = 128
DILATIONS = (1, 4, 16)
N_STATE = 64
HALF_STATES = 1024
NORM_EPS = 1e-6
NEG = -1e30
VMEM_LIMIT = 56 * 1024 * 1024
ADAM_LR, ADAM_B1, ADAM_B2, ADAM_EPS, ADAM_WD, ADAM_STEP = 1e-3, 0.9, 0.999, 1e-8, 0.01, 10
SCAN_TM = 512
SCAN_CW = 512


def _params(n_grid):
    return pltpu.CompilerParams(dimension_semantics=("arbitrary",) * n_grid, vmem_limit_bytes=VMEM_LIMIT)


def _dot(a, b):
    return jnp.dot(a, b, preferred_element_type=F32)


def _dot_nt(a, b):
    return lax.dot_general(a, b, (((1,), (1,)), ((), ())), preferred_element_type=F32)


def _dot_tn(a, b):
    return lax.dot_general(a, b, (((0,), (0,)), ((), ())), preferred_element_type=F32)


def _sigmoid(v):
    return 0.5 * jnp.tanh(0.5 * v) + 0.5


def _resident(shape):
    return pl.BlockSpec(shape, lambda i: (0,) * len(shape), pipeline_mode=pl.Buffered(1))


ROW_SPLIT = 2


def _exchange_phase(ins, outs, scatter, sems, start):
    send_sems, recv_sems, loc_sems = sems
    x, y, c = lax.axis_index("x"), lax.axis_index("y"), lax.axis_index("c")
    me = 4 * x + 2 * y + c
    own_copies, sends, arrivals = [], [], []
    for i in range(len(ins)):
        own = ins[i].at[me] if scatter[i] else ins[i]
        own_copies.append(pltpu.make_async_copy(own, outs[i].at[me], loc_sems.at[i]))
        for k in range(1, N_DEV):
            px = 1 - x if k & 4 else x
            py = 1 - y if k & 2 else y
            pc = 1 - c if k & 1 else c
            peer = 4 * px + 2 * py + pc
            src = ins[i].at[peer] if scatter[i] else ins[i]
            common = dict(src_ref=src, send_sem=send_sems.at[i, k - 1], recv_sem=recv_sems.at[i, k - 1],
                          device_id=(px, py, pc), device_id_type=pl.DeviceIdType.MESH)
            sends.append(pltpu.make_async_remote_copy(dst_ref=outs[i].at[me], **common))
            if not start:
                arrivals.append(pltpu.make_async_remote_copy(dst_ref=outs[i].at[peer], **common))
    if start:
        for cp in own_copies + sends:
            cp.start()
    else:
        for cp in arrivals:
            cp.wait_recv()
        for cp in sends:
            cp.wait_send()
        for cp in own_copies:
            cp.wait()


def _comm_shapes(arrs, scatter):
    n = len(arrs)
    out_shapes = [SDS(a.shape if scatter[i] else (N_DEV,) + a.shape, a.dtype) for i, a in enumerate(arrs)]
    sems = [pltpu.SemaphoreType.DMA((n, N_DEV - 1)), pltpu.SemaphoreType.DMA((n, N_DEV - 1)),
            pltpu.SemaphoreType.DMA((n,))]
    return out_shapes, sems


def gather_two_level(arr, name):
    def body(x_ref, out_ref, send_sems, recv_sems, local_sem):
        x, y, c = lax.axis_index("x"), lax.axis_index("y"), lax.axis_index("c")
        sibling = (x, y, 1 - c)
        chips = [(1 - x, y), (x, 1 - y), (1 - x, 1 - y)]

        def slot(px, py, pc):
            return out_ref.at[4 * px + 2 * py + pc]

        def copy(k, block, to, src=None):
            return pltpu.make_async_remote_copy(
                src_ref=slot(*block) if src is None else src, dst_ref=slot(*block),
                send_sem=send_sems.at[k], recv_sem=recv_sems.at[k], device_id=to, device_id_type=pl.DeviceIdType.MESH)

        mine = pltpu.make_async_copy(x_ref, slot(x, y, c), local_sem)
        mine.start()
        first = [copy(0, (x, y, c), sibling, src=x_ref)]
        first += [copy(1 + j, (x, y, c), (*chip, c), src=x_ref) for j, chip in enumerate(chips)]
        for cp in first:
            cp.start()
        passed = [copy(4 + j, (*chip, c), sibling) for j, chip in enumerate(chips)]
        for j, chip in enumerate(chips):
            copy(1 + j, (*chip, c), (x, y, c)).wait_recv()
            passed[j].start()
        copy(0, sibling, (x, y, c)).wait_recv()
        for j, chip in enumerate(chips):
            copy(4 + j, (*chip, 1 - c), (x, y, c)).wait_recv()
        for cp in first + passed:
            cp.wait_send()
        mine.wait()

    anyspec = pl.BlockSpec(memory_space=pl.ANY)
    return pl.pallas_call(
        body, in_specs=[anyspec], out_specs=anyspec, out_shape=SDS((N_DEV,) + arr.shape, arr.dtype),
        scratch_shapes=[pltpu.SemaphoreType.DMA((N_DEV - 1,)), pltpu.SemaphoreType.DMA((N_DEV - 1,)),
                        pltpu.SemaphoreType.DMA],
        compiler_params=pltpu.CompilerParams(has_side_effects=True), name=name)(arr)


def _call(body, *, grid, in_specs, out_specs, out_shape, args, name, scratch_shapes=(), comm=None):
    n_grid, scratch_shapes = len(grid), list(scratch_shapes)
    if comm is None:
        outs = pl.pallas_call(body, grid=grid, in_specs=in_specs, out_specs=out_specs, out_shape=out_shape,
                              scratch_shapes=scratch_shapes, compiler_params=_params(n_grid), name=name)(*args)
        return outs, []
    arrs, scatter = comm
    nc, n_in, n_out, n_sc = len(arrs), len(in_specs), len(out_specs), len(scratch_shapes)
    comm_shapes, sems = _comm_shapes(arrs, scatter)

    def wrapped(*refs):
        ins, cins = refs[:n_in], refs[n_in:n_in + nc]
        o0 = n_in + nc
        outs, couts = refs[o0:o0 + n_out], refs[o0 + n_out:o0 + n_out + nc]
        s0 = o0 + n_out + nc
        scratch, sem_refs = refs[s0:s0 + n_sc], refs[s0 + n_sc:]
        first = functools.reduce(jnp.logical_and, [pl.program_id(k) == 0 for k in range(n_grid)])
        last = functools.reduce(jnp.logical_and, [pl.program_id(k) == grid[k] - 1 for k in range(n_grid)])

        @pl.when(first)
        def _():
            _exchange_phase(cins, couts, scatter, sem_refs, True)

        body(*ins, *outs, *scratch)

        @pl.when(last)
        def _():
            _exchange_phase(cins, couts, scatter, sem_refs, False)

    anyspec = pl.BlockSpec(memory_space=pl.ANY)
    res = pl.pallas_call(
        wrapped, grid=grid, in_specs=list(in_specs) + [anyspec] * nc, out_specs=list(out_specs) + [anyspec] * nc,
        out_shape=list(out_shape) + comm_shapes, scratch_shapes=scratch_shapes + sems,
        compiler_params=pltpu.CompilerParams(dimension_semantics=("arbitrary",) * n_grid,
                                             vmem_limit_bytes=VMEM_LIMIT, has_side_effects=True),
        name=name)(*args, *arrs)
    return res[:n_out], res[n_out:]


def _rms(xv, g):
    r = lax.rsqrt(jnp.mean(xv * xv, axis=-1, keepdims=True) + NORM_EPS)
    return (xv * r * g).astype(BF16)


def ffn_in(x, g, w, name, comm=None):
    T, D = x.shape
    F = w.shape[1]
    tm = 512

    def body(x_ref, g_ref, w_ref, h_ref, z_ref, a_ref):
        hv = _rms(x_ref[...], g_ref[...])
        h_ref[...] = hv
        pending = None
        for j in range(5):
            if j < 4:
                zs = (_dot_nt(hv, w_ref[j]), _dot_nt(hv, w_ref[j + 4]))
            if pending is not None:
                zg, zu = pending
                sg = _sigmoid(zg)
                silu = zg * sg
                z_ref[0, j - 1] = (zu * (sg + silu - silu * sg)).astype(BF16)
                z_ref[1, j - 1] = silu.astype(BF16)
                a_ref[j - 1] = (silu * zu).astype(BF16)
            pending = zs

    return _call(
        body, grid=(T // tm,),
        in_specs=[pl.BlockSpec((tm, D), lambda i: (i, 0)), pl.BlockSpec((1, D), lambda i: (0, 0)),
                  _resident((8, F, D))],
        out_specs=[pl.BlockSpec((tm, D), lambda i: (i, 0)), pl.BlockSpec((2, 4, tm, F), lambda i: (0, 0, i, 0)),
                   pl.BlockSpec((4, tm, F), lambda i: (0, i, 0))],
        out_shape=[SDS((T, D), BF16), SDS((2, 4, T, F), BF16), SDS((4, T, F), BF16)],
        args=(x, g, w), name=name, comm=comm)


def norm_proj(x, g, w, name):
    T, K = x.shape
    nb, _, N = w.shape
    tm = 512

    def body(x_ref, g_ref, w_ref, h_ref, o_ref):
        hv = _rms(x_ref[...], g_ref[...])
        h_ref[...] = hv
        for b in range(nb):
            o_ref[b] = _dot(hv, w_ref[b])

    return pl.pallas_call(
        body, grid=(T // tm,),
        in_specs=[pl.BlockSpec((tm, K), lambda i: (i, 0)), pl.BlockSpec((1, K), lambda i: (0, 0)),
                  _resident((nb, K, N))],
        out_specs=[pl.BlockSpec((tm, K), lambda i: (i, 0)), pl.BlockSpec((nb, tm, N), lambda i: (0, i, 0))],
        out_shape=[SDS((T, K), BF16), SDS((nb, T, N), F32)], compiler_params=_params(1), name=name)(x, g, w)


def mm_acc_norm(a, w, xres, g, scale, name, comm=None, tgt=None):
    nb, T, K = a.shape
    D = w.shape[2]
    tm = 1024
    rc = tm // ROW_SPLIT
    with_loss = tgt is not None

    def body(a_ref, w_ref, x_ref, g_ref, *rest):
        if with_loss:
            t_ref, dy_ref, sq_ref, do_ref, dg_ref = rest

            @pl.when(pl.program_id(0) == 0)
            def _():
                sq_ref[...] = jnp.zeros_like(sq_ref)
                dg_ref[...] = jnp.zeros_like(dg_ref)
        else:
            o_ref, y_ref = rest
        accs = []
        for c in range(ROW_SPLIT):
            rows = pl.ds(c * rc, rc)
            o = _dot(a_ref[0, rows, :].astype(BF16), w_ref[0])
            for b in range(1, nb):
                o += _dot(a_ref[b, rows, :].astype(BF16), w_ref[b])
            accs.append(o)
        for c, o in enumerate(accs):
            rows = pl.ds(c * rc, rc)
            r = lax.rsqrt(jnp.mean(o * o, axis=-1, keepdims=True) + NORM_EPS)
            y = x_ref[rows, :] + scale * (o * r * g_ref[...])
            if with_loss:
                e = y - t_ref[rows, :]
                dy = e * (1.0 / D)
                dy_ref[rows, :] = dy
                sq_ref[...] += jnp.sum(e * e, axis=0, keepdims=True)
                do, dg = _post_bwd(dy, o, g_ref[...], scale)
                do_ref[rows, :] = do
                dg_ref[...] += dg
            else:
                o_ref[rows, :] = o
                y_ref[rows, :] = y

    tile = pl.BlockSpec((tm, D), lambda i: (i, 0))
    row = pl.BlockSpec((1, D), lambda i: (0, 0))
    in_specs = [pl.BlockSpec((nb, tm, K), lambda i: (0, i, 0)), _resident((nb, K, D)), tile, row]
    args = (a, w, xres, g)
    if with_loss:
        return _call(body, grid=(T // tm,), in_specs=in_specs + [tile], out_specs=[tile, row, tile, row],
                     out_shape=[SDS((T, D), F32), SDS((1, D), F32), SDS((T, D), BF16), SDS((1, D), F32)],
                     args=args + (tgt,), name=name, comm=comm)
    return _call(body, grid=(T // tm,), in_specs=in_specs, out_specs=[tile, tile],
                 out_shape=[SDS((T, D), F32), SDS((T, D), F32)], args=args, name=name, comm=comm)


def _post_bwd(dy, ov, g, scale):
    r = scale * dy
    rstd = lax.rsqrt(jnp.mean(ov * ov, axis=-1, keepdims=True) + NORM_EPS)
    oh = ov * rstd
    rg = r * g
    do = rstd * (rg - oh * jnp.mean(rg * oh, axis=-1, keepdims=True))
    return do.astype(BF16), jnp.sum(r * oh, axis=0, keepdims=True)


def mm_nt_b(gr, w, name):
    T, N = gr.shape
    nb, K, _ = w.shape
    tm = 512

    def body(g_ref, w_ref, o_ref):
        gv = g_ref[...]
        for b in range(nb):
            o_ref[b] = _dot_nt(gv, w_ref[b])

    return pl.pallas_call(
        body, grid=(T // tm,),
        in_specs=[pl.BlockSpec((tm, N), lambda i: (i, 0)), _resident((nb, K, N))],
        out_specs=pl.BlockSpec((nb, tm, K), lambda i: (0, i, 0)),
        out_shape=SDS((nb, T, K), F32), compiler_params=_params(1), name=name)(gr, w)


def ffn_dact(do, w_out, z, name):
    T, D = do.shape
    nb, F, _ = w_out.shape
    tm = 512

    def body(g_ref, w_ref, z_ref, dz_ref):
        gv = g_ref[...]
        pending = None
        for b in range(nb + 1):
            da = _dot_nt(gv, w_ref[b]) if b < nb else None
            if pending is not None:
                dz_ref[0, b - 1] = (pending * z_ref[0, b - 1].astype(F32)).astype(BF16)
                dz_ref[1, b - 1] = (pending * z_ref[1, b - 1].astype(F32)).astype(BF16)
            pending = da

    blk = pl.BlockSpec((2, nb, tm, F), lambda i: (0, 0, i, 0))
    return pl.pallas_call(
        body, grid=(T // tm,),
        in_specs=[pl.BlockSpec((tm, D), lambda i: (i, 0)), _resident((nb, F, D)), blk],
        out_specs=blk, out_shape=SDS((2, nb, T, F), BF16), compiler_params=_params(1), name=name)(do, w_out, z)


def mm_tn(a, g, a_batched, g_batched, nb, name, comm=None):
    T = a.shape[-2]
    K, N = a.shape[-1], g.shape[-1]
    tk = 2048
    nk = T // tk

    def body(a_ref, g_ref, o_ref, acc):
        k = pl.program_id(1)

        @pl.when(k == 0)
        def _():
            acc[...] = jnp.zeros_like(acc)

        acc[...] += _dot_tn(a_ref[...].astype(BF16), g_ref[...].astype(BF16))

        @pl.when(k == nk - 1)
        def _():
            o_ref[...] = acc[...].astype(BF16)

    a_spec = (pl.BlockSpec((None, tk, K), lambda b, k: (b, k, 0)) if a_batched
              else pl.BlockSpec((tk, K), lambda b, k: (k, 0)))
    g_spec = (pl.BlockSpec((None, tk, N), lambda b, k: (b, k, 0)) if g_batched
              else pl.BlockSpec((tk, N), lambda b, k: (k, 0)))
    (out,), slots = _call(
        body, grid=(nb, nk), in_specs=[a_spec, g_spec],
        out_specs=[pl.BlockSpec((None, K, N), lambda b, k: (b, 0, 0))],
        out_shape=[SDS((nb, K, N), BF16)], scratch_shapes=[pltpu.VMEM((K, N), F32)],
        args=(a, g), name=name, comm=comm)
    return out, slots


def mm_tn_shared_a(a, g, name):
    T, K = a.shape
    nb, _, N = g.shape
    tk = 1024
    nk = T // tk

    def body(a_ref, g_ref, o_ref, acc):
        k = pl.program_id(0)

        @pl.when(k == 0)
        def _():
            acc[...] = jnp.zeros_like(acc)

        av = a_ref[...]
        for b in range(nb):
            acc[b] += _dot_tn(av, g_ref[b].astype(BF16))

        @pl.when(k == nk - 1)
        def _():
            o_ref[...] = acc[...].astype(BF16)

    return pl.pallas_call(
        body, grid=(nk,),
        in_specs=[pl.BlockSpec((tk, K), lambda k: (k, 0)), pl.BlockSpec((nb, tk, N), lambda k: (0, k, 0))],
        out_specs=_resident((nb, K, N)), out_shape=SDS((nb, K, N), BF16),
        scratch_shapes=[pltpu.VMEM((nb, K, N), F32)], compiler_params=_params(1), name=name)(a, g)


def dh_pre_bwd(dz, w, x, g, dyres, name, comm=None, post=None, w_transposed=False):
    nb, T, F = dz.shape
    D = x.shape[1]
    tm = 512
    rc = tm // ROW_SPLIT
    mm = _dot if w_transposed else _dot_nt

    def body(dz_ref, w_ref, x_ref, g_ref, dy_ref, *rest):
        if post is None:
            dx_ref, dg_ref = rest
        else:
            o_ref, gp_ref, dx_ref, dg_ref, do_ref, dgp_ref = rest

        @pl.when(pl.program_id(0) == 0)
        def _():
            dg_ref[...] = jnp.zeros_like(dg_ref)
            if post is not None:
                dgp_ref[...] = jnp.zeros_like(dgp_ref)

        accs = []
        for c in range(ROW_SPLIT):
            rows = pl.ds(c * rc, rc)
            dh = mm(dz_ref[0, rows, :].astype(BF16), w_ref[0])
            for b in range(1, nb):
                dh += mm(dz_ref[b, rows, :].astype(BF16), w_ref[b])
            accs.append(dh)
        for c, dh in enumerate(accs):
            rows = pl.ds(c * rc, rc)
            xv = x_ref[rows, :]
            rstd = lax.rsqrt(jnp.mean(xv * xv, axis=-1, keepdims=True) + NORM_EPS)
            xh = xv * rstd
            dg_ref[...] += jnp.sum(dh * xh, axis=0, keepdims=True)
            dhg = dh * g_ref[...]
            dx = dy_ref[rows, :] + rstd * (dhg - xh * jnp.mean(dhg * xh, axis=-1, keepdims=True))
            dx_ref[rows, :] = dx
            if post is not None:
                do, dgp = _post_bwd(dx, o_ref[rows, :], gp_ref[...], post[2])
                do_ref[rows, :] = do
                dgp_ref[...] += dgp

    tile = pl.BlockSpec((tm, D), lambda i: (i, 0))
    row = pl.BlockSpec((1, D), lambda i: (0, 0))
    in_specs = [pl.BlockSpec((nb, tm, F), lambda i: (0, i, 0)), _resident(w.shape), tile, row, tile]
    out_specs, out_shape, args = [tile, row], [SDS((T, D), F32), SDS((1, D), F32)], (dz, w, x, g, dyres)
    if post is not None:
        in_specs += [tile, row]
        out_specs += [tile, row]
        out_shape += [SDS((T, D), BF16), SDS((1, D), F32)]
        args += (post[0], post[1])
    return _call(body, grid=(T // tm,), in_specs=in_specs, out_specs=out_specs, out_shape=out_shape,
                 args=args, name=name, comm=comm)


ATTN_GROUP = {1: 8, 4: 2, 16: 1}
ATTN_GROUP_BWD = {1: 16, 4: 4, 16: 1}
ATTN_UNROLL = 4


def _attn_masks():
    qi = lax.broadcasted_iota(jnp.int32, (QBLK, QBLK), 0)
    kj = lax.broadcasted_iota(jnp.int32, (QBLK, QBLK), 1)
    cur_ok = kj <= qi
    prev_ok = kj >= qi
    dcur = (qi - kj).astype(F32)
    return cur_ok, prev_ok, dcur, dcur + float(QBLK)


def _head_slopes(p, d):
    out = []
    for hq in range(2):
        v = [float(d) * 2.0 ** -(2 * q + hq + 1) for q in range(4)]
        out.append(jnp.where(p == 0, v[0], jnp.where(p == 1, v[1], jnp.where(p == 2, v[2], v[3]))))
    return out


def _rows(start, d):
    return pl.ds(start, QBLK, stride=d) if d > 1 else pl.ds(start, QBLK)


def _pair_spec(rows, part, blk):
    return pl.BlockSpec((None, rows, PAIR_W), lambda p, n: (2 * part + p // 2, blk(n), p % 2))


def _for_query_blocks(d, groups, several):
    blocks = [(g, r) for r in range(d) for g in range(groups)]
    for s in range(0, len(blocks), ATTN_UNROLL):
        several(blocks[s:s + ATTN_UNROLL])


def attn_fwd(proj, d, name, comm=None):
    T = proj.shape[1]
    sb, groups = QBLK * d, ATTN_GROUP[d]
    rb = sb * groups
    nblk = T // rb

    def body(q_ref, kc_ref, kp_ref, vc_ref, vp_ref, o_ref, l_ref):
        p, n = pl.program_id(0), pl.program_id(1)
        cur_ok, prev_ok, dcur, dprev = _attn_masks()
        first_ok = jnp.logical_and(prev_ok, n > 0)
        lane_head = lax.broadcasted_iota(jnp.int32, (QBLK, PAIR_W), 1) // HEAD_DIM
        slopes = _head_slopes(p, d)

        def several(blocks):
            work = []
            for g, r in blocks:
                rows = _rows(g * sb + r, d)
                q = q_ref[rows, :]
                kc, vc = kc_ref[rows, :].astype(BF16), vc_ref[rows, :].astype(BF16)
                if g == 0:
                    prow, pok = _rows(r, d), first_ok
                    kp, vp = kp_ref[prow, :].astype(BF16), vp_ref[prow, :].astype(BF16)
                else:
                    prow, pok = _rows((g - 1) * sb + r, d), prev_ok
                    kp, vp = kc_ref[prow, :].astype(BF16), vc_ref[prow, :].astype(BF16)
                for hq in range(2):
                    qm = jnp.where(lane_head == hq, q, 0.0).astype(BF16)
                    work.append([rows, hq, pok, vc, vp, _dot_nt(qm, kc), _dot_nt(qm, kp)])
            for w in work:
                _, hq, pok, _, _, sc, sp = w
                sc = jnp.where(cur_ok, sc * 0.125 - slopes[hq] * dcur, NEG)
                sp = jnp.where(pok, sp * 0.125 - slopes[hq] * dprev, NEG)
                m = jnp.maximum(jnp.max(sc, axis=1, keepdims=True), jnp.max(sp, axis=1, keepdims=True))
                pc = jnp.exp(sc - m)
                pp = jnp.exp(sp - m)
                den = jnp.sum(pc, axis=1, keepdims=True) + jnp.sum(pp, axis=1, keepdims=True)
                w[5:] = [pc.astype(BF16), pp.astype(BF16), 1.0 / den, m + jnp.log(den)]
            for i in range(0, len(work), 2):
                o_acc = jnp.zeros((QBLK, PAIR_W), F32)
                l_acc = jnp.zeros((QBLK, PAIR_W), F32)
                for rows, hq, _, vc, vp, pc, pp, inv, lse in work[i:i + 2]:
                    hm = lane_head == hq
                    o_acc = jnp.where(hm, (_dot(pc, vc) + _dot(pp, vp)) * inv, o_acc)
                    l_acc = jnp.where(hm, lse, l_acc)
                o_ref[rows, :] = o_acc
                l_ref[rows, :] = l_acc

        _for_query_blocks(d, groups, several)

    cur = lambda part: _pair_spec(rb, part, lambda n: n)
    prv = lambda part: _pair_spec(sb, part, lambda n: jnp.maximum(n * groups - 1, 0))
    return _call(
        body, grid=(4, nblk), in_specs=[cur(0), cur(1), prv(1), cur(2), prv(2)], out_specs=[cur(0), cur(0)],
        out_shape=[SDS((2, T, 2 * PAIR_W), F32), SDS((2, T, 2 * PAIR_W), F32)],
        args=(proj, proj, proj, proj, proj), name=name, comm=comm)


def mix_out(os_, ls_, y_ssm, w_glu, b_glu, w, xres, g, name):
    _, T, HW = y_ssm.shape
    D = w.shape[2]
    tm = 512

    def body(o1, o2, o3, l1, l2, l3, s_ref, wg_ref, bg_ref, w_ref, x_ref, g_ref, cat_ref, l_ref, lg_ref, m_ref, y_ref):
        y0, _ = _gelu(s_ref[0])
        y1, _ = _gelu(s_ref[1])
        lg = _dot(y0.astype(BF16), wg_ref[0]) + _dot(y1.astype(BF16), wg_ref[1]) + bg_ref[...]
        a, b, c = l1[...], l2[...], l3[...]
        m = jnp.maximum(jnp.maximum(a, b), c)
        ea, eb, ec = jnp.exp(a - m), jnp.exp(b - m), jnp.exp(c - m)
        s = ea + eb + ec
        att = (ea * o1[...] + eb * o2[...] + ec * o3[...]) * (1.0 / s)
        sg = _sigmoid(lg)
        ssm0, ssm1 = y0 * sg[:, :HW], y1 * sg[:, HW:]
        cat_ref[pl.ds(0, 2)] = att
        cat_ref[2] = ssm0
        cat_ref[3] = ssm1
        l_ref[...] = m + jnp.log(s)
        lg_ref[0] = lg[:, :HW]
        lg_ref[1] = lg[:, HW:]
        o = _dot(att[0].astype(BF16), w_ref[0]) + _dot(att[1].astype(BF16), w_ref[1])
        o += _dot(ssm0.astype(BF16), w_ref[2]) + _dot(ssm1.astype(BF16), w_ref[3])
        r = lax.rsqrt(jnp.mean(o * o, axis=-1, keepdims=True) + NORM_EPS)
        m_ref[...] = o
        y_ref[...] = x_ref[...] + o * r * g_ref[...]

    spec = pl.BlockSpec((2, tm, HW), lambda i: (0, i, 0))
    tile = pl.BlockSpec((tm, D), lambda i: (i, 0))
    return pl.pallas_call(
        body, grid=(T // tm,),
        in_specs=[spec] * 7 + [_resident(w_glu.shape), _resident(b_glu.shape), _resident(w.shape), tile,
                               pl.BlockSpec((1, D), lambda i: (0, 0))],
        out_specs=[pl.BlockSpec((4, tm, HW), lambda i: (0, i, 0)), spec, spec, tile, tile],
        out_shape=[SDS((4, T, HW), F32), SDS((2, T, HW), F32), SDS((2, T, HW), F32), SDS((T, D), F32),
                   SDS((T, D), F32)],
        compiler_params=_params(1), name=name)(*os_, *ls_, y_ssm, w_glu, b_glu, w, xres, g)


def attn_bwd(proj, dcat, o, lse, acc, d, name, du=None):
    T = proj.shape[1]
    sb, groups = QBLK * d, ATTN_GROUP_BWD[d]
    rb = sb * groups
    nblk = T // rb
    has_acc = acc is not None
    n_parts = 3 if du is None else 4

    def body(*refs):
        (qc_ref, qn_ref, kc_ref, kp_ref, vc_ref, vp_ref, dc_ref, dn_ref, oc_ref, on_ref, lc_ref, ln_ref) = refs[:12]
        acc_ref = refs[12] if has_acc else None
        out_ref = refs[-1]
        if du is not None:
            out_ref[3] = refs[-2][...]
        p, n = pl.program_id(0), pl.program_id(1)
        cur_ok, prev_ok, dcur, dprev = _attn_masks()
        first_ok = jnp.logical_and(prev_ok, n > 0)
        last_ok = jnp.logical_and(prev_ok, n < nblk - 1)
        lane_head = lax.broadcasted_iota(jnp.int32, (QBLK, PAIR_W), 1) // HEAD_DIM
        slopes = _head_slopes(p, d)

        def one(g, r, shared):
            rows = _rows(g * sb + r, d)
            q_c, do_c, o_c, l_c = qc_ref[rows, :], dc_ref[rows, :], oc_ref[rows, :], lc_ref[rows, :]
            k_c, v_c = kc_ref[rows, :].astype(BF16), vc_ref[rows, :].astype(BF16)
            if shared:
                pok_c, k_p, v_p = prev_ok, None, None
            elif g == 0:
                prow, pok_c = _rows(r, d), first_ok
                k_p, v_p = kp_ref[prow, :].astype(BF16), vp_ref[prow, :].astype(BF16)
            else:
                prow, pok_c = _rows((g - 1) * sb + r, d), prev_ok
                k_p, v_p = kc_ref[prow, :].astype(BF16), vc_ref[prow, :].astype(BF16)
            if g == groups - 1:
                nrow, pok_n = _rows(r, d), last_ok
                q_n, do_n, o_n, l_n = qn_ref[nrow, :], dn_ref[nrow, :], on_ref[nrow, :], ln_ref[nrow, :]
            else:
                nrow, pok_n = _rows((g + 1) * sb + r, d), prev_ok
                q_n, do_n, o_n, l_n = qc_ref[nrow, :], dc_ref[nrow, :], oc_ref[nrow, :], lc_ref[nrow, :]
            heads = []
            for hq in range(2):
                hm = lane_head == hq
                qm_c = jnp.where(hm, q_c, 0.0).astype(BF16)
                qm_n = jnp.where(hm, q_n, 0.0).astype(BF16)
                dom_c = jnp.where(hm, do_c, 0.0)
                dom_n = jnp.where(hm, do_n, 0.0)
                dd_c = jnp.sum(dom_c * o_c, axis=1, keepdims=True)
                dd_n = jnp.sum(dom_n * o_n, axis=1, keepdims=True)
                ls_c = jnp.max(jnp.where(hm, l_c, NEG), axis=1, keepdims=True)
                ls_n = jnp.max(jnp.where(hm, l_n, NEG), axis=1, keepdims=True)
                dob_c, dob_n = dom_c.astype(BF16), dom_n.astype(BF16)
                mm = [(_dot_nt(qm_c, k_c), _dot_nt(dob_c, v_c)),
                      None if shared else (_dot_nt(qm_c, k_p), _dot_nt(dob_c, v_p)),
                      (_dot_nt(qm_n, k_c), _dot_nt(dob_n, v_c))]
                heads.append(dict(hq=hq, qm_c=qm_c, qm_n=qm_n, dob_c=dob_c, dob_n=dob_n, mm=mm,
                                  dd=(dd_c, dd_c, dd_n), ls=(ls_c, ls_c, ls_n)))
            return dict(rows=rows, k_c=k_c, k_p=k_p, heads=heads, oks=(cur_ok, pok_c, pok_n), shared=shared)

        def several(blocks):
            work = []
            for i, (g, r) in enumerate(blocks):
                work.append(one(g, r, i > 0 and blocks[i - 1] == (g - 1, r)))
            for i, w in enumerate(work):
                if w["shared"]:
                    w["k_p"] = work[i - 1]["k_c"]
                for hi, h in enumerate(w["heads"]):
                    slope, dist = slopes[h["hq"]], (dcur, dprev, dprev)
                    h["pr"], h["ds"] = [], []
                    for j in range(3):
                        if h["mm"][j] is None:
                            h["pr"].append(work[i - 1]["heads"][hi]["pr"][2])
                            h["ds"].append(work[i - 1]["heads"][hi]["ds"][2])
                            continue
                        s = jnp.where(w["oks"][j], h["mm"][j][0] * 0.125 - slope * dist[j], NEG)
                        pr = jnp.exp(s - h["ls"][j])
                        h["pr"].append(pr.astype(BF16))
                        h["ds"].append((pr * (h["mm"][j][1] - h["dd"][j])).astype(BF16))
            for w in work:
                dq = jnp.zeros((QBLK, PAIR_W), F32)
                dk = jnp.zeros((QBLK, PAIR_W), F32)
                dv = jnp.zeros((QBLK, PAIR_W), F32)
                for h in w["heads"]:
                    ds, pr = h["ds"], h["pr"]
                    dq_h = _dot(ds[0], w["k_c"]) + _dot(ds[1], w["k_p"])
                    dk += (_dot_tn(ds[0], h["qm_c"]) + _dot_tn(ds[2], h["qm_n"])) * 0.125
                    dv += _dot_tn(pr[0], h["dob_c"]) + _dot_tn(pr[2], h["dob_n"])
                    dq = jnp.where(lane_head == h["hq"], dq_h * 0.125, dq)
                for part, val in enumerate((dq, dk, dv)):
                    if has_acc:
                        val = val + acc_ref.at[part][w["rows"], :]
                    out_ref.at[part][w["rows"], :] = val

        _for_query_blocks(d, groups, several)

    cur = lambda part: _pair_spec(rb, part, lambda n: n)
    prv = lambda part: _pair_spec(sb, part, lambda n: jnp.maximum(n * groups - 1, 0))
    nxt = lambda part: _pair_spec(sb, part, lambda n: jnp.minimum((n + 1) * groups, T // sb - 1))
    full = pl.BlockSpec((3, None, rb, PAIR_W), lambda p, n: (0, p // 2, n, p % 2))
    in_specs = [cur(0), nxt(0), cur(1), prv(1), cur(2), prv(2), cur(0), nxt(0), cur(0), nxt(0), cur(0), nxt(0)]
    args = [proj, proj, proj, proj, proj, proj, dcat, dcat, o, o, lse, lse]
    if has_acc:
        in_specs.append(full)
        args.append(acc)
    if du is not None:
        in_specs.append(cur(0))
        args.append(du)
    out_spec = pl.BlockSpec((n_parts, None, rb, PAIR_W), lambda p, n: (0, p // 2, n, p % 2))
    return pl.pallas_call(
        body, grid=(4, nblk), in_specs=in_specs, out_specs=out_spec,
        out_shape=SDS((n_parts, 2, T, 2 * PAIR_W), F32), compiler_params=_params(2), name=name)(*args)


def _scan_rows(buf, tab_ref, reverse, half):
    n_tiles = (buf.shape[0] - 8) // 8
    per_half = HALF_STATES // SCAN_CW
    row = lax.broadcasted_iota(jnp.int32, (8, SCAN_CW), 0)
    sgn = -1.0 if reverse else 1.0

    for j in range(per_half):
        c0 = half * 2 * HALF_STATES + j * SCAN_CW
        cre = pl.ds(c0, SCAN_CW)
        cim = pl.ds(c0 + HALF_STATES, SCAN_CW)
        steps = []
        for s, k in enumerate((1, 2, 4)):
            ok, shift = (row < 8 - k, 8 - k) if reverse else (row >= k, k)
            steps.append((shift, jnp.where(ok, tab_ref[pl.ds(s, 1), cre], 0.0),
                          jnp.where(ok, sgn * tab_ref[pl.ds(s, 1), cim], 0.0)))
        trow = 16 if reverse else 8
        pr, pi = tab_ref[pl.ds(trow, 8), cre], tab_ref[pl.ds(trow, 8), cim]
        for t in range(n_tiles):
            base = 8 * (n_tiles - 1 - t) if reverse else 8 + 8 * t
            rows = pl.ds(base, 8)
            re, im = buf[rows, cre], buf[rows, cim]
            for shift, ar, ai in steps:
                sre, sim = pltpu.roll(re, shift, 0), pltpu.roll(im, shift, 0)
                re, im = re + ar * sre - ai * sim, im + ar * sim + ai * sre
            crow = pl.ds(base + 8 if reverse else base - 1, 1)
            cr, ci = buf[crow, cre], buf[crow, cim]
            buf[rows, cre] = re + pr * cr - pi * ci
            buf[rows, cim] = im + pr * ci + pi * cr


def ssm_fwd(proj, bh, ch, apow, dskip, name, comm=None):
    _, T, C = proj.shape
    tm = SCAN_TM
    SW = 4 * HALF_STATES

    def body(u_ref, bh_ref, ch_ref, tab_ref, dsk_ref, y_ref, s_ref, buf):
        @pl.when(pl.program_id(0) == 0)
        def _():
            buf[pl.ds(0, 8), :] = jnp.zeros((8, SW), F32)

        for h in range(2):
            buf[pl.ds(8, tm), pl.ds(h * 2 * HALF_STATES, 2 * HALF_STATES)] = _dot(u_ref[h].astype(BF16), bh_ref[h])
        for h in range(2):
            cols = pl.ds(h * 2 * HALF_STATES, 2 * HALF_STATES)
            _scan_rows(buf, tab_ref, False, h)
            sv = buf[pl.ds(8, tm), cols]
            s_ref[:, cols] = sv
            y_ref[h] = _dot(sv.astype(BF16), ch_ref[h]) + dsk_ref[h] * u_ref[h]
        buf[pl.ds(0, 8), :] = buf[pl.ds(tm, 8), :]

    return _call(
        body, grid=(T // tm,),
        in_specs=[pl.BlockSpec((2, tm, C), lambda i: (3, i, 0)),
                  _resident((2, C, 2 * HALF_STATES)), _resident((2, 2 * HALF_STATES, C)), _resident((24, SW)),
                  _resident((2, 1, C))],
        out_specs=[pl.BlockSpec((2, tm, C), lambda i: (0, i, 0)), pl.BlockSpec((tm, SW), lambda i: (i, 0))],
        out_shape=[SDS((2, T, C), F32), SDS((T, SW), F32)],
        scratch_shapes=[pltpu.VMEM((tm + 8, SW), F32)],
        args=(proj, bh, ch, apow, dskip), name=name, comm=comm)


def ssm_bwd(dy, proj, st, bh, ch, apow, dskip, name, comm=None):
    _, T, C = proj.shape
    tm = SCAN_TM
    nt = T // tm
    SW = 4 * HALF_STATES
    HS2 = 2 * HALF_STATES

    def body(dy_ref, u_ref, s_ref, sp_ref, bh_ref, ch_ref, tab_ref, dsk_ref,
             du_ref, da_ref, dbh_ref, dch_ref, dd_ref, lam):
        i = pl.program_id(0)

        @pl.when(i == 0)
        def _():
            lam[pl.ds(tm, 8), :] = jnp.zeros((8, SW), F32)
            da_ref[...] = jnp.zeros_like(da_ref)
            dbh_ref[...] = jnp.zeros_like(dbh_ref)
            dch_ref[...] = jnp.zeros_like(dch_ref)
            dd_ref[...] = jnp.zeros_like(dd_ref)

        for h in range(2):
            lam[pl.ds(0, tm), pl.ds(h * HS2, HS2)] = _dot_nt(dy_ref[h].astype(BF16), ch_ref[h])
        for h in range(2):
            dyv, uv = dy_ref[h], u_ref[h]
            dch_ref[h] += _dot_tn(s_ref[:, pl.ds(h * HS2, HS2)].astype(BF16), dyv.astype(BF16))
            dd_ref[h] += jnp.sum(dyv * uv, axis=0, keepdims=True)
        for h in range(2):
            _scan_rows(lam, tab_ref, True, h)
            lb = lam[pl.ds(0, tm), pl.ds(h * HS2, HS2)].astype(BF16)
            du_ref[h] = _dot_nt(lb, bh_ref[h]) + dsk_ref[h] * dy_ref[h]
            dbh_ref[h] += _dot_tn(u_ref[h].astype(BF16), lb)

        first = i == nt - 1
        per_half = HALF_STATES // SCAN_CW

        def chunk(j, _):
            c0 = pl.multiple_of((j // per_half) * HS2 + (j % per_half) * SCAN_CW, 128)
            cre, cim = pl.ds(c0, SCAN_CW), pl.ds(pl.multiple_of(c0 + HALF_STATES, 128), SCAN_CW)
            row0 = lax.broadcasted_iota(jnp.int32, (8, SCAN_CW), 0) == 0
            acc_r = jnp.zeros((8, SCAN_CW), F32)
            acc_i = jnp.zeros((8, SCAN_CW), F32)
            for t in range(tm // 8):
                rows = pl.ds(8 * t, 8)
                if t == 0:
                    pre = jnp.where(first, 0.0, sp_ref[pl.ds(7, 1), cre])
                    pim = jnp.where(first, 0.0, sp_ref[pl.ds(7, 1), cim])
                else:
                    pre, pim = s_ref[pl.ds(8 * t - 1, 1), cre], s_ref[pl.ds(8 * t - 1, 1), cim]
                spr = jnp.where(row0, pre, pltpu.roll(s_ref[rows, cre], 1, 0))
                spi = jnp.where(row0, pim, pltpu.roll(s_ref[rows, cim], 1, 0))
                lr, li = lam[rows, cre], lam[rows, cim]
                acc_r += lr * spr + li * spi
                acc_i += li * spr - lr * spi
            da_ref[:, cre] += jnp.sum(acc_r, axis=0, keepdims=True)
            da_ref[:, cim] += jnp.sum(acc_i, axis=0, keepdims=True)
            return 0

        lax.fori_loop(0, 2 * per_half, chunk, 0)
        lam[pl.ds(tm, 8), :] = lam[pl.ds(0, 8), :]

    rev = lambda i: nt - 1 - i
    return _call(
        body, grid=(nt,),
        in_specs=[pl.BlockSpec((2, tm, C), lambda i: (0, rev(i), 0)),
                  pl.BlockSpec((2, tm, C), lambda i: (3, rev(i), 0)),
                  pl.BlockSpec((tm, SW), lambda i: (rev(i), 0)),
                  pl.BlockSpec((8, SW), lambda i: (jnp.maximum(rev(i) * (tm // 8) - 1, 0), 0)),
                  _resident((2, C, HS2)), _resident((2, HS2, C)), _resident((24, SW)), _resident((2, 1, C))],
        out_specs=[pl.BlockSpec((2, tm, C), lambda i: (0, rev(i), 0)),
                   _resident((1, SW)), _resident((2, C, HS2)), _resident((2, HS2, C)), _resident((2, 1, C))],
        out_shape=[SDS((2, T, C), F32), SDS((1, SW), F32), SDS((2, C, HS2), F32), SDS((2, HS2, C), F32),
                   SDS((2, 1, C), F32)],
        scratch_shapes=[pltpu.VMEM((tm + 8, SW), F32)],
        args=(dy, proj, st, st, bh, ch, apow, dskip), name=name, comm=comm)


_GELU_C = math.sqrt(2.0 / math.pi)


def _gelu(x):
    t = jnp.tanh(_GELU_C * (x + 0.044715 * x * x * x))
    return 0.5 * x * (1.0 + t), t


def glu_bwd(dcat, y, lg, w, name):
    _, T, C = y.shape
    tm = 512

    def body(d_ref, y_ref, lg_ref, w_ref, dy_ref, dw_ref, db_ref):
        @pl.when(pl.program_id(0) == 0)
        def _():
            dw_ref[...] = jnp.zeros_like(dw_ref)
            db_ref[...] = jnp.zeros_like(db_ref)

        y2, th, sg, dlg = [], [], [], []
        for h in range(2):
            yy, tt = _gelu(y_ref[h])
            ss = _sigmoid(lg_ref[h])
            y2.append(yy)
            th.append(tt)
            sg.append(ss)
            dlg.append(d_ref[h] * yy * ss * (1.0 - ss))
        dl = jnp.concatenate(dlg, axis=1)
        dlb = dl.astype(BF16)
        db_ref[...] += jnp.sum(dl, axis=0, keepdims=True)
        for h in range(2):
            dy2 = d_ref[h] * sg[h] + _dot_nt(dlb, w_ref[h])
            yv = y_ref[h]
            dgelu = 0.5 * (1.0 + th[h]) + 0.5 * yv * (1.0 - th[h] * th[h]) * _GELU_C * (1.0 + 3 * 0.044715 * yv * yv)
            dy_ref[h] = dy2 * dgelu
            dw_ref[h] += _dot_tn(y2[h].astype(BF16), dlb)

    return pl.pallas_call(
        body, grid=(T // tm,),
        in_specs=[pl.BlockSpec((2, tm, C), lambda i: (1, i, 0)), pl.BlockSpec((2, tm, C), lambda i: (0, i, 0)),
                  pl.BlockSpec((2, tm, C), lambda i: (0, i, 0)), pl.BlockSpec((2, C, 2 * C), lambda i: (0, 0, 0))],
        out_specs=[pl.BlockSpec((2, tm, C), lambda i: (0, i, 0)), pl.BlockSpec((2, C, 2 * C), lambda i: (0, 0, 0)),
                   pl.BlockSpec((1, 2 * C), lambda i: (0, 0))],
        out_shape=[SDS((2, T, C), F32), SDS((2, C, 2 * C), F32), SDS((1, 2 * C), F32)],
        compiler_params=_params(1), name=name)(dcat, y, lg, w)


def adamw(w, m, v, slots, name):
    R, C = w.shape
    tr = R
    for cand in (512, 256, 128, 64, 32, 16, 8):
        if R % cand == 0 and cand * C * 4 <= 2 * 1024 * 1024:
            tr = cand
            break
    c1 = 1.0 / (1.0 - ADAM_B1 ** ADAM_STEP)
    c2 = 1.0 / (1.0 - ADAM_B2 ** ADAM_STEP)

    def body(w_ref, m_ref, v_ref, s_ref, g_ref, d_ref, nm_ref, nv_ref):
        g = s_ref[0].astype(F32)
        for j in range(1, N_DEV):
            g = g + s_ref[j].astype(F32)
        nm = ADAM_B1 * m_ref[...] + (1.0 - ADAM_B1) * g
        nv = ADAM_B2 * v_ref[...] + (1.0 - ADAM_B2) * (g * g)
        g_ref[...] = g
        nm_ref[...] = nm
        nv_ref[...] = nv
        d_ref[...] = -ADAM_LR * ((nm * c1) / (jnp.sqrt(nv * c2) + ADAM_EPS) + ADAM_WD * w_ref[...])

    spec = pl.BlockSpec((tr, C), lambda i: (i, 0))
    return pl.pallas_call(
        body, grid=(R // tr,),
        in_specs=[spec, spec, spec, pl.BlockSpec((N_DEV, tr, C), lambda i: (0, i, 0))],
        out_specs=[spec] * 4, out_shape=[SDS((R, C), F32)] * 4, compiler_params=_params(1), name=name)(w, m, v, slots)


def _discretise(a_re, a_im, log_dt, b_re, b_im):
    dt = jnp.exp(log_dt)[:, None]
    e = jnp.exp(dt * a_re)
    ar, ai = e * jnp.cos(dt * a_im), e * jnp.sin(dt * a_im)
    den = a_re * a_re + a_im * a_im
    nr, ni = ar - 1.0, ai
    wr = (nr * a_re + ni * a_im) / den
    wi = (ni * a_re - nr * a_im) / den
    bbr = wr[..., None] * b_re - wi[..., None] * b_im
    bbi = wr[..., None] * b_im + wi[..., None] * b_re
    return ar, ai, bbr, bbi


def _block_diag(t):
    eye = jnp.eye(16, dtype=t.dtype).reshape(1, 16, 1, 16, 1)
    r, c = t.shape[1], t.shape[2]
    return (t.reshape(2, 16, r, 1, c) * eye).reshape(2, 16 * r, 16 * c)


def _diag_blocks(m, r, c):
    eye = jnp.eye(16, dtype=m.dtype).reshape(1, 16, 1, 16, 1)
    return jnp.sum(m.reshape(2, 16, r, 16, c) * eye, axis=3).reshape(32, r, c)


def _state_vec(re, im):
    return jnp.stack([re.reshape(2, HALF_STATES), im.reshape(2, HALF_STATES)], axis=1).reshape(-1)


BIG = ("ffn1_w_in", "ffn1_w_out", "w_mix_in", "w_glu", "w_mix_out", "ffn2_w_in", "ffn2_w_out")
WEIGHTS = ("ffn1_pre_g", "ffn1_w_in", "ffn1_w_out", "ffn1_post_g", "mix_pre_g", "w_mix_in", "a_re", "a_im", "log_dt",
           "b_re", "b_im", "c_re", "c_im", "d_skip", "w_glu", "b_glu", "w_mix_out", "mix_post_g", "ffn2_pre_g",
           "ffn2_w_in", "ffn2_w_out", "ffn2_post_g")
SMALL = tuple(n for n in WEIGHTS if n not in BIG)
TRANSPOSED = ("ffn1_w_in", "ffn2_w_in")
PACK_COLS = 1024


def _pack(parts):
    flat = jnp.concatenate([p.reshape(-1) for p in parts])
    rows = -(-flat.shape[0] // (8 * PACK_COLS)) * 8
    return jnp.pad(flat, (0, rows * PACK_COLS - flat.shape[0])).reshape(rows, PACK_COLS)


def _unpack(packed, shapes):
    flat, out, off = packed.reshape(-1), [], 0
    for s in shapes:
        n = math.prod(s)
        out.append(flat[off:off + n].reshape(s))
        off += n
    return out


def _gather(names, wb):
    return [wb[n] for n in names], [False] * len(names)


def _ffn_bwd(dy, do, saved, x, pre_g, w_in, w_out4, tag, post=None, dwout_comm=None, carry_dw_in=True):
    h, z, a = saved
    T = x.shape[0]
    dz = ffn_dact(do, w_out4, z, f"{tag}_dact")
    dz8 = dz.reshape(8, T, dz.shape[-1])
    dw_out, extra = mm_tn(a, do, True, False, 4, f"{tag}_dwout", comm=dwout_comm)
    dw_in, (s_out,) = mm_tn(dz8, h, True, False, 8, f"{tag}_dwin", comm=([dw_out.reshape(8, -1, D_MODEL)], [True]))
    outs, s_in = dh_pre_bwd(dz8, w_in, x, pre_g, dy, f"{tag}_dh", comm=([dw_in], [True]) if carry_dw_in else None,
                            post=post, w_transposed=True)
    return outs, (s_in[0] if carry_dw_in else dw_in, s_out), extra


def local_step(x, tgt, sp, wb):
    T = x.shape[0]
    ar, ai, bbr, bbi = _discretise(sp["a_re"], sp["a_im"], sp["log_dt"], sp["b_re"], sp["b_im"])
    powers = [(ar, ai)]
    for _ in range(7):
        pr, pi = powers[-1]
        powers.append((pr * ar - pi * ai, pr * ai + pi * ar))
    zero = jnp.zeros_like(ar)
    rows = [_state_vec(*powers[k - 1]) for k in (1, 2, 4)] + [_state_vec(zero, zero)] * 5
    rows += [_state_vec(pr, pi) for pr, pi in powers]
    rows += [_state_vec(pr, -pi) for pr, pi in reversed(powers)]
    apow = jnp.stack(rows)
    bh = jnp.concatenate([_block_diag(bbr.transpose(0, 2, 1)), _block_diag(bbi.transpose(0, 2, 1))], axis=2)
    ch = jnp.concatenate([_block_diag(sp["c_re"].transpose(0, 2, 1)), _block_diag(-sp["c_im"].transpose(0, 2, 1))], axis=1)
    bh, ch = bh.astype(BF16), ch.astype(BF16)
    dskip = sp["d_skip"].reshape(2, 1, 256)

    w1_in = gather_two_level(wb["ffn1_w_in"], "gather_w1in")
    (h1, z1, a1), (w1_out, w_mi) = ffn_in(
        x, sp["ffn1_pre_g"], w1_in, "ffn1_in", comm=_gather(["ffn1_w_out", "w_mix_in"], wb))
    w1_out4 = w1_out.reshape(4, -1, D_MODEL)
    (o1, x1), (w_glu, w_mo) = mm_acc_norm(
        a1, w1_out4, x, sp["ffn1_post_g"], 0.5, "ffn1_out", comm=_gather(["w_glu", "w_mix_out"], wb))
    w_glu2, w_mo4 = w_glu.reshape(2, 256, 512), w_mo.reshape(4, 256, D_MODEL)
    h2, proj = norm_proj(x1, sp["mix_pre_g"], w_mi, "mix_proj")
    (y_ssm, states), (w2_in,) = ssm_fwd(proj, bh, ch, apow, dskip, "ssm_fwd", comm=_gather(["ffn2_w_in"], wb))
    os_, ls_ = [], []
    for d in DILATIONS:
        (o_d, l_d), got = attn_fwd(proj, d, f"attn_fwd_d{d}",
                                   comm=_gather(["ffn2_w_out"], wb) if d == DILATIONS[-1] else None)
        os_.append(o_d)
        ls_.append(l_d)
    w2_out4 = got[0].reshape(4, -1, D_MODEL)
    cat, lse, lg, mixed, x2 = mix_out(os_, ls_, y_ssm, w_glu2, sp["b_glu"], w_mo4, x1, sp["mix_post_g"], "mix_out")
    (h3, z3, a3), _ = ffn_in(x2, sp["ffn2_pre_g"], w2_in, "ffn2_in")
    (dy3, sq, do3, dg_f2post), _ = mm_acc_norm(a3, w2_out4, x2, sp["ffn2_post_g"], 0.5, "ffn2_out", tgt=tgt)

    (dx2, dg_f2pre, dmixed, dg_mpost), (dw2_in, s_w2out), _ = _ffn_bwd(
        dy3, do3, (h3, z3, a3), x2, sp["ffn2_pre_g"], w2_in, w2_out4, "ffn2", post=(mixed, sp["mix_post_g"], 1.0),
        carry_dw_in=False)
    dcat = mm_nt_b(dmixed, w_mo4, "mix_dcat")
    dw_mo, _ = mm_tn(cat, dmixed, True, False, 4, "mix_dwout")
    dy_ssm, dw_glu, db_glu = glu_bwd(dcat, y_ssm, lg, w_glu2, "glu_bwd")
    (du, da, dbh, dch, dd), (s_wmo, s_wglu, s_w2in) = ssm_bwd(
        dy_ssm, proj, states, bh, ch, apow, dskip, "ssm_bwd",
        comm=([dw_mo.reshape(8, 128, D_MODEL), dw_glu.astype(BF16).reshape(8, 64, 512), dw2_in], [True] * 3))
    dqkv = None
    for d in DILATIONS:
        dqkv = attn_bwd(proj, dcat, cat, lse, dqkv, d, f"attn_bwd_d{d}", du=du if d == DILATIONS[-1] else None)
    dproj = dqkv.reshape(8, T, 256)
    dw_mi = mm_tn_shared_a(h2, dproj, "mix_dwin")
    (dx1, dg_mpre, do1, dg_f1post), (s_wmi,) = dh_pre_bwd(
        dproj, w_mi, x1, sp["mix_pre_g"], dx2, "mix_dh", comm=([dw_mi], [True]), post=(o1, sp["ffn1_post_g"], 0.5))

    da4 = da.reshape(2, 2, HALF_STATES)
    d_ar, d_ai = da4[:, 0].reshape(32, N_STATE), da4[:, 1].reshape(32, N_STATE)
    d_bbr = _diag_blocks(dbh[:, :, :HALF_STATES], 16, N_STATE).transpose(0, 2, 1)
    d_bbi = _diag_blocks(dbh[:, :, HALF_STATES:], 16, N_STATE).transpose(0, 2, 1)
    _, disc_vjp = jax.vjp(_discretise, sp["a_re"], sp["a_im"], sp["log_dt"], sp["b_re"], sp["b_im"])
    g_are, g_aim, g_ldt, g_bre, g_bim = disc_vjp((d_ar, d_ai, d_bbr, d_bbi))
    g_cre = _diag_blocks(dch[:, :HALF_STATES], N_STATE, 16).transpose(0, 2, 1)
    g_cim = -_diag_blocks(dch[:, HALF_STATES:], N_STATE, 16).transpose(0, 2, 1)
    small = {
        "ffn1_pre_g": jnp.zeros((1, D_MODEL), F32), "ffn1_post_g": dg_f1post, "mix_pre_g": dg_mpre, "a_re": g_are,
        "a_im": g_aim, "log_dt": g_ldt, "b_re": g_bre, "b_im": g_bim, "c_re": g_cre, "c_im": g_cim,
        "d_skip": dd.reshape(1, 512), "b_glu": db_glu, "mix_post_g": dg_mpost, "ffn2_pre_g": dg_f2pre,
        "ffn2_post_g": dg_f2post,
    }
    (dx0, dg_f1pre), (s_w1in, s_w1out), (early,) = _ffn_bwd(
        dx1, do1, (h1, z1, a1), x, sp["ffn1_pre_g"], w1_in, w1_out4, "ffn1",
        dwout_comm=([_pack([small[n] for n in SMALL])], [False]))
    late = gather_two_level(dg_f1pre, "exchange_small")
    small_slots = lax.dynamic_update_slice(early, late, (0, 0, 0))
    big_slots = {"ffn1_w_in": s_w1in, "ffn1_w_out": s_w1out, "w_mix_in": s_wmi, "w_glu": s_wglu, "w_mix_out": s_wmo,
                 "ffn2_w_in": s_w2in, "ffn2_w_out": s_w2out}
    return sq, dx0, big_slots, small_slots


def kernel(x, ffn1_pre_g, ffn1_w_in, ffn1_w_out, ffn1_post_g, mix_pre_g, w_mix_in, a_re, a_im, log_dt, b_re, b_im, c_re, c_im, d_skip, w_glu, b_glu, w_mix_out, mix_post_g, ffn2_pre_g, ffn2_w_in, ffn2_w_out, ffn2_post_g, loss_target, m_ffn1_pre_g, m_ffn1_w_in, m_ffn1_w_out, m_ffn1_post_g, m_mix_pre_g, m_w_mix_in, m_a_re, m_a_im, m_log_dt, m_b_re, m_b_im, m_c_re, m_c_im, m_d_skip, m_w_glu, m_b_glu, m_w_mix_out, m_mix_post_g, m_ffn2_pre_g, m_ffn2_w_in, m_ffn2_w_out, m_ffn2_post_g, v_ffn1_pre_g, v_ffn1_w_in, v_ffn1_w_out, v_ffn1_post_g, v_mix_pre_g, v_w_mix_in, v_a_re, v_a_im, v_log_dt, v_b_re, v_b_im, v_c_re, v_c_im, v_d_skip, v_w_glu, v_b_glu, v_w_mix_out, v_mix_post_g, v_ffn2_pre_g, v_ffn2_w_in, v_ffn2_w_out, v_ffn2_post_g):
    args = dict(locals())
    w = {n: args[n][0] for n in WEIGHTS}
    m = {n: args["m_" + n][0] for n in WEIGHTS}
    v = {n: args["v_" + n][0] for n in WEIGHTS}

    for d in (w, m, v):
        for n in TRANSPOSED:
            d[n] = jnp.swapaxes(d[n], 0, 1)
    wb = {n: w[n].astype(BF16) for n in BIG}
    sp = {n: w[n] for n in SMALL}
    for n in ("ffn1_pre_g", "ffn1_post_g", "mix_pre_g", "mix_post_g", "ffn2_pre_g", "ffn2_post_g", "b_glu", "d_skip"):
        sp[n] = w[n].reshape(1, -1)

    sq, grad_x, big_slots, small_slots = local_step(x[0], loss_target[0], sp, wb)
    loss = lax.psum(0.5 / D_MODEL * jnp.sum(sq), ("x", "y", "c"))

    outs = {}
    for n in BIG:
        shp = w[n].shape
        r2 = lambda t: t.reshape(-1, shp[-1])
        res = adamw(r2(w[n]), r2(m[n]), r2(v[n]), big_slots[n].reshape(N_DEV, -1, shp[-1]), f"adamw_{n}")
        outs[n] = [(jnp.swapaxes(t, 0, 1) if n in TRANSPOSED else t.reshape(shp))[None] for t in res]
    res = adamw(_pack([w[n] for n in SMALL]), _pack([m[n] for n in SMALL]), _pack([v[n] for n in SMALL]),
                small_slots, "adamw_small")
    shapes = [(1,) + w[n].shape for n in SMALL]
    unpacked = [_unpack(t, shapes) for t in res]
    for j, n in enumerate(SMALL):
        outs[n] = [unpacked[k][j] for k in range(4)]

    result = [loss, grad_x[None]]
    for k in range(4):
        result += [outs[n][k] for n in WEIGHTS]
    return tuple(result)
```

```python
import functools
import math

import jax
import jax.numpy as jnp
from jax import lax
from jax.experimental import pallas as pl
from jax.experimental.pallas import tpu as pltpu

F32, BF16 = jnp.float32, jnp.bfloat16
SDS = jax.ShapeDtypeStruct

D_MODEL = 1024
N_DEV = 8
HEAD_DIM = 64
PAIR_W = 128
QBLK = 128
DILATIONS = (1, 4, 16)
N_STATE = 64
HALF_STATES = 1024
NORM_EPS = 1e-6
NEG = -1e30
VMEM_LIMIT = 56 * 1024 * 1024
ADAM_LR, ADAM_B1, ADAM_B2, ADAM_EPS, ADAM_WD, ADAM_STEP = 1e-3, 0.9, 0.999, 1e-8, 0.01, 10
SCAN_TM = 512
SCAN_CW = 512


def _params(n_grid):
    return pltpu.CompilerParams(dimension_semantics=("arbitrary",) * n_grid, vmem_limit_bytes=VMEM_LIMIT)


def _dot(a, b):
    return jnp.dot(a, b, preferred_element_type=F32)


def _dot_nt(a, b):
    return lax.dot_general(a, b, (((1,), (1,)), ((), ())), preferred_element_type=F32)


def _dot_tn(a, b):
    return lax.dot_general(a, b, (((0,), (0,)), ((), ())), preferred_element_type=F32)


def _sigmoid(v):
    return 0.5 * jnp.tanh(0.5 * v) + 0.5


def _resident(shape):
    return pl.BlockSpec(shape, lambda i: (0,) * len(shape), pipeline_mode=pl.Buffered(1))


ROW_SPLIT = 2


def _exchange_phase(ins, outs, scatter, sems, start):
    send_sems, recv_sems, loc_sems = sems
    x, y, c = lax.axis_index("x"), lax.axis_index("y"), lax.axis_index("c")
    me = 4 * x + 2 * y + c
    own_copies, sends, arrivals = [], [], []
    for i in range(len(ins)):
        own = ins[i].at[me] if scatter[i] else ins[i]
        own_copies.append(pltpu.make_async_copy(own, outs[i].at[me], loc_sems.at[i]))
        for k in range(1, N_DEV):
            px = 1 - x if k & 4 else x
            py = 1 - y if k & 2 else y
            pc = 1 - c if k & 1 else c
            peer = 4 * px + 2 * py + pc
            src = ins[i].at[peer] if scatter[i] else ins[i]
            common = dict(src_ref=src, send_sem=send_sems.at[i, k - 1], recv_sem=recv_sems.at[i, k - 1],
                          device_id=(px, py, pc), device_id_type=pl.DeviceIdType.MESH)
            sends.append(pltpu.make_async_remote_copy(dst_ref=outs[i].at[me], **common))
            if not start:
                arrivals.append(pltpu.make_async_remote_copy(dst_ref=outs[i].at[peer], **common))
    if start:
        for cp in own_copies + sends:
            cp.start()
    else:
        for cp in arrivals:
            cp.wait_recv()
        for cp in sends:
            cp.wait_send()
        for cp in own_copies:
            cp.wait()


def _comm_shapes(arrs, scatter):
    n = len(arrs)
    out_shapes = [SDS(a.shape if scatter[i] else (N_DEV,) + a.shape, a.dtype) for i, a in enumerate(arrs)]
    sems = [pltpu.SemaphoreType.DMA((n, N_DEV - 1)), pltpu.SemaphoreType.DMA((n, N_DEV - 1)),
            pltpu.SemaphoreType.DMA((n,))]
    return out_shapes, sems


def gather_two_level(arr, name):
    def body(x_ref, out_ref, send_sems, recv_sems, local_sem):
        x, y, c = lax.axis_index("x"), lax.axis_index("y"), lax.axis_index("c")
        sibling = (x, y, 1 - c)
        chips = [(1 - x, y), (x, 1 - y), (1 - x, 1 - y)]

        def slot(px, py, pc):
            return out_ref.at[4 * px + 2 * py + pc]

        def copy(k, block, to, src=None):
            return pltpu.make_async_remote_copy(
                src_ref=slot(*block) if src is None else src, dst_ref=slot(*block),
                send_sem=send_sems.at[k], recv_sem=recv_sems.at[k], device_id=to, device_id_type=pl.DeviceIdType.MESH)

        mine = pltpu.make_async_copy(x_ref, slot(x, y, c), local_sem)
        mine.start()
        first = [copy(0, (x, y, c), sibling, src=x_ref)]
        first += [copy(1 + j, (x, y, c), (*chip, c), src=x_ref) for j, chip in enumerate(chips)]
        for cp in first:
            cp.start()
        passed = [copy(4 + j, (*chip, c), sibling) for j, chip in enumerate(chips)]
        for j, chip in enumerate(chips):
            copy(1 + j, (*chip, c), (x, y, c)).wait_recv()
            passed[j].start()
        copy(0, sibling, (x, y, c)).wait_recv()
        for j, chip in enumerate(chips):
            copy(4 + j, (*chip, 1 - c), (x, y, c)).wait_recv()
        for cp in first + passed:
            cp.wait_send()
        mine.wait()

    anyspec = pl.BlockSpec(memory_space=pl.ANY)
    return pl.pallas_call(
        body, in_specs=[anyspec], out_specs=anyspec, out_shape=SDS((N_DEV,) + arr.shape, arr.dtype),
        scratch_shapes=[pltpu.SemaphoreType.DMA((N_DEV - 1,)), pltpu.SemaphoreType.DMA((N_DEV - 1,)),
                        pltpu.SemaphoreType.DMA],
        compiler_params=pltpu.CompilerParams(has_side_effects=True), name=name)(arr)


def _call(body, *, grid, in_specs, out_specs, out_shape, args, name, scratch_shapes=(), comm=None):
    n_grid, scratch_shapes = len(grid), list(scratch_shapes)
    if comm is None:
        outs = pl.pallas_call(body, grid=grid, in_specs=in_specs, out_specs=out_specs, out_shape=out_shape,
                              scratch_shapes=scratch_shapes, compiler_params=_params(n_grid), name=name)(*args)
        return outs, []
    arrs, scatter = comm
    nc, n_in, n_out, n_sc = len(arrs), len(in_specs), len(out_specs), len(scratch_shapes)
    comm_shapes, sems = _comm_shapes(arrs, scatter)

    def wrapped(*refs):
        ins, cins = refs[:n_in], refs[n_in:n_in + nc]
        o0 = n_in + nc
        outs, couts = refs[o0:o0 + n_out], refs[o0 + n_out:o0 + n_out + nc]
        s0 = o0 + n_out + nc
        scratch, sem_refs = refs[s0:s0 + n_sc], refs[s0 + n_sc:]
        first = functools.reduce(jnp.logical_and, [pl.program_id(k) == 0 for k in range(n_grid)])
        last = functools.reduce(jnp.logical_and, [pl.program_id(k) == grid[k] - 1 for k in range(n_grid)])

        @pl.when(first)
        def _():
            _exchange_phase(cins, couts, scatter, sem_refs, True)

        body(*ins, *outs, *scratch)

        @pl.when(last)
        def _():
            _exchange_phase(cins, couts, scatter, sem_refs, False)

    anyspec = pl.BlockSpec(memory_space=pl.ANY)
    res = pl.pallas_call(
        wrapped, grid=grid, in_specs=list(in_specs) + [anyspec] * nc, out_specs=list(out_specs) + [anyspec] * nc,
        out_shape=list(out_shape) + comm_shapes, scratch_shapes=scratch_shapes + sems,
        compiler_params=pltpu.CompilerParams(dimension_semantics=("arbitrary",) * n_grid,
                                             vmem_limit_bytes=VMEM_LIMIT, has_side_effects=True),
        name=name)(*args, *arrs)
    return res[:n_out], res[n_out:]


def _rms(xv, g):
    r = lax.rsqrt(jnp.mean(xv * xv, axis=-1, keepdims=True) + NORM_EPS)
    return (xv * r * g).astype(BF16)


def ffn_in(x, g, w, name, comm=None):
    T, D = x.shape
    F = w.shape[1]
    tm = 512

    def body(x_ref, g_ref, w_ref, h_ref, z_ref, a_ref):
        hv = _rms(x_ref[...], g_ref[...])
        h_ref[...] = hv
        pending = None
        for j in range(5):
            if j < 4:
                zs = (_dot_nt(hv, w_ref[j]), _dot_nt(hv, w_ref[j + 4]))
            if pending is not None:
                zg, zu = pending
                sg = _sigmoid(zg)
                silu = zg * sg
                z_ref[0, j - 1] = (zu * (sg + silu - silu * sg)).astype(BF16)
                z_ref[1, j - 1] = silu.astype(BF16)
                a_ref[j - 1] = (silu * zu).astype(BF16)
            pending = zs

    return _call(
        body, grid=(T // tm,),
        in_specs=[pl.BlockSpec((tm, D), lambda i: (i, 0)), pl.BlockSpec((1, D), lambda i: (0, 0)),
                  _resident((8, F, D))],
        out_specs=[pl.BlockSpec((tm, D), lambda i: (i, 0)), pl.BlockSpec((2, 4, tm, F), lambda i: (0, 0, i, 0)),
                   pl.BlockSpec((4, tm, F), lambda i: (0, i, 0))],
        out_shape=[SDS((T, D), BF16), SDS((2, 4, T, F), BF16), SDS((4, T, F), BF16)],
        args=(x, g, w), name=name, comm=comm)


def norm_proj(x, g, w, name):
    T, K = x.shape
    nb, _, N = w.shape
    tm = 512

    def body(x_ref, g_ref, w_ref, h_ref, o_ref):
        hv = _rms(x_ref[...], g_ref[...])
        h_ref[...] = hv
        for b in range(nb):
            o_ref[b] = _dot(hv, w_ref[b])

    return pl.pallas_call(
        body, grid=(T // tm,),
        in_specs=[pl.BlockSpec((tm, K), lambda i: (i, 0)), pl.BlockSpec((1, K), lambda i: (0, 0)),
                  _resident((nb, K, N))],
        out_specs=[pl.BlockSpec((tm, K), lambda i: (i, 0)), pl.BlockSpec((nb, tm, N), lambda i: (0, i, 0))],
        out_shape=[SDS((T, K), BF16), SDS((nb, T, N), F32)], compiler_params=_params(1), name=name)(x, g, w)


def mm_acc_norm(a, w, xres, g, scale, name, comm=None, tgt=None):
    nb, T, K = a.shape
    D = w.shape[2]
    with_loss = tgt is not None
    tm = 512 if with_loss else 1024
    rc = tm // ROW_SPLIT

    def body(a_ref, w_ref, x_ref, g_ref, *rest):
        if with_loss:
            t_ref, dy_ref, sq_ref, do_ref, dg_ref = rest

            @pl.when(pl.program_id(0) == 0)
            def _():
                sq_ref[...] = jnp.zeros_like(sq_ref)
                dg_ref[...] = jnp.zeros_like(dg_ref)
        else:
            o_ref, y_ref = rest
        accs = []
        for c in range(ROW_SPLIT):
            rows = pl.ds(c * rc, rc)
            o = _dot(a_ref[0, rows, :].astype(BF16), w_ref[0])
            for b in range(1, nb):
                o += _dot(a_ref[b, rows, :].astype(BF16), w_ref[b])
            accs.append(o)
        for c, o in enumerate(accs):
            rows = pl.ds(c * rc, rc)
            r = lax.rsqrt(jnp.mean(o * o, axis=-1, keepdims=True) + NORM_EPS)
            y = x_ref[rows, :] + scale * (o * r * g_ref[...])
            if with_loss:
                e = y - t_ref[rows, :]
                dy = e * (1.0 / D)
                dy_ref[rows, :] = dy
                sq_ref[...] += jnp.sum(e * e, axis=0, keepdims=True)
                do, dg = _post_bwd(dy, o, g_ref[...], scale)
                do_ref[rows, :] = do
                dg_ref[...] += dg
            else:
                o_ref[rows, :] = o
                y_ref[rows, :] = y

    tile = pl.BlockSpec((tm, D), lambda i: (i, 0))
    row = pl.BlockSpec((1, D), lambda i: (0, 0))
    in_specs = [pl.BlockSpec((nb, tm, K), lambda i: (0, i, 0)), _resident((nb, K, D)), tile, row]
    args = (a, w, xres, g)
    if with_loss:
        return _call(body, grid=(T // tm,), in_specs=in_specs + [tile], out_specs=[tile, row, tile, row],
                     out_shape=[SDS((T, D), F32), SDS((1, D), F32), SDS((T, D), BF16), SDS((1, D), F32)],
                     args=args + (tgt,), name=name, comm=comm)
    return _call(body, grid=(T // tm,), in_specs=in_specs, out_specs=[tile, tile],
                 out_shape=[SDS((T, D), F32), SDS((T, D), F32)], args=args, name=name, comm=comm)


def _post_bwd(dy, ov, g, scale):
    r = scale * dy
    rstd = lax.rsqrt(jnp.mean(ov * ov, axis=-1, keepdims=True) + NORM_EPS)
    oh = ov * rstd
    rg = r * g
    do = rstd * (rg - oh * jnp.mean(rg * oh, axis=-1, keepdims=True))
    return do.astype(BF16), jnp.sum(r * oh, axis=0, keepdims=True)


def mm_nt_b(gr, w, name):
    T, N = gr.shape
    nb, K, _ = w.shape
    tm = 512

    def body(g_ref, w_ref, o_ref):
        gv = g_ref[...]
        for b in range(nb):
            o_ref[b] = _dot_nt(gv, w_ref[b])

    return pl.pallas_call(
        body, grid=(T // tm,),
        in_specs=[pl.BlockSpec((tm, N), lambda i: (i, 0)), _resident((nb, K, N))],
        out_specs=pl.BlockSpec((nb, tm, K), lambda i: (0, i, 0)),
        out_shape=SDS((nb, T, K), F32), compiler_params=_params(1), name=name)(gr, w)


def ffn_dact(do, w_out, z, name):
    T, D = do.shape
    nb, F, _ = w_out.shape
    tm = 512

    def body(g_ref, w_ref, z_ref, dz_ref):
        gv = g_ref[...]
        pending = None
        for b in range(nb + 1):
            da = _dot_nt(gv, w_ref[b]) if b < nb else None
            if pending is not None:
                dz_ref[0, b - 1] = (pending * z_ref[0, b - 1].astype(F32)).astype(BF16)
                dz_ref[1, b - 1] = (pending * z_ref[1, b - 1].astype(F32)).astype(BF16)
            pending = da

    blk = pl.BlockSpec((2, nb, tm, F), lambda i: (0, 0, i, 0))
    return pl.pallas_call(
        body, grid=(T // tm,),
        in_specs=[pl.BlockSpec((tm, D), lambda i: (i, 0)), _resident((nb, F, D)), blk],
        out_specs=blk, out_shape=SDS((2, nb, T, F), BF16), compiler_params=_params(1), name=name)(do, w_out, z)


def mm_tn(a, g, a_batched, g_batched, nb, name, comm=None):
    T = a.shape[-2]
    K, N = a.shape[-1], g.shape[-1]
    tk = 2048
    nk = T // tk

    def body(a_ref, g_ref, o_ref, acc):
        k = pl.program_id(1)

        @pl.when(k == 0)
        def _():
            acc[...] = jnp.zeros_like(acc)

        acc[...] += _dot_tn(a_ref[...].astype(BF16), g_ref[...].astype(BF16))

        @pl.when(k == nk - 1)
        def _():
            o_ref[...] = acc[...].astype(BF16)

    a_spec = (pl.BlockSpec((None, tk, K), lambda b, k: (b, k, 0)) if a_batched
              else pl.BlockSpec((tk, K), lambda b, k: (k, 0)))
    g_spec = (pl.BlockSpec((None, tk, N), lambda b, k: (b, k, 0)) if g_batched
              else pl.BlockSpec((tk, N), lambda b, k: (k, 0)))
    (out,), slots = _call(
        body, grid=(nb, nk), in_specs=[a_spec, g_spec],
        out_specs=[pl.BlockSpec((None, K, N), lambda b, k: (b, 0, 0))],
        out_shape=[SDS((nb, K, N), BF16)], scratch_shapes=[pltpu.VMEM((K, N), F32)],
        args=(a, g), name=name, comm=comm)
    return out, slots


def mm_tn_shared_a(a, g, name):
    T, K = a.shape
    nb, _, N = g.shape
    tk = 1024
    nk = T // tk

    def body(a_ref, g_ref, o_ref, acc):
        k = pl.program_id(0)

        @pl.when(k == 0)
        def _():
            acc[...] = jnp.zeros_like(acc)

        av = a_ref[...]
        for b in range(nb):
            acc[b] += _dot_tn(av, g_ref[b].astype(BF16))

        @pl.when(k == nk - 1)
        def _():
            o_ref[...] = acc[...].astype(BF16)

    return pl.pallas_call(
        body, grid=(nk,),
        in_specs=[pl.BlockSpec((tk, K), lambda k: (k, 0)), pl.BlockSpec((nb, tk, N), lambda k: (0, k, 0))],
        out_specs=_resident((nb, K, N)), out_shape=SDS((nb, K, N), BF16),
        scratch_shapes=[pltpu.VMEM((nb, K, N), F32)], compiler_params=_params(1), name=name)(a, g)


def dh_pre_bwd(dz, w, x, g, dyres, name, comm=None, post=None, w_transposed=False):
    nb, T, F = dz.shape
    D = x.shape[1]
    tm = 512
    rc = tm // ROW_SPLIT
    mm = _dot if w_transposed else _dot_nt

    def body(dz_ref, w_ref, x_ref, g_ref, dy_ref, *rest):
        if post is None:
            dx_ref, dg_ref = rest
        else:
            o_ref, gp_ref, dx_ref, dg_ref, do_ref, dgp_ref = rest

        @pl.when(pl.program_id(0) == 0)
        def _():
            dg_ref[...] = jnp.zeros_like(dg_ref)
            if post is not None:
                dgp_ref[...] = jnp.zeros_like(dgp_ref)

        accs = []
        for c in range(ROW_SPLIT):
            rows = pl.ds(c * rc, rc)
            dh = mm(dz_ref[0, rows, :].astype(BF16), w_ref[0])
            for b in range(1, nb):
                dh += mm(dz_ref[b, rows, :].astype(BF16), w_ref[b])
            accs.append(dh)
        for c, dh in enumerate(accs):
            rows = pl.ds(c * rc, rc)
            xv = x_ref[rows, :]
            rstd = lax.rsqrt(jnp.mean(xv * xv, axis=-1, keepdims=True) + NORM_EPS)
            xh = xv * rstd
            dg_ref[...] += jnp.sum(dh * xh, axis=0, keepdims=True)
            dhg = dh * g_ref[...]
            dx = dy_ref[rows, :] + rstd * (dhg - xh * jnp.mean(dhg * xh, axis=-1, keepdims=True))
            dx_ref[rows, :] = dx
            if post is not None:
                do, dgp = _post_bwd(dx, o_ref[rows, :], gp_ref[...], post[2])
                do_ref[rows, :] = do
                dgp_ref[...] += dgp

    tile = pl.BlockSpec((tm, D), lambda i: (i, 0))
    row = pl.BlockSpec((1, D), lambda i: (0, 0))
    in_specs = [pl.BlockSpec((nb, tm, F), lambda i: (0, i, 0)), _resident(w.shape), tile, row, tile]
    out_specs, out_shape, args = [tile, row], [SDS((T, D), F32), SDS((1, D), F32)], (dz, w, x, g, dyres)
    if post is not None:
        in_specs += [tile, row]
        out_specs += [tile, row]
        out_shape += [SDS((T, D), BF16), SDS((1, D), F32)]
        args += (post[0], post[1])
    return _call(body, grid=(T // tm,), in_specs=in_specs, out_specs=out_specs, out_shape=out_shape,
                 args=args, name=name, comm=comm)


ATTN_GROUP = {1: 8, 4: 2, 16: 1}
ATTN_GROUP_BWD = {1: 16, 4: 4, 16: 1}
ATTN_UNROLL = 4


def _attn_masks():
    qi = lax.broadcasted_iota(jnp.int32, (QBLK, QBLK), 0)
    kj = lax.broadcasted_iota(jnp.int32, (QBLK, QBLK), 1)
    cur_ok = kj <= qi
    prev_ok = kj >= qi
    dcur = (qi - kj).astype(F32)
    return cur_ok, prev_ok, dcur, dcur + float(QBLK)


def _head_slopes(p, d):
    out = []
    for hq in range(2):
        v = [float(d) * 2.0 ** -(2 * q + hq + 1) for q in range(4)]
        out.append(jnp.where(p == 0, v[0], jnp.where(p == 1, v[1], jnp.where(p == 2, v[2], v[3]))))
    return out


def _rows(start, d):
    return pl.ds(start, QBLK, stride=d) if d > 1 else pl.ds(start, QBLK)


def _pair_spec(rows, part, blk):
    return pl.BlockSpec((None, rows, PAIR_W), lambda p, n: (2 * part + p // 2, blk(n), p % 2))


def _for_query_blocks(d, groups, several):
    blocks = [(g, r) for r in range(d) for g in range(groups)]
    for s in range(0, len(blocks), ATTN_UNROLL):
        several(blocks[s:s + ATTN_UNROLL])


def attn_fwd(proj, d, name, comm=None):
    T = proj.shape[1]
    sb, groups = QBLK * d, ATTN_GROUP[d]
    rb = sb * groups
    nblk = T // rb

    def body(q_ref, kc_ref, kp_ref, vc_ref, vp_ref, o_ref, l_ref):
        p, n = pl.program_id(0), pl.program_id(1)
        cur_ok, prev_ok, dcur, dprev = _attn_masks()
        first_ok = jnp.logical_and(prev_ok, n > 0)
        lane_head = lax.broadcasted_iota(jnp.int32, (QBLK, PAIR_W), 1) // HEAD_DIM
        slopes = _head_slopes(p, d)

        def several(blocks):
            work = []
            for g, r in blocks:
                rows = _rows(g * sb + r, d)
                q = q_ref[rows, :]
                kc, vc = kc_ref[rows, :].astype(BF16), vc_ref[rows, :].astype(BF16)
                if g == 0:
                    prow, pok = _rows(r, d), first_ok
                    kp, vp = kp_ref[prow, :].astype(BF16), vp_ref[prow, :].astype(BF16)
                else:
                    prow, pok = _rows((g - 1) * sb + r, d), prev_ok
                    kp, vp = kc_ref[prow, :].astype(BF16), vc_ref[prow, :].astype(BF16)
                for hq in range(2):
                    qm = jnp.where(lane_head == hq, q, 0.0).astype(BF16)
                    work.append([rows, hq, pok, vc, vp, _dot_nt(qm, kc), _dot_nt(qm, kp)])
            for w in work:
                _, hq, pok, _, _, sc, sp = w
                sc = jnp.where(cur_ok, sc * 0.125 - slopes[hq] * dcur, NEG)
                sp = jnp.where(pok, sp * 0.125 - slopes[hq] * dprev, NEG)
                m = jnp.maximum(jnp.max(sc, axis=1, keepdims=True), jnp.max(sp, axis=1, keepdims=True))
                pc = jnp.exp(sc - m)
                pp = jnp.exp(sp - m)
                den = jnp.sum(pc, axis=1, keepdims=True) + jnp.sum(pp, axis=1, keepdims=True)
                w[5:] = [pc.astype(BF16), pp.astype(BF16), 1.0 / den, m + jnp.log(den)]
            for i in range(0, len(work), 2):
                o_acc = jnp.zeros((QBLK, PAIR_W), F32)
                l_acc = jnp.zeros((QBLK, PAIR_W), F32)
                for rows, hq, _, vc, vp, pc, pp, inv, lse in work[i:i + 2]:
                    hm = lane_head == hq
                    o_acc = jnp.where(hm, (_dot(pc, vc) + _dot(pp, vp)) * inv, o_acc)
                    l_acc = jnp.where(hm, lse, l_acc)
                o_ref[rows, :] = o_acc
                l_ref[rows, :] = l_acc

        _for_query_blocks(d, groups, several)

    cur = lambda part: _pair_spec(rb, part, lambda n: n)
    prv = lambda part: _pair_spec(sb, part, lambda n: jnp.maximum(n * groups - 1, 0))
    return _call(
        body, grid=(4, nblk), in_specs=[cur(0), cur(1), prv(1), cur(2), prv(2)], out_specs=[cur(0), cur(0)],
        out_shape=[SDS((2, T, 2 * PAIR_W), F32), SDS((2, T, 2 * PAIR_W), F32)],
        args=(proj, proj, proj, proj, proj), name=name, comm=comm)


def mix_out(os_, ls_, y_ssm, w_glu, b_glu, w, xres, g, name):
    _, T, HW = y_ssm.shape
    D = w.shape[2]
    tm = 512

    def body(o1, o2, o3, l1, l2, l3, s_ref, wg_ref, bg_ref, w_ref, x_ref, g_ref, cat_ref, l_ref, lg_ref, m_ref, y_ref):
        y0, _ = _gelu(s_ref[0])
        y1, _ = _gelu(s_ref[1])
        lg = _dot(y0.astype(BF16), wg_ref[0]) + _dot(y1.astype(BF16), wg_ref[1]) + bg_ref[...]
        a, b, c = l1[...], l2[...], l3[...]
        m = jnp.maximum(jnp.maximum(a, b), c)
        ea, eb, ec = jnp.exp(a - m), jnp.exp(b - m), jnp.exp(c - m)
        s = ea + eb + ec
        att = (ea * o1[...] + eb * o2[...] + ec * o3[...]) * (1.0 / s)
        sg = _sigmoid(lg)
        ssm0, ssm1 = y0 * sg[:, :HW], y1 * sg[:, HW:]
        cat_ref[pl.ds(0, 2)] = att
        cat_ref[2] = ssm0
        cat_ref[3] = ssm1
        l_ref[...] = m + jnp.log(s)
        lg_ref[0] = lg[:, :HW]
        lg_ref[1] = lg[:, HW:]
        o = _dot(att[0].astype(BF16), w_ref[0]) + _dot(att[1].astype(BF16), w_ref[1])
        o += _dot(ssm0.astype(BF16), w_ref[2]) + _dot(ssm1.astype(BF16), w_ref[3])
        r = lax.rsqrt(jnp.mean(o * o, axis=-1, keepdims=True) + NORM_EPS)
        m_ref[...] = o
        y_ref[...] = x_ref[...] + o * r * g_ref[...]

    spec = pl.BlockSpec((2, tm, HW), lambda i: (0, i, 0))
    tile = pl.BlockSpec((tm, D), lambda i: (i, 0))
    return pl.pallas_call(
        body, grid=(T // tm,),
        in_specs=[spec] * 7 + [_resident(w_glu.shape), _resident(b_glu.shape), _resident(w.shape), tile,
                               pl.BlockSpec((1, D), lambda i: (0, 0))],
        out_specs=[pl.BlockSpec((4, tm, HW), lambda i: (0, i, 0)), spec, spec, tile, tile],
        out_shape=[SDS((4, T, HW), F32), SDS((2, T, HW), F32), SDS((2, T, HW), F32), SDS((T, D), F32),
                   SDS((T, D), F32)],
        compiler_params=_params(1), name=name)(*os_, *ls_, y_ssm, w_glu, b_glu, w, xres, g)


def attn_bwd(proj, dcat, o, lse, acc, d, name, du=None):
    T = proj.shape[1]
    sb, groups = QBLK * d, ATTN_GROUP_BWD[d]
    rb = sb * groups
    nblk = T // rb
    has_acc = acc is not None
    n_parts = 3 if du is None else 4

    def body(*refs):
        (qc_ref, qn_ref, kc_ref, kp_ref, vc_ref, vp_ref, dc_ref, dn_ref, oc_ref, on_ref, lc_ref, ln_ref) = refs[:12]
        acc_ref = refs[12] if has_acc else None
        out_ref = refs[-1]
        if du is not None:
            out_ref[3] = refs[-2][...]
        p, n = pl.program_id(0), pl.program_id(1)
        cur_ok, prev_ok, dcur, dprev = _attn_masks()
        first_ok = jnp.logical_and(prev_ok, n > 0)
        last_ok = jnp.logical_and(prev_ok, n < nblk - 1)
        lane_head = lax.broadcasted_iota(jnp.int32, (QBLK, PAIR_W), 1) // HEAD_DIM
        slopes = _head_slopes(p, d)

        def one(g, r, shared):
            rows = _rows(g * sb + r, d)
            q_c, do_c, o_c, l_c = qc_ref[rows, :], dc_ref[rows, :], oc_ref[rows, :], lc_ref[rows, :]
            k_c, v_c = kc_ref[rows, :].astype(BF16), vc_ref[rows, :].astype(BF16)
            if shared:
                pok_c, k_p, v_p = prev_ok, None, None
            elif g == 0:
                prow, pok_c = _rows(r, d), first_ok
                k_p, v_p = kp_ref[prow, :].astype(BF16), vp_ref[prow, :].astype(BF16)
            else:
                prow, pok_c = _rows((g - 1) * sb + r, d), prev_ok
                k_p, v_p = kc_ref[prow, :].astype(BF16), vc_ref[prow, :].astype(BF16)
            if g == groups - 1:
                nrow, pok_n = _rows(r, d), last_ok
                q_n, do_n, o_n, l_n = qn_ref[nrow, :], dn_ref[nrow, :], on_ref[nrow, :], ln_ref[nrow, :]
            else:
                nrow, pok_n = _rows((g + 1) * sb + r, d), prev_ok
                q_n, do_n, o_n, l_n = qc_ref[nrow, :], dc_ref[nrow, :], oc_ref[nrow, :], lc_ref[nrow, :]
            heads = []
            for hq in range(2):
                hm = lane_head == hq
                qm_c = jnp.where(hm, q_c, 0.0).astype(BF16)
                qm_n = jnp.where(hm, q_n, 0.0).astype(BF16)
                dom_c = jnp.where(hm, do_c, 0.0)
                dom_n = jnp.where(hm, do_n, 0.0)
                dd_c = jnp.sum(dom_c * o_c, axis=1, keepdims=True)
                dd_n = jnp.sum(dom_n * o_n, axis=1, keepdims=True)
                ls_c = jnp.max(jnp.where(hm, l_c, NEG), axis=1, keepdims=True)
                ls_n = jnp.max(jnp.where(hm, l_n, NEG), axis=1, keepdims=True)
                dob_c, dob_n = dom_c.astype(BF16), dom_n.astype(BF16)
                mm = [(_dot_nt(qm_c, k_c), _dot_nt(dob_c, v_c)),
                      None if shared else (_dot_nt(qm_c, k_p), _dot_nt(dob_c, v_p)),
                      (_dot_nt(qm_n, k_c), _dot_nt(dob_n, v_c))]
                heads.append(dict(hq=hq, qm_c=qm_c, qm_n=qm_n, dob_c=dob_c, dob_n=dob_n, mm=mm,
                                  dd=(dd_c, dd_c, dd_n), ls=(ls_c, ls_c, ls_n)))
            return dict(rows=rows, k_c=k_c, k_p=k_p, heads=heads, oks=(cur_ok, pok_c, pok_n), shared=shared)

        def several(blocks):
            work = []
            for i, (g, r) in enumerate(blocks):
                work.append(one(g, r, i > 0 and blocks[i - 1] == (g - 1, r)))
            for i, w in enumerate(work):
                if w["shared"]:
                    w["k_p"] = work[i - 1]["k_c"]
                for hi, h in enumerate(w["heads"]):
                    slope, dist = slopes[h["hq"]], (dcur, dprev, dprev)
                    h["pr"], h["ds"] = [], []
                    for j in range(3):
                        if h["mm"][j] is None:
                            h["pr"].append(work[i - 1]["heads"][hi]["pr"][2])
                            h["ds"].append(work[i - 1]["heads"][hi]["ds"][2])
                            continue
                        s = jnp.where(w["oks"][j], h["mm"][j][0] * 0.125 - slope * dist[j], NEG)
                        pr = jnp.exp(s - h["ls"][j])
                        h["pr"].append(pr.astype(BF16))
                        h["ds"].append((pr * (h["mm"][j][1] - h["dd"][j])).astype(BF16))
            for w in work:
                dq = jnp.zeros((QBLK, PAIR_W), F32)
                dk = jnp.zeros((QBLK, PAIR_W), F32)
                dv = jnp.zeros((QBLK, PAIR_W), F32)
                for h in w["heads"]:
                    ds, pr = h["ds"], h["pr"]
                    dq_h = _dot(ds[0], w["k_c"]) + _dot(ds[1], w["k_p"])
                    dk += (_dot_tn(ds[0], h["qm_c"]) + _dot_tn(ds[2], h["qm_n"])) * 0.125
                    dv += _dot_tn(pr[0], h["dob_c"]) + _dot_tn(pr[2], h["dob_n"])
                    dq = jnp.where(lane_head == h["hq"], dq_h * 0.125, dq)
                for part, val in enumerate((dq, dk, dv)):
                    if has_acc:
                        val = val + acc_ref.at[part][w["rows"], :]
                    out_ref.at[part][w["rows"], :] = val

        _for_query_blocks(d, groups, several)

    cur = lambda part: _pair_spec(rb, part, lambda n: n)
    prv = lambda part: _pair_spec(sb, part, lambda n: jnp.maximum(n * groups - 1, 0))
    nxt = lambda part: _pair_spec(sb, part, lambda n: jnp.minimum((n + 1) * groups, T // sb - 1))
    full = pl.BlockSpec((3, None, rb, PAIR_W), lambda p, n: (0, p // 2, n, p % 2))
    in_specs = [cur(0), nxt(0), cur(1), prv(1), cur(2), prv(2), cur(0), nxt(0), cur(0), nxt(0), cur(0), nxt(0)]
    args = [proj, proj, proj, proj, proj, proj, dcat, dcat, o, o, lse, lse]
    if has_acc:
        in_specs.append(full)
        args.append(acc)
    if du is not None:
        in_specs.append(cur(0))
        args.append(du)
    out_spec = pl.BlockSpec((n_parts, None, rb, PAIR_W), lambda p, n: (0, p // 2, n, p % 2))
    return pl.pallas_call(
        body, grid=(4, nblk), in_specs=in_specs, out_specs=out_spec,
        out_shape=SDS((n_parts, 2, T, 2 * PAIR_W), F32), compiler_params=_params(2), name=name)(*args)


def _scan_rows(buf, tab_ref, reverse, half):
    n_tiles = (buf.shape[0] - 8) // 8
    per_half = HALF_STATES // SCAN_CW
    row = lax.broadcasted_iota(jnp.int32, (8, SCAN_CW), 0)
    sgn = -1.0 if reverse else 1.0

    for j in range(per_half):
        c0 = half * 2 * HALF_STATES + j * SCAN_CW
        cre = pl.ds(c0, SCAN_CW)
        cim = pl.ds(c0 + HALF_STATES, SCAN_CW)
        steps = []
        for s, k in enumerate((1, 2, 4)):
            ok, shift = (row < 8 - k, 8 - k) if reverse else (row >= k, k)
            steps.append((shift, jnp.where(ok, tab_ref[pl.ds(s, 1), cre], 0.0),
                          jnp.where(ok, sgn * tab_ref[pl.ds(s, 1), cim], 0.0)))
        trow = 16 if reverse else 8
        pr, pi = tab_ref[pl.ds(trow, 8), cre], tab_ref[pl.ds(trow, 8), cim]
        for t in range(n_tiles):
            base = 8 * (n_tiles - 1 - t) if reverse else 8 + 8 * t
            rows = pl.ds(base, 8)
            re, im = buf[rows, cre], buf[rows, cim]
            for shift, ar, ai in steps:
                sre, sim = pltpu.roll(re, shift, 0), pltpu.roll(im, shift, 0)
                re, im = re + ar * sre - ai * sim, im + ar * sim + ai * sre
            crow = pl.ds(base + 8 if reverse else base - 1, 1)
            cr, ci = buf[crow, cre], buf[crow, cim]
            buf[rows, cre] = re + pr * cr - pi * ci
            buf[rows, cim] = im + pr * ci + pi * cr


def ssm_fwd(proj, bh, ch, apow, dskip, name, comm=None):
    _, T, C = proj.shape
    tm = SCAN_TM
    SW = 4 * HALF_STATES

    def body(u_ref, bh_ref, ch_ref, tab_ref, dsk_ref, y_ref, s_ref, buf):
        @pl.when(pl.program_id(0) == 0)
        def _():
            buf[pl.ds(0, 8), :] = jnp.zeros((8, SW), F32)

        for h in range(2):
            buf[pl.ds(8, tm), pl.ds(h * 2 * HALF_STATES, 2 * HALF_STATES)] = _dot(u_ref[h].astype(BF16), bh_ref[h])
        for h in range(2):
            cols = pl.ds(h * 2 * HALF_STATES, 2 * HALF_STATES)
            _scan_rows(buf, tab_ref, False, h)
            sv = buf[pl.ds(8, tm), cols]
            s_ref[:, cols] = sv
            y_ref[h] = _dot(sv.astype(BF16), ch_ref[h]) + dsk_ref[h] * u_ref[h]
        buf[pl.ds(0, 8), :] = buf[pl.ds(tm, 8), :]

    return _call(
        body, grid=(T // tm,),
        in_specs=[pl.BlockSpec((2, tm, C), lambda i: (3, i, 0)),
                  _resident((2, C, 2 * HALF_STATES)), _resident((2, 2 * HALF_STATES, C)), _resident((24, SW)),
                  _resident((2, 1, C))],
        out_specs=[pl.BlockSpec((2, tm, C), lambda i: (0, i, 0)), pl.BlockSpec((tm, SW), lambda i: (i, 0))],
        out_shape=[SDS((2, T, C), F32), SDS((T, SW), F32)],
        scratch_shapes=[pltpu.VMEM((tm + 8, SW), F32)],
        args=(proj, bh, ch, apow, dskip), name=name, comm=comm)


def ssm_bwd(dy, proj, st, bh, ch, apow, dskip, name, comm=None):
    _, T, C = proj.shape
    tm = SCAN_TM
    nt = T // tm
    SW = 4 * HALF_STATES
    HS2 = 2 * HALF_STATES

    def body(dy_ref, u_ref, s_ref, sp_ref, bh_ref, ch_ref, tab_ref, dsk_ref,
             du_ref, da_ref, dbh_ref, dch_ref, dd_ref, lam):
        i = pl.program_id(0)

        @pl.when(i == 0)
        def _():
            lam[pl.ds(tm, 8), :] = jnp.zeros((8, SW), F32)
            da_ref[...] = jnp.zeros_like(da_ref)
            dbh_ref[...] = jnp.zeros_like(dbh_ref)
            dch_ref[...] = jnp.zeros_like(dch_ref)
            dd_ref[...] = jnp.zeros_like(dd_ref)

        for h in range(2):
            lam[pl.ds(0, tm), pl.ds(h * HS2, HS2)] = _dot_nt(dy_ref[h].astype(BF16), ch_ref[h])
        for h in range(2):
            dyv, uv = dy_ref[h], u_ref[h]
            dch_ref[h] += _dot_tn(s_ref[:, pl.ds(h * HS2, HS2)].astype(BF16), dyv.astype(BF16))
            dd_ref[h] += jnp.sum(dyv * uv, axis=0, keepdims=True)
        for h in range(2):
            _scan_rows(lam, tab_ref, True, h)
            lb = lam[pl.ds(0, tm), pl.ds(h * HS2, HS2)].astype(BF16)
            du_ref[h] = _dot_nt(lb, bh_ref[h]) + dsk_ref[h] * dy_ref[h]
            dbh_ref[h] += _dot_tn(u_ref[h].astype(BF16), lb)

        first = i == nt - 1
        per_half = HALF_STATES // SCAN_CW

        def chunk(j, _):
            c0 = pl.multiple_of((j // per_half) * HS2 + (j % per_half) * SCAN_CW, 128)
            cre, cim = pl.ds(c0, SCAN_CW), pl.ds(pl.multiple_of(c0 + HALF_STATES, 128), SCAN_CW)
            row0 = lax.broadcasted_iota(jnp.int32, (8, SCAN_CW), 0) == 0
            acc_r = jnp.zeros((8, SCAN_CW), F32)
            acc_i = jnp.zeros((8, SCAN_CW), F32)
            for t in range(tm // 8):
                rows = pl.ds(8 * t, 8)
                if t == 0:
                    pre = jnp.where(first, 0.0, sp_ref[pl.ds(7, 1), cre])
                    pim = jnp.where(first, 0.0, sp_ref[pl.ds(7, 1), cim])
                else:
                    pre, pim = s_ref[pl.ds(8 * t - 1, 1), cre], s_ref[pl.ds(8 * t - 1, 1), cim]
                spr = jnp.where(row0, pre, pltpu.roll(s_ref[rows, cre], 1, 0))
                spi = jnp.where(row0, pim, pltpu.roll(s_ref[rows, cim], 1, 0))
                lr, li = lam[rows, cre], lam[rows, cim]
                acc_r += lr * spr + li * spi
                acc_i += li * spr - lr * spi
            da_ref[:, cre] += jnp.sum(acc_r, axis=0, keepdims=True)
            da_ref[:, cim] += jnp.sum(acc_i, axis=0, keepdims=True)
            return 0

        lax.fori_loop(0, 2 * per_half, chunk, 0)
        lam[pl.ds(tm, 8), :] = lam[pl.ds(0, 8), :]

    rev = lambda i: nt - 1 - i
    return _call(
        body, grid=(nt,),
        in_specs=[pl.BlockSpec((2, tm, C), lambda i: (0, rev(i), 0)),
                  pl.BlockSpec((2, tm, C), lambda i: (3, rev(i), 0)),
                  pl.BlockSpec((tm, SW), lambda i: (rev(i), 0)),
                  pl.BlockSpec((8, SW), lambda i: (jnp.maximum(rev(i) * (tm // 8) - 1, 0), 0)),
                  _resident((2, C, HS2)), _resident((2, HS2, C)), _resident((24, SW)), _resident((2, 1, C))],
        out_specs=[pl.BlockSpec((2, tm, C), lambda i: (0, rev(i), 0)),
                   _resident((1, SW)), _resident((2, C, HS2)), _resident((2, HS2, C)), _resident((2, 1, C))],
        out_shape=[SDS((2, T, C), F32), SDS((1, SW), F32), SDS((2, C, HS2), F32), SDS((2, HS2, C), F32),
                   SDS((2, 1, C), F32)],
        scratch_shapes=[pltpu.VMEM((tm + 8, SW), F32)],
        args=(dy, proj, st, st, bh, ch, apow, dskip), name=name, comm=comm)


_GELU_C = math.sqrt(2.0 / math.pi)


def _gelu(x):
    t = jnp.tanh(_GELU_C * (x + 0.044715 * x * x * x))
    return 0.5 * x * (1.0 + t), t


def glu_bwd(dcat, y, lg, w, name):
    _, T, C = y.shape
    tm = 512

    def body(d_ref, y_ref, lg_ref, w_ref, dy_ref, dw_ref, db_ref):
        @pl.when(pl.program_id(0) == 0)
        def _():
            dw_ref[...] = jnp.zeros_like(dw_ref)
            db_ref[...] = jnp.zeros_like(db_ref)

        y2, th, sg, dlg = [], [], [], []
        for h in range(2):
            yy, tt = _gelu(y_ref[h])
            ss = _sigmoid(lg_ref[h])
            y2.append(yy)
            th.append(tt)
            sg.append(ss)
            dlg.append(d_ref[h] * yy * ss * (1.0 - ss))
        dl = jnp.concatenate(dlg, axis=1)
        dlb = dl.astype(BF16)
        db_ref[...] += jnp.sum(dl, axis=0, keepdims=True)
        for h in range(2):
            dy2 = d_ref[h] * sg[h] + _dot_nt(dlb, w_ref[h])
            yv = y_ref[h]
            dgelu = 0.5 * (1.0 + th[h]) + 0.5 * yv * (1.0 - th[h] * th[h]) * _GELU_C * (1.0 + 3 * 0.044715 * yv * yv)
            dy_ref[h] = dy2 * dgelu
            dw_ref[h] += _dot_tn(y2[h].astype(BF16), dlb)

    return pl.pallas_call(
        body, grid=(T // tm,),
        in_specs=[pl.BlockSpec((2, tm, C), lambda i: (1, i, 0)), pl.BlockSpec((2, tm, C), lambda i: (0, i, 0)),
                  pl.BlockSpec((2, tm, C), lambda i: (0, i, 0)), pl.BlockSpec((2, C, 2 * C), lambda i: (0, 0, 0))],
        out_specs=[pl.BlockSpec((2, tm, C), lambda i: (0, i, 0)), pl.BlockSpec((2, C, 2 * C), lambda i: (0, 0, 0)),
                   pl.BlockSpec((1, 2 * C), lambda i: (0, 0))],
        out_shape=[SDS((2, T, C), F32), SDS((2, C, 2 * C), F32), SDS((1, 2 * C), F32)],
        compiler_params=_params(1), name=name)(dcat, y, lg, w)


def adamw(w, m, v, slots, name):
    R, C = w.shape
    tr = R
    for cand in (512, 256, 128, 64, 32, 16, 8):
        if R % cand == 0 and cand * C * 4 <= 2 * 1024 * 1024:
            tr = cand
            break
    c1 = 1.0 / (1.0 - ADAM_B1 ** ADAM_STEP)
    c2 = 1.0 / (1.0 - ADAM_B2 ** ADAM_STEP)

    def body(w_ref, m_ref, v_ref, s_ref, g_ref, d_ref, nm_ref, nv_ref):
        g = s_ref[0].astype(F32)
        for j in range(1, N_DEV):
            g = g + s_ref[j].astype(F32)
        nm = ADAM_B1 * m_ref[...] + (1.0 - ADAM_B1) * g
        nv = ADAM_B2 * v_ref[...] + (1.0 - ADAM_B2) * (g * g)
        g_ref[...] = g
        nm_ref[...] = nm
        nv_ref[...] = nv
        d_ref[...] = -ADAM_LR * ((nm * c1) / (jnp.sqrt(nv * c2) + ADAM_EPS) + ADAM_WD * w_ref[...])

    spec = pl.BlockSpec((tr, C), lambda i: (i, 0))
    return pl.pallas_call(
        body, grid=(R // tr,),
        in_specs=[spec, spec, spec, pl.BlockSpec((N_DEV, tr, C), lambda i: (0, i, 0))],
        out_specs=[spec] * 4, out_shape=[SDS((R, C), F32)] * 4, compiler_params=_params(1), name=name)(w, m, v, slots)


def _discretise(a_re, a_im, log_dt, b_re, b_im):
    dt = jnp.exp(log_dt)[:, None]
    e = jnp.exp(dt * a_re)
    ar, ai = e * jnp.cos(dt * a_im), e * jnp.sin(dt * a_im)
    den = a_re * a_re + a_im * a_im
    nr, ni = ar - 1.0, ai
    wr = (nr * a_re + ni * a_im) / den
    wi = (ni * a_re - nr * a_im) / den
    bbr = wr[..., None] * b_re - wi[..., None] * b_im
    bbi = wr[..., None] * b_im + wi[..., None] * b_re
    return ar, ai, bbr, bbi


def _block_diag(t):
    eye = jnp.eye(16, dtype=t.dtype).reshape(1, 16, 1, 16, 1)
    r, c = t.shape[1], t.shape[2]
    return (t.reshape(2, 16, r, 1, c) * eye).reshape(2, 16 * r, 16 * c)


def _diag_blocks(m, r, c):
    eye = jnp.eye(16, dtype=m.dtype).reshape(1, 16, 1, 16, 1)
    return jnp.sum(m.reshape(2, 16, r, 16, c) * eye, axis=3).reshape(32, r, c)


def _state_vec(re, im):
    return jnp.stack([re.reshape(2, HALF_STATES), im.reshape(2, HALF_STATES)], axis=1).reshape(-1)


BIG = ("ffn1_w_in", "ffn1_w_out", "w_mix_in", "w_glu", "w_mix_out", "ffn2_w_in", "ffn2_w_out")
WEIGHTS = ("ffn1_pre_g", "ffn1_w_in", "ffn1_w_out", "ffn1_post_g", "mix_pre_g", "w_mix_in", "a_re", "a_im", "log_dt",
           "b_re", "b_im", "c_re", "c_im", "d_skip", "w_glu", "b_glu", "w_mix_out", "mix_post_g", "ffn2_pre_g",
           "ffn2_w_in", "ffn2_w_out", "ffn2_post_g")
SMALL = tuple(n for n in WEIGHTS if n not in BIG)
TRANSPOSED = ("ffn1_w_in", "ffn2_w_in")
PACK_COLS = 1024


def _pack(parts):
    flat = jnp.concatenate([p.reshape(-1) for p in parts])
    rows = -(-flat.shape[0] // (8 * PACK_COLS)) * 8
    return jnp.pad(flat, (0, rows * PACK_COLS - flat.shape[0])).reshape(rows, PACK_COLS)


def _unpack(packed, shapes):
    flat, out, off = packed.reshape(-1), [], 0
    for s in shapes:
        n = math.prod(s)
        out.append(flat[off:off + n].reshape(s))
        off += n
    return out


def _gather(names, wb):
    return [wb[n] for n in names], [False] * len(names)


def _ffn_bwd(dy, do, saved, x, pre_g, w_in, w_out4, tag, post=None, dwout_comm=None, carry_dw_in=True):
    h, z, a = saved
    T = x.shape[0]
    dz = ffn_dact(do, w_out4, z, f"{tag}_dact")
    dz8 = dz.reshape(8, T, dz.shape[-1])
    dw_out, extra = mm_tn(a, do, True, False, 4, f"{tag}_dwout", comm=dwout_comm)
    dw_in, (s_out,) = mm_tn(dz8, h, True, False, 8, f"{tag}_dwin", comm=([dw_out.reshape(8, -1, D_MODEL)], [True]))
    outs, s_in = dh_pre_bwd(dz8, w_in, x, pre_g, dy, f"{tag}_dh", comm=([dw_in], [True]) if carry_dw_in else None,
                            post=post, w_transposed=True)
    return outs, (s_in[0] if carry_dw_in else dw_in, s_out), extra


def local_step(x, tgt, sp, wb):
    T = x.shape[0]
    ar, ai, bbr, bbi = _discretise(sp["a_re"], sp["a_im"], sp["log_dt"], sp["b_re"], sp["b_im"])
    powers = [(ar, ai)]
    for _ in range(7):
        pr, pi = powers[-1]
        powers.append((pr * ar - pi * ai, pr * ai + pi * ar))
    zero = jnp.zeros_like(ar)
    rows = [_state_vec(*powers[k - 1]) for k in (1, 2, 4)] + [_state_vec(zero, zero)] * 5
    rows += [_state_vec(pr, pi) for pr, pi in powers]
    rows += [_state_vec(pr, -pi) for pr, pi in reversed(powers)]
    apow = jnp.stack(rows)
    bh = jnp.concatenate([_block_diag(bbr.transpose(0, 2, 1)), _block_diag(bbi.transpose(0, 2, 1))], axis=2)
    ch = jnp.concatenate([_block_diag(sp["c_re"].transpose(0, 2, 1)), _block_diag(-sp["c_im"].transpose(0, 2, 1))], axis=1)
    bh, ch = bh.astype(BF16), ch.astype(BF16)
    dskip = sp["d_skip"].reshape(2, 1, 256)

    w1_in = gather_two_level(wb["ffn1_w_in"], "gather_w1in")
    (h1, z1, a1), (w1_out, w_mi) = ffn_in(
        x, sp["ffn1_pre_g"], w1_in, "ffn1_in", comm=_gather(["ffn1_w_out", "w_mix_in"], wb))
    w1_out4 = w1_out.reshape(4, -1, D_MODEL)
    (o1, x1), (w_glu, w_mo) = mm_acc_norm(
        a1, w1_out4, x, sp["ffn1_post_g"], 0.5, "ffn1_out", comm=_gather(["w_glu", "w_mix_out"], wb))
    w_glu2, w_mo4 = w_glu.reshape(2, 256, 512), w_mo.reshape(4, 256, D_MODEL)
    h2, proj = norm_proj(x1, sp["mix_pre_g"], w_mi, "mix_proj")
    (y_ssm, states), (w2_in,) = ssm_fwd(proj, bh, ch, apow, dskip, "ssm_fwd", comm=_gather(["ffn2_w_in"], wb))
    os_, ls_ = [], []
    for d in DILATIONS:
        (o_d, l_d), got = attn_fwd(proj, d, f"attn_fwd_d{d}",
                                   comm=_gather(["ffn2_w_out"], wb) if d == DILATIONS[-1] else None)
        os_.append(o_d)
        ls_.append(l_d)
    w2_out4 = got[0].reshape(4, -1, D_MODEL)
    cat, lse, lg, mixed, x2 = mix_out(os_, ls_, y_ssm, w_glu2, sp["b_glu"], w_mo4, x1, sp["mix_post_g"], "mix_out")
    (h3, z3, a3), _ = ffn_in(x2, sp["ffn2_pre_g"], w2_in, "ffn2_in")
    (dy3, sq, do3, dg_f2post), _ = mm_acc_norm(a3, w2_out4, x2, sp["ffn2_post_g"], 0.5, "ffn2_out", tgt=tgt)

    (dx2, dg_f2pre, dmixed, dg_mpost), (dw2_in, s_w2out), _ = _ffn_bwd(
        dy3, do3, (h3, z3, a3), x2, sp["ffn2_pre_g"], w2_in, w2_out4, "ffn2", post=(mixed, sp["mix_post_g"], 1.0),
        carry_dw_in=False)
    dcat = mm_nt_b(dmixed, w_mo4, "mix_dcat")
    dw_mo, _ = mm_tn(cat, dmixed, True, False, 4, "mix_dwout")
    dy_ssm, dw_glu, db_glu = glu_bwd(dcat, y_ssm, lg, w_glu2, "glu_bwd")
    (du, da, dbh, dch, dd), (s_wmo, s_wglu, s_w2in) = ssm_bwd(
        dy_ssm, proj, states, bh, ch, apow, dskip, "ssm_bwd",
        comm=([dw_mo.reshape(8, 128, D_MODEL), dw_glu.astype(BF16).reshape(8, 64, 512), dw2_in], [True] * 3))
    dqkv = None
    for d in DILATIONS:
        dqkv = attn_bwd(proj, dcat, cat, lse, dqkv, d, f"attn_bwd_d{d}", du=du if d == DILATIONS[-1] else None)
    dproj = dqkv.reshape(8, T, 256)
    dw_mi = mm_tn_shared_a(h2, dproj, "mix_dwin")
    (dx1, dg_mpre, do1, dg_f1post), (s_wmi,) = dh_pre_bwd(
        dproj, w_mi, x1, sp["mix_pre_g"], dx2, "mix_dh", comm=([dw_mi], [True]), post=(o1, sp["ffn1_post_g"], 0.5))

    da4 = da.reshape(2, 2, HALF_STATES)
    d_ar, d_ai = da4[:, 0].reshape(32, N_STATE), da4[:, 1].reshape(32, N_STATE)
    d_bbr = _diag_blocks(dbh[:, :, :HALF_STATES], 16, N_STATE).transpose(0, 2, 1)
    d_bbi = _diag_blocks(dbh[:, :, HALF_STATES:], 16, N_STATE).transpose(0, 2, 1)
    _, disc_vjp = jax.vjp(_discretise, sp["a_re"], sp["a_im"], sp["log_dt"], sp["b_re"], sp["b_im"])
    g_are, g_aim, g_ldt, g_bre, g_bim = disc_vjp((d_ar, d_ai, d_bbr, d_bbi))
    g_cre = _diag_blocks(dch[:, :HALF_STATES], N_STATE, 16).transpose(0, 2, 1)
    g_cim = -_diag_blocks(dch[:, HALF_STATES:], N_STATE, 16).transpose(0, 2, 1)
    small = {
        "ffn1_pre_g": jnp.zeros((1, D_MODEL), F32), "ffn1_post_g": dg_f1post, "mix_pre_g": dg_mpre, "a_re": g_are,
        "a_im": g_aim, "log_dt": g_ldt, "b_re": g_bre, "b_im": g_bim, "c_re": g_cre, "c_im": g_cim,
        "d_skip": dd.reshape(1, 512), "b_glu": db_glu, "mix_post_g": dg_mpost, "ffn2_pre_g": dg_f2pre,
        "ffn2_post_g": dg_f2post,
    }
    (dx0, dg_f1pre), (s_w1in, s_w1out), (early,) = _ffn_bwd(
        dx1, do1, (h1, z1, a1), x, sp["ffn1_pre_g"], w1_in, w1_out4, "ffn1",
        dwout_comm=([_pack([small[n] for n in SMALL])], [False]))
    late = gather_two_level(dg_f1pre, "exchange_small")
    small_slots = lax.dynamic_update_slice(early, late, (0, 0, 0))
    big_slots = {"ffn1_w_in": s_w1in, "ffn1_w_out": s_w1out, "w_mix_in": s_wmi, "w_glu": s_wglu, "w_mix_out": s_wmo,
                 "ffn2_w_in": s_w2in, "ffn2_w_out": s_w2out}
    return sq, dx0, big_slots, small_slots


def kernel(x, ffn1_pre_g, ffn1_w_in, ffn1_w_out, ffn1_post_g, mix_pre_g, w_mix_in, a_re, a_im, log_dt, b_re, b_im, c_re, c_im, d_skip, w_glu, b_glu, w_mix_out, mix_post_g, ffn2_pre_g, ffn2_w_in, ffn2_w_out, ffn2_post_g, loss_target, m_ffn1_pre_g, m_ffn1_w_in, m_ffn1_w_out, m_ffn1_post_g, m_mix_pre_g, m_w_mix_in, m_a_re, m_a_im, m_log_dt, m_b_re, m_b_im, m_c_re, m_c_im, m_d_skip, m_w_glu, m_b_glu, m_w_mix_out, m_mix_post_g, m_ffn2_pre_g, m_ffn2_w_in, m_ffn2_w_out, m_ffn2_post_g, v_ffn1_pre_g, v_ffn1_w_in, v_ffn1_w_out, v_ffn1_post_g, v_mix_pre_g, v_w_mix_in, v_a_re, v_a_im, v_log_dt, v_b_re, v_b_im, v_c_re, v_c_im, v_d_skip, v_w_glu, v_b_glu, v_w_mix_out, v_mix_post_g, v_ffn2_pre_g, v_ffn2_w_in, v_ffn2_w_out, v_ffn2_post_g):
    args = dict(locals())
    w = {n: args[n][0] for n in WEIGHTS}
    m = {n: args["m_" + n][0] for n in WEIGHTS}
    v = {n: args["v_" + n][0] for n in WEIGHTS}

    for d in (w, m, v):
        for n in TRANSPOSED:
            d[n] = jnp.swapaxes(d[n], 0, 1)
    wb = {n: w[n].astype(BF16) for n in BIG}
    sp = {n: w[n] for n in SMALL}
    for n in ("ffn1_pre_g", "ffn1_post_g", "mix_pre_g", "mix_post_g", "ffn2_pre_g", "ffn2_post_g", "b_glu", "d_skip"):
        sp[n] = w[n].reshape(1, -1)

    sq, grad_x, big_slots, small_slots = local_step(x[0], loss_target[0], sp, wb)
    loss = lax.psum(0.5 / D_MODEL * jnp.sum(sq), ("x", "y", "c"))

    outs = {}
    for n in BIG:
        shp = w[n].shape
        r2 = lambda t: t.reshape(-1, shp[-1])
        res = adamw(r2(w[n]), r2(m[n]), r2(v[n]), big_slots[n].reshape(N_DEV, -1, shp[-1]), f"adamw_{n}")
        outs[n] = [(jnp.swapaxes(t, 0, 1) if n in TRANSPOSED else t.reshape(shp))[None] for t in res]
    res = adamw(_pack([w[n] for n in SMALL]), _pack([m[n] for n in SMALL]), _pack([v[n] for n in SMALL]),
                small_slots, "adamw_small")
    shapes = [(1,) + w[n].shape for n in SMALL]
    unpacked = [_unpack(t, shapes) for t in res]
    for j, n in enumerate(SMALL):
        outs[n] = [unpacked[k][j] for k in range(4)]

    result = [loss, grad_x[None]]
    for k in range(4):
        result += [outs[n][k] for n in WEIGHTS]
    return tuple(result)
```

```python
import functools
import math

import jax
import jax.numpy as jnp
from jax import lax
from jax.experimental import pallas as pl
from jax.experimental.pallas import tpu as pltpu

F32, BF16 = jnp.float32, jnp.bfloat16
SDS = jax.ShapeDtypeStruct

D_MODEL = 1024
N_DEV = 8
HEAD_DIM = 64
PAIR_W = 128
QBLK = 128
DILATIONS = (1, 4, 16)
N_STATE = 64
HALF_STATES = 1024
NORM_EPS = 1e-6
NEG = -1e30
VMEM_LIMIT = 56 * 1024 * 1024
ADAM_LR, ADAM_B1, ADAM_B2, ADAM_EPS, ADAM_WD, ADAM_STEP = 1e-3, 0.9, 0.999, 1e-8, 0.01, 10
SCAN_TM = 512
SCAN_CW = 512


def _params(n_grid):
    return pltpu.CompilerParams(dimension_semantics=("arbitrary",) * n_grid, vmem_limit_bytes=VMEM_LIMIT)


def _dot(a, b):
    return jnp.dot(a, b, preferred_element_type=F32)


def _dot_nt(a, b):
    return lax.dot_general(a, b, (((1,), (1,)), ((), ())), preferred_element_type=F32)


def _dot_tn(a, b):
    return lax.dot_general(a, b, (((0,), (0,)), ((), ())), preferred_element_type=F32)


def _sigmoid(v):
    return 0.5 * jnp.tanh(0.5 * v) + 0.5


def _resident(shape):
    return pl.BlockSpec(shape, lambda i: (0,) * len(shape), pipeline_mode=pl.Buffered(1))


ROW_SPLIT = 2


def _exchange_phase(ins, outs, scatter, sems, start):
    send_sems, recv_sems, loc_sems = sems
    x, y, c = lax.axis_index("x"), lax.axis_index("y"), lax.axis_index("c")
    me = 4 * x + 2 * y + c
    own_copies, sends, arrivals = [], [], []
    for i in range(len(ins)):
        own = ins[i].at[me] if scatter[i] else ins[i]
        own_copies.append(pltpu.make_async_copy(own, outs[i].at[me], loc_sems.at[i]))
        for k in range(1, N_DEV):
            px = 1 - x if k & 4 else x
            py = 1 - y if k & 2 else y
            pc = 1 - c if k & 1 else c
            peer = 4 * px + 2 * py + pc
            src = ins[i].at[peer] if scatter[i] else ins[i]
            common = dict(src_ref=src, send_sem=send_sems.at[i, k - 1], recv_sem=recv_sems.at[i, k - 1],
                          device_id=(px, py, pc), device_id_type=pl.DeviceIdType.MESH)
            sends.append(pltpu.make_async_remote_copy(dst_ref=outs[i].at[me], **common))
            if not start:
                arrivals.append(pltpu.make_async_remote_copy(dst_ref=outs[i].at[peer], **common))
    if start:
        for cp in own_copies + sends:
            cp.start()
    else:
        for cp in arrivals:
            cp.wait_recv()
        for cp in sends:
            cp.wait_send()
        for cp in own_copies:
            cp.wait()


def _comm_shapes(arrs, scatter):
    n = len(arrs)
    out_shapes = [SDS(a.shape if scatter[i] else (N_DEV,) + a.shape, a.dtype) for i, a in enumerate(arrs)]
    sems = [pltpu.SemaphoreType.DMA((n, N_DEV - 1)), pltpu.SemaphoreType.DMA((n, N_DEV - 1)),
            pltpu.SemaphoreType.DMA((n,))]
    return out_shapes, sems


def gather_two_level(arr, name):
    def body(x_ref, out_ref, send_sems, recv_sems, local_sem):
        x, y, c = lax.axis_index("x"), lax.axis_index("y"), lax.axis_index("c")
        sibling = (x, y, 1 - c)
        chips = [(1 - x, y), (x, 1 - y), (1 - x, 1 - y)]

        def slot(px, py, pc):
            return out_ref.at[4 * px + 2 * py + pc]

        def copy(k, block, to, src=None):
            return pltpu.make_async_remote_copy(
                src_ref=slot(*block) if src is None else src, dst_ref=slot(*block),
                send_sem=send_sems.at[k], recv_sem=recv_sems.at[k], device_id=to, device_id_type=pl.DeviceIdType.MESH)

        mine = pltpu.make_async_copy(x_ref, slot(x, y, c), local_sem)
        mine.start()
        first = [copy(0, (x, y, c), sibling, src=x_ref)]
        first += [copy(1 + j, (x, y, c), (*chip, c), src=x_ref) for j, chip in enumerate(chips)]
        for cp in first:
            cp.start()
        passed = [copy(4 + j, (*chip, c), sibling) for j, chip in enumerate(chips)]
        for j, chip in enumerate(chips):
            copy(1 + j, (*chip, c), (x, y, c)).wait_recv()
            passed[j].start()
        copy(0, sibling, (x, y, c)).wait_recv()
        for j, chip in enumerate(chips):
            copy(4 + j, (*chip, 1 - c), (x, y, c)).wait_recv()
        for cp in first + passed:
            cp.wait_send()
        mine.wait()

    anyspec = pl.BlockSpec(memory_space=pl.ANY)
    return pl.pallas_call(
        body, in_specs=[anyspec], out_specs=anyspec, out_shape=SDS((N_DEV,) + arr.shape, arr.dtype),
        scratch_shapes=[pltpu.SemaphoreType.DMA((N_DEV - 1,)), pltpu.SemaphoreType.DMA((N_DEV - 1,)),
                        pltpu.SemaphoreType.DMA],
        compiler_params=pltpu.CompilerParams(has_side_effects=True), name=name)(arr)


def _call(body, *, grid, in_specs, out_specs, out_shape, args, name, scratch_shapes=(), comm=None):
    n_grid, scratch_shapes = len(grid), list(scratch_shapes)
    if comm is None:
        outs = pl.pallas_call(body, grid=grid, in_specs=in_specs, out_specs=out_specs, out_shape=out_shape,
                              scratch_shapes=scratch_shapes, compiler_params=_params(n_grid), name=name)(*args)
        return outs, []
    arrs, scatter = comm
    nc, n_in, n_out, n_sc = len(arrs), len(in_specs), len(out_specs), len(scratch_shapes)
    comm_shapes, sems = _comm_shapes(arrs, scatter)

    def wrapped(*refs):
        ins, cins = refs[:n_in], refs[n_in:n_in + nc]
        o0 = n_in + nc
        outs, couts = refs[o0:o0 + n_out], refs[o0 + n_out:o0 + n_out + nc]
        s0 = o0 + n_out + nc
        scratch, sem_refs = refs[s0:s0 + n_sc], refs[s0 + n_sc:]
        first = functools.reduce(jnp.logical_and, [pl.program_id(k) == 0 for k in range(n_grid)])
        last = functools.reduce(jnp.logical_and, [pl.program_id(k) == grid[k] - 1 for k in range(n_grid)])

        @pl.when(first)
        def _():
            _exchange_phase(cins, couts, scatter, sem_refs, True)

        body(*ins, *outs, *scratch)

        @pl.when(last)
        def _():
            _exchange_phase(cins, couts, scatter, sem_refs, False)

    anyspec = pl.BlockSpec(memory_space=pl.ANY)
    res = pl.pallas_call(
        wrapped, grid=grid, in_specs=list(in_specs) + [anyspec] * nc, out_specs=list(out_specs) + [anyspec] * nc,
        out_shape=list(out_shape) + comm_shapes, scratch_shapes=scratch_shapes + sems,
        compiler_params=pltpu.CompilerParams(dimension_semantics=("arbitrary",) * n_grid,
                                             vmem_limit_bytes=VMEM_LIMIT, has_side_effects=True),
        name=name)(*args, *arrs)
    return res[:n_out], res[n_out:]


def _rms(xv, g):
    r = lax.rsqrt(jnp.mean(xv * xv, axis=-1, keepdims=True) + NORM_EPS)
    return (xv * r * g).astype(BF16)


def ffn_in(x, g, w, name, comm=None):
    T, D = x.shape
    F = w.shape[1]
    tm = 512

    def body(x_ref, g_ref, w_ref, h_ref, z_ref, a_ref):
        hv = _rms(x_ref[...], g_ref[...])
        h_ref[...] = hv
        pending = None
        for j in range(5):
            if j < 4:
                zs = (_dot_nt(hv, w_ref[j]), _dot_nt(hv, w_ref[j + 4]))
            if pending is not None:
                zg, zu = pending
                sg = _sigmoid(zg)
                silu = zg * sg
                z_ref[0, j - 1] = (zu * (sg + silu - silu * sg)).astype(BF16)
                z_ref[1, j - 1] = silu.astype(BF16)
                a_ref[j - 1] = (silu * zu).astype(BF16)
            pending = zs

    return _call(
        body, grid=(T // tm,),
        in_specs=[pl.BlockSpec((tm, D), lambda i: (i, 0)), pl.BlockSpec((1, D), lambda i: (0, 0)),
                  _resident((8, F, D))],
        out_specs=[pl.BlockSpec((tm, D), lambda i: (i, 0)), pl.BlockSpec((2, 4, tm, F), lambda i: (0, 0, i, 0)),
                   pl.BlockSpec((4, tm, F), lambda i: (0, i, 0))],
        out_shape=[SDS((T, D), BF16), SDS((2, 4, T, F), BF16), SDS((4, T, F), BF16)],
        args=(x, g, w), name=name, comm=comm)


def norm_proj(x, g, w, name):
    T, K = x.shape
    nb, _, N = w.shape
    tm = 512

    def body(x_ref, g_ref, w_ref, h_ref, o_ref):
        hv = _rms(x_ref[...], g_ref[...])
        h_ref[...] = hv
        for b in range(nb):
            o_ref[b] = _dot(hv, w_ref[b])

    return pl.pallas_call(
        body, grid=(T // tm,),
        in_specs=[pl.BlockSpec((tm, K), lambda i: (i, 0)), pl.BlockSpec((1, K), lambda i: (0, 0)),
                  _resident((nb, K, N))],
        out_specs=[pl.BlockSpec((tm, K), lambda i: (i, 0)), pl.BlockSpec((nb, tm, N), lambda i: (0, i, 0))],
        out_shape=[SDS((T, K), BF16), SDS((nb, T, N), F32)], compiler_params=_params(1), name=name)(x, g, w)


def mm_acc_norm(a, w, xres, g, scale, name, comm=None, tgt=None):
    nb, T, K = a.shape
    D = w.shape[2]
    with_loss = tgt is not None
    tm = 512 if with_loss else 1024
    rc = tm // ROW_SPLIT

    def body(a_ref, w_ref, x_ref, g_ref, *rest):
        if with_loss:
            t_ref, dy_ref, sq_ref, do_ref, dg_ref = rest

            @pl.when(pl.program_id(0) == 0)
            def _():
                sq_ref[...] = jnp.zeros_like(sq_ref)
                dg_ref[...] = jnp.zeros_like(dg_ref)
        else:
            o_ref, y_ref = rest
        accs = []
        for c in range(ROW_SPLIT):
            rows = pl.ds(c * rc, rc)
            o = _dot(a_ref[0, rows, :].astype(BF16), w_ref[0])
            for b in range(1, nb):
                o += _dot(a_ref[b, rows, :].astype(BF16), w_ref[b])
            accs.append(o)
        for c, o in enumerate(accs):
            rows = pl.ds(c * rc, rc)
            r = lax.rsqrt(jnp.mean(o * o, axis=-1, keepdims=True) + NORM_EPS)
            y = x_ref[rows, :] + scale * (o * r * g_ref[...])
            if with_loss:
                e = y - t_ref[rows, :]
                dy = e * (1.0 / D)
                dy_ref[rows, :] = dy
                sq_ref[...] += jnp.sum(e * e, axis=0, keepdims=True)
                do, dg = _post_bwd(dy, o, g_ref[...], scale)
                do_ref[rows, :] = do
                dg_ref[...] += dg
            else:
                o_ref[rows, :] = o
                y_ref[rows, :] = y

    tile = pl.BlockSpec((tm, D), lambda i: (i, 0))
    row = pl.BlockSpec((1, D), lambda i: (0, 0))
    in_specs = [pl.BlockSpec((nb, tm, K), lambda i: (0, i, 0)), _resident((nb, K, D)), tile, row]
    args = (a, w, xres, g)
    if with_loss:
        return _call(body, grid=(T // tm,), in_specs=in_specs + [tile], out_specs=[tile, row, tile, row],
                     out_shape=[SDS((T, D), F32), SDS((1, D), F32), SDS((T, D), BF16), SDS((1, D), F32)],
                     args=args + (tgt,), name=name, comm=comm)
    return _call(body, grid=(T // tm,), in_specs=in_specs, out_specs=[tile, tile],
                 out_shape=[SDS((T, D), F32), SDS((T, D), F32)], args=args, name=name, comm=comm)


def _post_bwd(dy, ov, g, scale):
    r = scale * dy
    rstd = lax.rsqrt(jnp.mean(ov * ov, axis=-1, keepdims=True) + NORM_EPS)
    oh = ov * rstd
    rg = r * g
    do = rstd * (rg - oh * jnp.mean(rg * oh, axis=-1, keepdims=True))
    return do.astype(BF16), jnp.sum(r * oh, axis=0, keepdims=True)


def mm_nt_b(gr, w, name):
    T, N = gr.shape
    nb, K, _ = w.shape
    tm = 512

    def body(g_ref, w_ref, o_ref):
        gv = g_ref[...]
        for b in range(nb):
            o_ref[b] = _dot_nt(gv, w_ref[b])

    return pl.pallas_call(
        body, grid=(T // tm,),
        in_specs=[pl.BlockSpec((tm, N), lambda i: (i, 0)), _resident((nb, K, N))],
        out_specs=pl.BlockSpec((nb, tm, K), lambda i: (0, i, 0)),
        out_shape=SDS((nb, T, K), F32), compiler_params=_params(1), name=name)(gr, w)


def ffn_dact(do, w_out, z, name):
    T, D = do.shape
    nb, F, _ = w_out.shape
    tm = 512

    def body(g_ref, w_ref, z_ref, dz_ref):
        gv = g_ref[...]
        pending = None
        for b in range(nb + 1):
            da = _dot_nt(gv, w_ref[b]) if b < nb else None
            if pending is not None:
                dz_ref[0, b - 1] = (pending * z_ref[0, b - 1].astype(F32)).astype(BF16)
                dz_ref[1, b - 1] = (pending * z_ref[1, b - 1].astype(F32)).astype(BF16)
            pending = da

    blk = pl.BlockSpec((2, nb, tm, F), lambda i: (0, 0, i, 0))
    return pl.pallas_call(
        body, grid=(T // tm,),
        in_specs=[pl.BlockSpec((tm, D), lambda i: (i, 0)), _resident((nb, F, D)), blk],
        out_specs=blk, out_shape=SDS((2, nb, T, F), BF16), compiler_params=_params(1), name=name)(do, w_out, z)


def mm_tn(a, g, a_batched, g_batched, nb, name, comm=None):
    T = a.shape[-2]
    K, N = a.shape[-1], g.shape[-1]
    tk = 4096
    nk = T // tk

    def body(a_ref, g_ref, o_ref, acc):
        k = pl.program_id(1)

        @pl.when(k == 0)
        def _():
            acc[...] = jnp.zeros_like(acc)

        acc[...] += _dot_tn(a_ref[...].astype(BF16), g_ref[...].astype(BF16))

        @pl.when(k == nk - 1)
        def _():
            o_ref[...] = acc[...].astype(BF16)

    a_spec = (pl.BlockSpec((None, tk, K), lambda b, k: (b, k, 0)) if a_batched
              else pl.BlockSpec((tk, K), lambda b, k: (k, 0)))
    g_spec = (pl.BlockSpec((None, tk, N), lambda b, k: (b, k, 0)) if g_batched
              else pl.BlockSpec((tk, N), lambda b, k: (k, 0)))
    (out,), slots = _call(
        body, grid=(nb, nk), in_specs=[a_spec, g_spec],
        out_specs=[pl.BlockSpec((None, K, N), lambda b, k: (b, 0, 0))],
        out_shape=[SDS((nb, K, N), BF16)], scratch_shapes=[pltpu.VMEM((K, N), F32)],
        args=(a, g), name=name, comm=comm)
    return out, slots


def mm_tn_shared_a(a, g, name):
    T, K = a.shape
    nb, _, N = g.shape
    tk = 1024
    nk = T // tk

    def body(a_ref, g_ref, o_ref, acc):
        k = pl.program_id(0)

        @pl.when(k == 0)
        def _():
            acc[...] = jnp.zeros_like(acc)

        av = a_ref[...]
        for b in range(nb):
            acc[b] += _dot_tn(av, g_ref[b].astype(BF16))

        @pl.when(k == nk - 1)
        def _():
            o_ref[...] = acc[...].astype(BF16)

    return pl.pallas_call(
        body, grid=(nk,),
        in_specs=[pl.BlockSpec((tk, K), lambda k: (k, 0)), pl.BlockSpec((nb, tk, N), lambda k: (0, k, 0))],
        out_specs=_resident((nb, K, N)), out_shape=SDS((nb, K, N), BF16),
        scratch_shapes=[pltpu.VMEM((nb, K, N), F32)], compiler_params=_params(1), name=name)(a, g)


def dh_pre_bwd(dz, w, x, g, dyres, name, comm=None, post=None, w_transposed=False):
    nb, T, F = dz.shape
    D = x.shape[1]
    tm = 512
    rc = tm // ROW_SPLIT
    mm = _dot if w_transposed else _dot_nt

    def body(dz_ref, w_ref, x_ref, g_ref, dy_ref, *rest):
        if post is None:
            dx_ref, dg_ref = rest
        else:
            o_ref, gp_ref, dx_ref, dg_ref, do_ref, dgp_ref = rest

        @pl.when(pl.program_id(0) == 0)
        def _():
            dg_ref[...] = jnp.zeros_like(dg_ref)
            if post is not None:
                dgp_ref[...] = jnp.zeros_like(dgp_ref)

        accs = []
        for c in range(ROW_SPLIT):
            rows = pl.ds(c * rc, rc)
            dh = mm(dz_ref[0, rows, :].astype(BF16), w_ref[0])
            for b in range(1, nb):
                dh += mm(dz_ref[b, rows, :].astype(BF16), w_ref[b])
            accs.append(dh)
        for c, dh in enumerate(accs):
            rows = pl.ds(c * rc, rc)
            xv = x_ref[rows, :]
            rstd = lax.rsqrt(jnp.mean(xv * xv, axis=-1, keepdims=True) + NORM_EPS)
            xh = xv * rstd
            dg_ref[...] += jnp.sum(dh * xh, axis=0, keepdims=True)
            dhg = dh * g_ref[...]
            dx = dy_ref[rows, :] + rstd * (dhg - xh * jnp.mean(dhg * xh, axis=-1, keepdims=True))
            dx_ref[rows, :] = dx
            if post is not None:
                do, dgp = _post_bwd(dx, o_ref[rows, :], gp_ref[...], post[2])
                do_ref[rows, :] = do
                dgp_ref[...] += dgp

    tile = pl.BlockSpec((tm, D), lambda i: (i, 0))
    row = pl.BlockSpec((1, D), lambda i: (0, 0))
    in_specs = [pl.BlockSpec((nb, tm, F), lambda i: (0, i, 0)), _resident(w.shape), tile, row, tile]
    out_specs, out_shape, args = [tile, row], [SDS((T, D), F32), SDS((1, D), F32)], (dz, w, x, g, dyres)
    if post is not None:
        in_specs += [tile, row]
        out_specs += [tile, row]
        out_shape += [SDS((T, D), BF16), SDS((1, D), F32)]
        args += (post[0], post[1])
    return _call(body, grid=(T // tm,), in_specs=in_specs, out_specs=out_specs, out_shape=out_shape,
                 args=args, name=name, comm=comm)


ATTN_GROUP = {1: 8, 4: 2, 16: 1}
ATTN_GROUP_BWD = {1: 16, 4: 4, 16: 1}
ATTN_UNROLL = 4


def _attn_masks():
    qi = lax.broadcasted_iota(jnp.int32, (QBLK, QBLK), 0)
    kj = lax.broadcasted_iota(jnp.int32, (QBLK, QBLK), 1)
    cur_ok = kj <= qi
    prev_ok = kj >= qi
    dcur = (qi - kj).astype(F32)
    return cur_ok, prev_ok, dcur, dcur + float(QBLK)


def _head_slopes(p, d):
    out = []
    for hq in range(2):
        v = [float(d) * 2.0 ** -(2 * q + hq + 1) for q in range(4)]
        out.append(jnp.where(p == 0, v[0], jnp.where(p == 1, v[1], jnp.where(p == 2, v[2], v[3]))))
    return out


def _rows(start, d):
    return pl.ds(start, QBLK, stride=d) if d > 1 else pl.ds(start, QBLK)


def _pair_spec(rows, part, blk):
    return pl.BlockSpec((None, rows, PAIR_W), lambda p, n: (2 * part + p // 2, blk(n), p % 2))


def _for_query_blocks(d, groups, several):
    blocks = [(g, r) for r in range(d) for g in range(groups)]
    for s in range(0, len(blocks), ATTN_UNROLL):
        several(blocks[s:s + ATTN_UNROLL])


def attn_fwd(proj, d, name, comm=None):
    T = proj.shape[1]
    sb, groups = QBLK * d, ATTN_GROUP[d]
    rb = sb * groups
    nblk = T // rb

    def body(q_ref, kc_ref, kp_ref, vc_ref, vp_ref, o_ref, l_ref):
        p, n = pl.program_id(0), pl.program_id(1)
        cur_ok, prev_ok, dcur, dprev = _attn_masks()
        first_ok = jnp.logical_and(prev_ok, n > 0)
        lane_head = lax.broadcasted_iota(jnp.int32, (QBLK, PAIR_W), 1) // HEAD_DIM
        slopes = _head_slopes(p, d)

        def several(blocks):
            work = []
            for g, r in blocks:
                rows = _rows(g * sb + r, d)
                q = q_ref[rows, :]
                kc, vc = kc_ref[rows, :].astype(BF16), vc_ref[rows, :].astype(BF16)
                if g == 0:
                    prow, pok = _rows(r, d), first_ok
                    kp, vp = kp_ref[prow, :].astype(BF16), vp_ref[prow, :].astype(BF16)
                else:
                    prow, pok = _rows((g - 1) * sb + r, d), prev_ok
                    kp, vp = kc_ref[prow, :].astype(BF16), vc_ref[prow, :].astype(BF16)
                for hq in range(2):
                    qm = jnp.where(lane_head == hq, q, 0.0).astype(BF16)
                    work.append([rows, hq, pok, vc, vp, _dot_nt(qm, kc), _dot_nt(qm, kp)])
            for w in work:
                _, hq, pok, _, _, sc, sp = w
                sc = jnp.where(cur_ok, sc * 0.125 - slopes[hq] * dcur, NEG)
                sp = jnp.where(pok, sp * 0.125 - slopes[hq] * dprev, NEG)
                m = jnp.maximum(jnp.max(sc, axis=1, keepdims=True), jnp.max(sp, axis=1, keepdims=True))
                pc = jnp.exp(sc - m)
                pp = jnp.exp(sp - m)
                den = jnp.sum(pc, axis=1, keepdims=True) + jnp.sum(pp, axis=1, keepdims=True)
                w[5:] = [pc.astype(BF16), pp.astype(BF16), 1.0 / den, m + jnp.log(den)]
            for i in range(0, len(work), 2):
                o_acc = jnp.zeros((QBLK, PAIR_W), F32)
                l_acc = jnp.zeros((QBLK, PAIR_W), F32)
                for rows, hq, _, vc, vp, pc, pp, inv, lse in work[i:i + 2]:
                    hm = lane_head == hq
                    o_acc = jnp.where(hm, (_dot(pc, vc) + _dot(pp, vp)) * inv, o_acc)
                    l_acc = jnp.where(hm, lse, l_acc)
                o_ref[rows, :] = o_acc
                l_ref[rows, :] = l_acc

        _for_query_blocks(d, groups, several)

    cur = lambda part: _pair_spec(rb, part, lambda n: n)
    prv = lambda part: _pair_spec(sb, part, lambda n: jnp.maximum(n * groups - 1, 0))
    return _call(
        body, grid=(4, nblk), in_specs=[cur(0), cur(1), prv(1), cur(2), prv(2)], out_specs=[cur(0), cur(0)],
        out_shape=[SDS((2, T, 2 * PAIR_W), F32), SDS((2, T, 2 * PAIR_W), F32)],
        args=(proj, proj, proj, proj, proj), name=name, comm=comm)


def mix_out(os_, ls_, y_ssm, w_glu, b_glu, w, xres, g, name):
    _, T, HW = y_ssm.shape
    D = w.shape[2]
    tm = 512

    def body(o1, o2, o3, l1, l2, l3, s_ref, wg_ref, bg_ref, w_ref, x_ref, g_ref, cat_ref, l_ref, lg_ref, m_ref, y_ref):
        y0, _ = _gelu(s_ref[0])
        y1, _ = _gelu(s_ref[1])
        lg = _dot(y0.astype(BF16), wg_ref[0]) + _dot(y1.astype(BF16), wg_ref[1]) + bg_ref[...]
        a, b, c = l1[...], l2[...], l3[...]
        m = jnp.maximum(jnp.maximum(a, b), c)
        ea, eb, ec = jnp.exp(a - m), jnp.exp(b - m), jnp.exp(c - m)
        s = ea + eb + ec
        att = (ea * o1[...] + eb * o2[...] + ec * o3[...]) * (1.0 / s)
        sg = _sigmoid(lg)
        ssm0, ssm1 = y0 * sg[:, :HW], y1 * sg[:, HW:]
        cat_ref[pl.ds(0, 2)] = att
        cat_ref[2] = ssm0
        cat_ref[3] = ssm1
        l_ref[...] = m + jnp.log(s)
        lg_ref[0] = lg[:, :HW]
        lg_ref[1] = lg[:, HW:]
        o = _dot(att[0].astype(BF16), w_ref[0]) + _dot(att[1].astype(BF16), w_ref[1])
        o += _dot(ssm0.astype(BF16), w_ref[2]) + _dot(ssm1.astype(BF16), w_ref[3])
        r = lax.rsqrt(jnp.mean(o * o, axis=-1, keepdims=True) + NORM_EPS)
        m_ref[...] = o
        y_ref[...] = x_ref[...] + o * r * g_ref[...]

    spec = pl.BlockSpec((2, tm, HW), lambda i: (0, i, 0))
    tile = pl.BlockSpec((tm, D), lambda i: (i, 0))
    return pl.pallas_call(
        body, grid=(T // tm,),
        in_specs=[spec] * 7 + [_resident(w_glu.shape), _resident(b_glu.shape), _resident(w.shape), tile,
                               pl.BlockSpec((1, D), lambda i: (0, 0))],
        out_specs=[pl.BlockSpec((4, tm, HW), lambda i: (0, i, 0)), spec, spec, tile, tile],
        out_shape=[SDS((4, T, HW), F32), SDS((2, T, HW), F32), SDS((2, T, HW), F32), SDS((T, D), F32),
                   SDS((T, D), F32)],
        compiler_params=_params(1), name=name)(*os_, *ls_, y_ssm, w_glu, b_glu, w, xres, g)


def attn_bwd(proj, dcat, o, lse, acc, d, name, du=None):
    T = proj.shape[1]
    sb, groups = QBLK * d, ATTN_GROUP_BWD[d]
    rb = sb * groups
    nblk = T // rb
    has_acc = acc is not None
    n_parts = 3 if du is None else 4

    def body(*refs):
        (qc_ref, qn_ref, kc_ref, kp_ref, vc_ref, vp_ref, dc_ref, dn_ref, oc_ref, on_ref, lc_ref, ln_ref) = refs[:12]
        acc_ref = refs[12] if has_acc else None
        out_ref = refs[-1]
        if du is not None:
            out_ref[3] = refs[-2][...]
        p, n = pl.program_id(0), pl.program_id(1)
        cur_ok, prev_ok, dcur, dprev = _attn_masks()
        first_ok = jnp.logical_and(prev_ok, n > 0)
        last_ok = jnp.logical_and(prev_ok, n < nblk - 1)
        lane_head = lax.broadcasted_iota(jnp.int32, (QBLK, PAIR_W), 1) // HEAD_DIM
        slopes = _head_slopes(p, d)

        def one(g, r, shared):
            rows = _rows(g * sb + r, d)
            q_c, do_c, o_c, l_c = qc_ref[rows, :], dc_ref[rows, :], oc_ref[rows, :], lc_ref[rows, :]
            k_c, v_c = kc_ref[rows, :].astype(BF16), vc_ref[rows, :].astype(BF16)
            if shared:
                pok_c, k_p, v_p = prev_ok, None, None
            elif g == 0:
                prow, pok_c = _rows(r, d), first_ok
                k_p, v_p = kp_ref[prow, :].astype(BF16), vp_ref[prow, :].astype(BF16)
            else:
                prow, pok_c = _rows((g - 1) * sb + r, d), prev_ok
                k_p, v_p = kc_ref[prow, :].astype(BF16), vc_ref[prow, :].astype(BF16)
            if g == groups - 1:
                nrow, pok_n = _rows(r, d), last_ok
                q_n, do_n, o_n, l_n = qn_ref[nrow, :], dn_ref[nrow, :], on_ref[nrow, :], ln_ref[nrow, :]
            else:
                nrow, pok_n = _rows((g + 1) * sb + r, d), prev_ok
                q_n, do_n, o_n, l_n = qc_ref[nrow, :], dc_ref[nrow, :], oc_ref[nrow, :], lc_ref[nrow, :]
            heads = []
            for hq in range(2):
                hm = lane_head == hq
                qm_c = jnp.where(hm, q_c, 0.0).astype(BF16)
                qm_n = jnp.where(hm, q_n, 0.0).astype(BF16)
                dom_c = jnp.where(hm, do_c, 0.0)
                dom_n = jnp.where(hm, do_n, 0.0)
                dd_c = jnp.sum(dom_c * o_c, axis=1, keepdims=True)
                dd_n = jnp.sum(dom_n * o_n, axis=1, keepdims=True)
                ls_c = jnp.max(jnp.where(hm, l_c, NEG), axis=1, keepdims=True)
                ls_n = jnp.max(jnp.where(hm, l_n, NEG), axis=1, keepdims=True)
                dob_c, dob_n = dom_c.astype(BF16), dom_n.astype(BF16)
                mm = [(_dot_nt(qm_c, k_c), _dot_nt(dob_c, v_c)),
                      None if shared else (_dot_nt(qm_c, k_p), _dot_nt(dob_c, v_p)),
                      (_dot_nt(qm_n, k_c), _dot_nt(dob_n, v_c))]
                heads.append(dict(hq=hq, qm_c=qm_c, qm_n=qm_n, dob_c=dob_c, dob_n=dob_n, mm=mm,
                                  dd=(dd_c, dd_c, dd_n), ls=(ls_c, ls_c, ls_n)))
            return dict(rows=rows, k_c=k_c, k_p=k_p, heads=heads, oks=(cur_ok, pok_c, pok_n), shared=shared)

        def several(blocks):
            work = []
            for i, (g, r) in enumerate(blocks):
                work.append(one(g, r, i > 0 and blocks[i - 1] == (g - 1, r)))
            for i, w in enumerate(work):
                if w["shared"]:
                    w["k_p"] = work[i - 1]["k_c"]
                for hi, h in enumerate(w["heads"]):
                    slope, dist = slopes[h["hq"]], (dcur, dprev, dprev)
                    h["pr"], h["ds"] = [], []
                    for j in range(3):
                        if h["mm"][j] is None:
                            h["pr"].append(work[i - 1]["heads"][hi]["pr"][2])
                            h["ds"].append(work[i - 1]["heads"][hi]["ds"][2])
                            continue
                        s = jnp.where(w["oks"][j], h["mm"][j][0] * 0.125 - slope * dist[j], NEG)
                        pr = jnp.exp(s - h["ls"][j])
                        h["pr"].append(pr.astype(BF16))
                        h["ds"].append((pr * (h["mm"][j][1] - h["dd"][j])).astype(BF16))
            for w in work:
                dq = jnp.zeros((QBLK, PAIR_W), F32)
                dk = jnp.zeros((QBLK, PAIR_W), F32)
                dv = jnp.zeros((QBLK, PAIR_W), F32)
                for h in w["heads"]:
                    ds, pr = h["ds"], h["pr"]
                    dq_h = _dot(ds[0], w["k_c"]) + _dot(ds[1], w["k_p"])
                    dk += (_dot_tn(ds[0], h["qm_c"]) + _dot_tn(ds[2], h["qm_n"])) * 0.125
                    dv += _dot_tn(pr[0], h["dob_c"]) + _dot_tn(pr[2], h["dob_n"])
                    dq = jnp.where(lane_head == h["hq"], dq_h * 0.125, dq)
                for part, val in enumerate((dq, dk, dv)):
                    if has_acc:
                        val = val + acc_ref.at[part][w["rows"], :]
                    out_ref.at[part][w["rows"], :] = val

        _for_query_blocks(d, groups, several)

    cur = lambda part: _pair_spec(rb, part, lambda n: n)
    prv = lambda part: _pair_spec(sb, part, lambda n: jnp.maximum(n * groups - 1, 0))
    nxt = lambda part: _pair_spec(sb, part, lambda n: jnp.minimum((n + 1) * groups, T // sb - 1))
    full = pl.BlockSpec((3, None, rb, PAIR_W), lambda p, n: (0, p // 2, n, p % 2))
    in_specs = [cur(0), nxt(0), cur(1), prv(1), cur(2), prv(2), cur(0), nxt(0), cur(0), nxt(0), cur(0), nxt(0)]
    args = [proj, proj, proj, proj, proj, proj, dcat, dcat, o, o, lse, lse]
    if has_acc:
        in_specs.append(full)
        args.append(acc)
    if du is not None:
        in_specs.append(cur(0))
        args.append(du)
    out_spec = pl.BlockSpec((n_parts, None, rb, PAIR_W), lambda p, n: (0, p // 2, n, p % 2))
    return pl.pallas_call(
        body, grid=(4, nblk), in_specs=in_specs, out_specs=out_spec,
        out_shape=SDS((n_parts, 2, T, 2 * PAIR_W), F32), compiler_params=_params(2), name=name)(*args)


def _scan_rows(buf, tab_ref, reverse, half):
    n_tiles = (buf.shape[0] - 8) // 8
    per_half = HALF_STATES // SCAN_CW
    row = lax.broadcasted_iota(jnp.int32, (8, SCAN_CW), 0)
    sgn = -1.0 if reverse else 1.0

    for j in range(per_half):
        c0 = half * 2 * HALF_STATES + j * SCAN_CW
        cre = pl.ds(c0, SCAN_CW)
        cim = pl.ds(c0 + HALF_STATES, SCAN_CW)
        steps = []
        for s, k in enumerate((1, 2, 4)):
            ok, shift = (row < 8 - k, 8 - k) if reverse else (row >= k, k)
            steps.append((shift, jnp.where(ok, tab_ref[pl.ds(s, 1), cre], 0.0),
                          jnp.where(ok, sgn * tab_ref[pl.ds(s, 1), cim], 0.0)))
        trow = 16 if reverse else 8
        pr, pi = tab_ref[pl.ds(trow, 8), cre], tab_ref[pl.ds(trow, 8), cim]
        for t in range(n_tiles):
            base = 8 * (n_tiles - 1 - t) if reverse else 8 + 8 * t
            rows = pl.ds(base, 8)
            re, im = buf[rows, cre], buf[rows, cim]
            for shift, ar, ai in steps:
                sre, sim = pltpu.roll(re, shift, 0), pltpu.roll(im, shift, 0)
                re, im = re + ar * sre - ai * sim, im + ar * sim + ai * sre
            crow = pl.ds(base + 8 if reverse else base - 1, 1)
            cr, ci = buf[crow, cre], buf[crow, cim]
            buf[rows, cre] = re + pr * cr - pi * ci
            buf[rows, cim] = im + pr * ci + pi * cr


def ssm_fwd(proj, bh, ch, apow, dskip, name, comm=None):
    _, T, C = proj.shape
    tm = SCAN_TM
    SW = 4 * HALF_STATES

    def body(u_ref, bh_ref, ch_ref, tab_ref, dsk_ref, y_ref, s_ref, buf):
        @pl.when(pl.program_id(0) == 0)
        def _():
            buf[pl.ds(0, 8), :] = jnp.zeros((8, SW), F32)

        for h in range(2):
            buf[pl.ds(8, tm), pl.ds(h * 2 * HALF_STATES, 2 * HALF_STATES)] = _dot(u_ref[h].astype(BF16), bh_ref[h])
        for h in range(2):
            cols = pl.ds(h * 2 * HALF_STATES, 2 * HALF_STATES)
            _scan_rows(buf, tab_ref, False, h)
            sv = buf[pl.ds(8, tm), cols]
            s_ref[:, cols] = sv
            y_ref[h] = _dot(sv.astype(BF16), ch_ref[h]) + dsk_ref[h] * u_ref[h]
        buf[pl.ds(0, 8), :] = buf[pl.ds(tm, 8), :]

    return _call(
        body, grid=(T // tm,),
        in_specs=[pl.BlockSpec((2, tm, C), lambda i: (3, i, 0)),
                  _resident((2, C, 2 * HALF_STATES)), _resident((2, 2 * HALF_STATES, C)), _resident((24, SW)),
                  _resident((2, 1, C))],
        out_specs=[pl.BlockSpec((2, tm, C), lambda i: (0, i, 0)), pl.BlockSpec((tm, SW), lambda i: (i, 0))],
        out_shape=[SDS((2, T, C), F32), SDS((T, SW), F32)],
        scratch_shapes=[pltpu.VMEM((tm + 8, SW), F32)],
        args=(proj, bh, ch, apow, dskip), name=name, comm=comm)


def ssm_bwd(dy, proj, st, bh, ch, apow, dskip, name, comm=None):
    _, T, C = proj.shape
    tm = SCAN_TM
    nt = T // tm
    SW = 4 * HALF_STATES
    HS2 = 2 * HALF_STATES

    def body(dy_ref, u_ref, s_ref, sp_ref, bh_ref, ch_ref, tab_ref, dsk_ref,
             du_ref, da_ref, dbh_ref, dch_ref, dd_ref, lam):
        i = pl.program_id(0)

        @pl.when(i == 0)
        def _():
            lam[pl.ds(tm, 8), :] = jnp.zeros((8, SW), F32)
            da_ref[...] = jnp.zeros_like(da_ref)
            dbh_ref[...] = jnp.zeros_like(dbh_ref)
            dch_ref[...] = jnp.zeros_like(dch_ref)
            dd_ref[...] = jnp.zeros_like(dd_ref)

        for h in range(2):
            lam[pl.ds(0, tm), pl.ds(h * HS2, HS2)] = _dot_nt(dy_ref[h].astype(BF16), ch_ref[h])
        for h in range(2):
            dyv, uv = dy_ref[h], u_ref[h]
            dch_ref[h] += _dot_tn(s_ref[:, pl.ds(h * HS2, HS2)].astype(BF16), dyv.astype(BF16))
            dd_ref[h] += jnp.sum(dyv * uv, axis=0, keepdims=True)
        for h in range(2):
            _scan_rows(lam, tab_ref, True, h)
            lb = lam[pl.ds(0, tm), pl.ds(h * HS2, HS2)].astype(BF16)
            du_ref[h] = _dot_nt(lb, bh_ref[h]) + dsk_ref[h] * dy_ref[h]
            dbh_ref[h] += _dot_tn(u_ref[h].astype(BF16), lb)

        first = i == nt - 1
        per_half = HALF_STATES // SCAN_CW

        def chunk(j, _):
            c0 = pl.multiple_of((j // per_half) * HS2 + (j % per_half) * SCAN_CW, 128)
            cre, cim = pl.ds(c0, SCAN_CW), pl.ds(pl.multiple_of(c0 + HALF_STATES, 128), SCAN_CW)
            row0 = lax.broadcasted_iota(jnp.int32, (8, SCAN_CW), 0) == 0
            acc_r = jnp.zeros((8, SCAN_CW), F32)
            acc_i = jnp.zeros((8, SCAN_CW), F32)
            for t in range(tm // 8):
                rows = pl.ds(8 * t, 8)
                if t == 0:
                    pre = jnp.where(first, 0.0, sp_ref[pl.ds(7, 1), cre])
                    pim = jnp.where(first, 0.0, sp_ref[pl.ds(7, 1), cim])
                else:
                    pre, pim = s_ref[pl.ds(8 * t - 1, 1), cre], s_ref[pl.ds(8 * t - 1, 1), cim]
                spr = jnp.where(row0, pre, pltpu.roll(s_ref[rows, cre], 1, 0))
                spi = jnp.where(row0, pim, pltpu.roll(s_ref[rows, cim], 1, 0))
                lr, li = lam[rows, cre], lam[rows, cim]
                acc_r += lr * spr + li * spi
                acc_i += li * spr - lr * spi
            da_ref[:, cre] += jnp.sum(acc_r, axis=0, keepdims=True)
            da_ref[:, cim] += jnp.sum(acc_i, axis=0, keepdims=True)
            return 0

        lax.fori_loop(0, 2 * per_half, chunk, 0)
        lam[pl.ds(tm, 8), :] = lam[pl.ds(0, 8), :]

    rev = lambda i: nt - 1 - i
    return _call(
        body, grid=(nt,),
        in_specs=[pl.BlockSpec((2, tm, C), lambda i: (0, rev(i), 0)),
                  pl.BlockSpec((2, tm, C), lambda i: (3, rev(i), 0)),
                  pl.BlockSpec((tm, SW), lambda i: (rev(i), 0)),
                  pl.BlockSpec((8, SW), lambda i: (jnp.maximum(rev(i) * (tm // 8) - 1, 0), 0)),
                  _resident((2, C, HS2)), _resident((2, HS2, C)), _resident((24, SW)), _resident((2, 1, C))],
        out_specs=[pl.BlockSpec((2, tm, C), lambda i: (0, rev(i), 0)),
                   _resident((1, SW)), _resident((2, C, HS2)), _resident((2, HS2, C)), _resident((2, 1, C))],
        out_shape=[SDS((2, T, C), F32), SDS((1, SW), F32), SDS((2, C, HS2), F32), SDS((2, HS2, C), F32),
                   SDS((2, 1, C), F32)],
        scratch_shapes=[pltpu.VMEM((tm + 8, SW), F32)],
        args=(dy, proj, st, st, bh, ch, apow, dskip), name=name, comm=comm)


_GELU_C = math.sqrt(2.0 / math.pi)


def _gelu(x):
    t = jnp.tanh(_GELU_C * (x + 0.044715 * x * x * x))
    return 0.5 * x * (1.0 + t), t


def glu_bwd(dcat, y, lg, w, name):
    _, T, C = y.shape
    tm = 512

    def body(d_ref, y_ref, lg_ref, w_ref, dy_ref, dw_ref, db_ref):
        @pl.when(pl.program_id(0) == 0)
        def _():
            dw_ref[...] = jnp.zeros_like(dw_ref)
            db_ref[...] = jnp.zeros_like(db_ref)

        y2, th, sg, dlg = [], [], [], []
        for h in range(2):
            yy, tt = _gelu(y_ref[h])
            ss = _sigmoid(lg_ref[h])
            y2.append(yy)
            th.append(tt)
            sg.append(ss)
            dlg.append(d_ref[h] * yy * ss * (1.0 - ss))
        dl = jnp.concatenate(dlg, axis=1)
        dlb = dl.astype(BF16)
        db_ref[...] += jnp.sum(dl, axis=0, keepdims=True)
        for h in range(2):
            dy2 = d_ref[h] * sg[h] + _dot_nt(dlb, w_ref[h])
            yv = y_ref[h]
            dgelu = 0.5 * (1.0 + th[h]) + 0.5 * yv * (1.0 - th[h] * th[h]) * _GELU_C * (1.0 + 3 * 0.044715 * yv * yv)
            dy_ref[h] = dy2 * dgelu
            dw_ref[h] += _dot_tn(y2[h].astype(BF16), dlb)

    return pl.pallas_call(
        body, grid=(T // tm,),
        in_specs=[pl.BlockSpec((2, tm, C), lambda i: (1, i, 0)), pl.BlockSpec((2, tm, C), lambda i: (0, i, 0)),
                  pl.BlockSpec((2, tm, C), lambda i: (0, i, 0)), pl.BlockSpec((2, C, 2 * C), lambda i: (0, 0, 0))],
        out_specs=[pl.BlockSpec((2, tm, C), lambda i: (0, i, 0)), pl.BlockSpec((2, C, 2 * C), lambda i: (0, 0, 0)),
                   pl.BlockSpec((1, 2 * C), lambda i: (0, 0))],
        out_shape=[SDS((2, T, C), F32), SDS((2, C, 2 * C), F32), SDS((1, 2 * C), F32)],
        compiler_params=_params(1), name=name)(dcat, y, lg, w)


def adamw(w, m, v, slots, name):
    R, C = w.shape
    tr = R
    for cand in (512, 256, 128, 64, 32, 16, 8):
        if R % cand == 0 and cand * C * 4 <= 2 * 1024 * 1024:
            tr = cand
            break
    c1 = 1.0 / (1.0 - ADAM_B1 ** ADAM_STEP)
    c2 = 1.0 / (1.0 - ADAM_B2 ** ADAM_STEP)

    def body(w_ref, m_ref, v_ref, s_ref, g_ref, d_ref, nm_ref, nv_ref):
        g = s_ref[0].astype(F32)
        for j in range(1, N_DEV):
            g = g + s_ref[j].astype(F32)
        nm = ADAM_B1 * m_ref[...] + (1.0 - ADAM_B1) * g
        nv = ADAM_B2 * v_ref[...] + (1.0 - ADAM_B2) * (g * g)
        g_ref[...] = g
        nm_ref[...] = nm
        nv_ref[...] = nv
        d_ref[...] = -ADAM_LR * ((nm * c1) / (jnp.sqrt(nv * c2) + ADAM_EPS) + ADAM_WD * w_ref[...])

    spec = pl.BlockSpec((tr, C), lambda i: (i, 0))
    return pl.pallas_call(
        body, grid=(R // tr,),
        in_specs=[spec, spec, spec, pl.BlockSpec((N_DEV, tr, C), lambda i: (0, i, 0))],
        out_specs=[spec] * 4, out_shape=[SDS((R, C), F32)] * 4, compiler_params=_params(1), name=name)(w, m, v, slots)


def _discretise(a_re, a_im, log_dt, b_re, b_im):
    dt = jnp.exp(log_dt)[:, None]
    e = jnp.exp(dt * a_re)
    ar, ai = e * jnp.cos(dt * a_im), e * jnp.sin(dt * a_im)
    den = a_re * a_re + a_im * a_im
    nr, ni = ar - 1.0, ai
    wr = (nr * a_re + ni * a_im) / den
    wi = (ni * a_re - nr * a_im) / den
    bbr = wr[..., None] * b_re - wi[..., None] * b_im
    bbi = wr[..., None] * b_im + wi[..., None] * b_re
    return ar, ai, bbr, bbi


def _block_diag(t):
    eye = jnp.eye(16, dtype=t.dtype).reshape(1, 16, 1, 16, 1)
    r, c = t.shape[1], t.shape[2]
    return (t.reshape(2, 16, r, 1, c) * eye).reshape(2, 16 * r, 16 * c)


def _diag_blocks(m, r, c):
    eye = jnp.eye(16, dtype=m.dtype).reshape(1, 16, 1, 16, 1)
    return jnp.sum(m.reshape(2, 16, r, 16, c) * eye, axis=3).reshape(32, r, c)


def _state_vec(re, im):
    return jnp.stack([re.reshape(2, HALF_STATES), im.reshape(2, HALF_STATES)], axis=1).reshape(-1)


BIG = ("ffn1_w_in", "ffn1_w_out", "w_mix_in", "w_glu", "w_mix_out", "ffn2_w_in", "ffn2_w_out")
WEIGHTS = ("ffn1_pre_g", "ffn1_w_in", "ffn1_w_out", "ffn1_post_g", "mix_pre_g", "w_mix_in", "a_re", "a_im", "log_dt",
           "b_re", "b_im", "c_re", "c_im", "d_skip", "w_glu", "b_glu", "w_mix_out", "mix_post_g", "ffn2_pre_g",
           "ffn2_w_in", "ffn2_w_out", "ffn2_post_g")
SMALL = tuple(n for n in WEIGHTS if n not in BIG)
TRANSPOSED = ("ffn1_w_in", "ffn2_w_in")
PACK_COLS = 1024


def _pack(parts):
    flat = jnp.concatenate([p.reshape(-1) for p in parts])
    rows = -(-flat.shape[0] // (8 * PACK_COLS)) * 8
    return jnp.pad(flat, (0, rows * PACK_COLS - flat.shape[0])).reshape(rows, PACK_COLS)


def _unpack(packed, shapes):
    flat, out, off = packed.reshape(-1), [], 0
    for s in shapes:
        n = math.prod(s)
        out.append(flat[off:off + n].reshape(s))
        off += n
    return out


def _gather(names, wb):
    return [wb[n] for n in names], [False] * len(names)


def _ffn_bwd(dy, do, saved, x, pre_g, w_in, w_out4, tag, post=None, dwout_comm=None, carry_dw_in=True):
    h, z, a = saved
    T = x.shape[0]
    dz = ffn_dact(do, w_out4, z, f"{tag}_dact")
    dz8 = dz.reshape(8, T, dz.shape[-1])
    dw_out, extra = mm_tn(a, do, True, False, 4, f"{tag}_dwout", comm=dwout_comm)
    dw_in, (s_out,) = mm_tn(dz8, h, True, False, 8, f"{tag}_dwin", comm=([dw_out.reshape(8, -1, D_MODEL)], [True]))
    outs, s_in = dh_pre_bwd(dz8, w_in, x, pre_g, dy, f"{tag}_dh", comm=([dw_in], [True]) if carry_dw_in else None,
                            post=post, w_transposed=True)
    return outs, (s_in[0] if carry_dw_in else dw_in, s_out), extra


def local_step(x, tgt, sp, wb):
    T = x.shape[0]
    ar, ai, bbr, bbi = _discretise(sp["a_re"], sp["a_im"], sp["log_dt"], sp["b_re"], sp["b_im"])
    powers = [(ar, ai)]
    for _ in range(7):
        pr, pi = powers[-1]
        powers.append((pr * ar - pi * ai, pr * ai + pi * ar))
    zero = jnp.zeros_like(ar)
    rows = [_state_vec(*powers[k - 1]) for k in (1, 2, 4)] + [_state_vec(zero, zero)] * 5
    rows += [_state_vec(pr, pi) for pr, pi in powers]
    rows += [_state_vec(pr, -pi) for pr, pi in reversed(powers)]
    apow = jnp.stack(rows)
    bh = jnp.concatenate([_block_diag(bbr.transpose(0, 2, 1)), _block_diag(bbi.transpose(0, 2, 1))], axis=2)
    ch = jnp.concatenate([_block_diag(sp["c_re"].transpose(0, 2, 1)), _block_diag(-sp["c_im"].transpose(0, 2, 1))], axis=1)
    bh, ch = bh.astype(BF16), ch.astype(BF16)
    dskip = sp["d_skip"].reshape(2, 1, 256)

    w1_in = gather_two_level(wb["ffn1_w_in"], "gather_w1in")
    (h1, z1, a1), (w1_out, w_mi) = ffn_in(
        x, sp["ffn1_pre_g"], w1_in, "ffn1_in", comm=_gather(["ffn1_w_out", "w_mix_in"], wb))
    w1_out4 = w1_out.reshape(4, -1, D_MODEL)
    (o1, x1), (w_glu, w_mo) = mm_acc_norm(
        a1, w1_out4, x, sp["ffn1_post_g"], 0.5, "ffn1_out", comm=_gather(["w_glu", "w_mix_out"], wb))
    w_glu2, w_mo4 = w_glu.reshape(2, 256, 512), w_mo.reshape(4, 256, D_MODEL)
    h2, proj = norm_proj(x1, sp["mix_pre_g"], w_mi, "mix_proj")
    (y_ssm, states), (w2_in,) = ssm_fwd(proj, bh, ch, apow, dskip, "ssm_fwd", comm=_gather(["ffn2_w_in"], wb))
    os_, ls_ = [], []
    for d in DILATIONS:
        (o_d, l_d), got = attn_fwd(proj, d, f"attn_fwd_d{d}",
                                   comm=_gather(["ffn2_w_out"], wb) if d == DILATIONS[-1] else None)
        os_.append(o_d)
        ls_.append(l_d)
    w2_out4 = got[0].reshape(4, -1, D_MODEL)
    cat, lse, lg, mixed, x2 = mix_out(os_, ls_, y_ssm, w_glu2, sp["b_glu"], w_mo4, x1, sp["mix_post_g"], "mix_out")
    (h3, z3, a3), _ = ffn_in(x2, sp["ffn2_pre_g"], w2_in, "ffn2_in")
    (dy3, sq, do3, dg_f2post), _ = mm_acc_norm(a3, w2_out4, x2, sp["ffn2_post_g"], 0.5, "ffn2_out", tgt=tgt)

    (dx2, dg_f2pre, dmixed, dg_mpost), (dw2_in, s_w2out), _ = _ffn_bwd(
        dy3, do3, (h3, z3, a3), x2, sp["ffn2_pre_g"], w2_in, w2_out4, "ffn2", post=(mixed, sp["mix_post_g"], 1.0),
        carry_dw_in=False)
    dcat = mm_nt_b(dmixed, w_mo4, "mix_dcat")
    dw_mo, _ = mm_tn(cat, dmixed, True, False, 4, "mix_dwout")
    dy_ssm, dw_glu, db_glu = glu_bwd(dcat, y_ssm, lg, w_glu2, "glu_bwd")
    (du, da, dbh, dch, dd), (s_wmo, s_wglu, s_w2in) = ssm_bwd(
        dy_ssm, proj, states, bh, ch, apow, dskip, "ssm_bwd",
        comm=([dw_mo.reshape(8, 128, D_MODEL), dw_glu.astype(BF16).reshape(8, 64, 512), dw2_in], [True] * 3))
    dqkv = None
    for d in DILATIONS:
        dqkv = attn_bwd(proj, dcat, cat, lse, dqkv, d, f"attn_bwd_d{d}", du=du if d == DILATIONS[-1] else None)
    dproj = dqkv.reshape(8, T, 256)
    dw_mi = mm_tn_shared_a(h2, dproj, "mix_dwin")
    (dx1, dg_mpre, do1, dg_f1post), (s_wmi,) = dh_pre_bwd(
        dproj, w_mi, x1, sp["mix_pre_g"], dx2, "mix_dh", comm=([dw_mi], [True]), post=(o1, sp["ffn1_post_g"], 0.5))

    da4 = da.reshape(2, 2, HALF_STATES)
    d_ar, d_ai = da4[:, 0].reshape(32, N_STATE), da4[:, 1].reshape(32, N_STATE)
    d_bbr = _diag_blocks(dbh[:, :, :HALF_STATES], 16, N_STATE).transpose(0, 2, 1)
    d_bbi = _diag_blocks(dbh[:, :, HALF_STATES:], 16, N_STATE).transpose(0, 2, 1)
    _, disc_vjp = jax.vjp(_discretise, sp["a_re"], sp["a_im"], sp["log_dt"], sp["b_re"], sp["b_im"])
    g_are, g_aim, g_ldt, g_bre, g_bim = disc_vjp((d_ar, d_ai, d_bbr, d_bbi))
    g_cre = _diag_blocks(dch[:, :HALF_STATES], N_STATE, 16).transpose(0, 2, 1)
    g_cim = -_diag_blocks(dch[:, HALF_STATES:], N_STATE, 16).transpose(0, 2, 1)
    small = {
        "ffn1_pre_g": jnp.zeros((1, D_MODEL), F32), "ffn1_post_g": dg_f1post, "mix_pre_g": dg_mpre, "a_re": g_are,
        "a_im": g_aim, "log_dt": g_ldt, "b_re": g_bre, "b_im": g_bim, "c_re": g_cre, "c_im": g_cim,
        "d_skip": dd.reshape(1, 512), "b_glu": db_glu, "mix_post_g": dg_mpost, "ffn2_pre_g": dg_f2pre,
        "ffn2_post_g": dg_f2post,
    }
    (dx0, dg_f1pre), (s_w1in, s_w1out), (early,) = _ffn_bwd(
        dx1, do1, (h1, z1, a1), x, sp["ffn1_pre_g"], w1_in, w1_out4, "ffn1",
        dwout_comm=([_pack([small[n] for n in SMALL])], [False]))
    late = gather_two_level(dg_f1pre, "exchange_small")
    small_slots = lax.dynamic_update_slice(early, late, (0, 0, 0))
    big_slots = {"ffn1_w_in": s_w1in, "ffn1_w_out": s_w1out, "w_mix_in": s_wmi, "w_glu": s_wglu, "w_mix_out": s_wmo,
                 "ffn2_w_in": s_w2in, "ffn2_w_out": s_w2out}
    return sq, dx0, big_slots, small_slots


def kernel(x, ffn1_pre_g, ffn1_w_in, ffn1_w_out, ffn1_post_g, mix_pre_g, w_mix_in, a_re, a_im, log_dt, b_re, b_im, c_re, c_im, d_skip, w_glu, b_glu, w_mix_out, mix_post_g, ffn2_pre_g, ffn2_w_in, ffn2_w_out, ffn2_post_g, loss_target, m_ffn1_pre_g, m_ffn1_w_in, m_ffn1_w_out, m_ffn1_post_g, m_mix_pre_g, m_w_mix_in, m_a_re, m_a_im, m_log_dt, m_b_re, m_b_im, m_c_re, m_c_im, m_d_skip, m_w_glu, m_b_glu, m_w_mix_out, m_mix_post_g, m_ffn2_pre_g, m_ffn2_w_in, m_ffn2_w_out, m_ffn2_post_g, v_ffn1_pre_g, v_ffn1_w_in, v_ffn1_w_out, v_ffn1_post_g, v_mix_pre_g, v_w_mix_in, v_a_re, v_a_im, v_log_dt, v_b_re, v_b_im, v_c_re, v_c_im, v_d_skip, v_w_glu, v_b_glu, v_w_mix_out, v_mix_post_g, v_ffn2_pre_g, v_ffn2_w_in, v_ffn2_w_out, v_ffn2_post_g):
    args = dict(locals())
    w = {n: args[n][0] for n in WEIGHTS}
    m = {n: args["m_" + n][0] for n in WEIGHTS}
    v = {n: args["v_" + n][0] for n in WEIGHTS}

    for d in (w, m, v):
        for n in TRANSPOSED:
            d[n] = jnp.swapaxes(d[n], 0, 1)
    wb = {n: w[n].astype(BF16) for n in BIG}
    sp = {n: w[n] for n in SMALL}
    for n in ("ffn1_pre_g", "ffn1_post_g", "mix_pre_g", "mix_post_g", "ffn2_pre_g", "ffn2_post_g", "b_glu", "d_skip"):
        sp[n] = w[n].reshape(1, -1)

    sq, grad_x, big_slots, small_slots = local_step(x[0], loss_target[0], sp, wb)
    loss = lax.psum(0.5 / D_MODEL * jnp.sum(sq), ("x", "y", "c"))

    outs = {}
    for n in BIG:
        shp = w[n].shape
        r2 = lambda t: t.reshape(-1, shp[-1])
        res = adamw(r2(w[n]), r2(m[n]), r2(v[n]), big_slots[n].reshape(N_DEV, -1, shp[-1]), f"adamw_{n}")
        outs[n] = [(jnp.swapaxes(t, 0, 1) if n in TRANSPOSED else t.reshape(shp))[None] for t in res]
    res = adamw(_pack([w[n] for n in SMALL]), _pack([m[n] for n in SMALL]), _pack([v[n] for n in SMALL]),
                small_slots, "adamw_small")
    shapes = [(1,) + w[n].shape for n in SMALL]
    unpacked = [_unpack(t, shapes) for t in res]
    for j, n in enumerate(SMALL):
        outs[n] = [unpacked[k][j] for k in range(4)]

    result = [loss, grad_x[None]]
    for k in range(4):
        result += [outs[n][k] for n in WEIGHTS]
    return tuple(result)
```

```python
import functools
import math

import jax
import jax.numpy as jnp
from jax import lax
from jax.experimental import pallas as pl
from jax.experimental.pallas import tpu as pltpu

F32, BF16 = jnp.float32, jnp.bfloat16
SDS = jax.ShapeDtypeStruct

D_MODEL = 1024
N_DEV = 8
HEAD_DIM = 64
PAIR_W = 128
QBLK = 128
DILATIONS = (1, 4, 16)
N_STATE = 64
HALF_STATES = 1024
NORM_EPS = 1e-6
NEG = -1e30
VMEM_LIMIT = 56 * 1024 * 1024
ADAM_LR, ADAM_B1, ADAM_B2, ADAM_EPS, ADAM_WD, ADAM_STEP = 1e-3, 0.9, 0.999, 1e-8, 0.01, 10
SCAN_TM = 512
SCAN_CW = 512


def _params(n_grid):
    return pltpu.CompilerParams(dimension_semantics=("arbitrary",) * n_grid, vmem_limit_bytes=VMEM_LIMIT)


def _dot(a, b):
    return jnp.dot(a, b, preferred_element_type=F32)


def _dot_nt(a, b):
    return lax.dot_general(a, b, (((1,), (1,)), ((), ())), preferred_element_type=F32)


def _dot_tn(a, b):
    return lax.dot_general(a, b, (((0,), (0,)), ((), ())), preferred_element_type=F32)


def _sigmoid(v):
    return 0.5 * jnp.tanh(0.5 * v) + 0.5


def _resident(shape):
    return pl.BlockSpec(shape, lambda i: (0,) * len(shape), pipeline_mode=pl.Buffered(1))


ROW_SPLIT = 2


def _exchange_phase(ins, outs, scatter, sems, start):
    send_sems, recv_sems, loc_sems = sems
    x, y, c = lax.axis_index("x"), lax.axis_index("y"), lax.axis_index("c")
    me = 4 * x + 2 * y + c
    own_copies, sends, arrivals = [], [], []
    for i in range(len(ins)):
        own = ins[i].at[me] if scatter[i] else ins[i]
        own_copies.append(pltpu.make_async_copy(own, outs[i].at[me], loc_sems.at[i]))
        for k in (6, 7, 4, 5, 2, 3, 1):
            px = 1 - x if k & 4 else x
            py = 1 - y if k & 2 else y
            pc = 1 - c if k & 1 else c
            peer = 4 * px + 2 * py + pc
            src = ins[i].at[peer] if scatter[i] else ins[i]
            common = dict(src_ref=src, send_sem=send_sems.at[i, k - 1], recv_sem=recv_sems.at[i, k - 1],
                          device_id=(px, py, pc), device_id_type=pl.DeviceIdType.MESH)
            sends.append(pltpu.make_async_remote_copy(dst_ref=outs[i].at[me], **common))
            if not start:
                arrivals.append(pltpu.make_async_remote_copy(dst_ref=outs[i].at[peer], **common))
    if start:
        for cp in own_copies + sends:
            cp.start()
    else:
        for cp in arrivals:
            cp.wait_recv()
        for cp in sends:
            cp.wait_send()
        for cp in own_copies:
            cp.wait()


def _comm_shapes(arrs, scatter):
    n = len(arrs)
    out_shapes = [SDS(a.shape if scatter[i] else (N_DEV,) + a.shape, a.dtype) for i, a in enumerate(arrs)]
    sems = [pltpu.SemaphoreType.DMA((n, N_DEV - 1)), pltpu.SemaphoreType.DMA((n, N_DEV - 1)),
            pltpu.SemaphoreType.DMA((n,))]
    return out_shapes, sems


def gather_two_level(arr, name):
    def body(x_ref, out_ref, send_sems, recv_sems, local_sem):
        x, y, c = lax.axis_index("x"), lax.axis_index("y"), lax.axis_index("c")
        sibling = (x, y, 1 - c)
        chips = [(1 - x, y), (x, 1 - y), (1 - x, 1 - y)]

        def slot(px, py, pc):
            return out_ref.at[4 * px + 2 * py + pc]

        def copy(k, block, to, src=None):
            return pltpu.make_async_remote_copy(
                src_ref=slot(*block) if src is None else src, dst_ref=slot(*block),
                send_sem=send_sems.at[k], recv_sem=recv_sems.at[k], device_id=to, device_id_type=pl.DeviceIdType.MESH)

        mine = pltpu.make_async_copy(x_ref, slot(x, y, c), local_sem)
        mine.start()
        first = [copy(0, (x, y, c), sibling, src=x_ref)]
        first += [copy(1 + j, (x, y, c), (*chip, c), src=x_ref) for j, chip in enumerate(chips)]
        for cp in first:
            cp.start()
        passed = [copy(4 + j, (*chip, c), sibling) for j, chip in enumerate(chips)]
        for j, chip in enumerate(chips):
            copy(1 + j, (*chip, c), (x, y, c)).wait_recv()
            passed[j].start()
        copy(0, sibling, (x, y, c)).wait_recv()
        for j, chip in enumerate(chips):
            copy(4 + j, (*chip, 1 - c), (x, y, c)).wait_recv()
        for cp in first + passed:
            cp.wait_send()
        mine.wait()

    anyspec = pl.BlockSpec(memory_space=pl.ANY)
    return pl.pallas_call(
        body, in_specs=[anyspec], out_specs=anyspec, out_shape=SDS((N_DEV,) + arr.shape, arr.dtype),
        scratch_shapes=[pltpu.SemaphoreType.DMA((N_DEV - 1,)), pltpu.SemaphoreType.DMA((N_DEV - 1,)),
                        pltpu.SemaphoreType.DMA],
        compiler_params=pltpu.CompilerParams(has_side_effects=True), name=name)(arr)


def _call(body, *, grid, in_specs, out_specs, out_shape, args, name, scratch_shapes=(), comm=None):
    n_grid, scratch_shapes = len(grid), list(scratch_shapes)
    if comm is None:
        outs = pl.pallas_call(body, grid=grid, in_specs=in_specs, out_specs=out_specs, out_shape=out_shape,
                              scratch_shapes=scratch_shapes, compiler_params=_params(n_grid), name=name)(*args)
        return outs, []
    arrs, scatter = comm
    nc, n_in, n_out, n_sc = len(arrs), len(in_specs), len(out_specs), len(scratch_shapes)
    comm_shapes, sems = _comm_shapes(arrs, scatter)

    def wrapped(*refs):
        ins, cins = refs[:n_in], refs[n_in:n_in + nc]
        o0 = n_in + nc
        outs, couts = refs[o0:o0 + n_out], refs[o0 + n_out:o0 + n_out + nc]
        s0 = o0 + n_out + nc
        scratch, sem_refs = refs[s0:s0 + n_sc], refs[s0 + n_sc:]
        first = functools.reduce(jnp.logical_and, [pl.program_id(k) == 0 for k in range(n_grid)])
        last = functools.reduce(jnp.logical_and, [pl.program_id(k) == grid[k] - 1 for k in range(n_grid)])

        @pl.when(first)
        def _():
            _exchange_phase(cins, couts, scatter, sem_refs, True)

        body(*ins, *outs, *scratch)

        @pl.when(last)
        def _():
            _exchange_phase(cins, couts, scatter, sem_refs, False)

    anyspec = pl.BlockSpec(memory_space=pl.ANY)
    res = pl.pallas_call(
        wrapped, grid=grid, in_specs=list(in_specs) + [anyspec] * nc, out_specs=list(out_specs) + [anyspec] * nc,
        out_shape=list(out_shape) + comm_shapes, scratch_shapes=scratch_shapes + sems,
        compiler_params=pltpu.CompilerParams(dimension_semantics=("arbitrary",) * n_grid,
                                             vmem_limit_bytes=VMEM_LIMIT, has_side_effects=True),
        name=name)(*args, *arrs)
    return res[:n_out], res[n_out:]


def _rms(xv, g):
    r = lax.rsqrt(jnp.mean(xv * xv, axis=-1, keepdims=True) + NORM_EPS)
    return (xv * r * g).astype(BF16)


def ffn_in(x, g, w, name, comm=None):
    T, D = x.shape
    F = w.shape[1]
    tm = 512

    def body(x_ref, g_ref, w_ref, h_ref, z_ref, a_ref):
        hv = _rms(x_ref[...], g_ref[...])
        h_ref[...] = hv
        pending = None
        for j in range(5):
            if j < 4:
                zs = (_dot_nt(hv, w_ref[j]), _dot_nt(hv, w_ref[j + 4]))
            if pending is not None:
                zg, zu = pending
                sg = _sigmoid(zg)
                silu = zg * sg
                z_ref[0, j - 1] = (zu * (sg + silu - silu * sg)).astype(BF16)
                z_ref[1, j - 1] = silu.astype(BF16)
                a_ref[j - 1] = (silu * zu).astype(BF16)
            pending = zs

    return _call(
        body, grid=(T // tm,),
        in_specs=[pl.BlockSpec((tm, D), lambda i: (i, 0)), pl.BlockSpec((1, D), lambda i: (0, 0)),
                  _resident((8, F, D))],
        out_specs=[pl.BlockSpec((tm, D), lambda i: (i, 0)), pl.BlockSpec((2, 4, tm, F), lambda i: (0, 0, i, 0)),
                   pl.BlockSpec((4, tm, F), lambda i: (0, i, 0))],
        out_shape=[SDS((T, D), BF16), SDS((2, 4, T, F), BF16), SDS((4, T, F), BF16)],
        args=(x, g, w), name=name, comm=comm)


def norm_proj(x, g, w, name):
    T, K = x.shape
    nb, _, N = w.shape
    tm = 512

    def body(x_ref, g_ref, w_ref, h_ref, o_ref):
        hv = _rms(x_ref[...], g_ref[...])
        h_ref[...] = hv
        for b in range(nb):
            o_ref[b] = _dot(hv, w_ref[b])

    return pl.pallas_call(
        body, grid=(T // tm,),
        in_specs=[pl.BlockSpec((tm, K), lambda i: (i, 0)), pl.BlockSpec((1, K), lambda i: (0, 0)),
                  _resident((nb, K, N))],
        out_specs=[pl.BlockSpec((tm, K), lambda i: (i, 0)), pl.BlockSpec((nb, tm, N), lambda i: (0, i, 0))],
        out_shape=[SDS((T, K), BF16), SDS((nb, T, N), F32)], compiler_params=_params(1), name=name)(x, g, w)


def mm_acc_norm(a, w, xres, g, scale, name, comm=None, tgt=None):
    nb, T, K = a.shape
    D = w.shape[2]
    with_loss = tgt is not None
    tm = 512 if with_loss else 1024
    rc = tm // ROW_SPLIT

    def body(a_ref, w_ref, x_ref, g_ref, *rest):
        if with_loss:
            t_ref, dy_ref, sq_ref, do_ref, dg_ref = rest

            @pl.when(pl.program_id(0) == 0)
            def _():
                sq_ref[...] = jnp.zeros_like(sq_ref)
                dg_ref[...] = jnp.zeros_like(dg_ref)
        else:
            o_ref, y_ref = rest
        accs = []
        for c in range(ROW_SPLIT):
            rows = pl.ds(c * rc, rc)
            o = _dot(a_ref[0, rows, :].astype(BF16), w_ref[0])
            for b in range(1, nb):
                o += _dot(a_ref[b, rows, :].astype(BF16), w_ref[b])
            accs.append(o)
        for c, o in enumerate(accs):
            rows = pl.ds(c * rc, rc)
            r = lax.rsqrt(jnp.mean(o * o, axis=-1, keepdims=True) + NORM_EPS)
            y = x_ref[rows, :] + scale * (o * r * g_ref[...])
            if with_loss:
                e = y - t_ref[rows, :]
                dy = e * (1.0 / D)
                dy_ref[rows, :] = dy
                sq_ref[...] += jnp.sum(e * e, axis=0, keepdims=True)
                do, dg = _post_bwd(dy, o, g_ref[...], scale)
                do_ref[rows, :] = do
                dg_ref[...] += dg
            else:
                o_ref[rows, :] = o
                y_ref[rows, :] = y

    tile = pl.BlockSpec((tm, D), lambda i: (i, 0))
    row = pl.BlockSpec((1, D), lambda i: (0, 0))
    in_specs = [pl.BlockSpec((nb, tm, K), lambda i: (0, i, 0)), _resident((nb, K, D)), tile, row]
    args = (a, w, xres, g)
    if with_loss:
        return _call(body, grid=(T // tm,), in_specs=in_specs + [tile], out_specs=[tile, row, tile, row],
                     out_shape=[SDS((T, D), F32), SDS((1, D), F32), SDS((T, D), BF16), SDS((1, D), F32)],
                     args=args + (tgt,), name=name, comm=comm)
    return _call(body, grid=(T // tm,), in_specs=in_specs, out_specs=[tile, tile],
                 out_shape=[SDS((T, D), F32), SDS((T, D), F32)], args=args, name=name, comm=comm)


def _post_bwd(dy, ov, g, scale):
    r = scale * dy
    rstd = lax.rsqrt(jnp.mean(ov * ov, axis=-1, keepdims=True) + NORM_EPS)
    oh = ov * rstd
    rg = r * g
    do = rstd * (rg - oh * jnp.mean(rg * oh, axis=-1, keepdims=True))
    return do.astype(BF16), jnp.sum(r * oh, axis=0, keepdims=True)


def mm_nt_b(gr, w, name):
    T, N = gr.shape
    nb, K, _ = w.shape
    tm = 512

    def body(g_ref, w_ref, o_ref):
        gv = g_ref[...]
        for b in range(nb):
            o_ref[b] = _dot_nt(gv, w_ref[b])

    return pl.pallas_call(
        body, grid=(T // tm,),
        in_specs=[pl.BlockSpec((tm, N), lambda i: (i, 0)), _resident((nb, K, N))],
        out_specs=pl.BlockSpec((nb, tm, K), lambda i: (0, i, 0)),
        out_shape=SDS((nb, T, K), F32), compiler_params=_params(1), name=name)(gr, w)


def ffn_dact(do, w_out, z, name):
    T, D = do.shape
    nb, F, _ = w_out.shape
    tm = 512

    def body(g_ref, w_ref, z_ref, dz_ref):
        gv = g_ref[...]
        pending = None
        for b in range(nb + 1):
            da = _dot_nt(gv, w_ref[b]) if b < nb else None
            if pending is not None:
                dz_ref[0, b - 1] = (pending * z_ref[0, b - 1].astype(F32)).astype(BF16)
                dz_ref[1, b - 1] = (pending * z_ref[1, b - 1].astype(F32)).astype(BF16)
            pending = da

    blk = pl.BlockSpec((2, nb, tm, F), lambda i: (0, 0, i, 0))
    return pl.pallas_call(
        body, grid=(T // tm,),
        in_specs=[pl.BlockSpec((tm, D), lambda i: (i, 0)), _resident((nb, F, D)), blk],
        out_specs=blk, out_shape=SDS((2, nb, T, F), BF16), compiler_params=_params(1), name=name)(do, w_out, z)


def mm_tn(a, g, a_batched, g_batched, nb, name, comm=None):
    T = a.shape[-2]
    K, N = a.shape[-1], g.shape[-1]
    tk = 4096
    nk = T // tk

    def body(a_ref, g_ref, o_ref, acc):
        k = pl.program_id(1)

        @pl.when(k == 0)
        def _():
            acc[...] = jnp.zeros_like(acc)

        acc[...] += _dot_tn(a_ref[...].astype(BF16), g_ref[...].astype(BF16))

        @pl.when(k == nk - 1)
        def _():
            o_ref[...] = acc[...].astype(BF16)

    a_spec = (pl.BlockSpec((None, tk, K), lambda b, k: (b, k, 0)) if a_batched
              else pl.BlockSpec((tk, K), lambda b, k: (k, 0)))
    g_spec = (pl.BlockSpec((None, tk, N), lambda b, k: (b, k, 0)) if g_batched
              else pl.BlockSpec((tk, N), lambda b, k: (k, 0)))
    (out,), slots = _call(
        body, grid=(nb, nk), in_specs=[a_spec, g_spec],
        out_specs=[pl.BlockSpec((None, K, N), lambda b, k: (b, 0, 0))],
        out_shape=[SDS((nb, K, N), BF16)], scratch_shapes=[pltpu.VMEM((K, N), F32)],
        args=(a, g), name=name, comm=comm)
    return out, slots


def mm_tn_shared_a(a, g, name):
    T, K = a.shape
    nb, _, N = g.shape
    tk = 1024
    nk = T // tk

    def body(a_ref, g_ref, o_ref, acc):
        k = pl.program_id(0)

        @pl.when(k == 0)
        def _():
            acc[...] = jnp.zeros_like(acc)

        av = a_ref[...]
        for b in range(nb):
            acc[b] += _dot_tn(av, g_ref[b].astype(BF16))

        @pl.when(k == nk - 1)
        def _():
            o_ref[...] = acc[...].astype(BF16)

    return pl.pallas_call(
        body, grid=(nk,),
        in_specs=[pl.BlockSpec((tk, K), lambda k: (k, 0)), pl.BlockSpec((nb, tk, N), lambda k: (0, k, 0))],
        out_specs=_resident((nb, K, N)), out_shape=SDS((nb, K, N), BF16),
        scratch_shapes=[pltpu.VMEM((nb, K, N), F32)], compiler_params=_params(1), name=name)(a, g)


def dh_pre_bwd(dz, w, x, g, dyres, name, comm=None, post=None, w_transposed=False):
    nb, T, F = dz.shape
    D = x.shape[1]
    tm = 512
    rc = tm // ROW_SPLIT
    mm = _dot if w_transposed else _dot_nt

    def body(dz_ref, w_ref, x_ref, g_ref, dy_ref, *rest):
        if post is None:
            dx_ref, dg_ref = rest
        else:
            o_ref, gp_ref, dx_ref, dg_ref, do_ref, dgp_ref = rest

        @pl.when(pl.program_id(0) == 0)
        def _():
            dg_ref[...] = jnp.zeros_like(dg_ref)
            if post is not None:
                dgp_ref[...] = jnp.zeros_like(dgp_ref)

        accs = []
        for c in range(ROW_SPLIT):
            rows = pl.ds(c * rc, rc)
            dh = mm(dz_ref[0, rows, :].astype(BF16), w_ref[0])
            for b in range(1, nb):
                dh += mm(dz_ref[b, rows, :].astype(BF16), w_ref[b])
            accs.append(dh)
        for c, dh in enumerate(accs):
            rows = pl.ds(c * rc, rc)
            xv = x_ref[rows, :]
            rstd = lax.rsqrt(jnp.mean(xv * xv, axis=-1, keepdims=True) + NORM_EPS)
            xh = xv * rstd
            dg_ref[...] += jnp.sum(dh * xh, axis=0, keepdims=True)
            dhg = dh * g_ref[...]
            dx = dy_ref[rows, :] + rstd * (dhg - xh * jnp.mean(dhg * xh, axis=-1, keepdims=True))
            dx_ref[rows, :] = dx
            if post is not None:
                do, dgp = _post_bwd(dx, o_ref[rows, :], gp_ref[...], post[2])
                do_ref[rows, :] = do
                dgp_ref[...] += dgp

    tile = pl.BlockSpec((tm, D), lambda i: (i, 0))
    row = pl.BlockSpec((1, D), lambda i: (0, 0))
    in_specs = [pl.BlockSpec((nb, tm, F), lambda i: (0, i, 0)), _resident(w.shape), tile, row, tile]
    out_specs, out_shape, args = [tile, row], [SDS((T, D), F32), SDS((1, D), F32)], (dz, w, x, g, dyres)
    if post is not None:
        in_specs += [tile, row]
        out_specs += [tile, row]
        out_shape += [SDS((T, D), BF16), SDS((1, D), F32)]
        args += (post[0], post[1])
    return _call(body, grid=(T // tm,), in_specs=in_specs, out_specs=out_specs, out_shape=out_shape,
                 args=args, name=name, comm=comm)


ATTN_GROUP = {1: 8, 4: 2, 16: 1}
ATTN_GROUP_BWD = {1: 16, 4: 4, 16: 1}
ATTN_UNROLL = 4


def _attn_masks():
    qi = lax.broadcasted_iota(jnp.int32, (QBLK, QBLK), 0)
    kj = lax.broadcasted_iota(jnp.int32, (QBLK, QBLK), 1)
    cur_ok = kj <= qi
    prev_ok = kj >= qi
    dcur = (qi - kj).astype(F32)
    return cur_ok, prev_ok, dcur, dcur + float(QBLK)


def _head_slopes(p, d):
    out = []
    for hq in range(2):
        v = [float(d) * 2.0 ** -(2 * q + hq + 1) for q in range(4)]
        out.append(jnp.where(p == 0, v[0], jnp.where(p == 1, v[1], jnp.where(p == 2, v[2], v[3]))))
    return out


def _rows(start, d):
    return pl.ds(start, QBLK, stride=d) if d > 1 else pl.ds(start, QBLK)


def _pair_spec(rows, part, blk):
    return pl.BlockSpec((None, rows, PAIR_W), lambda p, n: (2 * part + p // 2, blk(n), p % 2))


def _for_query_blocks(d, groups, several):
    blocks = [(g, r) for r in range(d) for g in range(groups)]
    for s in range(0, len(blocks), ATTN_UNROLL):
        several(blocks[s:s + ATTN_UNROLL])


def attn_fwd(proj, d, name, comm=None):
    T = proj.shape[1]
    sb, groups = QBLK * d, ATTN_GROUP[d]
    rb = sb * groups
    nblk = T // rb

    def body(q_ref, kc_ref, kp_ref, vc_ref, vp_ref, o_ref, l_ref):
        p, n = pl.program_id(0), pl.program_id(1)
        cur_ok, prev_ok, dcur, dprev = _attn_masks()
        first_ok = jnp.logical_and(prev_ok, n > 0)
        lane_head = lax.broadcasted_iota(jnp.int32, (QBLK, PAIR_W), 1) // HEAD_DIM
        slopes = _head_slopes(p, d)

        def several(blocks):
            work = []
            for g, r in blocks:
                rows = _rows(g * sb + r, d)
                q = q_ref[rows, :]
                kc, vc = kc_ref[rows, :].astype(BF16), vc_ref[rows, :].astype(BF16)
                if g == 0:
                    prow, pok = _rows(r, d), first_ok
                    kp, vp = kp_ref[prow, :].astype(BF16), vp_ref[prow, :].astype(BF16)
                else:
                    prow, pok = _rows((g - 1) * sb + r, d), prev_ok
                    kp, vp = kc_ref[prow, :].astype(BF16), vc_ref[prow, :].astype(BF16)
                for hq in range(2):
                    qm = jnp.where(lane_head == hq, q, 0.0).astype(BF16)
                    work.append([rows, hq, pok, vc, vp, _dot_nt(qm, kc), _dot_nt(qm, kp)])
            for w in work:
                _, hq, pok, _, _, sc, sp = w
                sc = jnp.where(cur_ok, sc * 0.125 - slopes[hq] * dcur, NEG)
                sp = jnp.where(pok, sp * 0.125 - slopes[hq] * dprev, NEG)
                m = jnp.maximum(jnp.max(sc, axis=1, keepdims=True), jnp.max(sp, axis=1, keepdims=True))
                pc = jnp.exp(sc - m)
                pp = jnp.exp(sp - m)
                den = jnp.sum(pc, axis=1, keepdims=True) + jnp.sum(pp, axis=1, keepdims=True)
                w[5:] = [pc.astype(BF16), pp.astype(BF16), 1.0 / den, m + jnp.log(den)]
            for i in range(0, len(work), 2):
                o_acc = jnp.zeros((QBLK, PAIR_W), F32)
                l_acc = jnp.zeros((QBLK, PAIR_W), F32)
                for rows, hq, _, vc, vp, pc, pp, inv, lse in work[i:i + 2]:
                    hm = lane_head == hq
                    o_acc = jnp.where(hm, (_dot(pc, vc) + _dot(pp, vp)) * inv, o_acc)
                    l_acc = jnp.where(hm, lse, l_acc)
                o_ref[rows, :] = o_acc
                l_ref[rows, :] = l_acc

        _for_query_blocks(d, groups, several)

    cur = lambda part: _pair_spec(rb, part, lambda n: n)
    prv = lambda part: _pair_spec(sb, part, lambda n: jnp.maximum(n * groups - 1, 0))
    return _call(
        body, grid=(4, nblk), in_specs=[cur(0), cur(1), prv(1), cur(2), prv(2)], out_specs=[cur(0), cur(0)],
        out_shape=[SDS((2, T, 2 * PAIR_W), F32), SDS((2, T, 2 * PAIR_W), F32)],
        args=(proj, proj, proj, proj, proj), name=name, comm=comm)


def mix_out(os_, ls_, y_ssm, w_glu, b_glu, w, xres, g, name):
    _, T, HW = y_ssm.shape
    D = w.shape[2]
    tm = 512

    def body(o1, o2, o3, l1, l2, l3, s_ref, wg_ref, bg_ref, w_ref, x_ref, g_ref, cat_ref, l_ref, lg_ref, m_ref, y_ref):
        y0, _ = _gelu(s_ref[0])
        y1, _ = _gelu(s_ref[1])
        lg = _dot(y0.astype(BF16), wg_ref[0]) + _dot(y1.astype(BF16), wg_ref[1]) + bg_ref[...]
        a, b, c = l1[...], l2[...], l3[...]
        m = jnp.maximum(jnp.maximum(a, b), c)
        ea, eb, ec = jnp.exp(a - m), jnp.exp(b - m), jnp.exp(c - m)
        s = ea + eb + ec
        att = (ea * o1[...] + eb * o2[...] + ec * o3[...]) * (1.0 / s)
        sg = _sigmoid(lg)
        ssm0, ssm1 = y0 * sg[:, :HW], y1 * sg[:, HW:]
        cat_ref[pl.ds(0, 2)] = att
        cat_ref[2] = ssm0
        cat_ref[3] = ssm1
        l_ref[...] = m + jnp.log(s)
        lg_ref[0] = lg[:, :HW]
        lg_ref[1] = lg[:, HW:]
        o = _dot(att[0].astype(BF16), w_ref[0]) + _dot(att[1].astype(BF16), w_ref[1])
        o += _dot(ssm0.astype(BF16), w_ref[2]) + _dot(ssm1.astype(BF16), w_ref[3])
        r = lax.rsqrt(jnp.mean(o * o, axis=-1, keepdims=True) + NORM_EPS)
        m_ref[...] = o
        y_ref[...] = x_ref[...] + o * r * g_ref[...]

    spec = pl.BlockSpec((2, tm, HW), lambda i: (0, i, 0))
    tile = pl.BlockSpec((tm, D), lambda i: (i, 0))
    return pl.pallas_call(
        body, grid=(T // tm,),
        in_specs=[spec] * 7 + [_resident(w_glu.shape), _resident(b_glu.shape), _resident(w.shape), tile,
                               pl.BlockSpec((1, D), lambda i: (0, 0))],
        out_specs=[pl.BlockSpec((4, tm, HW), lambda i: (0, i, 0)), spec, spec, tile, tile],
        out_shape=[SDS((4, T, HW), F32), SDS((2, T, HW), F32), SDS((2, T, HW), F32), SDS((T, D), F32),
                   SDS((T, D), F32)],
        compiler_params=_params(1), name=name)(*os_, *ls_, y_ssm, w_glu, b_glu, w, xres, g)


def attn_bwd(proj, dcat, o, lse, acc, d, name, du=None):
    T = proj.shape[1]
    sb, groups = QBLK * d, ATTN_GROUP_BWD[d]
    rb = sb * groups
    nblk = T // rb
    has_acc = acc is not None
    n_parts = 3 if du is None else 4

    def body(*refs):
        (qc_ref, qn_ref, kc_ref, kp_ref, vc_ref, vp_ref, dc_ref, dn_ref, oc_ref, on_ref, lc_ref, ln_ref) = refs[:12]
        acc_ref = refs[12] if has_acc else None
        out_ref = refs[-1]
        if du is not None:
            out_ref[3] = refs[-2][...]
        p, n = pl.program_id(0), pl.program_id(1)
        cur_ok, prev_ok, dcur, dprev = _attn_masks()
        first_ok = jnp.logical_and(prev_ok, n > 0)
        last_ok = jnp.logical_and(prev_ok, n < nblk - 1)
        lane_head = lax.broadcasted_iota(jnp.int32, (QBLK, PAIR_W), 1) // HEAD_DIM
        slopes = _head_slopes(p, d)

        def one(g, r, shared):
            rows = _rows(g * sb + r, d)
            q_c, do_c, o_c, l_c = qc_ref[rows, :], dc_ref[rows, :], oc_ref[rows, :], lc_ref[rows, :]
            k_c, v_c = kc_ref[rows, :].astype(BF16), vc_ref[rows, :].astype(BF16)
            if shared:
                pok_c, k_p, v_p = prev_ok, None, None
            elif g == 0:
                prow, pok_c = _rows(r, d), first_ok
                k_p, v_p = kp_ref[prow, :].astype(BF16), vp_ref[prow, :].astype(BF16)
            else:
                prow, pok_c = _rows((g - 1) * sb + r, d), prev_ok
                k_p, v_p = kc_ref[prow, :].astype(BF16), vc_ref[prow, :].astype(BF16)
            if g == groups - 1:
                nrow, pok_n = _rows(r, d), last_ok
                q_n, do_n, o_n, l_n = qn_ref[nrow, :], dn_ref[nrow, :], on_ref[nrow, :], ln_ref[nrow, :]
            else:
                nrow, pok_n = _rows((g + 1) * sb + r, d), prev_ok
                q_n, do_n, o_n, l_n = qc_ref[nrow, :], dc_ref[nrow, :], oc_ref[nrow, :], lc_ref[nrow, :]
            heads = []
            for hq in range(2):
                hm = lane_head == hq
                qm_c = jnp.where(hm, q_c, 0.0).astype(BF16)
                qm_n = jnp.where(hm, q_n, 0.0).astype(BF16)
                dom_c = jnp.where(hm, do_c, 0.0)
                dom_n = jnp.where(hm, do_n, 0.0)
                dd_c = jnp.sum(dom_c * o_c, axis=1, keepdims=True)
                dd_n = jnp.sum(dom_n * o_n, axis=1, keepdims=True)
                ls_c = jnp.max(jnp.where(hm, l_c, NEG), axis=1, keepdims=True)
                ls_n = jnp.max(jnp.where(hm, l_n, NEG), axis=1, keepdims=True)
                dob_c, dob_n = dom_c.astype(BF16), dom_n.astype(BF16)
                mm = [(_dot_nt(qm_c, k_c), _dot_nt(dob_c, v_c)),
                      None if shared else (_dot_nt(qm_c, k_p), _dot_nt(dob_c, v_p)),
                      (_dot_nt(qm_n, k_c), _dot_nt(dob_n, v_c))]
                heads.append(dict(hq=hq, qm_c=qm_c, qm_n=qm_n, dob_c=dob_c, dob_n=dob_n, mm=mm,
                                  dd=(dd_c, dd_c, dd_n), ls=(ls_c, ls_c, ls_n)))
            return dict(rows=rows, k_c=k_c, k_p=k_p, heads=heads, oks=(cur_ok, pok_c, pok_n), shared=shared)

        def several(blocks):
            work = []
            for i, (g, r) in enumerate(blocks):
                work.append(one(g, r, i > 0 and blocks[i - 1] == (g - 1, r)))
            for i, w in enumerate(work):
                if w["shared"]:
                    w["k_p"] = work[i - 1]["k_c"]
                for hi, h in enumerate(w["heads"]):
                    slope, dist = slopes[h["hq"]], (dcur, dprev, dprev)
                    h["pr"], h["ds"] = [], []
                    for j in range(3):
                        if h["mm"][j] is None:
                            h["pr"].append(work[i - 1]["heads"][hi]["pr"][2])
                            h["ds"].append(work[i - 1]["heads"][hi]["ds"][2])
                            continue
                        s = jnp.where(w["oks"][j], h["mm"][j][0] * 0.125 - slope * dist[j], NEG)
                        pr = jnp.exp(s - h["ls"][j])
                        h["pr"].append(pr.astype(BF16))
                        h["ds"].append((pr * (h["mm"][j][1] - h["dd"][j])).astype(BF16))
            for w in work:
                dq = jnp.zeros((QBLK, PAIR_W), F32)
                dk = jnp.zeros((QBLK, PAIR_W), F32)
                dv = jnp.zeros((QBLK, PAIR_W), F32)
                for h in w["heads"]:
                    ds, pr = h["ds"], h["pr"]
                    dq_h = _dot(ds[0], w["k_c"]) + _dot(ds[1], w["k_p"])
                    dk += (_dot_tn(ds[0], h["qm_c"]) + _dot_tn(ds[2], h["qm_n"])) * 0.125
                    dv += _dot_tn(pr[0], h["dob_c"]) + _dot_tn(pr[2], h["dob_n"])
                    dq = jnp.where(lane_head == h["hq"], dq_h * 0.125, dq)
                for part, val in enumerate((dq, dk, dv)):
                    if has_acc:
                        val = val + acc_ref.at[part][w["rows"], :]
                    out_ref.at[part][w["rows"], :] = val

        _for_query_blocks(d, groups, several)

    cur = lambda part: _pair_spec(rb, part, lambda n: n)
    prv = lambda part: _pair_spec(sb, part, lambda n: jnp.maximum(n * groups - 1, 0))
    nxt = lambda part: _pair_spec(sb, part, lambda n: jnp.minimum((n + 1) * groups, T // sb - 1))
    full = pl.BlockSpec((3, None, rb, PAIR_W), lambda p, n: (0, p // 2, n, p % 2))
    in_specs = [cur(0), nxt(0), cur(1), prv(1), cur(2), prv(2), cur(0), nxt(0), cur(0), nxt(0), cur(0), nxt(0)]
    args = [proj, proj, proj, proj, proj, proj, dcat, dcat, o, o, lse, lse]
    if has_acc:
        in_specs.append(full)
        args.append(acc)
    if du is not None:
        in_specs.append(cur(0))
        args.append(du)
    out_spec = pl.BlockSpec((n_parts, None, rb, PAIR_W), lambda p, n: (0, p // 2, n, p % 2))
    return pl.pallas_call(
        body, grid=(4, nblk), in_specs=in_specs, out_specs=out_spec,
        out_shape=SDS((n_parts, 2, T, 2 * PAIR_W), F32), compiler_params=_params(2), name=name)(*args)


def _scan_rows(buf, tab_ref, reverse, half):
    n_tiles = (buf.shape[0] - 8) // 8
    per_half = HALF_STATES // SCAN_CW
    row = lax.broadcasted_iota(jnp.int32, (8, SCAN_CW), 0)
    sgn = -1.0 if reverse else 1.0

    for j in range(per_half):
        c0 = half * 2 * HALF_STATES + j * SCAN_CW
        cre = pl.ds(c0, SCAN_CW)
        cim = pl.ds(c0 + HALF_STATES, SCAN_CW)
        steps = []
        for s, k in enumerate((1, 2, 4)):
            ok, shift = (row < 8 - k, 8 - k) if reverse else (row >= k, k)
            steps.append((shift, jnp.where(ok, tab_ref[pl.ds(s, 1), cre], 0.0),
                          jnp.where(ok, sgn * tab_ref[pl.ds(s, 1), cim], 0.0)))
        trow = 16 if reverse else 8
        pr, pi = tab_ref[pl.ds(trow, 8), cre], tab_ref[pl.ds(trow, 8), cim]
        for t in range(n_tiles):
            base = 8 * (n_tiles - 1 - t) if reverse else 8 + 8 * t
            rows = pl.ds(base, 8)
            re, im = buf[rows, cre], buf[rows, cim]
            for shift, ar, ai in steps:
                sre, sim = pltpu.roll(re, shift, 0), pltpu.roll(im, shift, 0)
                re, im = re + ar * sre - ai * sim, im + ar * sim + ai * sre
            crow = pl.ds(base + 8 if reverse else base - 1, 1)
            cr, ci = buf[crow, cre], buf[crow, cim]
            buf[rows, cre] = re + pr * cr - pi * ci
            buf[rows, cim] = im + pr * ci + pi * cr


def ssm_fwd(proj, bh, ch, apow, dskip, name, comm=None):
    _, T, C = proj.shape
    tm = SCAN_TM
    SW = 4 * HALF_STATES

    def body(u_ref, bh_ref, ch_ref, tab_ref, dsk_ref, y_ref, s_ref, buf):
        @pl.when(pl.program_id(0) == 0)
        def _():
            buf[pl.ds(0, 8), :] = jnp.zeros((8, SW), F32)

        for h in range(2):
            buf[pl.ds(8, tm), pl.ds(h * 2 * HALF_STATES, 2 * HALF_STATES)] = _dot(u_ref[h].astype(BF16), bh_ref[h])
        for h in range(2):
            cols = pl.ds(h * 2 * HALF_STATES, 2 * HALF_STATES)
            _scan_rows(buf, tab_ref, False, h)
            sv = buf[pl.ds(8, tm), cols]
            s_ref[:, cols] = sv
            y_ref[h] = _dot(sv.astype(BF16), ch_ref[h]) + dsk_ref[h] * u_ref[h]
        buf[pl.ds(0, 8), :] = buf[pl.ds(tm, 8), :]

    return _call(
        body, grid=(T // tm,),
        in_specs=[pl.BlockSpec((2, tm, C), lambda i: (3, i, 0)),
                  _resident((2, C, 2 * HALF_STATES)), _resident((2, 2 * HALF_STATES, C)), _resident((24, SW)),
                  _resident((2, 1, C))],
        out_specs=[pl.BlockSpec((2, tm, C), lambda i: (0, i, 0)), pl.BlockSpec((tm, SW), lambda i: (i, 0))],
        out_shape=[SDS((2, T, C), F32), SDS((T, SW), F32)],
        scratch_shapes=[pltpu.VMEM((tm + 8, SW), F32)],
        args=(proj, bh, ch, apow, dskip), name=name, comm=comm)


def ssm_bwd(dy, proj, st, bh, ch, apow, dskip, name, comm=None):
    _, T, C = proj.shape
    tm = SCAN_TM
    nt = T // tm
    SW = 4 * HALF_STATES
    HS2 = 2 * HALF_STATES

    def body(dy_ref, u_ref, s_ref, sp_ref, bh_ref, ch_ref, tab_ref, dsk_ref,
             du_ref, da_ref, dbh_ref, dch_ref, dd_ref, lam):
        i = pl.program_id(0)

        @pl.when(i == 0)
        def _():
            lam[pl.ds(tm, 8), :] = jnp.zeros((8, SW), F32)
            da_ref[...] = jnp.zeros_like(da_ref)
            dbh_ref[...] = jnp.zeros_like(dbh_ref)
            dch_ref[...] = jnp.zeros_like(dch_ref)
            dd_ref[...] = jnp.zeros_like(dd_ref)

        for h in range(2):
            lam[pl.ds(0, tm), pl.ds(h * HS2, HS2)] = _dot_nt(dy_ref[h].astype(BF16), ch_ref[h])
        for h in range(2):
            dyv, uv = dy_ref[h], u_ref[h]
            dch_ref[h] += _dot_tn(s_ref[:, pl.ds(h * HS2, HS2)].astype(BF16), dyv.astype(BF16))
            dd_ref[h] += jnp.sum(dyv * uv, axis=0, keepdims=True)
        for h in range(2):
            _scan_rows(lam, tab_ref, True, h)
            lb = lam[pl.ds(0, tm), pl.ds(h * HS2, HS2)].astype(BF16)
            du_ref[h] = _dot_nt(lb, bh_ref[h]) + dsk_ref[h] * dy_ref[h]
            dbh_ref[h] += _dot_tn(u_ref[h].astype(BF16), lb)

        first = i == nt - 1
        per_half = HALF_STATES // SCAN_CW

        def chunk(j, _):
            c0 = pl.multiple_of((j // per_half) * HS2 + (j % per_half) * SCAN_CW, 128)
            cre, cim = pl.ds(c0, SCAN_CW), pl.ds(pl.multiple_of(c0 + HALF_STATES, 128), SCAN_CW)
            row0 = lax.broadcasted_iota(jnp.int32, (8, SCAN_CW), 0) == 0
            acc_r = jnp.zeros((8, SCAN_CW), F32)
            acc_i = jnp.zeros((8, SCAN_CW), F32)
            for t in range(tm // 8):
                rows = pl.ds(8 * t, 8)
                if t == 0:
                    pre = jnp.where(first, 0.0, sp_ref[pl.ds(7, 1), cre])
                    pim = jnp.where(first, 0.0, sp_ref[pl.ds(7, 1), cim])
                else:
                    pre, pim = s_ref[pl.ds(8 * t - 1, 1), cre], s_ref[pl.ds(8 * t - 1, 1), cim]
                spr = jnp.where(row0, pre, pltpu.roll(s_ref[rows, cre], 1, 0))
                spi = jnp.where(row0, pim, pltpu.roll(s_ref[rows, cim], 1, 0))
                lr, li = lam[rows, cre], lam[rows, cim]
                acc_r += lr * spr + li * spi
                acc_i += li * spr - lr * spi
            da_ref[:, cre] += jnp.sum(acc_r, axis=0, keepdims=True)
            da_ref[:, cim] += jnp.sum(acc_i, axis=0, keepdims=True)
            return 0

        lax.fori_loop(0, 2 * per_half, chunk, 0)
        lam[pl.ds(tm, 8), :] = lam[pl.ds(0, 8), :]

    rev = lambda i: nt - 1 - i
    return _call(
        body, grid=(nt,),
        in_specs=[pl.BlockSpec((2, tm, C), lambda i: (0, rev(i), 0)),
                  pl.BlockSpec((2, tm, C), lambda i: (3, rev(i), 0)),
                  pl.BlockSpec((tm, SW), lambda i: (rev(i), 0)),
                  pl.BlockSpec((8, SW), lambda i: (jnp.maximum(rev(i) * (tm // 8) - 1, 0), 0)),
                  _resident((2, C, HS2)), _resident((2, HS2, C)), _resident((24, SW)), _resident((2, 1, C))],
        out_specs=[pl.BlockSpec((2, tm, C), lambda i: (0, rev(i), 0)),
                   _resident((1, SW)), _resident((2, C, HS2)), _resident((2, HS2, C)), _resident((2, 1, C))],
        out_shape=[SDS((2, T, C), F32), SDS((1, SW), F32), SDS((2, C, HS2), F32), SDS((2, HS2, C), F32),
                   SDS((2, 1, C), F32)],
        scratch_shapes=[pltpu.VMEM((tm + 8, SW), F32)],
        args=(dy, proj, st, st, bh, ch, apow, dskip), name=name, comm=comm)


_GELU_C = math.sqrt(2.0 / math.pi)


def _gelu(x):
    t = jnp.tanh(_GELU_C * (x + 0.044715 * x * x * x))
    return 0.5 * x * (1.0 + t), t


def glu_bwd(dcat, y, lg, w, name):
    _, T, C = y.shape
    tm = 512

    def body(d_ref, y_ref, lg_ref, w_ref, dy_ref, dw_ref, db_ref):
        @pl.when(pl.program_id(0) == 0)
        def _():
            dw_ref[...] = jnp.zeros_like(dw_ref)
            db_ref[...] = jnp.zeros_like(db_ref)

        y2, th, sg, dlg = [], [], [], []
        for h in range(2):
            yy, tt = _gelu(y_ref[h])
            ss = _sigmoid(lg_ref[h])
            y2.append(yy)
            th.append(tt)
            sg.append(ss)
            dlg.append(d_ref[h] * yy * ss * (1.0 - ss))
        dl = jnp.concatenate(dlg, axis=1)
        dlb = dl.astype(BF16)
        db_ref[...] += jnp.sum(dl, axis=0, keepdims=True)
        for h in range(2):
            dy2 = d_ref[h] * sg[h] + _dot_nt(dlb, w_ref[h])
            yv = y_ref[h]
            dgelu = 0.5 * (1.0 + th[h]) + 0.5 * yv * (1.0 - th[h] * th[h]) * _GELU_C * (1.0 + 3 * 0.044715 * yv * yv)
            dy_ref[h] = dy2 * dgelu
            dw_ref[h] += _dot_tn(y2[h].astype(BF16), dlb)

    return pl.pallas_call(
        body, grid=(T // tm,),
        in_specs=[pl.BlockSpec((2, tm, C), lambda i: (1, i, 0)), pl.BlockSpec((2, tm, C), lambda i: (0, i, 0)),
                  pl.BlockSpec((2, tm, C), lambda i: (0, i, 0)), pl.BlockSpec((2, C, 2 * C), lambda i: (0, 0, 0))],
        out_specs=[pl.BlockSpec((2, tm, C), lambda i: (0, i, 0)), pl.BlockSpec((2, C, 2 * C), lambda i: (0, 0, 0)),
                   pl.BlockSpec((1, 2 * C), lambda i: (0, 0))],
        out_shape=[SDS((2, T, C), F32), SDS((2, C, 2 * C), F32), SDS((1, 2 * C), F32)],
        compiler_params=_params(1), name=name)(dcat, y, lg, w)


def adamw(w, m, v, slots, name):
    R, C = w.shape
    tr = R
    for cand in (512, 256, 128, 64, 32, 16, 8):
        if R % cand == 0 and cand * C * 4 <= 2 * 1024 * 1024:
            tr = cand
            break
    c1 = 1.0 / (1.0 - ADAM_B1 ** ADAM_STEP)
    c2 = 1.0 / (1.0 - ADAM_B2 ** ADAM_STEP)

    def body(w_ref, m_ref, v_ref, s_ref, g_ref, d_ref, nm_ref, nv_ref):
        g = s_ref[0].astype(F32)
        for j in range(1, N_DEV):
            g = g + s_ref[j].astype(F32)
        nm = ADAM_B1 * m_ref[...] + (1.0 - ADAM_B1) * g
        nv = ADAM_B2 * v_ref[...] + (1.0 - ADAM_B2) * (g * g)
        g_ref[...] = g
        nm_ref[...] = nm
        nv_ref[...] = nv
        d_ref[...] = -ADAM_LR * ((nm * c1) / (jnp.sqrt(nv * c2) + ADAM_EPS) + ADAM_WD * w_ref[...])

    spec = pl.BlockSpec((tr, C), lambda i: (i, 0))
    return pl.pallas_call(
        body, grid=(R // tr,),
        in_specs=[spec, spec, spec, pl.BlockSpec((N_DEV, tr, C), lambda i: (0, i, 0))],
        out_specs=[spec] * 4, out_shape=[SDS((R, C), F32)] * 4, compiler_params=_params(1), name=name)(w, m, v, slots)


def _discretise(a_re, a_im, log_dt, b_re, b_im):
    dt = jnp.exp(log_dt)[:, None]
    e = jnp.exp(dt * a_re)
    ar, ai = e * jnp.cos(dt * a_im), e * jnp.sin(dt * a_im)
    den = a_re * a_re + a_im * a_im
    nr, ni = ar - 1.0, ai
    wr = (nr * a_re + ni * a_im) / den
    wi = (ni * a_re - nr * a_im) / den
    bbr = wr[..., None] * b_re - wi[..., None] * b_im
    bbi = wr[..., None] * b_im + wi[..., None] * b_re
    return ar, ai, bbr, bbi


def _block_diag(t):
    eye = jnp.eye(16, dtype=t.dtype).reshape(1, 16, 1, 16, 1)
    r, c = t.shape[1], t.shape[2]
    return (t.reshape(2, 16, r, 1, c) * eye).reshape(2, 16 * r, 16 * c)


def _diag_blocks(m, r, c):
    eye = jnp.eye(16, dtype=m.dtype).reshape(1, 16, 1, 16, 1)
    return jnp.sum(m.reshape(2, 16, r, 16, c) * eye, axis=3).reshape(32, r, c)


def _state_vec(re, im):
    return jnp.stack([re.reshape(2, HALF_STATES), im.reshape(2, HALF_STATES)], axis=1).reshape(-1)


BIG = ("ffn1_w_in", "ffn1_w_out", "w_mix_in", "w_glu", "w_mix_out", "ffn2_w_in", "ffn2_w_out")
WEIGHTS = ("ffn1_pre_g", "ffn1_w_in", "ffn1_w_out", "ffn1_post_g", "mix_pre_g", "w_mix_in", "a_re", "a_im", "log_dt",
           "b_re", "b_im", "c_re", "c_im", "d_skip", "w_glu", "b_glu", "w_mix_out", "mix_post_g", "ffn2_pre_g",
           "ffn2_w_in", "ffn2_w_out", "ffn2_post_g")
SMALL = tuple(n for n in WEIGHTS if n not in BIG)
TRANSPOSED = ("ffn1_w_in", "ffn2_w_in")
PACK_COLS = 1024


def _pack(parts):
    flat = jnp.concatenate([p.reshape(-1) for p in parts])
    rows = -(-flat.shape[0] // (8 * PACK_COLS)) * 8
    return jnp.pad(flat, (0, rows * PACK_COLS - flat.shape[0])).reshape(rows, PACK_COLS)


def _unpack(packed, shapes):
    flat, out, off = packed.reshape(-1), [], 0
    for s in shapes:
        n = math.prod(s)
        out.append(flat[off:off + n].reshape(s))
        off += n
    return out


def _gather(names, wb):
    return [wb[n] for n in names], [False] * len(names)


def _ffn_bwd(dy, do, saved, x, pre_g, w_in, w_out4, tag, post=None, dwout_comm=None, carry_dw_in=True):
    h, z, a = saved
    T = x.shape[0]
    dz = ffn_dact(do, w_out4, z, f"{tag}_dact")
    dz8 = dz.reshape(8, T, dz.shape[-1])
    dw_out, extra = mm_tn(a, do, True, False, 4, f"{tag}_dwout", comm=dwout_comm)
    dw_in, (s_out,) = mm_tn(dz8, h, True, False, 8, f"{tag}_dwin", comm=([dw_out.reshape(8, -1, D_MODEL)], [True]))
    outs, s_in = dh_pre_bwd(dz8, w_in, x, pre_g, dy, f"{tag}_dh", comm=([dw_in], [True]) if carry_dw_in else None,
                            post=post, w_transposed=True)
    return outs, (s_in[0] if carry_dw_in else dw_in, s_out), extra


def local_step(x, tgt, sp, wb):
    T = x.shape[0]
    ar, ai, bbr, bbi = _discretise(sp["a_re"], sp["a_im"], sp["log_dt"], sp["b_re"], sp["b_im"])
    powers = [(ar, ai)]
    for _ in range(7):
        pr, pi = powers[-1]
        powers.append((pr * ar - pi * ai, pr * ai + pi * ar))
    zero = jnp.zeros_like(ar)
    rows = [_state_vec(*powers[k - 1]) for k in (1, 2, 4)] + [_state_vec(zero, zero)] * 5
    rows += [_state_vec(pr, pi) for pr, pi in powers]
    rows += [_state_vec(pr, -pi) for pr, pi in reversed(powers)]
    apow = jnp.stack(rows)
    bh = jnp.concatenate([_block_diag(bbr.transpose(0, 2, 1)), _block_diag(bbi.transpose(0, 2, 1))], axis=2)
    ch = jnp.concatenate([_block_diag(sp["c_re"].transpose(0, 2, 1)), _block_diag(-sp["c_im"].transpose(0, 2, 1))], axis=1)
    bh, ch = bh.astype(BF16), ch.astype(BF16)
    dskip = sp["d_skip"].reshape(2, 1, 256)

    w1_in = gather_two_level(wb["ffn1_w_in"], "gather_w1in")
    (h1, z1, a1), (w1_out, w_mi) = ffn_in(
        x, sp["ffn1_pre_g"], w1_in, "ffn1_in", comm=_gather(["ffn1_w_out", "w_mix_in"], wb))
    w1_out4 = w1_out.reshape(4, -1, D_MODEL)
    (o1, x1), (w_glu, w_mo) = mm_acc_norm(
        a1, w1_out4, x, sp["ffn1_post_g"], 0.5, "ffn1_out", comm=_gather(["w_glu", "w_mix_out"], wb))
    w_glu2, w_mo4 = w_glu.reshape(2, 256, 512), w_mo.reshape(4, 256, D_MODEL)
    h2, proj = norm_proj(x1, sp["mix_pre_g"], w_mi, "mix_proj")
    (y_ssm, states), (w2_in,) = ssm_fwd(proj, bh, ch, apow, dskip, "ssm_fwd", comm=_gather(["ffn2_w_in"], wb))
    os_, ls_ = [], []
    for d in DILATIONS:
        (o_d, l_d), got = attn_fwd(proj, d, f"attn_fwd_d{d}",
                                   comm=_gather(["ffn2_w_out"], wb) if d == DILATIONS[-1] else None)
        os_.append(o_d)
        ls_.append(l_d)
    w2_out4 = got[0].reshape(4, -1, D_MODEL)
    cat, lse, lg, mixed, x2 = mix_out(os_, ls_, y_ssm, w_glu2, sp["b_glu"], w_mo4, x1, sp["mix_post_g"], "mix_out")
    (h3, z3, a3), _ = ffn_in(x2, sp["ffn2_pre_g"], w2_in, "ffn2_in")
    (dy3, sq, do3, dg_f2post), _ = mm_acc_norm(a3, w2_out4, x2, sp["ffn2_post_g"], 0.5, "ffn2_out", tgt=tgt)

    (dx2, dg_f2pre, dmixed, dg_mpost), (dw2_in, s_w2out), _ = _ffn_bwd(
        dy3, do3, (h3, z3, a3), x2, sp["ffn2_pre_g"], w2_in, w2_out4, "ffn2", post=(mixed, sp["mix_post_g"], 1.0),
        carry_dw_in=False)
    dcat = mm_nt_b(dmixed, w_mo4, "mix_dcat")
    dw_mo, _ = mm_tn(cat, dmixed, True, False, 4, "mix_dwout")
    dy_ssm, dw_glu, db_glu = glu_bwd(dcat, y_ssm, lg, w_glu2, "glu_bwd")
    (du, da, dbh, dch, dd), (s_wmo, s_wglu, s_w2in) = ssm_bwd(
        dy_ssm, proj, states, bh, ch, apow, dskip, "ssm_bwd",
        comm=([dw_mo.reshape(8, 128, D_MODEL), dw_glu.astype(BF16).reshape(8, 64, 512), dw2_in], [True] * 3))
    dqkv = None
    for d in DILATIONS:
        dqkv = attn_bwd(proj, dcat, cat, lse, dqkv, d, f"attn_bwd_d{d}", du=du if d == DILATIONS[-1] else None)
    dproj = dqkv.reshape(8, T, 256)
    dw_mi = mm_tn_shared_a(h2, dproj, "mix_dwin")
    (dx1, dg_mpre, do1, dg_f1post), (s_wmi,) = dh_pre_bwd(
        dproj, w_mi, x1, sp["mix_pre_g"], dx2, "mix_dh", comm=([dw_mi], [True]), post=(o1, sp["ffn1_post_g"], 0.5))

    da4 = da.reshape(2, 2, HALF_STATES)
    d_ar, d_ai = da4[:, 0].reshape(32, N_STATE), da4[:, 1].reshape(32, N_STATE)
    d_bbr = _diag_blocks(dbh[:, :, :HALF_STATES], 16, N_STATE).transpose(0, 2, 1)
    d_bbi = _diag_blocks(dbh[:, :, HALF_STATES:], 16, N_STATE).transpose(0, 2, 1)
    _, disc_vjp = jax.vjp(_discretise, sp["a_re"], sp["a_im"], sp["log_dt"], sp["b_re"], sp["b_im"])
    g_are, g_aim, g_ldt, g_bre, g_bim = disc_vjp((d_ar, d_ai, d_bbr, d_bbi))
    g_cre = _diag_blocks(dch[:, :HALF_STATES], N_STATE, 16).transpose(0, 2, 1)
    g_cim = -_diag_blocks(dch[:, HALF_STATES:], N_STATE, 16).transpose(0, 2, 1)
    small = {
        "ffn1_pre_g": jnp.zeros((1, D_MODEL), F32), "ffn1_post_g": dg_f1post, "mix_pre_g": dg_mpre, "a_re": g_are,
        "a_im": g_aim, "log_dt": g_ldt, "b_re": g_bre, "b_im": g_bim, "c_re": g_cre, "c_im": g_cim,
        "d_skip": dd.reshape(1, 512), "b_glu": db_glu, "mix_post_g": dg_mpost, "ffn2_pre_g": dg_f2pre,
        "ffn2_post_g": dg_f2post,
    }
    (dx0, dg_f1pre), (s_w1in, s_w1out), (early,) = _ffn_bwd(
        dx1, do1, (h1, z1, a1), x, sp["ffn1_pre_g"], w1_in, w1_out4, "ffn1",
        dwout_comm=([_pack([small[n] for n in SMALL])], [False]))
    late = gather_two_level(dg_f1pre, "exchange_small")
    small_slots = lax.dynamic_update_slice(early, late, (0, 0, 0))
    big_slots = {"ffn1_w_in": s_w1in, "ffn1_w_out": s_w1out, "w_mix_in": s_wmi, "w_glu": s_wglu, "w_mix_out": s_wmo,
                 "ffn2_w_in": s_w2in, "ffn2_w_out": s_w2out}
    return sq, dx0, big_slots, small_slots


def kernel(x, ffn1_pre_g, ffn1_w_in, ffn1_w_out, ffn1_post_g, mix_pre_g, w_mix_in, a_re, a_im, log_dt, b_re, b_im, c_re, c_im, d_skip, w_glu, b_glu, w_mix_out, mix_post_g, ffn2_pre_g, ffn2_w_in, ffn2_w_out, ffn2_post_g, loss_target, m_ffn1_pre_g, m_ffn1_w_in, m_ffn1_w_out, m_ffn1_post_g, m_mix_pre_g, m_w_mix_in, m_a_re, m_a_im, m_log_dt, m_b_re, m_b_im, m_c_re, m_c_im, m_d_skip, m_w_glu, m_b_glu, m_w_mix_out, m_mix_post_g, m_ffn2_pre_g, m_ffn2_w_in, m_ffn2_w_out, m_ffn2_post_g, v_ffn1_pre_g, v_ffn1_w_in, v_ffn1_w_out, v_ffn1_post_g, v_mix_pre_g, v_w_mix_in, v_a_re, v_a_im, v_log_dt, v_b_re, v_b_im, v_c_re, v_c_im, v_d_skip, v_w_glu, v_b_glu, v_w_mix_out, v_mix_post_g, v_ffn2_pre_g, v_ffn2_w_in, v_ffn2_w_out, v_ffn2_post_g):
    args = dict(locals())
    w = {n: args[n][0] for n in WEIGHTS}
    m = {n: args["m_" + n][0] for n in WEIGHTS}
    v = {n: args["v_" + n][0] for n in WEIGHTS}

    for d in (w, m, v):
        for n in TRANSPOSED:
            d[n] = jnp.swapaxes(d[n], 0, 1)
    wb = {n: w[n].astype(BF16) for n in BIG}
    sp = {n: w[n] for n in SMALL}
    for n in ("ffn1_pre_g", "ffn1_post_g", "mix_pre_g", "mix_post_g", "ffn2_pre_g", "ffn2_post_g", "b_glu", "d_skip"):
        sp[n] = w[n].reshape(1, -1)

    sq, grad_x, big_slots, small_slots = local_step(x[0], loss_target[0], sp, wb)
    loss = lax.psum(0.5 / D_MODEL * jnp.sum(sq), ("x", "y", "c"))

    outs = {}
    for n in BIG:
        shp = w[n].shape
        r2 = lambda t: t.reshape(-1, shp[-1])
        res = adamw(r2(w[n]), r2(m[n]), r2(v[n]), big_slots[n].reshape(N_DEV, -1, shp[-1]), f"adamw_{n}")
        outs[n] = [(jnp.swapaxes(t, 0, 1) if n in TRANSPOSED else t.reshape(shp))[None] for t in res]
    res = adamw(_pack([w[n] for n in SMALL]), _pack([m[n] for n in SMALL]), _pack([v[n] for n in SMALL]),
                small_slots, "adamw_small")
    shapes = [(1,) + w[n].shape for n in SMALL]
    unpacked = [_unpack(t, shapes) for t in res]
    for j, n in enumerate(SMALL):
        outs[n] = [unpacked[k][j] for k in range(4)]

    result = [loss, grad_x[None]]
    for k in range(4):
        result += [outs[n][k] for n in WEIGHTS]
    return tuple(result)
```
